```python
import jax, jax.numpy as jnp
from jax import lax
import numpy as np

D_MODEL = 2048
BATCH = 8
SEQ = 2048
DEPTH = 1

N_MEM = 256
EPS = 1e-6

GLA_HEADS = 4
GLA_DV = D_MODEL // 2
GLA_DK = GLA_DV // 2
GLA_HK = GLA_DK // GLA_HEADS
GLA_HV = GLA_DV // GLA_HEADS
GLA_GATE_RANK = 16
GLA_GATE_NORM = 16.0
GLA_CHUNK = 64

POOL_WIDTH = D_MODEL // 2
POOL_WINDOWS = (2, 4, 8, 16)
POOL_GROUPS = len(POOL_WINDOWS)
POOL_GW = POOL_WIDTH // POOL_GROUPS

N_BRANCH = 2

CROSS_HEADS = 4
CROSS_HD = D_MODEL // CROSS_HEADS

D_FF = 256 * ((8 * D_MODEL // 3 + 255) // 256)
CONV_W = 3

OFF_K = GLA_DK
OFF_V = 2 * GLA_DK
OFF_R = OFF_V + GLA_DV
OFF_A = OFF_R + GLA_DV
OFF_P = OFF_A + GLA_GATE_RANK
OFF_G = OFF_P + POOL_WIDTH
D_IN = OFF_G + N_BRANCH * D_MODEL

kernel_name = "gla_pool_gated_hybrid_block"


def rms_norm(x, g):
    xf = x.astype(jnp.float32)
    y = xf * lax.rsqrt(jnp.mean(xf * xf, axis=-1, keepdims=True) + EPS)
    return (y * g.astype(jnp.float32)).astype(x.dtype)


def gla_chunked(q, k, v, log_a):
    B, H, T, dk = q.shape
    dv = v.shape[-1]
    C = GLA_CHUNK
    n = T // C

    def to_chunks(t):
        return jnp.moveaxis(t.reshape(B, H, n, C, t.shape[-1]), 2, 0)

    qc, kc, vc, gc = to_chunks(q), to_chunks(k), to_chunks(v), to_chunks(log_a)
    causal = jnp.tril(jnp.ones((C, C), dtype=bool))[:, :, None]

    def step(S, inp):
        qi, ki, vi, gi = inp
        b = jnp.cumsum(gi, axis=2)
        diff = b[:, :, :, None, :] - b[:, :, None, :, :]
        decay = jnp.exp(jnp.where(causal, diff, -jnp.inf))
        A = jnp.einsum('bhid,bhjd,bhijd->bhij', qi, ki, decay)
        o = (jnp.einsum('bhij,bhjv->bhiv', A, vi)
             + jnp.einsum('bhid,bhdv->bhiv', qi * jnp.exp(b), S))
        b_last = b[:, :, -1:, :]
        S = (jnp.exp(b_last[:, :, 0, :])[..., None] * S
             + jnp.einsum('bhjd,bhjv->bhdv', ki * jnp.exp(b_last - b), vi))
        return S, o

    S0 = jnp.zeros((B, H, dk, dv), jnp.float32)
    _, o = lax.scan(step, S0, (qc, kc, vc, gc))
    return jnp.moveaxis(o, 0, 2).reshape(B, H, T, dv)


def multiscale_pool(p, w_pool, pool_scale):
    B, T, _ = p.shape
    pf = p.astype(jnp.float32)
    cs = jnp.concatenate([jnp.zeros((B, 1, POOL_WIDTH), jnp.float32),
                          jnp.cumsum(pf, axis=1)], axis=1)
    pos = jnp.arange(T)
    outs = []
    for gi, w in enumerate(POOL_WINDOWS):
        sl = slice(gi * POOL_GW, (gi + 1) * POOL_GW)
        start = jnp.maximum(pos + 1 - w, 0)
        cnt = (pos + 1 - start).astype(jnp.float32)
        window_sum = cs[:, 1:, sl] - cs[:, start, sl]
        outs.append(window_sum / cnt[None, :, None] - pf[:, :, sl])
    pooled = jnp.stack(outs, axis=2)
    mixed = jnp.einsum('btgc,gcd->btgd', pooled, w_pool.astype(jnp.float32))
    mixed = mixed.reshape(B, T, POOL_WIDTH) * pool_scale.astype(jnp.float32)
    return mixed.astype(p.dtype)


def hybrid_mixer(h, w_in, w_a2, b_a, g_gla, w_pool, pool_scale, w_branch, w_out):
    B, T, _ = h.shape
    f32 = jnp.float32
    proj = h @ w_in

    def heads(t, d):
        return t.reshape(B, T, GLA_HEADS, d).transpose(0, 2, 1, 3).astype(f32)

    q = heads(proj[..., :OFF_K], GLA_HK) * (GLA_HK ** -0.5)
    k = heads(proj[..., OFF_K:OFF_V], GLA_HK)
    v = heads(proj[..., OFF_V:OFF_R], GLA_HV)
    r = proj[..., OFF_R:OFF_A]
    gate_pre = (proj[..., OFF_A:OFF_P] @ w_a2 + b_a).astype(f32)
    log_a = heads(jax.nn.log_sigmoid(gate_pre) / GLA_GATE_NORM, GLA_HK)
    o = gla_chunked(q, k, v, log_a)
    o = o * lax.rsqrt(jnp.mean(o * o, axis=-1, keepdims=True) + EPS)
    o = o.transpose(0, 2, 1, 3).reshape(B, T, GLA_DV) * g_gla.astype(f32)
    o_gla = o.astype(h.dtype) * jax.nn.silu(r)

    o_pool = multiscale_pool(proj[..., OFF_P:OFF_G], w_pool, pool_scale)

    y_gla = o_gla @ w_branch[:GLA_DV]
    y_pool = o_pool @ w_branch[GLA_DV:]
    gates = jax.nn.sigmoid(proj[..., OFF_G:].astype(f32)).astype(h.dtype)
    merged = gates[..., :D_MODEL] * y_gla + gates[..., D_MODEL:] * y_pool
    return merged @ w_out


def memory_cross_attention(h, mem_n, w_cq, w_ckv, w_co):
    B, T, _ = h.shape
    M = mem_n.shape[1]
    q = (h @ w_cq).reshape(B, T, CROSS_HEADS, CROSS_HD)
    kv = (mem_n @ w_ckv).reshape(B, M, 2, CROSS_HEADS, CROSS_HD)
    k, v = kv[:, :, 0], kv[:, :, 1]
    s = jnp.einsum('bthd,bmhd->bhtm', q, k).astype(jnp.float32) * (CROSS_HD ** -0.5)
    pr = jax.nn.softmax(s, axis=-1).astype(v.dtype)
    o = jnp.einsum('bhtm,bmhd->bthd', pr, v).reshape(B, T, D_MODEL)
    return o @ w_co


def conv_glu_ffn(h, w_up, conv_w, conv_b, w_down):
    u = h @ w_up
    u = lax.conv_general_dilated(
        u, conv_w[:, None, :], window_strides=(1,), padding=[(CONV_W - 1, 0)],
        dimension_numbers=('NWC', 'WIO', 'NWC'), feature_group_count=2 * D_FF) + conv_b
    gate, val = u[..., :D_FF], u[..., D_FF:]
    return (jax.nn.silu(gate) * val) @ w_down


def _fwd_setup_inputs(seed: int = 0) -> dict:
    key = jax.random.key(seed)
    ks = jax.random.split(key, 24)
    L, D = DEPTH, D_MODEL
    nrm = lambda k, shape, fan_in: jax.random.normal(k, shape, jnp.float32) * (fan_in ** -0.5)
    gain = lambda k, shape: 1.0 + 0.02 * jax.random.normal(k, shape, jnp.float32)
    return {
        "x": jax.random.normal(ks[0], (BATCH, SEQ, D), jnp.float32),
        "mem": jax.random.normal(ks[1], (BATCH, N_MEM, D), jnp.float32),
        "g_mix": gain(ks[2], (L, D)),
        "w_in": nrm(ks[3], (L, D, D_IN), D),
        "w_a2": nrm(ks[4], (L, GLA_GATE_RANK, GLA_DK), GLA_GATE_RANK),
        "b_a": 0.1 * jax.random.normal(ks[5], (L, GLA_DK), jnp.float32),
        "g_gla": gain(ks[6], (L, GLA_DV)),
        "w_pool": nrm(ks[7], (L, POOL_GROUPS, POOL_GW, POOL_GW), POOL_GW),
        "pool_scale": gain(ks[8], (L, POOL_WIDTH)),
        "w_branch": nrm(ks[9], (L, GLA_DV + POOL_WIDTH, D), GLA_DV),
        "w_out": nrm(ks[10], (L, D, D), D),
        "g_cross": gain(ks[11], (L, D)),
        "g_mem": gain(ks[12], (L, D)),
        "w_cq": nrm(ks[13], (L, D, D), D),
        "w_ckv": nrm(ks[14], (L, D, 2 * D), D),
        "w_co": nrm(ks[15], (L, D, D), D),
        "g_ffn": gain(ks[16], (L, D)),
        "w_up": nrm(ks[17], (L, D, 2 * D_FF), D),
        "conv_w": nrm(ks[18], (L, CONV_W, 2 * D_FF), CONV_W),
        "conv_b": 0.02 * jax.random.normal(ks[19], (L, 2 * D_FF), jnp.float32),
        "w_down": nrm(ks[20], (L, D_FF, D), D_FF),
        "g_final": gain(ks[21], (D,)),
    }


def _fwd_reference(x, mem, g_mix, w_in, w_a2, b_a, g_gla, w_pool, pool_scale, w_branch, w_out,
              g_cross, g_mem, w_cq, w_ckv, w_co, g_ffn, w_up, conv_w, conv_b, w_down, g_final):
    for l in range(DEPTH):
        x = x + hybrid_mixer(rms_norm(x, g_mix[l]), w_in[l], w_a2[l], b_a[l], g_gla[l],
                             w_pool[l], pool_scale[l], w_branch[l], w_out[l])
        x = x + memory_cross_attention(rms_norm(x, g_cross[l]), rms_norm(mem, g_mem[l]),
                                       w_cq[l], w_ckv[l], w_co[l])
        x = x + conv_glu_ffn(rms_norm(x, g_ffn[l]), w_up[l], conv_w[l], conv_b[l], w_down[l])
    return rms_norm(x, g_final)


import jax as _jax
import jax.numpy as _jnp

TWIN_FORMAT = 'train_step'
FWD_PARAMS = ['x', 'mem', 'g_mix', 'w_in', 'w_a2', 'b_a', 'g_gla', 'w_pool', 'pool_scale', 'w_branch', 'w_out', 'g_cross', 'g_mem', 'w_cq', 'w_ckv', 'w_co', 'g_ffn', 'w_up', 'conv_w', 'conv_b', 'w_down', 'g_final']
TWIN_WEIGHTS = ['g_mix', 'w_in', 'w_a2', 'b_a', 'g_gla', 'w_pool', 'pool_scale', 'w_branch', 'w_out', 'g_cross', 'g_mem', 'w_cq', 'w_ckv', 'w_co', 'g_ffn', 'w_up', 'conv_w', 'conv_b', 'w_down', 'g_final']
TWIN_DIFF_INPUT = 'x'
TWIN_INPUTS = ['x', 'mem', 'g_mix', 'w_in', 'w_a2', 'b_a', 'g_gla', 'w_pool', 'pool_scale', 'w_branch', 'w_out', 'g_cross', 'g_mem', 'w_cq', 'w_ckv', 'w_co', 'g_ffn', 'w_up', 'conv_w', 'conv_b', 'w_down', 'g_final', 'loss_target', 'm_g_mix', 'm_w_in', 'm_w_a2', 'm_b_a', 'm_g_gla', 'm_w_pool', 'm_pool_scale', 'm_w_branch', 'm_w_out', 'm_g_cross', 'm_g_mem', 'm_w_cq', 'm_w_ckv', 'm_w_co', 'm_g_ffn', 'm_w_up', 'm_conv_w', 'm_conv_b', 'm_w_down', 'm_g_final', 'v_g_mix', 'v_w_in', 'v_w_a2', 'v_b_a', 'v_g_gla', 'v_w_pool', 'v_pool_scale', 'v_w_branch', 'v_w_out', 'v_g_cross', 'v_g_mem', 'v_w_cq', 'v_w_ckv', 'v_w_co', 'v_g_ffn', 'v_w_up', 'v_conv_w', 'v_conv_b', 'v_w_down', 'v_g_final']
TWIN_OUTPUTS = ['loss', 'grad_x', 'grad_g_mix', 'grad_w_in', 'grad_w_a2', 'grad_b_a', 'grad_g_gla', 'grad_w_pool', 'grad_pool_scale', 'grad_w_branch', 'grad_w_out', 'grad_g_cross', 'grad_g_mem', 'grad_w_cq', 'grad_w_ckv', 'grad_w_co', 'grad_g_ffn', 'grad_w_up', 'grad_conv_w', 'grad_conv_b', 'grad_w_down', 'grad_g_final', 'delta_g_mix', 'delta_w_in', 'delta_w_a2', 'delta_b_a', 'delta_g_gla', 'delta_w_pool', 'delta_pool_scale', 'delta_w_branch', 'delta_w_out', 'delta_g_cross', 'delta_g_mem', 'delta_w_cq', 'delta_w_ckv', 'delta_w_co', 'delta_g_ffn', 'delta_w_up', 'delta_conv_w', 'delta_conv_b', 'delta_w_down', 'delta_g_final', 'new_m_g_mix', 'new_m_w_in', 'new_m_w_a2', 'new_m_b_a', 'new_m_g_gla', 'new_m_w_pool', 'new_m_pool_scale', 'new_m_w_branch', 'new_m_w_out', 'new_m_g_cross', 'new_m_g_mem', 'new_m_w_cq', 'new_m_w_ckv', 'new_m_w_co', 'new_m_g_ffn', 'new_m_w_up', 'new_m_conv_w', 'new_m_conv_b', 'new_m_w_down', 'new_m_g_final', 'new_v_g_mix', 'new_v_w_in', 'new_v_w_a2', 'new_v_b_a', 'new_v_g_gla', 'new_v_w_pool', 'new_v_pool_scale', 'new_v_w_branch', 'new_v_w_out', 'new_v_g_cross', 'new_v_g_mem', 'new_v_w_cq', 'new_v_w_ckv', 'new_v_w_co', 'new_v_g_ffn', 'new_v_w_up', 'new_v_conv_w', 'new_v_conv_b', 'new_v_w_down', 'new_v_g_final']
TWIN_LEAF_KINDS = {'loss': 'loss', 'grad_x': 'grad_x', 'grad_g_mix': 'grad_w', 'grad_w_in': 'grad_w', 'grad_w_a2': 'grad_w', 'grad_b_a': 'grad_w', 'grad_g_gla': 'grad_w', 'grad_w_pool': 'grad_w', 'grad_pool_scale': 'grad_w', 'grad_w_branch': 'grad_w', 'grad_w_out': 'grad_w', 'grad_g_cross': 'grad_w', 'grad_g_mem': 'grad_w', 'grad_w_cq': 'grad_w', 'grad_w_ckv': 'grad_w', 'grad_w_co': 'grad_w', 'grad_g_ffn': 'grad_w', 'grad_w_up': 'grad_w', 'grad_conv_w': 'grad_w', 'grad_conv_b': 'grad_w', 'grad_w_down': 'grad_w', 'grad_g_final': 'grad_w', 'delta_g_mix': 'delta_w', 'delta_w_in': 'delta_w', 'delta_w_a2': 'delta_w', 'delta_b_a': 'delta_w', 'delta_g_gla': 'delta_w', 'delta_w_pool': 'delta_w', 'delta_pool_scale': 'delta_w', 'delta_w_branch': 'delta_w', 'delta_w_out': 'delta_w', 'delta_g_cross': 'delta_w', 'delta_g_mem': 'delta_w', 'delta_w_cq': 'delta_w', 'delta_w_ckv': 'delta_w', 'delta_w_co': 'delta_w', 'delta_g_ffn': 'delta_w', 'delta_w_up': 'delta_w', 'delta_conv_w': 'delta_w', 'delta_conv_b': 'delta_w', 'delta_w_down': 'delta_w', 'delta_g_final': 'delta_w', 'new_m_g_mix': 'new_m', 'new_m_w_in': 'new_m', 'new_m_w_a2': 'new_m', 'new_m_b_a': 'new_m', 'new_m_g_gla': 'new_m', 'new_m_w_pool': 'new_m', 'new_m_pool_scale': 'new_m', 'new_m_w_branch': 'new_m', 'new_m_w_out': 'new_m', 'new_m_g_cross': 'new_m', 'new_m_g_mem': 'new_m', 'new_m_w_cq': 'new_m', 'new_m_w_ckv': 'new_m', 'new_m_w_co': 'new_m', 'new_m_g_ffn': 'new_m', 'new_m_w_up': 'new_m', 'new_m_conv_w': 'new_m', 'new_m_conv_b': 'new_m', 'new_m_w_down': 'new_m', 'new_m_g_final': 'new_m', 'new_v_g_mix': 'new_v', 'new_v_w_in': 'new_v', 'new_v_w_a2': 'new_v', 'new_v_b_a': 'new_v', 'new_v_g_gla': 'new_v', 'new_v_w_pool': 'new_v', 'new_v_pool_scale': 'new_v', 'new_v_w_branch': 'new_v', 'new_v_w_out': 'new_v', 'new_v_g_cross': 'new_v', 'new_v_g_mem': 'new_v', 'new_v_w_cq': 'new_v', 'new_v_w_ckv': 'new_v', 'new_v_w_co': 'new_v', 'new_v_g_ffn': 'new_v', 'new_v_w_up': 'new_v', 'new_v_conv_w': 'new_v', 'new_v_conv_b': 'new_v', 'new_v_w_down': 'new_v', 'new_v_g_final': 'new_v'}


def _forward(args):
    return _fwd_reference(*[args[k] for k in FWD_PARAMS])


def _output_shape():
    out = _jax.eval_shape(lambda: _forward(_fwd_setup_inputs(0)))
    return out.shape, out.dtype

N_MICROBATCH = 1
ADAM_LR = 0.001
ADAM_B1 = 0.9
ADAM_B2 = 0.999
ADAM_EPS = 1e-08
ADAM_WD = 0.01
ADAM_STEP = 10
PER_EXAMPLE_BATCH_AXIS = {'x': 0, 'mem': 0, 'loss_target': 0}
SHARED_INPUTS = []
_WEIGHT_DTYPES = {'g_mix': _jnp.float32, 'w_in': _jnp.float32, 'w_a2': _jnp.float32, 'b_a': _jnp.float32, 'g_gla': _jnp.float32, 'w_pool': _jnp.float32, 'pool_scale': _jnp.float32, 'w_branch': _jnp.float32, 'w_out': _jnp.float32, 'g_cross': _jnp.float32, 'g_mem': _jnp.float32, 'w_cq': _jnp.float32, 'w_ckv': _jnp.float32, 'w_co': _jnp.float32, 'g_ffn': _jnp.float32, 'w_up': _jnp.float32, 'conv_w': _jnp.float32, 'conv_b': _jnp.float32, 'w_down': _jnp.float32, 'g_final': _jnp.float32}
MOMENT_SCALE = {'g_mix': 5.099891e-02, 'w_in': 2.541546e-02, 'w_a2': 4.564132e-03, 'b_a': 1.906933e-02, 'g_gla': 2.730840e-02, 'w_pool': 4.026811e-02, 'pool_scale': 4.177071e-02, 'w_branch': 2.443206e-02, 'w_out': 3.461793e-02, 'g_cross': 6.288624e-03, 'g_mem': 9.170062e-03, 'w_cq': 6.159597e-03, 'w_ckv': 6.193950e-03, 'w_co': 6.226153e-03, 'g_ffn': 4.109474e-02, 'w_up': 1.765954e-02, 'conv_w': 1.825927e-02, 'conv_b': 1.762610e-02, 'w_down': 2.879850e-02, 'g_final': 8.004191e+00}


def _to_microbatches(a, axis):
    t = _jnp.moveaxis(a, axis, 0)
    t = t.reshape((N_MICROBATCH, t.shape[0] // N_MICROBATCH) + t.shape[1:])
    return _jnp.moveaxis(t, 1, axis + 1)


def setup_inputs(seed: int = 0) -> dict:
    inp = _fwd_setup_inputs(seed)
    key = _jax.random.fold_in(_jax.random.key(seed), 7919)
    shape, _ = _output_shape()
    out = dict(inp)
    out["loss_target"] = _jax.random.normal(_jax.random.fold_in(key, 0), shape, _jnp.float32)
    for i, name in enumerate(TWIN_WEIGHTS):
        w = inp[name].astype(_jnp.float32)
        if MOMENT_SCALE is None:
            s = _jnp.sqrt(_jnp.mean(_jnp.square(w)) + 1e-30)
        else:
            s = MOMENT_SCALE[name]
        km, kv = _jax.random.split(_jax.random.fold_in(key, i + 1))
        out[name] = w
        out["m_" + name] = s * _jax.random.normal(km, w.shape, _jnp.float32)
        out["v_" + name] = (s * s) * _jax.random.uniform(kv, w.shape, _jnp.float32, 0.5, 1.5)
    if N_MICROBATCH > 1:
        for name, axis in PER_EXAMPLE_BATCH_AXIS.items():
            out[name] = _to_microbatches(out[name], axis)
    return {'x': out['x'], 'mem': out['mem'], 'g_mix': out['g_mix'], 'w_in': out['w_in'], 'w_a2': out['w_a2'], 'b_a': out['b_a'], 'g_gla': out['g_gla'], 'w_pool': out['w_pool'], 'pool_scale': out['pool_scale'], 'w_branch': out['w_branch'], 'w_out': out['w_out'], 'g_cross': out['g_cross'], 'g_mem': out['g_mem'], 'w_cq': out['w_cq'], 'w_ckv': out['w_ckv'], 'w_co': out['w_co'], 'g_ffn': out['g_ffn'], 'w_up': out['w_up'], 'conv_w': out['conv_w'], 'conv_b': out['conv_b'], 'w_down': out['w_down'], 'g_final': out['g_final'], 'loss_target': out['loss_target'], 'm_g_mix': out['m_g_mix'], 'm_w_in': out['m_w_in'], 'm_w_a2': out['m_w_a2'], 'm_b_a': out['m_b_a'], 'm_g_gla': out['m_g_gla'], 'm_w_pool': out['m_w_pool'], 'm_pool_scale': out['m_pool_scale'], 'm_w_branch': out['m_w_branch'], 'm_w_out': out['m_w_out'], 'm_g_cross': out['m_g_cross'], 'm_g_mem': out['m_g_mem'], 'm_w_cq': out['m_w_cq'], 'm_w_ckv': out['m_w_ckv'], 'm_w_co': out['m_w_co'], 'm_g_ffn': out['m_g_ffn'], 'm_w_up': out['m_w_up'], 'm_conv_w': out['m_conv_w'], 'm_conv_b': out['m_conv_b'], 'm_w_down': out['m_w_down'], 'm_g_final': out['m_g_final'], 'v_g_mix': out['v_g_mix'], 'v_w_in': out['v_w_in'], 'v_w_a2': out['v_w_a2'], 'v_b_a': out['v_b_a'], 'v_g_gla': out['v_g_gla'], 'v_w_pool': out['v_w_pool'], 'v_pool_scale': out['v_pool_scale'], 'v_w_branch': out['v_w_branch'], 'v_w_out': out['v_w_out'], 'v_g_cross': out['v_g_cross'], 'v_g_mem': out['v_g_mem'], 'v_w_cq': out['v_w_cq'], 'v_w_ckv': out['v_w_ckv'], 'v_w_co': out['v_w_co'], 'v_g_ffn': out['v_g_ffn'], 'v_w_up': out['v_w_up'], 'v_conv_w': out['v_conv_w'], 'v_conv_b': out['v_conv_b'], 'v_w_down': out['v_w_down'], 'v_g_final': out['v_g_final']}


def _loss(weights, diff, rest, loss_target):
    with _jax.named_scope("forward"):
        args = {**rest, TWIN_DIFF_INPUT: diff, **{k: w.astype(_WEIGHT_DTYPES[k]) for k, w in weights.items()}}
        y = _forward(args)
    with _jax.named_scope("loss_head"):
        err = _jnp.square(y.astype(_jnp.float32) - loss_target)
        return 0.5 * _jnp.sum(_jnp.mean(err, axis=-1)) if err.ndim else 0.5 * err


def _adamw(w, g, m, v):
    m = ADAM_B1 * m + (1.0 - ADAM_B1) * g
    v = ADAM_B2 * v + (1.0 - ADAM_B2) * _jnp.square(g)
    m_hat = m / (1.0 - ADAM_B1 ** ADAM_STEP)
    v_hat = v / (1.0 - ADAM_B2 ** ADAM_STEP)
    delta = -ADAM_LR * (m_hat / (_jnp.sqrt(v_hat) + ADAM_EPS) + ADAM_WD * w)
    return delta, m, v


def reference(x, mem, g_mix, w_in, w_a2, b_a, g_gla, w_pool, pool_scale, w_branch, w_out, g_cross, g_mem, w_cq, w_ckv, w_co, g_ffn, w_up, conv_w, conv_b, w_down, g_final, loss_target, m_g_mix, m_w_in, m_w_a2, m_b_a, m_g_gla, m_w_pool, m_pool_scale, m_w_branch, m_w_out, m_g_cross, m_g_mem, m_w_cq, m_w_ckv, m_w_co, m_g_ffn, m_w_up, m_conv_w, m_conv_b, m_w_down, m_g_final, v_g_mix, v_w_in, v_w_a2, v_b_a, v_g_gla, v_w_pool, v_pool_scale, v_w_branch, v_w_out, v_g_cross, v_g_mem, v_w_cq, v_w_ckv, v_w_co, v_g_ffn, v_w_up, v_conv_w, v_conv_b, v_w_down, v_g_final):
    given = dict(x=x, mem=mem, g_mix=g_mix, w_in=w_in, w_a2=w_a2, b_a=b_a, g_gla=g_gla, w_pool=w_pool, pool_scale=pool_scale, w_branch=w_branch, w_out=w_out, g_cross=g_cross, g_mem=g_mem, w_cq=w_cq, w_ckv=w_ckv, w_co=w_co, g_ffn=g_ffn, w_up=w_up, conv_w=conv_w, conv_b=conv_b, w_down=w_down, g_final=g_final, loss_target=loss_target, m_g_mix=m_g_mix, m_w_in=m_w_in, m_w_a2=m_w_a2, m_b_a=m_b_a, m_g_gla=m_g_gla, m_w_pool=m_w_pool, m_pool_scale=m_pool_scale, m_w_branch=m_w_branch, m_w_out=m_w_out, m_g_cross=m_g_cross, m_g_mem=m_g_mem, m_w_cq=m_w_cq, m_w_ckv=m_w_ckv, m_w_co=m_w_co, m_g_ffn=m_g_ffn, m_w_up=m_w_up, m_conv_w=m_conv_w, m_conv_b=m_conv_b, m_w_down=m_w_down, m_g_final=m_g_final, v_g_mix=v_g_mix, v_w_in=v_w_in, v_w_a2=v_w_a2, v_b_a=v_b_a, v_g_gla=v_g_gla, v_w_pool=v_w_pool, v_pool_scale=v_pool_scale, v_w_branch=v_w_branch, v_w_out=v_w_out, v_g_cross=v_g_cross, v_g_mem=v_g_mem, v_w_cq=v_w_cq, v_w_ckv=v_w_ckv, v_w_co=v_w_co, v_g_ffn=v_g_ffn, v_w_up=v_w_up, v_conv_w=v_conv_w, v_conv_b=v_conv_b, v_w_down=v_w_down, v_g_final=v_g_final)
    weights = {n: given[n] for n in TWIN_WEIGHTS}
    shared = {n: given[n] for n in SHARED_INPUTS}
    per_example = {n: given[n] for n in ['x', 'mem']}
    grad_fn = _jax.value_and_grad(_loss, argnums=(0, 1))

    def one_microbatch(ex, loss_target):
        ex = dict(ex)
        diff = ex.pop(TWIN_DIFF_INPUT)
        return grad_fn(weights, diff, {**shared, **ex}, loss_target)

    if N_MICROBATCH == 1:
        loss, (grad_w, grad_x) = one_microbatch(per_example, given["loss_target"])
    else:
        def body(carry, xs):
            loss_sum, grad_sum = carry
            l_k, (gw_k, gx_k) = one_microbatch(xs[0], xs[1])
            with _jax.named_scope("update"):
                return (loss_sum + l_k, _jax.tree.map(_jnp.add, grad_sum, gw_k)), gx_k

        init = (_jnp.zeros((), _jnp.float32), _jax.tree.map(_jnp.zeros_like, weights))
        (loss, grad_w), grad_x = _jax.lax.scan(body, init, (per_example, given["loss_target"]))
    with _jax.named_scope("update"):
        delta_w, new_m, new_v = {}, {}, {}
        for n in TWIN_WEIGHTS:
            delta_w[n], new_m[n], new_v[n] = _adamw(weights[n], grad_w[n], given["m_" + n], given["v_" + n])
    return (loss, grad_x, *[grad_w[n] for n in TWIN_WEIGHTS], *[delta_w[n] for n in TWIN_WEIGHTS],
            *[new_m[n] for n in TWIN_WEIGHTS], *[new_v[n] for n in TWIN_WEIGHTS])
```

```python
import functools

import jax
import jax.numpy as jnp
from jax import lax
from jax.experimental import pallas as pl
from jax.experimental.pallas import tpu as pltpu

F32 = jnp.float32
BF16 = jnp.bfloat16
MESH = pl.DeviceIdType.MESH
HIGHEST = lax.Precision.HIGHEST

EPS = 1e-6
GLA_HEADS = 4
GLA_CHUNK = 64
GLA_GATE_NORM = 16.0
POOL_GROUPS = 4
CROSS_HEADS = 4
CONV_W = 3
N_CHIPS = 4
LANES = 128
SUBLANES = 8
VMEM_LIMIT = 56 << 20

ADAM_LR = 0.001
ADAM_B1 = 0.9
ADAM_B2 = 0.999
ADAM_EPS = 1e-08
ADAM_WD = 0.01
ADAM_STEP = 10

NN = (((1,), (0,)), ((), ()))
NT = (((1,), (1,)), ((), ()))
TN = (((0,), (0,)), ((), ()))


def _dot(a, b, dn=NN, precision=None):
    return lax.dot_general(a, b, dn, precision=precision, preferred_element_type=F32)


def _tile(n, pref, align=LANES):
    t = (min(pref, n) // align) * align
    while t >= align:
        if n % t == 0:
            return t
        t -= align
    return n


def _pcall(body, *, name, out_shape, grid=(), in_specs=None, out_specs=None, scratch_shapes=(),
           semantics=None, prefetch=0):
    params = dict(vmem_limit_bytes=VMEM_LIMIT)
    if semantics is not None:
        params["dimension_semantics"] = semantics
    if prefetch:
        grid_spec = pltpu.PrefetchScalarGridSpec(
            num_scalar_prefetch=prefetch, grid=grid, in_specs=in_specs, out_specs=out_specs,
            scratch_shapes=scratch_shapes)
        return pl.pallas_call(body, name=name, out_shape=out_shape, grid_spec=grid_spec,
                              compiler_params=pltpu.CompilerParams(**params))
    kw = {}
    if in_specs is not None:
        kw["in_specs"] = in_specs
    if out_specs is not None:
        kw["out_specs"] = out_specs
    return pl.pallas_call(body, name=name, out_shape=out_shape, grid=grid,
                          scratch_shapes=scratch_shapes,
                          compiler_params=pltpu.CompilerParams(**params), **kw)


def _sigmoid(x):
    return 1.0 / (1.0 + jnp.exp(-x))


def _log_sigmoid(x):
    return jnp.minimum(x, 0.0) - jnp.log(1.0 + jnp.exp(-jnp.abs(x)))


def _mm(a, b, mode, *, name, out_dtype, M=None, N=None, K=None, a_off=(0, 0), b_off=(0, 0),
        add=None, tm=1024, tn=1024, tk=1024):
    if mode == "nn":
        M = M or a.shape[0]; K = K or a.shape[1]; N = N or b.shape[1]
    elif mode == "nt":
        M = M or a.shape[0]; K = K or a.shape[1]; N = N or b.shape[0]
    else:
        K = K or a.shape[0]; M = M or a.shape[1]; N = N or b.shape[1]
    tm_align = LANES if mode == "tn" else 16
    tm = _tile(M, tm, tm_align)
    tn = _tile(N, tn)
    tk = _tile(K, tk)
    nk = K // tk
    dn = {"nn": NN, "nt": NT, "tn": TN}[mode]

    def off(o, t):
        assert o % t == 0, (name, o, t)
        return o // t

    if mode == "tn":
        ar, ac = off(a_off[0], tk), off(a_off[1], tm)
        a_spec = pl.BlockSpec((tk, tm), lambda i, j, k: (k + ar, i + ac))
    else:
        ar, ac = off(a_off[0], tm), off(a_off[1], tk)
        a_spec = pl.BlockSpec((tm, tk), lambda i, j, k: (i + ar, k + ac))
    if mode == "nt":
        br, bc = off(b_off[0], tn), off(b_off[1], tk)
        b_spec = pl.BlockSpec((tn, tk), lambda i, j, k: (j + br, k + bc))
    else:
        br, bc = off(b_off[0], tk), off(b_off[1], tn)
        b_spec = pl.BlockSpec((tk, tn), lambda i, j, k: (k + br, j + bc))
    o_spec = pl.BlockSpec((tm, tn), lambda i, j, k: (i, j))
    in_specs = [a_spec, b_spec]
    args = [a, b]
    if add is not None:
        in_specs.append(o_spec)
        args.append(add)

    def body(*refs):
        if add is not None:
            a_ref, b_ref, add_ref, o_ref, acc_ref = refs
        else:
            a_ref, b_ref, o_ref, acc_ref = refs
        k = pl.program_id(2)

        @pl.when(k == 0)
        def _():
            acc_ref[...] = jnp.zeros_like(acc_ref)

        acc_ref[...] += _dot(a_ref[...].astype(BF16), b_ref[...].astype(BF16), dn)

        @pl.when(k == nk - 1)
        def _():
            r = acc_ref[...]
            if add is not None:
                r = r + add_ref[...]
            o_ref[...] = r.astype(o_ref.dtype)

    return _pcall(body, name=name, out_shape=jax.ShapeDtypeStruct((M, N), out_dtype),
                  grid=(M // tm, N // tn, nk), in_specs=in_specs, out_specs=o_spec,
                  scratch_shapes=[pltpu.VMEM((tm, tn), F32)],
                  semantics=("parallel", "parallel", "arbitrary"))(*args)


def _rms_fwd(x, g, *, name):
    T, D = x.shape
    tr = _tile(T, 128, 16)

    def body(x_ref, g_ref, h_ref, r_ref):
        xv = x_ref[...]
        r = lax.rsqrt(jnp.mean(xv * xv, axis=-1, keepdims=True) + EPS)
        h_ref[...] = (xv * r * g_ref[...]).astype(h_ref.dtype)
        r_ref[...] = r

    row = pl.BlockSpec((tr, D), lambda i: (i, 0))
    return _pcall(body, name=name,
                  out_shape=(jax.ShapeDtypeStruct((T, D), BF16), jax.ShapeDtypeStruct((T, 1), F32)),
                  grid=(T // tr,),
                  in_specs=[row, pl.BlockSpec((1, D), lambda i: (0, 0))],
                  out_specs=(row, pl.BlockSpec((tr, 1), lambda i: (i, 0))),
                  semantics=("parallel",))(x, g)


def _rms_bwd(dh, x, rstd, g, dres, *, name):
    T, D = x.shape
    tr = _tile(T, 128, 16)
    has_res = dres is not None

    def body(*refs):
        if has_res:
            dh_ref, x_ref, r_ref, g_ref, res_ref, dx_ref, dxb_ref, dg_ref = refs
        else:
            dh_ref, x_ref, r_ref, g_ref, dx_ref, dxb_ref, dg_ref = refs
        r = r_ref[...]
        xh = x_ref[...] * r
        dhv = dh_ref[...].astype(F32)
        dxh = dhv * g_ref[...]
        m = jnp.mean(dxh * xh, axis=-1, keepdims=True)
        dx = r * (dxh - xh * m)
        if has_res:
            dx = dx + res_ref[...]
        dx_ref[...] = dx
        dxb_ref[...] = dx.astype(BF16)

        @pl.when(pl.program_id(0) == 0)
        def _():
            dg_ref[...] = jnp.zeros_like(dg_ref)

        dg_ref[...] += jnp.sum(dhv * xh, axis=0, keepdims=True)

    row = pl.BlockSpec((tr, D), lambda i: (i, 0))
    vec = pl.BlockSpec((1, D), lambda i: (0, 0))
    in_specs = [row, row, pl.BlockSpec((tr, 1), lambda i: (i, 0)), vec]
    args = [dh, x, rstd, g]
    if has_res:
        in_specs.append(row)
        args.append(dres)
    return _pcall(body, name=name,
                  out_shape=(jax.ShapeDtypeStruct((T, D), F32), jax.ShapeDtypeStruct((T, D), BF16),
                             jax.ShapeDtypeStruct((1, D), F32)),
                  grid=(T // tr,), in_specs=in_specs, out_specs=(row, row, vec),
                  semantics=("arbitrary",))(*args)


def _loss_head(x3, g, tgt):
    T, D = x3.shape
    tr = _tile(T, 128, 16)

    def body(x_ref, g_ref, t_ref, loss_ref, dx_ref, dxb_ref, dg_ref):
        xv = x_ref[...]
        gv = g_ref[...]
        r = lax.rsqrt(jnp.mean(xv * xv, axis=-1, keepdims=True) + EPS)
        xh = xv * r
        err = xh * gv - t_ref[...]
        dy = err * (1.0 / D)
        dxh = dy * gv
        m = jnp.mean(dxh * xh, axis=-1, keepdims=True)
        dx = r * (dxh - xh * m)
        dx_ref[...] = dx
        dxb_ref[...] = dx.astype(BF16)

        @pl.when(pl.program_id(0) == 0)
        def _():
            dg_ref[...] = jnp.zeros_like(dg_ref)
            loss_ref[...] = jnp.zeros_like(loss_ref)

        dg_ref[...] += jnp.sum(dy * xh, axis=0, keepdims=True)
        part = 0.5 * jnp.sum(jnp.mean(err * err, axis=-1, keepdims=True), axis=0, keepdims=True)
        loss_ref[...] += jnp.broadcast_to(part, loss_ref.shape)

    row = pl.BlockSpec((tr, D), lambda i: (i, 0))
    vec = pl.BlockSpec((1, D), lambda i: (0, 0))
    return _pcall(body, name="loss_head",
                  out_shape=(jax.ShapeDtypeStruct((1, LANES), F32), jax.ShapeDtypeStruct((T, D), F32),
                             jax.ShapeDtypeStruct((T, D), BF16), jax.ShapeDtypeStruct((1, D), F32)),
                  grid=(T // tr,), in_specs=[row, vec, row],
                  out_specs=(pl.BlockSpec((1, LANES), lambda i: (0, 0)), row, row, vec),
                  semantics=("arbitrary",))(x3, g, tgt)


def _gla_chunk_terms(qk, a_ref, w2_ref, ba_ref, DK):
    C = qk.shape[0]
    gp = _dot(a_ref[...].astype(BF16), w2_ref[...]) + ba_ref[...]
    la = _log_sigmoid(gp) * (1.0 / GLA_GATE_NORM)
    row = lax.broadcasted_iota(jnp.int32, (C, C), 0)
    col = lax.broadcasted_iota(jnp.int32, (C, C), 1)
    causal = row >= col
    b = _dot(causal.astype(F32), la, precision=HIGHEST)
    return gp, b, causal


def _gla_fwd(proj, a_pad, w2, b_a, g_gla, *, T, DK, DV):
    assert 2 * DK == DV
    H = GLA_HEADS
    HK, HV = DK // H, DV // H
    C = GLA_CHUNK
    n = T // C
    RP = a_pad.shape[1]
    scale = HK ** -0.5

    def body(qk_ref, v_ref, r_ref, a_ref, w2_ref, ba_ref, gg_ref, og_ref, oraw_ref, st_ref, s_ref):
        @pl.when(pl.program_id(0) == 0)
        def _():
            s_ref[...] = jnp.zeros_like(s_ref)

        st_ref[...] = s_ref[...]
        qk = qk_ref[...]
        _, b, causal = _gla_chunk_terms(qk, a_ref, w2_ref, ba_ref, DK)
        for h in range(H):
            ks = slice(h * HK, (h + 1) * HK)
            vs = slice(h * HV, (h + 1) * HV)
            bh = b[:, ks]
            b_last = bh[C - 1:C, :]
            qt = qk[:, ks] * scale * jnp.exp(bh)
            kh = qk[:, DK + h * HK:DK + (h + 1) * HK]
            kt = kh * jnp.exp(-bh)
            khat = kh * jnp.exp(b_last - bh)
            a_mat = jnp.where(causal, _dot(qt, kt, NT, HIGHEST), 0.0)
            vh = v_ref[:, vs]
            s_t = s_ref[h]
            o = _dot(a_mat, vh, NN, HIGHEST) + _dot(qt, s_t, NT, HIGHEST)
            s_ref[h] = s_t * jnp.exp(b_last) + _dot(vh, khat, TN, HIGHEST)
            rs = lax.rsqrt(jnp.mean(o * o, axis=-1, keepdims=True) + EPS)
            rr = r_ref[:, vs]
            og = o * rs * gg_ref[:, vs] * (rr * _sigmoid(rr))
            oraw_ref[:, vs] = o
            og_ref[:, vs] = og.astype(BF16)

    blk = lambda j: pl.BlockSpec((C, DV), lambda i: (i, j))
    full = lambda s: pl.BlockSpec(s, lambda i: (0,) * len(s))
    return _pcall(
        body, name="gla_fwd",
        out_shape=(jax.ShapeDtypeStruct((T, DV), BF16), jax.ShapeDtypeStruct((T, DV), F32),
                   jax.ShapeDtypeStruct((n, H, HV, HK), F32)),
        grid=(n,),
        in_specs=[blk(0), blk(1), blk(2), pl.BlockSpec((C, RP), lambda i: (i, 0)),
                  full((RP, DK)), full((1, DK)), full((1, DV))],
        out_specs=(blk(0), blk(0), pl.BlockSpec((None, H, HV, HK), lambda i: (i, 0, 0, 0))),
        scratch_shapes=[pltpu.VMEM((H, HV, HK), F32)],
        semantics=("arbitrary",))(proj, proj, proj, a_pad, w2, b_a, g_gla)


def _gla_bwd(proj, a_pad, w2, b_a, g_gla, o_raw, states, do_gla, *, T, DK, DV):
    H = GLA_HEADS
    HK, HV = DK // H, DV // H
    C = GLA_CHUNK
    n = T // C
    RP = a_pad.shape[1]
    scale = HK ** -0.5

    def body(qk_ref, v_ref, r_ref, a_ref, w2_ref, ba_ref, gg_ref, oraw_ref, st_ref, dog_ref,
             dqkvr_ref, da_ref, dw2_ref, dba_ref, dgg_ref, ds_ref):
        @pl.when(pl.program_id(0) == 0)
        def _():
            ds_ref[...] = jnp.zeros_like(ds_ref)
            dw2_ref[...] = jnp.zeros_like(dw2_ref)
            dba_ref[...] = jnp.zeros_like(dba_ref)
            dgg_ref[...] = jnp.zeros_like(dgg_ref)

        qk = qk_ref[...]
        gp, b, causal = _gla_chunk_terms(qk, a_ref, w2_ref, ba_ref, DK)
        row = lax.broadcasted_iota(jnp.int32, (C, C), 0)
        col = lax.broadcasted_iota(jnp.int32, (C, C), 1)
        upper = (col >= row).astype(F32)
        dla_parts = []
        for h in range(H):
            ks = slice(h * HK, (h + 1) * HK)
            vs = slice(h * HV, (h + 1) * HV)
            bh = b[:, ks]
            b_last = bh[C - 1:C, :]
            eb = jnp.exp(bh)
            emb = jnp.exp(-bh)
            ehat = jnp.exp(b_last - bh)
            e_last = jnp.exp(b_last)
            qt = qk[:, ks] * scale * eb
            kh = qk[:, DK + h * HK:DK + (h + 1) * HK]
            kt = kh * emb
            khat = kh * ehat
            a_mat = jnp.where(causal, _dot(qt, kt, NT, HIGHEST), 0.0)
            vh = v_ref[:, vs]
            o = oraw_ref[:, vs]
            rs = lax.rsqrt(jnp.mean(o * o, axis=-1, keepdims=True) + EPS)
            on = o * rs
            gg = gg_ref[:, vs]
            rr = r_ref[:, vs]
            sg = _sigmoid(rr)
            d_out = dog_ref[:, vs]
            dr = d_out * (on * gg) * (sg * (1.0 + rr * (1.0 - sg)))
            d_og = d_out * (rr * sg)
            dgg_ref[:, vs] += jnp.sum(d_og * on, axis=0, keepdims=True)
            d_on = d_og * gg
            d_o = rs * (d_on - on * jnp.mean(d_on * on, axis=-1, keepdims=True))
            s_t = st_ref[h]
            ds_t = ds_ref[h]
            d_a = jnp.where(causal, _dot(d_o, vh, NT, HIGHEST), 0.0)
            dv = _dot(a_mat, d_o, TN, HIGHEST) + _dot(khat, ds_t, NT, HIGHEST)
            dqt = _dot(d_a, kt, NN, HIGHEST) + _dot(d_o, s_t, NN, HIGHEST)
            dkt = _dot(d_a, qt, TN, HIGHEST)
            dkhat = _dot(vh, ds_t, NN, HIGHEST)
            ds_ref[h] = ds_t * e_last + _dot(d_o, qt, TN, HIGHEST)
            dq = dqt * eb * scale
            dk = dkt * emb + dkhat * ehat
            db = dqt * qt - dkt * kt - dkhat * khat
            d_last = (jnp.sum(dkhat * khat, axis=0, keepdims=True)
                      + e_last * jnp.sum(ds_t * s_t, axis=0, keepdims=True))
            dla_parts.append(_dot(upper, db, NN, HIGHEST) + d_last)
            dqkvr_ref[:, ks] = dq.astype(BF16)
            dqkvr_ref[:, DK + h * HK:DK + (h + 1) * HK] = dk.astype(BF16)
            dqkvr_ref[:, DV + h * HV:DV + (h + 1) * HV] = dv.astype(BF16)
            dqkvr_ref[:, 2 * DV + h * HV:2 * DV + (h + 1) * HV] = dr.astype(BF16)
        dla = jnp.concatenate(dla_parts, axis=1)
        dgp = dla * (1.0 / GLA_GATE_NORM) * _sigmoid(-gp)
        dba_ref[...] += jnp.sum(dgp, axis=0, keepdims=True)
        dgp_b = dgp.astype(BF16)
        dw2_ref[...] += _dot(a_ref[...].astype(BF16), dgp_b, TN)
        da_ref[...] = _dot(dgp_b, w2_ref[...], NT).astype(BF16)

    rev = lambda j: pl.BlockSpec((C, DV), lambda i: (n - 1 - i, j))
    full = lambda s: pl.BlockSpec(s, lambda i: (0,) * len(s))
    return _pcall(
        body, name="gla_bwd",
        out_shape=(jax.ShapeDtypeStruct((T, 3 * DV), BF16), jax.ShapeDtypeStruct((T, RP), BF16),
                   jax.ShapeDtypeStruct((RP, DK), F32), jax.ShapeDtypeStruct((1, DK), F32),
                   jax.ShapeDtypeStruct((1, DV), F32)),
        grid=(n,),
        in_specs=[rev(0), rev(1), rev(2), pl.BlockSpec((C, RP), lambda i: (n - 1 - i, 0)),
                  full((RP, DK)), full((1, DK)), full((1, DV)), rev(0),
                  pl.BlockSpec((None, H, HV, HK), lambda i: (n - 1 - i, 0, 0, 0)), rev(0)],
        out_specs=(pl.BlockSpec((C, 3 * DV), lambda i: (n - 1 - i, 0)),
                   pl.BlockSpec((C, RP), lambda i: (n - 1 - i, 0)),
                   full((RP, DK)), full((1, DK)), full((1, DV))),
        scratch_shapes=[pltpu.VMEM((H, HV, HK), F32)],
        semantics=("arbitrary",))(proj, proj, proj, a_pad, w2, b_a, g_gla, o_raw, states, do_gla)


def _pool_windows(p, g, T):
    t = lax.broadcasted_iota(jnp.int32, (T, 1), 0)
    s = p
    for lvl in range(POOL_GROUPS):
        sh = 1 << lvl
        nxt = s + jnp.where(t >= sh, pltpu.roll(s, sh, 0), 0.0)
        s = jnp.where(lvl <= g, nxt, s)
    win = jnp.left_shift(2, g)
    inv = 1.0 / jnp.minimum(t + 1, win).astype(F32)
    return s * inv - p, inv


def _pool_fwd(proj, w_pool, scale, *, T, PW, col_block):
    GW = PW // POOL_GROUPS
    per = PW // GW

    def body(p_ref, w_ref, s_ref, o_ref):
        g = pl.program_id(0)
        pooled, _ = _pool_windows(p_ref[...], g, T)
        mixed = _dot(pooled.astype(BF16), w_ref[...])
        o_ref[...] = (mixed * s_ref[...]).astype(BF16)

    return _pcall(body, name="pool_fwd", out_shape=jax.ShapeDtypeStruct((T, PW), BF16),
                  grid=(POOL_GROUPS,),
                  in_specs=[pl.BlockSpec((T, GW), lambda g: (0, col_block * per + g)),
                            pl.BlockSpec((None, GW, GW), lambda g: (g, 0, 0)),
                            pl.BlockSpec((1, GW), lambda g: (0, g))],
                  out_specs=pl.BlockSpec((T, GW), lambda g: (0, g)),
                  semantics=("parallel",))(proj, w_pool, scale)


def _pool_bwd(proj, w_pool, scale, do_pool, *, T, PW, col_block):
    GW = PW // POOL_GROUPS
    per = PW // GW

    def body(p_ref, w_ref, s_ref, do_ref, dp_ref, dw_ref, dsc_ref):
        g = pl.program_id(0)
        pooled, inv = _pool_windows(p_ref[...], g, T)
        pooled_b = pooled.astype(BF16)
        w = w_ref[...]
        mixed = _dot(pooled_b, w)
        d_out = do_ref[...]
        dsc_ref[...] = jnp.sum(d_out * mixed, axis=0, keepdims=True)
        dmixed = (d_out * s_ref[...]).astype(BF16)
        dw_ref[...] = _dot(pooled_b, dmixed, TN)
        dpooled = _dot(dmixed, w, NT)
        t = lax.broadcasted_iota(jnp.int32, (T, 1), 0)
        s = dpooled * inv
        for lvl in range(POOL_GROUPS):
            sh = 1 << lvl
            nxt = s + jnp.where(t < T - sh, pltpu.roll(s, T - sh, 0), 0.0)
            s = jnp.where(lvl <= g, nxt, s)
        dp_ref[...] = (s - dpooled).astype(BF16)

    return _pcall(body, name="pool_bwd",
                  out_shape=(jax.ShapeDtypeStruct((T, PW), BF16),
                             jax.ShapeDtypeStruct((POOL_GROUPS, GW, GW), F32),
                             jax.ShapeDtypeStruct((1, PW), F32)),
                  grid=(POOL_GROUPS,),
                  in_specs=[pl.BlockSpec((T, GW), lambda g: (0, col_block * per + g)),
                            pl.BlockSpec((None, GW, GW), lambda g: (g, 0, 0)),
                            pl.BlockSpec((1, GW), lambda g: (0, g)),
                            pl.BlockSpec((T, GW), lambda g: (0, g))],
                  out_specs=(pl.BlockSpec((T, GW), lambda g: (0, g)),
                             pl.BlockSpec((None, GW, GW), lambda g: (g, 0, 0)),
                             pl.BlockSpec((1, GW), lambda g: (0, g))),
                  semantics=("parallel",))(proj, w_pool, scale, do_pool)


def _merge_fwd(y_gla, y_pool, proj, *, T, D, col_block):
    tr = _tile(T, 128, 16)

    def body(yg_ref, yp_ref, g1_ref, g2_ref, o_ref):
        o_ref[...] = (_sigmoid(g1_ref[...]) * yg_ref[...]
                      + _sigmoid(g2_ref[...]) * yp_ref[...]).astype(BF16)

    row = pl.BlockSpec((tr, D), lambda i: (i, 0))
    return _pcall(body, name="merge_fwd", out_shape=jax.ShapeDtypeStruct((T, D), BF16),
                  grid=(T // tr,),
                  in_specs=[row, row, pl.BlockSpec((tr, D), lambda i: (i, col_block)),
                            pl.BlockSpec((tr, D), lambda i: (i, col_block + 1))],
                  out_specs=row, semantics=("parallel",))(y_gla, y_pool, proj, proj)


def _merge_bwd(dmerged, y_gla, y_pool, proj, *, T, D, col_block):
    tr = _tile(T, 128, 16)

    def body(dm_ref, yg_ref, yp_ref, g1_ref, g2_ref, dyg_ref, dyp_ref, dg_ref):
        dm = dm_ref[...]
        s1 = _sigmoid(g1_ref[...])
        s2 = _sigmoid(g2_ref[...])
        dyg_ref[...] = (dm * s1).astype(BF16)
        dyp_ref[...] = (dm * s2).astype(BF16)
        dg_ref[:, :D] = (dm * yg_ref[...] * s1 * (1.0 - s1)).astype(BF16)
        dg_ref[:, D:] = (dm * yp_ref[...] * s2 * (1.0 - s2)).astype(BF16)

    row = pl.BlockSpec((tr, D), lambda i: (i, 0))
    return _pcall(body, name="merge_bwd",
                  out_shape=(jax.ShapeDtypeStruct((T, D), BF16), jax.ShapeDtypeStruct((T, D), BF16),
                             jax.ShapeDtypeStruct((T, 2 * D), BF16)),
                  grid=(T // tr,),
                  in_specs=[row, row, row, pl.BlockSpec((tr, D), lambda i: (i, col_block)),
                            pl.BlockSpec((tr, D), lambda i: (i, col_block + 1))],
                  out_specs=(row, row, pl.BlockSpec((tr, 2 * D), lambda i: (i, 0))),
                  semantics=("parallel",))(dmerged, y_gla, y_pool, proj, proj)


def _attn_fwd(q, kv, *, T, D, M):
    H = CROSS_HEADS
    HD = D // H
    tq = _tile(T, 512, 16)
    scale = HD ** -0.5

    def body(q_ref, kv_ref, o_ref):
        for h in range(H):
            hs = slice(h * HD, (h + 1) * HD)
            s = _dot(q_ref[:, hs], kv_ref[:, hs], NT) * scale
            e = jnp.exp(s - jnp.max(s, axis=-1, keepdims=True))
            p = e / jnp.sum(e, axis=-1, keepdims=True)
            o_ref[:, hs] = _dot(p.astype(BF16), kv_ref[:, D + h * HD:D + (h + 1) * HD]).astype(BF16)

    row = pl.BlockSpec((tq, D), lambda i: (i, 0))
    return _pcall(body, name="attn_fwd", out_shape=jax.ShapeDtypeStruct((T, D), BF16),
                  grid=(T // tq,), in_specs=[row, pl.BlockSpec((M, 2 * D), lambda i: (0, 0))],
                  out_specs=row, semantics=("parallel",))(q, kv)


def _attn_bwd(q, kv, do, *, T, D, M):
    H = CROSS_HEADS
    HD = D // H
    tq = _tile(T, 512, 16)
    scale = HD ** -0.5

    def body(q_ref, kv_ref, do_ref, dq_ref, dkv_ref):
        @pl.when(pl.program_id(0) == 0)
        def _():
            dkv_ref[...] = jnp.zeros_like(dkv_ref)

        for h in range(H):
            hs = slice(h * HD, (h + 1) * HD)
            vs = slice(D + h * HD, D + (h + 1) * HD)
            qh = q_ref[:, hs]
            kh = kv_ref[:, hs]
            s = _dot(qh, kh, NT) * scale
            e = jnp.exp(s - jnp.max(s, axis=-1, keepdims=True))
            p = e / jnp.sum(e, axis=-1, keepdims=True)
            p_b = p.astype(BF16)
            d_o = do_ref[:, hs]
            dkv_ref[:, vs] += _dot(p_b, d_o, TN)
            dp = _dot(d_o, kv_ref[:, vs], NT)
            ds = (p * (dp - jnp.sum(dp * p, axis=-1, keepdims=True)) * scale).astype(BF16)
            dq_ref[:, hs] = _dot(ds, kh).astype(BF16)
            dkv_ref[:, hs] += _dot(ds, qh, TN)

    row = pl.BlockSpec((tq, D), lambda i: (i, 0))
    full = pl.BlockSpec((M, 2 * D), lambda i: (0, 0))
    return _pcall(body, name="attn_bwd",
                  out_shape=(jax.ShapeDtypeStruct((T, D), BF16), jax.ShapeDtypeStruct((M, 2 * D), F32)),
                  grid=(T // tq,), in_specs=[row, full, row], out_specs=(row, full),
                  semantics=("arbitrary",))(q, kv, do)


def _shift_down(x, halo, s, t):
    out = pltpu.roll(x, s, 0)
    for j in range(s):
        out = jnp.where(t == j, halo[SUBLANES - s + j:SUBLANES - s + j + 1, :], out)
    return out


def _shift_up(x, halo, s, t, rows):
    out = pltpu.roll(x, rows - s, 0)
    for j in range(s):
        out = jnp.where(t == rows - s + j, halo[j:j + 1, :], out)
    return out


def _conv_tiles(T):
    tt = _tile(T, 128, SUBLANES)
    return tt, tt // SUBLANES, T // SUBLANES


def _conv_fwd(u0, conv_w, conv_b, *, T, F):
    tt, hb, _ = _conv_tiles(T)
    cw = _tile(F, 512)

    def body(u_ref, prev_ref, w_ref, b_ref, f_ref):
        i = pl.program_id(0)
        t = lax.broadcasted_iota(jnp.int32, (tt, 1), 0)

        def conv(cs):
            x = u_ref[:, cs]
            halo = jnp.where(i > 0, prev_ref[:, cs], 0.0)
            return (w_ref[2:3, cs] * x + w_ref[1:2, cs] * _shift_down(x, halo, 1, t)
                    + w_ref[0:1, cs] * _shift_down(x, halo, 2, t) + b_ref[:, cs])

        for j in range(F // cw):
            gate = conv(slice(j * cw, (j + 1) * cw))
            val = conv(slice(F + j * cw, F + (j + 1) * cw))
            f_ref[:, j * cw:(j + 1) * cw] = (gate * _sigmoid(gate) * val).astype(BF16)

    return _pcall(body, name="conv_fwd", out_shape=jax.ShapeDtypeStruct((T, F), BF16),
                  grid=(T // tt,),
                  in_specs=[pl.BlockSpec((tt, 2 * F), lambda i: (i, 0)),
                            pl.BlockSpec((SUBLANES, 2 * F), lambda i: (jnp.maximum(i * hb - 1, 0), 0)),
                            pl.BlockSpec((CONV_W, 2 * F), lambda i: (0, 0)),
                            pl.BlockSpec((1, 2 * F), lambda i: (0, 0))],
                  out_specs=pl.BlockSpec((tt, F), lambda i: (i, 0)),
                  semantics=("parallel",))(u0, u0, conv_w, conv_b)


def _conv_bwd(u0, conv_w, conv_b, df, *, T, F):
    tt, hb, nb = _conv_tiles(T)
    nt = T // tt
    cw = _tile(F, 512)

    def body(u_ref, prev_ref, next_ref, df_ref, dfn_ref, w_ref, b_ref, du0_ref, dw_ref, db_ref):
        i = pl.program_id(0)
        t = lax.broadcasted_iota(jnp.int32, (tt, 1), 0)
        t8 = lax.broadcasted_iota(jnp.int32, (SUBLANES, 1), 0)

        @pl.when(i == 0)
        def _():
            dw_ref[...] = jnp.zeros_like(dw_ref)
            db_ref[...] = jnp.zeros_like(db_ref)

        def conv(cs):
            x = u_ref[:, cs]
            halo = jnp.where(i > 0, prev_ref[:, cs], 0.0)
            x1 = _shift_down(x, halo, 1, t)
            x2 = _shift_down(x, halo, 2, t)
            u = w_ref[2:3, cs] * x + w_ref[1:2, cs] * x1 + w_ref[0:1, cs] * x2 + b_ref[:, cs]
            xn = next_ref[:, cs]
            tail = x[tt - SUBLANES:, :]
            un = (w_ref[2:3, cs] * xn + w_ref[1:2, cs] * _shift_down(xn, tail, 1, t8)
                  + w_ref[0:1, cs] * _shift_down(xn, tail, 2, t8) + b_ref[:, cs])
            return u, un, (x, x1, x2)

        def glu_grad(gate, val, dff):
            sg = _sigmoid(gate)
            return dff * val * (sg * (1.0 + gate * (1.0 - sg))), dff * (gate * sg)

        def finish(cs, du, dun, xs):
            du0 = (w_ref[2:3, cs] * du + w_ref[1:2, cs] * _shift_up(du, dun, 1, t, tt)
                   + w_ref[0:1, cs] * _shift_up(du, dun, 2, t, tt))
            du0_ref[:, cs] = du0.astype(BF16)
            db_ref[:, cs] += jnp.sum(du, axis=0, keepdims=True)
            dw_ref[2:3, cs] += jnp.sum(du * xs[0], axis=0, keepdims=True)
            dw_ref[1:2, cs] += jnp.sum(du * xs[1], axis=0, keepdims=True)
            dw_ref[0:1, cs] += jnp.sum(du * xs[2], axis=0, keepdims=True)

        for j in range(F // cw):
            fs = slice(j * cw, (j + 1) * cw)
            gs, vs = fs, slice(F + j * cw, F + (j + 1) * cw)
            ug, ung, xg = conv(gs)
            uv, unv, xv = conv(vs)
            dug, duv = glu_grad(ug, uv, df_ref[:, fs].astype(F32))
            dung, dunv = glu_grad(ung, unv, dfn_ref[0:SUBLANES, fs].astype(F32))
            dung = jnp.where(i < nt - 1, dung, 0.0)
            dunv = jnp.where(i < nt - 1, dunv, 0.0)
            finish(gs, dug, dung, xg)
            finish(vs, duv, dunv, xv)

    wide = lambda rows, fn: pl.BlockSpec((rows, 2 * F), fn)
    nxt = lambda i: (jnp.minimum((i + 1) * hb, nb - 1), 0)
    return _pcall(body, name="conv_bwd",
                  out_shape=(jax.ShapeDtypeStruct((T, 2 * F), BF16),
                             jax.ShapeDtypeStruct((CONV_W, 2 * F), F32),
                             jax.ShapeDtypeStruct((1, 2 * F), F32)),
                  grid=(nt,),
                  in_specs=[wide(tt, lambda i: (i, 0)),
                            wide(SUBLANES, lambda i: (jnp.maximum(i * hb - 1, 0), 0)),
                            wide(SUBLANES, nxt),
                            pl.BlockSpec((tt, F), lambda i: (i, 0)),
                            pl.BlockSpec((2 * SUBLANES, F),
                                         lambda i: (jnp.minimum((i + 1) * (hb // 2), nb // 2 - 1), 0)),
                            wide(CONV_W, lambda i: (0, 0)), wide(1, lambda i: (0, 0))],
                  out_specs=(wide(tt, lambda i: (i, 0)), wide(CONV_W, lambda i: (0, 0)),
                             wide(1, lambda i: (0, 0))),
                  semantics=("arbitrary",))(u0, u0, u0, df, df, conv_w, conv_b)


def _adamw(w, g, m, v, *, name):
    R, C = w.shape
    tr = _tile(R, max(SUBLANES, (1 << 19) // max(C, 1) // SUBLANES * SUBLANES), SUBLANES)
    c1 = 1.0 / (1.0 - ADAM_B1 ** ADAM_STEP)
    c2 = 1.0 / (1.0 - ADAM_B2 ** ADAM_STEP)

    def body(w_ref, g_ref, m_ref, v_ref, d_ref, mo_ref, vo_ref):
        gv = g_ref[...]
        mn = ADAM_B1 * m_ref[...] + (1.0 - ADAM_B1) * gv
        vn = ADAM_B2 * v_ref[...] + (1.0 - ADAM_B2) * (gv * gv)
        d_ref[...] = -ADAM_LR * ((mn * c1) / (jnp.sqrt(vn * c2) + ADAM_EPS) + ADAM_WD * w_ref[...])
        mo_ref[...] = mn
        vo_ref[...] = vn

    blk = pl.BlockSpec((tr, C), lambda i: (i, 0))
    shp = jax.ShapeDtypeStruct((R, C), F32)
    return _pcall(body, name=name, out_shape=(shp, shp, shp), grid=(R // tr,),
                  in_specs=[blk] * 4, out_specs=(blk,) * 3, semantics=("parallel",))(w, g, m, v)


def _mesh_pos():
    x, y, c = lax.axis_index("x"), lax.axis_index("y"), lax.axis_index("c")
    others = [(1 - x, y), (x, 1 - y), (1 - x, 1 - y)]
    return x, y, c, others


def _all_gather_weights(shards):
    n = len(shards)
    ANY = pl.BlockSpec(memory_space=pl.ANY)

    def body(*refs):
        ins, outs = refs[:n], refs[n:2 * n]
        send_sems, recv_sems, local_sems = refs[2 * n:]
        x, y, c, others = _mesh_pos()
        me = 2 * x + y
        sibling = (x, y, 1 - c)

        def copy(a, k, src, dst, to):
            return pltpu.make_async_remote_copy(
                src_ref=src, dst_ref=dst, send_sem=send_sems.at[6 * a + k],
                recv_sem=recv_sems.at[6 * a + k], device_id=to, device_id_type=MESH)

        local = [pltpu.make_async_copy(ins[a], outs[a].at[me], local_sems.at[a]) for a in range(n)]
        for cp in local:
            cp.start()
        sent = []
        for a in range(n):
            for j, chip in enumerate(others):
                cp = copy(a, j, ins[a].at[c], outs[a].at[me, c], (*chip, c))
                cp.start()
                sent.append(cp)
        for a in range(n):
            for j, chip in enumerate(others):
                theirs = outs[a].at[2 * chip[0] + chip[1], c]
                copy(a, j, theirs, theirs, (*chip, c)).wait_recv()
                cp = copy(a, 3 + j, theirs, theirs, sibling)
                cp.start()
                sent.append(cp)
        for a in range(n):
            for j, chip in enumerate(others):
                theirs = outs[a].at[2 * chip[0] + chip[1], 1 - c]
                copy(a, 3 + j, theirs, theirs, sibling).wait_recv()
        for cp in sent:
            cp.wait_send()
        for cp in local:
            cp.wait()

    return _pcall(body, name="all_gather_weights",
                  out_shape=[jax.ShapeDtypeStruct((N_CHIPS, *s.shape), s.dtype) for s in shards],
                  in_specs=[ANY] * n, out_specs=[ANY] * n,
                  scratch_shapes=[pltpu.SemaphoreType.DMA((6 * n,)), pltpu.SemaphoreType.DMA((6 * n,)),
                                  pltpu.SemaphoreType.DMA((n,))])(*shards)


def _exchange_halves(grads):
    n = len(grads)
    ANY = pl.BlockSpec(memory_space=pl.ANY)

    def body(*refs):
        ins, outs = refs[:n], refs[n:2 * n]
        send_sems, recv_sems = refs[2 * n:]
        x, y, c, _ = _mesh_pos()
        copies = [pltpu.make_async_remote_copy(
            src_ref=ins[a].at[:, 1 - c], dst_ref=outs[a], send_sem=send_sems.at[a],
            recv_sem=recv_sems.at[a], device_id=(x, y, 1 - c), device_id_type=MESH) for a in range(n)]
        for cp in copies:
            cp.start()
        for cp in copies:
            cp.wait()

    return _pcall(body, name="grad_exchange_halves",
                  out_shape=[jax.ShapeDtypeStruct((g.shape[0], *g.shape[2:]), g.dtype) for g in grads],
                  in_specs=[ANY] * n, out_specs=[ANY] * n,
                  scratch_shapes=[pltpu.SemaphoreType.DMA((n,)), pltpu.SemaphoreType.DMA((n,))])(*grads)


def _add_halves(grad, recv, c_idx, *, name):
    S, _, h, C = grad.shape
    th = _tile(h, max(16, (1 << 19) // C // 16 * 16), 16)

    def body(c_ref, g_ref, r_ref, o_ref):
        o_ref[...] = (g_ref[...].astype(F32) + r_ref[...].astype(F32)).astype(o_ref.dtype)

    return _pcall(body, name=name, out_shape=jax.ShapeDtypeStruct((S, h, C), grad.dtype),
                  grid=(S, h // th), prefetch=1,
                  in_specs=[pl.BlockSpec((None, None, th, C), lambda s, i, c: (s, c[0], i, 0)),
                            pl.BlockSpec((None, th, C), lambda s, i, c: (s, i, 0))],
                  out_specs=pl.BlockSpec((None, th, C), lambda s, i, c: (s, i, 0)),
                  semantics=("parallel", "parallel"))(c_idx, grad, recv)


def _scatter_to_chips(sums):
    n = len(sums)
    ANY = pl.BlockSpec(memory_space=pl.ANY)

    def body(*refs):
        ins, outs = refs[:n], refs[n:2 * n]
        send_sems, recv_sems = refs[2 * n:]
        x, y, c, others = _mesh_pos()
        copies = []
        for a in range(n):
            for j, chip in enumerate(others):
                copies.append(pltpu.make_async_remote_copy(
                    src_ref=ins[a].at[2 * chip[0] + chip[1]], dst_ref=outs[a].at[j],
                    send_sem=send_sems.at[3 * a + j], recv_sem=recv_sems.at[3 * a + j],
                    device_id=(*chip, c), device_id_type=MESH))
        for cp in copies:
            cp.start()
        for cp in copies:
            cp.wait()

    return _pcall(body, name="grad_scatter_to_chips",
                  out_shape=[jax.ShapeDtypeStruct((3, *s.shape[1:]), s.dtype) for s in sums],
                  in_specs=[ANY] * n, out_specs=[ANY] * n,
                  scratch_shapes=[pltpu.SemaphoreType.DMA((3 * n,)), pltpu.SemaphoreType.DMA((3 * n,))])(*sums)


def _add_chips(sums, recv, chip_idx, *, name):
    _, h, C = sums.shape
    th = _tile(h, max(16, (1 << 19) // C // 16 * 16), 16)

    def body(k_ref, s_ref, r_ref, o_ref):
        acc = s_ref[...].astype(F32) + r_ref[0].astype(F32)
        acc = acc + r_ref[1].astype(F32)
        o_ref[...] = acc + r_ref[2].astype(F32)

    return _pcall(body, name=name, out_shape=jax.ShapeDtypeStruct((h, C), F32),
                  grid=(h // th,), prefetch=1,
                  in_specs=[pl.BlockSpec((None, th, C), lambda i, k: (k[0], i, 0)),
                            pl.BlockSpec((3, th, C), lambda i, k: (0, i, 0))],
                  out_specs=pl.BlockSpec((th, C), lambda i, k: (i, 0)),
                  semantics=("parallel",))(chip_idx, sums, recv)


def _share_halves(halves):
    n = len(halves)
    ANY = pl.BlockSpec(memory_space=pl.ANY)

    def body(*refs):
        ins, outs = refs[:n], refs[n:2 * n]
        send_sems, recv_sems, local_sems = refs[2 * n:]
        x, y, c, _ = _mesh_pos()
        local = [pltpu.make_async_copy(ins[a], outs[a].at[c], local_sems.at[a]) for a in range(n)]
        copies = [pltpu.make_async_remote_copy(
            src_ref=ins[a], dst_ref=outs[a].at[c], send_sem=send_sems.at[a], recv_sem=recv_sems.at[a],
            device_id=(x, y, 1 - c), device_id_type=MESH) for a in range(n)]
        for cp in local + copies:
            cp.start()
        for a in range(n):
            pltpu.make_async_remote_copy(
                src_ref=ins[a], dst_ref=outs[a].at[1 - c], send_sem=send_sems.at[a],
                recv_sem=recv_sems.at[a], device_id=(x, y, 1 - c), device_id_type=MESH).wait_recv()
        for cp in copies:
            cp.wait_send()
        for cp in local:
            cp.wait()

    return _pcall(body, name="grad_share_halves",
                  out_shape=[jax.ShapeDtypeStruct((2, *s.shape), s.dtype) for s in halves],
                  in_specs=[ANY] * n, out_specs=[ANY] * n,
                  scratch_shapes=[pltpu.SemaphoreType.DMA((n,)), pltpu.SemaphoreType.DMA((n,)),
                                  pltpu.SemaphoreType.DMA((n,))])(*halves)


def _all_reduce_small(buf):
    R, L = buf.shape
    NDEV = 8

    def body(x_ref, sum_ref, all_ref, send_sems, recv_sems, local_sem):
        x, y, c, others = _mesh_pos()
        me, sibling = (x, y, c), (x, y, 1 - c)

        def slot(px, py, pc):
            return all_ref.at[4 * px + 2 * py + pc]

        def copy(k, block, to, src=None):
            return pltpu.make_async_remote_copy(
                src_ref=slot(*block) if src is None else src, dst_ref=slot(*block),
                send_sem=send_sems.at[k], recv_sem=recv_sems.at[k], device_id=to, device_id_type=MESH)

        mine = pltpu.make_async_copy(x_ref, slot(*me), local_sem)
        mine.start()
        first = [copy(0, me, sibling, src=x_ref)]
        first += [copy(1 + j, me, (*chip, c), src=x_ref) for j, chip in enumerate(others)]
        for cp in first:
            cp.start()
        passed = [copy(4 + j, (*chip, c), sibling) for j, chip in enumerate(others)]
        for j, chip in enumerate(others):
            copy(1 + j, (*chip, c), me).wait_recv()
            passed[j].start()
        copy(0, sibling, me).wait_recv()
        for j, chip in enumerate(others):
            copy(4 + j, (*chip, 1 - c), me).wait_recv()
        for cp in first + passed:
            cp.wait_send()
        mine.wait()
        acc = all_ref[0]
        for d in range(1, NDEV):
            acc = acc + all_ref[d]
        sum_ref[...] = acc

    VM = pl.BlockSpec(memory_space=pltpu.VMEM)
    return _pcall(body, name="all_reduce_small",
                  out_shape=(jax.ShapeDtypeStruct((R, L), F32), jax.ShapeDtypeStruct((NDEV, R, L), F32)),
                  in_specs=[VM], out_specs=(VM, VM),
                  scratch_shapes=[pltpu.SemaphoreType.DMA((7,)), pltpu.SemaphoreType.DMA((7,)),
                                  pltpu.SemaphoreType.DMA])(buf)[0]


def _pack(arrs, rows_multiple=16):
    flat = [a.reshape(-1).astype(F32) for a in arrs]
    sizes = [f.shape[0] for f in flat]
    total = sum(sizes)
    per = LANES * rows_multiple
    padded = -(-total // per) * per
    flat.append(jnp.zeros((padded - total,), F32))
    offs = [0]
    for s in sizes:
        offs.append(offs[-1] + s)
    return jnp.concatenate(flat).reshape(padded // LANES, LANES), offs


def _unpack(buf, offs, shapes):
    flat = buf.reshape(-1)
    return [flat[offs[i]:offs[i + 1]].reshape(s) for i, s in enumerate(shapes)]


def kernel(x, mem, g_mix, w_in, w_a2, b_a, g_gla, w_pool, pool_scale, w_branch, w_out, g_cross, g_mem, w_cq, w_ckv, w_co, g_ffn, w_up, conv_w, conv_b, w_down, g_final, loss_target, m_g_mix, m_w_in, m_w_a2, m_b_a, m_g_gla, m_w_pool, m_pool_scale, m_w_branch, m_w_out, m_g_cross, m_g_mem, m_w_cq, m_w_ckv, m_w_co, m_g_ffn, m_w_up, m_conv_w, m_conv_b, m_w_down, m_g_final, v_g_mix, v_w_in, v_w_a2, v_b_a, v_g_gla, v_w_pool, v_pool_scale, v_w_branch, v_w_out, v_g_cross, v_g_mem, v_w_cq, v_w_ckv, v_w_co, v_g_ffn, v_w_up, v_conv_w, v_conv_b, v_w_down, v_g_final):
    weights = dict(g_mix=g_mix, w_in=w_in, w_a2=w_a2, b_a=b_a, g_gla=g_gla, w_pool=w_pool,
                   pool_scale=pool_scale, w_branch=w_branch, w_out=w_out, g_cross=g_cross, g_mem=g_mem,
                   w_cq=w_cq, w_ckv=w_ckv, w_co=w_co, g_ffn=g_ffn, w_up=w_up, conv_w=conv_w,
                   conv_b=conv_b, w_down=w_down, g_final=g_final)
    mom_m = dict(g_mix=m_g_mix, w_in=m_w_in, w_a2=m_w_a2, b_a=m_b_a, g_gla=m_g_gla, w_pool=m_w_pool,
                 pool_scale=m_pool_scale, w_branch=m_w_branch, w_out=m_w_out, g_cross=m_g_cross,
                 g_mem=m_g_mem, w_cq=m_w_cq, w_ckv=m_w_ckv, w_co=m_w_co, g_ffn=m_g_ffn, w_up=m_w_up,
                 conv_w=m_conv_w, conv_b=m_conv_b, w_down=m_w_down, g_final=m_g_final)
    mom_v = dict(g_mix=v_g_mix, w_in=v_w_in, w_a2=v_w_a2, b_a=v_b_a, g_gla=v_g_gla, w_pool=v_w_pool,
                 pool_scale=v_pool_scale, w_branch=v_w_branch, w_out=v_w_out, g_cross=v_g_cross,
                 g_mem=v_g_mem, w_cq=v_w_cq, w_ckv=v_w_ckv, w_co=v_w_co, g_ffn=v_g_ffn, w_up=v_w_up,
                 conv_w=v_conv_w, conv_b=v_conv_b, w_down=v_w_down, g_final=v_g_final)
    order = list(weights)
    big = ["w_in", "w_branch", "w_out", "w_cq", "w_ckv", "w_co", "w_up", "w_down"]
    small_sharded = ["w_a2", "w_pool", "conv_w"]
    small_repl = ["g_mix", "b_a", "g_gla", "pool_scale", "g_cross", "g_mem", "g_ffn", "conv_b", "g_final"]

    xs, ms, tgt = x[0], mem[0], loss_target[0]
    T, D = xs.shape
    M = ms.shape[0]
    DK, DV, PW = b_a.shape[1], g_gla.shape[1], pool_scale.shape[1]
    RANK = w_a2.shape[1]
    F2 = conv_b.shape[1]
    F = F2 // 2
    DIN = N_CHIPS * w_in.shape[2]
    OFF_A = 2 * DK + 2 * DV
    OFF_P = OFF_A + RANK
    RP = LANES
    GW = PW // POOL_GROUPS
    assert PW == DV and 4 * DV == 2 * D and OFF_P + PW + 2 * D == DIN

    cx, cy, cc = lax.axis_index("x"), lax.axis_index("y"), lax.axis_index("c")
    chip = 2 * cx + cy
    c_idx = jnp.reshape(cc, (1,)).astype(jnp.int32)
    chip_idx = jnp.reshape(chip, (1,)).astype(jnp.int32)

    def halves(a):
        return a.reshape(2, a.shape[0] // 2, a.shape[1])

    small_pack, small_offs = _pack([weights[k][0] for k in small_sharded], rows_multiple=32)
    gathered = _all_gather_weights([halves(weights[k][0].astype(BF16)) for k in big] + [halves(small_pack)])
    gw = {k: g.reshape(N_CHIPS, g.shape[1] * g.shape[2], g.shape[3]) for k, g in zip(big, gathered)}
    small_all = gathered[-1].reshape(N_CHIPS, -1, LANES)

    def cols(g):
        return jnp.transpose(g, (1, 0, 2)).reshape(g.shape[1], -1)

    def rows(g):
        return g.reshape(-1, g.shape[2])

    W_in = cols(gw["w_in"])
    W_main = jnp.concatenate([W_in[:, :OFF_A], W_in[:, OFF_P:]], axis=1)
    W_a = jnp.pad(W_in[:, OFF_A:OFF_P], ((0, 0), (0, RP - RANK)))
    W_branch, W_out, W_cq, W_co, W_down = (rows(gw[k]) for k in ("w_branch", "w_out", "w_cq", "w_co", "w_down"))
    W_ckv, W_up = cols(gw["w_ckv"]), cols(gw["w_up"])
    sm = [_unpack(small_all[j], small_offs, [weights[k].shape[1:] for k in small_sharded]) for j in range(N_CHIPS)]
    W_a2 = jnp.concatenate([sm[j][0] for j in range(N_CHIPS)], axis=1)
    W_a2p = jnp.pad(W_a2, ((0, RP - RANK), (0, 0))).astype(BF16)
    W_pool = jnp.concatenate([sm[j][1] for j in range(N_CHIPS)], axis=1).astype(BF16)
    W_conv = jnp.concatenate([sm[j][2] for j in range(N_CHIPS)], axis=1)

    h1, r1 = _rms_fwd(xs, g_mix, name="norm_mix")
    proj = _mm(h1, W_main, "nn", name="proj_main", out_dtype=F32)
    a_pad = _mm(h1, W_a, "nn", name="proj_gate_rank", out_dtype=F32)
    o_gla, o_raw, states = _gla_fwd(proj, a_pad, W_a2p, b_a, g_gla, T=T, DK=DK, DV=DV)
    o_pool = _pool_fwd(proj, W_pool, pool_scale, T=T, PW=PW, col_block=3)
    y_gla = _mm(o_gla, W_branch, "nn", name="branch_gla", out_dtype=F32, K=DV)
    y_pool = _mm(o_pool, W_branch, "nn", name="branch_pool", out_dtype=F32, K=PW, b_off=(DV, 0))
    merged = _merge_fwd(y_gla, y_pool, proj, T=T, D=D, col_block=2)
    x1 = _mm(merged, W_out, "nn", name="mix_out", out_dtype=F32, add=xs)

    h2, r2 = _rms_fwd(x1, g_cross, name="norm_cross")
    mem_n, rm = _rms_fwd(ms, g_mem, name="norm_mem")
    qc = _mm(h2, W_cq, "nn", name="cross_q", out_dtype=BF16)
    kv = _mm(mem_n, W_ckv, "nn", name="cross_kv", out_dtype=BF16)
    o_att = _attn_fwd(qc, kv, T=T, D=D, M=M)
    x2 = _mm(o_att, W_co, "nn", name="cross_out", out_dtype=F32, add=x1)

    h3, r3 = _rms_fwd(x2, g_ffn, name="norm_ffn")
    u0 = _mm(h3, W_up, "nn", name="ffn_up", out_dtype=F32)
    f_act = _conv_fwd(u0, W_conv, conv_b, T=T, F=F)
    x3 = _mm(f_act, W_down, "nn", name="ffn_down", out_dtype=F32, add=x2)

    loss_part, dx3, dx3_b, dg_final = _loss_head(x3, g_final.reshape(1, D), tgt)

    df = _mm(dx3_b, W_down, "nt", name="d_ffn_act", out_dtype=BF16)
    dW_down = _mm(f_act, dx3_b, "tn", name="dw_down", out_dtype=BF16)
    du0, dconv_w, dconv_b = _conv_bwd(u0, W_conv, conv_b, df, T=T, F=F)
    dh3 = _mm(du0, W_up, "nt", name="d_ffn_in", out_dtype=F32)
    dW_up = _mm(h3, du0, "tn", name="dw_up", out_dtype=BF16)
    dx2, dx2_b, dg_ffn = _rms_bwd(dh3, x2, r3, g_ffn, dx3, name="norm_ffn_bwd")

    do_att = _mm(dx2_b, W_co, "nt", name="d_cross_o", out_dtype=BF16)
    dW_co = _mm(o_att, dx2_b, "tn", name="dw_co", out_dtype=BF16)
    dq, dkv = _attn_bwd(qc, kv, do_att, T=T, D=D, M=M)
    dkv_b = dkv.astype(BF16)
    dW_cq = _mm(h2, dq, "tn", name="dw_cq", out_dtype=BF16)
    dh2 = _mm(dq, W_cq, "nt", name="d_cross_in", out_dtype=F32)
    dW_ckv = _mm(mem_n, dkv_b, "tn", name="dw_ckv", out_dtype=BF16)
    dmem_n = _mm(dkv_b, W_ckv, "nt", name="d_mem", out_dtype=F32)
    _, _, dg_mem = _rms_bwd(dmem_n, ms, rm, g_mem, None, name="norm_mem_bwd")
    dx1, dx1_b, dg_cross = _rms_bwd(dh2, x1, r2, g_cross, dx2, name="norm_cross_bwd")

    dmerged = _mm(dx1_b, W_out, "nt", name="d_merged", out_dtype=F32)
    dW_out = _mm(merged, dx1_b, "tn", name="dw_out", out_dtype=BF16)
    dy_gla, dy_pool, dgates = _merge_bwd(dmerged, y_gla, y_pool, proj, T=T, D=D, col_block=2)
    dW_br_gla = _mm(o_gla, dy_gla, "tn", name="dw_branch_gla", out_dtype=BF16)
    dW_br_pool = _mm(o_pool, dy_pool, "tn", name="dw_branch_pool", out_dtype=BF16)
    do_gla = _mm(dy_gla, W_branch, "nt", name="d_o_gla", out_dtype=F32, N=DV)
    do_pool = _mm(dy_pool, W_branch, "nt", name="d_o_pool", out_dtype=F32, N=PW, b_off=(DV, 0))
    dp, dw_pool, dpool_scale = _pool_bwd(proj, W_pool, pool_scale, do_pool, T=T, PW=PW, col_block=3)
    dqkvr, da_pad, dw2, db_a, dg_gla = _gla_bwd(proj, a_pad, W_a2p, b_a, g_gla, o_raw, states, do_gla,
                                               T=T, DK=DK, DV=DV)
    dproj = jnp.concatenate([dqkvr, dp, dgates], axis=1)
    dh1 = _mm(dproj, W_main, "nt", name="d_mix_in_main", out_dtype=F32)
    dh1 = _mm(da_pad, W_a, "nt", name="d_mix_in_rank", out_dtype=F32, add=dh1)
    dW_main = _mm(h1, dproj, "tn", name="dw_in_main", out_dtype=BF16)
    dW_a = _mm(h1, da_pad, "tn", name="dw_in_rank", out_dtype=BF16)
    dx0, _, dg_mix = _rms_bwd(dh1, xs, r1, g_mix, dx1, name="norm_mix_bwd")

    def col_shards(g):
        K, N = g.shape
        return jnp.transpose(g.reshape(K, N_CHIPS, N // N_CHIPS), (1, 0, 2)).reshape(N_CHIPS, 2, K // 2, N // N_CHIPS)

    def row_shards(g):
        R, N = g.shape
        return g.reshape(N_CHIPS, 2, R // N_CHIPS // 2, N)

    dW_in = jnp.concatenate([dW_main[:, :OFF_A], dW_a[:, :RANK], dW_main[:, OFF_A:]], axis=1)
    partial = [col_shards(dW_in), row_shards(jnp.concatenate([dW_br_gla, dW_br_pool], axis=0)),
               row_shards(dW_out), row_shards(dW_cq), col_shards(dW_ckv), row_shards(dW_co),
               col_shards(dW_up), row_shards(dW_down)]

    from_sibling = _exchange_halves(partial)
    chip_sums = [_add_halves(p, r, c_idx, name=f"grad_add_halves_{k}") for k, p, r in zip(big, partial, from_sibling)]
    from_chips = _scatter_to_chips(chip_sums)
    half_sums = [_add_chips(s, r, chip_idx, name=f"grad_add_chips_{k}") for k, s, r in zip(big, chip_sums, from_chips)]
    shared = _share_halves(half_sums)
    grads = {k: s.reshape(-1, s.shape[2]) for k, s in zip(big, shared)}

    small_grads = [loss_part, dg_mix, db_a, dg_gla, dpool_scale, dg_cross, dg_mem, dg_ffn, dconv_b, dg_final,
                   dw2[:RANK], dw_pool, dconv_w]
    small_buf, offs = _pack(small_grads)
    red = _unpack(_all_reduce_small(small_buf), offs, [g.shape for g in small_grads])
    loss = red[0][0, 0]
    for k, g in zip(small_repl, red[1:10]):
        grads[k] = g.reshape(weights[k].shape)
    nb = DK // N_CHIPS
    grads["w_a2"] = lax.dynamic_slice_in_dim(red[10], chip * nb, nb, axis=1)[None]
    nb = GW // N_CHIPS
    grads["w_pool"] = lax.dynamic_slice_in_dim(red[11], chip * nb, nb, axis=1)[None]
    nb = F2 // N_CHIPS
    grads["conv_w"] = lax.dynamic_slice_in_dim(red[12], chip * nb, nb, axis=1)[None]

    delta, new_m, new_v = {}, {}, {}
    for k in big:
        w2d = weights[k][0]
        d, mn, vn = _adamw(w2d, grads[k], mom_m[k][0], mom_v[k][0], name=f"adamw_{k}")
        grads[k] = grads[k][None]
        delta[k], new_m[k], new_v[k] = d[None], mn[None], vn[None]
    small = small_repl + small_sharded
    packs = [_pack([src[k] for k in small])[0] for src in (weights, grads, mom_m, mom_v)]
    _, offs = _pack([weights[k] for k in small])
    outs = _adamw(*packs, name="adamw_small")
    for res, o in zip((delta, new_m, new_v), outs):
        for k, a in zip(small, _unpack(o, offs, [weights[k].shape for k in small])):
            res[k] = a

    return (loss, dx0[None], *[grads[k] for k in order], *[delta[k] for k in order],
            *[new_m[k] for k in order], *[new_v[k] for k in order])
```

```python
import functools

import jax
import jax.numpy as jnp
from jax import lax
from jax.experimental import pallas as pl
from jax.experimental.pallas import tpu as pltpu

F32 = jnp.float32
BF16 = jnp.bfloat16
MESH = pl.DeviceIdType.MESH
HIGHEST = lax.Precision.HIGHEST

EPS = 1e-6
GLA_HEADS = 4
GLA_CHUNK = 64
GLA_GATE_NORM = 16.0
POOL_GROUPS = 4
CROSS_HEADS = 4
CONV_W = 3
N_CHIPS = 4
LANES = 128
SUBLANES = 8
VMEM_LIMIT = 56 << 20

ADAM_LR = 0.001
ADAM_B1 = 0.9
ADAM_B2 = 0.999
ADAM_EPS = 1e-08
ADAM_WD = 0.01
ADAM_STEP = 10

NN = (((1,), (0,)), ((), ()))
NT = (((1,), (1,)), ((), ()))
TN = (((0,), (0,)), ((), ()))


def _dot(a, b, dn=NN, precision=None):
    return lax.dot_general(a, b, dn, precision=precision, preferred_element_type=F32)


def _tile(n, pref, align=LANES):
    t = (min(pref, n) // align) * align
    while t >= align:
        if n % t == 0:
            return t
        t -= align
    return n


def _pcall(body, *, name, out_shape, grid=(), in_specs=None, out_specs=None, scratch_shapes=(),
           semantics=None, prefetch=0):
    params = dict(vmem_limit_bytes=VMEM_LIMIT)
    if semantics is not None:
        params["dimension_semantics"] = semantics
    if prefetch:
        grid_spec = pltpu.PrefetchScalarGridSpec(
            num_scalar_prefetch=prefetch, grid=grid, in_specs=in_specs, out_specs=out_specs,
            scratch_shapes=scratch_shapes)
        return pl.pallas_call(body, name=name, out_shape=out_shape, grid_spec=grid_spec,
                              compiler_params=pltpu.CompilerParams(**params))
    kw = {}
    if in_specs is not None:
        kw["in_specs"] = in_specs
    if out_specs is not None:
        kw["out_specs"] = out_specs
    return pl.pallas_call(body, name=name, out_shape=out_shape, grid=grid,
                          scratch_shapes=scratch_shapes,
                          compiler_params=pltpu.CompilerParams(**params), **kw)


def _sigmoid(x):
    return 1.0 / (1.0 + jnp.exp(-x))


def _log_sigmoid(x):
    return jnp.minimum(x, 0.0) - jnp.log(1.0 + jnp.exp(-jnp.abs(x)))


def _mm(a, b, mode, *, name, out_dtype, M=None, N=None, K=None, a_off=(0, 0), b_off=(0, 0),
        add=None, tm=1024, tn=1024, tk=1024):
    if mode == "nn":
        M = M or a.shape[0]; K = K or a.shape[1]; N = N or b.shape[1]
    elif mode == "nt":
        M = M or a.shape[0]; K = K or a.shape[1]; N = N or b.shape[0]
    else:
        K = K or a.shape[0]; M = M or a.shape[1]; N = N or b.shape[1]
    tm_align = LANES if mode == "tn" else 16
    tm = _tile(M, tm, tm_align)
    tn = _tile(N, tn)
    tk = _tile(K, tk)
    nk = K // tk
    dn = {"nn": NN, "nt": NT, "tn": TN}[mode]

    def off(o, t):
        assert o % t == 0, (name, o, t)
        return o // t

    if mode == "tn":
        ar, ac = off(a_off[0], tk), off(a_off[1], tm)
        a_spec = pl.BlockSpec((tk, tm), lambda i, j, k: (k + ar, i + ac))
    else:
        ar, ac = off(a_off[0], tm), off(a_off[1], tk)
        a_spec = pl.BlockSpec((tm, tk), lambda i, j, k: (i + ar, k + ac))
    if mode == "nt":
        br, bc = off(b_off[0], tn), off(b_off[1], tk)
        b_spec = pl.BlockSpec((tn, tk), lambda i, j, k: (j + br, k + bc))
    else:
        br, bc = off(b_off[0], tk), off(b_off[1], tn)
        b_spec = pl.BlockSpec((tk, tn), lambda i, j, k: (k + br, j + bc))
    o_spec = pl.BlockSpec((tm, tn), lambda i, j, k: (i, j))
    in_specs = [a_spec, b_spec]
    args = [a, b]
    if add is not None:
        in_specs.append(o_spec)
        args.append(add)

    def body(*refs):
        if add is not None:
            a_ref, b_ref, add_ref, o_ref, acc_ref = refs
        else:
            a_ref, b_ref, o_ref, acc_ref = refs
        k = pl.program_id(2)

        @pl.when(k == 0)
        def _():
            acc_ref[...] = jnp.zeros_like(acc_ref)

        acc_ref[...] += _dot(a_ref[...].astype(BF16), b_ref[...].astype(BF16), dn)

        @pl.when(k == nk - 1)
        def _():
            r = acc_ref[...]
            if add is not None:
                r = r + add_ref[...]
            o_ref[...] = r.astype(o_ref.dtype)

    return _pcall(body, name=name, out_shape=jax.ShapeDtypeStruct((M, N), out_dtype),
                  grid=(M // tm, N // tn, nk), in_specs=in_specs, out_specs=o_spec,
                  scratch_shapes=[pltpu.VMEM((tm, tn), F32)],
                  semantics=("parallel", "parallel", "arbitrary"))(*args)


def _rms_fwd(x, g, *, name):
    T, D = x.shape
    tr = _tile(T, 128, 16)

    def body(x_ref, g_ref, h_ref, r_ref):
        xv = x_ref[...]
        r = lax.rsqrt(jnp.mean(xv * xv, axis=-1, keepdims=True) + EPS)
        h_ref[...] = (xv * r * g_ref[...]).astype(h_ref.dtype)
        r_ref[...] = r

    row = pl.BlockSpec((tr, D), lambda i: (i, 0))
    return _pcall(body, name=name,
                  out_shape=(jax.ShapeDtypeStruct((T, D), BF16), jax.ShapeDtypeStruct((T, 1), F32)),
                  grid=(T // tr,),
                  in_specs=[row, pl.BlockSpec((1, D), lambda i: (0, 0))],
                  out_specs=(row, pl.BlockSpec((tr, 1), lambda i: (i, 0))),
                  semantics=("parallel",))(x, g)


def _rms_bwd(dh, x, rstd, g, dres, *, name):
    T, D = x.shape
    tr = _tile(T, 128, 16)
    has_res = dres is not None

    def body(*refs):
        if has_res:
            dh_ref, x_ref, r_ref, g_ref, res_ref, dx_ref, dxb_ref, dg_ref = refs
        else:
            dh_ref, x_ref, r_ref, g_ref, dx_ref, dxb_ref, dg_ref = refs
        r = r_ref[...]
        xh = x_ref[...] * r
        dhv = dh_ref[...].astype(F32)
        dxh = dhv * g_ref[...]
        m = jnp.mean(dxh * xh, axis=-1, keepdims=True)
        dx = r * (dxh - xh * m)
        if has_res:
            dx = dx + res_ref[...]
        dx_ref[...] = dx
        dxb_ref[...] = dx.astype(BF16)

        @pl.when(pl.program_id(0) == 0)
        def _():
            dg_ref[...] = jnp.zeros_like(dg_ref)

        dg_ref[...] += jnp.sum(dhv * xh, axis=0, keepdims=True)

    row = pl.BlockSpec((tr, D), lambda i: (i, 0))
    vec = pl.BlockSpec((1, D), lambda i: (0, 0))
    in_specs = [row, row, pl.BlockSpec((tr, 1), lambda i: (i, 0)), vec]
    args = [dh, x, rstd, g]
    if has_res:
        in_specs.append(row)
        args.append(dres)
    return _pcall(body, name=name,
                  out_shape=(jax.ShapeDtypeStruct((T, D), F32), jax.ShapeDtypeStruct((T, D), BF16),
                             jax.ShapeDtypeStruct((1, D), F32)),
                  grid=(T // tr,), in_specs=in_specs, out_specs=(row, row, vec),
                  semantics=("arbitrary",))(*args)


def _loss_head(x3, g, tgt):
    T, D = x3.shape
    tr = _tile(T, 128, 16)

    def body(x_ref, g_ref, t_ref, loss_ref, dx_ref, dxb_ref, dg_ref):
        xv = x_ref[...]
        gv = g_ref[...]
        r = lax.rsqrt(jnp.mean(xv * xv, axis=-1, keepdims=True) + EPS)
        xh = xv * r
        err = xh * gv - t_ref[...]
        dy = err * (1.0 / D)
        dxh = dy * gv
        m = jnp.mean(dxh * xh, axis=-1, keepdims=True)
        dx = r * (dxh - xh * m)
        dx_ref[...] = dx
        dxb_ref[...] = dx.astype(BF16)

        @pl.when(pl.program_id(0) == 0)
        def _():
            dg_ref[...] = jnp.zeros_like(dg_ref)
            loss_ref[...] = jnp.zeros_like(loss_ref)

        dg_ref[...] += jnp.sum(dy * xh, axis=0, keepdims=True)
        part = 0.5 * jnp.sum(jnp.mean(err * err, axis=-1, keepdims=True), axis=0, keepdims=True)
        loss_ref[...] += jnp.broadcast_to(part, loss_ref.shape)

    row = pl.BlockSpec((tr, D), lambda i: (i, 0))
    vec = pl.BlockSpec((1, D), lambda i: (0, 0))
    return _pcall(body, name="loss_head",
                  out_shape=(jax.ShapeDtypeStruct((1, LANES), F32), jax.ShapeDtypeStruct((T, D), F32),
                             jax.ShapeDtypeStruct((T, D), BF16), jax.ShapeDtypeStruct((1, D), F32)),
                  grid=(T // tr,), in_specs=[row, vec, row],
                  out_specs=(pl.BlockSpec((1, LANES), lambda i: (0, 0)), row, row, vec),
                  semantics=("arbitrary",))(x3, g, tgt)


def _gla_chunk_terms(qk, a_ref, w2_ref, ba_ref, DK):
    C = qk.shape[0]
    gp = _dot(a_ref[...].astype(BF16), w2_ref[...]) + ba_ref[...]
    la = _log_sigmoid(gp) * (1.0 / GLA_GATE_NORM)
    row = lax.broadcasted_iota(jnp.int32, (C, C), 0)
    col = lax.broadcasted_iota(jnp.int32, (C, C), 1)
    causal = row >= col
    b = _dot(causal.astype(F32), la, precision=HIGHEST)
    return gp, b, causal


def _gla_fwd(proj, a_pad, w2, b_a, g_gla, *, T, DK, DV):
    assert 2 * DK == DV
    H = GLA_HEADS
    HK, HV = DK // H, DV // H
    C = GLA_CHUNK
    n = T // C
    RP = a_pad.shape[1]
    scale = HK ** -0.5

    def body(qk_ref, v_ref, r_ref, a_ref, w2_ref, ba_ref, gg_ref, og_ref, oraw_ref, st_ref, s_ref):
        @pl.when(pl.program_id(0) == 0)
        def _():
            s_ref[...] = jnp.zeros_like(s_ref)

        st_ref[...] = s_ref[...]
        qk = qk_ref[...]
        _, b, causal = _gla_chunk_terms(qk, a_ref, w2_ref, ba_ref, DK)
        for h in range(H):
            ks = slice(h * HK, (h + 1) * HK)
            vs = slice(h * HV, (h + 1) * HV)
            bh = b[:, ks]
            b_last = bh[C - 1:C, :]
            qt = qk[:, ks] * scale * jnp.exp(bh)
            kh = qk[:, DK + h * HK:DK + (h + 1) * HK]
            kt = kh * jnp.exp(-bh)
            khat = kh * jnp.exp(b_last - bh)
            a_mat = jnp.where(causal, _dot(qt, kt, NT, HIGHEST), 0.0)
            vh = v_ref[:, vs]
            s_t = s_ref[h]
            o = _dot(a_mat, vh, NN, HIGHEST) + _dot(qt, s_t, NT, HIGHEST)
            s_ref[h] = s_t * jnp.exp(b_last) + _dot(vh, khat, TN, HIGHEST)
            rs = lax.rsqrt(jnp.mean(o * o, axis=-1, keepdims=True) + EPS)
            rr = r_ref[:, vs]
            og = o * rs * gg_ref[:, vs] * (rr * _sigmoid(rr))
            oraw_ref[:, vs] = o
            og_ref[:, vs] = og.astype(BF16)

    blk = lambda j: pl.BlockSpec((C, DV), lambda i: (i, j))
    full = lambda s: pl.BlockSpec(s, lambda i: (0,) * len(s))
    return _pcall(
        body, name="gla_fwd",
        out_shape=(jax.ShapeDtypeStruct((T, DV), BF16), jax.ShapeDtypeStruct((T, DV), F32),
                   jax.ShapeDtypeStruct((n, H, HV, HK), F32)),
        grid=(n,),
        in_specs=[blk(0), blk(1), blk(2), pl.BlockSpec((C, RP), lambda i: (i, 0)),
                  full((RP, DK)), full((1, DK)), full((1, DV))],
        out_specs=(blk(0), blk(0), pl.BlockSpec((None, H, HV, HK), lambda i: (i, 0, 0, 0))),
        scratch_shapes=[pltpu.VMEM((H, HV, HK), F32)],
        semantics=("arbitrary",))(proj, proj, proj, a_pad, w2, b_a, g_gla)


def _gla_bwd(proj, a_pad, w2, b_a, g_gla, o_raw, states, do_gla, *, T, DK, DV):
    H = GLA_HEADS
    HK, HV = DK // H, DV // H
    C = GLA_CHUNK
    n = T // C
    RP = a_pad.shape[1]
    scale = HK ** -0.5

    def body(qk_ref, v_ref, r_ref, a_ref, w2_ref, ba_ref, gg_ref, oraw_ref, st_ref, dog_ref,
             dqkvr_ref, da_ref, dw2_ref, dba_ref, dgg_ref, ds_ref):
        @pl.when(pl.program_id(0) == 0)
        def _():
            ds_ref[...] = jnp.zeros_like(ds_ref)
            dw2_ref[...] = jnp.zeros_like(dw2_ref)
            dba_ref[...] = jnp.zeros_like(dba_ref)
            dgg_ref[...] = jnp.zeros_like(dgg_ref)

        qk = qk_ref[...]
        gp, b, causal = _gla_chunk_terms(qk, a_ref, w2_ref, ba_ref, DK)
        row = lax.broadcasted_iota(jnp.int32, (C, C), 0)
        col = lax.broadcasted_iota(jnp.int32, (C, C), 1)
        upper = (col >= row).astype(F32)
        dla_parts = []
        for h in range(H):
            ks = slice(h * HK, (h + 1) * HK)
            vs = slice(h * HV, (h + 1) * HV)
            bh = b[:, ks]
            b_last = bh[C - 1:C, :]
            eb = jnp.exp(bh)
            emb = jnp.exp(-bh)
            ehat = jnp.exp(b_last - bh)
            e_last = jnp.exp(b_last)
            qt = qk[:, ks] * scale * eb
            kh = qk[:, DK + h * HK:DK + (h + 1) * HK]
            kt = kh * emb
            khat = kh * ehat
            a_mat = jnp.where(causal, _dot(qt, kt, NT, HIGHEST), 0.0)
            vh = v_ref[:, vs]
            o = oraw_ref[:, vs]
            rs = lax.rsqrt(jnp.mean(o * o, axis=-1, keepdims=True) + EPS)
            on = o * rs
            gg = gg_ref[:, vs]
            rr = r_ref[:, vs]
            sg = _sigmoid(rr)
            d_out = dog_ref[:, vs]
            dr = d_out * (on * gg) * (sg * (1.0 + rr * (1.0 - sg)))
            d_og = d_out * (rr * sg)
            dgg_ref[:, vs] += jnp.sum(d_og * on, axis=0, keepdims=True)
            d_on = d_og * gg
            d_o = rs * (d_on - on * jnp.mean(d_on * on, axis=-1, keepdims=True))
            s_t = st_ref[h]
            ds_t = ds_ref[h]
            d_a = jnp.where(causal, _dot(d_o, vh, NT, HIGHEST), 0.0)
            dv = _dot(a_mat, d_o, TN, HIGHEST) + _dot(khat, ds_t, NT, HIGHEST)
            dqt = _dot(d_a, kt, NN, HIGHEST) + _dot(d_o, s_t, NN, HIGHEST)
            dkt = _dot(d_a, qt, TN, HIGHEST)
            dkhat = _dot(vh, ds_t, NN, HIGHEST)
            ds_ref[h] = ds_t * e_last + _dot(d_o, qt, TN, HIGHEST)
            dq = dqt * eb * scale
            dk = dkt * emb + dkhat * ehat
            db = dqt * qt - dkt * kt - dkhat * khat
            d_last = (jnp.sum(dkhat * khat, axis=0, keepdims=True)
                      + e_last * jnp.sum(ds_t * s_t, axis=0, keepdims=True))
            dla_parts.append(_dot(upper, db, NN, HIGHEST) + d_last)
            dqkvr_ref[:, ks] = dq.astype(BF16)
            dqkvr_ref[:, DK + h * HK:DK + (h + 1) * HK] = dk.astype(BF16)
            dqkvr_ref[:, DV + h * HV:DV + (h + 1) * HV] = dv.astype(BF16)
            dqkvr_ref[:, 2 * DV + h * HV:2 * DV + (h + 1) * HV] = dr.astype(BF16)
        dla = jnp.concatenate(dla_parts, axis=1)
        dgp = dla * (1.0 / GLA_GATE_NORM) * _sigmoid(-gp)
        dba_ref[...] += jnp.sum(dgp, axis=0, keepdims=True)
        dgp_b = dgp.astype(BF16)
        dw2_ref[...] += _dot(a_ref[...].astype(BF16), dgp_b, TN)
        da_ref[...] = _dot(dgp_b, w2_ref[...], NT).astype(BF16)

    rev = lambda j: pl.BlockSpec((C, DV), lambda i: (n - 1 - i, j))
    full = lambda s: pl.BlockSpec(s, lambda i: (0,) * len(s))
    return _pcall(
        body, name="gla_bwd",
        out_shape=(jax.ShapeDtypeStruct((T, 3 * DV), BF16), jax.ShapeDtypeStruct((T, RP), BF16),
                   jax.ShapeDtypeStruct((RP, DK), F32), jax.ShapeDtypeStruct((1, DK), F32),
                   jax.ShapeDtypeStruct((1, DV), F32)),
        grid=(n,),
        in_specs=[rev(0), rev(1), rev(2), pl.BlockSpec((C, RP), lambda i: (n - 1 - i, 0)),
                  full((RP, DK)), full((1, DK)), full((1, DV)), rev(0),
                  pl.BlockSpec((None, H, HV, HK), lambda i: (n - 1 - i, 0, 0, 0)), rev(0)],
        out_specs=(pl.BlockSpec((C, 3 * DV), lambda i: (n - 1 - i, 0)),
                   pl.BlockSpec((C, RP), lambda i: (n - 1 - i, 0)),
                   full((RP, DK)), full((1, DK)), full((1, DV))),
        scratch_shapes=[pltpu.VMEM((H, HV, HK), F32)],
        semantics=("arbitrary",))(proj, proj, proj, a_pad, w2, b_a, g_gla, o_raw, states, do_gla)


def _pool_windows(p, g, T):
    t = lax.broadcasted_iota(jnp.int32, (T, 1), 0)
    s = p
    for lvl in range(POOL_GROUPS):
        sh = 1 << lvl
        nxt = s + jnp.where(t >= sh, pltpu.roll(s, sh, 0), 0.0)
        s = jnp.where(lvl <= g, nxt, s)
    win = jnp.left_shift(2, g)
    inv = 1.0 / jnp.minimum(t + 1, win).astype(F32)
    return s * inv - p, inv


def _pool_fwd(proj, w_pool, scale, *, T, PW, col_block):
    GW = PW // POOL_GROUPS
    per = PW // GW

    def body(p_ref, w_ref, s_ref, o_ref):
        g = pl.program_id(0)
        pooled, _ = _pool_windows(p_ref[...], g, T)
        mixed = _dot(pooled.astype(BF16), w_ref[...])
        o_ref[...] = (mixed * s_ref[...]).astype(BF16)

    return _pcall(body, name="pool_fwd", out_shape=jax.ShapeDtypeStruct((T, PW), BF16),
                  grid=(POOL_GROUPS,),
                  in_specs=[pl.BlockSpec((T, GW), lambda g: (0, col_block * per + g)),
                            pl.BlockSpec((None, GW, GW), lambda g: (g, 0, 0)),
                            pl.BlockSpec((1, GW), lambda g: (0, g))],
                  out_specs=pl.BlockSpec((T, GW), lambda g: (0, g)),
                  semantics=("parallel",))(proj, w_pool, scale)


def _pool_bwd(proj, w_pool, scale, do_pool, *, T, PW, col_block):
    GW = PW // POOL_GROUPS
    per = PW // GW

    def body(p_ref, w_ref, s_ref, do_ref, dp_ref, dw_ref, dsc_ref):
        g = pl.program_id(0)
        pooled, inv = _pool_windows(p_ref[...], g, T)
        pooled_b = pooled.astype(BF16)
        w = w_ref[...]
        mixed = _dot(pooled_b, w)
        d_out = do_ref[...]
        dsc_ref[...] = jnp.sum(d_out * mixed, axis=0, keepdims=True)
        dmixed = (d_out * s_ref[...]).astype(BF16)
        dw_ref[...] = _dot(pooled_b, dmixed, TN)
        dpooled = _dot(dmixed, w, NT)
        t = lax.broadcasted_iota(jnp.int32, (T, 1), 0)
        s = dpooled * inv
        for lvl in range(POOL_GROUPS):
            sh = 1 << lvl
            nxt = s + jnp.where(t < T - sh, pltpu.roll(s, T - sh, 0), 0.0)
            s = jnp.where(lvl <= g, nxt, s)
        dp_ref[...] = (s - dpooled).astype(BF16)

    return _pcall(body, name="pool_bwd",
                  out_shape=(jax.ShapeDtypeStruct((T, PW), BF16),
                             jax.ShapeDtypeStruct((POOL_GROUPS, GW, GW), F32),
                             jax.ShapeDtypeStruct((1, PW), F32)),
                  grid=(POOL_GROUPS,),
                  in_specs=[pl.BlockSpec((T, GW), lambda g: (0, col_block * per + g)),
                            pl.BlockSpec((None, GW, GW), lambda g: (g, 0, 0)),
                            pl.BlockSpec((1, GW), lambda g: (0, g)),
                            pl.BlockSpec((T, GW), lambda g: (0, g))],
                  out_specs=(pl.BlockSpec((T, GW), lambda g: (0, g)),
                             pl.BlockSpec((None, GW, GW), lambda g: (g, 0, 0)),
                             pl.BlockSpec((1, GW), lambda g: (0, g))),
                  semantics=("parallel",))(proj, w_pool, scale, do_pool)


def _merge_fwd(y_gla, y_pool, proj, *, T, D, col_block):
    tr = _tile(T, 128, 16)

    def body(yg_ref, yp_ref, g1_ref, g2_ref, o_ref):
        o_ref[...] = (_sigmoid(g1_ref[...]) * yg_ref[...]
                      + _sigmoid(g2_ref[...]) * yp_ref[...]).astype(BF16)

    row = pl.BlockSpec((tr, D), lambda i: (i, 0))
    return _pcall(body, name="merge_fwd", out_shape=jax.ShapeDtypeStruct((T, D), BF16),
                  grid=(T // tr,),
                  in_specs=[row, row, pl.BlockSpec((tr, D), lambda i: (i, col_block)),
                            pl.BlockSpec((tr, D), lambda i: (i, col_block + 1))],
                  out_specs=row, semantics=("parallel",))(y_gla, y_pool, proj, proj)


def _merge_bwd(dmerged, y_gla, y_pool, proj, *, T, D, col_block):
    tr = _tile(T, 128, 16)

    def body(dm_ref, yg_ref, yp_ref, g1_ref, g2_ref, dyg_ref, dyp_ref, dg_ref):
        dm = dm_ref[...]
        s1 = _sigmoid(g1_ref[...])
        s2 = _sigmoid(g2_ref[...])
        dyg_ref[...] = (dm * s1).astype(BF16)
        dyp_ref[...] = (dm * s2).astype(BF16)
        dg_ref[:, :D] = (dm * yg_ref[...] * s1 * (1.0 - s1)).astype(BF16)
        dg_ref[:, D:] = (dm * yp_ref[...] * s2 * (1.0 - s2)).astype(BF16)

    row = pl.BlockSpec((tr, D), lambda i: (i, 0))
    return _pcall(body, name="merge_bwd",
                  out_shape=(jax.ShapeDtypeStruct((T, D), BF16), jax.ShapeDtypeStruct((T, D), BF16),
                             jax.ShapeDtypeStruct((T, 2 * D), BF16)),
                  grid=(T // tr,),
                  in_specs=[row, row, row, pl.BlockSpec((tr, D), lambda i: (i, col_block)),
                            pl.BlockSpec((tr, D), lambda i: (i, col_block + 1))],
                  out_specs=(row, row, pl.BlockSpec((tr, 2 * D), lambda i: (i, 0))),
                  semantics=("parallel",))(dmerged, y_gla, y_pool, proj, proj)


def _attn_fwd(q, kv, *, T, D, M):
    H = CROSS_HEADS
    HD = D // H
    tq = _tile(T, 512, 16)
    scale = HD ** -0.5

    def body(q_ref, kv_ref, o_ref):
        for h in range(H):
            hs = slice(h * HD, (h + 1) * HD)
            s = _dot(q_ref[:, hs], kv_ref[:, hs], NT) * scale
            e = jnp.exp(s - jnp.max(s, axis=-1, keepdims=True))
            p = e / jnp.sum(e, axis=-1, keepdims=True)
            o_ref[:, hs] = _dot(p.astype(BF16), kv_ref[:, D + h * HD:D + (h + 1) * HD]).astype(BF16)

    row = pl.BlockSpec((tq, D), lambda i: (i, 0))
    return _pcall(body, name="attn_fwd", out_shape=jax.ShapeDtypeStruct((T, D), BF16),
                  grid=(T // tq,), in_specs=[row, pl.BlockSpec((M, 2 * D), lambda i: (0, 0))],
                  out_specs=row, semantics=("parallel",))(q, kv)


def _attn_bwd(q, kv, do, *, T, D, M):
    H = CROSS_HEADS
    HD = D // H
    tq = _tile(T, 512, 16)
    scale = HD ** -0.5

    def body(q_ref, kv_ref, do_ref, dq_ref, dkv_ref):
        @pl.when(pl.program_id(0) == 0)
        def _():
            dkv_ref[...] = jnp.zeros_like(dkv_ref)

        for h in range(H):
            hs = slice(h * HD, (h + 1) * HD)
            vs = slice(D + h * HD, D + (h + 1) * HD)
            qh = q_ref[:, hs]
            kh = kv_ref[:, hs]
            s = _dot(qh, kh, NT) * scale
            e = jnp.exp(s - jnp.max(s, axis=-1, keepdims=True))
            p = e / jnp.sum(e, axis=-1, keepdims=True)
            p_b = p.astype(BF16)
            d_o = do_ref[:, hs]
            dkv_ref[:, vs] += _dot(p_b, d_o, TN)
            dp = _dot(d_o, kv_ref[:, vs], NT)
            ds = (p * (dp - jnp.sum(dp * p, axis=-1, keepdims=True)) * scale).astype(BF16)
            dq_ref[:, hs] = _dot(ds, kh).astype(BF16)
            dkv_ref[:, hs] += _dot(ds, qh, TN)

    row = pl.BlockSpec((tq, D), lambda i: (i, 0))
    full = pl.BlockSpec((M, 2 * D), lambda i: (0, 0))
    return _pcall(body, name="attn_bwd",
                  out_shape=(jax.ShapeDtypeStruct((T, D), BF16), jax.ShapeDtypeStruct((M, 2 * D), F32)),
                  grid=(T // tq,), in_specs=[row, full, row], out_specs=(row, full),
                  semantics=("arbitrary",))(q, kv, do)


def _shift_down(x, halo, s, t):
    out = pltpu.roll(x, s, 0)
    for j in range(s):
        out = jnp.where(t == j, halo[SUBLANES - s + j:SUBLANES - s + j + 1, :], out)
    return out


def _shift_up(x, halo, s, t, rows):
    out = pltpu.roll(x, rows - s, 0)
    for j in range(s):
        out = jnp.where(t == rows - s + j, halo[j:j + 1, :], out)
    return out


def _conv_tiles(T):
    tt = _tile(T, 128, SUBLANES)
    return tt, tt // SUBLANES, T // SUBLANES


def _conv_fwd(u0, conv_w, conv_b, *, T, F):
    tt, hb, _ = _conv_tiles(T)
    cw = _tile(F, 512)

    def body(u_ref, prev_ref, w_ref, b_ref, f_ref):
        i = pl.program_id(0)
        t = lax.broadcasted_iota(jnp.int32, (tt, 1), 0)

        def conv(cs):
            x = u_ref[:, cs]
            halo = jnp.where(i > 0, prev_ref[:, cs], 0.0)
            return (w_ref[2:3, cs] * x + w_ref[1:2, cs] * _shift_down(x, halo, 1, t)
                    + w_ref[0:1, cs] * _shift_down(x, halo, 2, t) + b_ref[:, cs])

        for j in range(F // cw):
            gate = conv(slice(j * cw, (j + 1) * cw))
            val = conv(slice(F + j * cw, F + (j + 1) * cw))
            f_ref[:, j * cw:(j + 1) * cw] = (gate * _sigmoid(gate) * val).astype(BF16)

    return _pcall(body, name="conv_fwd", out_shape=jax.ShapeDtypeStruct((T, F), BF16),
                  grid=(T // tt,),
                  in_specs=[pl.BlockSpec((tt, 2 * F), lambda i: (i, 0)),
                            pl.BlockSpec((SUBLANES, 2 * F), lambda i: (jnp.maximum(i * hb - 1, 0), 0)),
                            pl.BlockSpec((CONV_W, 2 * F), lambda i: (0, 0)),
                            pl.BlockSpec((1, 2 * F), lambda i: (0, 0))],
                  out_specs=pl.BlockSpec((tt, F), lambda i: (i, 0)),
                  semantics=("parallel",))(u0, u0, conv_w, conv_b)


def _conv_bwd(u0, conv_w, conv_b, df, *, T, F):
    tt, hb, nb = _conv_tiles(T)
    nt = T // tt
    cw = _tile(F, 512)

    def body(u_ref, prev_ref, next_ref, df_ref, dfn_ref, w_ref, b_ref, du0_ref, dw_ref, db_ref):
        i = pl.program_id(0)
        t = lax.broadcasted_iota(jnp.int32, (tt, 1), 0)
        t8 = lax.broadcasted_iota(jnp.int32, (SUBLANES, 1), 0)

        @pl.when(i == 0)
        def _():
            dw_ref[...] = jnp.zeros_like(dw_ref)
            db_ref[...] = jnp.zeros_like(db_ref)

        def conv(cs):
            x = u_ref[:, cs]
            halo = jnp.where(i > 0, prev_ref[:, cs], 0.0)
            x1 = _shift_down(x, halo, 1, t)
            x2 = _shift_down(x, halo, 2, t)
            u = w_ref[2:3, cs] * x + w_ref[1:2, cs] * x1 + w_ref[0:1, cs] * x2 + b_ref[:, cs]
            xn = next_ref[:, cs]
            tail = x[tt - SUBLANES:, :]
            un = (w_ref[2:3, cs] * xn + w_ref[1:2, cs] * _shift_down(xn, tail, 1, t8)
                  + w_ref[0:1, cs] * _shift_down(xn, tail, 2, t8) + b_ref[:, cs])
            return u, un, (x, x1, x2)

        def glu_grad(gate, val, dff):
            sg = _sigmoid(gate)
            return dff * val * (sg * (1.0 + gate * (1.0 - sg))), dff * (gate * sg)

        def finish(cs, du, dun, xs):
            du0 = (w_ref[2:3, cs] * du + w_ref[1:2, cs] * _shift_up(du, dun, 1, t, tt)
                   + w_ref[0:1, cs] * _shift_up(du, dun, 2, t, tt))
            du0_ref[:, cs] = du0.astype(BF16)
            db_ref[:, cs] += jnp.sum(du, axis=0, keepdims=True)
            dw_ref[2:3, cs] += jnp.sum(du * xs[0], axis=0, keepdims=True)
            dw_ref[1:2, cs] += jnp.sum(du * xs[1], axis=0, keepdims=True)
            dw_ref[0:1, cs] += jnp.sum(du * xs[2], axis=0, keepdims=True)

        for j in range(F // cw):
            fs = slice(j * cw, (j + 1) * cw)
            gs, vs = fs, slice(F + j * cw, F + (j + 1) * cw)
            ug, ung, xg = conv(gs)
            uv, unv, xv = conv(vs)
            dug, duv = glu_grad(ug, uv, df_ref[:, fs].astype(F32))
            dung, dunv = glu_grad(ung, unv, dfn_ref[0:SUBLANES, fs].astype(F32))
            dung = jnp.where(i < nt - 1, dung, 0.0)
            dunv = jnp.where(i < nt - 1, dunv, 0.0)
            finish(gs, dug, dung, xg)
            finish(vs, duv, dunv, xv)

    wide = lambda rows, fn: pl.BlockSpec((rows, 2 * F), fn)
    nxt = lambda i: (jnp.minimum((i + 1) * hb, nb - 1), 0)
    return _pcall(body, name="conv_bwd",
                  out_shape=(jax.ShapeDtypeStruct((T, 2 * F), BF16),
                             jax.ShapeDtypeStruct((CONV_W, 2 * F), F32),
                             jax.ShapeDtypeStruct((1, 2 * F), F32)),
                  grid=(nt,),
                  in_specs=[wide(tt, lambda i: (i, 0)),
                            wide(SUBLANES, lambda i: (jnp.maximum(i * hb - 1, 0), 0)),
                            wide(SUBLANES, nxt),
                            pl.BlockSpec((tt, F), lambda i: (i, 0)),
                            pl.BlockSpec((2 * SUBLANES, F),
                                         lambda i: (jnp.minimum((i + 1) * (hb // 2), nb // 2 - 1), 0)),
                            wide(CONV_W, lambda i: (0, 0)), wide(1, lambda i: (0, 0))],
                  out_specs=(wide(tt, lambda i: (i, 0)), wide(CONV_W, lambda i: (0, 0)),
                             wide(1, lambda i: (0, 0))),
                  semantics=("arbitrary",))(u0, u0, u0, df, df, conv_w, conv_b)


def _adamw(w, g, m, v, *, name):
    R, C = w.shape
    tr = _tile(R, max(SUBLANES, (1 << 19) // max(C, 1) // SUBLANES * SUBLANES), SUBLANES)
    c1 = 1.0 / (1.0 - ADAM_B1 ** ADAM_STEP)
    c2 = 1.0 / (1.0 - ADAM_B2 ** ADAM_STEP)

    def body(w_ref, g_ref, m_ref, v_ref, d_ref, mo_ref, vo_ref):
        gv = g_ref[...]
        mn = ADAM_B1 * m_ref[...] + (1.0 - ADAM_B1) * gv
        vn = ADAM_B2 * v_ref[...] + (1.0 - ADAM_B2) * (gv * gv)
        d_ref[...] = -ADAM_LR * ((mn * c1) / (jnp.sqrt(vn * c2) + ADAM_EPS) + ADAM_WD * w_ref[...])
        mo_ref[...] = mn
        vo_ref[...] = vn

    blk = pl.BlockSpec((tr, C), lambda i: (i, 0))
    shp = jax.ShapeDtypeStruct((R, C), F32)
    return _pcall(body, name=name, out_shape=(shp, shp, shp), grid=(R // tr,),
                  in_specs=[blk] * 4, out_specs=(blk,) * 3, semantics=("parallel",))(w, g, m, v)


def _blk(h, C, elems=1 << 19, align=16):
    th = _tile(h, max(align, elems // C // align * align), align)
    if th < h or h * C <= 2 * elems:
        return th, C
    return h, _tile(C, max(LANES, elems // h // LANES * LANES))


def _adamw_halves(w, m, v, g_mine, g_other, c_idx, *, name):
    _, h, C = w.shape
    th, tc = _blk(h, C, align=SUBLANES)
    c1 = 1.0 / (1.0 - ADAM_B1 ** ADAM_STEP)
    c2 = 1.0 / (1.0 - ADAM_B2 ** ADAM_STEP)

    def body(c_ref, w_ref, m_ref, v_ref, gm_ref, go_ref, g_ref, d_ref, mo_ref, vo_ref):
        gv = jnp.where(pl.program_id(0) == c_ref[0], gm_ref[...], go_ref[...])
        mn = ADAM_B1 * m_ref[...] + (1.0 - ADAM_B1) * gv
        vn = ADAM_B2 * v_ref[...] + (1.0 - ADAM_B2) * (gv * gv)
        d_ref[...] = -ADAM_LR * ((mn * c1) / (jnp.sqrt(vn * c2) + ADAM_EPS) + ADAM_WD * w_ref[...])
        g_ref[...] = gv
        mo_ref[...] = mn
        vo_ref[...] = vn

    blk = pl.BlockSpec((None, th, tc), lambda s, i, j, c: (s, i, j))

    def pick(mine):
        def index(s, i, j, c):
            use = (s == c[0]) if mine else (s != c[0])
            return jnp.where(use, i, 0), jnp.where(use, j, 0)
        return pl.BlockSpec((th, tc), index)

    shp = jax.ShapeDtypeStruct((2, h, C), F32)
    return _pcall(body, name=name, out_shape=(shp,) * 4, grid=(2, h // th, C // tc), prefetch=1,
                  in_specs=[blk, blk, blk, pick(True), pick(False)], out_specs=(blk,) * 4,
                  semantics=("parallel", "parallel", "parallel"))(c_idx, w, m, v, g_mine, g_other)


def _mesh_pos():
    x, y, c = lax.axis_index("x"), lax.axis_index("y"), lax.axis_index("c")
    others = [(1 - x, y), (x, 1 - y), (1 - x, 1 - y)]
    return x, y, c, others


def _all_gather_weights(shards):
    n = len(shards)
    ANY = pl.BlockSpec(memory_space=pl.ANY)
    PER = 7

    def body(*refs):
        ins, outs = refs[:n], refs[n:2 * n]
        send_sems, recv_sems = refs[2 * n:]
        x, y, c, others = _mesh_pos()
        me = 2 * x + y
        sibling = (x, y, 1 - c)

        def copy(a, k, src, dst, to):
            return pltpu.make_async_remote_copy(
                src_ref=src, dst_ref=dst, send_sem=send_sems.at[PER * a + k],
                recv_sem=recv_sems.at[PER * a + k], device_id=to, device_id_type=MESH)

        sent = []
        for a in range(n):
            for j, chip in enumerate(others):
                cp = copy(a, j, ins[a].at[c], outs[a].at[me, c], (*chip, c))
                cp.start()
                sent.append(cp)
        own = [copy(a, 6, ins[a], outs[a].at[me], sibling) for a in range(n)]
        for cp in own:
            cp.start()
        for a in range(n):
            for j, chip in enumerate(others):
                theirs = outs[a].at[2 * chip[0] + chip[1], c]
                copy(a, j, theirs, theirs, (*chip, c)).wait_recv()
                cp = copy(a, 3 + j, theirs, theirs, sibling)
                cp.start()
                sent.append(cp)
        for a in range(n):
            for j, chip in enumerate(others):
                theirs = outs[a].at[2 * chip[0] + chip[1], 1 - c]
                copy(a, 3 + j, theirs, theirs, sibling).wait_recv()
        for cp in own:
            cp.wait()
        for cp in sent:
            cp.wait_send()

    return _pcall(body, name="all_gather_weights",
                  out_shape=[jax.ShapeDtypeStruct((N_CHIPS, *s.shape), s.dtype) for s in shards],
                  in_specs=[ANY] * n, out_specs=[ANY] * n,
                  scratch_shapes=[pltpu.SemaphoreType.DMA((PER * n,)),
                                  pltpu.SemaphoreType.DMA((PER * n,))])(*shards)


def _exchange_halves(grads):
    n = len(grads)
    ANY = pl.BlockSpec(memory_space=pl.ANY)

    def body(*refs):
        ins, outs = refs[:n], refs[n:2 * n]
        send_sems, recv_sems = refs[2 * n:]
        x, y, c, _ = _mesh_pos()
        copies = [pltpu.make_async_remote_copy(
            src_ref=ins[a].at[:, 1 - c], dst_ref=outs[a], send_sem=send_sems.at[a],
            recv_sem=recv_sems.at[a], device_id=(x, y, 1 - c), device_id_type=MESH) for a in range(n)]
        for cp in copies:
            cp.start()
        for cp in copies:
            cp.wait()

    return _pcall(body, name="grad_exchange_halves",
                  out_shape=[jax.ShapeDtypeStruct((g.shape[0], *g.shape[2:]), g.dtype) for g in grads],
                  in_specs=[ANY] * n, out_specs=[ANY] * n,
                  scratch_shapes=[pltpu.SemaphoreType.DMA((n,)), pltpu.SemaphoreType.DMA((n,))])(*grads)


def _add_halves(grad, recv, c_idx, *, name):
    S, _, h, C = grad.shape
    th, tc = _blk(h, C)

    def body(c_ref, g_ref, r_ref, o_ref):
        o_ref[...] = (g_ref[...].astype(F32) + r_ref[...].astype(F32)).astype(o_ref.dtype)

    return _pcall(body, name=name, out_shape=jax.ShapeDtypeStruct((S, h, C), grad.dtype),
                  grid=(S, h // th, C // tc), prefetch=1,
                  in_specs=[pl.BlockSpec((None, None, th, tc), lambda s, i, j, c: (s, c[0], i, j)),
                            pl.BlockSpec((None, th, tc), lambda s, i, j, c: (s, i, j))],
                  out_specs=pl.BlockSpec((None, th, tc), lambda s, i, j, c: (s, i, j)),
                  semantics=("parallel", "parallel", "parallel"))(c_idx, grad, recv)


def _scatter_to_chips(sums):
    n = len(sums)
    ANY = pl.BlockSpec(memory_space=pl.ANY)

    def body(*refs):
        ins, outs = refs[:n], refs[n:2 * n]
        send_sems, recv_sems = refs[2 * n:]
        x, y, c, others = _mesh_pos()
        copies = []
        for a in range(n):
            for j, chip in enumerate(others):
                copies.append(pltpu.make_async_remote_copy(
                    src_ref=ins[a].at[2 * chip[0] + chip[1]], dst_ref=outs[a].at[j],
                    send_sem=send_sems.at[3 * a + j], recv_sem=recv_sems.at[3 * a + j],
                    device_id=(*chip, c), device_id_type=MESH))
        for cp in copies:
            cp.start()
        for cp in copies:
            cp.wait()

    return _pcall(body, name="grad_scatter_to_chips",
                  out_shape=[jax.ShapeDtypeStruct((3, *s.shape[1:]), s.dtype) for s in sums],
                  in_specs=[ANY] * n, out_specs=[ANY] * n,
                  scratch_shapes=[pltpu.SemaphoreType.DMA((3 * n,)), pltpu.SemaphoreType.DMA((3 * n,))])(*sums)


def _add_chips(sums, recv, chip_idx, *, name):
    _, h, C = sums.shape
    th, tc = _blk(h, C)

    def body(k_ref, s_ref, r_ref, o_ref):
        acc = s_ref[...].astype(F32) + r_ref[0].astype(F32)
        acc = acc + r_ref[1].astype(F32)
        o_ref[...] = acc + r_ref[2].astype(F32)

    return _pcall(body, name=name, out_shape=jax.ShapeDtypeStruct((h, C), F32),
                  grid=(h // th, C // tc), prefetch=1,
                  in_specs=[pl.BlockSpec((None, th, tc), lambda i, j, k: (k[0], i, j)),
                            pl.BlockSpec((3, th, tc), lambda i, j, k: (0, i, j))],
                  out_specs=pl.BlockSpec((th, tc), lambda i, j, k: (i, j)),
                  semantics=("parallel", "parallel"))(chip_idx, sums, recv)


def _swap_halves(halves):
    n = len(halves)
    ANY = pl.BlockSpec(memory_space=pl.ANY)

    def body(*refs):
        ins, outs = refs[:n], refs[n:2 * n]
        send_sems, recv_sems = refs[2 * n:]
        x, y, c, _ = _mesh_pos()
        copies = [pltpu.make_async_remote_copy(
            src_ref=ins[a], dst_ref=outs[a], send_sem=send_sems.at[a], recv_sem=recv_sems.at[a],
            device_id=(x, y, 1 - c), device_id_type=MESH) for a in range(n)]
        for cp in copies:
            cp.start()
        for cp in copies:
            cp.wait()

    return _pcall(body, name="grad_swap_halves",
                  out_shape=[jax.ShapeDtypeStruct(s.shape, s.dtype) for s in halves],
                  in_specs=[ANY] * n, out_specs=[ANY] * n,
                  scratch_shapes=[pltpu.SemaphoreType.DMA((n,)), pltpu.SemaphoreType.DMA((n,))])(*halves)


def _all_reduce_small(buf):
    R, L = buf.shape
    NDEV = 8

    def body(x_ref, sum_ref, all_ref, send_sems, recv_sems, local_sem):
        x, y, c, others = _mesh_pos()
        me, sibling = (x, y, c), (x, y, 1 - c)

        def slot(px, py, pc):
            return all_ref.at[4 * px + 2 * py + pc]

        def copy(k, block, to, src=None):
            return pltpu.make_async_remote_copy(
                src_ref=slot(*block) if src is None else src, dst_ref=slot(*block),
                send_sem=send_sems.at[k], recv_sem=recv_sems.at[k], device_id=to, device_id_type=MESH)

        mine = pltpu.make_async_copy(x_ref, slot(*me), local_sem)
        mine.start()
        first = [copy(0, me, sibling, src=x_ref)]
        first += [copy(1 + j, me, (*chip, c), src=x_ref) for j, chip in enumerate(others)]
        for cp in first:
            cp.start()
        passed = [copy(4 + j, (*chip, c), sibling) for j, chip in enumerate(others)]
        for j, chip in enumerate(others):
            copy(1 + j, (*chip, c), me).wait_recv()
            passed[j].start()
        copy(0, sibling, me).wait_recv()
        for j, chip in enumerate(others):
            copy(4 + j, (*chip, 1 - c), me).wait_recv()
        for cp in first + passed:
            cp.wait_send()
        mine.wait()
        acc = all_ref[0]
        for d in range(1, NDEV):
            acc = acc + all_ref[d]
        sum_ref[...] = acc

    VM = pl.BlockSpec(memory_space=pltpu.VMEM)
    return _pcall(body, name="all_reduce_small",
                  out_shape=(jax.ShapeDtypeStruct((R, L), F32), jax.ShapeDtypeStruct((NDEV, R, L), F32)),
                  in_specs=[VM], out_specs=(VM, VM),
                  scratch_shapes=[pltpu.SemaphoreType.DMA((7,)), pltpu.SemaphoreType.DMA((7,)),
                                  pltpu.SemaphoreType.DMA])(buf)[0]


def _pack(arrs, rows_multiple=16):
    flat = [a.reshape(-1).astype(F32) for a in arrs]
    sizes = [f.shape[0] for f in flat]
    total = sum(sizes)
    per = LANES * rows_multiple
    padded = -(-total // per) * per
    flat.append(jnp.zeros((padded - total,), F32))
    offs = [0]
    for s in sizes:
        offs.append(offs[-1] + s)
    return jnp.concatenate(flat).reshape(padded // LANES, LANES), offs


def _unpack(buf, offs, shapes):
    flat = buf.reshape(-1)
    return [flat[offs[i]:offs[i + 1]].reshape(s) for i, s in enumerate(shapes)]


def kernel(x, mem, g_mix, w_in, w_a2, b_a, g_gla, w_pool, pool_scale, w_branch, w_out, g_cross, g_mem, w_cq, w_ckv, w_co, g_ffn, w_up, conv_w, conv_b, w_down, g_final, loss_target, m_g_mix, m_w_in, m_w_a2, m_b_a, m_g_gla, m_w_pool, m_pool_scale, m_w_branch, m_w_out, m_g_cross, m_g_mem, m_w_cq, m_w_ckv, m_w_co, m_g_ffn, m_w_up, m_conv_w, m_conv_b, m_w_down, m_g_final, v_g_mix, v_w_in, v_w_a2, v_b_a, v_g_gla, v_w_pool, v_pool_scale, v_w_branch, v_w_out, v_g_cross, v_g_mem, v_w_cq, v_w_ckv, v_w_co, v_g_ffn, v_w_up, v_conv_w, v_conv_b, v_w_down, v_g_final):
    weights = dict(g_mix=g_mix, w_in=w_in, w_a2=w_a2, b_a=b_a, g_gla=g_gla, w_pool=w_pool,
                   pool_scale=pool_scale, w_branch=w_branch, w_out=w_out, g_cross=g_cross, g_mem=g_mem,
                   w_cq=w_cq, w_ckv=w_ckv, w_co=w_co, g_ffn=g_ffn, w_up=w_up, conv_w=conv_w,
                   conv_b=conv_b, w_down=w_down, g_final=g_final)
    mom_m = dict(g_mix=m_g_mix, w_in=m_w_in, w_a2=m_w_a2, b_a=m_b_a, g_gla=m_g_gla, w_pool=m_w_pool,
                 pool_scale=m_pool_scale, w_branch=m_w_branch, w_out=m_w_out, g_cross=m_g_cross,
                 g_mem=m_g_mem, w_cq=m_w_cq, w_ckv=m_w_ckv, w_co=m_w_co, g_ffn=m_g_ffn, w_up=m_w_up,
                 conv_w=m_conv_w, conv_b=m_conv_b, w_down=m_w_down, g_final=m_g_final)
    mom_v = dict(g_mix=v_g_mix, w_in=v_w_in, w_a2=v_w_a2, b_a=v_b_a, g_gla=v_g_gla, w_pool=v_w_pool,
                 pool_scale=v_pool_scale, w_branch=v_w_branch, w_out=v_w_out, g_cross=v_g_cross,
                 g_mem=v_g_mem, w_cq=v_w_cq, w_ckv=v_w_ckv, w_co=v_w_co, g_ffn=v_g_ffn, w_up=v_w_up,
                 conv_w=v_conv_w, conv_b=v_conv_b, w_down=v_w_down, g_final=v_g_final)
    order = list(weights)
    big = ["w_in", "w_branch", "w_out", "w_cq", "w_ckv", "w_co", "w_up", "w_down"]
    small_sharded = ["w_a2", "w_pool", "conv_w"]
    small_repl = ["g_mix", "b_a", "g_gla", "pool_scale", "g_cross", "g_mem", "g_ffn", "conv_b", "g_final"]

    xs, ms, tgt = x[0], mem[0], loss_target[0]
    T, D = xs.shape
    M = ms.shape[0]
    DK, DV, PW = b_a.shape[1], g_gla.shape[1], pool_scale.shape[1]
    RANK = w_a2.shape[1]
    F2 = conv_b.shape[1]
    F = F2 // 2
    DIN = N_CHIPS * w_in.shape[2]
    OFF_A = 2 * DK + 2 * DV
    OFF_P = OFF_A + RANK
    RP = LANES
    GW = PW // POOL_GROUPS
    assert PW == DV and 4 * DV == 2 * D and OFF_P + PW + 2 * D == DIN

    cx, cy, cc = lax.axis_index("x"), lax.axis_index("y"), lax.axis_index("c")
    chip = 2 * cx + cy
    c_idx = jnp.reshape(cc, (1,)).astype(jnp.int32)
    chip_idx = jnp.reshape(chip, (1,)).astype(jnp.int32)

    def halves(a):
        return a.reshape(2, a.shape[0] // 2, a.shape[1])

    shard2d = {k: (weights[k][0].T if k == "w_in" else weights[k][0]) for k in big}
    small_pack, small_offs = _pack([weights[k][0] for k in small_sharded], rows_multiple=32)
    gathered = _all_gather_weights([halves(shard2d[k].astype(BF16)) for k in big] + [halves(small_pack)])
    gw = {k: g.reshape(N_CHIPS, g.shape[1] * g.shape[2], g.shape[3]) for k, g in zip(big, gathered)}
    small_all = gathered[-1].reshape(N_CHIPS, -1, LANES)

    def cols(g):
        return jnp.transpose(g, (1, 0, 2)).reshape(g.shape[1], -1)

    def rows(g):
        return g.reshape(-1, g.shape[2])

    W_in = rows(gw["w_in"])
    W_main = jnp.concatenate([W_in[:OFF_A], W_in[OFF_P:]], axis=0)
    W_a = jnp.pad(W_in[OFF_A:OFF_P], ((0, RP - RANK), (0, 0)))
    W_branch, W_out, W_cq, W_co, W_down = (rows(gw[k]) for k in ("w_branch", "w_out", "w_cq", "w_co", "w_down"))
    W_ckv, W_up = cols(gw["w_ckv"]), cols(gw["w_up"])
    sm = [_unpack(small_all[j], small_offs, [weights[k].shape[1:] for k in small_sharded]) for j in range(N_CHIPS)]
    W_a2 = jnp.concatenate([sm[j][0] for j in range(N_CHIPS)], axis=1)
    W_a2p = jnp.pad(W_a2, ((0, RP - RANK), (0, 0))).astype(BF16)
    W_pool = jnp.concatenate([sm[j][1] for j in range(N_CHIPS)], axis=1).astype(BF16)
    W_conv = jnp.concatenate([sm[j][2] for j in range(N_CHIPS)], axis=1)

    h1, r1 = _rms_fwd(xs, g_mix, name="norm_mix")
    proj = _mm(h1, W_main, "nt", name="proj_main", out_dtype=F32)
    a_pad = _mm(h1, W_a, "nt", name="proj_gate_rank", out_dtype=F32)
    o_gla, o_raw, states = _gla_fwd(proj, a_pad, W_a2p, b_a, g_gla, T=T, DK=DK, DV=DV)
    o_pool = _pool_fwd(proj, W_pool, pool_scale, T=T, PW=PW, col_block=3)
    y_gla = _mm(o_gla, W_branch, "nn", name="branch_gla", out_dtype=F32, K=DV)
    y_pool = _mm(o_pool, W_branch, "nn", name="branch_pool", out_dtype=F32, K=PW, b_off=(DV, 0))
    merged = _merge_fwd(y_gla, y_pool, proj, T=T, D=D, col_block=2)
    x1 = _mm(merged, W_out, "nn", name="mix_out", out_dtype=F32, add=xs)

    h2, r2 = _rms_fwd(x1, g_cross, name="norm_cross")
    mem_n, rm = _rms_fwd(ms, g_mem, name="norm_mem")
    qc = _mm(h2, W_cq, "nn", name="cross_q", out_dtype=BF16)
    kv = _mm(mem_n, W_ckv, "nn", name="cross_kv", out_dtype=BF16)
    o_att = _attn_fwd(qc, kv, T=T, D=D, M=M)
    x2 = _mm(o_att, W_co, "nn", name="cross_out", out_dtype=F32, add=x1)

    h3, r3 = _rms_fwd(x2, g_ffn, name="norm_ffn")
    u0 = _mm(h3, W_up, "nn", name="ffn_up", out_dtype=F32)
    f_act = _conv_fwd(u0, W_conv, conv_b, T=T, F=F)
    x3 = _mm(f_act, W_down, "nn", name="ffn_down", out_dtype=F32, add=x2)

    loss_part, dx3, dx3_b, dg_final = _loss_head(x3, g_final.reshape(1, D), tgt)

    df = _mm(dx3_b, W_down, "nt", name="d_ffn_act", out_dtype=BF16)
    dW_down = _mm(f_act, dx3_b, "tn", name="dw_down", out_dtype=BF16)
    du0, dconv_w, dconv_b = _conv_bwd(u0, W_conv, conv_b, df, T=T, F=F)
    dh3 = _mm(du0, W_up, "nt", name="d_ffn_in", out_dtype=F32)
    dW_up = _mm(h3, du0, "tn", name="dw_up", out_dtype=BF16)
    dx2, dx2_b, dg_ffn = _rms_bwd(dh3, x2, r3, g_ffn, dx3, name="norm_ffn_bwd")

    do_att = _mm(dx2_b, W_co, "nt", name="d_cross_o", out_dtype=BF16)
    dW_co = _mm(o_att, dx2_b, "tn", name="dw_co", out_dtype=BF16)
    dq, dkv = _attn_bwd(qc, kv, do_att, T=T, D=D, M=M)
    dkv_b = dkv.astype(BF16)
    dW_cq = _mm(h2, dq, "tn", name="dw_cq", out_dtype=BF16)
    dh2 = _mm(dq, W_cq, "nt", name="d_cross_in", out_dtype=F32)
    dW_ckv = _mm(mem_n, dkv_b, "tn", name="dw_ckv", out_dtype=BF16)
    dmem_n = _mm(dkv_b, W_ckv, "nt", name="d_mem", out_dtype=F32)
    _, _, dg_mem = _rms_bwd(dmem_n, ms, rm, g_mem, None, name="norm_mem_bwd")
    dx1, dx1_b, dg_cross = _rms_bwd(dh2, x1, r2, g_cross, dx2, name="norm_cross_bwd")

    dmerged = _mm(dx1_b, W_out, "nt", name="d_merged", out_dtype=F32)
    dW_out = _mm(merged, dx1_b, "tn", name="dw_out", out_dtype=BF16)
    dy_gla, dy_pool, dgates = _merge_bwd(dmerged, y_gla, y_pool, proj, T=T, D=D, col_block=2)
    dW_br_gla = _mm(o_gla, dy_gla, "tn", name="dw_branch_gla", out_dtype=BF16)
    dW_br_pool = _mm(o_pool, dy_pool, "tn", name="dw_branch_pool", out_dtype=BF16)
    do_gla = _mm(dy_gla, W_branch, "nt", name="d_o_gla", out_dtype=F32, N=DV)
    do_pool = _mm(dy_pool, W_branch, "nt", name="d_o_pool", out_dtype=F32, N=PW, b_off=(DV, 0))
    dp, dw_pool, dpool_scale = _pool_bwd(proj, W_pool, pool_scale, do_pool, T=T, PW=PW, col_block=3)
    dqkvr, da_pad, dw2, db_a, dg_gla = _gla_bwd(proj, a_pad, W_a2p, b_a, g_gla, o_raw, states, do_gla,
                                               T=T, DK=DK, DV=DV)
    dproj = jnp.concatenate([dqkvr, dp, dgates], axis=1)
    dh1 = _mm(dproj, W_main, "nn", name="d_mix_in_main", out_dtype=F32)
    dh1 = _mm(da_pad, W_a, "nn", name="d_mix_in_rank", out_dtype=F32, add=dh1)
    dW_main = _mm(dproj, h1, "tn", name="dw_in_main", out_dtype=BF16)
    dW_a = _mm(da_pad, h1, "tn", name="dw_in_rank", out_dtype=BF16)
    dx0, _, dg_mix = _rms_bwd(dh1, xs, r1, g_mix, dx1, name="norm_mix_bwd")

    def col_shards(g):
        K, N = g.shape
        return jnp.transpose(g.reshape(K, N_CHIPS, N // N_CHIPS), (1, 0, 2)).reshape(N_CHIPS, 2, K // 2, N // N_CHIPS)

    def row_shards(g):
        R, N = g.shape
        return g.reshape(N_CHIPS, 2, R // N_CHIPS // 2, N)

    dW_in = jnp.concatenate([dW_main[:OFF_A], dW_a[:RANK], dW_main[OFF_A:]], axis=0)
    partial = [row_shards(dW_in), row_shards(jnp.concatenate([dW_br_gla, dW_br_pool], axis=0)),
               row_shards(dW_out), row_shards(dW_cq), col_shards(dW_ckv), row_shards(dW_co),
               col_shards(dW_up), row_shards(dW_down)]

    from_sibling = _exchange_halves(partial)
    chip_sums = [_add_halves(p, r, c_idx, name=f"grad_add_halves_{k}") for k, p, r in zip(big, partial, from_sibling)]
    from_chips = _scatter_to_chips(chip_sums)
    half_sums = [_add_chips(s, r, chip_idx, name=f"grad_add_chips_{k}") for k, s, r in zip(big, chip_sums, from_chips)]
    other_sums = _swap_halves(half_sums)
    grads = {}

    small_grads = [loss_part, dg_mix, db_a, dg_gla, dpool_scale, dg_cross, dg_mem, dg_ffn, dconv_b, dg_final,
                   dw2[:RANK], dw_pool, dconv_w]
    small_buf, offs = _pack(small_grads)
    red = _unpack(_all_reduce_small(small_buf), offs, [g.shape for g in small_grads])
    loss = red[0][0, 0]
    for k, g in zip(small_repl, red[1:10]):
        grads[k] = g.reshape(weights[k].shape)
    nb = DK // N_CHIPS
    grads["w_a2"] = lax.dynamic_slice_in_dim(red[10], chip * nb, nb, axis=1)[None]
    nb = GW // N_CHIPS
    grads["w_pool"] = lax.dynamic_slice_in_dim(red[11], chip * nb, nb, axis=1)[None]
    nb = F2 // N_CHIPS
    grads["conv_w"] = lax.dynamic_slice_in_dim(red[12], chip * nb, nb, axis=1)[None]

    delta, new_m, new_v = {}, {}, {}

    def whole(k, a):
        a = a.reshape(-1, a.shape[2])
        return (a.T if k == "w_in" else a)[None]

    for k, mine, other in zip(big, half_sums, other_sums):
        wmv = [halves(src[k][0].T if k == "w_in" else src[k][0]) for src in (weights, mom_m, mom_v)]
        res = _adamw_halves(*wmv, mine, other, c_idx, name=f"adamw_{k}")
        grads[k], delta[k], new_m[k], new_v[k] = (whole(k, a) for a in res)
    small = small_repl + small_sharded
    packs = [_pack([src[k] for k in small])[0] for src in (weights, grads, mom_m, mom_v)]
    _, offs = _pack([weights[k] for k in small])
    outs = _adamw(*packs, name="adamw_small")
    for res, o in zip((delta, new_m, new_v), outs):
        for k, a in zip(small, _unpack(o, offs, [weights[k].shape for k in small])):
            res[k] = a

    return (loss, dx0[None], *[grads[k] for k in order], *[delta[k] for k in order],
            *[new_m[k] for k in order], *[new_v[k] for k in order])
```

```python
import functools

import jax
import jax.numpy as jnp
from jax import lax
from jax.experimental import pallas as pl
from jax.experimental.pallas import tpu as pltpu

F32 = jnp.float32
BF16 = jnp.bfloat16
MESH = pl.DeviceIdType.MESH
HIGHEST = lax.Precision.HIGHEST

EPS = 1e-6
GLA_HEADS = 4
GLA_CHUNK = 64
GLA_GATE_NORM = 16.0
POOL_GROUPS = 4
CROSS_HEADS = 4
CONV_W = 3
N_CHIPS = 4
LANES = 128
SUBLANES = 8
VMEM_LIMIT = 56 << 20

ADAM_LR = 0.001
ADAM_B1 = 0.9
ADAM_B2 = 0.999
ADAM_EPS = 1e-08
ADAM_WD = 0.01
ADAM_STEP = 10

NN = (((1,), (0,)), ((), ()))
NT = (((1,), (1,)), ((), ()))
TN = (((0,), (0,)), ((), ()))


def _dot(a, b, dn=NN, precision=None):
    return lax.dot_general(a, b, dn, precision=precision, preferred_element_type=F32)


def _tile(n, pref, align=LANES):
    t = (min(pref, n) // align) * align
    while t >= align:
        if n % t == 0:
            return t
        t -= align
    return n


def _pcall(body, *, name, out_shape, grid=(), in_specs=None, out_specs=None, scratch_shapes=(),
           semantics=None, prefetch=0, aliases=None, split_copy=False):
    params = dict(vmem_limit_bytes=VMEM_LIMIT)
    if semantics is not None:
        params["dimension_semantics"] = semantics
    if split_copy:
        params["has_side_effects"] = pltpu.SideEffectType.DATAFLOW_SIDE_EFFECTING
    if prefetch:
        grid_spec = pltpu.PrefetchScalarGridSpec(
            num_scalar_prefetch=prefetch, grid=grid, in_specs=in_specs, out_specs=out_specs,
            scratch_shapes=scratch_shapes)
        return pl.pallas_call(body, name=name, out_shape=out_shape, grid_spec=grid_spec,
                              compiler_params=pltpu.CompilerParams(**params))
    kw = {}
    if aliases is not None:
        kw["input_output_aliases"] = aliases
    if in_specs is not None:
        kw["in_specs"] = in_specs
    if out_specs is not None:
        kw["out_specs"] = out_specs
    return pl.pallas_call(body, name=name, out_shape=out_shape, grid=grid,
                          scratch_shapes=scratch_shapes,
                          compiler_params=pltpu.CompilerParams(**params), **kw)


def _sigmoid(x):
    return 1.0 / (1.0 + jnp.exp(-x))


def _log_sigmoid(x):
    return jnp.minimum(x, 0.0) - jnp.log(1.0 + jnp.exp(-jnp.abs(x)))


def _mm(a, b, mode, *, name, out_dtype, M=None, N=None, K=None, a_off=(0, 0), b_off=(0, 0),
        add=None, tm=1024, tn=1024, tk=1024):
    if mode == "nn":
        M = M or a.shape[0]; K = K or a.shape[1]; N = N or b.shape[1]
    elif mode == "nt":
        M = M or a.shape[0]; K = K or a.shape[1]; N = N or b.shape[0]
    else:
        K = K or a.shape[0]; M = M or a.shape[1]; N = N or b.shape[1]
    tm_align = LANES if mode == "tn" else 16
    tm = _tile(M, tm, tm_align)
    tn = _tile(N, tn)
    tk = _tile(K, tk)
    nk = K // tk
    dn = {"nn": NN, "nt": NT, "tn": TN}[mode]

    def off(o, t):
        assert o % t == 0, (name, o, t)
        return o // t

    if mode == "tn":
        ar, ac = off(a_off[0], tk), off(a_off[1], tm)
        a_spec = pl.BlockSpec((tk, tm), lambda i, j, k: (k + ar, i + ac))
    else:
        ar, ac = off(a_off[0], tm), off(a_off[1], tk)
        a_spec = pl.BlockSpec((tm, tk), lambda i, j, k: (i + ar, k + ac))
    if mode == "nt":
        br, bc = off(b_off[0], tn), off(b_off[1], tk)
        b_spec = pl.BlockSpec((tn, tk), lambda i, j, k: (j + br, k + bc))
    else:
        br, bc = off(b_off[0], tk), off(b_off[1], tn)
        b_spec = pl.BlockSpec((tk, tn), lambda i, j, k: (k + br, j + bc))
    o_spec = pl.BlockSpec((tm, tn), lambda i, j, k: (i, j))
    in_specs = [a_spec, b_spec]
    args = [a, b]
    if add is not None:
        in_specs.append(o_spec)
        args.append(add)

    def body(*refs):
        if add is not None:
            a_ref, b_ref, add_ref, o_ref, acc_ref = refs
        else:
            a_ref, b_ref, o_ref, acc_ref = refs
        k = pl.program_id(2)

        @pl.when(k == 0)
        def _():
            acc_ref[...] = jnp.zeros_like(acc_ref)

        acc_ref[...] += _dot(a_ref[...].astype(BF16), b_ref[...].astype(BF16), dn)

        @pl.when(k == nk - 1)
        def _():
            r = acc_ref[...]
            if add is not None:
                r = r + add_ref[...]
            o_ref[...] = r.astype(o_ref.dtype)

    return _pcall(body, name=name, out_shape=jax.ShapeDtypeStruct((M, N), out_dtype),
                  grid=(M // tm, N // tn, nk), in_specs=in_specs, out_specs=o_spec,
                  scratch_shapes=[pltpu.VMEM((tm, tn), F32)],
                  semantics=("parallel", "parallel", "arbitrary"))(*args)


def _rms_fwd(x, g, *, name):
    T, D = x.shape
    tr = _tile(T, 128, 16)

    def body(x_ref, g_ref, h_ref, r_ref):
        xv = x_ref[...]
        r = lax.rsqrt(jnp.mean(xv * xv, axis=-1, keepdims=True) + EPS)
        h_ref[...] = (xv * r * g_ref[...]).astype(h_ref.dtype)
        r_ref[...] = r

    row = pl.BlockSpec((tr, D), lambda i: (i, 0))
    return _pcall(body, name=name,
                  out_shape=(jax.ShapeDtypeStruct((T, D), BF16), jax.ShapeDtypeStruct((T, 1), F32)),
                  grid=(T // tr,),
                  in_specs=[row, pl.BlockSpec((1, D), lambda i: (0, 0))],
                  out_specs=(row, pl.BlockSpec((tr, 1), lambda i: (i, 0))),
                  semantics=("parallel",))(x, g)


def _rms_bwd(dh, x, rstd, g, dres, *, name):
    T, D = x.shape
    tr = _tile(T, 128, 16)
    has_res = dres is not None

    def body(*refs):
        if has_res:
            dh_ref, x_ref, r_ref, g_ref, res_ref, dx_ref, dxb_ref, dg_ref = refs
        else:
            dh_ref, x_ref, r_ref, g_ref, dx_ref, dxb_ref, dg_ref = refs
        r = r_ref[...]
        xh = x_ref[...] * r
        dhv = dh_ref[...].astype(F32)
        dxh = dhv * g_ref[...]
        m = jnp.mean(dxh * xh, axis=-1, keepdims=True)
        dx = r * (dxh - xh * m)
        if has_res:
            dx = dx + res_ref[...]
        dx_ref[...] = dx
        dxb_ref[...] = dx.astype(BF16)

        @pl.when(pl.program_id(0) == 0)
        def _():
            dg_ref[...] = jnp.zeros_like(dg_ref)

        dg_ref[...] += jnp.sum(dhv * xh, axis=0, keepdims=True)

    row = pl.BlockSpec((tr, D), lambda i: (i, 0))
    vec = pl.BlockSpec((1, D), lambda i: (0, 0))
    in_specs = [row, row, pl.BlockSpec((tr, 1), lambda i: (i, 0)), vec]
    args = [dh, x, rstd, g]
    if has_res:
        in_specs.append(row)
        args.append(dres)
    return _pcall(body, name=name,
                  out_shape=(jax.ShapeDtypeStruct((T, D), F32), jax.ShapeDtypeStruct((T, D), BF16),
                             jax.ShapeDtypeStruct((1, D), F32)),
                  grid=(T // tr,), in_specs=in_specs, out_specs=(row, row, vec),
                  semantics=("arbitrary",))(*args)


def _loss_head(x3, g, tgt):
    T, D = x3.shape
    tr = _tile(T, 128, 16)

    def body(x_ref, g_ref, t_ref, loss_ref, dx_ref, dxb_ref, dg_ref):
        xv = x_ref[...]
        gv = g_ref[...]
        r = lax.rsqrt(jnp.mean(xv * xv, axis=-1, keepdims=True) + EPS)
        xh = xv * r
        err = xh * gv - t_ref[...]
        dy = err * (1.0 / D)
        dxh = dy * gv
        m = jnp.mean(dxh * xh, axis=-1, keepdims=True)
        dx = r * (dxh - xh * m)
        dx_ref[...] = dx
        dxb_ref[...] = dx.astype(BF16)

        @pl.when(pl.program_id(0) == 0)
        def _():
            dg_ref[...] = jnp.zeros_like(dg_ref)
            loss_ref[...] = jnp.zeros_like(loss_ref)

        dg_ref[...] += jnp.sum(dy * xh, axis=0, keepdims=True)
        part = 0.5 * jnp.sum(jnp.mean(err * err, axis=-1, keepdims=True), axis=0, keepdims=True)
        loss_ref[...] += jnp.broadcast_to(part, loss_ref.shape)

    row = pl.BlockSpec((tr, D), lambda i: (i, 0))
    vec = pl.BlockSpec((1, D), lambda i: (0, 0))
    return _pcall(body, name="loss_head",
                  out_shape=(jax.ShapeDtypeStruct((1, LANES), F32), jax.ShapeDtypeStruct((T, D), F32),
                             jax.ShapeDtypeStruct((T, D), BF16), jax.ShapeDtypeStruct((1, D), F32)),
                  grid=(T // tr,), in_specs=[row, vec, row],
                  out_specs=(pl.BlockSpec((1, LANES), lambda i: (0, 0)), row, row, vec),
                  semantics=("arbitrary",))(x3, g, tgt)


def _gla_chunk_terms(qk, a_ref, w2_ref, ba_ref, DK):
    C = qk.shape[0]
    gp = _dot(a_ref[...].astype(BF16), w2_ref[...]) + ba_ref[...]
    la = _log_sigmoid(gp) * (1.0 / GLA_GATE_NORM)
    row = lax.broadcasted_iota(jnp.int32, (C, C), 0)
    col = lax.broadcasted_iota(jnp.int32, (C, C), 1)
    causal = row >= col
    b = _dot(causal.astype(F32), la, precision=HIGHEST)
    return gp, b, causal


def _gla_fwd(proj, a_pad, w2, b_a, g_gla, *, T, DK, DV):
    assert 2 * DK == DV
    H = GLA_HEADS
    HK, HV = DK // H, DV // H
    C = GLA_CHUNK
    n = T // C
    RP = a_pad.shape[1]
    scale = HK ** -0.5

    def body(qk_ref, v_ref, r_ref, a_ref, w2_ref, ba_ref, gg_ref, og_ref, oraw_ref, st_ref, s_ref):
        @pl.when(pl.program_id(0) == 0)
        def _():
            s_ref[...] = jnp.zeros_like(s_ref)

        st_ref[...] = s_ref[...]
        qk = qk_ref[...]
        _, b, causal = _gla_chunk_terms(qk, a_ref, w2_ref, ba_ref, DK)
        for h in range(H):
            ks = slice(h * HK, (h + 1) * HK)
            vs = slice(h * HV, (h + 1) * HV)
            bh = b[:, ks]
            b_last = bh[C - 1:C, :]
            qt = qk[:, ks] * scale * jnp.exp(bh)
            kh = qk[:, DK + h * HK:DK + (h + 1) * HK]
            kt = kh * jnp.exp(-bh)
            khat = kh * jnp.exp(b_last - bh)
            a_mat = jnp.where(causal, _dot(qt, kt, NT, HIGHEST), 0.0)
            vh = v_ref[:, vs]
            s_t = s_ref[h]
            o = _dot(a_mat, vh, NN, HIGHEST) + _dot(qt, s_t, NT, HIGHEST)
            s_ref[h] = s_t * jnp.exp(b_last) + _dot(vh, khat, TN, HIGHEST)
            rs = lax.rsqrt(jnp.mean(o * o, axis=-1, keepdims=True) + EPS)
            rr = r_ref[:, vs]
            og = o * rs * gg_ref[:, vs] * (rr * _sigmoid(rr))
            oraw_ref[:, vs] = o
            og_ref[:, vs] = og.astype(BF16)

    blk = lambda j: pl.BlockSpec((C, DV), lambda i: (i, j))
    full = lambda s: pl.BlockSpec(s, lambda i: (0,) * len(s))
    return _pcall(
        body, name="gla_fwd",
        out_shape=(jax.ShapeDtypeStruct((T, DV), BF16), jax.ShapeDtypeStruct((T, DV), F32),
                   jax.ShapeDtypeStruct((n, H, HV, HK), F32)),
        grid=(n,),
        in_specs=[blk(0), blk(1), blk(2), pl.BlockSpec((C, RP), lambda i: (i, 0)),
                  full((RP, DK)), full((1, DK)), full((1, DV))],
        out_specs=(blk(0), blk(0), pl.BlockSpec((None, H, HV, HK), lambda i: (i, 0, 0, 0))),
        scratch_shapes=[pltpu.VMEM((H, HV, HK), F32)],
        semantics=("arbitrary",))(proj, proj, proj, a_pad, w2, b_a, g_gla)


def _gla_bwd(proj, a_pad, w2, b_a, g_gla, o_raw, states, do_gla, *, T, DK, DV):
    H = GLA_HEADS
    HK, HV = DK // H, DV // H
    C = GLA_CHUNK
    n = T // C
    RP = a_pad.shape[1]
    scale = HK ** -0.5

    def body(qk_ref, v_ref, r_ref, a_ref, w2_ref, ba_ref, gg_ref, oraw_ref, st_ref, dog_ref,
             dqkvr_ref, da_ref, dw2_ref, dba_ref, dgg_ref, ds_ref):
        @pl.when(pl.program_id(0) == 0)
        def _():
            ds_ref[...] = jnp.zeros_like(ds_ref)
            dw2_ref[...] = jnp.zeros_like(dw2_ref)
            dba_ref[...] = jnp.zeros_like(dba_ref)
            dgg_ref[...] = jnp.zeros_like(dgg_ref)

        qk = qk_ref[...]
        gp, b, causal = _gla_chunk_terms(qk, a_ref, w2_ref, ba_ref, DK)
        row = lax.broadcasted_iota(jnp.int32, (C, C), 0)
        col = lax.broadcasted_iota(jnp.int32, (C, C), 1)
        upper = (col >= row).astype(F32)
        dla_parts = []
        for h in range(H):
            ks = slice(h * HK, (h + 1) * HK)
            vs = slice(h * HV, (h + 1) * HV)
            bh = b[:, ks]
            b_last = bh[C - 1:C, :]
            eb = jnp.exp(bh)
            emb = jnp.exp(-bh)
            ehat = jnp.exp(b_last - bh)
            e_last = jnp.exp(b_last)
            qt = qk[:, ks] * scale * eb
            kh = qk[:, DK + h * HK:DK + (h + 1) * HK]
            kt = kh * emb
            khat = kh * ehat
            a_mat = jnp.where(causal, _dot(qt, kt, NT, HIGHEST), 0.0)
            vh = v_ref[:, vs]
            o = oraw_ref[:, vs]
            rs = lax.rsqrt(jnp.mean(o * o, axis=-1, keepdims=True) + EPS)
            on = o * rs
            gg = gg_ref[:, vs]
            rr = r_ref[:, vs]
            sg = _sigmoid(rr)
            d_out = dog_ref[:, vs]
            dr = d_out * (on * gg) * (sg * (1.0 + rr * (1.0 - sg)))
            d_og = d_out * (rr * sg)
            dgg_ref[:, vs] += jnp.sum(d_og * on, axis=0, keepdims=True)
            d_on = d_og * gg
            d_o = rs * (d_on - on * jnp.mean(d_on * on, axis=-1, keepdims=True))
            s_t = st_ref[h]
            ds_t = ds_ref[h]
            d_a = jnp.where(causal, _dot(d_o, vh, NT, HIGHEST), 0.0)
            dv = _dot(a_mat, d_o, TN, HIGHEST) + _dot(khat, ds_t, NT, HIGHEST)
            dqt = _dot(d_a, kt, NN, HIGHEST) + _dot(d_o, s_t, NN, HIGHEST)
            dkt = _dot(d_a, qt, TN, HIGHEST)
            dkhat = _dot(vh, ds_t, NN, HIGHEST)
            ds_ref[h] = ds_t * e_last + _dot(d_o, qt, TN, HIGHEST)
            dq = dqt * eb * scale
            dk = dkt * emb + dkhat * ehat
            db = dqt * qt - dkt * kt - dkhat * khat
            d_last = (jnp.sum(dkhat * khat, axis=0, keepdims=True)
                      + e_last * jnp.sum(ds_t * s_t, axis=0, keepdims=True))
            dla_parts.append(_dot(upper, db, NN, HIGHEST) + d_last)
            dqkvr_ref[:, ks] = dq.astype(BF16)
            dqkvr_ref[:, DK + h * HK:DK + (h + 1) * HK] = dk.astype(BF16)
            dqkvr_ref[:, DV + h * HV:DV + (h + 1) * HV] = dv.astype(BF16)
            dqkvr_ref[:, 2 * DV + h * HV:2 * DV + (h + 1) * HV] = dr.astype(BF16)
        dla = jnp.concatenate(dla_parts, axis=1)
        dgp = dla * (1.0 / GLA_GATE_NORM) * _sigmoid(-gp)
        dba_ref[...] += jnp.sum(dgp, axis=0, keepdims=True)
        dgp_b = dgp.astype(BF16)
        dw2_ref[...] += _dot(a_ref[...].astype(BF16), dgp_b, TN)
        da_ref[...] = _dot(dgp_b, w2_ref[...], NT).astype(BF16)

    rev = lambda j: pl.BlockSpec((C, DV), lambda i: (n - 1 - i, j))
    full = lambda s: pl.BlockSpec(s, lambda i: (0,) * len(s))
    return _pcall(
        body, name="gla_bwd",
        out_shape=(jax.ShapeDtypeStruct((T, 3 * DV), BF16), jax.ShapeDtypeStruct((T, RP), BF16),
                   jax.ShapeDtypeStruct((RP, DK), F32), jax.ShapeDtypeStruct((1, DK), F32),
                   jax.ShapeDtypeStruct((1, DV), F32)),
        grid=(n,),
        in_specs=[rev(0), rev(1), rev(2), pl.BlockSpec((C, RP), lambda i: (n - 1 - i, 0)),
                  full((RP, DK)), full((1, DK)), full((1, DV)), rev(0),
                  pl.BlockSpec((None, H, HV, HK), lambda i: (n - 1 - i, 0, 0, 0)), rev(0)],
        out_specs=(pl.BlockSpec((C, 3 * DV), lambda i: (n - 1 - i, 0)),
                   pl.BlockSpec((C, RP), lambda i: (n - 1 - i, 0)),
                   full((RP, DK)), full((1, DK)), full((1, DV))),
        scratch_shapes=[pltpu.VMEM((H, HV, HK), F32)],
        semantics=("arbitrary",))(proj, proj, proj, a_pad, w2, b_a, g_gla, o_raw, states, do_gla)


def _pool_windows(p, g, T):
    t = lax.broadcasted_iota(jnp.int32, (T, 1), 0)
    s = p
    for lvl in range(POOL_GROUPS):
        sh = 1 << lvl
        nxt = s + jnp.where(t >= sh, pltpu.roll(s, sh, 0), 0.0)
        s = jnp.where(lvl <= g, nxt, s)
    win = jnp.left_shift(2, g)
    inv = 1.0 / jnp.minimum(t + 1, win).astype(F32)
    return s * inv - p, inv


def _pool_fwd(proj, w_pool, scale, *, T, PW, col_block):
    GW = PW // POOL_GROUPS
    per = PW // GW

    def body(p_ref, w_ref, s_ref, o_ref):
        g = pl.program_id(0)
        pooled, _ = _pool_windows(p_ref[...], g, T)
        mixed = _dot(pooled.astype(BF16), w_ref[...])
        o_ref[...] = (mixed * s_ref[...]).astype(BF16)

    return _pcall(body, name="pool_fwd", out_shape=jax.ShapeDtypeStruct((T, PW), BF16),
                  grid=(POOL_GROUPS,),
                  in_specs=[pl.BlockSpec((T, GW), lambda g: (0, col_block * per + g)),
                            pl.BlockSpec((None, GW, GW), lambda g: (g, 0, 0)),
                            pl.BlockSpec((1, GW), lambda g: (0, g))],
                  out_specs=pl.BlockSpec((T, GW), lambda g: (0, g)),
                  semantics=("parallel",))(proj, w_pool, scale)


def _pool_bwd(proj, w_pool, scale, do_pool, *, T, PW, col_block):
    GW = PW // POOL_GROUPS
    per = PW // GW

    def body(p_ref, w_ref, s_ref, do_ref, dp_ref, dw_ref, dsc_ref):
        g = pl.program_id(0)
        pooled, inv = _pool_windows(p_ref[...], g, T)
        pooled_b = pooled.astype(BF16)
        w = w_ref[...]
        mixed = _dot(pooled_b, w)
        d_out = do_ref[...]
        dsc_ref[...] = jnp.sum(d_out * mixed, axis=0, keepdims=True)
        dmixed = (d_out * s_ref[...]).astype(BF16)
        dw_ref[...] = _dot(pooled_b, dmixed, TN)
        dpooled = _dot(dmixed, w, NT)
        t = lax.broadcasted_iota(jnp.int32, (T, 1), 0)
        s = dpooled * inv
        for lvl in range(POOL_GROUPS):
            sh = 1 << lvl
            nxt = s + jnp.where(t < T - sh, pltpu.roll(s, T - sh, 0), 0.0)
            s = jnp.where(lvl <= g, nxt, s)
        dp_ref[...] = (s - dpooled).astype(BF16)

    return _pcall(body, name="pool_bwd",
                  out_shape=(jax.ShapeDtypeStruct((T, PW), BF16),
                             jax.ShapeDtypeStruct((POOL_GROUPS, GW, GW), F32),
                             jax.ShapeDtypeStruct((1, PW), F32)),
                  grid=(POOL_GROUPS,),
                  in_specs=[pl.BlockSpec((T, GW), lambda g: (0, col_block * per + g)),
                            pl.BlockSpec((None, GW, GW), lambda g: (g, 0, 0)),
                            pl.BlockSpec((1, GW), lambda g: (0, g)),
                            pl.BlockSpec((T, GW), lambda g: (0, g))],
                  out_specs=(pl.BlockSpec((T, GW), lambda g: (0, g)),
                             pl.BlockSpec((None, GW, GW), lambda g: (g, 0, 0)),
                             pl.BlockSpec((1, GW), lambda g: (0, g))),
                  semantics=("parallel",))(proj, w_pool, scale, do_pool)


def _merge_fwd(y_gla, y_pool, proj, *, T, D, col_block):
    tr = _tile(T, 128, 16)

    def body(yg_ref, yp_ref, g1_ref, g2_ref, o_ref):
        o_ref[...] = (_sigmoid(g1_ref[...]) * yg_ref[...]
                      + _sigmoid(g2_ref[...]) * yp_ref[...]).astype(BF16)

    row = pl.BlockSpec((tr, D), lambda i: (i, 0))
    return _pcall(body, name="merge_fwd", out_shape=jax.ShapeDtypeStruct((T, D), BF16),
                  grid=(T // tr,),
                  in_specs=[row, row, pl.BlockSpec((tr, D), lambda i: (i, col_block)),
                            pl.BlockSpec((tr, D), lambda i: (i, col_block + 1))],
                  out_specs=row, semantics=("parallel",))(y_gla, y_pool, proj, proj)


def _merge_bwd(dmerged, y_gla, y_pool, proj, *, T, D, col_block):
    tr = _tile(T, 128, 16)

    def body(dm_ref, yg_ref, yp_ref, g1_ref, g2_ref, dyg_ref, dyp_ref, dg_ref):
        dm = dm_ref[...]
        s1 = _sigmoid(g1_ref[...])
        s2 = _sigmoid(g2_ref[...])
        dyg_ref[...] = (dm * s1).astype(BF16)
        dyp_ref[...] = (dm * s2).astype(BF16)
        dg_ref[:, :D] = (dm * yg_ref[...] * s1 * (1.0 - s1)).astype(BF16)
        dg_ref[:, D:] = (dm * yp_ref[...] * s2 * (1.0 - s2)).astype(BF16)

    row = pl.BlockSpec((tr, D), lambda i: (i, 0))
    return _pcall(body, name="merge_bwd",
                  out_shape=(jax.ShapeDtypeStruct((T, D), BF16), jax.ShapeDtypeStruct((T, D), BF16),
                             jax.ShapeDtypeStruct((T, 2 * D), BF16)),
                  grid=(T // tr,),
                  in_specs=[row, row, row, pl.BlockSpec((tr, D), lambda i: (i, col_block)),
                            pl.BlockSpec((tr, D), lambda i: (i, col_block + 1))],
                  out_specs=(row, row, pl.BlockSpec((tr, 2 * D), lambda i: (i, 0))),
                  semantics=("parallel",))(dmerged, y_gla, y_pool, proj, proj)


def _attn_fwd(q, kv, *, T, D, M):
    H = CROSS_HEADS
    HD = D // H
    tq = _tile(T, 512, 16)
    scale = HD ** -0.5

    def body(q_ref, kv_ref, o_ref):
        for h in range(H):
            hs = slice(h * HD, (h + 1) * HD)
            s = _dot(q_ref[:, hs], kv_ref[:, hs], NT) * scale
            e = jnp.exp(s - jnp.max(s, axis=-1, keepdims=True))
            p = e / jnp.sum(e, axis=-1, keepdims=True)
            o_ref[:, hs] = _dot(p.astype(BF16), kv_ref[:, D + h * HD:D + (h + 1) * HD]).astype(BF16)

    row = pl.BlockSpec((tq, D), lambda i: (i, 0))
    return _pcall(body, name="attn_fwd", out_shape=jax.ShapeDtypeStruct((T, D), BF16),
                  grid=(T // tq,), in_specs=[row, pl.BlockSpec((M, 2 * D), lambda i: (0, 0))],
                  out_specs=row, semantics=("parallel",))(q, kv)


def _attn_bwd(q, kv, do, *, T, D, M):
    H = CROSS_HEADS
    HD = D // H
    tq = _tile(T, 512, 16)
    scale = HD ** -0.5

    def body(q_ref, kv_ref, do_ref, dq_ref, dkv_ref):
        @pl.when(pl.program_id(0) == 0)
        def _():
            dkv_ref[...] = jnp.zeros_like(dkv_ref)

        for h in range(H):
            hs = slice(h * HD, (h + 1) * HD)
            vs = slice(D + h * HD, D + (h + 1) * HD)
            qh = q_ref[:, hs]
            kh = kv_ref[:, hs]
            s = _dot(qh, kh, NT) * scale
            e = jnp.exp(s - jnp.max(s, axis=-1, keepdims=True))
            p = e / jnp.sum(e, axis=-1, keepdims=True)
            p_b = p.astype(BF16)
            d_o = do_ref[:, hs]
            dkv_ref[:, vs] += _dot(p_b, d_o, TN)
            dp = _dot(d_o, kv_ref[:, vs], NT)
            ds = (p * (dp - jnp.sum(dp * p, axis=-1, keepdims=True)) * scale).astype(BF16)
            dq_ref[:, hs] = _dot(ds, kh).astype(BF16)
            dkv_ref[:, hs] += _dot(ds, qh, TN)

    row = pl.BlockSpec((tq, D), lambda i: (i, 0))
    full = pl.BlockSpec((M, 2 * D), lambda i: (0, 0))
    return _pcall(body, name="attn_bwd",
                  out_shape=(jax.ShapeDtypeStruct((T, D), BF16), jax.ShapeDtypeStruct((M, 2 * D), F32)),
                  grid=(T // tq,), in_specs=[row, full, row], out_specs=(row, full),
                  semantics=("arbitrary",))(q, kv, do)


def _shift_down(x, halo, s, t):
    out = pltpu.roll(x, s, 0)
    for j in range(s):
        out = jnp.where(t == j, halo[SUBLANES - s + j:SUBLANES - s + j + 1, :], out)
    return out


def _shift_up(x, halo, s, t, rows):
    out = pltpu.roll(x, rows - s, 0)
    for j in range(s):
        out = jnp.where(t == rows - s + j, halo[j:j + 1, :], out)
    return out


def _conv_tiles(T):
    tt = _tile(T, 128, SUBLANES)
    return tt, tt // SUBLANES, T // SUBLANES


def _conv_fwd(u0, conv_w, conv_b, *, T, F):
    tt, hb, _ = _conv_tiles(T)
    cw = _tile(F, 512)

    def body(u_ref, prev_ref, w_ref, b_ref, f_ref):
        i = pl.program_id(0)
        t = lax.broadcasted_iota(jnp.int32, (tt, 1), 0)

        def conv(cs):
            x = u_ref[:, cs]
            halo = jnp.where(i > 0, prev_ref[:, cs], 0.0)
            return (w_ref[2:3, cs] * x + w_ref[1:2, cs] * _shift_down(x, halo, 1, t)
                    + w_ref[0:1, cs] * _shift_down(x, halo, 2, t) + b_ref[:, cs])

        for j in range(F // cw):
            gate = conv(slice(j * cw, (j + 1) * cw))
            val = conv(slice(F + j * cw, F + (j + 1) * cw))
            f_ref[:, j * cw:(j + 1) * cw] = (gate * _sigmoid(gate) * val).astype(BF16)

    return _pcall(body, name="conv_fwd", out_shape=jax.ShapeDtypeStruct((T, F), BF16),
                  grid=(T // tt,),
                  in_specs=[pl.BlockSpec((tt, 2 * F), lambda i: (i, 0)),
                            pl.BlockSpec((SUBLANES, 2 * F), lambda i: (jnp.maximum(i * hb - 1, 0), 0)),
                            pl.BlockSpec((CONV_W, 2 * F), lambda i: (0, 0)),
                            pl.BlockSpec((1, 2 * F), lambda i: (0, 0))],
                  out_specs=pl.BlockSpec((tt, F), lambda i: (i, 0)),
                  semantics=("parallel",))(u0, u0, conv_w, conv_b)


def _conv_bwd(u0, conv_w, conv_b, df, *, T, F):
    tt, hb, nb = _conv_tiles(T)
    nt = T // tt
    cw = _tile(F, 512)

    def body(u_ref, prev_ref, next_ref, df_ref, dfn_ref, w_ref, b_ref, du0_ref, dw_ref, db_ref):
        i = pl.program_id(0)
        t = lax.broadcasted_iota(jnp.int32, (tt, 1), 0)
        t8 = lax.broadcasted_iota(jnp.int32, (SUBLANES, 1), 0)

        @pl.when(i == 0)
        def _():
            dw_ref[...] = jnp.zeros_like(dw_ref)
            db_ref[...] = jnp.zeros_like(db_ref)

        def conv(cs):
            x = u_ref[:, cs]
            halo = jnp.where(i > 0, prev_ref[:, cs], 0.0)
            x1 = _shift_down(x, halo, 1, t)
            x2 = _shift_down(x, halo, 2, t)
            u = w_ref[2:3, cs] * x + w_ref[1:2, cs] * x1 + w_ref[0:1, cs] * x2 + b_ref[:, cs]
            xn = next_ref[:, cs]
            tail = x[tt - SUBLANES:, :]
            un = (w_ref[2:3, cs] * xn + w_ref[1:2, cs] * _shift_down(xn, tail, 1, t8)
                  + w_ref[0:1, cs] * _shift_down(xn, tail, 2, t8) + b_ref[:, cs])
            return u, un, (x, x1, x2)

        def glu_grad(gate, val, dff):
            sg = _sigmoid(gate)
            return dff * val * (sg * (1.0 + gate * (1.0 - sg))), dff * (gate * sg)

        def finish(cs, du, dun, xs):
            du0 = (w_ref[2:3, cs] * du + w_ref[1:2, cs] * _shift_up(du, dun, 1, t, tt)
                   + w_ref[0:1, cs] * _shift_up(du, dun, 2, t, tt))
            du0_ref[:, cs] = du0.astype(BF16)
            db_ref[:, cs] += jnp.sum(du, axis=0, keepdims=True)
            dw_ref[2:3, cs] += jnp.sum(du * xs[0], axis=0, keepdims=True)
            dw_ref[1:2, cs] += jnp.sum(du * xs[1], axis=0, keepdims=True)
            dw_ref[0:1, cs] += jnp.sum(du * xs[2], axis=0, keepdims=True)

        for j in range(F // cw):
            fs = slice(j * cw, (j + 1) * cw)
            gs, vs = fs, slice(F + j * cw, F + (j + 1) * cw)
            ug, ung, xg = conv(gs)
            uv, unv, xv = conv(vs)
            dug, duv = glu_grad(ug, uv, df_ref[:, fs].astype(F32))
            dung, dunv = glu_grad(ung, unv, dfn_ref[0:SUBLANES, fs].astype(F32))
            dung = jnp.where(i < nt - 1, dung, 0.0)
            dunv = jnp.where(i < nt - 1, dunv, 0.0)
            finish(gs, dug, dung, xg)
            finish(vs, duv, dunv, xv)

    wide = lambda rows, fn: pl.BlockSpec((rows, 2 * F), fn)
    nxt = lambda i: (jnp.minimum((i + 1) * hb, nb - 1), 0)
    return _pcall(body, name="conv_bwd",
                  out_shape=(jax.ShapeDtypeStruct((T, 2 * F), BF16),
                             jax.ShapeDtypeStruct((CONV_W, 2 * F), F32),
                             jax.ShapeDtypeStruct((1, 2 * F), F32)),
                  grid=(nt,),
                  in_specs=[wide(tt, lambda i: (i, 0)),
                            wide(SUBLANES, lambda i: (jnp.maximum(i * hb - 1, 0), 0)),
                            wide(SUBLANES, nxt),
                            pl.BlockSpec((tt, F), lambda i: (i, 0)),
                            pl.BlockSpec((2 * SUBLANES, F),
                                         lambda i: (jnp.minimum((i + 1) * (hb // 2), nb // 2 - 1), 0)),
                            wide(CONV_W, lambda i: (0, 0)), wide(1, lambda i: (0, 0))],
                  out_specs=(wide(tt, lambda i: (i, 0)), wide(CONV_W, lambda i: (0, 0)),
                             wide(1, lambda i: (0, 0))),
                  semantics=("arbitrary",))(u0, u0, u0, df, df, conv_w, conv_b)


def _adamw(w, g, m, v, *, name):
    R, C = w.shape
    tr = _tile(R, max(SUBLANES, (1 << 19) // max(C, 1) // SUBLANES * SUBLANES), SUBLANES)
    c1 = 1.0 / (1.0 - ADAM_B1 ** ADAM_STEP)
    c2 = 1.0 / (1.0 - ADAM_B2 ** ADAM_STEP)

    def body(w_ref, g_ref, m_ref, v_ref, d_ref, mo_ref, vo_ref):
        gv = g_ref[...]
        mn = ADAM_B1 * m_ref[...] + (1.0 - ADAM_B1) * gv
        vn = ADAM_B2 * v_ref[...] + (1.0 - ADAM_B2) * (gv * gv)
        d_ref[...] = -ADAM_LR * ((mn * c1) / (jnp.sqrt(vn * c2) + ADAM_EPS) + ADAM_WD * w_ref[...])
        mo_ref[...] = mn
        vo_ref[...] = vn

    blk = pl.BlockSpec((tr, C), lambda i: (i, 0))
    shp = jax.ShapeDtypeStruct((R, C), F32)
    return _pcall(body, name=name, out_shape=(shp, shp, shp), grid=(R // tr,),
                  in_specs=[blk] * 4, out_specs=(blk,) * 3, semantics=("parallel",))(w, g, m, v)


def _blk(h, C, elems=1 << 19, align=16):
    th = _tile(h, max(align, elems // C // align * align), align)
    if th < h or h * C <= 2 * elems:
        return th, C
    return h, _tile(C, max(LANES, elems // h // LANES * LANES))


def _adamw_halves(w, m, v, g_mine, g_other, c_idx, *, name):
    _, h, C = w.shape
    th, tc = _blk(h, C, align=SUBLANES)
    c1 = 1.0 / (1.0 - ADAM_B1 ** ADAM_STEP)
    c2 = 1.0 / (1.0 - ADAM_B2 ** ADAM_STEP)

    def body(c_ref, w_ref, m_ref, v_ref, gm_ref, go_ref, g_ref, d_ref, mo_ref, vo_ref):
        gv = jnp.where(pl.program_id(0) == c_ref[0], gm_ref[...], go_ref[...])
        mn = ADAM_B1 * m_ref[...] + (1.0 - ADAM_B1) * gv
        vn = ADAM_B2 * v_ref[...] + (1.0 - ADAM_B2) * (gv * gv)
        d_ref[...] = -ADAM_LR * ((mn * c1) / (jnp.sqrt(vn * c2) + ADAM_EPS) + ADAM_WD * w_ref[...])
        g_ref[...] = gv
        mo_ref[...] = mn
        vo_ref[...] = vn

    blk = pl.BlockSpec((None, th, tc), lambda s, i, j, c: (s, i, j))

    def pick(mine):
        def index(s, i, j, c):
            use = (s == c[0]) if mine else (s != c[0])
            return jnp.where(use, i, 0), jnp.where(use, j, 0)
        return pl.BlockSpec((th, tc), index)

    shp = jax.ShapeDtypeStruct((2, h, C), F32)
    return _pcall(body, name=name, out_shape=(shp,) * 4, grid=(2, h // th, C // tc), prefetch=1,
                  in_specs=[blk, blk, blk, pick(True), pick(False)], out_specs=(blk,) * 4,
                  semantics=("parallel", "parallel", "parallel"))(c_idx, w, m, v, g_mine, g_other)


def _mesh_pos():
    x, y, c = lax.axis_index("x"), lax.axis_index("y"), lax.axis_index("c")
    others = [(1 - x, y), (x, 1 - y), (1 - x, 1 - y)]
    return x, y, c, others


def _all_gather_weights(shards):
    n = len(shards)
    ANY = pl.BlockSpec(memory_space=pl.ANY)
    PER = 7

    def body(*refs):
        ins, outs = refs[:n], refs[n:2 * n]
        send_sems, recv_sems = refs[2 * n:]
        x, y, c, others = _mesh_pos()
        me = 2 * x + y
        sibling = (x, y, 1 - c)

        def copy(a, k, src, dst, to):
            return pltpu.make_async_remote_copy(
                src_ref=src, dst_ref=dst, send_sem=send_sems.at[PER * a + k],
                recv_sem=recv_sems.at[PER * a + k], device_id=to, device_id_type=MESH)

        sent = []
        for a in range(n):
            for j, chip in enumerate(others):
                cp = copy(a, j, ins[a].at[c], outs[a].at[me, c], (*chip, c))
                cp.start()
                sent.append(cp)
        own = [copy(a, 6, ins[a], outs[a].at[me], sibling) for a in range(n)]
        for cp in own:
            cp.start()
        for a in range(n):
            for j, chip in enumerate(others):
                theirs = outs[a].at[2 * chip[0] + chip[1], c]
                copy(a, j, theirs, theirs, (*chip, c)).wait_recv()
                cp = copy(a, 3 + j, theirs, theirs, sibling)
                cp.start()
                sent.append(cp)
        for a in range(n):
            for j, chip in enumerate(others):
                theirs = outs[a].at[2 * chip[0] + chip[1], 1 - c]
                copy(a, 3 + j, theirs, theirs, sibling).wait_recv()
        for cp in own:
            cp.wait()
        for cp in sent:
            cp.wait_send()

    return _pcall(body, name="all_gather_weights",
                  out_shape=[jax.ShapeDtypeStruct((N_CHIPS, *s.shape), s.dtype) for s in shards],
                  in_specs=[ANY] * n, out_specs=[ANY] * n,
                  scratch_shapes=[pltpu.SemaphoreType.DMA((PER * n,)),
                                  pltpu.SemaphoreType.DMA((PER * n,))])(*shards)


def _exchange_halves(grads, *, name):
    n = len(grads)
    ANY = pl.BlockSpec(memory_space=pl.ANY)

    def body(*refs):
        ins, outs = refs[:n], refs[n:2 * n]
        send_sems, recv_sems = refs[2 * n:]
        x, y, c, _ = _mesh_pos()
        copies = [pltpu.make_async_remote_copy(
            src_ref=ins[a].at[:, 1 - c], dst_ref=outs[a], send_sem=send_sems.at[a],
            recv_sem=recv_sems.at[a], device_id=(x, y, 1 - c), device_id_type=MESH) for a in range(n)]
        for cp in copies:
            cp.start()
        for cp in copies:
            cp.wait()

    return _pcall(body, name=name,
                  out_shape=[jax.ShapeDtypeStruct((g.shape[0], *g.shape[2:]), g.dtype) for g in grads],
                  in_specs=[ANY] * n, out_specs=[ANY] * n,
                  scratch_shapes=[pltpu.SemaphoreType.DMA((n,)), pltpu.SemaphoreType.DMA((n,))])(*grads)


def _add_halves(grad, recv, c_idx, *, name):
    S, _, h, C = grad.shape
    th, tc = _blk(h, C)

    def body(c_ref, g_ref, r_ref, o_ref):
        o_ref[...] = (g_ref[...].astype(F32) + r_ref[...].astype(F32)).astype(o_ref.dtype)

    return _pcall(body, name=name, out_shape=jax.ShapeDtypeStruct((S, h, C), grad.dtype),
                  grid=(S, h // th, C // tc), prefetch=1,
                  in_specs=[pl.BlockSpec((None, None, th, tc), lambda s, i, j, c: (s, c[0], i, j)),
                            pl.BlockSpec((None, th, tc), lambda s, i, j, c: (s, i, j))],
                  out_specs=pl.BlockSpec((None, th, tc), lambda s, i, j, c: (s, i, j)),
                  semantics=("parallel", "parallel", "parallel"))(c_idx, grad, recv)


def _scatter_copies(srcs, lands, send_sems, recv_sems):
    x, y, c, others = _mesh_pos()
    return [pltpu.make_async_remote_copy(
        src_ref=srcs[a].at[2 * chip[0] + chip[1]], dst_ref=lands[a].at[j],
        send_sem=send_sems.at[3 * a + j], recv_sem=recv_sems.at[3 * a + j],
        device_id=(*chip, c), device_id_type=MESH)
        for a in range(len(srcs)) for j, chip in enumerate(others)]


def _scatter_start(sums, *, name):
    n = len(sums)
    lands = [lax.empty((3, *s.shape[1:]), s.dtype) for s in sums]
    HBM = pl.BlockSpec(memory_space=pltpu.HBM)
    SEM = pl.BlockSpec(memory_space=pltpu.SEMAPHORE)

    def body(*refs):
        srcs, zones = refs[:n], refs[n:2 * n]
        send_sems, recv_sems = refs[2 * n], refs[2 * n + 1]
        token = refs[-1]
        for cp in _scatter_copies(srcs, zones, send_sems, recv_sems):
            cp.start()
        token[...] = jnp.zeros_like(token)

    hbm = lambda a: pltpu.HBM(a.shape, a.dtype)
    res = _pcall(body, name=name,
                 out_shape=(pltpu.SemaphoreType.DMA((3 * n,)), pltpu.SemaphoreType.DMA((3 * n,)),
                            *[hbm(a) for a in sums], *[hbm(a) for a in lands],
                            jax.ShapeDtypeStruct((SUBLANES, LANES), F32)),
                 in_specs=[HBM] * (2 * n),
                 out_specs=(SEM, SEM, *[HBM] * (2 * n), pl.BlockSpec(memory_space=pltpu.VMEM)),
                 aliases={i: 2 + i for i in range(2 * n)}, split_copy=True)(
        *[pltpu.with_memory_space_constraint(a, pltpu.HBM) for a in sums + lands])
    return res[0], res[1], list(res[2:2 + n]), list(res[2 + n:2 + 2 * n]), res[-1]


def _scatter_wait(send_sems, recv_sems, sums, lands, after, *, name):
    n = len(sums)
    HBM = pl.BlockSpec(memory_space=pltpu.HBM)
    SEM = pl.BlockSpec(memory_space=pltpu.SEMAPHORE)

    def body(*refs):
        srcs, zones = refs[:n], refs[n:2 * n]
        s_sems, r_sems = refs[2 * n], refs[2 * n + 1]
        for cp in _scatter_copies(srcs, zones, s_sems, r_sems):
            cp.wait_send()
            cp.wait_recv()

    hbm = lambda a: pltpu.HBM(a.shape, a.dtype)
    res = _pcall(body, name=name, out_shape=(*[hbm(a) for a in sums], *[hbm(a) for a in lands]),
                 in_specs=[*[HBM] * (2 * n), SEM, SEM, pl.BlockSpec(memory_space=pl.ANY)],
                 out_specs=tuple([HBM] * (2 * n)), aliases={i: i for i in range(2 * n)},
                 split_copy=True)(*sums, *lands, send_sems, recv_sems, after)
    return list(res[:n]), list(res[n:])


def _add_chips(sums, recv, chip_idx, *, name):
    _, h, C = sums.shape
    th, tc = _blk(h, C)

    def body(k_ref, s_ref, r_ref, o_ref):
        acc = s_ref[...].astype(F32) + r_ref[0].astype(F32)
        acc = acc + r_ref[1].astype(F32)
        o_ref[...] = acc + r_ref[2].astype(F32)

    return _pcall(body, name=name, out_shape=jax.ShapeDtypeStruct((h, C), F32),
                  grid=(h // th, C // tc), prefetch=1,
                  in_specs=[pl.BlockSpec((None, th, tc), lambda i, j, k: (k[0], i, j)),
                            pl.BlockSpec((3, th, tc), lambda i, j, k: (0, i, j))],
                  out_specs=pl.BlockSpec((th, tc), lambda i, j, k: (i, j)),
                  semantics=("parallel", "parallel"))(chip_idx, sums, recv)


def _swap_halves(halves, *, name):
    n = len(halves)
    ANY = pl.BlockSpec(memory_space=pl.ANY)

    def body(*refs):
        ins, outs = refs[:n], refs[n:2 * n]
        send_sems, recv_sems = refs[2 * n:]
        x, y, c, _ = _mesh_pos()
        copies = [pltpu.make_async_remote_copy(
            src_ref=ins[a], dst_ref=outs[a], send_sem=send_sems.at[a], recv_sem=recv_sems.at[a],
            device_id=(x, y, 1 - c), device_id_type=MESH) for a in range(n)]
        for cp in copies:
            cp.start()
        for cp in copies:
            cp.wait()

    return _pcall(body, name=name,
                  out_shape=[jax.ShapeDtypeStruct(s.shape, s.dtype) for s in halves],
                  in_specs=[ANY] * n, out_specs=[ANY] * n,
                  scratch_shapes=[pltpu.SemaphoreType.DMA((n,)), pltpu.SemaphoreType.DMA((n,))])(*halves)


def _all_reduce_small(buf):
    R, L = buf.shape
    NDEV = 8

    def body(x_ref, sum_ref, all_ref, send_sems, recv_sems, local_sem):
        x, y, c, others = _mesh_pos()
        me, sibling = (x, y, c), (x, y, 1 - c)

        def slot(px, py, pc):
            return all_ref.at[4 * px + 2 * py + pc]

        def copy(k, block, to, src=None):
            return pltpu.make_async_remote_copy(
                src_ref=slot(*block) if src is None else src, dst_ref=slot(*block),
                send_sem=send_sems.at[k], recv_sem=recv_sems.at[k], device_id=to, device_id_type=MESH)

        mine = pltpu.make_async_copy(x_ref, slot(*me), local_sem)
        mine.start()
        first = [copy(0, me, sibling, src=x_ref)]
        first += [copy(1 + j, me, (*chip, c), src=x_ref) for j, chip in enumerate(others)]
        for cp in first:
            cp.start()
        passed = [copy(4 + j, (*chip, c), sibling) for j, chip in enumerate(others)]
        for j, chip in enumerate(others):
            copy(1 + j, (*chip, c), me).wait_recv()
            passed[j].start()
        copy(0, sibling, me).wait_recv()
        for j, chip in enumerate(others):
            copy(4 + j, (*chip, 1 - c), me).wait_recv()
        for cp in first + passed:
            cp.wait_send()
        mine.wait()
        acc = all_ref[0]
        for d in range(1, NDEV):
            acc = acc + all_ref[d]
        sum_ref[...] = acc

    VM = pl.BlockSpec(memory_space=pltpu.VMEM)
    return _pcall(body, name="all_reduce_small",
                  out_shape=(jax.ShapeDtypeStruct((R, L), F32), jax.ShapeDtypeStruct((NDEV, R, L), F32)),
                  in_specs=[VM], out_specs=(VM, VM),
                  scratch_shapes=[pltpu.SemaphoreType.DMA((7,)), pltpu.SemaphoreType.DMA((7,)),
                                  pltpu.SemaphoreType.DMA])(buf)[0]


def _pack(arrs, rows_multiple=16):
    flat = [a.reshape(-1).astype(F32) for a in arrs]
    sizes = [f.shape[0] for f in flat]
    total = sum(sizes)
    per = LANES * rows_multiple
    padded = -(-total // per) * per
    flat.append(jnp.zeros((padded - total,), F32))
    offs = [0]
    for s in sizes:
        offs.append(offs[-1] + s)
    return jnp.concatenate(flat).reshape(padded // LANES, LANES), offs


def _unpack(buf, offs, shapes):
    flat = buf.reshape(-1)
    return [flat[offs[i]:offs[i + 1]].reshape(s) for i, s in enumerate(shapes)]


def kernel(x, mem, g_mix, w_in, w_a2, b_a, g_gla, w_pool, pool_scale, w_branch, w_out, g_cross, g_mem, w_cq, w_ckv, w_co, g_ffn, w_up, conv_w, conv_b, w_down, g_final, loss_target, m_g_mix, m_w_in, m_w_a2, m_b_a, m_g_gla, m_w_pool, m_pool_scale, m_w_branch, m_w_out, m_g_cross, m_g_mem, m_w_cq, m_w_ckv, m_w_co, m_g_ffn, m_w_up, m_conv_w, m_conv_b, m_w_down, m_g_final, v_g_mix, v_w_in, v_w_a2, v_b_a, v_g_gla, v_w_pool, v_pool_scale, v_w_branch, v_w_out, v_g_cross, v_g_mem, v_w_cq, v_w_ckv, v_w_co, v_g_ffn, v_w_up, v_conv_w, v_conv_b, v_w_down, v_g_final):
    weights = dict(g_mix=g_mix, w_in=w_in, w_a2=w_a2, b_a=b_a, g_gla=g_gla, w_pool=w_pool,
                   pool_scale=pool_scale, w_branch=w_branch, w_out=w_out, g_cross=g_cross, g_mem=g_mem,
                   w_cq=w_cq, w_ckv=w_ckv, w_co=w_co, g_ffn=g_ffn, w_up=w_up, conv_w=conv_w,
                   conv_b=conv_b, w_down=w_down, g_final=g_final)
    mom_m = dict(g_mix=m_g_mix, w_in=m_w_in, w_a2=m_w_a2, b_a=m_b_a, g_gla=m_g_gla, w_pool=m_w_pool,
                 pool_scale=m_pool_scale, w_branch=m_w_branch, w_out=m_w_out, g_cross=m_g_cross,
                 g_mem=m_g_mem, w_cq=m_w_cq, w_ckv=m_w_ckv, w_co=m_w_co, g_ffn=m_g_ffn, w_up=m_w_up,
                 conv_w=m_conv_w, conv_b=m_conv_b, w_down=m_w_down, g_final=m_g_final)
    mom_v = dict(g_mix=v_g_mix, w_in=v_w_in, w_a2=v_w_a2, b_a=v_b_a, g_gla=v_g_gla, w_pool=v_w_pool,
                 pool_scale=v_pool_scale, w_branch=v_w_branch, w_out=v_w_out, g_cross=v_g_cross,
                 g_mem=v_g_mem, w_cq=v_w_cq, w_ckv=v_w_ckv, w_co=v_w_co, g_ffn=v_g_ffn, w_up=v_w_up,
                 conv_w=v_conv_w, conv_b=v_conv_b, w_down=v_w_down, g_final=v_g_final)
    order = list(weights)
    big = ["w_in", "w_branch", "w_out", "w_cq", "w_ckv", "w_co", "w_up", "w_down"]
    small_sharded = ["w_a2", "w_pool", "conv_w"]
    small_repl = ["g_mix", "b_a", "g_gla", "pool_scale", "g_cross", "g_mem", "g_ffn", "conv_b", "g_final"]

    xs, ms, tgt = x[0], mem[0], loss_target[0]
    T, D = xs.shape
    M = ms.shape[0]
    DK, DV, PW = b_a.shape[1], g_gla.shape[1], pool_scale.shape[1]
    RANK = w_a2.shape[1]
    F2 = conv_b.shape[1]
    F = F2 // 2
    DIN = N_CHIPS * w_in.shape[2]
    OFF_A = 2 * DK + 2 * DV
    OFF_P = OFF_A + RANK
    RP = LANES
    GW = PW // POOL_GROUPS
    assert PW == DV and 4 * DV == 2 * D and OFF_P + PW + 2 * D == DIN

    cx, cy, cc = lax.axis_index("x"), lax.axis_index("y"), lax.axis_index("c")
    chip = 2 * cx + cy
    c_idx = jnp.reshape(cc, (1,)).astype(jnp.int32)
    chip_idx = jnp.reshape(chip, (1,)).astype(jnp.int32)

    def halves(a):
        return a.reshape(2, a.shape[0] // 2, a.shape[1])

    shard2d = {k: (weights[k][0].T if k == "w_in" else weights[k][0]) for k in big}
    small_pack, small_offs = _pack([weights[k][0] for k in small_sharded], rows_multiple=32)
    gathered = _all_gather_weights([halves(shard2d[k].astype(BF16)) for k in big] + [halves(small_pack)])
    gw = {k: g.reshape(N_CHIPS, g.shape[1] * g.shape[2], g.shape[3]) for k, g in zip(big, gathered)}
    small_all = gathered[-1].reshape(N_CHIPS, -1, LANES)

    def cols(g):
        return jnp.transpose(g, (1, 0, 2)).reshape(g.shape[1], -1)

    def rows(g):
        return g.reshape(-1, g.shape[2])

    W_in = rows(gw["w_in"])
    W_main = jnp.concatenate([W_in[:OFF_A], W_in[OFF_P:]], axis=0)
    W_a = jnp.pad(W_in[OFF_A:OFF_P], ((0, RP - RANK), (0, 0)))
    W_branch, W_out, W_cq, W_co, W_down = (rows(gw[k]) for k in ("w_branch", "w_out", "w_cq", "w_co", "w_down"))
    W_ckv, W_up = cols(gw["w_ckv"]), cols(gw["w_up"])
    sm = [_unpack(small_all[j], small_offs, [weights[k].shape[1:] for k in small_sharded]) for j in range(N_CHIPS)]
    W_a2 = jnp.concatenate([sm[j][0] for j in range(N_CHIPS)], axis=1)
    W_a2p = jnp.pad(W_a2, ((0, RP - RANK), (0, 0))).astype(BF16)
    W_pool = jnp.concatenate([sm[j][1] for j in range(N_CHIPS)], axis=1).astype(BF16)
    W_conv = jnp.concatenate([sm[j][2] for j in range(N_CHIPS)], axis=1)

    h1, r1 = _rms_fwd(xs, g_mix, name="norm_mix")
    proj = _mm(h1, W_main, "nt", name="proj_main", out_dtype=F32)
    a_pad = _mm(h1, W_a, "nt", name="proj_gate_rank", out_dtype=F32)
    o_gla, o_raw, states = _gla_fwd(proj, a_pad, W_a2p, b_a, g_gla, T=T, DK=DK, DV=DV)
    o_pool = _pool_fwd(proj, W_pool, pool_scale, T=T, PW=PW, col_block=3)
    y_gla = _mm(o_gla, W_branch, "nn", name="branch_gla", out_dtype=F32, K=DV)
    y_pool = _mm(o_pool, W_branch, "nn", name="branch_pool", out_dtype=F32, K=PW, b_off=(DV, 0))
    merged = _merge_fwd(y_gla, y_pool, proj, T=T, D=D, col_block=2)
    x1 = _mm(merged, W_out, "nn", name="mix_out", out_dtype=F32, add=xs)

    h2, r2 = _rms_fwd(x1, g_cross, name="norm_cross")
    mem_n, rm = _rms_fwd(ms, g_mem, name="norm_mem")
    qc = _mm(h2, W_cq, "nn", name="cross_q", out_dtype=BF16)
    kv = _mm(mem_n, W_ckv, "nn", name="cross_kv", out_dtype=BF16)
    o_att = _attn_fwd(qc, kv, T=T, D=D, M=M)
    x2 = _mm(o_att, W_co, "nn", name="cross_out", out_dtype=F32, add=x1)

    h3, r3 = _rms_fwd(x2, g_ffn, name="norm_ffn")
    u0 = _mm(h3, W_up, "nn", name="ffn_up", out_dtype=F32)
    f_act = _conv_fwd(u0, W_conv, conv_b, T=T, F=F)
    x3 = _mm(f_act, W_down, "nn", name="ffn_down", out_dtype=F32, add=x2)

    loss_part, dx3, dx3_b, dg_final = _loss_head(x3, g_final.reshape(1, D), tgt)

    def col_shards(g):
        K, N = g.shape
        return jnp.transpose(g.reshape(K, N_CHIPS, N // N_CHIPS), (1, 0, 2)).reshape(N_CHIPS, 2, K // 2, N // N_CHIPS)

    def row_shards(g):
        R, N = g.shape
        return g.reshape(N_CHIPS, 2, R // N_CHIPS // 2, N)

    in_flight = []

    def reduce_start(group, keys, partials):
        from_sibling = _exchange_halves(partials, name=f"grad_exchange_halves_{group}")
        chip_sums = [_add_halves(p, r, c_idx, name=f"grad_add_halves_{k}")
                     for k, p, r in zip(keys, partials, from_sibling)]
        s_sems, r_sems, sums, lands, token = _scatter_start(chip_sums, name=f"grad_scatter_start_{group}")
        in_flight.append((group, keys, s_sems, r_sems, sums, lands))
        return token[0:1, 0:1]

    df = _mm(dx3_b, W_down, "nt", name="d_ffn_act", out_dtype=BF16)
    dW_down = _mm(f_act, dx3_b, "tn", name="dw_down", out_dtype=BF16)
    du0, dconv_w, dconv_b = _conv_bwd(u0, W_conv, conv_b, df, T=T, F=F)
    dh3 = _mm(du0, W_up, "nt", name="d_ffn_in", out_dtype=F32)
    dW_up = _mm(h3, du0, "tn", name="dw_up", out_dtype=BF16)
    tok = reduce_start("ffn", ["w_down", "w_up"], [row_shards(dW_down), col_shards(dW_up)])
    dx2, dx2_b, dg_ffn = _rms_bwd(dh3, x2, r3 + tok, g_ffn, dx3, name="norm_ffn_bwd")

    do_att = _mm(dx2_b, W_co, "nt", name="d_cross_o", out_dtype=BF16)
    dW_co = _mm(o_att, dx2_b, "tn", name="dw_co", out_dtype=BF16)
    dq, dkv = _attn_bwd(qc, kv, do_att, T=T, D=D, M=M)
    dkv_b = dkv.astype(BF16)
    dW_cq = _mm(h2, dq, "tn", name="dw_cq", out_dtype=BF16)
    dh2 = _mm(dq, W_cq, "nt", name="d_cross_in", out_dtype=F32)
    dW_ckv = _mm(mem_n, dkv_b, "tn", name="dw_ckv", out_dtype=BF16)
    dmem_n = _mm(dkv_b, W_ckv, "nt", name="d_mem", out_dtype=F32)
    tok = reduce_start("cross", ["w_co", "w_cq", "w_ckv"],
                       [row_shards(dW_co), row_shards(dW_cq), col_shards(dW_ckv)])
    _, _, dg_mem = _rms_bwd(dmem_n, ms, rm, g_mem, None, name="norm_mem_bwd")
    dx1, dx1_b, dg_cross = _rms_bwd(dh2, x1, r2 + tok, g_cross, dx2, name="norm_cross_bwd")

    dmerged = _mm(dx1_b, W_out, "nt", name="d_merged", out_dtype=F32)
    dW_out = _mm(merged, dx1_b, "tn", name="dw_out", out_dtype=BF16)
    dy_gla, dy_pool, dgates = _merge_bwd(dmerged, y_gla, y_pool, proj, T=T, D=D, col_block=2)
    dW_br_gla = _mm(o_gla, dy_gla, "tn", name="dw_branch_gla", out_dtype=BF16)
    dW_br_pool = _mm(o_pool, dy_pool, "tn", name="dw_branch_pool", out_dtype=BF16)
    do_gla = _mm(dy_gla, W_branch, "nt", name="d_o_gla", out_dtype=F32, N=DV)
    do_pool = _mm(dy_pool, W_branch, "nt", name="d_o_pool", out_dtype=F32, N=PW, b_off=(DV, 0))
    tok = reduce_start("mix", ["w_out", "w_branch"],
                       [row_shards(dW_out), row_shards(jnp.concatenate([dW_br_gla, dW_br_pool], axis=0))])
    dp, dw_pool, dpool_scale = _pool_bwd(proj, W_pool, pool_scale + tok, do_pool, T=T, PW=PW, col_block=3)
    dqkvr, da_pad, dw2, db_a, dg_gla = _gla_bwd(proj, a_pad, W_a2p, b_a + tok, g_gla, o_raw, states, do_gla,
                                               T=T, DK=DK, DV=DV)
    dproj = jnp.concatenate([dqkvr, dp, dgates], axis=1)
    dh1 = _mm(dproj, W_main, "nn", name="d_mix_in_main", out_dtype=F32)
    dh1 = _mm(da_pad, W_a, "nn", name="d_mix_in_rank", out_dtype=F32, add=dh1)
    dW_main = _mm(dproj, h1, "tn", name="dw_in_main", out_dtype=BF16)
    dW_a = _mm(da_pad, h1, "tn", name="dw_in_rank", out_dtype=BF16)
    dx0, _, dg_mix = _rms_bwd(dh1, xs, r1, g_mix, dx1, name="norm_mix_bwd")

    dW_in = jnp.concatenate([dW_main[:OFF_A], dW_a[:RANK], dW_main[OFF_A:]], axis=0)
    tok = reduce_start("in", ["w_in"], [row_shards(dW_in)])
    grads = {}

    small_grads = [loss_part + tok, dg_mix, db_a, dg_gla, dpool_scale, dg_cross, dg_mem, dg_ffn, dconv_b, dg_final,
                   dw2[:RANK], dw_pool, dconv_w]
    small_buf, offs = _pack(small_grads)
    small_sum = _all_reduce_small(small_buf)
    red = _unpack(small_sum, offs, [g.shape for g in small_grads])
    loss = red[0][0, 0]
    for k, g in zip(small_repl, red[1:10]):
        grads[k] = g.reshape(weights[k].shape)
    nb = DK // N_CHIPS
    grads["w_a2"] = lax.dynamic_slice_in_dim(red[10], chip * nb, nb, axis=1)[None]
    nb = GW // N_CHIPS
    grads["w_pool"] = lax.dynamic_slice_in_dim(red[11], chip * nb, nb, axis=1)[None]
    nb = F2 // N_CHIPS
    grads["conv_w"] = lax.dynamic_slice_in_dim(red[12], chip * nb, nb, axis=1)[None]

    delta, new_m, new_v = {}, {}, {}

    def whole(k, a):
        a = a.reshape(-1, a.shape[2])
        return (a.T if k == "w_in" else a)[None]

    after = small_sum
    for group, keys, s_sems, r_sems, sums, lands in in_flight:
        sums, from_chips = _scatter_wait(s_sems, r_sems, sums, lands, after, name=f"grad_scatter_wait_{group}")
        half_sums = [_add_chips(s, r, chip_idx, name=f"grad_add_chips_{k}") for k, s, r in zip(keys, sums, from_chips)]
        other_sums = _swap_halves(half_sums, name=f"grad_swap_halves_{group}")
        for k, mine, other in zip(keys, half_sums, other_sums):
            wmv = [halves(src[k][0].T if k == "w_in" else src[k][0]) for src in (weights, mom_m, mom_v)]
            res = _adamw_halves(*wmv, mine, other, c_idx, name=f"adamw_{k}")
            grads[k], delta[k], new_m[k], new_v[k] = (whole(k, a) for a in res)
            after = res[1]
    small = small_repl + small_sharded
    packs = [_pack([src[k] for k in small])[0] for src in (weights, grads, mom_m, mom_v)]
    _, offs = _pack([weights[k] for k in small])
    outs = _adamw(*packs, name="adamw_small")
    for res, o in zip((delta, new_m, new_v), outs):
        for k, a in zip(small, _unpack(o, offs, [weights[k].shape for k in small])):
            res[k] = a

    return (loss, dx0[None], *[grads[k] for k in order], *[delta[k] for k in order],
            *[new_m[k] for k in order], *[new_v[k] for k in order])
```

```python
import functools

import jax
import jax.numpy as jnp
from jax import lax
from jax.experimental import pallas as pl
from jax.experimental.pallas import tpu as pltpu

F32 = jnp.float32
BF16 = jnp.bfloat16
MESH = pl.DeviceIdType.MESH
HIGHEST = lax.Precision.HIGHEST

EPS = 1e-6
GLA_HEADS = 4
GLA_CHUNK = 64
GLA_GATE_NORM = 16.0
POOL_GROUPS = 4
CROSS_HEADS = 4
CONV_W = 3
N_CHIPS = 4
LANES = 128
SUBLANES = 8
VMEM_LIMIT = 56 << 20

ADAM_LR = 0.001
ADAM_B1 = 0.9
ADAM_B2 = 0.999
ADAM_EPS = 1e-08
ADAM_WD = 0.01
ADAM_STEP = 10

NN = (((1,), (0,)), ((), ()))
NT = (((1,), (1,)), ((), ()))
TN = (((0,), (0,)), ((), ()))


def _dot(a, b, dn=NN, precision=None):
    return lax.dot_general(a, b, dn, precision=precision, preferred_element_type=F32)


def _tile(n, pref, align=LANES):
    t = (min(pref, n) // align) * align
    while t >= align:
        if n % t == 0:
            return t
        t -= align
    return n


def _pcall(body, *, name, out_shape, grid=(), in_specs=None, out_specs=None, scratch_shapes=(),
           semantics=None, prefetch=0, aliases=None, split_copy=False):
    params = dict(vmem_limit_bytes=VMEM_LIMIT)
    if semantics is not None:
        params["dimension_semantics"] = semantics
    if split_copy:
        params["has_side_effects"] = pltpu.SideEffectType.DATAFLOW_SIDE_EFFECTING
    if prefetch:
        grid_spec = pltpu.PrefetchScalarGridSpec(
            num_scalar_prefetch=prefetch, grid=grid, in_specs=in_specs, out_specs=out_specs,
            scratch_shapes=scratch_shapes)
        return pl.pallas_call(body, name=name, out_shape=out_shape, grid_spec=grid_spec,
                              compiler_params=pltpu.CompilerParams(**params))
    kw = {}
    if aliases is not None:
        kw["input_output_aliases"] = aliases
    if in_specs is not None:
        kw["in_specs"] = in_specs
    if out_specs is not None:
        kw["out_specs"] = out_specs
    return pl.pallas_call(body, name=name, out_shape=out_shape, grid=grid,
                          scratch_shapes=scratch_shapes,
                          compiler_params=pltpu.CompilerParams(**params), **kw)


def _sigmoid(x):
    return 1.0 / (1.0 + jnp.exp(-x))


def _log_sigmoid(x):
    return jnp.minimum(x, 0.0) - jnp.log(1.0 + jnp.exp(-jnp.abs(x)))


def _mm(a, b, mode, *, name, out_dtype, M=None, N=None, K=None, a_off=(0, 0), b_off=(0, 0),
        add=None, tm=1024, tn=1024, tk=1024):
    if mode == "nn":
        M = M or a.shape[0]; K = K or a.shape[1]; N = N or b.shape[1]
    elif mode == "nt":
        M = M or a.shape[0]; K = K or a.shape[1]; N = N or b.shape[0]
    else:
        K = K or a.shape[0]; M = M or a.shape[1]; N = N or b.shape[1]
    tm_align = LANES if mode == "tn" else 16
    tm = _tile(M, tm, tm_align)
    tn = _tile(N, tn)
    tk = _tile(K, tk)
    nk = K // tk
    dn = {"nn": NN, "nt": NT, "tn": TN}[mode]

    def off(o, t):
        assert o % t == 0, (name, o, t)
        return o // t

    if mode == "tn":
        ar, ac = off(a_off[0], tk), off(a_off[1], tm)
        a_spec = pl.BlockSpec((tk, tm), lambda i, j, k: (k + ar, i + ac))
    else:
        ar, ac = off(a_off[0], tm), off(a_off[1], tk)
        a_spec = pl.BlockSpec((tm, tk), lambda i, j, k: (i + ar, k + ac))
    if mode == "nt":
        br, bc = off(b_off[0], tn), off(b_off[1], tk)
        b_spec = pl.BlockSpec((tn, tk), lambda i, j, k: (j + br, k + bc))
    else:
        br, bc = off(b_off[0], tk), off(b_off[1], tn)
        b_spec = pl.BlockSpec((tk, tn), lambda i, j, k: (k + br, j + bc))
    o_spec = pl.BlockSpec((tm, tn), lambda i, j, k: (i, j))
    in_specs = [a_spec, b_spec]
    args = [a, b]
    if add is not None:
        in_specs.append(o_spec)
        args.append(add)

    def body(*refs):
        if add is not None:
            a_ref, b_ref, add_ref, o_ref, acc_ref = refs
        else:
            a_ref, b_ref, o_ref, acc_ref = refs
        k = pl.program_id(2)

        @pl.when(k == 0)
        def _():
            acc_ref[...] = jnp.zeros_like(acc_ref)

        acc_ref[...] += _dot(a_ref[...].astype(BF16), b_ref[...].astype(BF16), dn)

        @pl.when(k == nk - 1)
        def _():
            r = acc_ref[...]
            if add is not None:
                r = r + add_ref[...]
            o_ref[...] = r.astype(o_ref.dtype)

    return _pcall(body, name=name, out_shape=jax.ShapeDtypeStruct((M, N), out_dtype),
                  grid=(M // tm, N // tn, nk), in_specs=in_specs, out_specs=o_spec,
                  scratch_shapes=[pltpu.VMEM((tm, tn), F32)],
                  semantics=("parallel", "parallel", "arbitrary"))(*args)


def _rms_fwd(x, g, *, name):
    T, D = x.shape
    tr = _tile(T, 128, 16)

    def body(x_ref, g_ref, h_ref, r_ref):
        xv = x_ref[...]
        r = lax.rsqrt(jnp.mean(xv * xv, axis=-1, keepdims=True) + EPS)
        h_ref[...] = (xv * r * g_ref[...]).astype(h_ref.dtype)
        r_ref[...] = r

    row = pl.BlockSpec((tr, D), lambda i: (i, 0))
    return _pcall(body, name=name,
                  out_shape=(jax.ShapeDtypeStruct((T, D), BF16), jax.ShapeDtypeStruct((T, 1), F32)),
                  grid=(T // tr,),
                  in_specs=[row, pl.BlockSpec((1, D), lambda i: (0, 0))],
                  out_specs=(row, pl.BlockSpec((tr, 1), lambda i: (i, 0))),
                  semantics=("parallel",))(x, g)


def _rms_bwd(dh, x, rstd, g, dres, *, name):
    T, D = x.shape
    tr = _tile(T, 128, 16)
    has_res = dres is not None

    def body(*refs):
        if has_res:
            dh_ref, x_ref, r_ref, g_ref, res_ref, dx_ref, dxb_ref, dg_ref = refs
        else:
            dh_ref, x_ref, r_ref, g_ref, dx_ref, dxb_ref, dg_ref = refs
        r = r_ref[...]
        xh = x_ref[...] * r
        dhv = dh_ref[...].astype(F32)
        dxh = dhv * g_ref[...]
        m = jnp.mean(dxh * xh, axis=-1, keepdims=True)
        dx = r * (dxh - xh * m)
        if has_res:
            dx = dx + res_ref[...]
        dx_ref[...] = dx
        dxb_ref[...] = dx.astype(BF16)

        @pl.when(pl.program_id(0) == 0)
        def _():
            dg_ref[...] = jnp.zeros_like(dg_ref)

        dg_ref[...] += jnp.sum(dhv * xh, axis=0, keepdims=True)

    row = pl.BlockSpec((tr, D), lambda i: (i, 0))
    vec = pl.BlockSpec((1, D), lambda i: (0, 0))
    in_specs = [row, row, pl.BlockSpec((tr, 1), lambda i: (i, 0)), vec]
    args = [dh, x, rstd, g]
    if has_res:
        in_specs.append(row)
        args.append(dres)
    return _pcall(body, name=name,
                  out_shape=(jax.ShapeDtypeStruct((T, D), F32), jax.ShapeDtypeStruct((T, D), BF16),
                             jax.ShapeDtypeStruct((1, D), F32)),
                  grid=(T // tr,), in_specs=in_specs, out_specs=(row, row, vec),
                  semantics=("arbitrary",))(*args)


def _loss_head(x3, g, tgt):
    T, D = x3.shape
    tr = _tile(T, 128, 16)

    def body(x_ref, g_ref, t_ref, loss_ref, dx_ref, dxb_ref, dg_ref):
        xv = x_ref[...]
        gv = g_ref[...]
        r = lax.rsqrt(jnp.mean(xv * xv, axis=-1, keepdims=True) + EPS)
        xh = xv * r
        err = xh * gv - t_ref[...]
        dy = err * (1.0 / D)
        dxh = dy * gv
        m = jnp.mean(dxh * xh, axis=-1, keepdims=True)
        dx = r * (dxh - xh * m)
        dx_ref[...] = dx
        dxb_ref[...] = dx.astype(BF16)

        @pl.when(pl.program_id(0) == 0)
        def _():
            dg_ref[...] = jnp.zeros_like(dg_ref)
            loss_ref[...] = jnp.zeros_like(loss_ref)

        dg_ref[...] += jnp.sum(dy * xh, axis=0, keepdims=True)
        part = 0.5 * jnp.sum(jnp.mean(err * err, axis=-1, keepdims=True), axis=0, keepdims=True)
        loss_ref[...] += jnp.broadcast_to(part, loss_ref.shape)

    row = pl.BlockSpec((tr, D), lambda i: (i, 0))
    vec = pl.BlockSpec((1, D), lambda i: (0, 0))
    return _pcall(body, name="loss_head",
                  out_shape=(jax.ShapeDtypeStruct((1, LANES), F32), jax.ShapeDtypeStruct((T, D), F32),
                             jax.ShapeDtypeStruct((T, D), BF16), jax.ShapeDtypeStruct((1, D), F32)),
                  grid=(T // tr,), in_specs=[row, vec, row],
                  out_specs=(pl.BlockSpec((1, LANES), lambda i: (0, 0)), row, row, vec),
                  semantics=("arbitrary",))(x3, g, tgt)


def _gla_chunk_terms(qk, a_ref, w2_ref, ba_ref, DK):
    C = qk.shape[0]
    gp = _dot(a_ref[...].astype(BF16), w2_ref[...]) + ba_ref[...]
    la = _log_sigmoid(gp) * (1.0 / GLA_GATE_NORM)
    row = lax.broadcasted_iota(jnp.int32, (C, C), 0)
    col = lax.broadcasted_iota(jnp.int32, (C, C), 1)
    causal = row >= col
    b = _dot(causal.astype(F32), la, precision=HIGHEST)
    return gp, b, causal


def _gla_fwd(proj, a_pad, w2, b_a, g_gla, *, T, DK, DV):
    assert 2 * DK == DV
    H = GLA_HEADS
    HK, HV = DK // H, DV // H
    C = GLA_CHUNK
    n = T // C
    RP = a_pad.shape[1]
    scale = HK ** -0.5

    def body(qk_ref, v_ref, r_ref, a_ref, w2_ref, ba_ref, gg_ref, og_ref, oraw_ref, st_ref, s_ref):
        @pl.when(pl.program_id(0) == 0)
        def _():
            s_ref[...] = jnp.zeros_like(s_ref)

        st_ref[...] = s_ref[...]
        qk = qk_ref[...]
        _, b, causal = _gla_chunk_terms(qk, a_ref, w2_ref, ba_ref, DK)
        for h in range(H):
            ks = slice(h * HK, (h + 1) * HK)
            vs = slice(h * HV, (h + 1) * HV)
            bh = b[:, ks]
            b_last = bh[C - 1:C, :]
            qt = qk[:, ks] * scale * jnp.exp(bh)
            kh = qk[:, DK + h * HK:DK + (h + 1) * HK]
            kt = kh * jnp.exp(-bh)
            khat = kh * jnp.exp(b_last - bh)
            a_mat = jnp.where(causal, _dot(qt, kt, NT, HIGHEST), 0.0)
            vh = v_ref[:, vs]
            s_t = s_ref[h]
            o = _dot(a_mat, vh, NN, HIGHEST) + _dot(qt, s_t, NT, HIGHEST)
            s_ref[h] = s_t * jnp.exp(b_last) + _dot(vh, khat, TN, HIGHEST)
            rs = lax.rsqrt(jnp.mean(o * o, axis=-1, keepdims=True) + EPS)
            rr = r_ref[:, vs]
            og = o * rs * gg_ref[:, vs] * (rr * _sigmoid(rr))
            oraw_ref[:, vs] = o
            og_ref[:, vs] = og.astype(BF16)

    blk = lambda j: pl.BlockSpec((C, DV), lambda i: (i, j))
    full = lambda s: pl.BlockSpec(s, lambda i: (0,) * len(s))
    return _pcall(
        body, name="gla_fwd",
        out_shape=(jax.ShapeDtypeStruct((T, DV), BF16), jax.ShapeDtypeStruct((T, DV), F32),
                   jax.ShapeDtypeStruct((n, H, HV, HK), F32)),
        grid=(n,),
        in_specs=[blk(0), blk(1), blk(2), pl.BlockSpec((C, RP), lambda i: (i, 0)),
                  full((RP, DK)), full((1, DK)), full((1, DV))],
        out_specs=(blk(0), blk(0), pl.BlockSpec((None, H, HV, HK), lambda i: (i, 0, 0, 0))),
        scratch_shapes=[pltpu.VMEM((H, HV, HK), F32)],
        semantics=("arbitrary",))(proj, proj, proj, a_pad, w2, b_a, g_gla)


def _gla_bwd(proj, a_pad, w2, b_a, g_gla, o_raw, states, do_gla, *, T, DK, DV):
    H = GLA_HEADS
    HK, HV = DK // H, DV // H
    C = GLA_CHUNK
    n = T // C
    RP = a_pad.shape[1]
    scale = HK ** -0.5

    def body(qk_ref, v_ref, r_ref, a_ref, w2_ref, ba_ref, gg_ref, oraw_ref, st_ref, dog_ref,
             dqkvr_ref, da_ref, dw2_ref, dba_ref, dgg_ref, ds_ref):
        @pl.when(pl.program_id(0) == 0)
        def _():
            ds_ref[...] = jnp.zeros_like(ds_ref)
            dw2_ref[...] = jnp.zeros_like(dw2_ref)
            dba_ref[...] = jnp.zeros_like(dba_ref)
            dgg_ref[...] = jnp.zeros_like(dgg_ref)

        qk = qk_ref[...]
        gp, b, causal = _gla_chunk_terms(qk, a_ref, w2_ref, ba_ref, DK)
        row = lax.broadcasted_iota(jnp.int32, (C, C), 0)
        col = lax.broadcasted_iota(jnp.int32, (C, C), 1)
        upper = (col >= row).astype(F32)
        dla_parts = []
        for h in range(H):
            ks = slice(h * HK, (h + 1) * HK)
            vs = slice(h * HV, (h + 1) * HV)
            bh = b[:, ks]
            b_last = bh[C - 1:C, :]
            eb = jnp.exp(bh)
            emb = jnp.exp(-bh)
            ehat = jnp.exp(b_last - bh)
            e_last = jnp.exp(b_last)
            qt = qk[:, ks] * scale * eb
            kh = qk[:, DK + h * HK:DK + (h + 1) * HK]
            kt = kh * emb
            khat = kh * ehat
            a_mat = jnp.where(causal, _dot(qt, kt, NT, HIGHEST), 0.0)
            vh = v_ref[:, vs]
            o = oraw_ref[:, vs]
            rs = lax.rsqrt(jnp.mean(o * o, axis=-1, keepdims=True) + EPS)
            on = o * rs
            gg = gg_ref[:, vs]
            rr = r_ref[:, vs]
            sg = _sigmoid(rr)
            d_out = dog_ref[:, vs]
            dr = d_out * (on * gg) * (sg * (1.0 + rr * (1.0 - sg)))
            d_og = d_out * (rr * sg)
            dgg_ref[:, vs] += jnp.sum(d_og * on, axis=0, keepdims=True)
            d_on = d_og * gg
            d_o = rs * (d_on - on * jnp.mean(d_on * on, axis=-1, keepdims=True))
            s_t = st_ref[h]
            ds_t = ds_ref[h]
            d_a = jnp.where(causal, _dot(d_o, vh, NT, HIGHEST), 0.0)
            dv = _dot(a_mat, d_o, TN, HIGHEST) + _dot(khat, ds_t, NT, HIGHEST)
            dqt = _dot(d_a, kt, NN, HIGHEST) + _dot(d_o, s_t, NN, HIGHEST)
            dkt = _dot(d_a, qt, TN, HIGHEST)
            dkhat = _dot(vh, ds_t, NN, HIGHEST)
            ds_ref[h] = ds_t * e_last + _dot(d_o, qt, TN, HIGHEST)
            dq = dqt * eb * scale
            dk = dkt * emb + dkhat * ehat
            db = dqt * qt - dkt * kt - dkhat * khat
            d_last = (jnp.sum(dkhat * khat, axis=0, keepdims=True)
                      + e_last * jnp.sum(ds_t * s_t, axis=0, keepdims=True))
            dla_parts.append(_dot(upper, db, NN, HIGHEST) + d_last)
            dqkvr_ref[:, ks] = dq.astype(BF16)
            dqkvr_ref[:, DK + h * HK:DK + (h + 1) * HK] = dk.astype(BF16)
            dqkvr_ref[:, DV + h * HV:DV + (h + 1) * HV] = dv.astype(BF16)
            dqkvr_ref[:, 2 * DV + h * HV:2 * DV + (h + 1) * HV] = dr.astype(BF16)
        dla = jnp.concatenate(dla_parts, axis=1)
        dgp = dla * (1.0 / GLA_GATE_NORM) * _sigmoid(-gp)
        dba_ref[...] += jnp.sum(dgp, axis=0, keepdims=True)
        dgp_b = dgp.astype(BF16)
        dw2_ref[...] += _dot(a_ref[...].astype(BF16), dgp_b, TN)
        da_ref[...] = _dot(dgp_b, w2_ref[...], NT).astype(BF16)

    rev = lambda j: pl.BlockSpec((C, DV), lambda i: (n - 1 - i, j))
    full = lambda s: pl.BlockSpec(s, lambda i: (0,) * len(s))
    return _pcall(
        body, name="gla_bwd",
        out_shape=(jax.ShapeDtypeStruct((T, 3 * DV), BF16), jax.ShapeDtypeStruct((T, RP), BF16),
                   jax.ShapeDtypeStruct((RP, DK), F32), jax.ShapeDtypeStruct((1, DK), F32),
                   jax.ShapeDtypeStruct((1, DV), F32)),
        grid=(n,),
        in_specs=[rev(0), rev(1), rev(2), pl.BlockSpec((C, RP), lambda i: (n - 1 - i, 0)),
                  full((RP, DK)), full((1, DK)), full((1, DV)), rev(0),
                  pl.BlockSpec((None, H, HV, HK), lambda i: (n - 1 - i, 0, 0, 0)), rev(0)],
        out_specs=(pl.BlockSpec((C, 3 * DV), lambda i: (n - 1 - i, 0)),
                   pl.BlockSpec((C, RP), lambda i: (n - 1 - i, 0)),
                   full((RP, DK)), full((1, DK)), full((1, DV))),
        scratch_shapes=[pltpu.VMEM((H, HV, HK), F32)],
        semantics=("arbitrary",))(proj, proj, proj, a_pad, w2, b_a, g_gla, o_raw, states, do_gla)


def _pool_windows(p, g, T):
    t = lax.broadcasted_iota(jnp.int32, (T, 1), 0)
    s = p
    for lvl in range(POOL_GROUPS):
        sh = 1 << lvl
        nxt = s + jnp.where(t >= sh, pltpu.roll(s, sh, 0), 0.0)
        s = jnp.where(lvl <= g, nxt, s)
    win = jnp.left_shift(2, g)
    inv = 1.0 / jnp.minimum(t + 1, win).astype(F32)
    return s * inv - p, inv


def _pool_fwd(proj, w_pool, scale, *, T, PW, col_block):
    GW = PW // POOL_GROUPS
    per = PW // GW

    def body(p_ref, w_ref, s_ref, o_ref):
        g = pl.program_id(0)
        pooled, _ = _pool_windows(p_ref[...], g, T)
        mixed = _dot(pooled.astype(BF16), w_ref[...])
        o_ref[...] = (mixed * s_ref[...]).astype(BF16)

    return _pcall(body, name="pool_fwd", out_shape=jax.ShapeDtypeStruct((T, PW), BF16),
                  grid=(POOL_GROUPS,),
                  in_specs=[pl.BlockSpec((T, GW), lambda g: (0, col_block * per + g)),
                            pl.BlockSpec((None, GW, GW), lambda g: (g, 0, 0)),
                            pl.BlockSpec((1, GW), lambda g: (0, g))],
                  out_specs=pl.BlockSpec((T, GW), lambda g: (0, g)),
                  semantics=("parallel",))(proj, w_pool, scale)


def _pool_bwd(proj, w_pool, scale, do_pool, *, T, PW, col_block):
    GW = PW // POOL_GROUPS
    per = PW // GW

    def body(p_ref, w_ref, s_ref, do_ref, dp_ref, dw_ref, dsc_ref):
        g = pl.program_id(0)
        pooled, inv = _pool_windows(p_ref[...], g, T)
        pooled_b = pooled.astype(BF16)
        w = w_ref[...]
        mixed = _dot(pooled_b, w)
        d_out = do_ref[...]
        dsc_ref[...] = jnp.sum(d_out * mixed, axis=0, keepdims=True)
        dmixed = (d_out * s_ref[...]).astype(BF16)
        dw_ref[...] = _dot(pooled_b, dmixed, TN)
        dpooled = _dot(dmixed, w, NT)
        t = lax.broadcasted_iota(jnp.int32, (T, 1), 0)
        s = dpooled * inv
        for lvl in range(POOL_GROUPS):
            sh = 1 << lvl
            nxt = s + jnp.where(t < T - sh, pltpu.roll(s, T - sh, 0), 0.0)
            s = jnp.where(lvl <= g, nxt, s)
        dp_ref[...] = (s - dpooled).astype(BF16)

    return _pcall(body, name="pool_bwd",
                  out_shape=(jax.ShapeDtypeStruct((T, PW), BF16),
                             jax.ShapeDtypeStruct((POOL_GROUPS, GW, GW), F32),
                             jax.ShapeDtypeStruct((1, PW), F32)),
                  grid=(POOL_GROUPS,),
                  in_specs=[pl.BlockSpec((T, GW), lambda g: (0, col_block * per + g)),
                            pl.BlockSpec((None, GW, GW), lambda g: (g, 0, 0)),
                            pl.BlockSpec((1, GW), lambda g: (0, g)),
                            pl.BlockSpec((T, GW), lambda g: (0, g))],
                  out_specs=(pl.BlockSpec((T, GW), lambda g: (0, g)),
                             pl.BlockSpec((None, GW, GW), lambda g: (g, 0, 0)),
                             pl.BlockSpec((1, GW), lambda g: (0, g))),
                  semantics=("parallel",))(proj, w_pool, scale, do_pool)


def _merge_fwd(y_gla, y_pool, proj, *, T, D, col_block):
    tr = _tile(T, 128, 16)

    def body(yg_ref, yp_ref, g1_ref, g2_ref, o_ref):
        o_ref[...] = (_sigmoid(g1_ref[...]) * yg_ref[...]
                      + _sigmoid(g2_ref[...]) * yp_ref[...]).astype(BF16)

    row = pl.BlockSpec((tr, D), lambda i: (i, 0))
    return _pcall(body, name="merge_fwd", out_shape=jax.ShapeDtypeStruct((T, D), BF16),
                  grid=(T // tr,),
                  in_specs=[row, row, pl.BlockSpec((tr, D), lambda i: (i, col_block)),
                            pl.BlockSpec((tr, D), lambda i: (i, col_block + 1))],
                  out_specs=row, semantics=("parallel",))(y_gla, y_pool, proj, proj)


def _merge_bwd(dmerged, y_gla, y_pool, proj, *, T, D, col_block):
    tr = _tile(T, 128, 16)

    def body(dm_ref, yg_ref, yp_ref, g1_ref, g2_ref, dyg_ref, dyp_ref, dg_ref):
        dm = dm_ref[...]
        s1 = _sigmoid(g1_ref[...])
        s2 = _sigmoid(g2_ref[...])
        dyg_ref[...] = (dm * s1).astype(BF16)
        dyp_ref[...] = (dm * s2).astype(BF16)
        dg_ref[:, :D] = (dm * yg_ref[...] * s1 * (1.0 - s1)).astype(BF16)
        dg_ref[:, D:] = (dm * yp_ref[...] * s2 * (1.0 - s2)).astype(BF16)

    row = pl.BlockSpec((tr, D), lambda i: (i, 0))
    return _pcall(body, name="merge_bwd",
                  out_shape=(jax.ShapeDtypeStruct((T, D), BF16), jax.ShapeDtypeStruct((T, D), BF16),
                             jax.ShapeDtypeStruct((T, 2 * D), BF16)),
                  grid=(T // tr,),
                  in_specs=[row, row, row, pl.BlockSpec((tr, D), lambda i: (i, col_block)),
                            pl.BlockSpec((tr, D), lambda i: (i, col_block + 1))],
                  out_specs=(row, row, pl.BlockSpec((tr, 2 * D), lambda i: (i, 0))),
                  semantics=("parallel",))(dmerged, y_gla, y_pool, proj, proj)


def _attn_fwd(q, kv, *, T, D, M):
    H = CROSS_HEADS
    HD = D // H
    tq = _tile(T, 512, 16)
    scale = HD ** -0.5

    def body(q_ref, kv_ref, o_ref):
        for h in range(H):
            hs = slice(h * HD, (h + 1) * HD)
            s = _dot(q_ref[:, hs], kv_ref[:, hs], NT) * scale
            e = jnp.exp(s - jnp.max(s, axis=-1, keepdims=True))
            p = e / jnp.sum(e, axis=-1, keepdims=True)
            o_ref[:, hs] = _dot(p.astype(BF16), kv_ref[:, D + h * HD:D + (h + 1) * HD]).astype(BF16)

    row = pl.BlockSpec((tq, D), lambda i: (i, 0))
    return _pcall(body, name="attn_fwd", out_shape=jax.ShapeDtypeStruct((T, D), BF16),
                  grid=(T // tq,), in_specs=[row, pl.BlockSpec((M, 2 * D), lambda i: (0, 0))],
                  out_specs=row, semantics=("parallel",))(q, kv)


def _attn_bwd(q, kv, do, *, T, D, M):
    H = CROSS_HEADS
    HD = D // H
    tq = _tile(T, 512, 16)
    scale = HD ** -0.5

    def body(q_ref, kv_ref, do_ref, dq_ref, dkv_ref):
        @pl.when(pl.program_id(0) == 0)
        def _():
            dkv_ref[...] = jnp.zeros_like(dkv_ref)

        for h in range(H):
            hs = slice(h * HD, (h + 1) * HD)
            vs = slice(D + h * HD, D + (h + 1) * HD)
            qh = q_ref[:, hs]
            kh = kv_ref[:, hs]
            s = _dot(qh, kh, NT) * scale
            e = jnp.exp(s - jnp.max(s, axis=-1, keepdims=True))
            p = e / jnp.sum(e, axis=-1, keepdims=True)
            p_b = p.astype(BF16)
            d_o = do_ref[:, hs]
            dkv_ref[:, vs] += _dot(p_b, d_o, TN)
            dp = _dot(d_o, kv_ref[:, vs], NT)
            ds = (p * (dp - jnp.sum(dp * p, axis=-1, keepdims=True)) * scale).astype(BF16)
            dq_ref[:, hs] = _dot(ds, kh).astype(BF16)
            dkv_ref[:, hs] += _dot(ds, qh, TN)

    row = pl.BlockSpec((tq, D), lambda i: (i, 0))
    full = pl.BlockSpec((M, 2 * D), lambda i: (0, 0))
    return _pcall(body, name="attn_bwd",
                  out_shape=(jax.ShapeDtypeStruct((T, D), BF16), jax.ShapeDtypeStruct((M, 2 * D), F32)),
                  grid=(T // tq,), in_specs=[row, full, row], out_specs=(row, full),
                  semantics=("arbitrary",))(q, kv, do)


def _shift_down(x, halo, s, t):
    out = pltpu.roll(x, s, 0)
    for j in range(s):
        out = jnp.where(t == j, halo[SUBLANES - s + j:SUBLANES - s + j + 1, :], out)
    return out


def _shift_up(x, halo, s, t, rows):
    out = pltpu.roll(x, rows - s, 0)
    for j in range(s):
        out = jnp.where(t == rows - s + j, halo[j:j + 1, :], out)
    return out


def _conv_tiles(T):
    tt = _tile(T, 128, SUBLANES)
    return tt, tt // SUBLANES, T // SUBLANES


def _conv_fwd(u0, conv_w, conv_b, *, T, F):
    tt, hb, _ = _conv_tiles(T)
    cw = _tile(F, 512)

    def body(u_ref, prev_ref, w_ref, b_ref, f_ref):
        i = pl.program_id(0)
        t = lax.broadcasted_iota(jnp.int32, (tt, 1), 0)

        def conv(cs):
            x = u_ref[:, cs]
            halo = jnp.where(i > 0, prev_ref[:, cs], 0.0)
            return (w_ref[2:3, cs] * x + w_ref[1:2, cs] * _shift_down(x, halo, 1, t)
                    + w_ref[0:1, cs] * _shift_down(x, halo, 2, t) + b_ref[:, cs])

        for j in range(F // cw):
            gate = conv(slice(j * cw, (j + 1) * cw))
            val = conv(slice(F + j * cw, F + (j + 1) * cw))
            f_ref[:, j * cw:(j + 1) * cw] = (gate * _sigmoid(gate) * val).astype(BF16)

    return _pcall(body, name="conv_fwd", out_shape=jax.ShapeDtypeStruct((T, F), BF16),
                  grid=(T // tt,),
                  in_specs=[pl.BlockSpec((tt, 2 * F), lambda i: (i, 0)),
                            pl.BlockSpec((SUBLANES, 2 * F), lambda i: (jnp.maximum(i * hb - 1, 0), 0)),
                            pl.BlockSpec((CONV_W, 2 * F), lambda i: (0, 0)),
                            pl.BlockSpec((1, 2 * F), lambda i: (0, 0))],
                  out_specs=pl.BlockSpec((tt, F), lambda i: (i, 0)),
                  semantics=("parallel",))(u0, u0, conv_w, conv_b)


def _conv_bwd(u0, conv_w, conv_b, df, *, T, F):
    tt, hb, nb = _conv_tiles(T)
    nt = T // tt
    cw = _tile(F, 512)

    def body(u_ref, prev_ref, next_ref, df_ref, dfn_ref, w_ref, b_ref, du0_ref, dw_ref, db_ref):
        i = pl.program_id(0)
        t = lax.broadcasted_iota(jnp.int32, (tt, 1), 0)
        t8 = lax.broadcasted_iota(jnp.int32, (SUBLANES, 1), 0)

        @pl.when(i == 0)
        def _():
            dw_ref[...] = jnp.zeros_like(dw_ref)
            db_ref[...] = jnp.zeros_like(db_ref)

        def conv(cs):
            x = u_ref[:, cs]
            halo = jnp.where(i > 0, prev_ref[:, cs], 0.0)
            x1 = _shift_down(x, halo, 1, t)
            x2 = _shift_down(x, halo, 2, t)
            u = w_ref[2:3, cs] * x + w_ref[1:2, cs] * x1 + w_ref[0:1, cs] * x2 + b_ref[:, cs]
            xn = next_ref[:, cs]
            tail = x[tt - SUBLANES:, :]
            un = (w_ref[2:3, cs] * xn + w_ref[1:2, cs] * _shift_down(xn, tail, 1, t8)
                  + w_ref[0:1, cs] * _shift_down(xn, tail, 2, t8) + b_ref[:, cs])
            return u, un, (x, x1, x2)

        def glu_grad(gate, val, dff):
            sg = _sigmoid(gate)
            return dff * val * (sg * (1.0 + gate * (1.0 - sg))), dff * (gate * sg)

        def finish(cs, du, dun, xs):
            du0 = (w_ref[2:3, cs] * du + w_ref[1:2, cs] * _shift_up(du, dun, 1, t, tt)
                   + w_ref[0:1, cs] * _shift_up(du, dun, 2, t, tt))
            du0_ref[:, cs] = du0.astype(BF16)
            db_ref[:, cs] += jnp.sum(du, axis=0, keepdims=True)
            dw_ref[2:3, cs] += jnp.sum(du * xs[0], axis=0, keepdims=True)
            dw_ref[1:2, cs] += jnp.sum(du * xs[1], axis=0, keepdims=True)
            dw_ref[0:1, cs] += jnp.sum(du * xs[2], axis=0, keepdims=True)

        for j in range(F // cw):
            fs = slice(j * cw, (j + 1) * cw)
            gs, vs = fs, slice(F + j * cw, F + (j + 1) * cw)
            ug, ung, xg = conv(gs)
            uv, unv, xv = conv(vs)
            dug, duv = glu_grad(ug, uv, df_ref[:, fs].astype(F32))
            dung, dunv = glu_grad(ung, unv, dfn_ref[0:SUBLANES, fs].astype(F32))
            dung = jnp.where(i < nt - 1, dung, 0.0)
            dunv = jnp.where(i < nt - 1, dunv, 0.0)
            finish(gs, dug, dung, xg)
            finish(vs, duv, dunv, xv)

    wide = lambda rows, fn: pl.BlockSpec((rows, 2 * F), fn)
    nxt = lambda i: (jnp.minimum((i + 1) * hb, nb - 1), 0)
    return _pcall(body, name="conv_bwd",
                  out_shape=(jax.ShapeDtypeStruct((T, 2 * F), BF16),
                             jax.ShapeDtypeStruct((CONV_W, 2 * F), F32),
                             jax.ShapeDtypeStruct((1, 2 * F), F32)),
                  grid=(nt,),
                  in_specs=[wide(tt, lambda i: (i, 0)),
                            wide(SUBLANES, lambda i: (jnp.maximum(i * hb - 1, 0), 0)),
                            wide(SUBLANES, nxt),
                            pl.BlockSpec((tt, F), lambda i: (i, 0)),
                            pl.BlockSpec((2 * SUBLANES, F),
                                         lambda i: (jnp.minimum((i + 1) * (hb // 2), nb // 2 - 1), 0)),
                            wide(CONV_W, lambda i: (0, 0)), wide(1, lambda i: (0, 0))],
                  out_specs=(wide(tt, lambda i: (i, 0)), wide(CONV_W, lambda i: (0, 0)),
                             wide(1, lambda i: (0, 0))),
                  semantics=("arbitrary",))(u0, u0, u0, df, df, conv_w, conv_b)


def _adamw(w, g, m, v, *, name):
    R, C = w.shape
    tr = _tile(R, max(SUBLANES, (1 << 19) // max(C, 1) // SUBLANES * SUBLANES), SUBLANES)
    c1 = 1.0 / (1.0 - ADAM_B1 ** ADAM_STEP)
    c2 = 1.0 / (1.0 - ADAM_B2 ** ADAM_STEP)

    def body(w_ref, g_ref, m_ref, v_ref, d_ref, mo_ref, vo_ref):
        gv = g_ref[...]
        mn = ADAM_B1 * m_ref[...] + (1.0 - ADAM_B1) * gv
        vn = ADAM_B2 * v_ref[...] + (1.0 - ADAM_B2) * (gv * gv)
        d_ref[...] = -ADAM_LR * ((mn * c1) / (jnp.sqrt(vn * c2) + ADAM_EPS) + ADAM_WD * w_ref[...])
        mo_ref[...] = mn
        vo_ref[...] = vn

    blk = pl.BlockSpec((tr, C), lambda i: (i, 0))
    shp = jax.ShapeDtypeStruct((R, C), F32)
    return _pcall(body, name=name, out_shape=(shp, shp, shp), grid=(R // tr,),
                  in_specs=[blk] * 4, out_specs=(blk,) * 3, semantics=("parallel",))(w, g, m, v)


def _blk(h, C, elems=1 << 19, align=16):
    th = _tile(h, max(align, elems // C // align * align), align)
    if th < h or h * C <= 2 * elems:
        return th, C
    return h, _tile(C, max(LANES, elems // h // LANES * LANES))


def _adamw_halves(w, m, v, g_mine, g_other, c_idx, *, name):
    _, h, C = w.shape
    th, tc = _blk(h, C, align=SUBLANES)
    c1 = 1.0 / (1.0 - ADAM_B1 ** ADAM_STEP)
    c2 = 1.0 / (1.0 - ADAM_B2 ** ADAM_STEP)

    def body(c_ref, w_ref, m_ref, v_ref, gm_ref, go_ref, g_ref, d_ref, mo_ref, vo_ref):
        gv = jnp.where(pl.program_id(0) == c_ref[0], gm_ref[...], go_ref[...])
        mn = ADAM_B1 * m_ref[...] + (1.0 - ADAM_B1) * gv
        vn = ADAM_B2 * v_ref[...] + (1.0 - ADAM_B2) * (gv * gv)
        d_ref[...] = -ADAM_LR * ((mn * c1) / (jnp.sqrt(vn * c2) + ADAM_EPS) + ADAM_WD * w_ref[...])
        g_ref[...] = gv
        mo_ref[...] = mn
        vo_ref[...] = vn

    blk = pl.BlockSpec((None, th, tc), lambda s, i, j, c: (s, i, j))

    def pick(mine):
        def index(s, i, j, c):
            use = (s == c[0]) if mine else (s != c[0])
            return jnp.where(use, i, 0), jnp.where(use, j, 0)
        return pl.BlockSpec((th, tc), index)

    shp = jax.ShapeDtypeStruct((2, h, C), F32)
    return _pcall(body, name=name, out_shape=(shp,) * 4, grid=(2, h // th, C // tc), prefetch=1,
                  in_specs=[blk, blk, blk, pick(True), pick(False)], out_specs=(blk,) * 4,
                  semantics=("parallel", "parallel", "parallel"))(c_idx, w, m, v, g_mine, g_other)


def _mesh_pos():
    x, y, c = lax.axis_index("x"), lax.axis_index("y"), lax.axis_index("c")
    others = [(1 - x, y), (x, 1 - y), (1 - x, 1 - y)]
    return x, y, c, others


def _gather_copies(shards, lands, send_sems, recv_sems):
    x, y, c, others = _mesh_pos()
    me = 2 * x + y
    return [pltpu.make_async_remote_copy(
        src_ref=shards[a].at[c], dst_ref=lands[a].at[me, c],
        send_sem=send_sems.at[3 * a + j], recv_sem=recv_sems.at[3 * a + j],
        device_id=(*chip, c), device_id_type=MESH)
        for a in range(len(shards)) for j, chip in enumerate(others)]


def _gather_start(shards, after, *, name):
    n = len(shards)
    lands = [lax.empty((N_CHIPS, *s.shape), s.dtype) for s in shards]
    HBM = pl.BlockSpec(memory_space=pltpu.HBM)
    SEM = pl.BlockSpec(memory_space=pltpu.SEMAPHORE)

    def body(*refs):
        srcs, zones = refs[:n], refs[n:2 * n]
        send_sems, recv_sems = refs[2 * n + 1], refs[2 * n + 2]
        token = refs[-1]
        for cp in _gather_copies(srcs, zones, send_sems, recv_sems):
            cp.start()
        token[...] = jnp.zeros_like(token)

    hbm = lambda a: pltpu.HBM(a.shape, a.dtype)
    res = _pcall(body, name=name,
                 out_shape=(pltpu.SemaphoreType.DMA((3 * n,)), pltpu.SemaphoreType.DMA((3 * n,)),
                            *[hbm(a) for a in shards], *[hbm(a) for a in lands],
                            jax.ShapeDtypeStruct((SUBLANES, LANES), F32)),
                 in_specs=[*[HBM] * (2 * n), pl.BlockSpec(memory_space=pl.ANY)],
                 out_specs=(SEM, SEM, *[HBM] * (2 * n), pl.BlockSpec(memory_space=pltpu.VMEM)),
                 aliases={i: 2 + i for i in range(2 * n)}, split_copy=True)(
        *[pltpu.with_memory_space_constraint(a, pltpu.HBM) for a in shards + lands], after)
    return res[0], res[1], list(res[2:2 + n]), list(res[2 + n:2 + 2 * n]), res[-1]


def _gather_wait(send_sems, recv_sems, shards, lands, after, *, name):
    n = len(shards)
    HBM = pl.BlockSpec(memory_space=pltpu.HBM)
    SEM = pl.BlockSpec(memory_space=pltpu.SEMAPHORE)

    def body(*refs):
        srcs, zones = refs[:n], refs[n:2 * n]
        s_sems, r_sems = refs[2 * n], refs[2 * n + 1]
        for cp in _gather_copies(srcs, zones, s_sems, r_sems):
            cp.wait_send()
            cp.wait_recv()

    hbm = lambda a: pltpu.HBM(a.shape, a.dtype)
    res = _pcall(body, name=name, out_shape=(*[hbm(a) for a in shards], *[hbm(a) for a in lands]),
                 in_specs=[*[HBM] * (2 * n), SEM, SEM, pl.BlockSpec(memory_space=pl.ANY)],
                 out_specs=tuple([HBM] * (2 * n)), aliases={i: i for i in range(2 * n)},
                 split_copy=True)(*shards, *lands, send_sems, recv_sems, after)
    return list(res[:n]), list(res[n:])


def _gather_pass_on(shards, lands, *, name):
    n = len(shards)
    ANY = pl.BlockSpec(memory_space=pl.ANY)
    PER = 4

    def body(*refs):
        srcs, zones, outs = refs[:n], refs[n:2 * n], refs[2 * n:3 * n]
        send_sems, recv_sems = refs[3 * n:]
        x, y, c, others = _mesh_pos()
        me = 2 * x + y
        sibling = (x, y, 1 - c)

        def copy(a, k, src, dst):
            return pltpu.make_async_remote_copy(
                src_ref=src, dst_ref=dst, send_sem=send_sems.at[PER * a + k],
                recv_sem=recv_sems.at[PER * a + k], device_id=sibling, device_id_type=MESH)

        copies = []
        for a in range(n):
            for j, chip in enumerate(others):
                idx = 2 * chip[0] + chip[1]
                copies.append(copy(a, j, zones[a].at[idx, c], outs[a].at[idx, c]))
            copies.append(copy(a, 3, srcs[a], outs[a].at[me]))
        for cp in copies:
            cp.start()
        for a in range(n):
            for j, chip in enumerate(others):
                idx = 2 * chip[0] + chip[1]
                copy(a, j, zones[a].at[idx, 1 - c], outs[a].at[idx, 1 - c]).wait_recv()
            copy(a, 3, srcs[a], outs[a].at[me]).wait_recv()
        for cp in copies:
            cp.wait_send()

    return _pcall(body, name=name,
                  out_shape=[jax.ShapeDtypeStruct(z.shape, z.dtype) for z in lands],
                  in_specs=[ANY] * (2 * n), out_specs=[ANY] * n,
                  aliases={n + i: i for i in range(n)},
                  scratch_shapes=[pltpu.SemaphoreType.DMA((PER * n,)),
                                  pltpu.SemaphoreType.DMA((PER * n,))])(*shards, *lands)


def _exchange_halves(grads, after, *, name):
    n = len(grads)
    ANY = pl.BlockSpec(memory_space=pl.ANY)

    def body(*refs):
        ins, outs = refs[:n], refs[n + 1:2 * n + 1]
        send_sems, recv_sems = refs[2 * n + 1:]
        x, y, c, _ = _mesh_pos()
        copies = [pltpu.make_async_remote_copy(
            src_ref=ins[a].at[:, 1 - c], dst_ref=outs[a], send_sem=send_sems.at[a],
            recv_sem=recv_sems.at[a], device_id=(x, y, 1 - c), device_id_type=MESH) for a in range(n)]
        for cp in copies:
            cp.start()
        for cp in copies:
            cp.wait()

    return _pcall(body, name=name,
                  out_shape=[jax.ShapeDtypeStruct((g.shape[0], *g.shape[2:]), g.dtype) for g in grads],
                  in_specs=[ANY] * (n + 1), out_specs=[ANY] * n,
                  scratch_shapes=[pltpu.SemaphoreType.DMA((n,)), pltpu.SemaphoreType.DMA((n,))])(*grads, after)


def _add_halves(grad, recv, c_idx, *, name):
    S, _, h, C = grad.shape
    th, tc = _blk(h, C)

    def body(c_ref, g_ref, r_ref, o_ref):
        o_ref[...] = (g_ref[...].astype(F32) + r_ref[...].astype(F32)).astype(o_ref.dtype)

    return _pcall(body, name=name, out_shape=jax.ShapeDtypeStruct((S, h, C), grad.dtype),
                  grid=(S, h // th, C // tc), prefetch=1,
                  in_specs=[pl.BlockSpec((None, None, th, tc), lambda s, i, j, c: (s, c[0], i, j)),
                            pl.BlockSpec((None, th, tc), lambda s, i, j, c: (s, i, j))],
                  out_specs=pl.BlockSpec((None, th, tc), lambda s, i, j, c: (s, i, j)),
                  semantics=("parallel", "parallel", "parallel"))(c_idx, grad, recv)


def _scatter_copies(srcs, lands, send_sems, recv_sems):
    x, y, c, others = _mesh_pos()
    return [pltpu.make_async_remote_copy(
        src_ref=srcs[a].at[2 * chip[0] + chip[1]], dst_ref=lands[a].at[j],
        send_sem=send_sems.at[3 * a + j], recv_sem=recv_sems.at[3 * a + j],
        device_id=(*chip, c), device_id_type=MESH)
        for a in range(len(srcs)) for j, chip in enumerate(others)]


def _scatter_start(sums, *, name):
    n = len(sums)
    lands = [lax.empty((3, *s.shape[1:]), s.dtype) for s in sums]
    HBM = pl.BlockSpec(memory_space=pltpu.HBM)
    SEM = pl.BlockSpec(memory_space=pltpu.SEMAPHORE)

    def body(*refs):
        srcs, zones = refs[:n], refs[n:2 * n]
        send_sems, recv_sems = refs[2 * n], refs[2 * n + 1]
        token = refs[-1]
        for cp in _scatter_copies(srcs, zones, send_sems, recv_sems):
            cp.start()
        token[...] = jnp.zeros_like(token)

    hbm = lambda a: pltpu.HBM(a.shape, a.dtype)
    res = _pcall(body, name=name,
                 out_shape=(pltpu.SemaphoreType.DMA((3 * n,)), pltpu.SemaphoreType.DMA((3 * n,)),
                            *[hbm(a) for a in sums], *[hbm(a) for a in lands],
                            jax.ShapeDtypeStruct((SUBLANES, LANES), F32)),
                 in_specs=[HBM] * (2 * n),
                 out_specs=(SEM, SEM, *[HBM] * (2 * n), pl.BlockSpec(memory_space=pltpu.VMEM)),
                 aliases={i: 2 + i for i in range(2 * n)}, split_copy=True)(
        *[pltpu.with_memory_space_constraint(a, pltpu.HBM) for a in sums + lands])
    return res[0], res[1], list(res[2:2 + n]), list(res[2 + n:2 + 2 * n]), res[-1]


def _scatter_wait(send_sems, recv_sems, sums, lands, after, *, name):
    n = len(sums)
    HBM = pl.BlockSpec(memory_space=pltpu.HBM)
    SEM = pl.BlockSpec(memory_space=pltpu.SEMAPHORE)

    def body(*refs):
        srcs, zones = refs[:n], refs[n:2 * n]
        s_sems, r_sems = refs[2 * n], refs[2 * n + 1]
        for cp in _scatter_copies(srcs, zones, s_sems, r_sems):
            cp.wait_send()
            cp.wait_recv()

    hbm = lambda a: pltpu.HBM(a.shape, a.dtype)
    res = _pcall(body, name=name, out_shape=(*[hbm(a) for a in sums], *[hbm(a) for a in lands]),
                 in_specs=[*[HBM] * (2 * n), SEM, SEM, pl.BlockSpec(memory_space=pl.ANY)],
                 out_specs=tuple([HBM] * (2 * n)), aliases={i: i for i in range(2 * n)},
                 split_copy=True)(*sums, *lands, send_sems, recv_sems, after)
    return list(res[:n]), list(res[n:])


def _add_chips(sums, recv, chip_idx, *, name):
    _, h, C = sums.shape
    th, tc = _blk(h, C)

    def body(k_ref, s_ref, r_ref, o_ref):
        acc = s_ref[...].astype(F32) + r_ref[0].astype(F32)
        acc = acc + r_ref[1].astype(F32)
        o_ref[...] = acc + r_ref[2].astype(F32)

    return _pcall(body, name=name, out_shape=jax.ShapeDtypeStruct((h, C), F32),
                  grid=(h // th, C // tc), prefetch=1,
                  in_specs=[pl.BlockSpec((None, th, tc), lambda i, j, k: (k[0], i, j)),
                            pl.BlockSpec((3, th, tc), lambda i, j, k: (0, i, j))],
                  out_specs=pl.BlockSpec((th, tc), lambda i, j, k: (i, j)),
                  semantics=("parallel", "parallel"))(chip_idx, sums, recv)


def _swap_halves(halves, *, name):
    n = len(halves)
    ANY = pl.BlockSpec(memory_space=pl.ANY)

    def body(*refs):
        ins, outs = refs[:n], refs[n:2 * n]
        send_sems, recv_sems = refs[2 * n:]
        x, y, c, _ = _mesh_pos()
        copies = [pltpu.make_async_remote_copy(
            src_ref=ins[a], dst_ref=outs[a], send_sem=send_sems.at[a], recv_sem=recv_sems.at[a],
            device_id=(x, y, 1 - c), device_id_type=MESH) for a in range(n)]
        for cp in copies:
            cp.start()
        for cp in copies:
            cp.wait()

    return _pcall(body, name=name,
                  out_shape=[jax.ShapeDtypeStruct(s.shape, s.dtype) for s in halves],
                  in_specs=[ANY] * n, out_specs=[ANY] * n,
                  scratch_shapes=[pltpu.SemaphoreType.DMA((n,)), pltpu.SemaphoreType.DMA((n,))])(*halves)


def _all_reduce_small(buf):
    R, L = buf.shape
    NDEV = 8

    def body(x_ref, sum_ref, all_ref, send_sems, recv_sems, local_sem):
        x, y, c, others = _mesh_pos()
        me, sibling = (x, y, c), (x, y, 1 - c)

        def slot(px, py, pc):
            return all_ref.at[4 * px + 2 * py + pc]

        def copy(k, block, to, src=None):
            return pltpu.make_async_remote_copy(
                src_ref=slot(*block) if src is None else src, dst_ref=slot(*block),
                send_sem=send_sems.at[k], recv_sem=recv_sems.at[k], device_id=to, device_id_type=MESH)

        mine = pltpu.make_async_copy(x_ref, slot(*me), local_sem)
        mine.start()
        first = [copy(0, me, sibling, src=x_ref)]
        first += [copy(1 + j, me, (*chip, c), src=x_ref) for j, chip in enumerate(others)]
        for cp in first:
            cp.start()
        passed = [copy(4 + j, (*chip, c), sibling) for j, chip in enumerate(others)]
        for j, chip in enumerate(others):
            copy(1 + j, (*chip, c), me).wait_recv()
            passed[j].start()
        copy(0, sibling, me).wait_recv()
        for j, chip in enumerate(others):
            copy(4 + j, (*chip, 1 - c), me).wait_recv()
        for cp in first + passed:
            cp.wait_send()
        mine.wait()
        acc = all_ref[0]
        for d in range(1, NDEV):
            acc = acc + all_ref[d]
        sum_ref[...] = acc

    VM = pl.BlockSpec(memory_space=pltpu.VMEM)
    return _pcall(body, name="all_reduce_small",
                  out_shape=(jax.ShapeDtypeStruct((R, L), F32), jax.ShapeDtypeStruct((NDEV, R, L), F32)),
                  in_specs=[VM], out_specs=(VM, VM),
                  scratch_shapes=[pltpu.SemaphoreType.DMA((7,)), pltpu.SemaphoreType.DMA((7,)),
                                  pltpu.SemaphoreType.DMA])(buf)[0]


def _pack(arrs, rows_multiple=16):
    flat = [a.reshape(-1).astype(F32) for a in arrs]
    sizes = [f.shape[0] for f in flat]
    total = sum(sizes)
    per = LANES * rows_multiple
    padded = -(-total // per) * per
    flat.append(jnp.zeros((padded - total,), F32))
    offs = [0]
    for s in sizes:
        offs.append(offs[-1] + s)
    return jnp.concatenate(flat).reshape(padded // LANES, LANES), offs


def _unpack(buf, offs, shapes):
    flat = buf.reshape(-1)
    return [flat[offs[i]:offs[i + 1]].reshape(s) for i, s in enumerate(shapes)]


def kernel(x, mem, g_mix, w_in, w_a2, b_a, g_gla, w_pool, pool_scale, w_branch, w_out, g_cross, g_mem, w_cq, w_ckv, w_co, g_ffn, w_up, conv_w, conv_b, w_down, g_final, loss_target, m_g_mix, m_w_in, m_w_a2, m_b_a, m_g_gla, m_w_pool, m_pool_scale, m_w_branch, m_w_out, m_g_cross, m_g_mem, m_w_cq, m_w_ckv, m_w_co, m_g_ffn, m_w_up, m_conv_w, m_conv_b, m_w_down, m_g_final, v_g_mix, v_w_in, v_w_a2, v_b_a, v_g_gla, v_w_pool, v_pool_scale, v_w_branch, v_w_out, v_g_cross, v_g_mem, v_w_cq, v_w_ckv, v_w_co, v_g_ffn, v_w_up, v_conv_w, v_conv_b, v_w_down, v_g_final):
    weights = dict(g_mix=g_mix, w_in=w_in, w_a2=w_a2, b_a=b_a, g_gla=g_gla, w_pool=w_pool,
                   pool_scale=pool_scale, w_branch=w_branch, w_out=w_out, g_cross=g_cross, g_mem=g_mem,
                   w_cq=w_cq, w_ckv=w_ckv, w_co=w_co, g_ffn=g_ffn, w_up=w_up, conv_w=conv_w,
                   conv_b=conv_b, w_down=w_down, g_final=g_final)
    mom_m = dict(g_mix=m_g_mix, w_in=m_w_in, w_a2=m_w_a2, b_a=m_b_a, g_gla=m_g_gla, w_pool=m_w_pool,
                 pool_scale=m_pool_scale, w_branch=m_w_branch, w_out=m_w_out, g_cross=m_g_cross,
                 g_mem=m_g_mem, w_cq=m_w_cq, w_ckv=m_w_ckv, w_co=m_w_co, g_ffn=m_g_ffn, w_up=m_w_up,
                 conv_w=m_conv_w, conv_b=m_conv_b, w_down=m_w_down, g_final=m_g_final)
    mom_v = dict(g_mix=v_g_mix, w_in=v_w_in, w_a2=v_w_a2, b_a=v_b_a, g_gla=v_g_gla, w_pool=v_w_pool,
                 pool_scale=v_pool_scale, w_branch=v_w_branch, w_out=v_w_out, g_cross=v_g_cross,
                 g_mem=v_g_mem, w_cq=v_w_cq, w_ckv=v_w_ckv, w_co=v_w_co, g_ffn=v_g_ffn, w_up=v_w_up,
                 conv_w=v_conv_w, conv_b=v_conv_b, w_down=v_w_down, g_final=v_g_final)
    order = list(weights)
    big = ["w_in", "w_branch", "w_out", "w_cq", "w_ckv", "w_co", "w_up", "w_down"]
    small_sharded = ["w_a2", "w_pool", "conv_w"]
    small_repl = ["g_mix", "b_a", "g_gla", "pool_scale", "g_cross", "g_mem", "g_ffn", "conv_b", "g_final"]

    xs, ms, tgt = x[0], mem[0], loss_target[0]
    T, D = xs.shape
    M = ms.shape[0]
    DK, DV, PW = b_a.shape[1], g_gla.shape[1], pool_scale.shape[1]
    RANK = w_a2.shape[1]
    F2 = conv_b.shape[1]
    F = F2 // 2
    DIN = N_CHIPS * w_in.shape[2]
    OFF_A = 2 * DK + 2 * DV
    OFF_P = OFF_A + RANK
    RP = LANES
    GW = PW // POOL_GROUPS
    assert PW == DV and 4 * DV == 2 * D and OFF_P + PW + 2 * D == DIN

    cx, cy, cc = lax.axis_index("x"), lax.axis_index("y"), lax.axis_index("c")
    chip = 2 * cx + cy
    c_idx = jnp.reshape(cc, (1,)).astype(jnp.int32)
    chip_idx = jnp.reshape(chip, (1,)).astype(jnp.int32)

    def halves(a):
        return a.reshape(2, a.shape[0] // 2, a.shape[1])

    shard2d = {k: (weights[k][0].T if k == "w_in" else weights[k][0]) for k in big}
    small_pack, small_offs = _pack([weights[k][0] for k in small_sharded], rows_multiple=32)
    shard_halves = {k: halves(shard2d[k].astype(BF16)) for k in big}
    shard_halves["small"] = halves(small_pack)
    flying = {}
    tok = small_pack
    for group, keys in (("in", ["w_in", "small"]), ("mix", ["w_branch", "w_out"]),
                        ("cross", ["w_cq", "w_ckv", "w_co"]), ("up", ["w_up"]), ("down", ["w_down"])):
        s_sems, r_sems, sh, zones, tok = _gather_start([shard_halves[k] for k in keys], tok,
                                                       name=f"gather_start_{group}")
        flying[group] = (keys, s_sems, r_sems, sh, zones)
    tok = tok[0:1, 0:1]

    def arrive(group, after):
        keys, s_sems, r_sems, sh, zones = flying[group]
        sh, zones = _gather_wait(s_sems, r_sems, sh, zones, after, name=f"gather_wait_{group}")
        full = _gather_pass_on(sh, zones, name=f"gather_pass_on_{group}")
        return {k: f.reshape(N_CHIPS, f.shape[1] * f.shape[2], f.shape[3]) for k, f in zip(keys, full)}

    def cols(g):
        return jnp.transpose(g, (1, 0, 2)).reshape(g.shape[1], -1)

    def rows(g):
        return g.reshape(-1, g.shape[2])

    h1, r1 = _rms_fwd(xs, g_mix + tok, name="norm_mix")
    gw = arrive("in", h1)
    small_all = gw["small"]
    W_in = rows(gw["w_in"])
    W_main = jnp.concatenate([W_in[:OFF_A], W_in[OFF_P:]], axis=0)
    W_a = jnp.pad(W_in[OFF_A:OFF_P], ((0, RP - RANK), (0, 0)))
    sm = [_unpack(small_all[j], small_offs, [weights[k].shape[1:] for k in small_sharded]) for j in range(N_CHIPS)]
    W_a2 = jnp.concatenate([sm[j][0] for j in range(N_CHIPS)], axis=1)
    W_a2p = jnp.pad(W_a2, ((0, RP - RANK), (0, 0))).astype(BF16)
    W_pool = jnp.concatenate([sm[j][1] for j in range(N_CHIPS)], axis=1).astype(BF16)
    W_conv = jnp.concatenate([sm[j][2] for j in range(N_CHIPS)], axis=1)

    proj = _mm(h1, W_main, "nt", name="proj_main", out_dtype=F32)
    a_pad = _mm(h1, W_a, "nt", name="proj_gate_rank", out_dtype=F32)
    o_gla, o_raw, states = _gla_fwd(proj, a_pad, W_a2p, b_a, g_gla, T=T, DK=DK, DV=DV)
    o_pool = _pool_fwd(proj, W_pool, pool_scale, T=T, PW=PW, col_block=3)
    gw = arrive("mix", o_pool)
    W_branch, W_out = rows(gw["w_branch"]), rows(gw["w_out"])
    y_gla = _mm(o_gla, W_branch, "nn", name="branch_gla", out_dtype=F32, K=DV)
    y_pool = _mm(o_pool, W_branch, "nn", name="branch_pool", out_dtype=F32, K=PW, b_off=(DV, 0))
    merged = _merge_fwd(y_gla, y_pool, proj, T=T, D=D, col_block=2)
    x1 = _mm(merged, W_out, "nn", name="mix_out", out_dtype=F32, add=xs)

    h2, r2 = _rms_fwd(x1, g_cross, name="norm_cross")
    mem_n, rm = _rms_fwd(ms, g_mem, name="norm_mem")
    gw = arrive("cross", h2)
    W_cq, W_ckv, W_co = rows(gw["w_cq"]), cols(gw["w_ckv"]), rows(gw["w_co"])
    qc = _mm(h2, W_cq, "nn", name="cross_q", out_dtype=BF16)
    kv = _mm(mem_n, W_ckv, "nn", name="cross_kv", out_dtype=BF16)
    o_att = _attn_fwd(qc, kv, T=T, D=D, M=M)
    x2 = _mm(o_att, W_co, "nn", name="cross_out", out_dtype=F32, add=x1)

    h3, r3 = _rms_fwd(x2, g_ffn, name="norm_ffn")
    W_up = cols(arrive("up", h3)["w_up"])
    u0 = _mm(h3, W_up, "nn", name="ffn_up", out_dtype=F32)
    f_act = _conv_fwd(u0, W_conv, conv_b, T=T, F=F)
    W_down = rows(arrive("down", f_act)["w_down"])
    x3 =_mm(f_act, W_down, "nn", name="ffn_down", out_dtype=F32, add=x2)

    loss_part, dx3, dx3_b, dg_final = _loss_head(x3, g_final.reshape(1, D), tgt)

    def col_shards(g):
        K, N = g.shape
        return jnp.transpose(g.reshape(K, N_CHIPS, N // N_CHIPS), (1, 0, 2)).reshape(N_CHIPS, 2, K // 2, N // N_CHIPS)

    def row_shards(g):
        R, N = g.shape
        return g.reshape(N_CHIPS, 2, R // N_CHIPS // 2, N)

    in_flight = []

    def reduce_start(group, keys, partials, after):
        from_sibling = _exchange_halves(partials, after, name=f"grad_exchange_halves_{group}")
        chip_sums = [_add_halves(p, r, c_idx, name=f"grad_add_halves_{k}")
                     for k, p, r in zip(keys, partials, from_sibling)]
        s_sems, r_sems, sums, lands, token = _scatter_start(chip_sums, name=f"grad_scatter_start_{group}")
        in_flight.append((group, keys, s_sems, r_sems, sums, lands))
        return token[0:1, 0:1]

    df = _mm(dx3_b, W_down, "nt", name="d_ffn_act", out_dtype=BF16)
    dW_down = _mm(f_act, dx3_b, "tn", name="dw_down", out_dtype=BF16)
    du0, dconv_w, dconv_b = _conv_bwd(u0, W_conv, conv_b, df, T=T, F=F)
    dh3 = _mm(du0, W_up, "nt", name="d_ffn_in", out_dtype=F32)
    dW_up = _mm(h3, du0, "tn", name="dw_up", out_dtype=BF16)
    tok = reduce_start("ffn", ["w_down", "w_up"], [row_shards(dW_down), col_shards(dW_up)], dh3)
    dx2, dx2_b, dg_ffn = _rms_bwd(dh3, x2, r3 + tok, g_ffn, dx3, name="norm_ffn_bwd")

    do_att = _mm(dx2_b, W_co, "nt", name="d_cross_o", out_dtype=BF16)
    dW_co = _mm(o_att, dx2_b, "tn", name="dw_co", out_dtype=BF16)
    dq, dkv = _attn_bwd(qc, kv, do_att, T=T, D=D, M=M)
    dkv_b = dkv.astype(BF16)
    dW_cq = _mm(h2, dq, "tn", name="dw_cq", out_dtype=BF16)
    dh2 = _mm(dq, W_cq, "nt", name="d_cross_in", out_dtype=F32)
    dW_ckv = _mm(mem_n, dkv_b, "tn", name="dw_ckv", out_dtype=BF16)
    dmem_n = _mm(dkv_b, W_ckv, "nt", name="d_mem", out_dtype=F32)
    tok = reduce_start("cross", ["w_co", "w_cq", "w_ckv"],
                       [row_shards(dW_co), row_shards(dW_cq), col_shards(dW_ckv)], dmem_n)
    _, _, dg_mem = _rms_bwd(dmem_n, ms, rm, g_mem, None, name="norm_mem_bwd")
    dx1, dx1_b, dg_cross = _rms_bwd(dh2, x1, r2 + tok, g_cross, dx2, name="norm_cross_bwd")

    dmerged = _mm(dx1_b, W_out, "nt", name="d_merged", out_dtype=F32)
    dW_out = _mm(merged, dx1_b, "tn", name="dw_out", out_dtype=BF16)
    dy_gla, dy_pool, dgates = _merge_bwd(dmerged, y_gla, y_pool, proj, T=T, D=D, col_block=2)
    dW_br_gla = _mm(o_gla, dy_gla, "tn", name="dw_branch_gla", out_dtype=BF16)
    dW_br_pool = _mm(o_pool, dy_pool, "tn", name="dw_branch_pool", out_dtype=BF16)
    do_gla = _mm(dy_gla, W_branch, "nt", name="d_o_gla", out_dtype=F32, N=DV)
    do_pool = _mm(dy_pool, W_branch, "nt", name="d_o_pool", out_dtype=F32, N=PW, b_off=(DV, 0))
    tok = reduce_start("mix", ["w_out", "w_branch"],
                       [row_shards(dW_out), row_shards(jnp.concatenate([dW_br_gla, dW_br_pool], axis=0))],
                       do_pool)
    dp, dw_pool, dpool_scale = _pool_bwd(proj, W_pool, pool_scale + tok, do_pool, T=T, PW=PW, col_block=3)
    dqkvr, da_pad, dw2, db_a, dg_gla = _gla_bwd(proj, a_pad, W_a2p, b_a + tok, g_gla, o_raw, states, do_gla,
                                               T=T, DK=DK, DV=DV)
    dproj = jnp.concatenate([dqkvr, dp, dgates], axis=1)
    dh1 = _mm(dproj, W_main, "nn", name="d_mix_in_main", out_dtype=F32)
    dh1 = _mm(da_pad, W_a, "nn", name="d_mix_in_rank", out_dtype=F32, add=dh1)
    dW_main = _mm(dproj, h1, "tn", name="dw_in_main", out_dtype=BF16)
    dW_a = _mm(da_pad, h1, "tn", name="dw_in_rank", out_dtype=BF16)
    dx0, _, dg_mix = _rms_bwd(dh1, xs, r1, g_mix, dx1, name="norm_mix_bwd")

    grads = {}

    small_grads = [loss_part, dg_mix, db_a, dg_gla, dpool_scale, dg_cross, dg_mem, dg_ffn, dconv_b, dg_final,
                   dw2[:RANK], dw_pool, dconv_w]
    small_buf, offs = _pack(small_grads)
    small_sum = _all_reduce_small(small_buf)
    red = _unpack(small_sum, offs, [g.shape for g in small_grads])
    loss = red[0][0, 0]
    for k, g in zip(small_repl, red[1:10]):
        grads[k] = g.reshape(weights[k].shape)
    nb = DK // N_CHIPS
    grads["w_a2"] = lax.dynamic_slice_in_dim(red[10], chip * nb, nb, axis=1)[None]
    nb = GW // N_CHIPS
    grads["w_pool"] = lax.dynamic_slice_in_dim(red[11], chip * nb, nb, axis=1)[None]
    nb = F2 // N_CHIPS
    grads["conv_w"] = lax.dynamic_slice_in_dim(red[12], chip * nb, nb, axis=1)[None]

    delta, new_m, new_v = {}, {}, {}

    def whole(k, a):
        a = a.reshape(-1, a.shape[2])
        return (a.T if k == "w_in" else a)[None]

    dW_in = jnp.concatenate([dW_main[:OFF_A], dW_a[:RANK], dW_main[OFF_A:]], axis=0)
    reduce_start("in", ["w_in"], [row_shards(dW_in)], small_sum)
    after = in_flight[-1][4][0]

    for group, keys, s_sems, r_sems, sums, lands in in_flight:
        sums, from_chips = _scatter_wait(s_sems, r_sems, sums, lands, after, name=f"grad_scatter_wait_{group}")
        half_sums = [_add_chips(s, r, chip_idx, name=f"grad_add_chips_{k}") for k, s, r in zip(keys, sums, from_chips)]
        other_sums = _swap_halves(half_sums, name=f"grad_swap_halves_{group}")
        for k, mine, other in zip(keys, half_sums, other_sums):
            wmv = [halves(src[k][0].T if k == "w_in" else src[k][0]) for src in (weights, mom_m, mom_v)]
            res = _adamw_halves(*wmv, mine, other, c_idx, name=f"adamw_{k}")
            grads[k], delta[k], new_m[k], new_v[k] = (whole(k, a) for a in res)
            after = res[1]
    small = small_repl + small_sharded
    packs = [_pack([src[k] for k in small])[0] for src in (weights, grads, mom_m, mom_v)]
    _, offs = _pack([weights[k] for k in small])
    outs = _adamw(*packs, name="adamw_small")
    for res, o in zip((delta, new_m, new_v), outs):
        for k, a in zip(small, _unpack(o, offs, [weights[k].shape for k in small])):
            res[k] = a

    return (loss, dx0[None], *[grads[k] for k in order], *[delta[k] for k in order],
            *[new_m[k] for k in order], *[new_v[k] for k in order])
```

```python
import functools

import jax
import jax.numpy as jnp
from jax import lax
from jax.experimental import pallas as pl
from jax.experimental.pallas import tpu as pltpu

F32 = jnp.float32
BF16 = jnp.bfloat16
MESH = pl.DeviceIdType.MESH
HIGHEST = lax.Precision.HIGHEST

EPS = 1e-6
GLA_HEADS = 4
GLA_CHUNK = 64
GLA_GATE_NORM = 16.0
POOL_GROUPS = 4
CROSS_HEADS = 4
CONV_W = 3
N_CHIPS = 4
LANES = 128
SUBLANES = 8
VMEM_LIMIT = 56 << 20

ADAM_LR = 0.001
ADAM_B1 = 0.9
ADAM_B2 = 0.999
ADAM_EPS = 1e-08
ADAM_WD = 0.01
ADAM_STEP = 10

NN = (((1,), (0,)), ((), ()))
NT = (((1,), (1,)), ((), ()))
TN = (((0,), (0,)), ((), ()))


def _dot(a, b, dn=NN, precision=None):
    return lax.dot_general(a, b, dn, precision=precision, preferred_element_type=F32)


def _tile(n, pref, align=LANES):
    t = (min(pref, n) // align) * align
    while t >= align:
        if n % t == 0:
            return t
        t -= align
    return n


def _pcall(body, *, name, out_shape, grid=(), in_specs=None, out_specs=None, scratch_shapes=(),
           semantics=None, prefetch=0, aliases=None, split_copy=False):
    params = dict(vmem_limit_bytes=VMEM_LIMIT)
    if semantics is not None:
        params["dimension_semantics"] = semantics
    if split_copy:
        params["has_side_effects"] = pltpu.SideEffectType.DATAFLOW_SIDE_EFFECTING
    if prefetch:
        grid_spec = pltpu.PrefetchScalarGridSpec(
            num_scalar_prefetch=prefetch, grid=grid, in_specs=in_specs, out_specs=out_specs,
            scratch_shapes=scratch_shapes)
        return pl.pallas_call(body, name=name, out_shape=out_shape, grid_spec=grid_spec,
                              compiler_params=pltpu.CompilerParams(**params))
    kw = {}
    if aliases is not None:
        kw["input_output_aliases"] = aliases
    if in_specs is not None:
        kw["in_specs"] = in_specs
    if out_specs is not None:
        kw["out_specs"] = out_specs
    return pl.pallas_call(body, name=name, out_shape=out_shape, grid=grid,
                          scratch_shapes=scratch_shapes,
                          compiler_params=pltpu.CompilerParams(**params), **kw)


def _sigmoid(x):
    return 1.0 / (1.0 + jnp.exp(-x))


def _log_sigmoid(x):
    return jnp.minimum(x, 0.0) - jnp.log(1.0 + jnp.exp(-jnp.abs(x)))


def _mm(a, b, mode, *, name, out_dtype, M=None, N=None, K=None, a_off=(0, 0), b_off=(0, 0),
        add=None, b_blocked=False, out_blocks=0, tm=1536, tn=1536, tk=2048):
    if b_blocked:
        nb, R, Cb = b.shape
        b_rows, b_cols = R, nb * Cb
    else:
        b_rows, b_cols = b.shape
    if mode == "nn":
        M = M or a.shape[0]; K = K or a.shape[1]; N = N or b_cols
    elif mode == "nt":
        M = M or a.shape[0]; K = K or a.shape[1]; N = N or b_rows
    else:
        K = K or a.shape[0]; M = M or a.shape[1]; N = N or b_cols
    tm = _tile(M, tm, LANES if mode == "tn" else 16)
    tn = _tile(Cb if (b_blocked and mode != "nt") else (N // out_blocks if out_blocks else N), tn)
    tk = _tile(Cb if (b_blocked and mode == "nt") else K, tk)
    nk = K // tk
    dn = {"nn": NN, "nt": NT, "tn": TN}[mode]

    def off(o, t):
        assert o % t == 0, (name, o, t)
        return o // t

    if mode == "tn":
        ar, ac = off(a_off[0], tk), off(a_off[1], tm)
        a_spec = pl.BlockSpec((tk, tm), lambda i, j, k: (k + ar, i + ac))
    else:
        ar, ac = off(a_off[0], tm), off(a_off[1], tk)
        a_spec = pl.BlockSpec((tm, tk), lambda i, j, k: (i + ar, k + ac))
    if b_blocked and mode == "nt":
        per = Cb // tk
        b_spec = pl.BlockSpec((None, tn, tk), lambda i, j, k: (k // per, j, k % per))
    elif b_blocked:
        per = Cb // tn
        b_spec = pl.BlockSpec((None, tk, tn), lambda i, j, k: (j // per, k, j % per))
    elif mode == "nt":
        br, bc = off(b_off[0], tn), off(b_off[1], tk)
        b_spec = pl.BlockSpec((tn, tk), lambda i, j, k: (j + br, k + bc))
    else:
        br, bc = off(b_off[0], tk), off(b_off[1], tn)
        b_spec = pl.BlockSpec((tk, tn), lambda i, j, k: (k + br, j + bc))
    if out_blocks:
        per_o = N // out_blocks // tn
        o_spec = pl.BlockSpec((None, tm, tn), lambda i, j, k: (j // per_o, i, j % per_o))
        out_shape = jax.ShapeDtypeStruct((out_blocks, M, N // out_blocks), out_dtype)
    else:
        o_spec = pl.BlockSpec((tm, tn), lambda i, j, k: (i, j))
        out_shape = jax.ShapeDtypeStruct((M, N), out_dtype)
    in_specs = [a_spec, b_spec]
    args = [a, b]
    if add is not None:
        assert not out_blocks
        in_specs.append(o_spec)
        args.append(add)

    def finish(r, refs):
        if add is not None:
            r = r + refs[2][...]
        o_ref = refs[3] if add is not None else refs[2]
        o_ref[...] = r.astype(o_ref.dtype)

    def body_one(*refs):
        finish(_dot(refs[0][...].astype(BF16), refs[1][...].astype(BF16), dn), refs)

    def body_acc(*refs):
        acc_ref = refs[-1]
        k = pl.program_id(2)

        @pl.when(k == 0)
        def _():
            acc_ref[...] = jnp.zeros_like(acc_ref)

        acc_ref[...] += _dot(refs[0][...].astype(BF16), refs[1][...].astype(BF16), dn)

        @pl.when(k == nk - 1)
        def _():
            finish(acc_ref[...], refs)

    return _pcall(body_one if nk == 1 else body_acc, name=name, out_shape=out_shape,
                  grid=(M // tm, N // tn, nk), in_specs=in_specs, out_specs=o_spec,
                  scratch_shapes=[] if nk == 1 else [pltpu.VMEM((tm, tn), F32)],
                  semantics=("parallel", "parallel", "arbitrary"))(*args)


def _rms_fwd(x, g, *, name):
    T, D = x.shape
    tr = _tile(T, 128, 16)

    def body(x_ref, g_ref, h_ref, r_ref):
        xv = x_ref[...]
        r = lax.rsqrt(jnp.mean(xv * xv, axis=-1, keepdims=True) + EPS)
        h_ref[...] = (xv * r * g_ref[...]).astype(h_ref.dtype)
        r_ref[...] = r

    row = pl.BlockSpec((tr, D), lambda i: (i, 0))
    return _pcall(body, name=name,
                  out_shape=(jax.ShapeDtypeStruct((T, D), BF16), jax.ShapeDtypeStruct((T, 1), F32)),
                  grid=(T // tr,),
                  in_specs=[row, pl.BlockSpec((1, D), lambda i: (0, 0))],
                  out_specs=(row, pl.BlockSpec((tr, 1), lambda i: (i, 0))),
                  semantics=("parallel",))(x, g)


def _rms_bwd(dh, x, rstd, g, dres, *, name):
    T, D = x.shape
    tr = _tile(T, 128, 16)
    has_res = dres is not None

    def body(*refs):
        if has_res:
            dh_ref, x_ref, r_ref, g_ref, res_ref, dx_ref, dxb_ref, dg_ref = refs
        else:
            dh_ref, x_ref, r_ref, g_ref, dx_ref, dxb_ref, dg_ref = refs
        r = r_ref[...]
        xh = x_ref[...] * r
        dhv = dh_ref[...].astype(F32)
        dxh = dhv * g_ref[...]
        m = jnp.mean(dxh * xh, axis=-1, keepdims=True)
        dx = r * (dxh - xh * m)
        if has_res:
            dx = dx + res_ref[...]
        dx_ref[...] = dx
        dxb_ref[...] = dx.astype(BF16)

        @pl.when(pl.program_id(0) == 0)
        def _():
            dg_ref[...] = jnp.zeros_like(dg_ref)

        dg_ref[...] += jnp.sum(dhv * xh, axis=0, keepdims=True)

    row = pl.BlockSpec((tr, D), lambda i: (i, 0))
    vec = pl.BlockSpec((1, D), lambda i: (0, 0))
    in_specs = [row, row, pl.BlockSpec((tr, 1), lambda i: (i, 0)), vec]
    args = [dh, x, rstd, g]
    if has_res:
        in_specs.append(row)
        args.append(dres)
    return _pcall(body, name=name,
                  out_shape=(jax.ShapeDtypeStruct((T, D), F32), jax.ShapeDtypeStruct((T, D), BF16),
                             jax.ShapeDtypeStruct((1, D), F32)),
                  grid=(T // tr,), in_specs=in_specs, out_specs=(row, row, vec),
                  semantics=("arbitrary",))(*args)


def _loss_head(x3, g, tgt):
    T, D = x3.shape
    tr = _tile(T, 128, 16)

    def body(x_ref, g_ref, t_ref, loss_ref, dx_ref, dxb_ref, dg_ref):
        xv = x_ref[...]
        gv = g_ref[...]
        r = lax.rsqrt(jnp.mean(xv * xv, axis=-1, keepdims=True) + EPS)
        xh = xv * r
        err = xh * gv - t_ref[...]
        dy = err * (1.0 / D)
        dxh = dy * gv
        m = jnp.mean(dxh * xh, axis=-1, keepdims=True)
        dx = r * (dxh - xh * m)
        dx_ref[...] = dx
        dxb_ref[...] = dx.astype(BF16)

        @pl.when(pl.program_id(0) == 0)
        def _():
            dg_ref[...] = jnp.zeros_like(dg_ref)
            loss_ref[...] = jnp.zeros_like(loss_ref)

        dg_ref[...] += jnp.sum(dy * xh, axis=0, keepdims=True)
        part = 0.5 * jnp.sum(jnp.mean(err * err, axis=-1, keepdims=True), axis=0, keepdims=True)
        loss_ref[...] += jnp.broadcast_to(part, loss_ref.shape)

    row = pl.BlockSpec((tr, D), lambda i: (i, 0))
    vec = pl.BlockSpec((1, D), lambda i: (0, 0))
    return _pcall(body, name="loss_head",
                  out_shape=(jax.ShapeDtypeStruct((1, LANES), F32), jax.ShapeDtypeStruct((T, D), F32),
                             jax.ShapeDtypeStruct((T, D), BF16), jax.ShapeDtypeStruct((1, D), F32)),
                  grid=(T // tr,), in_specs=[row, vec, row],
                  out_specs=(pl.BlockSpec((1, LANES), lambda i: (0, 0)), row, row, vec),
                  semantics=("arbitrary",))(x3, g, tgt)


def _gla_chunk_terms(qk, a_ref, w2_ref, ba_ref, DK):
    C = qk.shape[0]
    gp = _dot(a_ref[...].astype(BF16), w2_ref[...]) + ba_ref[...]
    la = _log_sigmoid(gp) * (1.0 / GLA_GATE_NORM)
    row = lax.broadcasted_iota(jnp.int32, (C, C), 0)
    col = lax.broadcasted_iota(jnp.int32, (C, C), 1)
    causal = row >= col
    b = _dot(causal.astype(F32), la, precision=HIGHEST)
    return gp, b, causal


def _gla_fwd(proj, a_pad, w2, b_a, g_gla, *, T, DK, DV):
    assert 2 * DK == DV
    H = GLA_HEADS
    HK, HV = DK // H, DV // H
    C = GLA_CHUNK
    n = T // C
    RP = a_pad.shape[1]
    scale = HK ** -0.5

    def body(qk_ref, v_ref, r_ref, a_ref, w2_ref, ba_ref, gg_ref, og_ref, oraw_ref, st_ref, s_ref):
        @pl.when(pl.program_id(0) == 0)
        def _():
            s_ref[...] = jnp.zeros_like(s_ref)

        st_ref[...] = s_ref[...]
        qk = qk_ref[...]
        _, b, causal = _gla_chunk_terms(qk, a_ref, w2_ref, ba_ref, DK)
        for h in range(H):
            ks = slice(h * HK, (h + 1) * HK)
            vs = slice(h * HV, (h + 1) * HV)
            bh = b[:, ks]
            b_last = bh[C - 1:C, :]
            qt = qk[:, ks] * scale * jnp.exp(bh)
            kh = qk[:, DK + h * HK:DK + (h + 1) * HK]
            kt = kh * jnp.exp(-bh)
            khat = kh * jnp.exp(b_last - bh)
            a_mat = jnp.where(causal, _dot(qt, kt, NT, HIGHEST), 0.0)
            vh = v_ref[:, vs]
            s_t = s_ref[h]
            o = _dot(a_mat, vh, NN, HIGHEST) + _dot(qt, s_t, NT, HIGHEST)
            s_ref[h] = s_t * jnp.exp(b_last) + _dot(vh, khat, TN, HIGHEST)
            rs = lax.rsqrt(jnp.mean(o * o, axis=-1, keepdims=True) + EPS)
            rr = r_ref[:, vs]
            og = o * rs * gg_ref[:, vs] * (rr * _sigmoid(rr))
            oraw_ref[:, vs] = o
            og_ref[:, vs] = og.astype(BF16)

    blk = lambda j: pl.BlockSpec((C, DV), lambda i: (i, j))
    full = lambda s: pl.BlockSpec(s, lambda i: (0,) * len(s))
    return _pcall(
        body, name="gla_fwd",
        out_shape=(jax.ShapeDtypeStruct((T, DV), BF16), jax.ShapeDtypeStruct((T, DV), F32),
                   jax.ShapeDtypeStruct((n, H, HV, HK), F32)),
        grid=(n,),
        in_specs=[blk(0), blk(1), blk(2), pl.BlockSpec((C, RP), lambda i: (i, 0)),
                  full((RP, DK)), full((1, DK)), full((1, DV))],
        out_specs=(blk(0), blk(0), pl.BlockSpec((None, H, HV, HK), lambda i: (i, 0, 0, 0))),
        scratch_shapes=[pltpu.VMEM((H, HV, HK), F32)],
        semantics=("arbitrary",))(proj, proj, proj, a_pad, w2, b_a, g_gla)


def _gla_bwd(proj, a_pad, w2, b_a, g_gla, o_raw, states, do_gla, *, T, DK, DV):
    H = GLA_HEADS
    HK, HV = DK // H, DV // H
    C = GLA_CHUNK
    n = T // C
    RP = a_pad.shape[1]
    scale = HK ** -0.5

    def body(qk_ref, v_ref, r_ref, a_ref, w2_ref, ba_ref, gg_ref, oraw_ref, st_ref, dog_ref,
             dqkvr_ref, da_ref, dw2_ref, dba_ref, dgg_ref, ds_ref):
        @pl.when(pl.program_id(0) == 0)
        def _():
            ds_ref[...] = jnp.zeros_like(ds_ref)
            dw2_ref[...] = jnp.zeros_like(dw2_ref)
            dba_ref[...] = jnp.zeros_like(dba_ref)
            dgg_ref[...] = jnp.zeros_like(dgg_ref)

        qk = qk_ref[...]
        gp, b, causal = _gla_chunk_terms(qk, a_ref, w2_ref, ba_ref, DK)
        row = lax.broadcasted_iota(jnp.int32, (C, C), 0)
        col = lax.broadcasted_iota(jnp.int32, (C, C), 1)
        upper = (col >= row).astype(F32)
        dla_parts = []
        for h in range(H):
            ks = slice(h * HK, (h + 1) * HK)
            vs = slice(h * HV, (h + 1) * HV)
            bh = b[:, ks]
            b_last = bh[C - 1:C, :]
            eb = jnp.exp(bh)
            emb = jnp.exp(-bh)
            ehat = jnp.exp(b_last - bh)
            e_last = jnp.exp(b_last)
            qt = qk[:, ks] * scale * eb
            kh = qk[:, DK + h * HK:DK + (h + 1) * HK]
            kt = kh * emb
            khat = kh * ehat
            a_mat = jnp.where(causal, _dot(qt, kt, NT, HIGHEST), 0.0)
            vh = v_ref[:, vs]
            o = oraw_ref[:, vs]
            rs = lax.rsqrt(jnp.mean(o * o, axis=-1, keepdims=True) + EPS)
            on = o * rs
            gg = gg_ref[:, vs]
            rr = r_ref[:, vs]
            sg = _sigmoid(rr)
            d_out = dog_ref[:, vs]
            dr = d_out * (on * gg) * (sg * (1.0 + rr * (1.0 - sg)))
            d_og = d_out * (rr * sg)
            dgg_ref[:, vs] += jnp.sum(d_og * on, axis=0, keepdims=True)
            d_on = d_og * gg
            d_o = rs * (d_on - on * jnp.mean(d_on * on, axis=-1, keepdims=True))
            s_t = st_ref[h]
            ds_t = ds_ref[h]
            d_a = jnp.where(causal, _dot(d_o, vh, NT, HIGHEST), 0.0)
            dv = _dot(a_mat, d_o, TN, HIGHEST) + _dot(khat, ds_t, NT, HIGHEST)
            dqt = _dot(d_a, kt, NN, HIGHEST) + _dot(d_o, s_t, NN, HIGHEST)
            dkt = _dot(d_a, qt, TN, HIGHEST)
            dkhat = _dot(vh, ds_t, NN, HIGHEST)
            ds_ref[h] = ds_t * e_last + _dot(d_o, qt, TN, HIGHEST)
            dq = dqt * eb * scale
            dk = dkt * emb + dkhat * ehat
            db = dqt * qt - dkt * kt - dkhat * khat
            d_last = (jnp.sum(dkhat * khat, axis=0, keepdims=True)
                      + e_last * jnp.sum(ds_t * s_t, axis=0, keepdims=True))
            dla_parts.append(_dot(upper, db, NN, HIGHEST) + d_last)
            dqkvr_ref[:, ks] = dq.astype(BF16)
            dqkvr_ref[:, DK + h * HK:DK + (h + 1) * HK] = dk.astype(BF16)
            dqkvr_ref[:, DV + h * HV:DV + (h + 1) * HV] = dv.astype(BF16)
            dqkvr_ref[:, 2 * DV + h * HV:2 * DV + (h + 1) * HV] = dr.astype(BF16)
        dla = jnp.concatenate(dla_parts, axis=1)
        dgp = dla * (1.0 / GLA_GATE_NORM) * _sigmoid(-gp)
        dba_ref[...] += jnp.sum(dgp, axis=0, keepdims=True)
        dgp_b = dgp.astype(BF16)
        dw2_ref[...] += _dot(a_ref[...].astype(BF16), dgp_b, TN)
        da_ref[...] = _dot(dgp_b, w2_ref[...], NT).astype(BF16)

    rev = lambda j: pl.BlockSpec((C, DV), lambda i: (n - 1 - i, j))
    full = lambda s: pl.BlockSpec(s, lambda i: (0,) * len(s))
    return _pcall(
        body, name="gla_bwd",
        out_shape=(jax.ShapeDtypeStruct((T, 3 * DV), BF16), jax.ShapeDtypeStruct((T, RP), BF16),
                   jax.ShapeDtypeStruct((RP, DK), F32), jax.ShapeDtypeStruct((1, DK), F32),
                   jax.ShapeDtypeStruct((1, DV), F32)),
        grid=(n,),
        in_specs=[rev(0), rev(1), rev(2), pl.BlockSpec((C, RP), lambda i: (n - 1 - i, 0)),
                  full((RP, DK)), full((1, DK)), full((1, DV)), rev(0),
                  pl.BlockSpec((None, H, HV, HK), lambda i: (n - 1 - i, 0, 0, 0)), rev(0)],
        out_specs=(pl.BlockSpec((C, 3 * DV), lambda i: (n - 1 - i, 0)),
                   pl.BlockSpec((C, RP), lambda i: (n - 1 - i, 0)),
                   full((RP, DK)), full((1, DK)), full((1, DV))),
        scratch_shapes=[pltpu.VMEM((H, HV, HK), F32)],
        semantics=("arbitrary",))(proj, proj, proj, a_pad, w2, b_a, g_gla, o_raw, states, do_gla)


def _pool_windows(p, g, T):
    t = lax.broadcasted_iota(jnp.int32, (T, 1), 0)
    s = p
    for lvl in range(POOL_GROUPS):
        sh = 1 << lvl
        nxt = s + jnp.where(t >= sh, pltpu.roll(s, sh, 0), 0.0)
        s = jnp.where(lvl <= g, nxt, s)
    win = jnp.left_shift(2, g)
    inv = 1.0 / jnp.minimum(t + 1, win).astype(F32)
    return s * inv - p, inv


def _pool_fwd(proj, w_pool, scale, *, T, PW, col_block):
    GW = PW // POOL_GROUPS
    per = PW // GW

    def body(p_ref, w_ref, s_ref, o_ref):
        g = pl.program_id(0)
        pooled, _ = _pool_windows(p_ref[...], g, T)
        mixed = _dot(pooled.astype(BF16), w_ref[...])
        o_ref[...] = (mixed * s_ref[...]).astype(BF16)

    return _pcall(body, name="pool_fwd", out_shape=jax.ShapeDtypeStruct((T, PW), BF16),
                  grid=(POOL_GROUPS,),
                  in_specs=[pl.BlockSpec((T, GW), lambda g: (0, col_block * per + g)),
                            pl.BlockSpec((None, GW, GW), lambda g: (g, 0, 0)),
                            pl.BlockSpec((1, GW), lambda g: (0, g))],
                  out_specs=pl.BlockSpec((T, GW), lambda g: (0, g)),
                  semantics=("parallel",))(proj, w_pool, scale)


def _pool_bwd(proj, w_pool, scale, do_pool, *, T, PW, col_block):
    GW = PW // POOL_GROUPS
    per = PW // GW

    def body(p_ref, w_ref, s_ref, do_ref, dp_ref, dw_ref, dsc_ref):
        g = pl.program_id(0)
        pooled, inv = _pool_windows(p_ref[...], g, T)
        pooled_b = pooled.astype(BF16)
        w = w_ref[...]
        mixed = _dot(pooled_b, w)
        d_out = do_ref[...]
        dsc_ref[...] = jnp.sum(d_out * mixed, axis=0, keepdims=True)
        dmixed = (d_out * s_ref[...]).astype(BF16)
        dw_ref[...] = _dot(pooled_b, dmixed, TN)
        dpooled = _dot(dmixed, w, NT)
        t = lax.broadcasted_iota(jnp.int32, (T, 1), 0)
        s = dpooled * inv
        for lvl in range(POOL_GROUPS):
            sh = 1 << lvl
            nxt = s + jnp.where(t < T - sh, pltpu.roll(s, T - sh, 0), 0.0)
            s = jnp.where(lvl <= g, nxt, s)
        dp_ref[...] = (s - dpooled).astype(BF16)

    return _pcall(body, name="pool_bwd",
                  out_shape=(jax.ShapeDtypeStruct((T, PW), BF16),
                             jax.ShapeDtypeStruct((POOL_GROUPS, GW, GW), F32),
                             jax.ShapeDtypeStruct((1, PW), F32)),
                  grid=(POOL_GROUPS,),
                  in_specs=[pl.BlockSpec((T, GW), lambda g: (0, col_block * per + g)),
                            pl.BlockSpec((None, GW, GW), lambda g: (g, 0, 0)),
                            pl.BlockSpec((1, GW), lambda g: (0, g)),
                            pl.BlockSpec((T, GW), lambda g: (0, g))],
                  out_specs=(pl.BlockSpec((T, GW), lambda g: (0, g)),
                             pl.BlockSpec((None, GW, GW), lambda g: (g, 0, 0)),
                             pl.BlockSpec((1, GW), lambda g: (0, g))),
                  semantics=("parallel",))(proj, w_pool, scale, do_pool)


def _merge_fwd(y_gla, y_pool, proj, *, T, D, col_block):
    tr = _tile(T, 128, 16)

    def body(yg_ref, yp_ref, g1_ref, g2_ref, o_ref):
        o_ref[...] = (_sigmoid(g1_ref[...]) * yg_ref[...]
                      + _sigmoid(g2_ref[...]) * yp_ref[...]).astype(BF16)

    row = pl.BlockSpec((tr, D), lambda i: (i, 0))
    return _pcall(body, name="merge_fwd", out_shape=jax.ShapeDtypeStruct((T, D), BF16),
                  grid=(T // tr,),
                  in_specs=[row, row, pl.BlockSpec((tr, D), lambda i: (i, col_block)),
                            pl.BlockSpec((tr, D), lambda i: (i, col_block + 1))],
                  out_specs=row, semantics=("parallel",))(y_gla, y_pool, proj, proj)


def _merge_bwd(dmerged, y_gla, y_pool, proj, *, T, D, col_block):
    tr = _tile(T, 128, 16)

    def body(dm_ref, yg_ref, yp_ref, g1_ref, g2_ref, dyg_ref, dyp_ref, dg_ref):
        dm = dm_ref[...]
        s1 = _sigmoid(g1_ref[...])
        s2 = _sigmoid(g2_ref[...])
        dyg_ref[...] = (dm * s1).astype(BF16)
        dyp_ref[...] = (dm * s2).astype(BF16)
        dg_ref[:, :D] = (dm * yg_ref[...] * s1 * (1.0 - s1)).astype(BF16)
        dg_ref[:, D:] = (dm * yp_ref[...] * s2 * (1.0 - s2)).astype(BF16)

    row = pl.BlockSpec((tr, D), lambda i: (i, 0))
    return _pcall(body, name="merge_bwd",
                  out_shape=(jax.ShapeDtypeStruct((T, D), BF16), jax.ShapeDtypeStruct((T, D), BF16),
                             jax.ShapeDtypeStruct((T, 2 * D), BF16)),
                  grid=(T // tr,),
                  in_specs=[row, row, row, pl.BlockSpec((tr, D), lambda i: (i, col_block)),
                            pl.BlockSpec((tr, D), lambda i: (i, col_block + 1))],
                  out_specs=(row, row, pl.BlockSpec((tr, 2 * D), lambda i: (i, 0))),
                  semantics=("parallel",))(dmerged, y_gla, y_pool, proj, proj)


def _attn_fwd(q, kv, *, T, D, M):
    H = CROSS_HEADS
    HD = D // H
    tq = _tile(T, 512, 16)
    scale = HD ** -0.5

    def body(q_ref, kv_ref, o_ref):
        for h in range(H):
            hs = slice(h * HD, (h + 1) * HD)
            s = _dot(q_ref[:, hs], kv_ref[:, hs], NT) * scale
            e = jnp.exp(s - jnp.max(s, axis=-1, keepdims=True))
            p = e / jnp.sum(e, axis=-1, keepdims=True)
            o_ref[:, hs] = _dot(p.astype(BF16), kv_ref[:, D + h * HD:D + (h + 1) * HD]).astype(BF16)

    row = pl.BlockSpec((tq, D), lambda i: (i, 0))
    return _pcall(body, name="attn_fwd", out_shape=jax.ShapeDtypeStruct((T, D), BF16),
                  grid=(T // tq,), in_specs=[row, pl.BlockSpec((M, 2 * D), lambda i: (0, 0))],
                  out_specs=row, semantics=("parallel",))(q, kv)


def _attn_bwd(q, kv, do, *, T, D, M):
    H = CROSS_HEADS
    HD = D // H
    tq = _tile(T, 512, 16)
    scale = HD ** -0.5

    def body(q_ref, kv_ref, do_ref, dq_ref, dkv_ref):
        @pl.when(pl.program_id(0) == 0)
        def _():
            dkv_ref[...] = jnp.zeros_like(dkv_ref)

        for h in range(H):
            hs = slice(h * HD, (h + 1) * HD)
            vs = slice(D + h * HD, D + (h + 1) * HD)
            qh = q_ref[:, hs]
            kh = kv_ref[:, hs]
            s = _dot(qh, kh, NT) * scale
            e = jnp.exp(s - jnp.max(s, axis=-1, keepdims=True))
            p = e / jnp.sum(e, axis=-1, keepdims=True)
            p_b = p.astype(BF16)
            d_o = do_ref[:, hs]
            dkv_ref[:, vs] += _dot(p_b, d_o, TN)
            dp = _dot(d_o, kv_ref[:, vs], NT)
            ds = (p * (dp - jnp.sum(dp * p, axis=-1, keepdims=True)) * scale).astype(BF16)
            dq_ref[:, hs] = _dot(ds, kh).astype(BF16)
            dkv_ref[:, hs] += _dot(ds, qh, TN)

    row = pl.BlockSpec((tq, D), lambda i: (i, 0))
    full = pl.BlockSpec((M, 2 * D), lambda i: (0, 0))
    return _pcall(body, name="attn_bwd",
                  out_shape=(jax.ShapeDtypeStruct((T, D), BF16), jax.ShapeDtypeStruct((M, 2 * D), F32)),
                  grid=(T // tq,), in_specs=[row, full, row], out_specs=(row, full),
                  semantics=("arbitrary",))(q, kv, do)


def _shift_down(x, halo, s, t):
    out = pltpu.roll(x, s, 0)
    for j in range(s):
        out = jnp.where(t == j, halo[SUBLANES - s + j:SUBLANES - s + j + 1, :], out)
    return out


def _shift_up(x, halo, s, t, rows):
    out = pltpu.roll(x, rows - s, 0)
    for j in range(s):
        out = jnp.where(t == rows - s + j, halo[j:j + 1, :], out)
    return out


def _conv_tiles(T):
    tt = _tile(T, 128, SUBLANES)
    return tt, tt // SUBLANES, T // SUBLANES


def _conv_fwd(u0, conv_w, conv_b, *, T, F):
    tt, hb, _ = _conv_tiles(T)
    cw = _tile(F, 512)

    def body(u_ref, prev_ref, w_ref, b_ref, f_ref):
        i = pl.program_id(0)
        t = lax.broadcasted_iota(jnp.int32, (tt, 1), 0)

        def conv(cs):
            x = u_ref[:, cs]
            halo = jnp.where(i > 0, prev_ref[:, cs], 0.0)
            return (w_ref[2:3, cs] * x + w_ref[1:2, cs] * _shift_down(x, halo, 1, t)
                    + w_ref[0:1, cs] * _shift_down(x, halo, 2, t) + b_ref[:, cs])

        for j in range(F // cw):
            gate = conv(slice(j * cw, (j + 1) * cw))
            val = conv(slice(F + j * cw, F + (j + 1) * cw))
            f_ref[:, j * cw:(j + 1) * cw] = (gate * _sigmoid(gate) * val).astype(BF16)

    return _pcall(body, name="conv_fwd", out_shape=jax.ShapeDtypeStruct((T, F), BF16),
                  grid=(T // tt,),
                  in_specs=[pl.BlockSpec((tt, 2 * F), lambda i: (i, 0)),
                            pl.BlockSpec((SUBLANES, 2 * F), lambda i: (jnp.maximum(i * hb - 1, 0), 0)),
                            pl.BlockSpec((CONV_W, 2 * F), lambda i: (0, 0)),
                            pl.BlockSpec((1, 2 * F), lambda i: (0, 0))],
                  out_specs=pl.BlockSpec((tt, F), lambda i: (i, 0)),
                  semantics=("parallel",))(u0, u0, conv_w, conv_b)


def _conv_bwd(u0, conv_w, conv_b, df, *, T, F):
    tt, hb, nb = _conv_tiles(T)
    nt = T // tt
    cw = _tile(F, 512)

    def body(u_ref, prev_ref, next_ref, df_ref, dfn_ref, w_ref, b_ref, du0_ref, dw_ref, db_ref):
        i = pl.program_id(0)
        t = lax.broadcasted_iota(jnp.int32, (tt, 1), 0)
        t8 = lax.broadcasted_iota(jnp.int32, (SUBLANES, 1), 0)

        @pl.when(i == 0)
        def _():
            dw_ref[...] = jnp.zeros_like(dw_ref)
            db_ref[...] = jnp.zeros_like(db_ref)

        def conv(cs):
            x = u_ref[:, cs]
            halo = jnp.where(i > 0, prev_ref[:, cs], 0.0)
            x1 = _shift_down(x, halo, 1, t)
            x2 = _shift_down(x, halo, 2, t)
            u = w_ref[2:3, cs] * x + w_ref[1:2, cs] * x1 + w_ref[0:1, cs] * x2 + b_ref[:, cs]
            xn = next_ref[:, cs]
            tail = x[tt - SUBLANES:, :]
            un = (w_ref[2:3, cs] * xn + w_ref[1:2, cs] * _shift_down(xn, tail, 1, t8)
                  + w_ref[0:1, cs] * _shift_down(xn, tail, 2, t8) + b_ref[:, cs])
            return u, un, (x, x1, x2)

        def glu_grad(gate, val, dff):
            sg = _sigmoid(gate)
            return dff * val * (sg * (1.0 + gate * (1.0 - sg))), dff * (gate * sg)

        def finish(cs, du, dun, xs):
            du0 = (w_ref[2:3, cs] * du + w_ref[1:2, cs] * _shift_up(du, dun, 1, t, tt)
                   + w_ref[0:1, cs] * _shift_up(du, dun, 2, t, tt))
            du0_ref[:, cs] = du0.astype(BF16)
            db_ref[:, cs] += jnp.sum(du, axis=0, keepdims=True)
            dw_ref[2:3, cs] += jnp.sum(du * xs[0], axis=0, keepdims=True)
            dw_ref[1:2, cs] += jnp.sum(du * xs[1], axis=0, keepdims=True)
            dw_ref[0:1, cs] += jnp.sum(du * xs[2], axis=0, keepdims=True)

        for j in range(F // cw):
            fs = slice(j * cw, (j + 1) * cw)
            gs, vs = fs, slice(F + j * cw, F + (j + 1) * cw)
            ug, ung, xg = conv(gs)
            uv, unv, xv = conv(vs)
            dug, duv = glu_grad(ug, uv, df_ref[:, fs].astype(F32))
            dung, dunv = glu_grad(ung, unv, dfn_ref[0:SUBLANES, fs].astype(F32))
            dung = jnp.where(i < nt - 1, dung, 0.0)
            dunv = jnp.where(i < nt - 1, dunv, 0.0)
            finish(gs, dug, dung, xg)
            finish(vs, duv, dunv, xv)

    wide = lambda rows, fn: pl.BlockSpec((rows, 2 * F), fn)
    nxt = lambda i: (jnp.minimum((i + 1) * hb, nb - 1), 0)
    return _pcall(body, name="conv_bwd",
                  out_shape=(jax.ShapeDtypeStruct((T, 2 * F), BF16),
                             jax.ShapeDtypeStruct((CONV_W, 2 * F), F32),
                             jax.ShapeDtypeStruct((1, 2 * F), F32)),
                  grid=(nt,),
                  in_specs=[wide(tt, lambda i: (i, 0)),
                            wide(SUBLANES, lambda i: (jnp.maximum(i * hb - 1, 0), 0)),
                            wide(SUBLANES, nxt),
                            pl.BlockSpec((tt, F), lambda i: (i, 0)),
                            pl.BlockSpec((2 * SUBLANES, F),
                                         lambda i: (jnp.minimum((i + 1) * (hb // 2), nb // 2 - 1), 0)),
                            wide(CONV_W, lambda i: (0, 0)), wide(1, lambda i: (0, 0))],
                  out_specs=(wide(tt, lambda i: (i, 0)), wide(CONV_W, lambda i: (0, 0)),
                             wide(1, lambda i: (0, 0))),
                  semantics=("arbitrary",))(u0, u0, u0, df, df, conv_w, conv_b)


def _adamw(w, g, m, v, *, name):
    R, C = w.shape
    tr = _tile(R, max(SUBLANES, (1 << 19) // max(C, 1) // SUBLANES * SUBLANES), SUBLANES)
    c1 = 1.0 / (1.0 - ADAM_B1 ** ADAM_STEP)
    c2 = 1.0 / (1.0 - ADAM_B2 ** ADAM_STEP)

    def body(w_ref, g_ref, m_ref, v_ref, d_ref, mo_ref, vo_ref):
        gv = g_ref[...]
        mn = ADAM_B1 * m_ref[...] + (1.0 - ADAM_B1) * gv
        vn = ADAM_B2 * v_ref[...] + (1.0 - ADAM_B2) * (gv * gv)
        d_ref[...] = -ADAM_LR * ((mn * c1) / (jnp.sqrt(vn * c2) + ADAM_EPS) + ADAM_WD * w_ref[...])
        mo_ref[...] = mn
        vo_ref[...] = vn

    blk = pl.BlockSpec((tr, C), lambda i: (i, 0))
    shp = jax.ShapeDtypeStruct((R, C), F32)
    return _pcall(body, name=name, out_shape=(shp, shp, shp), grid=(R // tr,),
                  in_specs=[blk] * 4, out_specs=(blk,) * 3, semantics=("parallel",))(w, g, m, v)


def _blk(h, C, elems=1 << 19, align=16):
    th = _tile(h, max(align, elems // C // align * align), align)
    if th < h or h * C <= 2 * elems:
        return th, C
    return h, _tile(C, max(LANES, elems // h // LANES * LANES))


def _adamw_halves(w, m, v, g_mine, g_other, c_idx, *, name):
    _, h, C = w.shape
    th, tc = _blk(h, C, align=SUBLANES)
    c1 = 1.0 / (1.0 - ADAM_B1 ** ADAM_STEP)
    c2 = 1.0 / (1.0 - ADAM_B2 ** ADAM_STEP)

    def body(c_ref, w_ref, m_ref, v_ref, gm_ref, go_ref, g_ref, d_ref, mo_ref, vo_ref):
        gv = jnp.where(pl.program_id(0) == c_ref[0], gm_ref[...], go_ref[...])
        mn = ADAM_B1 * m_ref[...] + (1.0 - ADAM_B1) * gv
        vn = ADAM_B2 * v_ref[...] + (1.0 - ADAM_B2) * (gv * gv)
        d_ref[...] = -ADAM_LR * ((mn * c1) / (jnp.sqrt(vn * c2) + ADAM_EPS) + ADAM_WD * w_ref[...])
        g_ref[...] = gv
        mo_ref[...] = mn
        vo_ref[...] = vn

    blk = pl.BlockSpec((None, th, tc), lambda s, i, j, c: (s, i, j))

    def pick(mine):
        def index(s, i, j, c):
            use = (s == c[0]) if mine else (s != c[0])
            return jnp.where(use, i, 0), jnp.where(use, j, 0)
        return pl.BlockSpec((th, tc), index)

    shp = jax.ShapeDtypeStruct((2, h, C), F32)
    return _pcall(body, name=name, out_shape=(shp,) * 4, grid=(2, h // th, C // tc), prefetch=1,
                  in_specs=[blk, blk, blk, pick(True), pick(False)], out_specs=(blk,) * 4,
                  semantics=("parallel", "parallel", "parallel"))(c_idx, w, m, v, g_mine, g_other)


def _mesh_pos():
    x, y, c = lax.axis_index("x"), lax.axis_index("y"), lax.axis_index("c")
    others = [(1 - x, y), (x, 1 - y), (1 - x, 1 - y)]
    return x, y, c, others


def _gather_copies(shards, lands, send_sems, recv_sems):
    x, y, c, others = _mesh_pos()
    me = 2 * x + y
    return [pltpu.make_async_remote_copy(
        src_ref=shards[a].at[c], dst_ref=lands[a].at[me, c],
        send_sem=send_sems.at[3 * a + j], recv_sem=recv_sems.at[3 * a + j],
        device_id=(*chip, c), device_id_type=MESH)
        for a in range(len(shards)) for j, chip in enumerate(others)]


def _gather_start(shards, after, *, name):
    n = len(shards)
    lands = [lax.empty((N_CHIPS, *s.shape), s.dtype) for s in shards]
    HBM = pl.BlockSpec(memory_space=pltpu.HBM)
    SEM = pl.BlockSpec(memory_space=pltpu.SEMAPHORE)

    def body(*refs):
        srcs, zones = refs[:n], refs[n:2 * n]
        send_sems, recv_sems = refs[2 * n + 1], refs[2 * n + 2]
        token = refs[-1]
        for cp in _gather_copies(srcs, zones, send_sems, recv_sems):
            cp.start()
        token[...] = jnp.zeros_like(token)

    hbm = lambda a: pltpu.HBM(a.shape, a.dtype)
    res = _pcall(body, name=name,
                 out_shape=(pltpu.SemaphoreType.DMA((3 * n,)), pltpu.SemaphoreType.DMA((3 * n,)),
                            *[hbm(a) for a in shards], *[hbm(a) for a in lands],
                            jax.ShapeDtypeStruct((SUBLANES, LANES), F32)),
                 in_specs=[*[HBM] * (2 * n), pl.BlockSpec(memory_space=pl.ANY)],
                 out_specs=(SEM, SEM, *[HBM] * (2 * n), pl.BlockSpec(memory_space=pltpu.VMEM)),
                 aliases={i: 2 + i for i in range(2 * n)}, split_copy=True)(
        *[pltpu.with_memory_space_constraint(a, pltpu.HBM) for a in shards + lands], after)
    return res[0], res[1], list(res[2:2 + n]), list(res[2 + n:2 + 2 * n]), res[-1]


def _gather_wait(send_sems, recv_sems, shards, lands, after, *, name):
    n = len(shards)
    HBM = pl.BlockSpec(memory_space=pltpu.HBM)
    SEM = pl.BlockSpec(memory_space=pltpu.SEMAPHORE)

    def body(*refs):
        srcs, zones = refs[:n], refs[n:2 * n]
        s_sems, r_sems = refs[2 * n], refs[2 * n + 1]
        for cp in _gather_copies(srcs, zones, s_sems, r_sems):
            cp.wait_send()
            cp.wait_recv()

    hbm = lambda a: pltpu.HBM(a.shape, a.dtype)
    res = _pcall(body, name=name, out_shape=(*[hbm(a) for a in shards], *[hbm(a) for a in lands]),
                 in_specs=[*[HBM] * (2 * n), SEM, SEM, pl.BlockSpec(memory_space=pl.ANY)],
                 out_specs=tuple([HBM] * (2 * n)), aliases={i: i for i in range(2 * n)},
                 split_copy=True)(*shards, *lands, send_sems, recv_sems, after)
    return list(res[:n]), list(res[n:])


def _gather_pass_on(shards, lands, *, name):
    n = len(shards)
    ANY = pl.BlockSpec(memory_space=pl.ANY)
    PER = 4

    def body(*refs):
        srcs, zones, outs = refs[:n], refs[n:2 * n], refs[2 * n:3 * n]
        send_sems, recv_sems = refs[3 * n:]
        x, y, c, others = _mesh_pos()
        me = 2 * x + y
        sibling = (x, y, 1 - c)

        def copy(a, k, src, dst):
            return pltpu.make_async_remote_copy(
                src_ref=src, dst_ref=dst, send_sem=send_sems.at[PER * a + k],
                recv_sem=recv_sems.at[PER * a + k], device_id=sibling, device_id_type=MESH)

        copies = []
        for a in range(n):
            for j, chip in enumerate(others):
                idx = 2 * chip[0] + chip[1]
                copies.append(copy(a, j, zones[a].at[idx, c], outs[a].at[idx, c]))
            copies.append(copy(a, 3, srcs[a], outs[a].at[me]))
        for cp in copies:
            cp.start()
        for a in range(n):
            for j, chip in enumerate(others):
                idx = 2 * chip[0] + chip[1]
                copy(a, j, zones[a].at[idx, 1 - c], outs[a].at[idx, 1 - c]).wait_recv()
            copy(a, 3, srcs[a], outs[a].at[me]).wait_recv()
        for cp in copies:
            cp.wait_send()

    return _pcall(body, name=name,
                  out_shape=[jax.ShapeDtypeStruct(z.shape, z.dtype) for z in lands],
                  in_specs=[ANY] * (2 * n), out_specs=[ANY] * n,
                  aliases={n + i: i for i in range(n)},
                  scratch_shapes=[pltpu.SemaphoreType.DMA((PER * n,)),
                                  pltpu.SemaphoreType.DMA((PER * n,))])(*shards, *lands)


def _exchange_halves(grads, after, *, name):
    n = len(grads)
    ANY = pl.BlockSpec(memory_space=pl.ANY)

    def body(*refs):
        ins, outs = refs[:n], refs[n + 1:2 * n + 1]
        send_sems, recv_sems = refs[2 * n + 1:]
        x, y, c, _ = _mesh_pos()
        copies = [pltpu.make_async_remote_copy(
            src_ref=ins[a].at[:, 1 - c], dst_ref=outs[a], send_sem=send_sems.at[a],
            recv_sem=recv_sems.at[a], device_id=(x, y, 1 - c), device_id_type=MESH) for a in range(n)]
        for cp in copies:
            cp.start()
        for cp in copies:
            cp.wait()

    return _pcall(body, name=name,
                  out_shape=[jax.ShapeDtypeStruct((g.shape[0], *g.shape[2:]), g.dtype) for g in grads],
                  in_specs=[ANY] * (n + 1), out_specs=[ANY] * n,
                  scratch_shapes=[pltpu.SemaphoreType.DMA((n,)), pltpu.SemaphoreType.DMA((n,))])(*grads, after)


def _add_halves(grad, recv, c_idx, *, name):
    S, _, h, C = grad.shape
    th, tc = _blk(h, C)

    def body(c_ref, g_ref, r_ref, o_ref):
        o_ref[...] = (g_ref[...].astype(F32) + r_ref[...].astype(F32)).astype(o_ref.dtype)

    return _pcall(body, name=name, out_shape=jax.ShapeDtypeStruct((S, h, C), grad.dtype),
                  grid=(S, h // th, C // tc), prefetch=1,
                  in_specs=[pl.BlockSpec((None, None, th, tc), lambda s, i, j, c: (s, c[0], i, j)),
                            pl.BlockSpec((None, th, tc), lambda s, i, j, c: (s, i, j))],
                  out_specs=pl.BlockSpec((None, th, tc), lambda s, i, j, c: (s, i, j)),
                  semantics=("parallel", "parallel", "parallel"))(c_idx, grad, recv)


def _scatter_copies(srcs, lands, send_sems, recv_sems):
    x, y, c, others = _mesh_pos()
    return [pltpu.make_async_remote_copy(
        src_ref=srcs[a].at[2 * chip[0] + chip[1]], dst_ref=lands[a].at[j],
        send_sem=send_sems.at[3 * a + j], recv_sem=recv_sems.at[3 * a + j],
        device_id=(*chip, c), device_id_type=MESH)
        for a in range(len(srcs)) for j, chip in enumerate(others)]


def _scatter_start(sums, *, name):
    n = len(sums)
    lands = [lax.empty((3, *s.shape[1:]), s.dtype) for s in sums]
    HBM = pl.BlockSpec(memory_space=pltpu.HBM)
    SEM = pl.BlockSpec(memory_space=pltpu.SEMAPHORE)

    def body(*refs):
        srcs, zones = refs[:n], refs[n:2 * n]
        send_sems, recv_sems = refs[2 * n], refs[2 * n + 1]
        token = refs[-1]
        for cp in _scatter_copies(srcs, zones, send_sems, recv_sems):
            cp.start()
        token[...] = jnp.zeros_like(token)

    hbm = lambda a: pltpu.HBM(a.shape, a.dtype)
    res = _pcall(body, name=name,
                 out_shape=(pltpu.SemaphoreType.DMA((3 * n,)), pltpu.SemaphoreType.DMA((3 * n,)),
                            *[hbm(a) for a in sums], *[hbm(a) for a in lands],
                            jax.ShapeDtypeStruct((SUBLANES, LANES), F32)),
                 in_specs=[HBM] * (2 * n),
                 out_specs=(SEM, SEM, *[HBM] * (2 * n), pl.BlockSpec(memory_space=pltpu.VMEM)),
                 aliases={i: 2 + i for i in range(2 * n)}, split_copy=True)(
        *[pltpu.with_memory_space_constraint(a, pltpu.HBM) for a in sums + lands])
    return res[0], res[1], list(res[2:2 + n]), list(res[2 + n:2 + 2 * n]), res[-1]


def _scatter_wait(send_sems, recv_sems, sums, lands, after, *, name):
    n = len(sums)
    HBM = pl.BlockSpec(memory_space=pltpu.HBM)
    SEM = pl.BlockSpec(memory_space=pltpu.SEMAPHORE)

    def body(*refs):
        srcs, zones = refs[:n], refs[n:2 * n]
        s_sems, r_sems = refs[2 * n], refs[2 * n + 1]
        for cp in _scatter_copies(srcs, zones, s_sems, r_sems):
            cp.wait_send()
            cp.wait_recv()

    hbm = lambda a: pltpu.HBM(a.shape, a.dtype)
    res = _pcall(body, name=name, out_shape=(*[hbm(a) for a in sums], *[hbm(a) for a in lands]),
                 in_specs=[*[HBM] * (2 * n), SEM, SEM, pl.BlockSpec(memory_space=pl.ANY)],
                 out_specs=tuple([HBM] * (2 * n)), aliases={i: i for i in range(2 * n)},
                 split_copy=True)(*sums, *lands, send_sems, recv_sems, after)
    return list(res[:n]), list(res[n:])


def _add_chips(sums, recv, chip_idx, *, name):
    _, h, C = sums.shape
    th, tc = _blk(h, C)

    def body(k_ref, s_ref, r_ref, o_ref):
        acc = s_ref[...].astype(F32) + r_ref[0].astype(F32)
        acc = acc + r_ref[1].astype(F32)
        o_ref[...] = acc + r_ref[2].astype(F32)

    return _pcall(body, name=name, out_shape=jax.ShapeDtypeStruct((h, C), F32),
                  grid=(h // th, C // tc), prefetch=1,
                  in_specs=[pl.BlockSpec((None, th, tc), lambda i, j, k: (k[0], i, j)),
                            pl.BlockSpec((3, th, tc), lambda i, j, k: (0, i, j))],
                  out_specs=pl.BlockSpec((th, tc), lambda i, j, k: (i, j)),
                  semantics=("parallel", "parallel"))(chip_idx, sums, recv)


def _swap_halves(halves, *, name):
    n = len(halves)
    ANY = pl.BlockSpec(memory_space=pl.ANY)

    def body(*refs):
        ins, outs = refs[:n], refs[n:2 * n]
        send_sems, recv_sems = refs[2 * n:]
        x, y, c, _ = _mesh_pos()
        copies = [pltpu.make_async_remote_copy(
            src_ref=ins[a], dst_ref=outs[a], send_sem=send_sems.at[a], recv_sem=recv_sems.at[a],
            device_id=(x, y, 1 - c), device_id_type=MESH) for a in range(n)]
        for cp in copies:
            cp.start()
        for cp in copies:
            cp.wait()

    return _pcall(body, name=name,
                  out_shape=[jax.ShapeDtypeStruct(s.shape, s.dtype) for s in halves],
                  in_specs=[ANY] * n, out_specs=[ANY] * n,
                  scratch_shapes=[pltpu.SemaphoreType.DMA((n,)), pltpu.SemaphoreType.DMA((n,))])(*halves)


def _all_reduce_small(buf):
    R, L = buf.shape
    NDEV = 8

    def body(x_ref, sum_ref, all_ref, send_sems, recv_sems, local_sem):
        x, y, c, others = _mesh_pos()
        me, sibling = (x, y, c), (x, y, 1 - c)

        def slot(px, py, pc):
            return all_ref.at[4 * px + 2 * py + pc]

        def copy(k, block, to, src=None):
            return pltpu.make_async_remote_copy(
                src_ref=slot(*block) if src is None else src, dst_ref=slot(*block),
                send_sem=send_sems.at[k], recv_sem=recv_sems.at[k], device_id=to, device_id_type=MESH)

        mine = pltpu.make_async_copy(x_ref, slot(*me), local_sem)
        mine.start()
        first = [copy(0, me, sibling, src=x_ref)]
        first += [copy(1 + j, me, (*chip, c), src=x_ref) for j, chip in enumerate(others)]
        for cp in first:
            cp.start()
        passed = [copy(4 + j, (*chip, c), sibling) for j, chip in enumerate(others)]
        for j, chip in enumerate(others):
            copy(1 + j, (*chip, c), me).wait_recv()
            passed[j].start()
        copy(0, sibling, me).wait_recv()
        for j, chip in enumerate(others):
            copy(4 + j, (*chip, 1 - c), me).wait_recv()
        for cp in first + passed:
            cp.wait_send()
        mine.wait()
        acc = all_ref[0]
        for d in range(1, NDEV):
            acc = acc + all_ref[d]
        sum_ref[...] = acc

    VM = pl.BlockSpec(memory_space=pltpu.VMEM)
    return _pcall(body, name="all_reduce_small",
                  out_shape=(jax.ShapeDtypeStruct((R, L), F32), jax.ShapeDtypeStruct((NDEV, R, L), F32)),
                  in_specs=[VM], out_specs=(VM, VM),
                  scratch_shapes=[pltpu.SemaphoreType.DMA((7,)), pltpu.SemaphoreType.DMA((7,)),
                                  pltpu.SemaphoreType.DMA])(buf)[0]


def _pack(arrs, rows_multiple=16):
    flat = [a.reshape(-1).astype(F32) for a in arrs]
    sizes = [f.shape[0] for f in flat]
    total = sum(sizes)
    per = LANES * rows_multiple
    padded = -(-total // per) * per
    flat.append(jnp.zeros((padded - total,), F32))
    offs = [0]
    for s in sizes:
        offs.append(offs[-1] + s)
    return jnp.concatenate(flat).reshape(padded // LANES, LANES), offs


def _unpack(buf, offs, shapes):
    flat = buf.reshape(-1)
    return [flat[offs[i]:offs[i + 1]].reshape(s) for i, s in enumerate(shapes)]


def kernel(x, mem, g_mix, w_in, w_a2, b_a, g_gla, w_pool, pool_scale, w_branch, w_out, g_cross, g_mem, w_cq, w_ckv, w_co, g_ffn, w_up, conv_w, conv_b, w_down, g_final, loss_target, m_g_mix, m_w_in, m_w_a2, m_b_a, m_g_gla, m_w_pool, m_pool_scale, m_w_branch, m_w_out, m_g_cross, m_g_mem, m_w_cq, m_w_ckv, m_w_co, m_g_ffn, m_w_up, m_conv_w, m_conv_b, m_w_down, m_g_final, v_g_mix, v_w_in, v_w_a2, v_b_a, v_g_gla, v_w_pool, v_pool_scale, v_w_branch, v_w_out, v_g_cross, v_g_mem, v_w_cq, v_w_ckv, v_w_co, v_g_ffn, v_w_up, v_conv_w, v_conv_b, v_w_down, v_g_final):
    weights = dict(g_mix=g_mix, w_in=w_in, w_a2=w_a2, b_a=b_a, g_gla=g_gla, w_pool=w_pool,
                   pool_scale=pool_scale, w_branch=w_branch, w_out=w_out, g_cross=g_cross, g_mem=g_mem,
                   w_cq=w_cq, w_ckv=w_ckv, w_co=w_co, g_ffn=g_ffn, w_up=w_up, conv_w=conv_w,
                   conv_b=conv_b, w_down=w_down, g_final=g_final)
    mom_m = dict(g_mix=m_g_mix, w_in=m_w_in, w_a2=m_w_a2, b_a=m_b_a, g_gla=m_g_gla, w_pool=m_w_pool,
                 pool_scale=m_pool_scale, w_branch=m_w_branch, w_out=m_w_out, g_cross=m_g_cross,
                 g_mem=m_g_mem, w_cq=m_w_cq, w_ckv=m_w_ckv, w_co=m_w_co, g_ffn=m_g_ffn, w_up=m_w_up,
                 conv_w=m_conv_w, conv_b=m_conv_b, w_down=m_w_down, g_final=m_g_final)
    mom_v = dict(g_mix=v_g_mix, w_in=v_w_in, w_a2=v_w_a2, b_a=v_b_a, g_gla=v_g_gla, w_pool=v_w_pool,
                 pool_scale=v_pool_scale, w_branch=v_w_branch, w_out=v_w_out, g_cross=v_g_cross,
                 g_mem=v_g_mem, w_cq=v_w_cq, w_ckv=v_w_ckv, w_co=v_w_co, g_ffn=v_g_ffn, w_up=v_w_up,
                 conv_w=v_conv_w, conv_b=v_conv_b, w_down=v_w_down, g_final=v_g_final)
    order = list(weights)
    big = ["w_in", "w_branch", "w_out", "w_cq", "w_ckv", "w_co", "w_up", "w_down"]
    small_sharded = ["w_a2", "w_pool", "conv_w"]
    small_repl = ["g_mix", "b_a", "g_gla", "pool_scale", "g_cross", "g_mem", "g_ffn", "conv_b", "g_final"]

    xs, ms, tgt = x[0], mem[0], loss_target[0]
    T, D = xs.shape
    M = ms.shape[0]
    DK, DV, PW = b_a.shape[1], g_gla.shape[1], pool_scale.shape[1]
    RANK = w_a2.shape[1]
    F2 = conv_b.shape[1]
    F = F2 // 2
    DIN = N_CHIPS * w_in.shape[2]
    OFF_A = 2 * DK + 2 * DV
    OFF_P = OFF_A + RANK
    RP = LANES
    GW = PW // POOL_GROUPS
    assert PW == DV and 4 * DV == 2 * D and OFF_P + PW + 2 * D == DIN

    cx, cy, cc = lax.axis_index("x"), lax.axis_index("y"), lax.axis_index("c")
    chip = 2 * cx + cy
    c_idx = jnp.reshape(cc, (1,)).astype(jnp.int32)
    chip_idx = jnp.reshape(chip, (1,)).astype(jnp.int32)

    def halves(a):
        return a.reshape(2, a.shape[0] // 2, a.shape[1])

    shard2d = {k: (weights[k][0].T if k == "w_in" else weights[k][0]) for k in big}
    small_pack, small_offs = _pack([weights[k][0] for k in small_sharded], rows_multiple=32)
    shard_halves = {k: halves(shard2d[k].astype(BF16)) for k in big}
    shard_halves["small"] = halves(small_pack)
    flying = {}
    tok = small_pack
    for group, keys in (("in", ["w_in", "small"]), ("mix", ["w_branch", "w_out"]),
                        ("cross", ["w_cq", "w_ckv", "w_co"]), ("up", ["w_up"]), ("down", ["w_down"])):
        s_sems, r_sems, sh, zones, tok = _gather_start([shard_halves[k] for k in keys], tok,
                                                       name=f"gather_start_{group}")
        flying[group] = (keys, s_sems, r_sems, sh, zones)
    tok = tok[0:1, 0:1]

    def arrive(group, after):
        keys, s_sems, r_sems, sh, zones = flying[group]
        sh, zones = _gather_wait(s_sems, r_sems, sh, zones, after, name=f"gather_wait_{group}")
        full = _gather_pass_on(sh, zones, name=f"gather_pass_on_{group}")
        return {k: f.reshape(N_CHIPS, f.shape[1] * f.shape[2], f.shape[3]) for k, f in zip(keys, full)}

    def rows(g):
        return g.reshape(-1, g.shape[2])

    h1, r1 = _rms_fwd(xs, g_mix + tok, name="norm_mix")
    gw = arrive("in", h1)
    small_all = gw["small"]
    W_in = rows(gw["w_in"])
    W_main = jnp.concatenate([W_in[:OFF_A], W_in[OFF_P:]], axis=0)
    W_a = jnp.pad(W_in[OFF_A:OFF_P], ((0, RP - RANK), (0, 0)))
    sm = [_unpack(small_all[j], small_offs, [weights[k].shape[1:] for k in small_sharded]) for j in range(N_CHIPS)]
    W_a2 = jnp.concatenate([sm[j][0] for j in range(N_CHIPS)], axis=1)
    W_a2p = jnp.pad(W_a2, ((0, RP - RANK), (0, 0))).astype(BF16)
    W_pool = jnp.concatenate([sm[j][1] for j in range(N_CHIPS)], axis=1).astype(BF16)
    W_conv = jnp.concatenate([sm[j][2] for j in range(N_CHIPS)], axis=1)

    proj = _mm(h1, W_main, "nt", name="proj_main", out_dtype=F32)
    a_pad = _mm(h1, W_a, "nt", name="proj_gate_rank", out_dtype=F32)
    o_gla, o_raw, states = _gla_fwd(proj, a_pad, W_a2p, b_a, g_gla, T=T, DK=DK, DV=DV)
    o_pool = _pool_fwd(proj, W_pool, pool_scale, T=T, PW=PW, col_block=3)
    gw = arrive("mix", o_pool)
    W_branch, W_out = rows(gw["w_branch"]), rows(gw["w_out"])
    y_gla = _mm(o_gla, W_branch, "nn", name="branch_gla", out_dtype=F32, K=DV)
    y_pool = _mm(o_pool, W_branch, "nn", name="branch_pool", out_dtype=F32, K=PW, b_off=(DV, 0))
    merged = _merge_fwd(y_gla, y_pool, proj, T=T, D=D, col_block=2)
    x1 = _mm(merged, W_out, "nn", name="mix_out", out_dtype=F32, add=xs)

    h2, r2 = _rms_fwd(x1, g_cross, name="norm_cross")
    mem_n, rm = _rms_fwd(ms, g_mem, name="norm_mem")
    gw = arrive("cross", h2)
    W_cq, W_ckv, W_co = rows(gw["w_cq"]), gw["w_ckv"], rows(gw["w_co"])
    qc = _mm(h2, W_cq, "nn", name="cross_q", out_dtype=BF16)
    kv = _mm(mem_n, W_ckv, "nn", name="cross_kv", out_dtype=BF16, b_blocked=True)
    o_att = _attn_fwd(qc, kv, T=T, D=D, M=M)
    x2 = _mm(o_att, W_co, "nn", name="cross_out", out_dtype=F32, add=x1)

    h3, r3 = _rms_fwd(x2, g_ffn, name="norm_ffn")
    W_up = arrive("up", h3)["w_up"]
    u0 = _mm(h3, W_up, "nn", name="ffn_up", out_dtype=F32, b_blocked=True)
    f_act = _conv_fwd(u0, W_conv, conv_b, T=T, F=F)
    W_down = rows(arrive("down", f_act)["w_down"])
    x3 =_mm(f_act, W_down, "nn", name="ffn_down", out_dtype=F32, add=x2)

    loss_part, dx3, dx3_b, dg_final = _loss_head(x3, g_final.reshape(1, D), tgt)

    def col_shards(g):
        nb, K, Nb = g.shape
        return g.reshape(nb, 2, K // 2, Nb)

    def row_shards(g):
        R, N = g.shape
        return g.reshape(N_CHIPS, 2, R // N_CHIPS // 2, N)

    in_flight = []

    def reduce_start(group, keys, partials, after):
        from_sibling = _exchange_halves(partials, after, name=f"grad_exchange_halves_{group}")
        chip_sums = [_add_halves(p, r, c_idx, name=f"grad_add_halves_{k}")
                     for k, p, r in zip(keys, partials, from_sibling)]
        s_sems, r_sems, sums, lands, token = _scatter_start(chip_sums, name=f"grad_scatter_start_{group}")
        in_flight.append((group, keys, s_sems, r_sems, sums, lands))
        return token[0:1, 0:1]

    df = _mm(dx3_b, W_down, "nt", name="d_ffn_act", out_dtype=BF16)
    dW_down = _mm(f_act, dx3_b, "tn", name="dw_down", out_dtype=BF16)
    du0, dconv_w, dconv_b = _conv_bwd(u0, W_conv, conv_b, df, T=T, F=F)
    dh3 = _mm(du0, W_up, "nt", name="d_ffn_in", out_dtype=F32, b_blocked=True)
    dW_up = _mm(h3, du0, "tn", name="dw_up", out_dtype=BF16, out_blocks=N_CHIPS)
    tok = reduce_start("ffn", ["w_down", "w_up"], [row_shards(dW_down), col_shards(dW_up)], dh3)
    dx2, dx2_b, dg_ffn = _rms_bwd(dh3, x2, r3 + tok, g_ffn, dx3, name="norm_ffn_bwd")

    do_att = _mm(dx2_b, W_co, "nt", name="d_cross_o", out_dtype=BF16)
    dW_co = _mm(o_att, dx2_b, "tn", name="dw_co", out_dtype=BF16)
    dq, dkv = _attn_bwd(qc, kv, do_att, T=T, D=D, M=M)
    dkv_b = dkv.astype(BF16)
    dW_cq = _mm(h2, dq, "tn", name="dw_cq", out_dtype=BF16)
    dh2 = _mm(dq, W_cq, "nt", name="d_cross_in", out_dtype=F32)
    dW_ckv = _mm(mem_n, dkv_b, "tn", name="dw_ckv", out_dtype=BF16, out_blocks=N_CHIPS)
    dmem_n = _mm(dkv_b, W_ckv, "nt", name="d_mem", out_dtype=F32, b_blocked=True)
    tok = reduce_start("cross", ["w_co", "w_cq", "w_ckv"],
                       [row_shards(dW_co), row_shards(dW_cq), col_shards(dW_ckv)], dmem_n)
    _, _, dg_mem = _rms_bwd(dmem_n, ms, rm, g_mem, None, name="norm_mem_bwd")
    dx1, dx1_b, dg_cross = _rms_bwd(dh2, x1, r2 + tok, g_cross, dx2, name="norm_cross_bwd")

    dmerged = _mm(dx1_b, W_out, "nt", name="d_merged", out_dtype=F32)
    dW_out = _mm(merged, dx1_b, "tn", name="dw_out", out_dtype=BF16)
    dy_gla, dy_pool, dgates = _merge_bwd(dmerged, y_gla, y_pool, proj, T=T, D=D, col_block=2)
    dW_br_gla = _mm(o_gla, dy_gla, "tn", name="dw_branch_gla", out_dtype=BF16)
    dW_br_pool = _mm(o_pool, dy_pool, "tn", name="dw_branch_pool", out_dtype=BF16)
    do_gla = _mm(dy_gla, W_branch, "nt", name="d_o_gla", out_dtype=F32, N=DV)
    do_pool = _mm(dy_pool, W_branch, "nt", name="d_o_pool", out_dtype=F32, N=PW, b_off=(DV, 0))
    tok = reduce_start("mix", ["w_out", "w_branch"],
                       [row_shards(dW_out), row_shards(jnp.concatenate([dW_br_gla, dW_br_pool], axis=0))],
                       do_pool)
    dp, dw_pool, dpool_scale = _pool_bwd(proj, W_pool, pool_scale + tok, do_pool, T=T, PW=PW, col_block=3)
    dqkvr, da_pad, dw2, db_a, dg_gla = _gla_bwd(proj, a_pad, W_a2p, b_a + tok, g_gla, o_raw, states, do_gla,
                                               T=T, DK=DK, DV=DV)
    dproj = jnp.concatenate([dqkvr, dp, dgates], axis=1)
    dh1 = _mm(dproj, W_main, "nn", name="d_mix_in_main", out_dtype=F32)
    dh1 = _mm(da_pad, W_a, "nn", name="d_mix_in_rank", out_dtype=F32, add=dh1)
    dW_main = _mm(dproj, h1, "tn", name="dw_in_main", out_dtype=BF16)
    dW_a = _mm(da_pad, h1, "tn", name="dw_in_rank", out_dtype=BF16)
    dx0, _, dg_mix = _rms_bwd(dh1, xs, r1, g_mix, dx1, name="norm_mix_bwd")

    grads = {}

    small_grads = [loss_part, dg_mix, db_a, dg_gla, dpool_scale, dg_cross, dg_mem, dg_ffn, dconv_b, dg_final,
                   dw2[:RANK], dw_pool, dconv_w]
    small_buf, offs = _pack(small_grads)
    small_sum = _all_reduce_small(small_buf)
    red = _unpack(small_sum, offs, [g.shape for g in small_grads])
    loss = red[0][0, 0]
    for k, g in zip(small_repl, red[1:10]):
        grads[k] = g.reshape(weights[k].shape)
    nb = DK // N_CHIPS
    grads["w_a2"] = lax.dynamic_slice_in_dim(red[10], chip * nb, nb, axis=1)[None]
    nb = GW // N_CHIPS
    grads["w_pool"] = lax.dynamic_slice_in_dim(red[11], chip * nb, nb, axis=1)[None]
    nb = F2 // N_CHIPS
    grads["conv_w"] = lax.dynamic_slice_in_dim(red[12], chip * nb, nb, axis=1)[None]

    delta, new_m, new_v = {}, {}, {}

    def whole(k, a):
        a = a.reshape(-1, a.shape[2])
        return (a.T if k == "w_in" else a)[None]

    dW_in = jnp.concatenate([dW_main[:OFF_A], dW_a[:RANK], dW_main[OFF_A:]], axis=0)
    reduce_start("in", ["w_in"], [row_shards(dW_in)], small_sum)
    after = in_flight[-1][4][0]

    for group, keys, s_sems, r_sems, sums, lands in in_flight:
        sums, from_chips = _scatter_wait(s_sems, r_sems, sums, lands, after, name=f"grad_scatter_wait_{group}")
        half_sums = [_add_chips(s, r, chip_idx, name=f"grad_add_chips_{k}") for k, s, r in zip(keys, sums, from_chips)]
        other_sums = _swap_halves(half_sums, name=f"grad_swap_halves_{group}")
        for k, mine, other in zip(keys, half_sums, other_sums):
            wmv = [halves(src[k][0].T if k == "w_in" else src[k][0]) for src in (weights, mom_m, mom_v)]
            res = _adamw_halves(*wmv, mine, other, c_idx, name=f"adamw_{k}")
            grads[k], delta[k], new_m[k], new_v[k] = (whole(k, a) for a in res)
            after = res[1]
    small = small_repl + small_sharded
    packs = [_pack([src[k] for k in small])[0] for src in (weights, grads, mom_m, mom_v)]
    _, offs = _pack([weights[k] for k in small])
    outs = _adamw(*packs, name="adamw_small")
    for res, o in zip((delta, new_m, new_v), outs):
        for k, a in zip(small, _unpack(o, offs, [weights[k].shape for k in small])):
            res[k] = a

    return (loss, dx0[None], *[grads[k] for k in order], *[delta[k] for k in order],
            *[new_m[k] for k in order], *[new_v[k] for k in order])
```

```python
import functools

import jax
import jax.numpy as jnp
from jax import lax
from jax.experimental import pallas as pl
from jax.experimental.pallas import tpu as pltpu

F32 = jnp.float32
BF16 = jnp.bfloat16
MESH = pl.DeviceIdType.MESH
HIGHEST = lax.Precision.HIGHEST

EPS = 1e-6
GLA_HEADS = 4
GLA_CHUNK = 64
GLA_GATE_NORM = 16.0
POOL_GROUPS = 4
CROSS_HEADS = 4
CONV_W = 3
N_CHIPS = 4
LANES = 128
SUBLANES = 8
VMEM_LIMIT = 56 << 20

ADAM_LR = 0.001
ADAM_B1 = 0.9
ADAM_B2 = 0.999
ADAM_EPS = 1e-08
ADAM_WD = 0.01
ADAM_STEP = 10

NN = (((1,), (0,)), ((), ()))
NT = (((1,), (1,)), ((), ()))
TN = (((0,), (0,)), ((), ()))


def _dot(a, b, dn=NN, precision=None):
    return lax.dot_general(a, b, dn, precision=precision, preferred_element_type=F32)


def _tile(n, pref, align=LANES):
    t = (min(pref, n) // align) * align
    while t >= align:
        if n % t == 0:
            return t
        t -= align
    return n


def _pcall(body, *, name, out_shape, grid=(), in_specs=None, out_specs=None, scratch_shapes=(),
           semantics=None, prefetch=0, aliases=None, split_copy=False):
    params = dict(vmem_limit_bytes=VMEM_LIMIT)
    if semantics is not None:
        params["dimension_semantics"] = semantics
    if split_copy:
        params["has_side_effects"] = pltpu.SideEffectType.DATAFLOW_SIDE_EFFECTING
    if prefetch:
        grid_spec = pltpu.PrefetchScalarGridSpec(
            num_scalar_prefetch=prefetch, grid=grid, in_specs=in_specs, out_specs=out_specs,
            scratch_shapes=scratch_shapes)
        return pl.pallas_call(body, name=name, out_shape=out_shape, grid_spec=grid_spec,
                              compiler_params=pltpu.CompilerParams(**params))
    kw = {}
    if aliases is not None:
        kw["input_output_aliases"] = aliases
    if in_specs is not None:
        kw["in_specs"] = in_specs
    if out_specs is not None:
        kw["out_specs"] = out_specs
    return pl.pallas_call(body, name=name, out_shape=out_shape, grid=grid,
                          scratch_shapes=scratch_shapes,
                          compiler_params=pltpu.CompilerParams(**params), **kw)


def _sigmoid(x):
    return 1.0 / (1.0 + jnp.exp(-x))


def _log_sigmoid(x):
    return jnp.minimum(x, 0.0) - jnp.log(1.0 + jnp.exp(-jnp.abs(x)))


def _mm(a, b, mode, *, name, out_dtype, M=None, N=None, K=None, a_off=(0, 0), b_off=(0, 0),
        add=None, b_blocked=False, out_blocks=0, after=None, tm=1536, tn=1536, tk=2048):
    if b_blocked:
        nb, R, Cb = b.shape
        b_rows, b_cols = R, nb * Cb
    else:
        b_rows, b_cols = b.shape
    if mode == "nn":
        M = M or a.shape[0]; K = K or a.shape[1]; N = N or b_cols
    elif mode == "nt":
        M = M or a.shape[0]; K = K or a.shape[1]; N = N or b_rows
    else:
        K = K or a.shape[0]; M = M or a.shape[1]; N = N or b_cols
    tm = _tile(M, tm, LANES if mode == "tn" else 16)
    tn = _tile(Cb if (b_blocked and mode != "nt") else (N // out_blocks if out_blocks else N), tn)
    tk = _tile(Cb if (b_blocked and mode == "nt") else K, tk)
    nk = K // tk
    dn = {"nn": NN, "nt": NT, "tn": TN}[mode]

    def off(o, t):
        assert o % t == 0, (name, o, t)
        return o // t

    if mode == "tn":
        ar, ac = off(a_off[0], tk), off(a_off[1], tm)
        a_spec = pl.BlockSpec((tk, tm), lambda i, j, k: (k + ar, i + ac))
    else:
        ar, ac = off(a_off[0], tm), off(a_off[1], tk)
        a_spec = pl.BlockSpec((tm, tk), lambda i, j, k: (i + ar, k + ac))
    if b_blocked and mode == "nt":
        per = Cb // tk
        b_spec = pl.BlockSpec((None, tn, tk), lambda i, j, k: (k // per, j, k % per))
    elif b_blocked:
        per = Cb // tn
        b_spec = pl.BlockSpec((None, tk, tn), lambda i, j, k: (j // per, k, j % per))
    elif mode == "nt":
        br, bc = off(b_off[0], tn), off(b_off[1], tk)
        b_spec = pl.BlockSpec((tn, tk), lambda i, j, k: (j + br, k + bc))
    else:
        br, bc = off(b_off[0], tk), off(b_off[1], tn)
        b_spec = pl.BlockSpec((tk, tn), lambda i, j, k: (k + br, j + bc))
    if out_blocks:
        per_o = N // out_blocks // tn
        o_spec = pl.BlockSpec((None, tm, tn), lambda i, j, k: (j // per_o, i, j % per_o))
        out_shape = jax.ShapeDtypeStruct((out_blocks, M, N // out_blocks), out_dtype)
    else:
        o_spec = pl.BlockSpec((tm, tn), lambda i, j, k: (i, j))
        out_shape = jax.ShapeDtypeStruct((M, N), out_dtype)
    in_specs = [a_spec, b_spec]
    args = [a, b]
    if add is not None:
        assert not out_blocks
        in_specs.append(o_spec)
        args.append(add)
    if after is not None:
        in_specs.append(pl.BlockSpec(memory_space=pl.ANY))
        args.append(after)
    n_in = len(args)

    def finish(r, refs):
        if add is not None:
            r = r + refs[2][...]
        o_ref = refs[n_in]
        o_ref[...] = r.astype(o_ref.dtype)

    def body_one(*refs):
        finish(_dot(refs[0][...].astype(BF16), refs[1][...].astype(BF16), dn), refs)

    def body_acc(*refs):
        acc_ref = refs[-1]
        k = pl.program_id(2)

        @pl.when(k == 0)
        def _():
            acc_ref[...] = jnp.zeros_like(acc_ref)

        acc_ref[...] += _dot(refs[0][...].astype(BF16), refs[1][...].astype(BF16), dn)

        @pl.when(k == nk - 1)
        def _():
            finish(acc_ref[...], refs)

    return _pcall(body_one if nk == 1 else body_acc, name=name, out_shape=out_shape,
                  grid=(M // tm, N // tn, nk), in_specs=in_specs, out_specs=o_spec,
                  scratch_shapes=[] if nk == 1 else [pltpu.VMEM((tm, tn), F32)],
                  semantics=("parallel", "parallel", "arbitrary"))(*args)


def _rms_fwd(x, g, *, name):
    T, D = x.shape
    tr = _tile(T, 128, 16)

    def body(x_ref, g_ref, h_ref, r_ref):
        xv = x_ref[...]
        r = lax.rsqrt(jnp.mean(xv * xv, axis=-1, keepdims=True) + EPS)
        h_ref[...] = (xv * r * g_ref[...]).astype(h_ref.dtype)
        r_ref[...] = r

    row = pl.BlockSpec((tr, D), lambda i: (i, 0))
    return _pcall(body, name=name,
                  out_shape=(jax.ShapeDtypeStruct((T, D), BF16), jax.ShapeDtypeStruct((T, 1), F32)),
                  grid=(T // tr,),
                  in_specs=[row, pl.BlockSpec((1, D), lambda i: (0, 0))],
                  out_specs=(row, pl.BlockSpec((tr, 1), lambda i: (i, 0))),
                  semantics=("parallel",))(x, g)


def _rms_bwd(dh, x, rstd, g, dres, *, name):
    T, D = x.shape
    tr = _tile(T, 128, 16)
    has_res = dres is not None

    def body(*refs):
        if has_res:
            dh_ref, x_ref, r_ref, g_ref, res_ref, dx_ref, dxb_ref, dg_ref = refs
        else:
            dh_ref, x_ref, r_ref, g_ref, dx_ref, dxb_ref, dg_ref = refs
        r = r_ref[...]
        xh = x_ref[...] * r
        dhv = dh_ref[...].astype(F32)
        dxh = dhv * g_ref[...]
        m = jnp.mean(dxh * xh, axis=-1, keepdims=True)
        dx = r * (dxh - xh * m)
        if has_res:
            dx = dx + res_ref[...]
        dx_ref[...] = dx
        dxb_ref[...] = dx.astype(BF16)

        @pl.when(pl.program_id(0) == 0)
        def _():
            dg_ref[...] = jnp.zeros_like(dg_ref)

        dg_ref[...] += jnp.sum(dhv * xh, axis=0, keepdims=True)

    row = pl.BlockSpec((tr, D), lambda i: (i, 0))
    vec = pl.BlockSpec((1, D), lambda i: (0, 0))
    in_specs = [row, row, pl.BlockSpec((tr, 1), lambda i: (i, 0)), vec]
    args = [dh, x, rstd, g]
    if has_res:
        in_specs.append(row)
        args.append(dres)
    return _pcall(body, name=name,
                  out_shape=(jax.ShapeDtypeStruct((T, D), F32), jax.ShapeDtypeStruct((T, D), BF16),
                             jax.ShapeDtypeStruct((1, D), F32)),
                  grid=(T // tr,), in_specs=in_specs, out_specs=(row, row, vec),
                  semantics=("arbitrary",))(*args)


def _loss_head(x3, g, tgt):
    T, D = x3.shape
    tr = _tile(T, 128, 16)

    def body(x_ref, g_ref, t_ref, loss_ref, dx_ref, dxb_ref, dg_ref):
        xv = x_ref[...]
        gv = g_ref[...]
        r = lax.rsqrt(jnp.mean(xv * xv, axis=-1, keepdims=True) + EPS)
        xh = xv * r
        err = xh * gv - t_ref[...]
        dy = err * (1.0 / D)
        dxh = dy * gv
        m = jnp.mean(dxh * xh, axis=-1, keepdims=True)
        dx = r * (dxh - xh * m)
        dx_ref[...] = dx
        dxb_ref[...] = dx.astype(BF16)

        @pl.when(pl.program_id(0) == 0)
        def _():
            dg_ref[...] = jnp.zeros_like(dg_ref)
            loss_ref[...] = jnp.zeros_like(loss_ref)

        dg_ref[...] += jnp.sum(dy * xh, axis=0, keepdims=True)
        part = 0.5 * jnp.sum(jnp.mean(err * err, axis=-1, keepdims=True), axis=0, keepdims=True)
        loss_ref[...] += jnp.broadcast_to(part, loss_ref.shape)

    row = pl.BlockSpec((tr, D), lambda i: (i, 0))
    vec = pl.BlockSpec((1, D), lambda i: (0, 0))
    return _pcall(body, name="loss_head",
                  out_shape=(jax.ShapeDtypeStruct((1, LANES), F32), jax.ShapeDtypeStruct((T, D), F32),
                             jax.ShapeDtypeStruct((T, D), BF16), jax.ShapeDtypeStruct((1, D), F32)),
                  grid=(T // tr,), in_specs=[row, vec, row],
                  out_specs=(pl.BlockSpec((1, LANES), lambda i: (0, 0)), row, row, vec),
                  semantics=("arbitrary",))(x3, g, tgt)


def _gla_chunk_terms(qk, a_ref, w2_ref, ba_ref, DK):
    C = qk.shape[0]
    gp = _dot(a_ref[...].astype(BF16), w2_ref[...]) + ba_ref[...]
    la = _log_sigmoid(gp) * (1.0 / GLA_GATE_NORM)
    row = lax.broadcasted_iota(jnp.int32, (C, C), 0)
    col = lax.broadcasted_iota(jnp.int32, (C, C), 1)
    causal = row >= col
    b = _dot(causal.astype(F32), la, precision=HIGHEST)
    return gp, b, causal


def _gla_fwd(proj, a_pad, w2, b_a, g_gla, *, T, DK, DV):
    assert 2 * DK == DV
    H = GLA_HEADS
    HK, HV = DK // H, DV // H
    C = GLA_CHUNK
    n = T // C
    RP = a_pad.shape[1]
    scale = HK ** -0.5

    def body(qk_ref, v_ref, r_ref, a_ref, w2_ref, ba_ref, gg_ref, og_ref, oraw_ref, st_ref, s_ref):
        @pl.when(pl.program_id(0) == 0)
        def _():
            s_ref[...] = jnp.zeros_like(s_ref)

        st_ref[...] = s_ref[...]
        qk = qk_ref[...]
        _, b, causal = _gla_chunk_terms(qk, a_ref, w2_ref, ba_ref, DK)
        for h in range(H):
            ks = slice(h * HK, (h + 1) * HK)
            vs = slice(h * HV, (h + 1) * HV)
            bh = b[:, ks]
            b_last = bh[C - 1:C, :]
            qt = qk[:, ks] * scale * jnp.exp(bh)
            kh = qk[:, DK + h * HK:DK + (h + 1) * HK]
            kt = kh * jnp.exp(-bh)
            khat = kh * jnp.exp(b_last - bh)
            a_mat = jnp.where(causal, _dot(qt, kt, NT, HIGHEST), 0.0)
            vh = v_ref[:, vs]
            s_t = s_ref[h]
            o = _dot(a_mat, vh, NN, HIGHEST) + _dot(qt, s_t, NT, HIGHEST)
            s_ref[h] = s_t * jnp.exp(b_last) + _dot(vh, khat, TN, HIGHEST)
            rs = lax.rsqrt(jnp.mean(o * o, axis=-1, keepdims=True) + EPS)
            rr = r_ref[:, vs]
            og = o * rs * gg_ref[:, vs] * (rr * _sigmoid(rr))
            oraw_ref[:, vs] = o
            og_ref[:, vs] = og.astype(BF16)

    blk = lambda j: pl.BlockSpec((C, DV), lambda i: (i, j))
    full = lambda s: pl.BlockSpec(s, lambda i: (0,) * len(s))
    return _pcall(
        body, name="gla_fwd",
        out_shape=(jax.ShapeDtypeStruct((T, DV), BF16), jax.ShapeDtypeStruct((T, DV), F32),
                   jax.ShapeDtypeStruct((n, H, HV, HK), F32)),
        grid=(n,),
        in_specs=[blk(0), blk(1), blk(2), pl.BlockSpec((C, RP), lambda i: (i, 0)),
                  full((RP, DK)), full((1, DK)), full((1, DV))],
        out_specs=(blk(0), blk(0), pl.BlockSpec((None, H, HV, HK), lambda i: (i, 0, 0, 0))),
        scratch_shapes=[pltpu.VMEM((H, HV, HK), F32)],
        semantics=("arbitrary",))(proj, proj, proj, a_pad, w2, b_a, g_gla)


def _gla_bwd(proj, a_pad, w2, b_a, g_gla, o_raw, states, do_gla, *, T, DK, DV):
    H = GLA_HEADS
    HK, HV = DK // H, DV // H
    C = GLA_CHUNK
    n = T // C
    RP = a_pad.shape[1]
    scale = HK ** -0.5

    def body(qk_ref, v_ref, r_ref, a_ref, w2_ref, ba_ref, gg_ref, oraw_ref, st_ref, dog_ref,
             dqkvr_ref, da_ref, dw2_ref, dba_ref, dgg_ref, ds_ref):
        @pl.when(pl.program_id(0) == 0)
        def _():
            ds_ref[...] = jnp.zeros_like(ds_ref)
            dw2_ref[...] = jnp.zeros_like(dw2_ref)
            dba_ref[...] = jnp.zeros_like(dba_ref)
            dgg_ref[...] = jnp.zeros_like(dgg_ref)

        qk = qk_ref[...]
        gp, b, causal = _gla_chunk_terms(qk, a_ref, w2_ref, ba_ref, DK)
        row = lax.broadcasted_iota(jnp.int32, (C, C), 0)
        col = lax.broadcasted_iota(jnp.int32, (C, C), 1)
        upper = (col >= row).astype(F32)
        dla_parts = []
        for h in range(H):
            ks = slice(h * HK, (h + 1) * HK)
            vs = slice(h * HV, (h + 1) * HV)
            bh = b[:, ks]
            b_last = bh[C - 1:C, :]
            eb = jnp.exp(bh)
            emb = jnp.exp(-bh)
            ehat = jnp.exp(b_last - bh)
            e_last = jnp.exp(b_last)
            qt = qk[:, ks] * scale * eb
            kh = qk[:, DK + h * HK:DK + (h + 1) * HK]
            kt = kh * emb
            khat = kh * ehat
            a_mat = jnp.where(causal, _dot(qt, kt, NT, HIGHEST), 0.0)
            vh = v_ref[:, vs]
            o = oraw_ref[:, vs]
            rs = lax.rsqrt(jnp.mean(o * o, axis=-1, keepdims=True) + EPS)
            on = o * rs
            gg = gg_ref[:, vs]
            rr = r_ref[:, vs]
            sg = _sigmoid(rr)
            d_out = dog_ref[:, vs]
            dr = d_out * (on * gg) * (sg * (1.0 + rr * (1.0 - sg)))
            d_og = d_out * (rr * sg)
            dgg_ref[:, vs] += jnp.sum(d_og * on, axis=0, keepdims=True)
            d_on = d_og * gg
            d_o = rs * (d_on - on * jnp.mean(d_on * on, axis=-1, keepdims=True))
            s_t = st_ref[h]
            ds_t = ds_ref[h]
            d_a = jnp.where(causal, _dot(d_o, vh, NT, HIGHEST), 0.0)
            dv = _dot(a_mat, d_o, TN, HIGHEST) + _dot(khat, ds_t, NT, HIGHEST)
            dqt = _dot(d_a, kt, NN, HIGHEST) + _dot(d_o, s_t, NN, HIGHEST)
            dkt = _dot(d_a, qt, TN, HIGHEST)
            dkhat = _dot(vh, ds_t, NN, HIGHEST)
            ds_ref[h] = ds_t * e_last + _dot(d_o, qt, TN, HIGHEST)
            dq = dqt * eb * scale
            dk = dkt * emb + dkhat * ehat
            db = dqt * qt - dkt * kt - dkhat * khat
            d_last = (jnp.sum(dkhat * khat, axis=0, keepdims=True)
                      + e_last * jnp.sum(ds_t * s_t, axis=0, keepdims=True))
            dla_parts.append(_dot(upper, db, NN, HIGHEST) + d_last)
            dqkvr_ref[:, ks] = dq.astype(BF16)
            dqkvr_ref[:, DK + h * HK:DK + (h + 1) * HK] = dk.astype(BF16)
            dqkvr_ref[:, DV + h * HV:DV + (h + 1) * HV] = dv.astype(BF16)
            dqkvr_ref[:, 2 * DV + h * HV:2 * DV + (h + 1) * HV] = dr.astype(BF16)
        dla = jnp.concatenate(dla_parts, axis=1)
        dgp = dla * (1.0 / GLA_GATE_NORM) * _sigmoid(-gp)
        dba_ref[...] += jnp.sum(dgp, axis=0, keepdims=True)
        dgp_b = dgp.astype(BF16)
        dw2_ref[...] += _dot(a_ref[...].astype(BF16), dgp_b, TN)
        da_ref[...] = _dot(dgp_b, w2_ref[...], NT).astype(BF16)

    rev = lambda j: pl.BlockSpec((C, DV), lambda i: (n - 1 - i, j))
    full = lambda s: pl.BlockSpec(s, lambda i: (0,) * len(s))
    return _pcall(
        body, name="gla_bwd",
        out_shape=(jax.ShapeDtypeStruct((T, 3 * DV), BF16), jax.ShapeDtypeStruct((T, RP), BF16),
                   jax.ShapeDtypeStruct((RP, DK), F32), jax.ShapeDtypeStruct((1, DK), F32),
                   jax.ShapeDtypeStruct((1, DV), F32)),
        grid=(n,),
        in_specs=[rev(0), rev(1), rev(2), pl.BlockSpec((C, RP), lambda i: (n - 1 - i, 0)),
                  full((RP, DK)), full((1, DK)), full((1, DV)), rev(0),
                  pl.BlockSpec((None, H, HV, HK), lambda i: (n - 1 - i, 0, 0, 0)), rev(0)],
        out_specs=(pl.BlockSpec((C, 3 * DV), lambda i: (n - 1 - i, 0)),
                   pl.BlockSpec((C, RP), lambda i: (n - 1 - i, 0)),
                   full((RP, DK)), full((1, DK)), full((1, DV))),
        scratch_shapes=[pltpu.VMEM((H, HV, HK), F32)],
        semantics=("arbitrary",))(proj, proj, proj, a_pad, w2, b_a, g_gla, o_raw, states, do_gla)


def _pool_windows(p, g, T):
    t = lax.broadcasted_iota(jnp.int32, (T, 1), 0)
    s = p
    for lvl in range(POOL_GROUPS):
        sh = 1 << lvl
        nxt = s + jnp.where(t >= sh, pltpu.roll(s, sh, 0), 0.0)
        s = jnp.where(lvl <= g, nxt, s)
    win = jnp.left_shift(2, g)
    inv = 1.0 / jnp.minimum(t + 1, win).astype(F32)
    return s * inv - p, inv


def _pool_fwd(proj, w_pool, scale, *, T, PW, col_block):
    GW = PW // POOL_GROUPS
    per = PW // GW

    def body(p_ref, w_ref, s_ref, o_ref):
        g = pl.program_id(0)
        pooled, _ = _pool_windows(p_ref[...], g, T)
        mixed = _dot(pooled.astype(BF16), w_ref[...])
        o_ref[...] = (mixed * s_ref[...]).astype(BF16)

    return _pcall(body, name="pool_fwd", out_shape=jax.ShapeDtypeStruct((T, PW), BF16),
                  grid=(POOL_GROUPS,),
                  in_specs=[pl.BlockSpec((T, GW), lambda g: (0, col_block * per + g)),
                            pl.BlockSpec((None, GW, GW), lambda g: (g, 0, 0)),
                            pl.BlockSpec((1, GW), lambda g: (0, g))],
                  out_specs=pl.BlockSpec((T, GW), lambda g: (0, g)),
                  semantics=("parallel",))(proj, w_pool, scale)


def _pool_bwd(proj, w_pool, scale, do_pool, *, T, PW, col_block):
    GW = PW // POOL_GROUPS
    per = PW // GW

    def body(p_ref, w_ref, s_ref, do_ref, dp_ref, dw_ref, dsc_ref):
        g = pl.program_id(0)
        pooled, inv = _pool_windows(p_ref[...], g, T)
        pooled_b = pooled.astype(BF16)
        w = w_ref[...]
        mixed = _dot(pooled_b, w)
        d_out = do_ref[...]
        dsc_ref[...] = jnp.sum(d_out * mixed, axis=0, keepdims=True)
        dmixed = (d_out * s_ref[...]).astype(BF16)
        dw_ref[...] = _dot(pooled_b, dmixed, TN)
        dpooled = _dot(dmixed, w, NT)
        t = lax.broadcasted_iota(jnp.int32, (T, 1), 0)
        s = dpooled * inv
        for lvl in range(POOL_GROUPS):
            sh = 1 << lvl
            nxt = s + jnp.where(t < T - sh, pltpu.roll(s, T - sh, 0), 0.0)
            s = jnp.where(lvl <= g, nxt, s)
        dp_ref[...] = (s - dpooled).astype(BF16)

    return _pcall(body, name="pool_bwd",
                  out_shape=(jax.ShapeDtypeStruct((T, PW), BF16),
                             jax.ShapeDtypeStruct((POOL_GROUPS, GW, GW), F32),
                             jax.ShapeDtypeStruct((1, PW), F32)),
                  grid=(POOL_GROUPS,),
                  in_specs=[pl.BlockSpec((T, GW), lambda g: (0, col_block * per + g)),
                            pl.BlockSpec((None, GW, GW), lambda g: (g, 0, 0)),
                            pl.BlockSpec((1, GW), lambda g: (0, g)),
                            pl.BlockSpec((T, GW), lambda g: (0, g))],
                  out_specs=(pl.BlockSpec((T, GW), lambda g: (0, g)),
                             pl.BlockSpec((None, GW, GW), lambda g: (g, 0, 0)),
                             pl.BlockSpec((1, GW), lambda g: (0, g))),
                  semantics=("parallel",))(proj, w_pool, scale, do_pool)


def _merge_fwd(y_gla, y_pool, proj, *, T, D, col_block):
    tr = _tile(T, 128, 16)

    def body(yg_ref, yp_ref, g1_ref, g2_ref, o_ref):
        o_ref[...] = (_sigmoid(g1_ref[...]) * yg_ref[...]
                      + _sigmoid(g2_ref[...]) * yp_ref[...]).astype(BF16)

    row = pl.BlockSpec((tr, D), lambda i: (i, 0))
    return _pcall(body, name="merge_fwd", out_shape=jax.ShapeDtypeStruct((T, D), BF16),
                  grid=(T // tr,),
                  in_specs=[row, row, pl.BlockSpec((tr, D), lambda i: (i, col_block)),
                            pl.BlockSpec((tr, D), lambda i: (i, col_block + 1))],
                  out_specs=row, semantics=("parallel",))(y_gla, y_pool, proj, proj)


def _merge_bwd(dmerged, y_gla, y_pool, proj, *, T, D, col_block):
    tr = _tile(T, 128, 16)

    def body(dm_ref, yg_ref, yp_ref, g1_ref, g2_ref, dyg_ref, dyp_ref, dg_ref):
        dm = dm_ref[...]
        s1 = _sigmoid(g1_ref[...])
        s2 = _sigmoid(g2_ref[...])
        dyg_ref[...] = (dm * s1).astype(BF16)
        dyp_ref[...] = (dm * s2).astype(BF16)
        dg_ref[:, :D] = (dm * yg_ref[...] * s1 * (1.0 - s1)).astype(BF16)
        dg_ref[:, D:] = (dm * yp_ref[...] * s2 * (1.0 - s2)).astype(BF16)

    row = pl.BlockSpec((tr, D), lambda i: (i, 0))
    return _pcall(body, name="merge_bwd",
                  out_shape=(jax.ShapeDtypeStruct((T, D), BF16), jax.ShapeDtypeStruct((T, D), BF16),
                             jax.ShapeDtypeStruct((T, 2 * D), BF16)),
                  grid=(T // tr,),
                  in_specs=[row, row, row, pl.BlockSpec((tr, D), lambda i: (i, col_block)),
                            pl.BlockSpec((tr, D), lambda i: (i, col_block + 1))],
                  out_specs=(row, row, pl.BlockSpec((tr, 2 * D), lambda i: (i, 0))),
                  semantics=("parallel",))(dmerged, y_gla, y_pool, proj, proj)


def _attn_fwd(q, kv, *, T, D, M):
    H = CROSS_HEADS
    HD = D // H
    tq = _tile(T, 512, 16)
    scale = HD ** -0.5

    def body(q_ref, kv_ref, o_ref):
        for h in range(H):
            hs = slice(h * HD, (h + 1) * HD)
            s = _dot(q_ref[:, hs], kv_ref[:, hs], NT) * scale
            e = jnp.exp(s - jnp.max(s, axis=-1, keepdims=True))
            p = e / jnp.sum(e, axis=-1, keepdims=True)
            o_ref[:, hs] = _dot(p.astype(BF16), kv_ref[:, D + h * HD:D + (h + 1) * HD]).astype(BF16)

    row = pl.BlockSpec((tq, D), lambda i: (i, 0))
    return _pcall(body, name="attn_fwd", out_shape=jax.ShapeDtypeStruct((T, D), BF16),
                  grid=(T // tq,), in_specs=[row, pl.BlockSpec((M, 2 * D), lambda i: (0, 0))],
                  out_specs=row, semantics=("parallel",))(q, kv)


def _attn_bwd(q, kv, do, *, T, D, M):
    H = CROSS_HEADS
    HD = D // H
    tq = _tile(T, 512, 16)
    scale = HD ** -0.5

    def body(q_ref, kv_ref, do_ref, dq_ref, dkv_ref):
        @pl.when(pl.program_id(0) == 0)
        def _():
            dkv_ref[...] = jnp.zeros_like(dkv_ref)

        for h in range(H):
            hs = slice(h * HD, (h + 1) * HD)
            vs = slice(D + h * HD, D + (h + 1) * HD)
            qh = q_ref[:, hs]
            kh = kv_ref[:, hs]
            s = _dot(qh, kh, NT) * scale
            e = jnp.exp(s - jnp.max(s, axis=-1, keepdims=True))
            p = e / jnp.sum(e, axis=-1, keepdims=True)
            p_b = p.astype(BF16)
            d_o = do_ref[:, hs]
            dkv_ref[:, vs] += _dot(p_b, d_o, TN)
            dp = _dot(d_o, kv_ref[:, vs], NT)
            ds = (p * (dp - jnp.sum(dp * p, axis=-1, keepdims=True)) * scale).astype(BF16)
            dq_ref[:, hs] = _dot(ds, kh).astype(BF16)
            dkv_ref[:, hs] += _dot(ds, qh, TN)

    row = pl.BlockSpec((tq, D), lambda i: (i, 0))
    full = pl.BlockSpec((M, 2 * D), lambda i: (0, 0))
    return _pcall(body, name="attn_bwd",
                  out_shape=(jax.ShapeDtypeStruct((T, D), BF16), jax.ShapeDtypeStruct((M, 2 * D), F32)),
                  grid=(T // tq,), in_specs=[row, full, row], out_specs=(row, full),
                  semantics=("arbitrary",))(q, kv, do)


def _shift_down(x, halo, s, t):
    out = pltpu.roll(x, s, 0)
    for j in range(s):
        out = jnp.where(t == j, halo[SUBLANES - s + j:SUBLANES - s + j + 1, :], out)
    return out


def _shift_up(x, halo, s, t, rows):
    out = pltpu.roll(x, rows - s, 0)
    for j in range(s):
        out = jnp.where(t == rows - s + j, halo[j:j + 1, :], out)
    return out


def _conv_tiles(T):
    tt = _tile(T, 128, SUBLANES)
    return tt, tt // SUBLANES, T // SUBLANES


def _conv_fwd(u0, conv_w, conv_b, *, T, F):
    tt, hb, _ = _conv_tiles(T)
    cw = _tile(F, 512)

    def body(u_ref, prev_ref, w_ref, b_ref, f_ref):
        i = pl.program_id(0)
        t = lax.broadcasted_iota(jnp.int32, (tt, 1), 0)

        def conv(cs):
            x = u_ref[:, cs]
            halo = jnp.where(i > 0, prev_ref[:, cs], 0.0)
            return (w_ref[2:3, cs] * x + w_ref[1:2, cs] * _shift_down(x, halo, 1, t)
                    + w_ref[0:1, cs] * _shift_down(x, halo, 2, t) + b_ref[:, cs])

        for j in range(F // cw):
            gate = conv(slice(j * cw, (j + 1) * cw))
            val = conv(slice(F + j * cw, F + (j + 1) * cw))
            f_ref[:, j * cw:(j + 1) * cw] = (gate * _sigmoid(gate) * val).astype(BF16)

    return _pcall(body, name="conv_fwd", out_shape=jax.ShapeDtypeStruct((T, F), BF16),
                  grid=(T // tt,),
                  in_specs=[pl.BlockSpec((tt, 2 * F), lambda i: (i, 0)),
                            pl.BlockSpec((SUBLANES, 2 * F), lambda i: (jnp.maximum(i * hb - 1, 0), 0)),
                            pl.BlockSpec((CONV_W, 2 * F), lambda i: (0, 0)),
                            pl.BlockSpec((1, 2 * F), lambda i: (0, 0))],
                  out_specs=pl.BlockSpec((tt, F), lambda i: (i, 0)),
                  semantics=("parallel",))(u0, u0, conv_w, conv_b)


def _conv_bwd(u0, conv_w, conv_b, df, *, T, F):
    tt, hb, nb = _conv_tiles(T)
    nt = T // tt
    cw = _tile(F, 512)

    def body(u_ref, prev_ref, next_ref, df_ref, dfn_ref, w_ref, b_ref, du0_ref, dw_ref, db_ref):
        i = pl.program_id(0)
        t = lax.broadcasted_iota(jnp.int32, (tt, 1), 0)
        t8 = lax.broadcasted_iota(jnp.int32, (SUBLANES, 1), 0)

        @pl.when(i == 0)
        def _():
            dw_ref[...] = jnp.zeros_like(dw_ref)
            db_ref[...] = jnp.zeros_like(db_ref)

        def conv(cs):
            x = u_ref[:, cs]
            halo = jnp.where(i > 0, prev_ref[:, cs], 0.0)
            x1 = _shift_down(x, halo, 1, t)
            x2 = _shift_down(x, halo, 2, t)
            u = w_ref[2:3, cs] * x + w_ref[1:2, cs] * x1 + w_ref[0:1, cs] * x2 + b_ref[:, cs]
            xn = next_ref[:, cs]
            tail = x[tt - SUBLANES:, :]
            un = (w_ref[2:3, cs] * xn + w_ref[1:2, cs] * _shift_down(xn, tail, 1, t8)
                  + w_ref[0:1, cs] * _shift_down(xn, tail, 2, t8) + b_ref[:, cs])
            return u, un, (x, x1, x2)

        def glu_grad(gate, val, dff):
            sg = _sigmoid(gate)
            return dff * val * (sg * (1.0 + gate * (1.0 - sg))), dff * (gate * sg)

        def finish(cs, du, dun, xs):
            du0 = (w_ref[2:3, cs] * du + w_ref[1:2, cs] * _shift_up(du, dun, 1, t, tt)
                   + w_ref[0:1, cs] * _shift_up(du, dun, 2, t, tt))
            du0_ref[:, cs] = du0.astype(BF16)
            db_ref[:, cs] += jnp.sum(du, axis=0, keepdims=True)
            dw_ref[2:3, cs] += jnp.sum(du * xs[0], axis=0, keepdims=True)
            dw_ref[1:2, cs] += jnp.sum(du * xs[1], axis=0, keepdims=True)
            dw_ref[0:1, cs] += jnp.sum(du * xs[2], axis=0, keepdims=True)

        for j in range(F // cw):
            fs = slice(j * cw, (j + 1) * cw)
            gs, vs = fs, slice(F + j * cw, F + (j + 1) * cw)
            ug, ung, xg = conv(gs)
            uv, unv, xv = conv(vs)
            dug, duv = glu_grad(ug, uv, df_ref[:, fs].astype(F32))
            dung, dunv = glu_grad(ung, unv, dfn_ref[0:SUBLANES, fs].astype(F32))
            dung = jnp.where(i < nt - 1, dung, 0.0)
            dunv = jnp.where(i < nt - 1, dunv, 0.0)
            finish(gs, dug, dung, xg)
            finish(vs, duv, dunv, xv)

    wide = lambda rows, fn: pl.BlockSpec((rows, 2 * F), fn)
    nxt = lambda i: (jnp.minimum((i + 1) * hb, nb - 1), 0)
    return _pcall(body, name="conv_bwd",
                  out_shape=(jax.ShapeDtypeStruct((T, 2 * F), BF16),
                             jax.ShapeDtypeStruct((CONV_W, 2 * F), F32),
                             jax.ShapeDtypeStruct((1, 2 * F), F32)),
                  grid=(nt,),
                  in_specs=[wide(tt, lambda i: (i, 0)),
                            wide(SUBLANES, lambda i: (jnp.maximum(i * hb - 1, 0), 0)),
                            wide(SUBLANES, nxt),
                            pl.BlockSpec((tt, F), lambda i: (i, 0)),
                            pl.BlockSpec((2 * SUBLANES, F),
                                         lambda i: (jnp.minimum((i + 1) * (hb // 2), nb // 2 - 1), 0)),
                            wide(CONV_W, lambda i: (0, 0)), wide(1, lambda i: (0, 0))],
                  out_specs=(wide(tt, lambda i: (i, 0)), wide(CONV_W, lambda i: (0, 0)),
                             wide(1, lambda i: (0, 0))),
                  semantics=("arbitrary",))(u0, u0, u0, df, df, conv_w, conv_b)


def _adamw(w, g, m, v, *, name):
    R, C = w.shape
    tr = _tile(R, max(SUBLANES, (1 << 19) // max(C, 1) // SUBLANES * SUBLANES), SUBLANES)
    c1 = 1.0 / (1.0 - ADAM_B1 ** ADAM_STEP)
    c2 = 1.0 / (1.0 - ADAM_B2 ** ADAM_STEP)

    def body(w_ref, g_ref, m_ref, v_ref, d_ref, mo_ref, vo_ref):
        gv = g_ref[...]
        mn = ADAM_B1 * m_ref[...] + (1.0 - ADAM_B1) * gv
        vn = ADAM_B2 * v_ref[...] + (1.0 - ADAM_B2) * (gv * gv)
        d_ref[...] = -ADAM_LR * ((mn * c1) / (jnp.sqrt(vn * c2) + ADAM_EPS) + ADAM_WD * w_ref[...])
        mo_ref[...] = mn
        vo_ref[...] = vn

    blk = pl.BlockSpec((tr, C), lambda i: (i, 0))
    shp = jax.ShapeDtypeStruct((R, C), F32)
    return _pcall(body, name=name, out_shape=(shp, shp, shp), grid=(R // tr,),
                  in_specs=[blk] * 4, out_specs=(blk,) * 3, semantics=("parallel",))(w, g, m, v)


def _blk(h, C, elems=1 << 19, align=16):
    th = _tile(h, max(align, elems // C // align * align), align)
    if th < h or h * C <= 2 * elems:
        return th, C
    return h, _tile(C, max(LANES, elems // h // LANES * LANES))


def _adamw_halves(w, m, v, g_mine, g_other, c_idx, *, name):
    _, h, C = w.shape
    th, tc = _blk(h, C, align=SUBLANES)
    c1 = 1.0 / (1.0 - ADAM_B1 ** ADAM_STEP)
    c2 = 1.0 / (1.0 - ADAM_B2 ** ADAM_STEP)

    def body(c_ref, w_ref, m_ref, v_ref, gm_ref, go_ref, g_ref, d_ref, mo_ref, vo_ref):
        gv = jnp.where(pl.program_id(0) == c_ref[0], gm_ref[...], go_ref[...])
        mn = ADAM_B1 * m_ref[...] + (1.0 - ADAM_B1) * gv
        vn = ADAM_B2 * v_ref[...] + (1.0 - ADAM_B2) * (gv * gv)
        d_ref[...] = -ADAM_LR * ((mn * c1) / (jnp.sqrt(vn * c2) + ADAM_EPS) + ADAM_WD * w_ref[...])
        g_ref[...] = gv
        mo_ref[...] = mn
        vo_ref[...] = vn

    blk = pl.BlockSpec((None, th, tc), lambda s, i, j, c: (s, i, j))

    def pick(mine):
        def index(s, i, j, c):
            use = (s == c[0]) if mine else (s != c[0])
            return jnp.where(use, i, 0), jnp.where(use, j, 0)
        return pl.BlockSpec((th, tc), index)

    shp = jax.ShapeDtypeStruct((2, h, C), F32)
    return _pcall(body, name=name, out_shape=(shp,) * 4, grid=(2, h // th, C // tc), prefetch=1,
                  in_specs=[blk, blk, blk, pick(True), pick(False)], out_specs=(blk,) * 4,
                  semantics=("parallel", "parallel", "parallel"))(c_idx, w, m, v, g_mine, g_other)


def _mesh_pos():
    x, y, c = lax.axis_index("x"), lax.axis_index("y"), lax.axis_index("c")
    others = [(1 - x, y), (x, 1 - y), (1 - x, 1 - y)]
    return x, y, c, others


def _gather_copies(shards, lands, send_sems, recv_sems):
    x, y, c, others = _mesh_pos()
    me = 2 * x + y
    return [pltpu.make_async_remote_copy(
        src_ref=shards[a].at[c], dst_ref=lands[a].at[me, c],
        send_sem=send_sems.at[3 * a + j], recv_sem=recv_sems.at[3 * a + j],
        device_id=(*chip, c), device_id_type=MESH)
        for a in range(len(shards)) for j, chip in enumerate(others)]


def _pass_copies(shards, zones, send_sems, recv_sems):
    x, y, c, others = _mesh_pos()
    me = 2 * x + y
    copies = []
    for a in range(len(shards)):
        srcs = [zones[a].at[2 * chip[0] + chip[1], c] for chip in others] + [shards[a]]
        dsts = [zones[a].at[2 * chip[0] + chip[1], c] for chip in others] + [zones[a].at[me]]
        copies += [pltpu.make_async_remote_copy(
            src_ref=s, dst_ref=d, send_sem=send_sems.at[4 * a + k], recv_sem=recv_sems.at[4 * a + k],
            device_id=(x, y, 1 - c), device_id_type=MESH) for k, (s, d) in enumerate(zip(srcs, dsts))]
    return copies


def _exchange_copies(grads, recvs, send_sems, recv_sems):
    x, y, c, _ = _mesh_pos()
    return [pltpu.make_async_remote_copy(
        src_ref=grads[a].at[:, 1 - c], dst_ref=recvs[a], send_sem=send_sems.at[a],
        recv_sem=recv_sems.at[a], device_id=(x, y, 1 - c), device_id_type=MESH) for a in range(len(grads))]


def _split_start(copies, per, srcs, zones, after, *, name):
    n = len(srcs)
    HBM = pl.BlockSpec(memory_space=pltpu.HBM)
    SEM = pl.BlockSpec(memory_space=pltpu.SEMAPHORE)

    def body(*refs):
        send_sems, recv_sems = refs[2 * n + 1], refs[2 * n + 2]
        for cp in copies(refs[:n], refs[n:2 * n], send_sems, recv_sems):
            cp.start()
        refs[-1][...] = jnp.zeros_like(refs[-1])

    hbm = lambda a: pltpu.HBM(a.shape, a.dtype)
    res = _pcall(body, name=name,
                 out_shape=(pltpu.SemaphoreType.DMA((per * n,)), pltpu.SemaphoreType.DMA((per * n,)),
                            *[hbm(a) for a in srcs], *[hbm(a) for a in zones],
                            jax.ShapeDtypeStruct((SUBLANES, LANES), F32)),
                 in_specs=[*[HBM] * (2 * n), pl.BlockSpec(memory_space=pl.ANY)],
                 out_specs=(SEM, SEM, *[HBM] * (2 * n), pl.BlockSpec(memory_space=pltpu.VMEM)),
                 aliases={i: 2 + i for i in range(2 * n)}, split_copy=True)(
        *[pltpu.with_memory_space_constraint(a, pltpu.HBM) for a in [*srcs, *zones]], after)
    return res[0], res[1], list(res[2:2 + n]), list(res[2 + n:2 + 2 * n]), res[-1]


def _split_wait(copies, send_sems, recv_sems, srcs, zones, after, *, name):
    n = len(srcs)
    HBM = pl.BlockSpec(memory_space=pltpu.HBM)
    SEM = pl.BlockSpec(memory_space=pltpu.SEMAPHORE)

    def body(*refs):
        for cp in copies(refs[:n], refs[n:2 * n], refs[2 * n], refs[2 * n + 1]):
            cp.wait_send()
            cp.wait_recv()

    hbm = lambda a: pltpu.HBM(a.shape, a.dtype)
    res = _pcall(body, name=name, out_shape=(*[hbm(a) for a in srcs], *[hbm(a) for a in zones]),
                 in_specs=[*[HBM] * (2 * n), SEM, SEM, pl.BlockSpec(memory_space=pl.ANY)],
                 out_specs=tuple([HBM] * (2 * n)), aliases={i: i for i in range(2 * n)},
                 split_copy=True)(*srcs, *zones, send_sems, recv_sems, after)
    return list(res[:n]), list(res[n:])


def _add_halves(grad, recv, c_idx, *, name):
    S, _, h, C = grad.shape
    th, tc = _blk(h, C)

    def body(c_ref, g_ref, r_ref, o_ref):
        o_ref[...] = (g_ref[...].astype(F32) + r_ref[...].astype(F32)).astype(o_ref.dtype)

    return _pcall(body, name=name, out_shape=jax.ShapeDtypeStruct((S, h, C), grad.dtype),
                  grid=(S, h // th, C // tc), prefetch=1,
                  in_specs=[pl.BlockSpec((None, None, th, tc), lambda s, i, j, c: (s, c[0], i, j)),
                            pl.BlockSpec((None, th, tc), lambda s, i, j, c: (s, i, j))],
                  out_specs=pl.BlockSpec((None, th, tc), lambda s, i, j, c: (s, i, j)),
                  semantics=("parallel", "parallel", "parallel"))(c_idx, grad, recv)


def _scatter_copies(srcs, lands, send_sems, recv_sems):
    x, y, c, others = _mesh_pos()
    return [pltpu.make_async_remote_copy(
        src_ref=srcs[a].at[2 * chip[0] + chip[1]], dst_ref=lands[a].at[j],
        send_sem=send_sems.at[3 * a + j], recv_sem=recv_sems.at[3 * a + j],
        device_id=(*chip, c), device_id_type=MESH)
        for a in range(len(srcs)) for j, chip in enumerate(others)]


def _add_chips(sums, recv, chip_idx, *, name):
    _, h, C = sums.shape
    th, tc = _blk(h, C)

    def body(k_ref, s_ref, r_ref, o_ref):
        acc = s_ref[...].astype(F32) + r_ref[0].astype(F32)
        acc = acc + r_ref[1].astype(F32)
        o_ref[...] = acc + r_ref[2].astype(F32)

    return _pcall(body, name=name, out_shape=jax.ShapeDtypeStruct((h, C), F32),
                  grid=(h // th, C // tc), prefetch=1,
                  in_specs=[pl.BlockSpec((None, th, tc), lambda i, j, k: (k[0], i, j)),
                            pl.BlockSpec((3, th, tc), lambda i, j, k: (0, i, j))],
                  out_specs=pl.BlockSpec((th, tc), lambda i, j, k: (i, j)),
                  semantics=("parallel", "parallel"))(chip_idx, sums, recv)


def _swap_halves(halves, *, name):
    n = len(halves)
    ANY = pl.BlockSpec(memory_space=pl.ANY)

    def body(*refs):
        ins, outs = refs[:n], refs[n:2 * n]
        send_sems, recv_sems = refs[2 * n:]
        x, y, c, _ = _mesh_pos()
        copies = [pltpu.make_async_remote_copy(
            src_ref=ins[a], dst_ref=outs[a], send_sem=send_sems.at[a], recv_sem=recv_sems.at[a],
            device_id=(x, y, 1 - c), device_id_type=MESH) for a in range(n)]
        for cp in copies:
            cp.start()
        for cp in copies:
            cp.wait()

    return _pcall(body, name=name,
                  out_shape=[jax.ShapeDtypeStruct(s.shape, s.dtype) for s in halves],
                  in_specs=[ANY] * n, out_specs=[ANY] * n,
                  scratch_shapes=[pltpu.SemaphoreType.DMA((n,)), pltpu.SemaphoreType.DMA((n,))])(*halves)


def _all_reduce_small(buf):
    R, L = buf.shape
    NDEV = 8

    def body(x_ref, sum_ref, all_ref, send_sems, recv_sems, local_sem):
        x, y, c, others = _mesh_pos()
        me, sibling = (x, y, c), (x, y, 1 - c)

        def slot(px, py, pc):
            return all_ref.at[4 * px + 2 * py + pc]

        def copy(k, block, to, src=None):
            return pltpu.make_async_remote_copy(
                src_ref=slot(*block) if src is None else src, dst_ref=slot(*block),
                send_sem=send_sems.at[k], recv_sem=recv_sems.at[k], device_id=to, device_id_type=MESH)

        mine = pltpu.make_async_copy(x_ref, slot(*me), local_sem)
        mine.start()
        first = [copy(0, me, sibling, src=x_ref)]
        first += [copy(1 + j, me, (*chip, c), src=x_ref) for j, chip in enumerate(others)]
        for cp in first:
            cp.start()
        passed = [copy(4 + j, (*chip, c), sibling) for j, chip in enumerate(others)]
        for j, chip in enumerate(others):
            copy(1 + j, (*chip, c), me).wait_recv()
            passed[j].start()
        copy(0, sibling, me).wait_recv()
        for j, chip in enumerate(others):
            copy(4 + j, (*chip, 1 - c), me).wait_recv()
        for cp in first + passed:
            cp.wait_send()
        mine.wait()
        acc = all_ref[0]
        for d in range(1, NDEV):
            acc = acc + all_ref[d]
        sum_ref[...] = acc

    VM = pl.BlockSpec(memory_space=pltpu.VMEM)
    return _pcall(body, name="all_reduce_small",
                  out_shape=(jax.ShapeDtypeStruct((R, L), F32), jax.ShapeDtypeStruct((NDEV, R, L), F32)),
                  in_specs=[VM], out_specs=(VM, VM),
                  scratch_shapes=[pltpu.SemaphoreType.DMA((7,)), pltpu.SemaphoreType.DMA((7,)),
                                  pltpu.SemaphoreType.DMA])(buf)[0]


def _pack(arrs, rows_multiple=16):
    flat = [a.reshape(-1).astype(F32) for a in arrs]
    sizes = [f.shape[0] for f in flat]
    total = sum(sizes)
    per = LANES * rows_multiple
    padded = -(-total // per) * per
    flat.append(jnp.zeros((padded - total,), F32))
    offs = [0]
    for s in sizes:
        offs.append(offs[-1] + s)
    return jnp.concatenate(flat).reshape(padded // LANES, LANES), offs


def _unpack(buf, offs, shapes):
    flat = buf.reshape(-1)
    return [flat[offs[i]:offs[i + 1]].reshape(s) for i, s in enumerate(shapes)]


def kernel(x, mem, g_mix, w_in, w_a2, b_a, g_gla, w_pool, pool_scale, w_branch, w_out, g_cross, g_mem, w_cq, w_ckv, w_co, g_ffn, w_up, conv_w, conv_b, w_down, g_final, loss_target, m_g_mix, m_w_in, m_w_a2, m_b_a, m_g_gla, m_w_pool, m_pool_scale, m_w_branch, m_w_out, m_g_cross, m_g_mem, m_w_cq, m_w_ckv, m_w_co, m_g_ffn, m_w_up, m_conv_w, m_conv_b, m_w_down, m_g_final, v_g_mix, v_w_in, v_w_a2, v_b_a, v_g_gla, v_w_pool, v_pool_scale, v_w_branch, v_w_out, v_g_cross, v_g_mem, v_w_cq, v_w_ckv, v_w_co, v_g_ffn, v_w_up, v_conv_w, v_conv_b, v_w_down, v_g_final):
    weights = dict(g_mix=g_mix, w_in=w_in, w_a2=w_a2, b_a=b_a, g_gla=g_gla, w_pool=w_pool,
                   pool_scale=pool_scale, w_branch=w_branch, w_out=w_out, g_cross=g_cross, g_mem=g_mem,
                   w_cq=w_cq, w_ckv=w_ckv, w_co=w_co, g_ffn=g_ffn, w_up=w_up, conv_w=conv_w,
                   conv_b=conv_b, w_down=w_down, g_final=g_final)
    mom_m = dict(g_mix=m_g_mix, w_in=m_w_in, w_a2=m_w_a2, b_a=m_b_a, g_gla=m_g_gla, w_pool=m_w_pool,
                 pool_scale=m_pool_scale, w_branch=m_w_branch, w_out=m_w_out, g_cross=m_g_cross,
                 g_mem=m_g_mem, w_cq=m_w_cq, w_ckv=m_w_ckv, w_co=m_w_co, g_ffn=m_g_ffn, w_up=m_w_up,
                 conv_w=m_conv_w, conv_b=m_conv_b, w_down=m_w_down, g_final=m_g_final)
    mom_v = dict(g_mix=v_g_mix, w_in=v_w_in, w_a2=v_w_a2, b_a=v_b_a, g_gla=v_g_gla, w_pool=v_w_pool,
                 pool_scale=v_pool_scale, w_branch=v_w_branch, w_out=v_w_out, g_cross=v_g_cross,
                 g_mem=v_g_mem, w_cq=v_w_cq, w_ckv=v_w_ckv, w_co=v_w_co, g_ffn=v_g_ffn, w_up=v_w_up,
                 conv_w=v_conv_w, conv_b=v_conv_b, w_down=v_w_down, g_final=v_g_final)
    order = list(weights)
    big = ["w_in", "w_branch", "w_out", "w_cq", "w_ckv", "w_co", "w_up", "w_down"]
    small_sharded = ["w_a2", "w_pool", "conv_w"]
    small_repl = ["g_mix", "b_a", "g_gla", "pool_scale", "g_cross", "g_mem", "g_ffn", "conv_b", "g_final"]

    xs, ms, tgt = x[0], mem[0], loss_target[0]
    T, D = xs.shape
    M = ms.shape[0]
    DK, DV, PW = b_a.shape[1], g_gla.shape[1], pool_scale.shape[1]
    RANK = w_a2.shape[1]
    F2 = conv_b.shape[1]
    F = F2 // 2
    DIN = N_CHIPS * w_in.shape[2]
    OFF_A = 2 * DK + 2 * DV
    OFF_P = OFF_A + RANK
    RP = LANES
    GW = PW // POOL_GROUPS
    assert PW == DV and 4 * DV == 2 * D and OFF_P + PW + 2 * D == DIN

    cx, cy, cc = lax.axis_index("x"), lax.axis_index("y"), lax.axis_index("c")
    chip = 2 * cx + cy
    c_idx = jnp.reshape(cc, (1,)).astype(jnp.int32)
    chip_idx = jnp.reshape(chip, (1,)).astype(jnp.int32)

    def halves(a):
        return a.reshape(2, a.shape[0] // 2, a.shape[1])

    RS = DIN // N_CHIPS
    RSP = -(-RS // 32) * 32

    def transposed(a):
        return jnp.pad(a.T, ((0, RSP - RS), (0, 0)))

    def in_rows(lo, hi):
        out = []
        for j in range(N_CHIPS):
            s, e = max(lo, RS * j), min(hi, RS * (j + 1))
            if s < e:
                out.append((j * RSP + s - RS * j, j * RSP + e - RS * j))
        return out

    shard2d = {k: (transposed(weights[k][0]) if k == "w_in" else weights[k][0]) for k in big}
    small_pack, small_offs = _pack([weights[k][0] for k in small_sharded], rows_multiple=32)
    shard_halves = {k: halves(shard2d[k].astype(BF16)) for k in big}
    shard_halves["small"] = halves(small_pack)
    flying, passing = {}, {}
    tok = small_pack
    for group, keys in (("in", ["w_in", "small"]), ("mix", ["w_branch", "w_out"]),
                        ("cross", ["w_cq", "w_ckv", "w_co"]), ("up", ["w_up"]), ("down", ["w_down"])):
        srcs = [shard_halves[k] for k in keys]
        zones = [lax.empty((N_CHIPS, *s.shape), s.dtype) for s in srcs]
        s_sems, r_sems, srcs, zones, tok = _split_start(_gather_copies, 3, srcs, zones, tok,
                                                        name=f"gather_start_{group}")
        flying[group] = (keys, s_sems, r_sems, srcs, zones)

    def landed(group, after):
        keys, s_sems, r_sems, srcs, zones = flying[group]
        srcs, zones = _split_wait(_gather_copies, s_sems, r_sems, srcs, zones, after,
                                  name=f"gather_wait_{group}")
        s_sems, r_sems, srcs, zones, token = _split_start(_pass_copies, 4, srcs, zones, after,
                                                          name=f"gather_pass_start_{group}")
        passing[group] = (keys, s_sems, r_sems, srcs, zones)
        return token

    def arrive(group, after):
        keys, s_sems, r_sems, srcs, zones = passing[group]
        _, full = _split_wait(_pass_copies, s_sems, r_sems, srcs, zones, after,
                              name=f"gather_pass_wait_{group}")
        return {k: f.reshape(N_CHIPS, f.shape[1] * f.shape[2], f.shape[3]) for k, f in zip(keys, full)}

    def rows(g):
        return g.reshape(-1, g.shape[2])

    h1, r1 = _rms_fwd(xs, g_mix + tok[0:1, 0:1], name="norm_mix")
    landed("in", h1)
    gw = arrive("in", h1)
    small_all = gw["small"]
    W_in = rows(gw["w_in"])
    W_main = jnp.concatenate([W_in[s:e] for s, e in in_rows(0, OFF_A) + in_rows(OFF_P, DIN)], axis=0)
    W_a = jnp.pad(jnp.concatenate([W_in[s:e] for s, e in in_rows(OFF_A, OFF_P)], axis=0),
                  ((0, RP - RANK), (0, 0)))
    sm = [_unpack(small_all[j], small_offs, [weights[k].shape[1:] for k in small_sharded]) for j in range(N_CHIPS)]
    W_a2 = jnp.concatenate([sm[j][0] for j in range(N_CHIPS)], axis=1)
    W_a2p = jnp.pad(W_a2, ((0, RP - RANK), (0, 0))).astype(BF16)
    W_pool = jnp.concatenate([sm[j][1] for j in range(N_CHIPS)], axis=1).astype(BF16)
    W_conv = jnp.concatenate([sm[j][2] for j in range(N_CHIPS)], axis=1)

    proj = _mm(h1, W_main, "nt", name="proj_main", out_dtype=F32)
    tok = landed("mix", proj)
    a_pad = _mm(h1, W_a, "nt", name="proj_gate_rank", out_dtype=F32, after=tok)
    o_gla, o_raw, states = _gla_fwd(proj, a_pad, W_a2p, b_a, g_gla, T=T, DK=DK, DV=DV)
    o_pool = _pool_fwd(proj, W_pool, pool_scale, T=T, PW=PW, col_block=3)
    gw = arrive("mix", o_pool)
    W_branch, W_out = rows(gw["w_branch"]), rows(gw["w_out"])
    tok = landed("cross", o_pool)
    y_gla = _mm(o_gla, W_branch, "nn", name="branch_gla", out_dtype=F32, K=DV, after=tok)
    y_pool = _mm(o_pool, W_branch, "nn", name="branch_pool", out_dtype=F32, K=PW, b_off=(DV, 0))
    merged = _merge_fwd(y_gla, y_pool, proj, T=T, D=D, col_block=2)
    x1 = _mm(merged, W_out, "nn", name="mix_out", out_dtype=F32, add=xs)

    h2, r2 = _rms_fwd(x1, g_cross, name="norm_cross")
    mem_n, rm = _rms_fwd(ms, g_mem, name="norm_mem")
    gw = arrive("cross", h2)
    W_cq, W_ckv, W_co = rows(gw["w_cq"]), gw["w_ckv"], rows(gw["w_co"])
    qc = _mm(h2, W_cq, "nn", name="cross_q", out_dtype=BF16)
    kv = _mm(mem_n, W_ckv, "nn", name="cross_kv", out_dtype=BF16, b_blocked=True)
    o_att = _attn_fwd(qc, kv, T=T, D=D, M=M)
    x2 = _mm(o_att, W_co, "nn", name="cross_out", out_dtype=F32, add=x1)

    tok = landed("up", x2)
    h3, r3 = _rms_fwd(x2, g_ffn + tok[0:1, 0:1], name="norm_ffn")
    W_up = arrive("up", h3)["w_up"]
    u0 = _mm(h3, W_up, "nn", name="ffn_up", out_dtype=F32, b_blocked=True)
    tok = landed("down", u0)
    f_act = _conv_fwd(u0, W_conv, conv_b + tok[0:1, 0:1], T=T, F=F)
    W_down = rows(arrive("down", f_act)["w_down"])
    x3 =_mm(f_act, W_down, "nn", name="ffn_down", out_dtype=F32, add=x2)

    loss_part, dx3, dx3_b, dg_final = _loss_head(x3, g_final.reshape(1, D), tgt)

    def col_shards(g):
        nb, K, Nb = g.shape
        return g.reshape(nb, 2, K // 2, Nb)

    def row_shards(g):
        R, N = g.shape
        return g.reshape(N_CHIPS, 2, R // N_CHIPS // 2, N)

    exchanging, in_flight = {}, []

    def exchange_start(group, keys, partials, after):
        recvs = [lax.empty((p.shape[0], *p.shape[2:]), p.dtype) for p in partials]
        s_sems, r_sems, partials, recvs, token = _split_start(
            _exchange_copies, 1, partials, recvs, after, name=f"grad_exchange_start_{group}")
        exchanging[group] = (keys, s_sems, r_sems, partials, recvs)
        return token

    def scatter_start(group, after):
        keys, s_sems, r_sems, partials, recvs = exchanging[group]
        partials, recvs = _split_wait(_exchange_copies, s_sems, r_sems, partials, recvs, after,
                                      name=f"grad_exchange_wait_{group}")
        chip_sums = [_add_halves(p, r, c_idx, name=f"grad_add_halves_{k}")
                     for k, p, r in zip(keys, partials, recvs)]
        lands = [lax.empty((3, *s.shape[1:]), s.dtype) for s in chip_sums]
        s_sems, r_sems, sums, lands, token = _split_start(
            _scatter_copies, 3, chip_sums, lands, after, name=f"grad_scatter_start_{group}")
        in_flight.append((group, keys, s_sems, r_sems, sums, lands))
        return token

    df = _mm(dx3_b, W_down, "nt", name="d_ffn_act", out_dtype=BF16)
    dW_down = _mm(f_act, dx3_b, "tn", name="dw_down", out_dtype=BF16)
    du0, dconv_w, dconv_b = _conv_bwd(u0, W_conv, conv_b, df, T=T, F=F)
    dh3 = _mm(du0, W_up, "nt", name="d_ffn_in", out_dtype=F32, b_blocked=True)
    dW_up = _mm(h3, du0, "tn", name="dw_up", out_dtype=BF16, out_blocks=N_CHIPS)
    tok = exchange_start("ffn", ["w_down", "w_up"], [row_shards(dW_down), col_shards(dW_up)], dh3)
    dx2, dx2_b, dg_ffn = _rms_bwd(dh3, x2, r3 + tok[0:1, 0:1], g_ffn, dx3, name="norm_ffn_bwd")

    do_att = _mm(dx2_b, W_co, "nt", name="d_cross_o", out_dtype=BF16)
    dW_co = _mm(o_att, dx2_b, "tn", name="dw_co", out_dtype=BF16)
    tok = scatter_start("ffn", dW_co)
    dq, dkv = _attn_bwd(qc, kv, do_att, T=T, D=D, M=M)
    dkv_b = dkv.astype(BF16)
    dW_cq = _mm(h2, dq, "tn", name="dw_cq", out_dtype=BF16, after=tok)
    dh2 = _mm(dq, W_cq, "nt", name="d_cross_in", out_dtype=F32)
    dW_ckv = _mm(mem_n, dkv_b, "tn", name="dw_ckv", out_dtype=BF16, out_blocks=N_CHIPS)
    dmem_n = _mm(dkv_b, W_ckv, "nt", name="d_mem", out_dtype=F32, b_blocked=True)
    tok = exchange_start("cross", ["w_co", "w_cq", "w_ckv"],
                         [row_shards(dW_co), row_shards(dW_cq), col_shards(dW_ckv)], dmem_n)
    _, _, dg_mem = _rms_bwd(dmem_n, ms, rm, g_mem, None, name="norm_mem_bwd")
    dx1, dx1_b, dg_cross = _rms_bwd(dh2, x1, r2 + tok[0:1, 0:1], g_cross, dx2, name="norm_cross_bwd")

    dmerged = _mm(dx1_b, W_out, "nt", name="d_merged", out_dtype=F32)
    dW_out = _mm(merged, dx1_b, "tn", name="dw_out", out_dtype=BF16)
    tok = scatter_start("cross", dW_out)
    dy_gla, dy_pool, dgates = _merge_bwd(dmerged, y_gla, y_pool, proj, T=T, D=D, col_block=2)
    dW_br_gla = _mm(o_gla, dy_gla, "tn", name="dw_branch_gla", out_dtype=BF16, after=tok)
    dW_br_pool = _mm(o_pool, dy_pool, "tn", name="dw_branch_pool", out_dtype=BF16)
    tok = exchange_start("mix", ["w_out", "w_branch"],
                         [row_shards(dW_out), row_shards(jnp.concatenate([dW_br_gla, dW_br_pool], axis=0))],
                         dW_br_pool)
    do_gla = _mm(dy_gla, W_branch, "nt", name="d_o_gla", out_dtype=F32, N=DV, after=tok)
    do_pool = _mm(dy_pool, W_branch, "nt", name="d_o_pool", out_dtype=F32, N=PW, b_off=(DV, 0))
    dp, dw_pool, dpool_scale = _pool_bwd(proj, W_pool, pool_scale, do_pool, T=T, PW=PW, col_block=3)
    tok = scatter_start("mix", dp)
    dqkvr, da_pad, dw2, db_a, dg_gla = _gla_bwd(proj, a_pad, W_a2p, b_a + tok[0:1, 0:1], g_gla, o_raw, states,
                                               do_gla, T=T, DK=DK, DV=DV)
    dproj = jnp.concatenate([dqkvr, dp, dgates], axis=1)
    dW_main = _mm(dproj, h1, "tn", name="dw_in_main", out_dtype=BF16)
    dW_a = _mm(da_pad, h1, "tn", name="dw_in_rank", out_dtype=BF16)

    def dw_in_rows(lo, hi):
        parts = []
        if lo < OFF_A:
            parts.append(dW_main[lo:min(hi, OFF_A)])
        if lo < OFF_P and hi > OFF_A:
            parts.append(dW_a[max(lo, OFF_A) - OFF_A:min(hi, OFF_P) - OFF_A])
        if hi > OFF_P:
            parts.append(dW_main[max(lo, OFF_P) - RANK:hi - RANK])
        return parts

    dW_in, at = 0, 0
    for j in range(N_CHIPS):
        for p in dw_in_rows(RS * j, RS * (j + 1)):
            dW_in = dW_in + jnp.pad(p, ((at, N_CHIPS * RSP - at - p.shape[0]), (0, 0)))
            at += p.shape[0]
        at += RSP - RS
    tok = exchange_start("in", ["w_in"], [row_shards(dW_in)], dW_a)
    dh1 = _mm(dproj, W_main, "nn", name="d_mix_in_main", out_dtype=F32, after=tok)
    dh1 = _mm(da_pad, W_a, "nn", name="d_mix_in_rank", out_dtype=F32, add=dh1)
    dx0, _, dg_mix = _rms_bwd(dh1, xs, r1, g_mix, dx1, name="norm_mix_bwd")

    grads = {}

    small_grads = [loss_part, dg_mix, db_a, dg_gla, dpool_scale, dg_cross, dg_mem, dg_ffn, dconv_b, dg_final,
                   dw2[:RANK], dw_pool, dconv_w]
    small_buf, offs = _pack(small_grads)
    small_sum = _all_reduce_small(small_buf)
    red = _unpack(small_sum, offs, [g.shape for g in small_grads])
    loss = red[0][0, 0]
    for k, g in zip(small_repl, red[1:10]):
        grads[k] = g.reshape(weights[k].shape)
    nb = DK // N_CHIPS
    grads["w_a2"] = lax.dynamic_slice_in_dim(red[10], chip * nb, nb, axis=1)[None]
    nb = GW // N_CHIPS
    grads["w_pool"] = lax.dynamic_slice_in_dim(red[11], chip * nb, nb, axis=1)[None]
    nb = F2 // N_CHIPS
    grads["conv_w"] = lax.dynamic_slice_in_dim(red[12], chip * nb, nb, axis=1)[None]

    delta, new_m, new_v = {}, {}, {}

    def whole(k, a):
        a = a.reshape(-1, a.shape[2])
        return (a[:RS].T if k == "w_in" else a)[None]

    scatter_start("in", small_sum)
    after = in_flight[-1][4][0]

    for group, keys, s_sems, r_sems, sums, lands in in_flight:
        sums, from_chips = _split_wait(_scatter_copies, s_sems, r_sems, sums, lands, after,
                                       name=f"grad_scatter_wait_{group}")
        half_sums = [_add_chips(s, r, chip_idx, name=f"grad_add_chips_{k}") for k, s, r in zip(keys, sums, from_chips)]
        other_sums = _swap_halves(half_sums, name=f"grad_swap_halves_{group}")
        for k, mine, other in zip(keys, half_sums, other_sums):
            wmv = [halves(transposed(src[k][0]) if k == "w_in" else src[k][0]) for src in (weights, mom_m, mom_v)]
            res = _adamw_halves(*wmv, mine, other, c_idx, name=f"adamw_{k}")
            grads[k], delta[k], new_m[k], new_v[k] = (whole(k, a) for a in res)
            after = res[1]
    small = small_repl + small_sharded
    packs = [_pack([src[k] for k in small])[0] for src in (weights, grads, mom_m, mom_v)]
    _, offs = _pack([weights[k] for k in small])
    outs = _adamw(*packs, name="adamw_small")
    for res, o in zip((delta, new_m, new_v), outs):
        for k, a in zip(small, _unpack(o, offs, [weights[k].shape for k in small])):
            res[k] = a

    return (loss, dx0[None], *[grads[k] for k in order], *[delta[k] for k in order],
            *[new_m[k] for k in order], *[new_v[k] for k in order])
```

```python
import functools

import jax
import jax.numpy as jnp
from jax import lax
from jax.experimental import pallas as pl
from jax.experimental.pallas import tpu as pltpu

F32 = jnp.float32
BF16 = jnp.bfloat16
MESH = pl.DeviceIdType.MESH
HIGHEST = lax.Precision.HIGHEST

EPS = 1e-6
GLA_HEADS = 4
GLA_CHUNK = 64
GLA_GATE_NORM = 16.0
POOL_GROUPS = 4
CROSS_HEADS = 4
CONV_W = 3
N_CHIPS = 4
LANES = 128
SUBLANES = 8
VMEM_LIMIT = 56 << 20

ADAM_LR = 0.001
ADAM_B1 = 0.9
ADAM_B2 = 0.999
ADAM_EPS = 1e-08
ADAM_WD = 0.01
ADAM_STEP = 10

NN = (((1,), (0,)), ((), ()))
NT = (((1,), (1,)), ((), ()))
TN = (((0,), (0,)), ((), ()))


def _dot(a, b, dn=NN, precision=None):
    return lax.dot_general(a, b, dn, precision=precision, preferred_element_type=F32)


def _tile(n, pref, align=LANES):
    t = (min(pref, n) // align) * align
    while t >= align:
        if n % t == 0:
            return t
        t -= align
    return n


def _pcall(body, *, name, out_shape, grid=(), in_specs=None, out_specs=None, scratch_shapes=(),
           semantics=None, prefetch=0, aliases=None, split_copy=False):
    params = dict(vmem_limit_bytes=VMEM_LIMIT)
    if semantics is not None:
        params["dimension_semantics"] = semantics
    if split_copy:
        params["has_side_effects"] = pltpu.SideEffectType.DATAFLOW_SIDE_EFFECTING
    if prefetch:
        grid_spec = pltpu.PrefetchScalarGridSpec(
            num_scalar_prefetch=prefetch, grid=grid, in_specs=in_specs, out_specs=out_specs,
            scratch_shapes=scratch_shapes)
        return pl.pallas_call(body, name=name, out_shape=out_shape, grid_spec=grid_spec,
                              compiler_params=pltpu.CompilerParams(**params))
    kw = {}
    if aliases is not None:
        kw["input_output_aliases"] = aliases
    if in_specs is not None:
        kw["in_specs"] = in_specs
    if out_specs is not None:
        kw["out_specs"] = out_specs
    return pl.pallas_call(body, name=name, out_shape=out_shape, grid=grid,
                          scratch_shapes=scratch_shapes,
                          compiler_params=pltpu.CompilerParams(**params), **kw)


def _sigmoid(x):
    return 1.0 / (1.0 + jnp.exp(-x))


def _log_sigmoid(x):
    return jnp.minimum(x, 0.0) - jnp.log(1.0 + jnp.exp(-jnp.abs(x)))


def _mm(a, b, mode, *, name, out_dtype, M=None, N=None, K=None, a_off=(0, 0), b_off=(0, 0),
        add=None, b_blocked=False, out_blocks=0, after=None, tm=1536, tn=1536, tk=2048):
    if b_blocked:
        nb, R, Cb = b.shape
        b_rows, b_cols = R, nb * Cb
    else:
        b_rows, b_cols = b.shape
    if mode == "nn":
        M = M or a.shape[0]; K = K or a.shape[1]; N = N or b_cols
    elif mode == "nt":
        M = M or a.shape[0]; K = K or a.shape[1]; N = N or b_rows
    else:
        K = K or a.shape[0]; M = M or a.shape[1]; N = N or b_cols
    tm = _tile(M, tm, LANES if mode == "tn" else 16)
    tn = _tile(Cb if (b_blocked and mode != "nt") else (N // out_blocks if out_blocks else N), tn)
    tk = _tile(Cb if (b_blocked and mode == "nt") else K, tk)
    nk = K // tk
    dn = {"nn": NN, "nt": NT, "tn": TN}[mode]

    def off(o, t):
        assert o % t == 0, (name, o, t)
        return o // t

    if mode == "tn":
        ar, ac = off(a_off[0], tk), off(a_off[1], tm)
        a_spec = pl.BlockSpec((tk, tm), lambda i, j, k: (k + ar, i + ac))
    else:
        ar, ac = off(a_off[0], tm), off(a_off[1], tk)
        a_spec = pl.BlockSpec((tm, tk), lambda i, j, k: (i + ar, k + ac))
    if b_blocked and mode == "nt":
        per = Cb // tk
        b_spec = pl.BlockSpec((None, tn, tk), lambda i, j, k: (k // per, j, k % per))
    elif b_blocked:
        per = Cb // tn
        b_spec = pl.BlockSpec((None, tk, tn), lambda i, j, k: (j // per, k, j % per))
    elif mode == "nt":
        br, bc = off(b_off[0], tn), off(b_off[1], tk)
        b_spec = pl.BlockSpec((tn, tk), lambda i, j, k: (j + br, k + bc))
    else:
        br, bc = off(b_off[0], tk), off(b_off[1], tn)
        b_spec = pl.BlockSpec((tk, tn), lambda i, j, k: (k + br, j + bc))
    if out_blocks:
        per_o = N // out_blocks // tn
        o_spec = pl.BlockSpec((None, tm, tn), lambda i, j, k: (j // per_o, i, j % per_o))
        out_shape = jax.ShapeDtypeStruct((out_blocks, M, N // out_blocks), out_dtype)
    else:
        o_spec = pl.BlockSpec((tm, tn), lambda i, j, k: (i, j))
        out_shape = jax.ShapeDtypeStruct((M, N), out_dtype)
    in_specs = [a_spec, b_spec]
    args = [a, b]
    if add is not None:
        assert not out_blocks
        in_specs.append(o_spec)
        args.append(add)
    if after is not None:
        in_specs.append(pl.BlockSpec(memory_space=pl.ANY))
        args.append(after)
    n_in = len(args)

    def finish(r, refs):
        if add is not None:
            r = r + refs[2][...]
        o_ref = refs[n_in]
        o_ref[...] = r.astype(o_ref.dtype)

    def body_one(*refs):
        finish(_dot(refs[0][...].astype(BF16), refs[1][...].astype(BF16), dn), refs)

    def body_acc(*refs):
        acc_ref = refs[-1]
        k = pl.program_id(2)

        @pl.when(k == 0)
        def _():
            acc_ref[...] = jnp.zeros_like(acc_ref)

        acc_ref[...] += _dot(refs[0][...].astype(BF16), refs[1][...].astype(BF16), dn)

        @pl.when(k == nk - 1)
        def _():
            finish(acc_ref[...], refs)

    return _pcall(body_one if nk == 1 else body_acc, name=name, out_shape=out_shape,
                  grid=(M // tm, N // tn, nk), in_specs=in_specs, out_specs=o_spec,
                  scratch_shapes=[] if nk == 1 else [pltpu.VMEM((tm, tn), F32)],
                  semantics=("parallel", "parallel", "arbitrary"))(*args)


def _rms_fwd(x, g, *, name):
    T, D = x.shape
    tr = _tile(T, 128, 16)

    def body(x_ref, g_ref, h_ref, r_ref):
        xv = x_ref[...]
        r = lax.rsqrt(jnp.mean(xv * xv, axis=-1, keepdims=True) + EPS)
        h_ref[...] = (xv * r * g_ref[...]).astype(h_ref.dtype)
        r_ref[...] = r

    row = pl.BlockSpec((tr, D), lambda i: (i, 0))
    return _pcall(body, name=name,
                  out_shape=(jax.ShapeDtypeStruct((T, D), BF16), jax.ShapeDtypeStruct((T, 1), F32)),
                  grid=(T // tr,),
                  in_specs=[row, pl.BlockSpec((1, D), lambda i: (0, 0))],
                  out_specs=(row, pl.BlockSpec((tr, 1), lambda i: (i, 0))),
                  semantics=("parallel",))(x, g)


def _rms_bwd(dh, x, rstd, g, dres, *, name):
    T, D = x.shape
    tr = _tile(T, 128, 16)
    has_res = dres is not None

    def body(*refs):
        if has_res:
            dh_ref, x_ref, r_ref, g_ref, res_ref, dx_ref, dxb_ref, dg_ref = refs
        else:
            dh_ref, x_ref, r_ref, g_ref, dx_ref, dxb_ref, dg_ref = refs
        r = r_ref[...]
        xh = x_ref[...] * r
        dhv = dh_ref[...].astype(F32)
        dxh = dhv * g_ref[...]
        m = jnp.mean(dxh * xh, axis=-1, keepdims=True)
        dx = r * (dxh - xh * m)
        if has_res:
            dx = dx + res_ref[...]
        dx_ref[...] = dx
        dxb_ref[...] = dx.astype(BF16)

        @pl.when(pl.program_id(0) == 0)
        def _():
            dg_ref[...] = jnp.zeros_like(dg_ref)

        dg_ref[...] += jnp.sum(dhv * xh, axis=0, keepdims=True)

    row = pl.BlockSpec((tr, D), lambda i: (i, 0))
    vec = pl.BlockSpec((1, D), lambda i: (0, 0))
    in_specs = [row, row, pl.BlockSpec((tr, 1), lambda i: (i, 0)), vec]
    args = [dh, x, rstd, g]
    if has_res:
        in_specs.append(row)
        args.append(dres)
    return _pcall(body, name=name,
                  out_shape=(jax.ShapeDtypeStruct((T, D), F32), jax.ShapeDtypeStruct((T, D), BF16),
                             jax.ShapeDtypeStruct((1, D), F32)),
                  grid=(T // tr,), in_specs=in_specs, out_specs=(row, row, vec),
                  semantics=("arbitrary",))(*args)


def _loss_head(x3, g, tgt):
    T, D = x3.shape
    tr = _tile(T, 128, 16)

    def body(x_ref, g_ref, t_ref, loss_ref, dx_ref, dxb_ref, dg_ref):
        xv = x_ref[...]
        gv = g_ref[...]
        r = lax.rsqrt(jnp.mean(xv * xv, axis=-1, keepdims=True) + EPS)
        xh = xv * r
        err = xh * gv - t_ref[...]
        dy = err * (1.0 / D)
        dxh = dy * gv
        m = jnp.mean(dxh * xh, axis=-1, keepdims=True)
        dx = r * (dxh - xh * m)
        dx_ref[...] = dx
        dxb_ref[...] = dx.astype(BF16)

        @pl.when(pl.program_id(0) == 0)
        def _():
            dg_ref[...] = jnp.zeros_like(dg_ref)
            loss_ref[...] = jnp.zeros_like(loss_ref)

        dg_ref[...] += jnp.sum(dy * xh, axis=0, keepdims=True)
        part = 0.5 * jnp.sum(jnp.mean(err * err, axis=-1, keepdims=True), axis=0, keepdims=True)
        loss_ref[...] += jnp.broadcast_to(part, loss_ref.shape)

    row = pl.BlockSpec((tr, D), lambda i: (i, 0))
    vec = pl.BlockSpec((1, D), lambda i: (0, 0))
    return _pcall(body, name="loss_head",
                  out_shape=(jax.ShapeDtypeStruct((1, LANES), F32), jax.ShapeDtypeStruct((T, D), F32),
                             jax.ShapeDtypeStruct((T, D), BF16), jax.ShapeDtypeStruct((1, D), F32)),
                  grid=(T // tr,), in_specs=[row, vec, row],
                  out_specs=(pl.BlockSpec((1, LANES), lambda i: (0, 0)), row, row, vec),
                  semantics=("arbitrary",))(x3, g, tgt)


def _gla_chunk_terms(qk, a_ref, w2_ref, ba_ref, DK):
    C = qk.shape[0]
    gp = _dot(a_ref[...].astype(BF16), w2_ref[...]) + ba_ref[...]
    la = _log_sigmoid(gp) * (1.0 / GLA_GATE_NORM)
    row = lax.broadcasted_iota(jnp.int32, (C, C), 0)
    col = lax.broadcasted_iota(jnp.int32, (C, C), 1)
    causal = row >= col
    b = _dot(causal.astype(F32), la, precision=HIGHEST)
    return gp, b, causal


def _gla_fwd(proj, a_pad, w2, b_a, g_gla, *, T, DK, DV):
    assert 2 * DK == DV
    H = GLA_HEADS
    HK, HV = DK // H, DV // H
    C = GLA_CHUNK
    n = T // C
    RP = a_pad.shape[1]
    scale = HK ** -0.5

    def body(qk_ref, v_ref, r_ref, a_ref, w2_ref, ba_ref, gg_ref, og_ref, oraw_ref, st_ref, s_ref):
        @pl.when(pl.program_id(0) == 0)
        def _():
            s_ref[...] = jnp.zeros_like(s_ref)

        st_ref[...] = s_ref[...]
        qk = qk_ref[...]
        _, b, causal = _gla_chunk_terms(qk, a_ref, w2_ref, ba_ref, DK)
        for h in range(H):
            ks = slice(h * HK, (h + 1) * HK)
            vs = slice(h * HV, (h + 1) * HV)
            bh = b[:, ks]
            b_last = bh[C - 1:C, :]
            qt = qk[:, ks] * scale * jnp.exp(bh)
            kh = qk[:, DK + h * HK:DK + (h + 1) * HK]
            kt = kh * jnp.exp(-bh)
            khat = kh * jnp.exp(b_last - bh)
            a_mat = jnp.where(causal, _dot(qt, kt, NT, HIGHEST), 0.0)
            vh = v_ref[:, vs]
            s_t = s_ref[h]
            o = _dot(a_mat, vh, NN, HIGHEST) + _dot(qt, s_t, NT, HIGHEST)
            s_ref[h] = s_t * jnp.exp(b_last) + _dot(vh, khat, TN, HIGHEST)
            rs = lax.rsqrt(jnp.mean(o * o, axis=-1, keepdims=True) + EPS)
            rr = r_ref[:, vs]
            og = o * rs * gg_ref[:, vs] * (rr * _sigmoid(rr))
            oraw_ref[:, vs] = o
            og_ref[:, vs] = og.astype(BF16)

    blk = lambda j: pl.BlockSpec((C, DV), lambda i: (i, j))
    full = lambda s: pl.BlockSpec(s, lambda i: (0,) * len(s))
    return _pcall(
        body, name="gla_fwd",
        out_shape=(jax.ShapeDtypeStruct((T, DV), BF16), jax.ShapeDtypeStruct((T, DV), F32),
                   jax.ShapeDtypeStruct((n, H, HV, HK), F32)),
        grid=(n,),
        in_specs=[blk(0), blk(1), blk(2), pl.BlockSpec((C, RP), lambda i: (i, 0)),
                  full((RP, DK)), full((1, DK)), full((1, DV))],
        out_specs=(blk(0), blk(0), pl.BlockSpec((None, H, HV, HK), lambda i: (i, 0, 0, 0))),
        scratch_shapes=[pltpu.VMEM((H, HV, HK), F32)],
        semantics=("arbitrary",))(proj, proj, proj, a_pad, w2, b_a, g_gla)


def _gla_bwd(proj, a_pad, w2, b_a, g_gla, o_raw, states, do_gla, *, T, DK, DV):
    H = GLA_HEADS
    HK, HV = DK // H, DV // H
    C = GLA_CHUNK
    n = T // C
    RP = a_pad.shape[1]
    scale = HK ** -0.5

    def body(qk_ref, v_ref, r_ref, a_ref, w2_ref, ba_ref, gg_ref, oraw_ref, st_ref, dog_ref,
             dqkvr_ref, da_ref, dw2_ref, dba_ref, dgg_ref, ds_ref):
        @pl.when(pl.program_id(0) == 0)
        def _():
            ds_ref[...] = jnp.zeros_like(ds_ref)
            dw2_ref[...] = jnp.zeros_like(dw2_ref)
            dba_ref[...] = jnp.zeros_like(dba_ref)
            dgg_ref[...] = jnp.zeros_like(dgg_ref)

        qk = qk_ref[...]
        gp, b, causal = _gla_chunk_terms(qk, a_ref, w2_ref, ba_ref, DK)
        row = lax.broadcasted_iota(jnp.int32, (C, C), 0)
        col = lax.broadcasted_iota(jnp.int32, (C, C), 1)
        upper = (col >= row).astype(F32)
        dla_parts = []
        for h in range(H):
            ks = slice(h * HK, (h + 1) * HK)
            vs = slice(h * HV, (h + 1) * HV)
            bh = b[:, ks]
            b_last = bh[C - 1:C, :]
            eb = jnp.exp(bh)
            emb = jnp.exp(-bh)
            ehat = jnp.exp(b_last - bh)
            e_last = jnp.exp(b_last)
            qt = qk[:, ks] * scale * eb
            kh = qk[:, DK + h * HK:DK + (h + 1) * HK]
            kt = kh * emb
            khat = kh * ehat
            a_mat = jnp.where(causal, _dot(qt, kt, NT, HIGHEST), 0.0)
            vh = v_ref[:, vs]
            o = oraw_ref[:, vs]
            rs = lax.rsqrt(jnp.mean(o * o, axis=-1, keepdims=True) + EPS)
            on = o * rs
            gg = gg_ref[:, vs]
            rr = r_ref[:, vs]
            sg = _sigmoid(rr)
            d_out = dog_ref[:, vs]
            dr = d_out * (on * gg) * (sg * (1.0 + rr * (1.0 - sg)))
            d_og = d_out * (rr * sg)
            dgg_ref[:, vs] += jnp.sum(d_og * on, axis=0, keepdims=True)
            d_on = d_og * gg
            d_o = rs * (d_on - on * jnp.mean(d_on * on, axis=-1, keepdims=True))
            s_t = st_ref[h]
            ds_t = ds_ref[h]
            d_a = jnp.where(causal, _dot(d_o, vh, NT, HIGHEST), 0.0)
            dv = _dot(a_mat, d_o, TN, HIGHEST) + _dot(khat, ds_t, NT, HIGHEST)
            dqt = _dot(d_a, kt, NN, HIGHEST) + _dot(d_o, s_t, NN, HIGHEST)
            dkt = _dot(d_a, qt, TN, HIGHEST)
            dkhat = _dot(vh, ds_t, NN, HIGHEST)
            ds_ref[h] = ds_t * e_last + _dot(d_o, qt, TN, HIGHEST)
            dq = dqt * eb * scale
            dk = dkt * emb + dkhat * ehat
            db = dqt * qt - dkt * kt - dkhat * khat
            d_last = (jnp.sum(dkhat * khat, axis=0, keepdims=True)
                      + e_last * jnp.sum(ds_t * s_t, axis=0, keepdims=True))
            dla_parts.append(_dot(upper, db, NN, HIGHEST) + d_last)
            dqkvr_ref[:, ks] = dq.astype(BF16)
            dqkvr_ref[:, DK + h * HK:DK + (h + 1) * HK] = dk.astype(BF16)
            dqkvr_ref[:, DV + h * HV:DV + (h + 1) * HV] = dv.astype(BF16)
            dqkvr_ref[:, 2 * DV + h * HV:2 * DV + (h + 1) * HV] = dr.astype(BF16)
        dla = jnp.concatenate(dla_parts, axis=1)
        dgp = dla * (1.0 / GLA_GATE_NORM) * _sigmoid(-gp)
        dba_ref[...] += jnp.sum(dgp, axis=0, keepdims=True)
        dgp_b = dgp.astype(BF16)
        dw2_ref[...] += _dot(a_ref[...].astype(BF16), dgp_b, TN)
        da_ref[...] = _dot(dgp_b, w2_ref[...], NT).astype(BF16)

    rev = lambda j: pl.BlockSpec((C, DV), lambda i: (n - 1 - i, j))
    full = lambda s: pl.BlockSpec(s, lambda i: (0,) * len(s))
    return _pcall(
        body, name="gla_bwd",
        out_shape=(jax.ShapeDtypeStruct((T, 3 * DV), BF16), jax.ShapeDtypeStruct((T, RP), BF16),
                   jax.ShapeDtypeStruct((RP, DK), F32), jax.ShapeDtypeStruct((1, DK), F32),
                   jax.ShapeDtypeStruct((1, DV), F32)),
        grid=(n,),
        in_specs=[rev(0), rev(1), rev(2), pl.BlockSpec((C, RP), lambda i: (n - 1 - i, 0)),
                  full((RP, DK)), full((1, DK)), full((1, DV)), rev(0),
                  pl.BlockSpec((None, H, HV, HK), lambda i: (n - 1 - i, 0, 0, 0)), rev(0)],
        out_specs=(pl.BlockSpec((C, 3 * DV), lambda i: (n - 1 - i, 0)),
                   pl.BlockSpec((C, RP), lambda i: (n - 1 - i, 0)),
                   full((RP, DK)), full((1, DK)), full((1, DV))),
        scratch_shapes=[pltpu.VMEM((H, HV, HK), F32)],
        semantics=("arbitrary",))(proj, proj, proj, a_pad, w2, b_a, g_gla, o_raw, states, do_gla)


def _pool_windows(p, g, T):
    t = lax.broadcasted_iota(jnp.int32, (T, 1), 0)
    s = p
    for lvl in range(POOL_GROUPS):
        sh = 1 << lvl
        nxt = s + jnp.where(t >= sh, pltpu.roll(s, sh, 0), 0.0)
        s = jnp.where(lvl <= g, nxt, s)
    win = jnp.left_shift(2, g)
    inv = 1.0 / jnp.minimum(t + 1, win).astype(F32)
    return s * inv - p, inv


def _pool_fwd(proj, w_pool, scale, *, T, PW, col_block):
    GW = PW // POOL_GROUPS
    per = PW // GW

    def body(p_ref, w_ref, s_ref, o_ref):
        g = pl.program_id(0)
        pooled, _ = _pool_windows(p_ref[...], g, T)
        mixed = _dot(pooled.astype(BF16), w_ref[...])
        o_ref[...] = (mixed * s_ref[...]).astype(BF16)

    return _pcall(body, name="pool_fwd", out_shape=jax.ShapeDtypeStruct((T, PW), BF16),
                  grid=(POOL_GROUPS,),
                  in_specs=[pl.BlockSpec((T, GW), lambda g: (0, col_block * per + g)),
                            pl.BlockSpec((None, GW, GW), lambda g: (g, 0, 0)),
                            pl.BlockSpec((1, GW), lambda g: (0, g))],
                  out_specs=pl.BlockSpec((T, GW), lambda g: (0, g)),
                  semantics=("parallel",))(proj, w_pool, scale)


def _pool_bwd(proj, w_pool, scale, do_pool, *, T, PW, col_block):
    GW = PW // POOL_GROUPS
    per = PW // GW

    def body(p_ref, w_ref, s_ref, do_ref, dp_ref, dw_ref, dsc_ref):
        g = pl.program_id(0)
        pooled, inv = _pool_windows(p_ref[...], g, T)
        pooled_b = pooled.astype(BF16)
        w = w_ref[...]
        mixed = _dot(pooled_b, w)
        d_out = do_ref[...]
        dsc_ref[...] = jnp.sum(d_out * mixed, axis=0, keepdims=True)
        dmixed = (d_out * s_ref[...]).astype(BF16)
        dw_ref[...] = _dot(pooled_b, dmixed, TN)
        dpooled = _dot(dmixed, w, NT)
        t = lax.broadcasted_iota(jnp.int32, (T, 1), 0)
        s = dpooled * inv
        for lvl in range(POOL_GROUPS):
            sh = 1 << lvl
            nxt = s + jnp.where(t < T - sh, pltpu.roll(s, T - sh, 0), 0.0)
            s = jnp.where(lvl <= g, nxt, s)
        dp_ref[...] = (s - dpooled).astype(BF16)

    return _pcall(body, name="pool_bwd",
                  out_shape=(jax.ShapeDtypeStruct((T, PW), BF16),
                             jax.ShapeDtypeStruct((POOL_GROUPS, GW, GW), F32),
                             jax.ShapeDtypeStruct((1, PW), F32)),
                  grid=(POOL_GROUPS,),
                  in_specs=[pl.BlockSpec((T, GW), lambda g: (0, col_block * per + g)),
                            pl.BlockSpec((None, GW, GW), lambda g: (g, 0, 0)),
                            pl.BlockSpec((1, GW), lambda g: (0, g)),
                            pl.BlockSpec((T, GW), lambda g: (0, g))],
                  out_specs=(pl.BlockSpec((T, GW), lambda g: (0, g)),
                             pl.BlockSpec((None, GW, GW), lambda g: (g, 0, 0)),
                             pl.BlockSpec((1, GW), lambda g: (0, g))),
                  semantics=("parallel",))(proj, w_pool, scale, do_pool)


def _merge_fwd(y_gla, y_pool, proj, *, T, D, col_block):
    tr = _tile(T, 128, 16)

    def body(yg_ref, yp_ref, g1_ref, g2_ref, o_ref):
        o_ref[...] = (_sigmoid(g1_ref[...]) * yg_ref[...]
                      + _sigmoid(g2_ref[...]) * yp_ref[...]).astype(BF16)

    row = pl.BlockSpec((tr, D), lambda i: (i, 0))
    return _pcall(body, name="merge_fwd", out_shape=jax.ShapeDtypeStruct((T, D), BF16),
                  grid=(T // tr,),
                  in_specs=[row, row, pl.BlockSpec((tr, D), lambda i: (i, col_block)),
                            pl.BlockSpec((tr, D), lambda i: (i, col_block + 1))],
                  out_specs=row, semantics=("parallel",))(y_gla, y_pool, proj, proj)


def _merge_bwd(dmerged, y_gla, y_pool, proj, *, T, D, col_block):
    tr = _tile(T, 128, 16)

    def body(dm_ref, yg_ref, yp_ref, g1_ref, g2_ref, dyg_ref, dyp_ref, dg_ref):
        dm = dm_ref[...]
        s1 = _sigmoid(g1_ref[...])
        s2 = _sigmoid(g2_ref[...])
        dyg_ref[...] = (dm * s1).astype(BF16)
        dyp_ref[...] = (dm * s2).astype(BF16)
        dg_ref[:, :D] = (dm * yg_ref[...] * s1 * (1.0 - s1)).astype(BF16)
        dg_ref[:, D:] = (dm * yp_ref[...] * s2 * (1.0 - s2)).astype(BF16)

    row = pl.BlockSpec((tr, D), lambda i: (i, 0))
    return _pcall(body, name="merge_bwd",
                  out_shape=(jax.ShapeDtypeStruct((T, D), BF16), jax.ShapeDtypeStruct((T, D), BF16),
                             jax.ShapeDtypeStruct((T, 2 * D), BF16)),
                  grid=(T // tr,),
                  in_specs=[row, row, row, pl.BlockSpec((tr, D), lambda i: (i, col_block)),
                            pl.BlockSpec((tr, D), lambda i: (i, col_block + 1))],
                  out_specs=(row, row, pl.BlockSpec((tr, 2 * D), lambda i: (i, 0))),
                  semantics=("parallel",))(dmerged, y_gla, y_pool, proj, proj)


def _attn_fwd(q, kv, *, T, D, M):
    H = CROSS_HEADS
    HD = D // H
    tq = _tile(T, 512, 16)
    scale = HD ** -0.5

    def body(q_ref, kv_ref, o_ref):
        for h in range(H):
            hs = slice(h * HD, (h + 1) * HD)
            s = _dot(q_ref[:, hs], kv_ref[:, hs], NT) * scale
            e = jnp.exp(s - jnp.max(s, axis=-1, keepdims=True))
            p = e / jnp.sum(e, axis=-1, keepdims=True)
            o_ref[:, hs] = _dot(p.astype(BF16), kv_ref[:, D + h * HD:D + (h + 1) * HD]).astype(BF16)

    row = pl.BlockSpec((tq, D), lambda i: (i, 0))
    return _pcall(body, name="attn_fwd", out_shape=jax.ShapeDtypeStruct((T, D), BF16),
                  grid=(T // tq,), in_specs=[row, pl.BlockSpec((M, 2 * D), lambda i: (0, 0))],
                  out_specs=row, semantics=("parallel",))(q, kv)


def _attn_bwd(q, kv, do, *, T, D, M):
    H = CROSS_HEADS
    HD = D // H
    tq = _tile(T, 512, 16)
    scale = HD ** -0.5

    def body(q_ref, kv_ref, do_ref, dq_ref, dkv_ref):
        @pl.when(pl.program_id(0) == 0)
        def _():
            dkv_ref[...] = jnp.zeros_like(dkv_ref)

        for h in range(H):
            hs = slice(h * HD, (h + 1) * HD)
            vs = slice(D + h * HD, D + (h + 1) * HD)
            qh = q_ref[:, hs]
            kh = kv_ref[:, hs]
            s = _dot(qh, kh, NT) * scale
            e = jnp.exp(s - jnp.max(s, axis=-1, keepdims=True))
            p = e / jnp.sum(e, axis=-1, keepdims=True)
            p_b = p.astype(BF16)
            d_o = do_ref[:, hs]
            dkv_ref[:, vs] += _dot(p_b, d_o, TN)
            dp = _dot(d_o, kv_ref[:, vs], NT)
            ds = (p * (dp - jnp.sum(dp * p, axis=-1, keepdims=True)) * scale).astype(BF16)
            dq_ref[:, hs] = _dot(ds, kh).astype(BF16)
            dkv_ref[:, hs] += _dot(ds, qh, TN)

    row = pl.BlockSpec((tq, D), lambda i: (i, 0))
    full = pl.BlockSpec((M, 2 * D), lambda i: (0, 0))
    return _pcall(body, name="attn_bwd",
                  out_shape=(jax.ShapeDtypeStruct((T, D), BF16), jax.ShapeDtypeStruct((M, 2 * D), F32)),
                  grid=(T // tq,), in_specs=[row, full, row], out_specs=(row, full),
                  semantics=("arbitrary",))(q, kv, do)


def _shift_down(x, halo, s, t):
    out = pltpu.roll(x, s, 0)
    for j in range(s):
        out = jnp.where(t == j, halo[SUBLANES - s + j:SUBLANES - s + j + 1, :], out)
    return out


def _shift_up(x, halo, s, t, rows):
    out = pltpu.roll(x, rows - s, 0)
    for j in range(s):
        out = jnp.where(t == rows - s + j, halo[j:j + 1, :], out)
    return out


def _conv_tiles(T):
    tt = _tile(T, 128, SUBLANES)
    return tt, tt // SUBLANES, T // SUBLANES


def _conv_fwd(u0, conv_w, conv_b, *, T, F):
    tt, hb, _ = _conv_tiles(T)
    cw = _tile(F, 512)

    def body(u_ref, prev_ref, w_ref, b_ref, f_ref):
        i = pl.program_id(0)
        t = lax.broadcasted_iota(jnp.int32, (tt, 1), 0)

        def conv(cs):
            x = u_ref[:, cs]
            halo = jnp.where(i > 0, prev_ref[:, cs], 0.0)
            return (w_ref[2:3, cs] * x + w_ref[1:2, cs] * _shift_down(x, halo, 1, t)
                    + w_ref[0:1, cs] * _shift_down(x, halo, 2, t) + b_ref[:, cs])

        for j in range(F // cw):
            gate = conv(slice(j * cw, (j + 1) * cw))
            val = conv(slice(F + j * cw, F + (j + 1) * cw))
            f_ref[:, j * cw:(j + 1) * cw] = (gate * _sigmoid(gate) * val).astype(BF16)

    return _pcall(body, name="conv_fwd", out_shape=jax.ShapeDtypeStruct((T, F), BF16),
                  grid=(T // tt,),
                  in_specs=[pl.BlockSpec((tt, 2 * F), lambda i: (i, 0)),
                            pl.BlockSpec((SUBLANES, 2 * F), lambda i: (jnp.maximum(i * hb - 1, 0), 0)),
                            pl.BlockSpec((CONV_W, 2 * F), lambda i: (0, 0)),
                            pl.BlockSpec((1, 2 * F), lambda i: (0, 0))],
                  out_specs=pl.BlockSpec((tt, F), lambda i: (i, 0)),
                  semantics=("parallel",))(u0, u0, conv_w, conv_b)


def _conv_bwd(u0, conv_w, conv_b, df, *, T, F):
    tt, hb, nb = _conv_tiles(T)
    nt = T // tt
    cw = _tile(F, 512)

    def body(u_ref, prev_ref, next_ref, df_ref, dfn_ref, w_ref, b_ref, du0_ref, dw_ref, db_ref):
        i = pl.program_id(0)
        t = lax.broadcasted_iota(jnp.int32, (tt, 1), 0)
        t8 = lax.broadcasted_iota(jnp.int32, (SUBLANES, 1), 0)

        @pl.when(i == 0)
        def _():
            dw_ref[...] = jnp.zeros_like(dw_ref)
            db_ref[...] = jnp.zeros_like(db_ref)

        def conv(cs):
            x = u_ref[:, cs]
            halo = jnp.where(i > 0, prev_ref[:, cs], 0.0)
            x1 = _shift_down(x, halo, 1, t)
            x2 = _shift_down(x, halo, 2, t)
            u = w_ref[2:3, cs] * x + w_ref[1:2, cs] * x1 + w_ref[0:1, cs] * x2 + b_ref[:, cs]
            xn = next_ref[:, cs]
            tail = x[tt - SUBLANES:, :]
            un = (w_ref[2:3, cs] * xn + w_ref[1:2, cs] * _shift_down(xn, tail, 1, t8)
                  + w_ref[0:1, cs] * _shift_down(xn, tail, 2, t8) + b_ref[:, cs])
            return u, un, (x, x1, x2)

        def glu_grad(gate, val, dff):
            sg = _sigmoid(gate)
            return dff * val * (sg * (1.0 + gate * (1.0 - sg))), dff * (gate * sg)

        def finish(cs, du, dun, xs):
            du0 = (w_ref[2:3, cs] * du + w_ref[1:2, cs] * _shift_up(du, dun, 1, t, tt)
                   + w_ref[0:1, cs] * _shift_up(du, dun, 2, t, tt))
            du0_ref[:, cs] = du0.astype(BF16)
            db_ref[:, cs] += jnp.sum(du, axis=0, keepdims=True)
            dw_ref[2:3, cs] += jnp.sum(du * xs[0], axis=0, keepdims=True)
            dw_ref[1:2, cs] += jnp.sum(du * xs[1], axis=0, keepdims=True)
            dw_ref[0:1, cs] += jnp.sum(du * xs[2], axis=0, keepdims=True)

        for j in range(F // cw):
            fs = slice(j * cw, (j + 1) * cw)
            gs, vs = fs, slice(F + j * cw, F + (j + 1) * cw)
            ug, ung, xg = conv(gs)
            uv, unv, xv = conv(vs)
            dug, duv = glu_grad(ug, uv, df_ref[:, fs].astype(F32))
            dung, dunv = glu_grad(ung, unv, dfn_ref[0:SUBLANES, fs].astype(F32))
            dung = jnp.where(i < nt - 1, dung, 0.0)
            dunv = jnp.where(i < nt - 1, dunv, 0.0)
            finish(gs, dug, dung, xg)
            finish(vs, duv, dunv, xv)

    wide = lambda rows, fn: pl.BlockSpec((rows, 2 * F), fn)
    nxt = lambda i: (jnp.minimum((i + 1) * hb, nb - 1), 0)
    return _pcall(body, name="conv_bwd",
                  out_shape=(jax.ShapeDtypeStruct((T, 2 * F), BF16),
                             jax.ShapeDtypeStruct((CONV_W, 2 * F), F32),
                             jax.ShapeDtypeStruct((1, 2 * F), F32)),
                  grid=(nt,),
                  in_specs=[wide(tt, lambda i: (i, 0)),
                            wide(SUBLANES, lambda i: (jnp.maximum(i * hb - 1, 0), 0)),
                            wide(SUBLANES, nxt),
                            pl.BlockSpec((tt, F), lambda i: (i, 0)),
                            pl.BlockSpec((2 * SUBLANES, F),
                                         lambda i: (jnp.minimum((i + 1) * (hb // 2), nb // 2 - 1), 0)),
                            wide(CONV_W, lambda i: (0, 0)), wide(1, lambda i: (0, 0))],
                  out_specs=(wide(tt, lambda i: (i, 0)), wide(CONV_W, lambda i: (0, 0)),
                             wide(1, lambda i: (0, 0))),
                  semantics=("arbitrary",))(u0, u0, u0, df, df, conv_w, conv_b)


def _adamw(w, g, m, v, *, name):
    R, C = w.shape
    tr = _tile(R, max(SUBLANES, (1 << 19) // max(C, 1) // SUBLANES * SUBLANES), SUBLANES)
    c1 = 1.0 / (1.0 - ADAM_B1 ** ADAM_STEP)
    c2 = 1.0 / (1.0 - ADAM_B2 ** ADAM_STEP)

    def body(w_ref, g_ref, m_ref, v_ref, d_ref, mo_ref, vo_ref):
        gv = g_ref[...]
        mn = ADAM_B1 * m_ref[...] + (1.0 - ADAM_B1) * gv
        vn = ADAM_B2 * v_ref[...] + (1.0 - ADAM_B2) * (gv * gv)
        d_ref[...] = -ADAM_LR * ((mn * c1) / (jnp.sqrt(vn * c2) + ADAM_EPS) + ADAM_WD * w_ref[...])
        mo_ref[...] = mn
        vo_ref[...] = vn

    blk = pl.BlockSpec((tr, C), lambda i: (i, 0))
    shp = jax.ShapeDtypeStruct((R, C), F32)
    return _pcall(body, name=name, out_shape=(shp, shp, shp), grid=(R // tr,),
                  in_specs=[blk] * 4, out_specs=(blk,) * 3, semantics=("parallel",))(w, g, m, v)


def _blk(h, C, elems=1 << 19, align=16):
    th = _tile(h, max(align, elems // C // align * align), align)
    if th < h or h * C <= 2 * elems:
        return th, C
    return h, _tile(C, max(LANES, elems // h // LANES * LANES))


def _adamw_halves(w, m, v, g_mine, g_other, c_idx, *, name):
    _, h, C = w.shape
    th, tc = _blk(h, C, align=SUBLANES)
    c1 = 1.0 / (1.0 - ADAM_B1 ** ADAM_STEP)
    c2 = 1.0 / (1.0 - ADAM_B2 ** ADAM_STEP)

    def body(c_ref, w_ref, m_ref, v_ref, gm_ref, go_ref, g_ref, d_ref, mo_ref, vo_ref):
        gv = jnp.where(pl.program_id(0) == c_ref[0], gm_ref[...], go_ref[...])
        mn = ADAM_B1 * m_ref[...] + (1.0 - ADAM_B1) * gv
        vn = ADAM_B2 * v_ref[...] + (1.0 - ADAM_B2) * (gv * gv)
        d_ref[...] = -ADAM_LR * ((mn * c1) / (jnp.sqrt(vn * c2) + ADAM_EPS) + ADAM_WD * w_ref[...])
        g_ref[...] = gv
        mo_ref[...] = mn
        vo_ref[...] = vn

    blk = pl.BlockSpec((None, th, tc), lambda s, i, j, c: (s, i, j))

    def pick(mine):
        def index(s, i, j, c):
            use = (s == c[0]) if mine else (s != c[0])
            return jnp.where(use, i, 0), jnp.where(use, j, 0)
        return pl.BlockSpec((th, tc), index)

    shp = jax.ShapeDtypeStruct((2, h, C), F32)
    return _pcall(body, name=name, out_shape=(shp,) * 4, grid=(2, h // th, C // tc), prefetch=1,
                  in_specs=[blk, blk, blk, pick(True), pick(False)], out_specs=(blk,) * 4,
                  semantics=("parallel", "parallel", "parallel"))(c_idx, w, m, v, g_mine, g_other)


def _mesh_pos():
    x, y, c = lax.axis_index("x"), lax.axis_index("y"), lax.axis_index("c")
    others = [(1 - x, y), (x, 1 - y), (1 - x, 1 - y)]
    return x, y, c, others


def _gather_copies(shards, lands, send_sems, recv_sems):
    x, y, c, others = _mesh_pos()
    me = 2 * x + y
    return [pltpu.make_async_remote_copy(
        src_ref=shards[a].at[c], dst_ref=lands[a].at[me, c],
        send_sem=send_sems.at[3 * a + j], recv_sem=recv_sems.at[3 * a + j],
        device_id=(*chip, c), device_id_type=MESH)
        for a in range(len(shards)) for j, chip in enumerate(others)]


def _pass_copies(shards, zones, send_sems, recv_sems):
    x, y, c, others = _mesh_pos()
    me = 2 * x + y
    copies = []
    for a in range(len(shards)):
        srcs = [zones[a].at[2 * chip[0] + chip[1], c] for chip in others] + [shards[a]]
        dsts = [zones[a].at[2 * chip[0] + chip[1], c] for chip in others] + [zones[a].at[me]]
        copies += [pltpu.make_async_remote_copy(
            src_ref=s, dst_ref=d, send_sem=send_sems.at[4 * a + k], recv_sem=recv_sems.at[4 * a + k],
            device_id=(x, y, 1 - c), device_id_type=MESH) for k, (s, d) in enumerate(zip(srcs, dsts))]
    return copies


def _exchange_copies(grads, recvs, send_sems, recv_sems):
    x, y, c, _ = _mesh_pos()
    return [pltpu.make_async_remote_copy(
        src_ref=grads[a].at[:, 1 - c], dst_ref=recvs[a], send_sem=send_sems.at[a],
        recv_sem=recv_sems.at[a], device_id=(x, y, 1 - c), device_id_type=MESH) for a in range(len(grads))]


def _split_start(copies, per, srcs, zones, after, *, name):
    n = len(srcs)
    HBM = pl.BlockSpec(memory_space=pltpu.HBM)
    SEM = pl.BlockSpec(memory_space=pltpu.SEMAPHORE)

    def body(*refs):
        send_sems, recv_sems = refs[2 * n + 1], refs[2 * n + 2]
        for cp in copies(refs[:n], refs[n:2 * n], send_sems, recv_sems):
            cp.start()
        refs[-1][...] = jnp.zeros_like(refs[-1])

    hbm = lambda a: pltpu.HBM(a.shape, a.dtype)
    res = _pcall(body, name=name,
                 out_shape=(pltpu.SemaphoreType.DMA((per * n,)), pltpu.SemaphoreType.DMA((per * n,)),
                            *[hbm(a) for a in srcs], *[hbm(a) for a in zones],
                            jax.ShapeDtypeStruct((SUBLANES, LANES), F32)),
                 in_specs=[*[HBM] * (2 * n), pl.BlockSpec(memory_space=pl.ANY)],
                 out_specs=(SEM, SEM, *[HBM] * (2 * n), pl.BlockSpec(memory_space=pltpu.VMEM)),
                 aliases={i: 2 + i for i in range(2 * n)}, split_copy=True)(
        *[pltpu.with_memory_space_constraint(a, pltpu.HBM) for a in [*srcs, *zones]], after)
    return res[0], res[1], list(res[2:2 + n]), list(res[2 + n:2 + 2 * n]), res[-1]


def _split_wait(copies, send_sems, recv_sems, srcs, zones, after, *, name):
    n = len(srcs)
    HBM = pl.BlockSpec(memory_space=pltpu.HBM)
    SEM = pl.BlockSpec(memory_space=pltpu.SEMAPHORE)

    def body(*refs):
        for cp in copies(refs[:n], refs[n:2 * n], refs[2 * n], refs[2 * n + 1]):
            cp.wait_send()
            cp.wait_recv()

    hbm = lambda a: pltpu.HBM(a.shape, a.dtype)
    res = _pcall(body, name=name, out_shape=(*[hbm(a) for a in srcs], *[hbm(a) for a in zones]),
                 in_specs=[*[HBM] * (2 * n), SEM, SEM, pl.BlockSpec(memory_space=pl.ANY)],
                 out_specs=tuple([HBM] * (2 * n)), aliases={i: i for i in range(2 * n)},
                 split_copy=True)(*srcs, *zones, send_sems, recv_sems, after)
    return list(res[:n]), list(res[n:])


def _add_halves(grad, recv, c_idx, *, name):
    S, _, h, C = grad.shape
    th, tc = _blk(h, C)

    def body(c_ref, g_ref, r_ref, o_ref):
        o_ref[...] = (g_ref[...].astype(F32) + r_ref[...].astype(F32)).astype(o_ref.dtype)

    return _pcall(body, name=name, out_shape=jax.ShapeDtypeStruct((S, h, C), grad.dtype),
                  grid=(S, h // th, C // tc), prefetch=1,
                  in_specs=[pl.BlockSpec((None, None, th, tc), lambda s, i, j, c: (s, c[0], i, j)),
                            pl.BlockSpec((None, th, tc), lambda s, i, j, c: (s, i, j))],
                  out_specs=pl.BlockSpec((None, th, tc), lambda s, i, j, c: (s, i, j)),
                  semantics=("parallel", "parallel", "parallel"))(c_idx, grad, recv)


def _scatter_copies(srcs, lands, send_sems, recv_sems):
    x, y, c, others = _mesh_pos()
    return [pltpu.make_async_remote_copy(
        src_ref=srcs[a].at[2 * chip[0] + chip[1]], dst_ref=lands[a].at[j],
        send_sem=send_sems.at[3 * a + j], recv_sem=recv_sems.at[3 * a + j],
        device_id=(*chip, c), device_id_type=MESH)
        for a in range(len(srcs)) for j, chip in enumerate(others)]


def _add_chips(sums, recv, chip_idx, *, name):
    _, h, C = sums.shape
    th, tc = _blk(h, C)

    def body(k_ref, s_ref, r_ref, o_ref):
        acc = s_ref[...].astype(F32) + r_ref[0].astype(F32)
        acc = acc + r_ref[1].astype(F32)
        o_ref[...] = acc + r_ref[2].astype(F32)

    return _pcall(body, name=name, out_shape=jax.ShapeDtypeStruct((h, C), F32),
                  grid=(h // th, C // tc), prefetch=1,
                  in_specs=[pl.BlockSpec((None, th, tc), lambda i, j, k: (k[0], i, j)),
                            pl.BlockSpec((3, th, tc), lambda i, j, k: (0, i, j))],
                  out_specs=pl.BlockSpec((th, tc), lambda i, j, k: (i, j)),
                  semantics=("parallel", "parallel"))(chip_idx, sums, recv)


def _swap_halves(halves, *, name):
    n = len(halves)
    ANY = pl.BlockSpec(memory_space=pl.ANY)

    def body(*refs):
        ins, outs = refs[:n], refs[n:2 * n]
        send_sems, recv_sems = refs[2 * n:]
        x, y, c, _ = _mesh_pos()
        copies = [pltpu.make_async_remote_copy(
            src_ref=ins[a], dst_ref=outs[a], send_sem=send_sems.at[a], recv_sem=recv_sems.at[a],
            device_id=(x, y, 1 - c), device_id_type=MESH) for a in range(n)]
        for cp in copies:
            cp.start()
        for cp in copies:
            cp.wait()

    return _pcall(body, name=name,
                  out_shape=[jax.ShapeDtypeStruct(s.shape, s.dtype) for s in halves],
                  in_specs=[ANY] * n, out_specs=[ANY] * n,
                  scratch_shapes=[pltpu.SemaphoreType.DMA((n,)), pltpu.SemaphoreType.DMA((n,))])(*halves)


def _all_reduce_small(buf):
    R, L = buf.shape
    NDEV = 8

    def body(x_ref, sum_ref, all_ref, send_sems, recv_sems, local_sem):
        x, y, c, others = _mesh_pos()
        me, sibling = (x, y, c), (x, y, 1 - c)

        def slot(px, py, pc):
            return all_ref.at[4 * px + 2 * py + pc]

        def copy(k, block, to, src=None):
            return pltpu.make_async_remote_copy(
                src_ref=slot(*block) if src is None else src, dst_ref=slot(*block),
                send_sem=send_sems.at[k], recv_sem=recv_sems.at[k], device_id=to, device_id_type=MESH)

        mine = pltpu.make_async_copy(x_ref, slot(*me), local_sem)
        mine.start()
        first = [copy(0, me, sibling, src=x_ref)]
        first += [copy(1 + j, me, (*chip, c), src=x_ref) for j, chip in enumerate(others)]
        for cp in first:
            cp.start()
        passed = [copy(4 + j, (*chip, c), sibling) for j, chip in enumerate(others)]
        for j, chip in enumerate(others):
            copy(1 + j, (*chip, c), me).wait_recv()
            passed[j].start()
        copy(0, sibling, me).wait_recv()
        for j, chip in enumerate(others):
            copy(4 + j, (*chip, 1 - c), me).wait_recv()
        for cp in first + passed:
            cp.wait_send()
        mine.wait()
        acc = all_ref[0]
        for d in range(1, NDEV):
            acc = acc + all_ref[d]
        sum_ref[...] = acc

    VM = pl.BlockSpec(memory_space=pltpu.VMEM)
    return _pcall(body, name="all_reduce_small",
                  out_shape=(jax.ShapeDtypeStruct((R, L), F32), jax.ShapeDtypeStruct((NDEV, R, L), F32)),
                  in_specs=[VM], out_specs=(VM, VM),
                  scratch_shapes=[pltpu.SemaphoreType.DMA((7,)), pltpu.SemaphoreType.DMA((7,)),
                                  pltpu.SemaphoreType.DMA])(buf)[0]


def _pack(arrs, rows_multiple=16):
    flat = [a.reshape(-1).astype(F32) for a in arrs]
    sizes = [f.shape[0] for f in flat]
    total = sum(sizes)
    per = LANES * rows_multiple
    padded = -(-total // per) * per
    flat.append(jnp.zeros((padded - total,), F32))
    offs = [0]
    for s in sizes:
        offs.append(offs[-1] + s)
    return jnp.concatenate(flat).reshape(padded // LANES, LANES), offs


def _unpack(buf, offs, shapes):
    flat = buf.reshape(-1)
    return [flat[offs[i]:offs[i + 1]].reshape(s) for i, s in enumerate(shapes)]


def kernel(x, mem, g_mix, w_in, w_a2, b_a, g_gla, w_pool, pool_scale, w_branch, w_out, g_cross, g_mem, w_cq, w_ckv, w_co, g_ffn, w_up, conv_w, conv_b, w_down, g_final, loss_target, m_g_mix, m_w_in, m_w_a2, m_b_a, m_g_gla, m_w_pool, m_pool_scale, m_w_branch, m_w_out, m_g_cross, m_g_mem, m_w_cq, m_w_ckv, m_w_co, m_g_ffn, m_w_up, m_conv_w, m_conv_b, m_w_down, m_g_final, v_g_mix, v_w_in, v_w_a2, v_b_a, v_g_gla, v_w_pool, v_pool_scale, v_w_branch, v_w_out, v_g_cross, v_g_mem, v_w_cq, v_w_ckv, v_w_co, v_g_ffn, v_w_up, v_conv_w, v_conv_b, v_w_down, v_g_final):
    weights = dict(g_mix=g_mix, w_in=w_in, w_a2=w_a2, b_a=b_a, g_gla=g_gla, w_pool=w_pool,
                   pool_scale=pool_scale, w_branch=w_branch, w_out=w_out, g_cross=g_cross, g_mem=g_mem,
                   w_cq=w_cq, w_ckv=w_ckv, w_co=w_co, g_ffn=g_ffn, w_up=w_up, conv_w=conv_w,
                   conv_b=conv_b, w_down=w_down, g_final=g_final)
    mom_m = dict(g_mix=m_g_mix, w_in=m_w_in, w_a2=m_w_a2, b_a=m_b_a, g_gla=m_g_gla, w_pool=m_w_pool,
                 pool_scale=m_pool_scale, w_branch=m_w_branch, w_out=m_w_out, g_cross=m_g_cross,
                 g_mem=m_g_mem, w_cq=m_w_cq, w_ckv=m_w_ckv, w_co=m_w_co, g_ffn=m_g_ffn, w_up=m_w_up,
                 conv_w=m_conv_w, conv_b=m_conv_b, w_down=m_w_down, g_final=m_g_final)
    mom_v = dict(g_mix=v_g_mix, w_in=v_w_in, w_a2=v_w_a2, b_a=v_b_a, g_gla=v_g_gla, w_pool=v_w_pool,
                 pool_scale=v_pool_scale, w_branch=v_w_branch, w_out=v_w_out, g_cross=v_g_cross,
                 g_mem=v_g_mem, w_cq=v_w_cq, w_ckv=v_w_ckv, w_co=v_w_co, g_ffn=v_g_ffn, w_up=v_w_up,
                 conv_w=v_conv_w, conv_b=v_conv_b, w_down=v_w_down, g_final=v_g_final)
    order = list(weights)
    big = ["w_in", "w_branch", "w_out", "w_cq", "w_ckv", "w_co", "w_up", "w_down"]
    small_sharded = ["w_a2", "w_pool", "conv_w"]
    small_repl = ["g_mix", "b_a", "g_gla", "pool_scale", "g_cross", "g_mem", "g_ffn", "conv_b", "g_final"]

    xs, ms, tgt = x[0], mem[0], loss_target[0]
    T, D = xs.shape
    M = ms.shape[0]
    DK, DV, PW = b_a.shape[1], g_gla.shape[1], pool_scale.shape[1]
    RANK = w_a2.shape[1]
    F2 = conv_b.shape[1]
    F = F2 // 2
    DIN = N_CHIPS * w_in.shape[2]
    OFF_A = 2 * DK + 2 * DV
    OFF_P = OFF_A + RANK
    RP = LANES
    GW = PW // POOL_GROUPS
    assert PW == DV and 4 * DV == 2 * D and OFF_P + PW + 2 * D == DIN

    cx, cy, cc = lax.axis_index("x"), lax.axis_index("y"), lax.axis_index("c")
    chip = 2 * cx + cy
    c_idx = jnp.reshape(cc, (1,)).astype(jnp.int32)
    chip_idx = jnp.reshape(chip, (1,)).astype(jnp.int32)

    def halves(a):
        return a.reshape(2, a.shape[0] // 2, a.shape[1])

    shard2d = {k: (weights[k][0].T if k == "w_in" else weights[k][0]) for k in big}
    small_pack, small_offs = _pack([weights[k][0] for k in small_sharded], rows_multiple=32)
    shard_halves = {k: halves(shard2d[k].astype(BF16)) for k in big}
    shard_halves["small"] = halves(small_pack)
    flying, passing = {}, {}
    tok = small_pack
    for group, keys in (("in", ["w_in", "small"]), ("mix", ["w_branch", "w_out"]),
                        ("cross", ["w_cq", "w_ckv", "w_co"]), ("up", ["w_up"]), ("down", ["w_down"])):
        srcs = [shard_halves[k] for k in keys]
        zones = [lax.empty((N_CHIPS, *s.shape), s.dtype) for s in srcs]
        s_sems, r_sems, srcs, zones, tok = _split_start(_gather_copies, 3, srcs, zones, tok,
                                                        name=f"gather_start_{group}")
        flying[group] = (keys, s_sems, r_sems, srcs, zones)

    def landed(group, after):
        keys, s_sems, r_sems, srcs, zones = flying[group]
        srcs, zones = _split_wait(_gather_copies, s_sems, r_sems, srcs, zones, after,
                                  name=f"gather_wait_{group}")
        s_sems, r_sems, srcs, zones, token = _split_start(_pass_copies, 4, srcs, zones, after,
                                                          name=f"gather_pass_start_{group}")
        passing[group] = (keys, s_sems, r_sems, srcs, zones)
        return token

    def arrive(group, after):
        keys, s_sems, r_sems, srcs, zones = passing[group]
        _, full = _split_wait(_pass_copies, s_sems, r_sems, srcs, zones, after,
                              name=f"gather_pass_wait_{group}")
        return {k: f.reshape(N_CHIPS, f.shape[1] * f.shape[2], f.shape[3]) for k, f in zip(keys, full)}

    def rows(g):
        return g.reshape(-1, g.shape[2])

    h1, r1 = _rms_fwd(xs, g_mix + tok[0:1, 0:1], name="norm_mix")
    landed("in", h1)
    gw = arrive("in", h1)
    small_all = gw["small"]
    W_in = rows(gw["w_in"])
    W_main = jnp.concatenate([W_in[:OFF_A], W_in[OFF_P:]], axis=0)
    W_a = jnp.pad(W_in[OFF_A:OFF_P], ((0, RP - RANK), (0, 0)))
    sm = [_unpack(small_all[j], small_offs, [weights[k].shape[1:] for k in small_sharded]) for j in range(N_CHIPS)]
    W_a2 = jnp.concatenate([sm[j][0] for j in range(N_CHIPS)], axis=1)
    W_a2p = jnp.pad(W_a2, ((0, RP - RANK), (0, 0))).astype(BF16)
    W_pool = jnp.concatenate([sm[j][1] for j in range(N_CHIPS)], axis=1).astype(BF16)
    W_conv = jnp.concatenate([sm[j][2] for j in range(N_CHIPS)], axis=1)

    proj = _mm(h1, W_main, "nt", name="proj_main", out_dtype=F32)
    tok = landed("mix", proj)
    a_pad = _mm(h1, W_a, "nt", name="proj_gate_rank", out_dtype=F32, after=tok)
    o_gla, o_raw, states = _gla_fwd(proj, a_pad, W_a2p, b_a, g_gla, T=T, DK=DK, DV=DV)
    o_pool = _pool_fwd(proj, W_pool, pool_scale, T=T, PW=PW, col_block=3)
    gw = arrive("mix", o_pool)
    W_branch, W_out = rows(gw["w_branch"]), rows(gw["w_out"])
    tok = landed("cross", o_pool)
    y_gla = _mm(o_gla, W_branch, "nn", name="branch_gla", out_dtype=F32, K=DV, after=tok)
    y_pool = _mm(o_pool, W_branch, "nn", name="branch_pool", out_dtype=F32, K=PW, b_off=(DV, 0))
    merged = _merge_fwd(y_gla, y_pool, proj, T=T, D=D, col_block=2)
    x1 = _mm(merged, W_out, "nn", name="mix_out", out_dtype=F32, add=xs)

    h2, r2 = _rms_fwd(x1, g_cross, name="norm_cross")
    mem_n, rm = _rms_fwd(ms, g_mem, name="norm_mem")
    gw = arrive("cross", h2)
    W_cq, W_ckv, W_co = rows(gw["w_cq"]), gw["w_ckv"], rows(gw["w_co"])
    qc = _mm(h2, W_cq, "nn", name="cross_q", out_dtype=BF16)
    kv = _mm(mem_n, W_ckv, "nn", name="cross_kv", out_dtype=BF16, b_blocked=True)
    o_att = _attn_fwd(qc, kv, T=T, D=D, M=M)
    x2 = _mm(o_att, W_co, "nn", name="cross_out", out_dtype=F32, add=x1)

    tok = landed("up", x2)
    h3, r3 = _rms_fwd(x2, g_ffn + tok[0:1, 0:1], name="norm_ffn")
    W_up = arrive("up", h3)["w_up"]
    u0 = _mm(h3, W_up, "nn", name="ffn_up", out_dtype=F32, b_blocked=True)
    tok = landed("down", u0)
    f_act = _conv_fwd(u0, W_conv, conv_b + tok[0:1, 0:1], T=T, F=F)
    W_down = rows(arrive("down", f_act)["w_down"])
    x3 =_mm(f_act, W_down, "nn", name="ffn_down", out_dtype=F32, add=x2)

    loss_part, dx3, dx3_b, dg_final = _loss_head(x3, g_final.reshape(1, D), tgt)

    def col_shards(g):
        nb, K, Nb = g.shape
        return g.reshape(nb, 2, K // 2, Nb)

    def row_shards(g):
        R, N = g.shape
        return g.reshape(N_CHIPS, 2, R // N_CHIPS // 2, N)

    exchanging, in_flight = {}, []

    def exchange_start(group, keys, partials, after):
        recvs = [lax.empty((p.shape[0], *p.shape[2:]), p.dtype) for p in partials]
        s_sems, r_sems, partials, recvs, token = _split_start(
            _exchange_copies, 1, partials, recvs, after, name=f"grad_exchange_start_{group}")
        exchanging[group] = (keys, s_sems, r_sems, partials, recvs)
        return token

    def scatter_start(group, after):
        keys, s_sems, r_sems, partials, recvs = exchanging[group]
        partials, recvs = _split_wait(_exchange_copies, s_sems, r_sems, partials, recvs, after,
                                      name=f"grad_exchange_wait_{group}")
        chip_sums = [_add_halves(p, r, c_idx, name=f"grad_add_halves_{k}")
                     for k, p, r in zip(keys, partials, recvs)]
        lands = [lax.empty((3, *s.shape[1:]), s.dtype) for s in chip_sums]
        s_sems, r_sems, sums, lands, token = _split_start(
            _scatter_copies, 3, chip_sums, lands, after, name=f"grad_scatter_start_{group}")
        in_flight.append((group, keys, s_sems, r_sems, sums, lands))
        return token

    df = _mm(dx3_b, W_down, "nt", name="d_ffn_act", out_dtype=BF16)
    dW_down = _mm(f_act, dx3_b, "tn", name="dw_down", out_dtype=BF16)
    du0, dconv_w, dconv_b = _conv_bwd(u0, W_conv, conv_b, df, T=T, F=F)
    dh3 = _mm(du0, W_up, "nt", name="d_ffn_in", out_dtype=F32, b_blocked=True)
    dW_up = _mm(h3, du0, "tn", name="dw_up", out_dtype=BF16, out_blocks=N_CHIPS)
    tok = exchange_start("ffn", ["w_down", "w_up"], [row_shards(dW_down), col_shards(dW_up)], dh3)
    dx2, dx2_b, dg_ffn = _rms_bwd(dh3, x2, r3 + tok[0:1, 0:1], g_ffn, dx3, name="norm_ffn_bwd")

    do_att = _mm(dx2_b, W_co, "nt", name="d_cross_o", out_dtype=BF16)
    dW_co = _mm(o_att, dx2_b, "tn", name="dw_co", out_dtype=BF16)
    tok = scatter_start("ffn", dW_co)
    dq, dkv = _attn_bwd(qc, kv, do_att, T=T, D=D, M=M)
    dkv_b = dkv.astype(BF16)
    dW_cq = _mm(h2, dq, "tn", name="dw_cq", out_dtype=BF16, after=tok)
    dh2 = _mm(dq, W_cq, "nt", name="d_cross_in", out_dtype=F32)
    dW_ckv = _mm(mem_n, dkv_b, "tn", name="dw_ckv", out_dtype=BF16, out_blocks=N_CHIPS)
    dmem_n = _mm(dkv_b, W_ckv, "nt", name="d_mem", out_dtype=F32, b_blocked=True)
    tok = exchange_start("cross", ["w_co", "w_cq", "w_ckv"],
                         [row_shards(dW_co), row_shards(dW_cq), col_shards(dW_ckv)], dmem_n)
    _, _, dg_mem = _rms_bwd(dmem_n, ms, rm, g_mem, None, name="norm_mem_bwd")
    dx1, dx1_b, dg_cross = _rms_bwd(dh2, x1, r2 + tok[0:1, 0:1], g_cross, dx2, name="norm_cross_bwd")

    dmerged = _mm(dx1_b, W_out, "nt", name="d_merged", out_dtype=F32)
    dW_out = _mm(merged, dx1_b, "tn", name="dw_out", out_dtype=BF16)
    tok = scatter_start("cross", dW_out)
    dy_gla, dy_pool, dgates = _merge_bwd(dmerged, y_gla, y_pool, proj, T=T, D=D, col_block=2)
    dW_br_gla = _mm(o_gla, dy_gla, "tn", name="dw_branch_gla", out_dtype=BF16, after=tok)
    dW_br_pool = _mm(o_pool, dy_pool, "tn", name="dw_branch_pool", out_dtype=BF16)
    tok = exchange_start("mix", ["w_out", "w_branch"],
                         [row_shards(dW_out), row_shards(jnp.concatenate([dW_br_gla, dW_br_pool], axis=0))],
                         dW_br_pool)
    do_gla = _mm(dy_gla, W_branch, "nt", name="d_o_gla", out_dtype=F32, N=DV, after=tok)
    do_pool = _mm(dy_pool, W_branch, "nt", name="d_o_pool", out_dtype=F32, N=PW, b_off=(DV, 0))
    dp, dw_pool, dpool_scale = _pool_bwd(proj, W_pool, pool_scale, do_pool, T=T, PW=PW, col_block=3)
    tok = scatter_start("mix", dp)
    dqkvr, da_pad, dw2, db_a, dg_gla = _gla_bwd(proj, a_pad, W_a2p, b_a + tok[0:1, 0:1], g_gla, o_raw, states,
                                               do_gla, T=T, DK=DK, DV=DV)
    dproj = jnp.concatenate([dqkvr, dp, dgates], axis=1)
    dW_main = _mm(dproj, h1, "tn", name="dw_in_main", out_dtype=BF16)
    dW_a = _mm(da_pad, h1, "tn", name="dw_in_rank", out_dtype=BF16)
    dW_in = jnp.concatenate([dW_main[:OFF_A], dW_a[:RANK], dW_main[OFF_A:]], axis=0)
    tok = exchange_start("in", ["w_in"], [row_shards(dW_in)], dW_a)
    dh1 = _mm(dproj, W_main, "nn", name="d_mix_in_main", out_dtype=F32, after=tok)
    dh1 = _mm(da_pad, W_a, "nn", name="d_mix_in_rank", out_dtype=F32, add=dh1)
    dx0, _, dg_mix = _rms_bwd(dh1, xs, r1, g_mix, dx1, name="norm_mix_bwd")

    grads = {}

    small_grads = [loss_part, dg_mix, db_a, dg_gla, dpool_scale, dg_cross, dg_mem, dg_ffn, dconv_b, dg_final,
                   dw2[:RANK], dw_pool, dconv_w]
    small_buf, offs = _pack(small_grads)
    small_sum = _all_reduce_small(small_buf)
    red = _unpack(small_sum, offs, [g.shape for g in small_grads])
    loss = red[0][0, 0]
    for k, g in zip(small_repl, red[1:10]):
        grads[k] = g.reshape(weights[k].shape)
    nb = DK // N_CHIPS
    grads["w_a2"] = lax.dynamic_slice_in_dim(red[10], chip * nb, nb, axis=1)[None]
    nb = GW // N_CHIPS
    grads["w_pool"] = lax.dynamic_slice_in_dim(red[11], chip * nb, nb, axis=1)[None]
    nb = F2 // N_CHIPS
    grads["conv_w"] = lax.dynamic_slice_in_dim(red[12], chip * nb, nb, axis=1)[None]

    delta, new_m, new_v = {}, {}, {}

    def whole(k, a):
        a = a.reshape(-1, a.shape[2])
        return (a.T if k == "w_in" else a)[None]

    scatter_start("in", small_sum)
    after = in_flight[-1][4][0]

    for group, keys, s_sems, r_sems, sums, lands in in_flight:
        sums, from_chips = _split_wait(_scatter_copies, s_sems, r_sems, sums, lands, after,
                                       name=f"grad_scatter_wait_{group}")
        half_sums = [_add_chips(s, r, chip_idx, name=f"grad_add_chips_{k}") for k, s, r in zip(keys, sums, from_chips)]
        other_sums = _swap_halves(half_sums, name=f"grad_swap_halves_{group}")
        for k, mine, other in zip(keys, half_sums, other_sums):
            wmv = [halves(src[k][0].T if k == "w_in" else src[k][0]) for src in (weights, mom_m, mom_v)]
            res = _adamw_halves(*wmv, mine, other, c_idx, name=f"adamw_{k}")
            grads[k], delta[k], new_m[k], new_v[k] = (whole(k, a) for a in res)
            after = res[1]
    small = small_repl + small_sharded
    packs = [_pack([src[k] for k in small])[0] for src in (weights, grads, mom_m, mom_v)]
    _, offs = _pack([weights[k] for k in small])
    outs = _adamw(*packs, name="adamw_small")
    for res, o in zip((delta, new_m, new_v), outs):
        for k, a in zip(small, _unpack(o, offs, [weights[k].shape for k in small])):
            res[k] = a

    return (loss, dx0[None], *[grads[k] for k in order], *[delta[k] for k in order],
            *[new_m[k] for k in order], *[new_v[k] for k in order])
```

```python
import functools

import jax
import jax.numpy as jnp
from jax import lax
from jax.experimental import pallas as pl
from jax.experimental.pallas import tpu as pltpu

F32 = jnp.float32
BF16 = jnp.bfloat16
MESH = pl.DeviceIdType.MESH
HIGHEST = lax.Precision.HIGHEST

EPS = 1e-6
GLA_HEADS = 4
GLA_CHUNK = 64
GLA_GATE_NORM = 16.0
POOL_GROUPS = 4
CROSS_HEADS = 4
CONV_W = 3
N_CHIPS = 4
LANES = 128
SUBLANES = 8
VMEM_LIMIT = 56 << 20

ADAM_LR = 0.001
ADAM_B1 = 0.9
ADAM_B2 = 0.999
ADAM_EPS = 1e-08
ADAM_WD = 0.01
ADAM_STEP = 10

NN = (((1,), (0,)), ((), ()))
NT = (((1,), (1,)), ((), ()))
TN = (((0,), (0,)), ((), ()))


ONE_PASS = lax.Precision.HIGH


def _dot(a, b, dn=NN, precision=None):
    return lax.dot_general(a, b, dn, precision=precision, preferred_element_type=F32)


def _tile(n, pref, align=LANES):
    t = (min(pref, n) // align) * align
    while t >= align:
        if n % t == 0:
            return t
        t -= align
    return n


def _pcall(body, *, name, out_shape, grid=(), in_specs=None, out_specs=None, scratch_shapes=(),
           semantics=None, prefetch=0, aliases=None, split_copy=False):
    params = dict(vmem_limit_bytes=VMEM_LIMIT)
    if semantics is not None:
        params["dimension_semantics"] = semantics
    if split_copy:
        params["has_side_effects"] = pltpu.SideEffectType.DATAFLOW_SIDE_EFFECTING
    if prefetch:
        grid_spec = pltpu.PrefetchScalarGridSpec(
            num_scalar_prefetch=prefetch, grid=grid, in_specs=in_specs, out_specs=out_specs,
            scratch_shapes=scratch_shapes)
        return pl.pallas_call(body, name=name, out_shape=out_shape, grid_spec=grid_spec,
                              compiler_params=pltpu.CompilerParams(**params))
    kw = {}
    if aliases is not None:
        kw["input_output_aliases"] = aliases
    if in_specs is not None:
        kw["in_specs"] = in_specs
    if out_specs is not None:
        kw["out_specs"] = out_specs
    return pl.pallas_call(body, name=name, out_shape=out_shape, grid=grid,
                          scratch_shapes=scratch_shapes,
                          compiler_params=pltpu.CompilerParams(**params), **kw)


def _sigmoid(x):
    return 1.0 / (1.0 + jnp.exp(-x))


def _log_sigmoid(x):
    return jnp.minimum(x, 0.0) - jnp.log(1.0 + jnp.exp(-jnp.abs(x)))


def _mm(a, b, mode, *, name, out_dtype, M=None, N=None, K=None, a_off=(0, 0), b_off=(0, 0),
        add=None, b_blocked=False, out_blocks=0, after=None, tm=1536, tn=1536, tk=2048):
    if b_blocked:
        nb, R, Cb = b.shape
        b_rows, b_cols = R, nb * Cb
    else:
        b_rows, b_cols = b.shape
    if mode == "nn":
        M = M or a.shape[0]; K = K or a.shape[1]; N = N or b_cols
    elif mode == "nt":
        M = M or a.shape[0]; K = K or a.shape[1]; N = N or b_rows
    else:
        K = K or a.shape[0]; M = M or a.shape[1]; N = N or b_cols
    tm = _tile(M, tm, LANES if mode == "tn" else 16)
    tn = _tile(Cb if (b_blocked and mode != "nt") else (N // out_blocks if out_blocks else N), tn)
    tk = _tile(Cb if (b_blocked and mode == "nt") else K, tk)
    nk = K // tk
    dn = {"nn": NN, "nt": NT, "tn": TN}[mode]

    def off(o, t):
        assert o % t == 0, (name, o, t)
        return o // t

    if mode == "tn":
        ar, ac = off(a_off[0], tk), off(a_off[1], tm)
        a_spec = pl.BlockSpec((tk, tm), lambda i, j, k: (k + ar, i + ac))
    else:
        ar, ac = off(a_off[0], tm), off(a_off[1], tk)
        a_spec = pl.BlockSpec((tm, tk), lambda i, j, k: (i + ar, k + ac))
    if b_blocked and mode == "nt":
        per = Cb // tk
        b_spec = pl.BlockSpec((None, tn, tk), lambda i, j, k: (k // per, j, k % per))
    elif b_blocked:
        per = Cb // tn
        b_spec = pl.BlockSpec((None, tk, tn), lambda i, j, k: (j // per, k, j % per))
    elif mode == "nt":
        br, bc = off(b_off[0], tn), off(b_off[1], tk)
        b_spec = pl.BlockSpec((tn, tk), lambda i, j, k: (j + br, k + bc))
    else:
        br, bc = off(b_off[0], tk), off(b_off[1], tn)
        b_spec = pl.BlockSpec((tk, tn), lambda i, j, k: (k + br, j + bc))
    if out_blocks:
        per_o = N // out_blocks // tn
        o_spec = pl.BlockSpec((None, tm, tn), lambda i, j, k: (j // per_o, i, j % per_o))
        out_shape = jax.ShapeDtypeStruct((out_blocks, M, N // out_blocks), out_dtype)
    else:
        o_spec = pl.BlockSpec((tm, tn), lambda i, j, k: (i, j))
        out_shape = jax.ShapeDtypeStruct((M, N), out_dtype)
    in_specs = [a_spec, b_spec]
    args = [a, b]
    if add is not None:
        assert not out_blocks
        in_specs.append(o_spec)
        args.append(add)
    if after is not None:
        in_specs.append(pl.BlockSpec(memory_space=pl.ANY))
        args.append(after)
    n_in = len(args)

    def finish(r, refs):
        if add is not None:
            r = r + refs[2][...]
        o_ref = refs[n_in]
        o_ref[...] = r.astype(o_ref.dtype)

    def body_one(*refs):
        finish(_dot(refs[0][...].astype(BF16), refs[1][...].astype(BF16), dn), refs)

    def body_acc(*refs):
        acc_ref = refs[-1]
        k = pl.program_id(2)

        @pl.when(k == 0)
        def _():
            acc_ref[...] = jnp.zeros_like(acc_ref)

        acc_ref[...] += _dot(refs[0][...].astype(BF16), refs[1][...].astype(BF16), dn)

        @pl.when(k == nk - 1)
        def _():
            finish(acc_ref[...], refs)

    return _pcall(body_one if nk == 1 else body_acc, name=name, out_shape=out_shape,
                  grid=(M // tm, N // tn, nk), in_specs=in_specs, out_specs=o_spec,
                  scratch_shapes=[] if nk == 1 else [pltpu.VMEM((tm, tn), F32)],
                  semantics=("parallel", "parallel", "arbitrary"))(*args)


def _rms_fwd(x, g, *, name):
    T, D = x.shape
    tr = _tile(T, 128, 16)

    def body(x_ref, g_ref, h_ref, r_ref):
        xv = x_ref[...]
        r = lax.rsqrt(jnp.mean(xv * xv, axis=-1, keepdims=True) + EPS)
        h_ref[...] = (xv * r * g_ref[...]).astype(h_ref.dtype)
        r_ref[...] = r

    row = pl.BlockSpec((tr, D), lambda i: (i, 0))
    return _pcall(body, name=name,
                  out_shape=(jax.ShapeDtypeStruct((T, D), BF16), jax.ShapeDtypeStruct((T, 1), F32)),
                  grid=(T // tr,),
                  in_specs=[row, pl.BlockSpec((1, D), lambda i: (0, 0))],
                  out_specs=(row, pl.BlockSpec((tr, 1), lambda i: (i, 0))),
                  semantics=("parallel",))(x, g)


def _rms_bwd(dh, x, rstd, g, dres, *, name):
    T, D = x.shape
    tr = _tile(T, 128, 16)
    has_res = dres is not None

    def body(*refs):
        if has_res:
            dh_ref, x_ref, r_ref, g_ref, res_ref, dx_ref, dxb_ref, dg_ref = refs
        else:
            dh_ref, x_ref, r_ref, g_ref, dx_ref, dxb_ref, dg_ref = refs
        r = r_ref[...]
        xh = x_ref[...] * r
        dhv = dh_ref[...].astype(F32)
        dxh = dhv * g_ref[...]
        m = jnp.mean(dxh * xh, axis=-1, keepdims=True)
        dx = r * (dxh - xh * m)
        if has_res:
            dx = dx + res_ref[...]
        dx_ref[...] = dx
        dxb_ref[...] = dx.astype(BF16)

        @pl.when(pl.program_id(0) == 0)
        def _():
            dg_ref[...] = jnp.zeros_like(dg_ref)

        dg_ref[...] += jnp.sum(dhv * xh, axis=0, keepdims=True)

    row = pl.BlockSpec((tr, D), lambda i: (i, 0))
    vec = pl.BlockSpec((1, D), lambda i: (0, 0))
    in_specs = [row, row, pl.BlockSpec((tr, 1), lambda i: (i, 0)), vec]
    args = [dh, x, rstd, g]
    if has_res:
        in_specs.append(row)
        args.append(dres)
    return _pcall(body, name=name,
                  out_shape=(jax.ShapeDtypeStruct((T, D), F32), jax.ShapeDtypeStruct((T, D), BF16),
                             jax.ShapeDtypeStruct((1, D), F32)),
                  grid=(T // tr,), in_specs=in_specs, out_specs=(row, row, vec),
                  semantics=("arbitrary",))(*args)


def _loss_head(x3, g, tgt):
    T, D = x3.shape
    tr = _tile(T, 128, 16)

    def body(x_ref, g_ref, t_ref, loss_ref, dx_ref, dxb_ref, dg_ref):
        xv = x_ref[...]
        gv = g_ref[...]
        r = lax.rsqrt(jnp.mean(xv * xv, axis=-1, keepdims=True) + EPS)
        xh = xv * r
        err = xh * gv - t_ref[...]
        dy = err * (1.0 / D)
        dxh = dy * gv
        m = jnp.mean(dxh * xh, axis=-1, keepdims=True)
        dx = r * (dxh - xh * m)
        dx_ref[...] = dx
        dxb_ref[...] = dx.astype(BF16)

        @pl.when(pl.program_id(0) == 0)
        def _():
            dg_ref[...] = jnp.zeros_like(dg_ref)
            loss_ref[...] = jnp.zeros_like(loss_ref)

        dg_ref[...] += jnp.sum(dy * xh, axis=0, keepdims=True)
        part = 0.5 * jnp.sum(jnp.mean(err * err, axis=-1, keepdims=True), axis=0, keepdims=True)
        loss_ref[...] += jnp.broadcast_to(part, loss_ref.shape)

    row = pl.BlockSpec((tr, D), lambda i: (i, 0))
    vec = pl.BlockSpec((1, D), lambda i: (0, 0))
    return _pcall(body, name="loss_head",
                  out_shape=(jax.ShapeDtypeStruct((1, LANES), F32), jax.ShapeDtypeStruct((T, D), F32),
                             jax.ShapeDtypeStruct((T, D), BF16), jax.ShapeDtypeStruct((1, D), F32)),
                  grid=(T // tr,), in_specs=[row, vec, row],
                  out_specs=(pl.BlockSpec((1, LANES), lambda i: (0, 0)), row, row, vec),
                  semantics=("arbitrary",))(x3, g, tgt)


def _gla_chunk_terms(qk, a_ref, w2_ref, ba_ref, DK):
    C = qk.shape[0]
    gp = _dot(a_ref[...].astype(BF16), w2_ref[...]) + ba_ref[...]
    la = _log_sigmoid(gp) * (1.0 / GLA_GATE_NORM)
    row = lax.broadcasted_iota(jnp.int32, (C, C), 0)
    col = lax.broadcasted_iota(jnp.int32, (C, C), 1)
    causal = row >= col
    b = _dot(causal.astype(F32), la, precision=HIGHEST)
    return gp, b, causal


def _gla_fwd(proj, a_pad, w2, b_a, g_gla, *, T, DK, DV):
    assert 2 * DK == DV
    H = GLA_HEADS
    HK, HV = DK // H, DV // H
    C = GLA_CHUNK
    n = T // C
    RP = a_pad.shape[1]
    scale = HK ** -0.5

    def body(qk_ref, v_ref, r_ref, a_ref, w2_ref, ba_ref, gg_ref, og_ref, oraw_ref, st_ref, s_ref):
        @pl.when(pl.program_id(0) == 0)
        def _():
            s_ref[...] = jnp.zeros_like(s_ref)

        st_ref[...] = s_ref[...]
        qk = qk_ref[...]
        _, b, causal = _gla_chunk_terms(qk, a_ref, w2_ref, ba_ref, DK)
        for h in range(H):
            ks = slice(h * HK, (h + 1) * HK)
            vs = slice(h * HV, (h + 1) * HV)
            bh = b[:, ks]
            b_last = bh[C - 1:C, :]
            qt = qk[:, ks] * scale * jnp.exp(bh)
            kh = qk[:, DK + h * HK:DK + (h + 1) * HK]
            kt = kh * jnp.exp(-bh)
            khat = kh * jnp.exp(b_last - bh)
            a_mat = jnp.where(causal, _dot(qt, kt, NT, ONE_PASS), 0.0)
            vh = v_ref[:, vs]
            s_t = s_ref[h]
            o = _dot(a_mat, vh, NN, ONE_PASS) + _dot(qt, s_t, NT, ONE_PASS)
            s_ref[h] = s_t * jnp.exp(b_last) + _dot(vh, khat, TN, ONE_PASS)
            rs = lax.rsqrt(jnp.mean(o * o, axis=-1, keepdims=True) + EPS)
            rr = r_ref[:, vs]
            og = o * rs * gg_ref[:, vs] * (rr * _sigmoid(rr))
            oraw_ref[:, vs] = o
            og_ref[:, vs] = og.astype(BF16)

    blk = lambda j: pl.BlockSpec((C, DV), lambda i: (i, j))
    full = lambda s: pl.BlockSpec(s, lambda i: (0,) * len(s))
    return _pcall(
        body, name="gla_fwd",
        out_shape=(jax.ShapeDtypeStruct((T, DV), BF16), jax.ShapeDtypeStruct((T, DV), F32),
                   jax.ShapeDtypeStruct((n, H, HV, HK), F32)),
        grid=(n,),
        in_specs=[blk(0), blk(1), blk(2), pl.BlockSpec((C, RP), lambda i: (i, 0)),
                  full((RP, DK)), full((1, DK)), full((1, DV))],
        out_specs=(blk(0), blk(0), pl.BlockSpec((None, H, HV, HK), lambda i: (i, 0, 0, 0))),
        scratch_shapes=[pltpu.VMEM((H, HV, HK), F32)],
        semantics=("arbitrary",))(proj, proj, proj, a_pad, w2, b_a, g_gla)


def _gla_bwd(proj, a_pad, w2, b_a, g_gla, o_raw, states, do_gla, *, T, DK, DV):
    H = GLA_HEADS
    HK, HV = DK // H, DV // H
    C = GLA_CHUNK
    n = T // C
    RP = a_pad.shape[1]
    scale = HK ** -0.5

    def body(qk_ref, v_ref, r_ref, a_ref, w2_ref, ba_ref, gg_ref, oraw_ref, st_ref, dog_ref,
             dqkvr_ref, da_ref, dw2_ref, dba_ref, dgg_ref, ds_ref):
        @pl.when(pl.program_id(0) == 0)
        def _():
            ds_ref[...] = jnp.zeros_like(ds_ref)
            dw2_ref[...] = jnp.zeros_like(dw2_ref)
            dba_ref[...] = jnp.zeros_like(dba_ref)
            dgg_ref[...] = jnp.zeros_like(dgg_ref)

        qk = qk_ref[...]
        gp, b, causal = _gla_chunk_terms(qk, a_ref, w2_ref, ba_ref, DK)
        row = lax.broadcasted_iota(jnp.int32, (C, C), 0)
        col = lax.broadcasted_iota(jnp.int32, (C, C), 1)
        upper = (col >= row).astype(F32)
        dla_parts = []
        for h in range(H):
            ks = slice(h * HK, (h + 1) * HK)
            vs = slice(h * HV, (h + 1) * HV)
            bh = b[:, ks]
            b_last = bh[C - 1:C, :]
            eb = jnp.exp(bh)
            emb = jnp.exp(-bh)
            ehat = jnp.exp(b_last - bh)
            e_last = jnp.exp(b_last)
            qt = qk[:, ks] * scale * eb
            kh = qk[:, DK + h * HK:DK + (h + 1) * HK]
            kt = kh * emb
            khat = kh * ehat
            a_mat = jnp.where(causal, _dot(qt, kt, NT, ONE_PASS), 0.0)
            vh = v_ref[:, vs]
            o = oraw_ref[:, vs]
            rs = lax.rsqrt(jnp.mean(o * o, axis=-1, keepdims=True) + EPS)
            on = o * rs
            gg = gg_ref[:, vs]
            rr = r_ref[:, vs]
            sg = _sigmoid(rr)
            d_out = dog_ref[:, vs]
            dr = d_out * (on * gg) * (sg * (1.0 + rr * (1.0 - sg)))
            d_og = d_out * (rr * sg)
            dgg_ref[:, vs] += jnp.sum(d_og * on, axis=0, keepdims=True)
            d_on = d_og * gg
            d_o = rs * (d_on - on * jnp.mean(d_on * on, axis=-1, keepdims=True))
            s_t = st_ref[h]
            ds_t = ds_ref[h]
            d_a = jnp.where(causal, _dot(d_o, vh, NT, ONE_PASS), 0.0)
            dv = _dot(a_mat, d_o, TN, ONE_PASS) + _dot(khat, ds_t, NT, ONE_PASS)
            dqt = _dot(d_a, kt, NN, ONE_PASS) + _dot(d_o, s_t, NN, ONE_PASS)
            dkt = _dot(d_a, qt, TN, ONE_PASS)
            dkhat = _dot(vh, ds_t, NN, ONE_PASS)
            ds_ref[h] = ds_t * e_last + _dot(d_o, qt, TN, ONE_PASS)
            dq = dqt * eb * scale
            dk = dkt * emb + dkhat * ehat
            db = dqt * qt - dkt * kt - dkhat * khat
            d_last = (jnp.sum(dkhat * khat, axis=0, keepdims=True)
                      + e_last * jnp.sum(ds_t * s_t, axis=0, keepdims=True))
            dla_parts.append(_dot(upper, db, NN, HIGHEST) + d_last)
            dqkvr_ref[:, ks] = dq.astype(BF16)
            dqkvr_ref[:, DK + h * HK:DK + (h + 1) * HK] = dk.astype(BF16)
            dqkvr_ref[:, DV + h * HV:DV + (h + 1) * HV] = dv.astype(BF16)
            dqkvr_ref[:, 2 * DV + h * HV:2 * DV + (h + 1) * HV] = dr.astype(BF16)
        dla = jnp.concatenate(dla_parts, axis=1)
        dgp = dla * (1.0 / GLA_GATE_NORM) * _sigmoid(-gp)
        dba_ref[...] += jnp.sum(dgp, axis=0, keepdims=True)
        dgp_b = dgp.astype(BF16)
        dw2_ref[...] += _dot(a_ref[...].astype(BF16), dgp_b, TN)
        da_ref[...] = _dot(dgp_b, w2_ref[...], NT).astype(BF16)

    rev = lambda j: pl.BlockSpec((C, DV), lambda i: (n - 1 - i, j))
    full = lambda s: pl.BlockSpec(s, lambda i: (0,) * len(s))
    return _pcall(
        body, name="gla_bwd",
        out_shape=(jax.ShapeDtypeStruct((T, 3 * DV), BF16), jax.ShapeDtypeStruct((T, RP), BF16),
                   jax.ShapeDtypeStruct((RP, DK), F32), jax.ShapeDtypeStruct((1, DK), F32),
                   jax.ShapeDtypeStruct((1, DV), F32)),
        grid=(n,),
        in_specs=[rev(0), rev(1), rev(2), pl.BlockSpec((C, RP), lambda i: (n - 1 - i, 0)),
                  full((RP, DK)), full((1, DK)), full((1, DV)), rev(0),
                  pl.BlockSpec((None, H, HV, HK), lambda i: (n - 1 - i, 0, 0, 0)), rev(0)],
        out_specs=(pl.BlockSpec((C, 3 * DV), lambda i: (n - 1 - i, 0)),
                   pl.BlockSpec((C, RP), lambda i: (n - 1 - i, 0)),
                   full((RP, DK)), full((1, DK)), full((1, DV))),
        scratch_shapes=[pltpu.VMEM((H, HV, HK), F32)],
        semantics=("arbitrary",))(proj, proj, proj, a_pad, w2, b_a, g_gla, o_raw, states, do_gla)


def _pool_windows(p, g, T):
    t = lax.broadcasted_iota(jnp.int32, (T, 1), 0)
    s = p
    for lvl in range(POOL_GROUPS):
        sh = 1 << lvl
        nxt = s + jnp.where(t >= sh, pltpu.roll(s, sh, 0), 0.0)
        s = jnp.where(lvl <= g, nxt, s)
    win = jnp.left_shift(2, g)
    inv = 1.0 / jnp.minimum(t + 1, win).astype(F32)
    return s * inv - p, inv


def _pool_fwd(proj, w_pool, scale, *, T, PW, col_block):
    GW = PW // POOL_GROUPS
    per = PW // GW

    def body(p_ref, w_ref, s_ref, o_ref):
        g = pl.program_id(0)
        pooled, _ = _pool_windows(p_ref[...], g, T)
        mixed = _dot(pooled.astype(BF16), w_ref[...])
        o_ref[...] = (mixed * s_ref[...]).astype(BF16)

    return _pcall(body, name="pool_fwd", out_shape=jax.ShapeDtypeStruct((T, PW), BF16),
                  grid=(POOL_GROUPS,),
                  in_specs=[pl.BlockSpec((T, GW), lambda g: (0, col_block * per + g)),
                            pl.BlockSpec((None, GW, GW), lambda g: (g, 0, 0)),
                            pl.BlockSpec((1, GW), lambda g: (0, g))],
                  out_specs=pl.BlockSpec((T, GW), lambda g: (0, g)),
                  semantics=("parallel",))(proj, w_pool, scale)


def _pool_bwd(proj, w_pool, scale, do_pool, *, T, PW, col_block):
    GW = PW // POOL_GROUPS
    per = PW // GW

    def body(p_ref, w_ref, s_ref, do_ref, dp_ref, dw_ref, dsc_ref):
        g = pl.program_id(0)
        pooled, inv = _pool_windows(p_ref[...], g, T)
        pooled_b = pooled.astype(BF16)
        w = w_ref[...]
        mixed = _dot(pooled_b, w)
        d_out = do_ref[...]
        dsc_ref[...] = jnp.sum(d_out * mixed, axis=0, keepdims=True)
        dmixed = (d_out * s_ref[...]).astype(BF16)
        dw_ref[...] = _dot(pooled_b, dmixed, TN)
        dpooled = _dot(dmixed, w, NT)
        t = lax.broadcasted_iota(jnp.int32, (T, 1), 0)
        s = dpooled * inv
        for lvl in range(POOL_GROUPS):
            sh = 1 << lvl
            nxt = s + jnp.where(t < T - sh, pltpu.roll(s, T - sh, 0), 0.0)
            s = jnp.where(lvl <= g, nxt, s)
        dp_ref[...] = (s - dpooled).astype(BF16)

    return _pcall(body, name="pool_bwd",
                  out_shape=(jax.ShapeDtypeStruct((T, PW), BF16),
                             jax.ShapeDtypeStruct((POOL_GROUPS, GW, GW), F32),
                             jax.ShapeDtypeStruct((1, PW), F32)),
                  grid=(POOL_GROUPS,),
                  in_specs=[pl.BlockSpec((T, GW), lambda g: (0, col_block * per + g)),
                            pl.BlockSpec((None, GW, GW), lambda g: (g, 0, 0)),
                            pl.BlockSpec((1, GW), lambda g: (0, g)),
                            pl.BlockSpec((T, GW), lambda g: (0, g))],
                  out_specs=(pl.BlockSpec((T, GW), lambda g: (0, g)),
                             pl.BlockSpec((None, GW, GW), lambda g: (g, 0, 0)),
                             pl.BlockSpec((1, GW), lambda g: (0, g))),
                  semantics=("parallel",))(proj, w_pool, scale, do_pool)


def _merge_fwd(y_gla, y_pool, proj, *, T, D, col_block):
    tr = _tile(T, 128, 16)

    def body(yg_ref, yp_ref, g1_ref, g2_ref, o_ref):
        o_ref[...] = (_sigmoid(g1_ref[...]) * yg_ref[...]
                      + _sigmoid(g2_ref[...]) * yp_ref[...]).astype(BF16)

    row = pl.BlockSpec((tr, D), lambda i: (i, 0))
    return _pcall(body, name="merge_fwd", out_shape=jax.ShapeDtypeStruct((T, D), BF16),
                  grid=(T // tr,),
                  in_specs=[row, row, pl.BlockSpec((tr, D), lambda i: (i, col_block)),
                            pl.BlockSpec((tr, D), lambda i: (i, col_block + 1))],
                  out_specs=row, semantics=("parallel",))(y_gla, y_pool, proj, proj)


def _merge_bwd(dmerged, y_gla, y_pool, proj, *, T, D, col_block):
    tr = _tile(T, 128, 16)

    def body(dm_ref, yg_ref, yp_ref, g1_ref, g2_ref, dyg_ref, dyp_ref, dg_ref):
        dm = dm_ref[...]
        s1 = _sigmoid(g1_ref[...])
        s2 = _sigmoid(g2_ref[...])
        dyg_ref[...] = (dm * s1).astype(BF16)
        dyp_ref[...] = (dm * s2).astype(BF16)
        dg_ref[:, :D] = (dm * yg_ref[...] * s1 * (1.0 - s1)).astype(BF16)
        dg_ref[:, D:] = (dm * yp_ref[...] * s2 * (1.0 - s2)).astype(BF16)

    row = pl.BlockSpec((tr, D), lambda i: (i, 0))
    return _pcall(body, name="merge_bwd",
                  out_shape=(jax.ShapeDtypeStruct((T, D), BF16), jax.ShapeDtypeStruct((T, D), BF16),
                             jax.ShapeDtypeStruct((T, 2 * D), BF16)),
                  grid=(T // tr,),
                  in_specs=[row, row, row, pl.BlockSpec((tr, D), lambda i: (i, col_block)),
                            pl.BlockSpec((tr, D), lambda i: (i, col_block + 1))],
                  out_specs=(row, row, pl.BlockSpec((tr, 2 * D), lambda i: (i, 0))),
                  semantics=("parallel",))(dmerged, y_gla, y_pool, proj, proj)


def _attn_fwd(q, kv, *, T, D, M):
    H = CROSS_HEADS
    HD = D // H
    tq = _tile(T, 512, 16)
    scale = HD ** -0.5

    def body(q_ref, kv_ref, o_ref):
        for h in range(H):
            hs = slice(h * HD, (h + 1) * HD)
            s = _dot(q_ref[:, hs], kv_ref[:, hs], NT) * scale
            e = jnp.exp(s - jnp.max(s, axis=-1, keepdims=True))
            p = e / jnp.sum(e, axis=-1, keepdims=True)
            o_ref[:, hs] = _dot(p.astype(BF16), kv_ref[:, D + h * HD:D + (h + 1) * HD]).astype(BF16)

    row = pl.BlockSpec((tq, D), lambda i: (i, 0))
    return _pcall(body, name="attn_fwd", out_shape=jax.ShapeDtypeStruct((T, D), BF16),
                  grid=(T // tq,), in_specs=[row, pl.BlockSpec((M, 2 * D), lambda i: (0, 0))],
                  out_specs=row, semantics=("parallel",))(q, kv)


def _attn_bwd(q, kv, do, *, T, D, M):
    H = CROSS_HEADS
    HD = D // H
    tq = _tile(T, 512, 16)
    scale = HD ** -0.5

    def body(q_ref, kv_ref, do_ref, dq_ref, dkv_ref):
        @pl.when(pl.program_id(0) == 0)
        def _():
            dkv_ref[...] = jnp.zeros_like(dkv_ref)

        for h in range(H):
            hs = slice(h * HD, (h + 1) * HD)
            vs = slice(D + h * HD, D + (h + 1) * HD)
            qh = q_ref[:, hs]
            kh = kv_ref[:, hs]
            s = _dot(qh, kh, NT) * scale
            e = jnp.exp(s - jnp.max(s, axis=-1, keepdims=True))
            p = e / jnp.sum(e, axis=-1, keepdims=True)
            p_b = p.astype(BF16)
            d_o = do_ref[:, hs]
            dkv_ref[:, vs] += _dot(p_b, d_o, TN)
            dp = _dot(d_o, kv_ref[:, vs], NT)
            ds = (p * (dp - jnp.sum(dp * p, axis=-1, keepdims=True)) * scale).astype(BF16)
            dq_ref[:, hs] = _dot(ds, kh).astype(BF16)
            dkv_ref[:, hs] += _dot(ds, qh, TN)

    row = pl.BlockSpec((tq, D), lambda i: (i, 0))
    full = pl.BlockSpec((M, 2 * D), lambda i: (0, 0))
    return _pcall(body, name="attn_bwd",
                  out_shape=(jax.ShapeDtypeStruct((T, D), BF16), jax.ShapeDtypeStruct((M, 2 * D), F32)),
                  grid=(T // tq,), in_specs=[row, full, row], out_specs=(row, full),
                  semantics=("arbitrary",))(q, kv, do)


def _shift_down(x, halo, s):
    out = pltpu.roll(x, s, 0)
    t8 = lax.broadcasted_iota(jnp.int32, (SUBLANES, 1), 0)
    head = out[:SUBLANES]
    for j in range(s):
        head = jnp.where(t8 == j, halo[SUBLANES - s + j:SUBLANES - s + j + 1, :], head)
    return head if x.shape[0] == SUBLANES else jnp.concatenate([head, out[SUBLANES:]], axis=0)


def _shift_up(x, halo, s):
    rows = x.shape[0]
    out = pltpu.roll(x, rows - s, 0)
    t8 = lax.broadcasted_iota(jnp.int32, (SUBLANES, 1), 0)
    tail = out[rows - SUBLANES:]
    for j in range(s):
        tail = jnp.where(t8 == SUBLANES - s + j, halo[j:j + 1, :], tail)
    return jnp.concatenate([out[:rows - SUBLANES], tail], axis=0)


def _conv_tiles(T):
    tt = _tile(T, 128, SUBLANES)
    return tt, tt // SUBLANES, T // SUBLANES


def _conv_fwd(u0, conv_w, conv_b, *, T, F):
    tt, hb, _ = _conv_tiles(T)
    cw = _tile(F, LANES)

    def body(u_ref, prev_ref, w_ref, b_ref, f_ref):
        i = pl.program_id(0)

        def conv(cs):
            x = u_ref[:, cs]
            halo = jnp.where(i > 0, prev_ref[:, cs], 0.0)
            return (w_ref[2:3, cs] * x + w_ref[1:2, cs] * _shift_down(x, halo, 1)
                    + w_ref[0:1, cs] * _shift_down(x, halo, 2) + b_ref[:, cs])

        for j in range(F // cw):
            gate = conv(slice(j * cw, (j + 1) * cw))
            val = conv(slice(F + j * cw, F + (j + 1) * cw))
            f_ref[:, j * cw:(j + 1) * cw] = (gate * _sigmoid(gate) * val).astype(BF16)

    return _pcall(body, name="conv_fwd", out_shape=jax.ShapeDtypeStruct((T, F), BF16),
                  grid=(T // tt,),
                  in_specs=[pl.BlockSpec((tt, 2 * F), lambda i: (i, 0)),
                            pl.BlockSpec((SUBLANES, 2 * F), lambda i: (jnp.maximum(i * hb - 1, 0), 0)),
                            pl.BlockSpec((CONV_W, 2 * F), lambda i: (0, 0)),
                            pl.BlockSpec((1, 2 * F), lambda i: (0, 0))],
                  out_specs=pl.BlockSpec((tt, F), lambda i: (i, 0)),
                  semantics=("parallel",))(u0, u0, conv_w, conv_b)


def _conv_bwd(u0, conv_w, conv_b, df, *, T, F):
    tt, hb, nb = _conv_tiles(T)
    nt = T // tt
    cw = _tile(F, LANES)

    def body(u_ref, prev_ref, next_ref, df_ref, dfn_ref, w_ref, b_ref, du0_ref, dw_ref, db_ref):
        i = pl.program_id(0)

        @pl.when(i == 0)
        def _():
            dw_ref[...] = jnp.zeros_like(dw_ref)
            db_ref[...] = jnp.zeros_like(db_ref)

        def conv(cs):
            x = u_ref[:, cs]
            halo = jnp.where(i > 0, prev_ref[:, cs], 0.0)
            x1 = _shift_down(x, halo, 1)
            x2 = _shift_down(x, halo, 2)
            u = w_ref[2:3, cs] * x + w_ref[1:2, cs] * x1 + w_ref[0:1, cs] * x2 + b_ref[:, cs]
            xn = next_ref[:, cs]
            tail = x[tt - SUBLANES:, :]
            un = (w_ref[2:3, cs] * xn + w_ref[1:2, cs] * _shift_down(xn, tail, 1)
                  + w_ref[0:1, cs] * _shift_down(xn, tail, 2) + b_ref[:, cs])
            return u, un, (x, x1, x2)

        def glu_grad(gate, val, dff):
            sg = _sigmoid(gate)
            return dff * val * (sg * (1.0 + gate * (1.0 - sg))), dff * (gate * sg)

        def finish(cs, du, dun, xs):
            du0 = (w_ref[2:3, cs] * du + w_ref[1:2, cs] * _shift_up(du, dun, 1)
                   + w_ref[0:1, cs] * _shift_up(du, dun, 2))
            du0_ref[:, cs] = du0.astype(BF16)
            db_ref[:, cs] += jnp.sum(du, axis=0, keepdims=True)
            dw_ref[2:3, cs] += jnp.sum(du * xs[0], axis=0, keepdims=True)
            dw_ref[1:2, cs] += jnp.sum(du * xs[1], axis=0, keepdims=True)
            dw_ref[0:1, cs] += jnp.sum(du * xs[2], axis=0, keepdims=True)

        for j in range(F // cw):
            fs = slice(j * cw, (j + 1) * cw)
            gs, vs = fs, slice(F + j * cw, F + (j + 1) * cw)
            ug, ung, xg = conv(gs)
            uv, unv, xv = conv(vs)
            dug, duv = glu_grad(ug, uv, df_ref[:, fs].astype(F32))
            dung, dunv = glu_grad(ung, unv, dfn_ref[0:SUBLANES, fs].astype(F32))
            dung = jnp.where(i < nt - 1, dung, 0.0)
            dunv = jnp.where(i < nt - 1, dunv, 0.0)
            finish(gs, dug, dung, xg)
            finish(vs, duv, dunv, xv)

    wide = lambda rows, fn: pl.BlockSpec((rows, 2 * F), fn)
    nxt = lambda i: (jnp.minimum((i + 1) * hb, nb - 1), 0)
    return _pcall(body, name="conv_bwd",
                  out_shape=(jax.ShapeDtypeStruct((T, 2 * F), BF16),
                             jax.ShapeDtypeStruct((CONV_W, 2 * F), F32),
                             jax.ShapeDtypeStruct((1, 2 * F), F32)),
                  grid=(nt,),
                  in_specs=[wide(tt, lambda i: (i, 0)),
                            wide(SUBLANES, lambda i: (jnp.maximum(i * hb - 1, 0), 0)),
                            wide(SUBLANES, nxt),
                            pl.BlockSpec((tt, F), lambda i: (i, 0)),
                            pl.BlockSpec((2 * SUBLANES, F),
                                         lambda i: (jnp.minimum((i + 1) * (hb // 2), nb // 2 - 1), 0)),
                            wide(CONV_W, lambda i: (0, 0)), wide(1, lambda i: (0, 0))],
                  out_specs=(wide(tt, lambda i: (i, 0)), wide(CONV_W, lambda i: (0, 0)),
                             wide(1, lambda i: (0, 0))),
                  semantics=("arbitrary",))(u0, u0, u0, df, df, conv_w, conv_b)


def _adamw(w, g, m, v, *, name):
    R, C = w.shape
    tr = _tile(R, max(SUBLANES, (1 << 19) // max(C, 1) // SUBLANES * SUBLANES), SUBLANES)
    c1 = 1.0 / (1.0 - ADAM_B1 ** ADAM_STEP)
    c2 = 1.0 / (1.0 - ADAM_B2 ** ADAM_STEP)

    def body(w_ref, g_ref, m_ref, v_ref, d_ref, mo_ref, vo_ref):
        gv = g_ref[...]
        mn = ADAM_B1 * m_ref[...] + (1.0 - ADAM_B1) * gv
        vn = ADAM_B2 * v_ref[...] + (1.0 - ADAM_B2) * (gv * gv)
        d_ref[...] = -ADAM_LR * ((mn * c1) / (jnp.sqrt(vn * c2) + ADAM_EPS) + ADAM_WD * w_ref[...])
        mo_ref[...] = mn
        vo_ref[...] = vn

    blk = pl.BlockSpec((tr, C), lambda i: (i, 0))
    shp = jax.ShapeDtypeStruct((R, C), F32)
    return _pcall(body, name=name, out_shape=(shp, shp, shp), grid=(R // tr,),
                  in_specs=[blk] * 4, out_specs=(blk,) * 3, semantics=("parallel",))(w, g, m, v)


def _blk(h, C, elems=1 << 19, align=16):
    th = _tile(h, max(align, elems // C // align * align), align)
    if th < h or h * C <= 2 * elems:
        return th, C
    return h, _tile(C, max(LANES, elems // h // LANES * LANES))


def _adamw_halves(w, m, v, g_mine, g_other, c_idx, *, name):
    _, h, C = w.shape
    th, tc = _blk(h, C, align=SUBLANES)
    c1 = 1.0 / (1.0 - ADAM_B1 ** ADAM_STEP)
    c2 = 1.0 / (1.0 - ADAM_B2 ** ADAM_STEP)

    def body(c_ref, w_ref, m_ref, v_ref, gm_ref, go_ref, g_ref, d_ref, mo_ref, vo_ref):
        gv = jnp.where(pl.program_id(0) == c_ref[0], gm_ref[...], go_ref[...])
        mn = ADAM_B1 * m_ref[...] + (1.0 - ADAM_B1) * gv
        vn = ADAM_B2 * v_ref[...] + (1.0 - ADAM_B2) * (gv * gv)
        d_ref[...] = -ADAM_LR * ((mn * c1) / (jnp.sqrt(vn * c2) + ADAM_EPS) + ADAM_WD * w_ref[...])
        g_ref[...] = gv
        mo_ref[...] = mn
        vo_ref[...] = vn

    blk = pl.BlockSpec((None, th, tc), lambda s, i, j, c: (s, i, j))

    def pick(mine):
        def index(s, i, j, c):
            use = (s == c[0]) if mine else (s != c[0])
            return jnp.where(use, i, 0), jnp.where(use, j, 0)
        return pl.BlockSpec((th, tc), index)

    shp = jax.ShapeDtypeStruct((2, h, C), F32)
    return _pcall(body, name=name, out_shape=(shp,) * 4, grid=(2, h // th, C // tc), prefetch=1,
                  in_specs=[blk, blk, blk, pick(True), pick(False)], out_specs=(blk,) * 4,
                  semantics=("parallel", "parallel", "parallel"))(c_idx, w, m, v, g_mine, g_other)


def _mesh_pos():
    x, y, c = lax.axis_index("x"), lax.axis_index("y"), lax.axis_index("c")
    others = [(1 - x, y), (x, 1 - y), (1 - x, 1 - y)]
    return x, y, c, others


def _gather_copies(shards, lands, send_sems, recv_sems):
    x, y, c, others = _mesh_pos()
    me = 2 * x + y
    return [pltpu.make_async_remote_copy(
        src_ref=shards[a].at[c], dst_ref=lands[a].at[me, c],
        send_sem=send_sems.at[3 * a + j], recv_sem=recv_sems.at[3 * a + j],
        device_id=(*chip, c), device_id_type=MESH)
        for a in range(len(shards)) for j, chip in enumerate(others)]


def _pass_copies(shards, zones, send_sems, recv_sems):
    x, y, c, others = _mesh_pos()
    me = 2 * x + y
    copies = []
    for a in range(len(shards)):
        srcs = [zones[a].at[2 * chip[0] + chip[1], c] for chip in others] + [shards[a]]
        dsts = [zones[a].at[2 * chip[0] + chip[1], c] for chip in others] + [zones[a].at[me]]
        copies += [pltpu.make_async_remote_copy(
            src_ref=s, dst_ref=d, send_sem=send_sems.at[4 * a + k], recv_sem=recv_sems.at[4 * a + k],
            device_id=(x, y, 1 - c), device_id_type=MESH) for k, (s, d) in enumerate(zip(srcs, dsts))]
    return copies


def _exchange_copies(grads, recvs, send_sems, recv_sems):
    x, y, c, _ = _mesh_pos()
    return [pltpu.make_async_remote_copy(
        src_ref=grads[a].at[:, 1 - c], dst_ref=recvs[a], send_sem=send_sems.at[a],
        recv_sem=recv_sems.at[a], device_id=(x, y, 1 - c), device_id_type=MESH) for a in range(len(grads))]


def _split_start(copies, per, srcs, zones, after, *, name):
    n = len(srcs)
    HBM = pl.BlockSpec(memory_space=pltpu.HBM)
    SEM = pl.BlockSpec(memory_space=pltpu.SEMAPHORE)

    def body(*refs):
        send_sems, recv_sems = refs[2 * n + 1], refs[2 * n + 2]
        for cp in copies(refs[:n], refs[n:2 * n], send_sems, recv_sems):
            cp.start()
        refs[-1][...] = jnp.zeros_like(refs[-1])

    hbm = lambda a: pltpu.HBM(a.shape, a.dtype)
    res = _pcall(body, name=name,
                 out_shape=(pltpu.SemaphoreType.DMA((per * n,)), pltpu.SemaphoreType.DMA((per * n,)),
                            *[hbm(a) for a in srcs], *[hbm(a) for a in zones],
                            jax.ShapeDtypeStruct((SUBLANES, LANES), F32)),
                 in_specs=[*[HBM] * (2 * n), pl.BlockSpec(memory_space=pl.ANY)],
                 out_specs=(SEM, SEM, *[HBM] * (2 * n), pl.BlockSpec(memory_space=pltpu.VMEM)),
                 aliases={i: 2 + i for i in range(2 * n)}, split_copy=True)(
        *[pltpu.with_memory_space_constraint(a, pltpu.HBM) for a in [*srcs, *zones]], after)
    return res[0], res[1], list(res[2:2 + n]), list(res[2 + n:2 + 2 * n]), res[-1]


def _split_wait(copies, send_sems, recv_sems, srcs, zones, after, *, name):
    n = len(srcs)
    HBM = pl.BlockSpec(memory_space=pltpu.HBM)
    SEM = pl.BlockSpec(memory_space=pltpu.SEMAPHORE)

    def body(*refs):
        for cp in copies(refs[:n], refs[n:2 * n], refs[2 * n], refs[2 * n + 1]):
            cp.wait_send()
            cp.wait_recv()

    hbm = lambda a: pltpu.HBM(a.shape, a.dtype)
    res = _pcall(body, name=name, out_shape=(*[hbm(a) for a in srcs], *[hbm(a) for a in zones]),
                 in_specs=[*[HBM] * (2 * n), SEM, SEM, pl.BlockSpec(memory_space=pl.ANY)],
                 out_specs=tuple([HBM] * (2 * n)), aliases={i: i for i in range(2 * n)},
                 split_copy=True)(*srcs, *zones, send_sems, recv_sems, after)
    return list(res[:n]), list(res[n:])


def _add_halves(grad, recv, c_idx, *, name):
    S, _, h, C = grad.shape
    th, tc = _blk(h, C)

    def body(c_ref, g_ref, r_ref, o_ref):
        o_ref[...] = (g_ref[...].astype(F32) + r_ref[...].astype(F32)).astype(o_ref.dtype)

    return _pcall(body, name=name, out_shape=jax.ShapeDtypeStruct((S, h, C), grad.dtype),
                  grid=(S, h // th, C // tc), prefetch=1,
                  in_specs=[pl.BlockSpec((None, None, th, tc), lambda s, i, j, c: (s, c[0], i, j)),
                            pl.BlockSpec((None, th, tc), lambda s, i, j, c: (s, i, j))],
                  out_specs=pl.BlockSpec((None, th, tc), lambda s, i, j, c: (s, i, j)),
                  semantics=("parallel", "parallel", "parallel"))(c_idx, grad, recv)


def _scatter_copies(srcs, lands, send_sems, recv_sems):
    x, y, c, others = _mesh_pos()
    return [pltpu.make_async_remote_copy(
        src_ref=srcs[a].at[2 * chip[0] + chip[1]], dst_ref=lands[a].at[j],
        send_sem=send_sems.at[3 * a + j], recv_sem=recv_sems.at[3 * a + j],
        device_id=(*chip, c), device_id_type=MESH)
        for a in range(len(srcs)) for j, chip in enumerate(others)]


def _add_chips(sums, recv, chip_idx, *, name):
    _, h, C = sums.shape
    th, tc = _blk(h, C)

    def body(k_ref, s_ref, r_ref, o_ref):
        acc = s_ref[...].astype(F32) + r_ref[0].astype(F32)
        acc = acc + r_ref[1].astype(F32)
        o_ref[...] = acc + r_ref[2].astype(F32)

    return _pcall(body, name=name, out_shape=jax.ShapeDtypeStruct((h, C), F32),
                  grid=(h // th, C // tc), prefetch=1,
                  in_specs=[pl.BlockSpec((None, th, tc), lambda i, j, k: (k[0], i, j)),
                            pl.BlockSpec((3, th, tc), lambda i, j, k: (0, i, j))],
                  out_specs=pl.BlockSpec((th, tc), lambda i, j, k: (i, j)),
                  semantics=("parallel", "parallel"))(chip_idx, sums, recv)


def _swap_halves(halves, *, name):
    n = len(halves)
    ANY = pl.BlockSpec(memory_space=pl.ANY)

    def body(*refs):
        ins, outs = refs[:n], refs[n:2 * n]
        send_sems, recv_sems = refs[2 * n:]
        x, y, c, _ = _mesh_pos()
        copies = [pltpu.make_async_remote_copy(
            src_ref=ins[a], dst_ref=outs[a], send_sem=send_sems.at[a], recv_sem=recv_sems.at[a],
            device_id=(x, y, 1 - c), device_id_type=MESH) for a in range(n)]
        for cp in copies:
            cp.start()
        for cp in copies:
            cp.wait()

    return _pcall(body, name=name,
                  out_shape=[jax.ShapeDtypeStruct(s.shape, s.dtype) for s in halves],
                  in_specs=[ANY] * n, out_specs=[ANY] * n,
                  scratch_shapes=[pltpu.SemaphoreType.DMA((n,)), pltpu.SemaphoreType.DMA((n,))])(*halves)


def _all_reduce_small(buf):
    R, L = buf.shape
    NDEV = 8

    def body(x_ref, sum_ref, all_ref, send_sems, recv_sems, local_sem):
        x, y, c, others = _mesh_pos()
        me, sibling = (x, y, c), (x, y, 1 - c)

        def slot(px, py, pc):
            return all_ref.at[4 * px + 2 * py + pc]

        def copy(k, block, to, src=None):
            return pltpu.make_async_remote_copy(
                src_ref=slot(*block) if src is None else src, dst_ref=slot(*block),
                send_sem=send_sems.at[k], recv_sem=recv_sems.at[k], device_id=to, device_id_type=MESH)

        mine = pltpu.make_async_copy(x_ref, slot(*me), local_sem)
        mine.start()
        first = [copy(0, me, sibling, src=x_ref)]
        first += [copy(1 + j, me, (*chip, c), src=x_ref) for j, chip in enumerate(others)]
        for cp in first:
            cp.start()
        passed = [copy(4 + j, (*chip, c), sibling) for j, chip in enumerate(others)]
        for j, chip in enumerate(others):
            copy(1 + j, (*chip, c), me).wait_recv()
            passed[j].start()
        copy(0, sibling, me).wait_recv()
        for j, chip in enumerate(others):
            copy(4 + j, (*chip, 1 - c), me).wait_recv()
        for cp in first + passed:
            cp.wait_send()
        mine.wait()
        acc = all_ref[0]
        for d in range(1, NDEV):
            acc = acc + all_ref[d]
        sum_ref[...] = acc

    VM = pl.BlockSpec(memory_space=pltpu.VMEM)
    return _pcall(body, name="all_reduce_small",
                  out_shape=(jax.ShapeDtypeStruct((R, L), F32), jax.ShapeDtypeStruct((NDEV, R, L), F32)),
                  in_specs=[VM], out_specs=(VM, VM),
                  scratch_shapes=[pltpu.SemaphoreType.DMA((7,)), pltpu.SemaphoreType.DMA((7,)),
                                  pltpu.SemaphoreType.DMA])(buf)[0]


def _pack(arrs, rows_multiple=16):
    flat = [a.reshape(-1).astype(F32) for a in arrs]
    sizes = [f.shape[0] for f in flat]
    total = sum(sizes)
    per = LANES * rows_multiple
    padded = -(-total // per) * per
    flat.append(jnp.zeros((padded - total,), F32))
    offs = [0]
    for s in sizes:
        offs.append(offs[-1] + s)
    return jnp.concatenate(flat).reshape(padded // LANES, LANES), offs


def _unpack(buf, offs, shapes):
    flat = buf.reshape(-1)
    return [flat[offs[i]:offs[i + 1]].reshape(s) for i, s in enumerate(shapes)]


def kernel(x, mem, g_mix, w_in, w_a2, b_a, g_gla, w_pool, pool_scale, w_branch, w_out, g_cross, g_mem, w_cq, w_ckv, w_co, g_ffn, w_up, conv_w, conv_b, w_down, g_final, loss_target, m_g_mix, m_w_in, m_w_a2, m_b_a, m_g_gla, m_w_pool, m_pool_scale, m_w_branch, m_w_out, m_g_cross, m_g_mem, m_w_cq, m_w_ckv, m_w_co, m_g_ffn, m_w_up, m_conv_w, m_conv_b, m_w_down, m_g_final, v_g_mix, v_w_in, v_w_a2, v_b_a, v_g_gla, v_w_pool, v_pool_scale, v_w_branch, v_w_out, v_g_cross, v_g_mem, v_w_cq, v_w_ckv, v_w_co, v_g_ffn, v_w_up, v_conv_w, v_conv_b, v_w_down, v_g_final):
    weights = dict(g_mix=g_mix, w_in=w_in, w_a2=w_a2, b_a=b_a, g_gla=g_gla, w_pool=w_pool,
                   pool_scale=pool_scale, w_branch=w_branch, w_out=w_out, g_cross=g_cross, g_mem=g_mem,
                   w_cq=w_cq, w_ckv=w_ckv, w_co=w_co, g_ffn=g_ffn, w_up=w_up, conv_w=conv_w,
                   conv_b=conv_b, w_down=w_down, g_final=g_final)
    mom_m = dict(g_mix=m_g_mix, w_in=m_w_in, w_a2=m_w_a2, b_a=m_b_a, g_gla=m_g_gla, w_pool=m_w_pool,
                 pool_scale=m_pool_scale, w_branch=m_w_branch, w_out=m_w_out, g_cross=m_g_cross,
                 g_mem=m_g_mem, w_cq=m_w_cq, w_ckv=m_w_ckv, w_co=m_w_co, g_ffn=m_g_ffn, w_up=m_w_up,
                 conv_w=m_conv_w, conv_b=m_conv_b, w_down=m_w_down, g_final=m_g_final)
    mom_v = dict(g_mix=v_g_mix, w_in=v_w_in, w_a2=v_w_a2, b_a=v_b_a, g_gla=v_g_gla, w_pool=v_w_pool,
                 pool_scale=v_pool_scale, w_branch=v_w_branch, w_out=v_w_out, g_cross=v_g_cross,
                 g_mem=v_g_mem, w_cq=v_w_cq, w_ckv=v_w_ckv, w_co=v_w_co, g_ffn=v_g_ffn, w_up=v_w_up,
                 conv_w=v_conv_w, conv_b=v_conv_b, w_down=v_w_down, g_final=v_g_final)
    order = list(weights)
    big = ["w_in", "w_branch", "w_out", "w_cq", "w_ckv", "w_co", "w_up", "w_down"]
    small_sharded = ["w_a2", "w_pool", "conv_w"]
    small_repl = ["g_mix", "b_a", "g_gla", "pool_scale", "g_cross", "g_mem", "g_ffn", "conv_b", "g_final"]

    xs, ms, tgt = x[0], mem[0], loss_target[0]
    T, D = xs.shape
    M = ms.shape[0]
    DK, DV, PW = b_a.shape[1], g_gla.shape[1], pool_scale.shape[1]
    RANK = w_a2.shape[1]
    F2 = conv_b.shape[1]
    F = F2 // 2
    DIN = N_CHIPS * w_in.shape[2]
    OFF_A = 2 * DK + 2 * DV
    OFF_P = OFF_A + RANK
    RP = LANES
    GW = PW // POOL_GROUPS
    assert PW == DV and 4 * DV == 2 * D and OFF_P + PW + 2 * D == DIN

    cx, cy, cc = lax.axis_index("x"), lax.axis_index("y"), lax.axis_index("c")
    chip = 2 * cx + cy
    c_idx = jnp.reshape(cc, (1,)).astype(jnp.int32)
    chip_idx = jnp.reshape(chip, (1,)).astype(jnp.int32)

    def halves(a):
        return a.reshape(2, a.shape[0] // 2, a.shape[1])

    shard2d = {k: (weights[k][0].T if k == "w_in" else weights[k][0]) for k in big}
    small_pack, small_offs = _pack([weights[k][0] for k in small_sharded], rows_multiple=32)
    shard_halves = {k: halves(shard2d[k].astype(BF16)) for k in big}
    shard_halves["small"] = halves(small_pack)
    flying, passing = {}, {}
    tok = small_pack
    for group, keys in (("in", ["w_in", "small"]), ("mix", ["w_branch", "w_out"]),
                        ("cross", ["w_cq", "w_ckv", "w_co"]), ("up", ["w_up"]), ("down", ["w_down"])):
        srcs = [shard_halves[k] for k in keys]
        zones = [lax.empty((N_CHIPS, *s.shape), s.dtype) for s in srcs]
        s_sems, r_sems, srcs, zones, tok = _split_start(_gather_copies, 3, srcs, zones, tok,
                                                        name=f"gather_start_{group}")
        flying[group] = (keys, s_sems, r_sems, srcs, zones)

    def landed(group, after):
        keys, s_sems, r_sems, srcs, zones = flying[group]
        srcs, zones = _split_wait(_gather_copies, s_sems, r_sems, srcs, zones, after,
                                  name=f"gather_wait_{group}")
        s_sems, r_sems, srcs, zones, token = _split_start(_pass_copies, 4, srcs, zones, after,
                                                          name=f"gather_pass_start_{group}")
        passing[group] = (keys, s_sems, r_sems, srcs, zones)
        return token

    def arrive(group, after):
        keys, s_sems, r_sems, srcs, zones = passing[group]
        _, full = _split_wait(_pass_copies, s_sems, r_sems, srcs, zones, after,
                              name=f"gather_pass_wait_{group}")
        return {k: f.reshape(N_CHIPS, f.shape[1] * f.shape[2], f.shape[3]) for k, f in zip(keys, full)}

    def rows(g):
        return g.reshape(-1, g.shape[2])

    h1, r1 = _rms_fwd(xs, g_mix + tok[0:1, 0:1], name="norm_mix")
    landed("in", h1)
    gw = arrive("in", h1)
    small_all = gw["small"]
    W_in = rows(gw["w_in"])
    W_main = jnp.concatenate([W_in[:OFF_A], W_in[OFF_P:]], axis=0)
    W_a = jnp.pad(W_in[OFF_A:OFF_P], ((0, RP - RANK), (0, 0)))
    sm = [_unpack(small_all[j], small_offs, [weights[k].shape[1:] for k in small_sharded]) for j in range(N_CHIPS)]
    W_a2 = jnp.concatenate([sm[j][0] for j in range(N_CHIPS)], axis=1)
    W_a2p = jnp.pad(W_a2, ((0, RP - RANK), (0, 0))).astype(BF16)
    W_pool = jnp.concatenate([sm[j][1] for j in range(N_CHIPS)], axis=1).astype(BF16)
    W_conv = jnp.concatenate([sm[j][2] for j in range(N_CHIPS)], axis=1)

    proj = _mm(h1, W_main, "nt", name="proj_main", out_dtype=F32)
    tok = landed("mix", proj)
    a_pad = _mm(h1, W_a, "nt", name="proj_gate_rank", out_dtype=F32, after=tok)
    o_gla, o_raw, states = _gla_fwd(proj, a_pad, W_a2p, b_a, g_gla, T=T, DK=DK, DV=DV)
    o_pool = _pool_fwd(proj, W_pool, pool_scale, T=T, PW=PW, col_block=3)
    gw = arrive("mix", o_pool)
    W_branch, W_out = rows(gw["w_branch"]), rows(gw["w_out"])
    tok = landed("cross", o_pool)
    y_gla = _mm(o_gla, W_branch, "nn", name="branch_gla", out_dtype=F32, K=DV, after=tok)
    y_pool = _mm(o_pool, W_branch, "nn", name="branch_pool", out_dtype=F32, K=PW, b_off=(DV, 0))
    merged = _merge_fwd(y_gla, y_pool, proj, T=T, D=D, col_block=2)
    x1 = _mm(merged, W_out, "nn", name="mix_out", out_dtype=F32, add=xs)

    h2, r2 = _rms_fwd(x1, g_cross, name="norm_cross")
    mem_n, rm = _rms_fwd(ms, g_mem, name="norm_mem")
    gw = arrive("cross", h2)
    W_cq, W_ckv, W_co = rows(gw["w_cq"]), gw["w_ckv"], rows(gw["w_co"])
    qc = _mm(h2, W_cq, "nn", name="cross_q", out_dtype=BF16)
    kv = _mm(mem_n, W_ckv, "nn", name="cross_kv", out_dtype=BF16, b_blocked=True)
    o_att = _attn_fwd(qc, kv, T=T, D=D, M=M)
    x2 = _mm(o_att, W_co, "nn", name="cross_out", out_dtype=F32, add=x1)

    tok = landed("up", x2)
    h3, r3 = _rms_fwd(x2, g_ffn + tok[0:1, 0:1], name="norm_ffn")
    W_up = arrive("up", h3)["w_up"]
    u0 = _mm(h3, W_up, "nn", name="ffn_up", out_dtype=F32, b_blocked=True)
    tok = landed("down", u0)
    f_act = _conv_fwd(u0, W_conv, conv_b + tok[0:1, 0:1], T=T, F=F)
    W_down = rows(arrive("down", f_act)["w_down"])
    x3 =_mm(f_act, W_down, "nn", name="ffn_down", out_dtype=F32, add=x2)

    loss_part, dx3, dx3_b, dg_final = _loss_head(x3, g_final.reshape(1, D), tgt)

    def col_shards(g):
        nb, K, Nb = g.shape
        return g.reshape(nb, 2, K // 2, Nb)

    def row_shards(g):
        R, N = g.shape
        return g.reshape(N_CHIPS, 2, R // N_CHIPS // 2, N)

    exchanging, in_flight = {}, []

    def exchange_start(group, keys, partials, after):
        recvs = [lax.empty((p.shape[0], *p.shape[2:]), p.dtype) for p in partials]
        s_sems, r_sems, partials, recvs, token = _split_start(
            _exchange_copies, 1, partials, recvs, after, name=f"grad_exchange_start_{group}")
        exchanging[group] = (keys, s_sems, r_sems, partials, recvs)
        return token

    def scatter_start(group, after):
        keys, s_sems, r_sems, partials, recvs = exchanging[group]
        partials, recvs = _split_wait(_exchange_copies, s_sems, r_sems, partials, recvs, after,
                                      name=f"grad_exchange_wait_{group}")
        chip_sums = [_add_halves(p, r, c_idx, name=f"grad_add_halves_{k}")
                     for k, p, r in zip(keys, partials, recvs)]
        lands = [lax.empty((3, *s.shape[1:]), s.dtype) for s in chip_sums]
        s_sems, r_sems, sums, lands, token = _split_start(
            _scatter_copies, 3, chip_sums, lands, after, name=f"grad_scatter_start_{group}")
        in_flight.append((group, keys, s_sems, r_sems, sums, lands))
        return token

    df = _mm(dx3_b, W_down, "nt", name="d_ffn_act", out_dtype=BF16)
    dW_down = _mm(f_act, dx3_b, "tn", name="dw_down", out_dtype=BF16)
    du0, dconv_w, dconv_b = _conv_bwd(u0, W_conv, conv_b, df, T=T, F=F)
    dh3 = _mm(du0, W_up, "nt", name="d_ffn_in", out_dtype=F32, b_blocked=True)
    dW_up = _mm(h3, du0, "tn", name="dw_up", out_dtype=BF16, out_blocks=N_CHIPS)
    tok = exchange_start("ffn", ["w_down", "w_up"], [row_shards(dW_down), col_shards(dW_up)], dh3)
    dx2, dx2_b, dg_ffn = _rms_bwd(dh3, x2, r3 + tok[0:1, 0:1], g_ffn, dx3, name="norm_ffn_bwd")

    do_att = _mm(dx2_b, W_co, "nt", name="d_cross_o", out_dtype=BF16)
    dW_co = _mm(o_att, dx2_b, "tn", name="dw_co", out_dtype=BF16)
    tok = scatter_start("ffn", dW_co)
    dq, dkv = _attn_bwd(qc, kv, do_att, T=T, D=D, M=M)
    dkv_b = dkv.astype(BF16)
    dW_cq = _mm(h2, dq, "tn", name="dw_cq", out_dtype=BF16, after=tok)
    dh2 = _mm(dq, W_cq, "nt", name="d_cross_in", out_dtype=F32)
    dW_ckv = _mm(mem_n, dkv_b, "tn", name="dw_ckv", out_dtype=BF16, out_blocks=N_CHIPS)
    dmem_n = _mm(dkv_b, W_ckv, "nt", name="d_mem", out_dtype=F32, b_blocked=True)
    tok = exchange_start("cross", ["w_co", "w_cq", "w_ckv"],
                         [row_shards(dW_co), row_shards(dW_cq), col_shards(dW_ckv)], dmem_n)
    _, _, dg_mem = _rms_bwd(dmem_n, ms, rm, g_mem, None, name="norm_mem_bwd")
    dx1, dx1_b, dg_cross = _rms_bwd(dh2, x1, r2 + tok[0:1, 0:1], g_cross, dx2, name="norm_cross_bwd")

    dmerged = _mm(dx1_b, W_out, "nt", name="d_merged", out_dtype=F32)
    dW_out = _mm(merged, dx1_b, "tn", name="dw_out", out_dtype=BF16)
    tok = scatter_start("cross", dW_out)
    dy_gla, dy_pool, dgates = _merge_bwd(dmerged, y_gla, y_pool, proj, T=T, D=D, col_block=2)
    dW_br_gla = _mm(o_gla, dy_gla, "tn", name="dw_branch_gla", out_dtype=BF16, after=tok)
    dW_br_pool = _mm(o_pool, dy_pool, "tn", name="dw_branch_pool", out_dtype=BF16)
    tok = exchange_start("mix", ["w_out", "w_branch"],
                         [row_shards(dW_out), row_shards(jnp.concatenate([dW_br_gla, dW_br_pool], axis=0))],
                         dW_br_pool)
    do_gla = _mm(dy_gla, W_branch, "nt", name="d_o_gla", out_dtype=F32, N=DV, after=tok)
    do_pool = _mm(dy_pool, W_branch, "nt", name="d_o_pool", out_dtype=F32, N=PW, b_off=(DV, 0))
    dp, dw_pool, dpool_scale = _pool_bwd(proj, W_pool, pool_scale, do_pool, T=T, PW=PW, col_block=3)
    tok = scatter_start("mix", dp)
    dqkvr, da_pad, dw2, db_a, dg_gla = _gla_bwd(proj, a_pad, W_a2p, b_a + tok[0:1, 0:1], g_gla, o_raw, states,
                                               do_gla, T=T, DK=DK, DV=DV)
    dproj = jnp.concatenate([dqkvr, dp, dgates], axis=1)
    dW_main = _mm(dproj, h1, "tn", name="dw_in_main", out_dtype=BF16)
    dW_a = _mm(da_pad, h1, "tn", name="dw_in_rank", out_dtype=BF16)
    dW_in = jnp.concatenate([dW_main[:OFF_A], dW_a[:RANK], dW_main[OFF_A:]], axis=0)
    tok = exchange_start("in", ["w_in"], [row_shards(dW_in)], dW_a)
    dh1 = _mm(dproj, W_main, "nn", name="d_mix_in_main", out_dtype=F32, after=tok)
    dh1 = _mm(da_pad, W_a, "nn", name="d_mix_in_rank", out_dtype=F32, add=dh1)
    dx0, _, dg_mix = _rms_bwd(dh1, xs, r1, g_mix, dx1, name="norm_mix_bwd")

    grads = {}

    small_grads = [loss_part, dg_mix, db_a, dg_gla, dpool_scale, dg_cross, dg_mem, dg_ffn, dconv_b, dg_final,
                   dw2[:RANK], dw_pool, dconv_w]
    small_buf, offs = _pack(small_grads)
    small_sum = _all_reduce_small(small_buf)
    red = _unpack(small_sum, offs, [g.shape for g in small_grads])
    loss = red[0][0, 0]
    for k, g in zip(small_repl, red[1:10]):
        grads[k] = g.reshape(weights[k].shape)
    nb = DK // N_CHIPS
    grads["w_a2"] = lax.dynamic_slice_in_dim(red[10], chip * nb, nb, axis=1)[None]
    nb = GW // N_CHIPS
    grads["w_pool"] = lax.dynamic_slice_in_dim(red[11], chip * nb, nb, axis=1)[None]
    nb = F2 // N_CHIPS
    grads["conv_w"] = lax.dynamic_slice_in_dim(red[12], chip * nb, nb, axis=1)[None]

    delta, new_m, new_v = {}, {}, {}

    def whole(k, a):
        a = a.reshape(-1, a.shape[2])
        return (a.T if k == "w_in" else a)[None]

    scatter_start("in", small_sum)
    after = in_flight[-1][4][0]

    for group, keys, s_sems, r_sems, sums, lands in in_flight:
        sums, from_chips = _split_wait(_scatter_copies, s_sems, r_sems, sums, lands, after,
                                       name=f"grad_scatter_wait_{group}")
        half_sums = [_add_chips(s, r, chip_idx, name=f"grad_add_chips_{k}") for k, s, r in zip(keys, sums, from_chips)]
        other_sums = _swap_halves(half_sums, name=f"grad_swap_halves_{group}")
        for k, mine, other in zip(keys, half_sums, other_sums):
            wmv = [halves(src[k][0].T if k == "w_in" else src[k][0]) for src in (weights, mom_m, mom_v)]
            res = _adamw_halves(*wmv, mine, other, c_idx, name=f"adamw_{k}")
            grads[k], delta[k], new_m[k], new_v[k] = (whole(k, a) for a in res)
            after = res[1]
    small = small_repl + small_sharded
    packs = [_pack([src[k] for k in small])[0] for src in (weights, grads, mom_m, mom_v)]
    _, offs = _pack([weights[k] for k in small])
    outs = _adamw(*packs, name="adamw_small")
    for res, o in zip((delta, new_m, new_v), outs):
        for k, a in zip(small, _unpack(o, offs, [weights[k].shape for k in small])):
            res[k] = a

    return (loss, dx0[None], *[grads[k] for k in order], *[delta[k] for k in order],
            *[new_m[k] for k in order], *[new_v[k] for k in order])
```

```python
import functools

import jax
import jax.numpy as jnp
from jax import lax
from jax.experimental import pallas as pl
from jax.experimental.pallas import tpu as pltpu

F32 = jnp.float32
BF16 = jnp.bfloat16
MESH = pl.DeviceIdType.MESH
HIGHEST = lax.Precision.HIGHEST

EPS = 1e-6
GLA_HEADS = 4
GLA_CHUNK = 64
GLA_GATE_NORM = 16.0
POOL_GROUPS = 4
CROSS_HEADS = 4
CONV_W = 3
N_CHIPS = 4
LANES = 128
SUBLANES = 8
VMEM_LIMIT = 56 << 20

ADAM_LR = 0.001
ADAM_B1 = 0.9
ADAM_B2 = 0.999
ADAM_EPS = 1e-08
ADAM_WD = 0.01
ADAM_STEP = 10

NN = (((1,), (0,)), ((), ()))
NT = (((1,), (1,)), ((), ()))
TN = (((0,), (0,)), ((), ()))


ONE_PASS = lax.Precision.HIGH


def _dot(a, b, dn=NN, precision=None):
    return lax.dot_general(a, b, dn, precision=precision, preferred_element_type=F32)


def _tile(n, pref, align=LANES):
    t = (min(pref, n) // align) * align
    while t >= align:
        if n % t == 0:
            return t
        t -= align
    return n


def _pcall(body, *, name, out_shape, grid=(), in_specs=None, out_specs=None, scratch_shapes=(),
           semantics=None, prefetch=0, aliases=None, split_copy=False):
    params = dict(vmem_limit_bytes=VMEM_LIMIT)
    if semantics is not None:
        params["dimension_semantics"] = semantics
    if split_copy:
        params["has_side_effects"] = pltpu.SideEffectType.DATAFLOW_SIDE_EFFECTING
    if prefetch:
        grid_spec = pltpu.PrefetchScalarGridSpec(
            num_scalar_prefetch=prefetch, grid=grid, in_specs=in_specs, out_specs=out_specs,
            scratch_shapes=scratch_shapes)
        return pl.pallas_call(body, name=name, out_shape=out_shape, grid_spec=grid_spec,
                              compiler_params=pltpu.CompilerParams(**params))
    kw = {}
    if aliases is not None:
        kw["input_output_aliases"] = aliases
    if in_specs is not None:
        kw["in_specs"] = in_specs
    if out_specs is not None:
        kw["out_specs"] = out_specs
    return pl.pallas_call(body, name=name, out_shape=out_shape, grid=grid,
                          scratch_shapes=scratch_shapes,
                          compiler_params=pltpu.CompilerParams(**params), **kw)


def _sigmoid(x):
    return 1.0 / (1.0 + jnp.exp(-x))


def _log_sigmoid(x):
    return jnp.minimum(x, 0.0) - jnp.log(1.0 + jnp.exp(-jnp.abs(x)))


def _mm(a, b, mode, *, name, out_dtype, M=None, N=None, K=None, a_off=(0, 0), b_off=(0, 0),
        add=None, b_blocked=False, out_blocks=0, after=None, tm=1536, tn=1536, tk=2048):
    if b_blocked:
        nb, R, Cb = b.shape
        b_rows, b_cols = R, nb * Cb
    else:
        b_rows, b_cols = b.shape
    if mode == "nn":
        M = M or a.shape[0]; K = K or a.shape[1]; N = N or b_cols
    elif mode == "nt":
        M = M or a.shape[0]; K = K or a.shape[1]; N = N or b_rows
    else:
        K = K or a.shape[0]; M = M or a.shape[1]; N = N or b_cols
    tm = _tile(M, tm, LANES if mode == "tn" else 16)
    tn = _tile(Cb if (b_blocked and mode != "nt") else (N // out_blocks if out_blocks else N), tn)
    tk = _tile(Cb if (b_blocked and mode == "nt") else K, tk)
    nk = K // tk
    dn = {"nn": NN, "nt": NT, "tn": TN}[mode]

    def off(o, t):
        assert o % t == 0, (name, o, t)
        return o // t

    if mode == "tn":
        ar, ac = off(a_off[0], tk), off(a_off[1], tm)
        a_spec = pl.BlockSpec((tk, tm), lambda i, j, k: (k + ar, i + ac))
    else:
        ar, ac = off(a_off[0], tm), off(a_off[1], tk)
        a_spec = pl.BlockSpec((tm, tk), lambda i, j, k: (i + ar, k + ac))
    if b_blocked and mode == "nt":
        per = Cb // tk
        b_spec = pl.BlockSpec((None, tn, tk), lambda i, j, k: (k // per, j, k % per))
    elif b_blocked:
        per = Cb // tn
        b_spec = pl.BlockSpec((None, tk, tn), lambda i, j, k: (j // per, k, j % per))
    elif mode == "nt":
        br, bc = off(b_off[0], tn), off(b_off[1], tk)
        b_spec = pl.BlockSpec((tn, tk), lambda i, j, k: (j + br, k + bc))
    else:
        br, bc = off(b_off[0], tk), off(b_off[1], tn)
        b_spec = pl.BlockSpec((tk, tn), lambda i, j, k: (k + br, j + bc))
    if out_blocks:
        per_o = N // out_blocks // tn
        o_spec = pl.BlockSpec((None, tm, tn), lambda i, j, k: (j // per_o, i, j % per_o))
        out_shape = jax.ShapeDtypeStruct((out_blocks, M, N // out_blocks), out_dtype)
    else:
        o_spec = pl.BlockSpec((tm, tn), lambda i, j, k: (i, j))
        out_shape = jax.ShapeDtypeStruct((M, N), out_dtype)
    in_specs = [a_spec, b_spec]
    args = [a, b]
    if add is not None:
        assert not out_blocks
        in_specs.append(o_spec)
        args.append(add)
    if after is not None:
        in_specs.append(pl.BlockSpec(memory_space=pl.ANY))
        args.append(after)
    n_in = len(args)

    def finish(r, refs):
        if add is not None:
            r = r + refs[2][...]
        o_ref = refs[n_in]
        o_ref[...] = r.astype(o_ref.dtype)

    def body_one(*refs):
        finish(_dot(refs[0][...].astype(BF16), refs[1][...].astype(BF16), dn), refs)

    def body_acc(*refs):
        acc_ref = refs[-1]
        k = pl.program_id(2)

        @pl.when(k == 0)
        def _():
            acc_ref[...] = jnp.zeros_like(acc_ref)

        acc_ref[...] += _dot(refs[0][...].astype(BF16), refs[1][...].astype(BF16), dn)

        @pl.when(k == nk - 1)
        def _():
            finish(acc_ref[...], refs)

    return _pcall(body_one if nk == 1 else body_acc, name=name, out_shape=out_shape,
                  grid=(M // tm, N // tn, nk), in_specs=in_specs, out_specs=o_spec,
                  scratch_shapes=[] if nk == 1 else [pltpu.VMEM((tm, tn), F32)],
                  semantics=("parallel", "parallel", "arbitrary"))(*args)


def _rms_fwd(x, g, *, name):
    T, D = x.shape
    tr = _tile(T, 128, 16)

    def body(x_ref, g_ref, h_ref, r_ref):
        xv = x_ref[...]
        r = lax.rsqrt(jnp.mean(xv * xv, axis=-1, keepdims=True) + EPS)
        h_ref[...] = (xv * r * g_ref[...]).astype(h_ref.dtype)
        r_ref[...] = r

    row = pl.BlockSpec((tr, D), lambda i: (i, 0))
    return _pcall(body, name=name,
                  out_shape=(jax.ShapeDtypeStruct((T, D), BF16), jax.ShapeDtypeStruct((T, 1), F32)),
                  grid=(T // tr,),
                  in_specs=[row, pl.BlockSpec((1, D), lambda i: (0, 0))],
                  out_specs=(row, pl.BlockSpec((tr, 1), lambda i: (i, 0))),
                  semantics=("parallel",))(x, g)


def _rms_bwd(dh, x, rstd, g, dres, *, name):
    T, D = x.shape
    tr = _tile(T, 128, 16)
    has_res = dres is not None

    def body(*refs):
        if has_res:
            dh_ref, x_ref, r_ref, g_ref, res_ref, dx_ref, dxb_ref, dg_ref = refs
        else:
            dh_ref, x_ref, r_ref, g_ref, dx_ref, dxb_ref, dg_ref = refs
        r = r_ref[...]
        xh = x_ref[...] * r
        dhv = dh_ref[...].astype(F32)
        dxh = dhv * g_ref[...]
        m = jnp.mean(dxh * xh, axis=-1, keepdims=True)
        dx = r * (dxh - xh * m)
        if has_res:
            dx = dx + res_ref[...]
        dx_ref[...] = dx
        dxb_ref[...] = dx.astype(BF16)

        @pl.when(pl.program_id(0) == 0)
        def _():
            dg_ref[...] = jnp.zeros_like(dg_ref)

        dg_ref[...] += jnp.sum(dhv * xh, axis=0, keepdims=True)

    row = pl.BlockSpec((tr, D), lambda i: (i, 0))
    vec = pl.BlockSpec((1, D), lambda i: (0, 0))
    in_specs = [row, row, pl.BlockSpec((tr, 1), lambda i: (i, 0)), vec]
    args = [dh, x, rstd, g]
    if has_res:
        in_specs.append(row)
        args.append(dres)
    return _pcall(body, name=name,
                  out_shape=(jax.ShapeDtypeStruct((T, D), F32), jax.ShapeDtypeStruct((T, D), BF16),
                             jax.ShapeDtypeStruct((1, D), F32)),
                  grid=(T // tr,), in_specs=in_specs, out_specs=(row, row, vec),
                  semantics=("arbitrary",))(*args)


def _loss_head(x3, g, tgt):
    T, D = x3.shape
    tr = _tile(T, 128, 16)

    def body(x_ref, g_ref, t_ref, loss_ref, dx_ref, dxb_ref, dg_ref):
        xv = x_ref[...]
        gv = g_ref[...]
        r = lax.rsqrt(jnp.mean(xv * xv, axis=-1, keepdims=True) + EPS)
        xh = xv * r
        err = xh * gv - t_ref[...]
        dy = err * (1.0 / D)
        dxh = dy * gv
        m = jnp.mean(dxh * xh, axis=-1, keepdims=True)
        dx = r * (dxh - xh * m)
        dx_ref[...] = dx
        dxb_ref[...] = dx.astype(BF16)

        @pl.when(pl.program_id(0) == 0)
        def _():
            dg_ref[...] = jnp.zeros_like(dg_ref)
            loss_ref[...] = jnp.zeros_like(loss_ref)

        dg_ref[...] += jnp.sum(dy * xh, axis=0, keepdims=True)
        part = 0.5 * jnp.sum(jnp.mean(err * err, axis=-1, keepdims=True), axis=0, keepdims=True)
        loss_ref[...] += jnp.broadcast_to(part, loss_ref.shape)

    row = pl.BlockSpec((tr, D), lambda i: (i, 0))
    vec = pl.BlockSpec((1, D), lambda i: (0, 0))
    return _pcall(body, name="loss_head",
                  out_shape=(jax.ShapeDtypeStruct((1, LANES), F32), jax.ShapeDtypeStruct((T, D), F32),
                             jax.ShapeDtypeStruct((T, D), BF16), jax.ShapeDtypeStruct((1, D), F32)),
                  grid=(T // tr,), in_specs=[row, vec, row],
                  out_specs=(pl.BlockSpec((1, LANES), lambda i: (0, 0)), row, row, vec),
                  semantics=("arbitrary",))(x3, g, tgt)


def _gla_chunk_terms(qk, a_ref, w2_ref, ba_ref, DK):
    C = qk.shape[0]
    gp = _dot(a_ref[...].astype(BF16), w2_ref[...]) + ba_ref[...]
    la = _log_sigmoid(gp) * (1.0 / GLA_GATE_NORM)
    row = lax.broadcasted_iota(jnp.int32, (C, C), 0)
    col = lax.broadcasted_iota(jnp.int32, (C, C), 1)
    causal = row >= col
    b = _dot(causal.astype(F32), la, precision=HIGHEST)
    return gp, b, causal


def _gla_fwd(proj, a_pad, w2, b_a, g_gla, *, T, DK, DV):
    assert 2 * DK == DV
    H = GLA_HEADS
    HK, HV = DK // H, DV // H
    C = GLA_CHUNK
    n = T // C
    RP = a_pad.shape[1]
    scale = HK ** -0.5

    def body(qk_ref, v_ref, r_ref, a_ref, w2_ref, ba_ref, gg_ref, og_ref, oraw_ref, st_ref, s_ref):
        @pl.when(pl.program_id(0) == 0)
        def _():
            s_ref[...] = jnp.zeros_like(s_ref)

        st_ref[...] = s_ref[...]
        qk = qk_ref[...]
        _, b, causal = _gla_chunk_terms(qk, a_ref, w2_ref, ba_ref, DK)
        for h in range(H):
            ks = slice(h * HK, (h + 1) * HK)
            vs = slice(h * HV, (h + 1) * HV)
            bh = b[:, ks]
            b_last = bh[C - 1:C, :]
            qt = qk[:, ks] * scale * jnp.exp(bh)
            kh = qk[:, DK + h * HK:DK + (h + 1) * HK]
            kt = kh * jnp.exp(-bh)
            khat = kh * jnp.exp(b_last - bh)
            a_mat = jnp.where(causal, _dot(qt, kt, NT, ONE_PASS), 0.0)
            vh = v_ref[:, vs]
            s_t = s_ref[h]
            o = _dot(a_mat, vh, NN, ONE_PASS) + _dot(qt, s_t, NT, ONE_PASS)
            s_ref[h] = s_t * jnp.exp(b_last) + _dot(vh, khat, TN, ONE_PASS)
            rs = lax.rsqrt(jnp.mean(o * o, axis=-1, keepdims=True) + EPS)
            rr = r_ref[:, vs]
            og = o * rs * gg_ref[:, vs] * (rr * _sigmoid(rr))
            oraw_ref[:, vs] = o
            og_ref[:, vs] = og.astype(BF16)

    blk = lambda j: pl.BlockSpec((C, DV), lambda i: (i, j))
    full = lambda s: pl.BlockSpec(s, lambda i: (0,) * len(s))
    return _pcall(
        body, name="gla_fwd",
        out_shape=(jax.ShapeDtypeStruct((T, DV), BF16), jax.ShapeDtypeStruct((T, DV), F32),
                   jax.ShapeDtypeStruct((n, H, HV, HK), F32)),
        grid=(n,),
        in_specs=[blk(0), blk(1), blk(2), pl.BlockSpec((C, RP), lambda i: (i, 0)),
                  full((RP, DK)), full((1, DK)), full((1, DV))],
        out_specs=(blk(0), blk(0), pl.BlockSpec((None, H, HV, HK), lambda i: (i, 0, 0, 0))),
        scratch_shapes=[pltpu.VMEM((H, HV, HK), F32)],
        semantics=("arbitrary",))(proj, proj, proj, a_pad, w2, b_a, g_gla)


def _gla_bwd(proj, a_pad, w2, b_a, g_gla, o_raw, states, do_gla, *, T, DK, DV):
    H = GLA_HEADS
    HK, HV = DK // H, DV // H
    C = GLA_CHUNK
    n = T // C
    RP = a_pad.shape[1]
    scale = HK ** -0.5

    def body(qk_ref, v_ref, r_ref, a_ref, w2_ref, ba_ref, gg_ref, oraw_ref, st_ref, dog_ref,
             dqkvr_ref, da_ref, dw2_ref, dba_ref, dgg_ref, ds_ref):
        @pl.when(pl.program_id(0) == 0)
        def _():
            ds_ref[...] = jnp.zeros_like(ds_ref)
            dw2_ref[...] = jnp.zeros_like(dw2_ref)
            dba_ref[...] = jnp.zeros_like(dba_ref)
            dgg_ref[...] = jnp.zeros_like(dgg_ref)

        qk = qk_ref[...]
        gp, b, causal = _gla_chunk_terms(qk, a_ref, w2_ref, ba_ref, DK)
        row = lax.broadcasted_iota(jnp.int32, (C, C), 0)
        col = lax.broadcasted_iota(jnp.int32, (C, C), 1)
        upper = (col >= row).astype(F32)
        dla_parts = []
        for h in range(H):
            ks = slice(h * HK, (h + 1) * HK)
            vs = slice(h * HV, (h + 1) * HV)
            bh = b[:, ks]
            b_last = bh[C - 1:C, :]
            eb = jnp.exp(bh)
            emb = jnp.exp(-bh)
            ehat = jnp.exp(b_last - bh)
            e_last = jnp.exp(b_last)
            qt = qk[:, ks] * scale * eb
            kh = qk[:, DK + h * HK:DK + (h + 1) * HK]
            kt = kh * emb
            khat = kh * ehat
            a_mat = jnp.where(causal, _dot(qt, kt, NT, ONE_PASS), 0.0)
            vh = v_ref[:, vs]
            o = oraw_ref[:, vs]
            rs = lax.rsqrt(jnp.mean(o * o, axis=-1, keepdims=True) + EPS)
            on = o * rs
            gg = gg_ref[:, vs]
            rr = r_ref[:, vs]
            sg = _sigmoid(rr)
            d_out = dog_ref[:, vs]
            dr = d_out * (on * gg) * (sg * (1.0 + rr * (1.0 - sg)))
            d_og = d_out * (rr * sg)
            dgg_ref[:, vs] += jnp.sum(d_og * on, axis=0, keepdims=True)
            d_on = d_og * gg
            d_o = rs * (d_on - on * jnp.mean(d_on * on, axis=-1, keepdims=True))
            s_t = st_ref[h]
            ds_t = ds_ref[h]
            d_a = jnp.where(causal, _dot(d_o, vh, NT, ONE_PASS), 0.0)
            dv = _dot(a_mat, d_o, TN, ONE_PASS) + _dot(khat, ds_t, NT, ONE_PASS)
            dqt = _dot(d_a, kt, NN, ONE_PASS) + _dot(d_o, s_t, NN, ONE_PASS)
            dkt = _dot(d_a, qt, TN, ONE_PASS)
            dkhat = _dot(vh, ds_t, NN, ONE_PASS)
            ds_ref[h] = ds_t * e_last + _dot(d_o, qt, TN, ONE_PASS)
            dq = dqt * eb * scale
            dk = dkt * emb + dkhat * ehat
            db = dqt * qt - dkt * kt - dkhat * khat
            d_last = (jnp.sum(dkhat * khat, axis=0, keepdims=True)
                      + e_last * jnp.sum(ds_t * s_t, axis=0, keepdims=True))
            dla_parts.append(_dot(upper, db, NN, HIGHEST) + d_last)
            dqkvr_ref[:, ks] = dq.astype(BF16)
            dqkvr_ref[:, DK + h * HK:DK + (h + 1) * HK] = dk.astype(BF16)
            dqkvr_ref[:, DV + h * HV:DV + (h + 1) * HV] = dv.astype(BF16)
            dqkvr_ref[:, 2 * DV + h * HV:2 * DV + (h + 1) * HV] = dr.astype(BF16)
        dla = jnp.concatenate(dla_parts, axis=1)
        dgp = dla * (1.0 / GLA_GATE_NORM) * _sigmoid(-gp)
        dba_ref[...] += jnp.sum(dgp, axis=0, keepdims=True)
        dgp_b = dgp.astype(BF16)
        dw2_ref[...] += _dot(a_ref[...].astype(BF16), dgp_b, TN)
        da_ref[...] = _dot(dgp_b, w2_ref[...], NT).astype(BF16)

    rev = lambda j: pl.BlockSpec((C, DV), lambda i: (n - 1 - i, j))
    full = lambda s: pl.BlockSpec(s, lambda i: (0,) * len(s))
    return _pcall(
        body, name="gla_bwd",
        out_shape=(jax.ShapeDtypeStruct((T, 3 * DV), BF16), jax.ShapeDtypeStruct((T, RP), BF16),
                   jax.ShapeDtypeStruct((RP, DK), F32), jax.ShapeDtypeStruct((1, DK), F32),
                   jax.ShapeDtypeStruct((1, DV), F32)),
        grid=(n,),
        in_specs=[rev(0), rev(1), rev(2), pl.BlockSpec((C, RP), lambda i: (n - 1 - i, 0)),
                  full((RP, DK)), full((1, DK)), full((1, DV)), rev(0),
                  pl.BlockSpec((None, H, HV, HK), lambda i: (n - 1 - i, 0, 0, 0)), rev(0)],
        out_specs=(pl.BlockSpec((C, 3 * DV), lambda i: (n - 1 - i, 0)),
                   pl.BlockSpec((C, RP), lambda i: (n - 1 - i, 0)),
                   full((RP, DK)), full((1, DK)), full((1, DV))),
        scratch_shapes=[pltpu.VMEM((H, HV, HK), F32)],
        semantics=("arbitrary",))(proj, proj, proj, a_pad, w2, b_a, g_gla, o_raw, states, do_gla)


def _pool_windows(p, g, T):
    t = lax.broadcasted_iota(jnp.int32, (T, 1), 0)
    s = p
    for lvl in range(POOL_GROUPS):
        sh = 1 << lvl
        nxt = s + jnp.where(t >= sh, pltpu.roll(s, sh, 0), 0.0)
        s = jnp.where(lvl <= g, nxt, s)
    win = jnp.left_shift(2, g)
    inv = 1.0 / jnp.minimum(t + 1, win).astype(F32)
    return s * inv - p, inv


def _pool_fwd(proj, w_pool, scale, *, T, PW, col_block):
    GW = PW // POOL_GROUPS
    per = PW // GW

    def body(p_ref, w_ref, s_ref, o_ref):
        g = pl.program_id(0)
        pooled, _ = _pool_windows(p_ref[...], g, T)
        mixed = _dot(pooled.astype(BF16), w_ref[...])
        o_ref[...] = (mixed * s_ref[...]).astype(BF16)

    return _pcall(body, name="pool_fwd", out_shape=jax.ShapeDtypeStruct((T, PW), BF16),
                  grid=(POOL_GROUPS,),
                  in_specs=[pl.BlockSpec((T, GW), lambda g: (0, col_block * per + g)),
                            pl.BlockSpec((None, GW, GW), lambda g: (g, 0, 0)),
                            pl.BlockSpec((1, GW), lambda g: (0, g))],
                  out_specs=pl.BlockSpec((T, GW), lambda g: (0, g)),
                  semantics=("parallel",))(proj, w_pool, scale)


def _pool_bwd(proj, w_pool, scale, do_pool, *, T, PW, col_block):
    GW = PW // POOL_GROUPS
    per = PW // GW

    def body(p_ref, w_ref, s_ref, do_ref, dp_ref, dw_ref, dsc_ref):
        g = pl.program_id(0)
        pooled, inv = _pool_windows(p_ref[...], g, T)
        pooled_b = pooled.astype(BF16)
        w = w_ref[...]
        mixed = _dot(pooled_b, w)
        d_out = do_ref[...]
        dsc_ref[...] = jnp.sum(d_out * mixed, axis=0, keepdims=True)
        dmixed = (d_out * s_ref[...]).astype(BF16)
        dw_ref[...] = _dot(pooled_b, dmixed, TN)
        dpooled = _dot(dmixed, w, NT)
        t = lax.broadcasted_iota(jnp.int32, (T, 1), 0)
        s = dpooled * inv
        for lvl in range(POOL_GROUPS):
            sh = 1 << lvl
            nxt = s + jnp.where(t < T - sh, pltpu.roll(s, T - sh, 0), 0.0)
            s = jnp.where(lvl <= g, nxt, s)
        dp_ref[...] = (s - dpooled).astype(BF16)

    return _pcall(body, name="pool_bwd",
                  out_shape=(jax.ShapeDtypeStruct((T, PW), BF16),
                             jax.ShapeDtypeStruct((POOL_GROUPS, GW, GW), F32),
                             jax.ShapeDtypeStruct((1, PW), F32)),
                  grid=(POOL_GROUPS,),
                  in_specs=[pl.BlockSpec((T, GW), lambda g: (0, col_block * per + g)),
                            pl.BlockSpec((None, GW, GW), lambda g: (g, 0, 0)),
                            pl.BlockSpec((1, GW), lambda g: (0, g)),
                            pl.BlockSpec((T, GW), lambda g: (0, g))],
                  out_specs=(pl.BlockSpec((T, GW), lambda g: (0, g)),
                             pl.BlockSpec((None, GW, GW), lambda g: (g, 0, 0)),
                             pl.BlockSpec((1, GW), lambda g: (0, g))),
                  semantics=("parallel",))(proj, w_pool, scale, do_pool)


def _merge_fwd(y_gla, y_pool, proj, *, T, D, col_block):
    tr = _tile(T, 128, 16)

    def body(yg_ref, yp_ref, g1_ref, g2_ref, o_ref):
        o_ref[...] = (_sigmoid(g1_ref[...]) * yg_ref[...]
                      + _sigmoid(g2_ref[...]) * yp_ref[...]).astype(BF16)

    row = pl.BlockSpec((tr, D), lambda i: (i, 0))
    return _pcall(body, name="merge_fwd", out_shape=jax.ShapeDtypeStruct((T, D), BF16),
                  grid=(T // tr,),
                  in_specs=[row, row, pl.BlockSpec((tr, D), lambda i: (i, col_block)),
                            pl.BlockSpec((tr, D), lambda i: (i, col_block + 1))],
                  out_specs=row, semantics=("parallel",))(y_gla, y_pool, proj, proj)


def _merge_bwd(dmerged, y_gla, y_pool, proj, *, T, D, col_block):
    tr = _tile(T, 128, 16)

    def body(dm_ref, yg_ref, yp_ref, g1_ref, g2_ref, dyg_ref, dyp_ref, dg_ref):
        dm = dm_ref[...]
        s1 = _sigmoid(g1_ref[...])
        s2 = _sigmoid(g2_ref[...])
        dyg_ref[...] = (dm * s1).astype(BF16)
        dyp_ref[...] = (dm * s2).astype(BF16)
        dg_ref[:, :D] = (dm * yg_ref[...] * s1 * (1.0 - s1)).astype(BF16)
        dg_ref[:, D:] = (dm * yp_ref[...] * s2 * (1.0 - s2)).astype(BF16)

    row = pl.BlockSpec((tr, D), lambda i: (i, 0))
    return _pcall(body, name="merge_bwd",
                  out_shape=(jax.ShapeDtypeStruct((T, D), BF16), jax.ShapeDtypeStruct((T, D), BF16),
                             jax.ShapeDtypeStruct((T, 2 * D), BF16)),
                  grid=(T // tr,),
                  in_specs=[row, row, row, pl.BlockSpec((tr, D), lambda i: (i, col_block)),
                            pl.BlockSpec((tr, D), lambda i: (i, col_block + 1))],
                  out_specs=(row, row, pl.BlockSpec((tr, 2 * D), lambda i: (i, 0))),
                  semantics=("parallel",))(dmerged, y_gla, y_pool, proj, proj)


def _attn_fwd(q, kv, *, T, D, M):
    H = CROSS_HEADS
    HD = D // H
    tq = _tile(T, 512, 16)
    scale = HD ** -0.5

    def body(q_ref, kv_ref, o_ref):
        for h in range(H):
            hs = slice(h * HD, (h + 1) * HD)
            s = _dot(q_ref[:, hs], kv_ref[:, hs], NT) * scale
            e = jnp.exp(s - jnp.max(s, axis=-1, keepdims=True))
            p = e / jnp.sum(e, axis=-1, keepdims=True)
            o_ref[:, hs] = _dot(p.astype(BF16), kv_ref[:, D + h * HD:D + (h + 1) * HD]).astype(BF16)

    row = pl.BlockSpec((tq, D), lambda i: (i, 0))
    return _pcall(body, name="attn_fwd", out_shape=jax.ShapeDtypeStruct((T, D), BF16),
                  grid=(T // tq,), in_specs=[row, pl.BlockSpec((M, 2 * D), lambda i: (0, 0))],
                  out_specs=row, semantics=("parallel",))(q, kv)


def _attn_bwd(q, kv, do, *, T, D, M):
    H = CROSS_HEADS
    HD = D // H
    tq = _tile(T, 512, 16)
    scale = HD ** -0.5

    def body(q_ref, kv_ref, do_ref, dq_ref, dkv_ref):
        @pl.when(pl.program_id(0) == 0)
        def _():
            dkv_ref[...] = jnp.zeros_like(dkv_ref)

        for h in range(H):
            hs = slice(h * HD, (h + 1) * HD)
            vs = slice(D + h * HD, D + (h + 1) * HD)
            qh = q_ref[:, hs]
            kh = kv_ref[:, hs]
            s = _dot(qh, kh, NT) * scale
            e = jnp.exp(s - jnp.max(s, axis=-1, keepdims=True))
            p = e / jnp.sum(e, axis=-1, keepdims=True)
            p_b = p.astype(BF16)
            d_o = do_ref[:, hs]
            dkv_ref[:, vs] += _dot(p_b, d_o, TN)
            dp = _dot(d_o, kv_ref[:, vs], NT)
            ds = (p * (dp - jnp.sum(dp * p, axis=-1, keepdims=True)) * scale).astype(BF16)
            dq_ref[:, hs] = _dot(ds, kh).astype(BF16)
            dkv_ref[:, hs] += _dot(ds, qh, TN)

    row = pl.BlockSpec((tq, D), lambda i: (i, 0))
    full = pl.BlockSpec((M, 2 * D), lambda i: (0, 0))
    return _pcall(body, name="attn_bwd",
                  out_shape=(jax.ShapeDtypeStruct((T, D), BF16), jax.ShapeDtypeStruct((M, 2 * D), F32)),
                  grid=(T // tq,), in_specs=[row, full, row], out_specs=(row, full),
                  semantics=("arbitrary",))(q, kv, do)


def _shift_down(x, halo, s):
    out = pltpu.roll(x, s, 0)
    t8 = lax.broadcasted_iota(jnp.int32, (SUBLANES, 1), 0)
    head = out[:SUBLANES]
    for j in range(s):
        head = jnp.where(t8 == j, halo[SUBLANES - s + j:SUBLANES - s + j + 1, :], head)
    return head if x.shape[0] == SUBLANES else jnp.concatenate([head, out[SUBLANES:]], axis=0)


def _shift_up(x, halo, s):
    rows = x.shape[0]
    out = pltpu.roll(x, rows - s, 0)
    t8 = lax.broadcasted_iota(jnp.int32, (SUBLANES, 1), 0)
    tail = out[rows - SUBLANES:]
    for j in range(s):
        tail = jnp.where(t8 == SUBLANES - s + j, halo[j:j + 1, :], tail)
    return jnp.concatenate([out[:rows - SUBLANES], tail], axis=0)


def _conv_tiles(T):
    tt = _tile(T, 128, SUBLANES)
    return tt, tt // SUBLANES, T // SUBLANES


def _conv_fwd(u0, conv_w, conv_b, *, T, F):
    tt, hb, _ = _conv_tiles(T)
    cw = _tile(F, LANES)

    def body(u_ref, prev_ref, w_ref, b_ref, f_ref):
        i = pl.program_id(0)

        def conv(cs):
            x = u_ref[:, cs]
            halo = jnp.where(i > 0, prev_ref[:, cs], 0.0)
            return (w_ref[2:3, cs] * x + w_ref[1:2, cs] * _shift_down(x, halo, 1)
                    + w_ref[0:1, cs] * _shift_down(x, halo, 2) + b_ref[:, cs])

        for j in range(F // cw):
            gate = conv(slice(j * cw, (j + 1) * cw))
            val = conv(slice(F + j * cw, F + (j + 1) * cw))
            f_ref[:, j * cw:(j + 1) * cw] = (gate * _sigmoid(gate) * val).astype(BF16)

    return _pcall(body, name="conv_fwd", out_shape=jax.ShapeDtypeStruct((T, F), BF16),
                  grid=(T // tt,),
                  in_specs=[pl.BlockSpec((tt, 2 * F), lambda i: (i, 0)),
                            pl.BlockSpec((SUBLANES, 2 * F), lambda i: (jnp.maximum(i * hb - 1, 0), 0)),
                            pl.BlockSpec((CONV_W, 2 * F), lambda i: (0, 0)),
                            pl.BlockSpec((1, 2 * F), lambda i: (0, 0))],
                  out_specs=pl.BlockSpec((tt, F), lambda i: (i, 0)),
                  semantics=("parallel",))(u0, u0, conv_w, conv_b)


def _conv_bwd(u0, conv_w, conv_b, df, *, T, F):
    tt, hb, nb = _conv_tiles(T)
    nt = T // tt
    cw = _tile(F, LANES)

    def body(u_ref, prev_ref, next_ref, df_ref, dfn_ref, w_ref, b_ref, du0_ref, dw_ref, db_ref):
        i = pl.program_id(0)

        @pl.when(i == 0)
        def _():
            dw_ref[...] = jnp.zeros_like(dw_ref)
            db_ref[...] = jnp.zeros_like(db_ref)

        def conv(cs):
            x = u_ref[:, cs]
            halo = jnp.where(i > 0, prev_ref[:, cs], 0.0)
            x1 = _shift_down(x, halo, 1)
            x2 = _shift_down(x, halo, 2)
            u = w_ref[2:3, cs] * x + w_ref[1:2, cs] * x1 + w_ref[0:1, cs] * x2 + b_ref[:, cs]
            xn = next_ref[:, cs]
            tail = x[tt - SUBLANES:, :]
            un = (w_ref[2:3, cs] * xn + w_ref[1:2, cs] * _shift_down(xn, tail, 1)
                  + w_ref[0:1, cs] * _shift_down(xn, tail, 2) + b_ref[:, cs])
            return u, un, (x, x1, x2)

        def glu_grad(gate, val, dff):
            sg = _sigmoid(gate)
            return dff * val * (sg * (1.0 + gate * (1.0 - sg))), dff * (gate * sg)

        def finish(cs, du, dun, xs):
            du0 = (w_ref[2:3, cs] * du + w_ref[1:2, cs] * _shift_up(du, dun, 1)
                   + w_ref[0:1, cs] * _shift_up(du, dun, 2))
            du0_ref[:, cs] = du0.astype(BF16)
            db_ref[:, cs] += jnp.sum(du, axis=0, keepdims=True)
            dw_ref[2:3, cs] += jnp.sum(du * xs[0], axis=0, keepdims=True)
            dw_ref[1:2, cs] += jnp.sum(du * xs[1], axis=0, keepdims=True)
            dw_ref[0:1, cs] += jnp.sum(du * xs[2], axis=0, keepdims=True)

        for j in range(F // cw):
            fs = slice(j * cw, (j + 1) * cw)
            gs, vs = fs, slice(F + j * cw, F + (j + 1) * cw)
            ug, ung, xg = conv(gs)
            uv, unv, xv = conv(vs)
            dug, duv = glu_grad(ug, uv, df_ref[:, fs].astype(F32))
            dung, dunv = glu_grad(ung, unv, dfn_ref[0:SUBLANES, fs].astype(F32))
            dung = jnp.where(i < nt - 1, dung, 0.0)
            dunv = jnp.where(i < nt - 1, dunv, 0.0)
            finish(gs, dug, dung, xg)
            finish(vs, duv, dunv, xv)

    wide = lambda rows, fn: pl.BlockSpec((rows, 2 * F), fn)
    nxt = lambda i: (jnp.minimum((i + 1) * hb, nb - 1), 0)
    return _pcall(body, name="conv_bwd",
                  out_shape=(jax.ShapeDtypeStruct((T, 2 * F), BF16),
                             jax.ShapeDtypeStruct((CONV_W, 2 * F), F32),
                             jax.ShapeDtypeStruct((1, 2 * F), F32)),
                  grid=(nt,),
                  in_specs=[wide(tt, lambda i: (i, 0)),
                            wide(SUBLANES, lambda i: (jnp.maximum(i * hb - 1, 0), 0)),
                            wide(SUBLANES, nxt),
                            pl.BlockSpec((tt, F), lambda i: (i, 0)),
                            pl.BlockSpec((2 * SUBLANES, F),
                                         lambda i: (jnp.minimum((i + 1) * (hb // 2), nb // 2 - 1), 0)),
                            wide(CONV_W, lambda i: (0, 0)), wide(1, lambda i: (0, 0))],
                  out_specs=(wide(tt, lambda i: (i, 0)), wide(CONV_W, lambda i: (0, 0)),
                             wide(1, lambda i: (0, 0))),
                  semantics=("arbitrary",))(u0, u0, u0, df, df, conv_w, conv_b)


def _adamw(w, g, m, v, *, name):
    R, C = w.shape
    tr = _tile(R, max(SUBLANES, (1 << 19) // max(C, 1) // SUBLANES * SUBLANES), SUBLANES)
    c1 = 1.0 / (1.0 - ADAM_B1 ** ADAM_STEP)
    c2 = 1.0 / (1.0 - ADAM_B2 ** ADAM_STEP)

    def body(w_ref, g_ref, m_ref, v_ref, d_ref, mo_ref, vo_ref):
        gv = g_ref[...]
        mn = ADAM_B1 * m_ref[...] + (1.0 - ADAM_B1) * gv
        vn = ADAM_B2 * v_ref[...] + (1.0 - ADAM_B2) * (gv * gv)
        d_ref[...] = -ADAM_LR * ((mn * c1) / (jnp.sqrt(vn * c2) + ADAM_EPS) + ADAM_WD * w_ref[...])
        mo_ref[...] = mn
        vo_ref[...] = vn

    blk = pl.BlockSpec((tr, C), lambda i: (i, 0))
    shp = jax.ShapeDtypeStruct((R, C), F32)
    return _pcall(body, name=name, out_shape=(shp, shp, shp), grid=(R // tr,),
                  in_specs=[blk] * 4, out_specs=(blk,) * 3, semantics=("parallel",))(w, g, m, v)


def _blk(h, C, elems=1 << 19, align=16):
    th = _tile(h, max(align, elems // C // align * align), align)
    if th < h or h * C <= 2 * elems:
        return th, C
    return h, _tile(C, max(LANES, elems // h // LANES * LANES))


def _adamw_halves(w, m, v, g_mine, g_other, c_idx, *, name):
    _, h, C = w.shape
    th, tc = _blk(h, C, align=SUBLANES)
    c1 = 1.0 / (1.0 - ADAM_B1 ** ADAM_STEP)
    c2 = 1.0 / (1.0 - ADAM_B2 ** ADAM_STEP)

    def body(c_ref, w_ref, m_ref, v_ref, gm_ref, go_ref, g_ref, d_ref, mo_ref, vo_ref):
        gv = jnp.where(pl.program_id(0) == c_ref[0], gm_ref[...], go_ref[...])
        mn = ADAM_B1 * m_ref[...] + (1.0 - ADAM_B1) * gv
        vn = ADAM_B2 * v_ref[...] + (1.0 - ADAM_B2) * (gv * gv)
        d_ref[...] = -ADAM_LR * ((mn * c1) / (jnp.sqrt(vn * c2) + ADAM_EPS) + ADAM_WD * w_ref[...])
        g_ref[...] = gv
        mo_ref[...] = mn
        vo_ref[...] = vn

    blk = pl.BlockSpec((None, th, tc), lambda s, i, j, c: (s, i, j))

    def pick(mine):
        def index(s, i, j, c):
            use = (s == c[0]) if mine else (s != c[0])
            return jnp.where(use, i, 0), jnp.where(use, j, 0)
        return pl.BlockSpec((th, tc), index)

    shp = jax.ShapeDtypeStruct((2, h, C), F32)
    return _pcall(body, name=name, out_shape=(shp,) * 4, grid=(2, h // th, C // tc), prefetch=1,
                  in_specs=[blk, blk, blk, pick(True), pick(False)], out_specs=(blk,) * 4,
                  semantics=("parallel", "parallel", "parallel"))(c_idx, w, m, v, g_mine, g_other)


def _mesh_pos():
    x, y, c = lax.axis_index("x"), lax.axis_index("y"), lax.axis_index("c")
    others = [(1 - x, y), (x, 1 - y), (1 - x, 1 - y)]
    return x, y, c, others


def _gather_copies(shards, lands, send_sems, recv_sems):
    x, y, c, others = _mesh_pos()
    me = 2 * x + y
    return [pltpu.make_async_remote_copy(
        src_ref=shards[a].at[c], dst_ref=lands[a].at[me, c],
        send_sem=send_sems.at[3 * a + j], recv_sem=recv_sems.at[3 * a + j],
        device_id=(*chip, c), device_id_type=MESH)
        for a in range(len(shards)) for j, chip in enumerate(others)]


def _pass_copies(shards, zones, send_sems, recv_sems):
    x, y, c, others = _mesh_pos()
    me = 2 * x + y
    copies = []
    for a in range(len(shards)):
        srcs = [zones[a].at[2 * chip[0] + chip[1], c] for chip in others] + [shards[a]]
        dsts = [zones[a].at[2 * chip[0] + chip[1], c] for chip in others] + [zones[a].at[me]]
        copies += [pltpu.make_async_remote_copy(
            src_ref=s, dst_ref=d, send_sem=send_sems.at[4 * a + k], recv_sem=recv_sems.at[4 * a + k],
            device_id=(x, y, 1 - c), device_id_type=MESH) for k, (s, d) in enumerate(zip(srcs, dsts))]
    return copies


def _exchange_copies(grads, recvs, send_sems, recv_sems):
    x, y, c, _ = _mesh_pos()
    return [pltpu.make_async_remote_copy(
        src_ref=grads[a].at[:, 1 - c], dst_ref=recvs[a], send_sem=send_sems.at[a],
        recv_sem=recv_sems.at[a], device_id=(x, y, 1 - c), device_id_type=MESH) for a in range(len(grads))]


def _split_start(copies, per, srcs, zones, after, *, name):
    n = len(srcs)
    HBM = pl.BlockSpec(memory_space=pltpu.HBM)
    SEM = pl.BlockSpec(memory_space=pltpu.SEMAPHORE)

    def body(*refs):
        send_sems, recv_sems = refs[2 * n + 1], refs[2 * n + 2]
        for cp in copies(refs[:n], refs[n:2 * n], send_sems, recv_sems):
            cp.start()
        refs[-1][...] = jnp.zeros_like(refs[-1])

    hbm = lambda a: pltpu.HBM(a.shape, a.dtype)
    res = _pcall(body, name=name,
                 out_shape=(pltpu.SemaphoreType.DMA((per * n,)), pltpu.SemaphoreType.DMA((per * n,)),
                            *[hbm(a) for a in srcs], *[hbm(a) for a in zones],
                            jax.ShapeDtypeStruct((SUBLANES, LANES), F32)),
                 in_specs=[*[HBM] * (2 * n), pl.BlockSpec(memory_space=pl.ANY)],
                 out_specs=(SEM, SEM, *[HBM] * (2 * n), pl.BlockSpec(memory_space=pltpu.VMEM)),
                 aliases={i: 2 + i for i in range(2 * n)}, split_copy=True)(
        *[pltpu.with_memory_space_constraint(a, pltpu.HBM) for a in [*srcs, *zones]], after)
    return res[0], res[1], list(res[2:2 + n]), list(res[2 + n:2 + 2 * n]), res[-1]


def _split_wait(copies, send_sems, recv_sems, srcs, zones, after, *, name):
    n = len(srcs)
    HBM = pl.BlockSpec(memory_space=pltpu.HBM)
    SEM = pl.BlockSpec(memory_space=pltpu.SEMAPHORE)

    def body(*refs):
        for cp in copies(refs[:n], refs[n:2 * n], refs[2 * n], refs[2 * n + 1]):
            cp.wait_send()
            cp.wait_recv()

    hbm = lambda a: pltpu.HBM(a.shape, a.dtype)
    res = _pcall(body, name=name, out_shape=(*[hbm(a) for a in srcs], *[hbm(a) for a in zones]),
                 in_specs=[*[HBM] * (2 * n), SEM, SEM, pl.BlockSpec(memory_space=pl.ANY)],
                 out_specs=tuple([HBM] * (2 * n)), aliases={i: i for i in range(2 * n)},
                 split_copy=True)(*srcs, *zones, send_sems, recv_sems, after)
    return list(res[:n]), list(res[n:])


def _add_halves(grad, recv, c_idx, *, name):
    S, _, h, C = grad.shape
    th, tc = _blk(h, C)

    def body(c_ref, g_ref, r_ref, o_ref):
        o_ref[...] = (g_ref[...].astype(F32) + r_ref[...].astype(F32)).astype(o_ref.dtype)

    return _pcall(body, name=name, out_shape=jax.ShapeDtypeStruct((S, h, C), grad.dtype),
                  grid=(S, h // th, C // tc), prefetch=1,
                  in_specs=[pl.BlockSpec((None, None, th, tc), lambda s, i, j, c: (s, c[0], i, j)),
                            pl.BlockSpec((None, th, tc), lambda s, i, j, c: (s, i, j))],
                  out_specs=pl.BlockSpec((None, th, tc), lambda s, i, j, c: (s, i, j)),
                  semantics=("parallel", "parallel", "parallel"))(c_idx, grad, recv)


def _scatter_copies(srcs, lands, send_sems, recv_sems):
    x, y, c, others = _mesh_pos()
    return [pltpu.make_async_remote_copy(
        src_ref=srcs[a].at[2 * chip[0] + chip[1]], dst_ref=lands[a].at[j],
        send_sem=send_sems.at[3 * a + j], recv_sem=recv_sems.at[3 * a + j],
        device_id=(*chip, c), device_id_type=MESH)
        for a in range(len(srcs)) for j, chip in enumerate(others)]


def _add_chips(sums, recv, chip_idx, *, name):
    _, h, C = sums.shape
    th, tc = _blk(h, C)

    def body(k_ref, s_ref, r_ref, o_ref):
        acc = s_ref[...].astype(F32) + r_ref[0].astype(F32)
        acc = acc + r_ref[1].astype(F32)
        o_ref[...] = acc + r_ref[2].astype(F32)

    return _pcall(body, name=name, out_shape=jax.ShapeDtypeStruct((h, C), F32),
                  grid=(h // th, C // tc), prefetch=1,
                  in_specs=[pl.BlockSpec((None, th, tc), lambda i, j, k: (k[0], i, j)),
                            pl.BlockSpec((3, th, tc), lambda i, j, k: (0, i, j))],
                  out_specs=pl.BlockSpec((th, tc), lambda i, j, k: (i, j)),
                  semantics=("parallel", "parallel"))(chip_idx, sums, recv)


def _swap_copies(halves, others, send_sems, recv_sems):
    x, y, c, _ = _mesh_pos()
    return [pltpu.make_async_remote_copy(
        src_ref=halves[a], dst_ref=others[a], send_sem=send_sems.at[a], recv_sem=recv_sems.at[a],
        device_id=(x, y, 1 - c), device_id_type=MESH) for a in range(len(halves))]


def _all_reduce_small(buf):
    R, L = buf.shape
    NDEV = 8

    def body(x_ref, sum_ref, all_ref, send_sems, recv_sems, local_sem):
        x, y, c, others = _mesh_pos()
        me, sibling = (x, y, c), (x, y, 1 - c)

        def slot(px, py, pc):
            return all_ref.at[4 * px + 2 * py + pc]

        def copy(k, block, to, src=None):
            return pltpu.make_async_remote_copy(
                src_ref=slot(*block) if src is None else src, dst_ref=slot(*block),
                send_sem=send_sems.at[k], recv_sem=recv_sems.at[k], device_id=to, device_id_type=MESH)

        mine = pltpu.make_async_copy(x_ref, slot(*me), local_sem)
        mine.start()
        first = [copy(0, me, sibling, src=x_ref)]
        first += [copy(1 + j, me, (*chip, c), src=x_ref) for j, chip in enumerate(others)]
        for cp in first:
            cp.start()
        passed = [copy(4 + j, (*chip, c), sibling) for j, chip in enumerate(others)]
        for j, chip in enumerate(others):
            copy(1 + j, (*chip, c), me).wait_recv()
            passed[j].start()
        copy(0, sibling, me).wait_recv()
        for j, chip in enumerate(others):
            copy(4 + j, (*chip, 1 - c), me).wait_recv()
        for cp in first + passed:
            cp.wait_send()
        mine.wait()
        acc = all_ref[0]
        for d in range(1, NDEV):
            acc = acc + all_ref[d]
        sum_ref[...] = acc

    VM = pl.BlockSpec(memory_space=pltpu.VMEM)
    return _pcall(body, name="all_reduce_small",
                  out_shape=(jax.ShapeDtypeStruct((R, L), F32), jax.ShapeDtypeStruct((NDEV, R, L), F32)),
                  in_specs=[VM], out_specs=(VM, VM),
                  scratch_shapes=[pltpu.SemaphoreType.DMA((7,)), pltpu.SemaphoreType.DMA((7,)),
                                  pltpu.SemaphoreType.DMA])(buf)[0]


def _pack(arrs, rows_multiple=16):
    flat = [a.reshape(-1).astype(F32) for a in arrs]
    sizes = [f.shape[0] for f in flat]
    total = sum(sizes)
    per = LANES * rows_multiple
    padded = -(-total // per) * per
    flat.append(jnp.zeros((padded - total,), F32))
    offs = [0]
    for s in sizes:
        offs.append(offs[-1] + s)
    return jnp.concatenate(flat).reshape(padded // LANES, LANES), offs


def _unpack(buf, offs, shapes):
    flat = buf.reshape(-1)
    return [flat[offs[i]:offs[i + 1]].reshape(s) for i, s in enumerate(shapes)]


def kernel(x, mem, g_mix, w_in, w_a2, b_a, g_gla, w_pool, pool_scale, w_branch, w_out, g_cross, g_mem, w_cq, w_ckv, w_co, g_ffn, w_up, conv_w, conv_b, w_down, g_final, loss_target, m_g_mix, m_w_in, m_w_a2, m_b_a, m_g_gla, m_w_pool, m_pool_scale, m_w_branch, m_w_out, m_g_cross, m_g_mem, m_w_cq, m_w_ckv, m_w_co, m_g_ffn, m_w_up, m_conv_w, m_conv_b, m_w_down, m_g_final, v_g_mix, v_w_in, v_w_a2, v_b_a, v_g_gla, v_w_pool, v_pool_scale, v_w_branch, v_w_out, v_g_cross, v_g_mem, v_w_cq, v_w_ckv, v_w_co, v_g_ffn, v_w_up, v_conv_w, v_conv_b, v_w_down, v_g_final):
    weights = dict(g_mix=g_mix, w_in=w_in, w_a2=w_a2, b_a=b_a, g_gla=g_gla, w_pool=w_pool,
                   pool_scale=pool_scale, w_branch=w_branch, w_out=w_out, g_cross=g_cross, g_mem=g_mem,
                   w_cq=w_cq, w_ckv=w_ckv, w_co=w_co, g_ffn=g_ffn, w_up=w_up, conv_w=conv_w,
                   conv_b=conv_b, w_down=w_down, g_final=g_final)
    mom_m = dict(g_mix=m_g_mix, w_in=m_w_in, w_a2=m_w_a2, b_a=m_b_a, g_gla=m_g_gla, w_pool=m_w_pool,
                 pool_scale=m_pool_scale, w_branch=m_w_branch, w_out=m_w_out, g_cross=m_g_cross,
                 g_mem=m_g_mem, w_cq=m_w_cq, w_ckv=m_w_ckv, w_co=m_w_co, g_ffn=m_g_ffn, w_up=m_w_up,
                 conv_w=m_conv_w, conv_b=m_conv_b, w_down=m_w_down, g_final=m_g_final)
    mom_v = dict(g_mix=v_g_mix, w_in=v_w_in, w_a2=v_w_a2, b_a=v_b_a, g_gla=v_g_gla, w_pool=v_w_pool,
                 pool_scale=v_pool_scale, w_branch=v_w_branch, w_out=v_w_out, g_cross=v_g_cross,
                 g_mem=v_g_mem, w_cq=v_w_cq, w_ckv=v_w_ckv, w_co=v_w_co, g_ffn=v_g_ffn, w_up=v_w_up,
                 conv_w=v_conv_w, conv_b=v_conv_b, w_down=v_w_down, g_final=v_g_final)
    order = list(weights)
    big = ["w_in", "w_branch", "w_out", "w_cq", "w_ckv", "w_co", "w_up", "w_down"]
    small_sharded = ["w_a2", "w_pool", "conv_w"]
    small_repl = ["g_mix", "b_a", "g_gla", "pool_scale", "g_cross", "g_mem", "g_ffn", "conv_b", "g_final"]

    xs, ms, tgt = x[0], mem[0], loss_target[0]
    T, D = xs.shape
    M = ms.shape[0]
    DK, DV, PW = b_a.shape[1], g_gla.shape[1], pool_scale.shape[1]
    RANK = w_a2.shape[1]
    F2 = conv_b.shape[1]
    F = F2 // 2
    DIN = N_CHIPS * w_in.shape[2]
    OFF_A = 2 * DK + 2 * DV
    OFF_P = OFF_A + RANK
    RP = LANES
    GW = PW // POOL_GROUPS
    assert PW == DV and 4 * DV == 2 * D and OFF_P + PW + 2 * D == DIN

    cx, cy, cc = lax.axis_index("x"), lax.axis_index("y"), lax.axis_index("c")
    chip = 2 * cx + cy
    c_idx = jnp.reshape(cc, (1,)).astype(jnp.int32)
    chip_idx = jnp.reshape(chip, (1,)).astype(jnp.int32)

    def halves(a):
        return a.reshape(2, a.shape[0] // 2, a.shape[1])

    shard2d = {k: (weights[k][0].T if k == "w_in" else weights[k][0]) for k in big}
    small_pack, small_offs = _pack([weights[k][0] for k in small_sharded], rows_multiple=32)
    flying, passing = {}, {}
    tok = xs
    for group, keys in (("in", ["w_in"]), ("mix", ["w_branch", "w_out", "small"]),
                        ("cross", ["w_cq", "w_ckv", "w_co"]), ("up", ["w_up"]), ("down", ["w_down"])):
        srcs = [small_pack if k == "small" else shard2d[k].astype(BF16) for k in keys]
        if group != "in":
            srcs = [a + tok[0:1, 0:1].astype(a.dtype) for a in srcs]
        srcs = [halves(a) for a in srcs]
        zones = [lax.empty((N_CHIPS, *s.shape), s.dtype) for s in srcs]
        s_sems, r_sems, srcs, zones, tok = _split_start(_gather_copies, 3, srcs, zones, tok,
                                                        name=f"gather_start_{group}")
        flying[group] = (keys, s_sems, r_sems, srcs, zones)

    def landed(group, after):
        keys, s_sems, r_sems, srcs, zones = flying[group]
        srcs, zones = _split_wait(_gather_copies, s_sems, r_sems, srcs, zones, after,
                                  name=f"gather_wait_{group}")
        s_sems, r_sems, srcs, zones, token = _split_start(_pass_copies, 4, srcs, zones, after,
                                                          name=f"gather_pass_start_{group}")
        passing[group] = (keys, s_sems, r_sems, srcs, zones)
        return token

    def arrive(group, after):
        keys, s_sems, r_sems, srcs, zones = passing[group]
        _, full = _split_wait(_pass_copies, s_sems, r_sems, srcs, zones, after,
                              name=f"gather_pass_wait_{group}")
        return {k: f.reshape(N_CHIPS, f.shape[1] * f.shape[2], f.shape[3]) for k, f in zip(keys, full)}

    def rows(g):
        return g.reshape(-1, g.shape[2])

    h1, r1 = _rms_fwd(xs, g_mix + tok[0:1, 0:1], name="norm_mix")
    landed("in", h1)
    gw = arrive("in", h1)
    W_in = rows(gw["w_in"])
    W_main = jnp.concatenate([W_in[:OFF_A], W_in[OFF_P:]], axis=0)
    W_a = jnp.pad(W_in[OFF_A:OFF_P], ((0, RP - RANK), (0, 0)))
    tok = landed("mix", W_a)
    proj = _mm(h1, W_main, "nt", name="proj_main", out_dtype=F32, after=tok)
    gw = arrive("mix", proj)
    W_branch, W_out, small_all = rows(gw["w_branch"]), rows(gw["w_out"]), gw["small"]
    sm = [_unpack(small_all[j], small_offs, [weights[k].shape[1:] for k in small_sharded]) for j in range(N_CHIPS)]
    W_a2 = jnp.concatenate([sm[j][0] for j in range(N_CHIPS)], axis=1)
    W_a2p = jnp.pad(W_a2, ((0, RP - RANK), (0, 0))).astype(BF16)
    W_pool = jnp.concatenate([sm[j][1] for j in range(N_CHIPS)], axis=1).astype(BF16)
    W_conv = jnp.concatenate([sm[j][2] for j in range(N_CHIPS)], axis=1)

    a_pad = _mm(h1, W_a, "nt", name="proj_gate_rank", out_dtype=F32)
    o_gla, o_raw, states = _gla_fwd(proj, a_pad, W_a2p, b_a, g_gla, T=T, DK=DK, DV=DV)
    o_pool = _pool_fwd(proj, W_pool, pool_scale, T=T, PW=PW, col_block=3)
    tok = landed("cross", o_pool)
    y_gla = _mm(o_gla, W_branch, "nn", name="branch_gla", out_dtype=F32, K=DV, after=tok)
    y_pool = _mm(o_pool, W_branch, "nn", name="branch_pool", out_dtype=F32, K=PW, b_off=(DV, 0))
    merged = _merge_fwd(y_gla, y_pool, proj, T=T, D=D, col_block=2)
    x1 = _mm(merged, W_out, "nn", name="mix_out", out_dtype=F32, add=xs)

    h2, r2 = _rms_fwd(x1, g_cross, name="norm_cross")
    mem_n, rm = _rms_fwd(ms, g_mem, name="norm_mem")
    gw = arrive("cross", h2)
    W_cq, W_ckv, W_co = rows(gw["w_cq"]), gw["w_ckv"], rows(gw["w_co"])
    qc = _mm(h2, W_cq, "nn", name="cross_q", out_dtype=BF16)
    kv = _mm(mem_n, W_ckv, "nn", name="cross_kv", out_dtype=BF16, b_blocked=True)
    o_att = _attn_fwd(qc, kv, T=T, D=D, M=M)
    x2 = _mm(o_att, W_co, "nn", name="cross_out", out_dtype=F32, add=x1)

    tok = landed("up", x2)
    h3, r3 = _rms_fwd(x2, g_ffn + tok[0:1, 0:1], name="norm_ffn")
    W_up = arrive("up", h3)["w_up"]
    u0 = _mm(h3, W_up, "nn", name="ffn_up", out_dtype=F32, b_blocked=True)
    tok = landed("down", u0)
    f_act = _conv_fwd(u0, W_conv, conv_b + tok[0:1, 0:1], T=T, F=F)
    W_down = rows(arrive("down", f_act)["w_down"])
    x3 =_mm(f_act, W_down, "nn", name="ffn_down", out_dtype=F32, add=x2)

    loss_part, dx3, dx3_b, dg_final = _loss_head(x3, g_final.reshape(1, D), tgt)

    def col_shards(g):
        nb, K, Nb = g.shape
        return g.reshape(nb, 2, K // 2, Nb)

    def row_shards(g):
        R, N = g.shape
        return g.reshape(N_CHIPS, 2, R // N_CHIPS // 2, N)

    exchanging, in_flight = {}, []

    def exchange_start(group, keys, partials, after):
        recvs = [lax.empty((p.shape[0], *p.shape[2:]), p.dtype) for p in partials]
        s_sems, r_sems, partials, recvs, token = _split_start(
            _exchange_copies, 1, partials, recvs, after, name=f"grad_exchange_start_{group}")
        exchanging[group] = (keys, s_sems, r_sems, partials, recvs)
        return token

    def scatter_start(group, after):
        keys, s_sems, r_sems, partials, recvs = exchanging[group]
        partials, recvs = _split_wait(_exchange_copies, s_sems, r_sems, partials, recvs, after,
                                      name=f"grad_exchange_wait_{group}")
        chip_sums = [_add_halves(p, r, c_idx, name=f"grad_add_halves_{k}")
                     for k, p, r in zip(keys, partials, recvs)]
        lands = [lax.empty((3, *s.shape[1:]), s.dtype) for s in chip_sums]
        s_sems, r_sems, sums, lands, token = _split_start(
            _scatter_copies, 3, chip_sums, lands, after, name=f"grad_scatter_start_{group}")
        in_flight.append((group, keys, s_sems, r_sems, sums, lands))
        return token

    df = _mm(dx3_b, W_down, "nt", name="d_ffn_act", out_dtype=BF16)
    dW_down = _mm(f_act, dx3_b, "tn", name="dw_down", out_dtype=BF16)
    du0, dconv_w, dconv_b = _conv_bwd(u0, W_conv, conv_b, df, T=T, F=F)
    dh3 = _mm(du0, W_up, "nt", name="d_ffn_in", out_dtype=F32, b_blocked=True)
    dW_up = _mm(h3, du0, "tn", name="dw_up", out_dtype=BF16, out_blocks=N_CHIPS)
    tok = exchange_start("ffn", ["w_down", "w_up"], [row_shards(dW_down), col_shards(dW_up)], dh3)
    dx2, dx2_b, dg_ffn = _rms_bwd(dh3, x2, r3 + tok[0:1, 0:1], g_ffn, dx3, name="norm_ffn_bwd")

    do_att = _mm(dx2_b, W_co, "nt", name="d_cross_o", out_dtype=BF16)
    dW_co = _mm(o_att, dx2_b, "tn", name="dw_co", out_dtype=BF16)
    tok = scatter_start("ffn", dW_co)
    dq, dkv = _attn_bwd(qc, kv, do_att, T=T, D=D, M=M)
    dkv_b = dkv.astype(BF16)
    dW_cq = _mm(h2, dq, "tn", name="dw_cq", out_dtype=BF16, after=tok)
    dh2 = _mm(dq, W_cq, "nt", name="d_cross_in", out_dtype=F32)
    dW_ckv = _mm(mem_n, dkv_b, "tn", name="dw_ckv", out_dtype=BF16, out_blocks=N_CHIPS)
    dmem_n = _mm(dkv_b, W_ckv, "nt", name="d_mem", out_dtype=F32, b_blocked=True)
    tok = exchange_start("cross", ["w_co", "w_cq", "w_ckv"],
                         [row_shards(dW_co), row_shards(dW_cq), col_shards(dW_ckv)], dmem_n)
    _, _, dg_mem = _rms_bwd(dmem_n, ms, rm, g_mem, None, name="norm_mem_bwd")
    dx1, dx1_b, dg_cross = _rms_bwd(dh2, x1, r2 + tok[0:1, 0:1], g_cross, dx2, name="norm_cross_bwd")

    dmerged = _mm(dx1_b, W_out, "nt", name="d_merged", out_dtype=F32)
    dW_out = _mm(merged, dx1_b, "tn", name="dw_out", out_dtype=BF16)
    tok = scatter_start("cross", dW_out)
    dy_gla, dy_pool, dgates = _merge_bwd(dmerged, y_gla, y_pool, proj, T=T, D=D, col_block=2)
    dW_br_gla = _mm(o_gla, dy_gla, "tn", name="dw_branch_gla", out_dtype=BF16, after=tok)
    dW_br_pool = _mm(o_pool, dy_pool, "tn", name="dw_branch_pool", out_dtype=BF16)
    do_gla = _mm(dy_gla, W_branch, "nt", name="d_o_gla", out_dtype=F32, N=DV)
    do_pool = _mm(dy_pool, W_branch, "nt", name="d_o_pool", out_dtype=F32, N=PW, b_off=(DV, 0))
    dp, dw_pool, dpool_scale = _pool_bwd(proj, W_pool, pool_scale, do_pool, T=T, PW=PW, col_block=3)
    dW_pool = jnp.transpose(dw_pool.reshape(POOL_GROUPS, N_CHIPS, GW // N_CHIPS, GW), (1, 0, 2, 3))
    tok = exchange_start("mix", ["w_out", "w_branch", "w_pool"],
                         [row_shards(dW_out), row_shards(jnp.concatenate([dW_br_gla, dW_br_pool], axis=0)),
                          row_shards(dW_pool.reshape(N_CHIPS * POOL_GROUPS * (GW // N_CHIPS), GW).astype(BF16))],
                         dp)
    dqkvr, da_pad, dw2, db_a, dg_gla = _gla_bwd(proj, a_pad, W_a2p, b_a + tok[0:1, 0:1], g_gla, o_raw, states,
                                               do_gla, T=T, DK=DK, DV=DV)
    tok = scatter_start("mix", dqkvr)
    dproj = jnp.concatenate([dqkvr, dp, dgates], axis=1)
    dW_main = _mm(dproj, h1, "tn", name="dw_in_main", out_dtype=BF16, after=tok)
    dW_a = _mm(da_pad, h1, "tn", name="dw_in_rank", out_dtype=BF16)
    dW_in = jnp.concatenate([dW_main[:OFF_A], dW_a[:RANK], dW_main[OFF_A:]], axis=0)
    tok = exchange_start("in", ["w_in"], [row_shards(dW_in)], dW_a)
    dh1 = _mm(dproj, W_main, "nn", name="d_mix_in_main", out_dtype=F32, after=tok)
    dh1 = _mm(da_pad, W_a, "nn", name="d_mix_in_rank", out_dtype=F32, add=dh1)
    dx0, _, dg_mix = _rms_bwd(dh1, xs, r1, g_mix, dx1, name="norm_mix_bwd")

    grads = {}

    small_grads = [loss_part, dg_mix, db_a, dg_gla, dpool_scale, dg_cross, dg_mem, dg_ffn, dconv_b, dg_final,
                   dw2[:RANK], dconv_w]
    small_buf, offs = _pack(small_grads)
    small_sum = _all_reduce_small(small_buf)
    red = _unpack(small_sum, offs, [g.shape for g in small_grads])
    loss = red[0][0, 0]
    for k, g in zip(small_repl, red[1:10]):
        grads[k] = g.reshape(weights[k].shape)
    nb = DK // N_CHIPS
    grads["w_a2"] = lax.dynamic_slice_in_dim(red[10], chip * nb, nb, axis=1)[None]
    nb = F2 // N_CHIPS
    grads["conv_w"] = lax.dynamic_slice_in_dim(red[11], chip * nb, nb, axis=1)[None]

    delta, new_m, new_v = {}, {}, {}

    def shard_rows(k, a):
        a = a[0]
        return a.T if k == "w_in" else a.reshape(-1, a.shape[-1])

    def whole(k, a):
        a = a.reshape(-1, a.shape[2])
        return (a.T if k == "w_in" else a).reshape(weights[k].shape)

    scatter_start("in", small_sum)
    after = in_flight[-1][4][0]

    def finish(swapping):
        keys, s_sems, r_sems, mine, others, after = swapping
        mine, others = _split_wait(_swap_copies, s_sems, r_sems, mine, others, after,
                                   name=f"grad_swap_wait_{keys[0]}")
        for k, g_mine, g_other in zip(keys, mine, others):
            wmv = [halves(shard_rows(k, src[k])) for src in (weights, mom_m, mom_v)]
            res = _adamw_halves(*wmv, g_mine, g_other, c_idx, name=f"adamw_{k}")
            grads[k], delta[k], new_m[k], new_v[k] = (whole(k, a) for a in res)
        return res[1]

    swapping = None
    for group, keys, s_sems, r_sems, sums, lands in in_flight:
        sums, from_chips = _split_wait(_scatter_copies, s_sems, r_sems, sums, lands, after,
                                       name=f"grad_scatter_wait_{group}")
        half_sums = [_add_chips(s, r, chip_idx, name=f"grad_add_chips_{k}") for k, s, r in zip(keys, sums, from_chips)]
        others = [lax.empty(h.shape, h.dtype) for h in half_sums]
        s_sems, r_sems, half_sums, others, token = _split_start(
            _swap_copies, 1, half_sums, others, after, name=f"grad_swap_start_{group}")
        if swapping is not None:
            after = finish((*swapping, token))
        swapping = (keys, s_sems, r_sems, half_sums, others)
    finish((*swapping, after))
    small = small_repl + ["w_a2", "conv_w"]
    packs = [_pack([src[k] for k in small])[0] for src in (weights, grads, mom_m, mom_v)]
    _, offs = _pack([weights[k] for k in small])
    outs = _adamw(*packs, name="adamw_small")
    for res, o in zip((delta, new_m, new_v), outs):
        for k, a in zip(small, _unpack(o, offs, [weights[k].shape for k in small])):
            res[k] = a

    return (loss, dx0[None], *[grads[k] for k in order], *[delta[k] for k in order],
            *[new_m[k] for k in order], *[new_v[k] for k in order])
```

```python
import functools

import jax
import jax.numpy as jnp
from jax import lax
from jax.experimental import pallas as pl
from jax.experimental.pallas import tpu as pltpu

F32 = jnp.float32
BF16 = jnp.bfloat16
MESH = pl.DeviceIdType.MESH
HIGHEST = lax.Precision.HIGHEST

EPS = 1e-6
GLA_HEADS = 4
GLA_CHUNK = 64
GLA_GATE_NORM = 16.0
POOL_GROUPS = 4
CROSS_HEADS = 4
CONV_W = 3
N_CHIPS = 4
LANES = 128
SUBLANES = 8
VMEM_LIMIT = 56 << 20

ADAM_LR = 0.001
ADAM_B1 = 0.9
ADAM_B2 = 0.999
ADAM_EPS = 1e-08
ADAM_WD = 0.01
ADAM_STEP = 10

NN = (((1,), (0,)), ((), ()))
NT = (((1,), (1,)), ((), ()))
TN = (((0,), (0,)), ((), ()))


ONE_PASS = lax.Precision.HIGH


def _dot(a, b, dn=NN, precision=None):
    return lax.dot_general(a, b, dn, precision=precision, preferred_element_type=F32)


def _tile(n, pref, align=LANES):
    t = (min(pref, n) // align) * align
    while t >= align:
        if n % t == 0:
            return t
        t -= align
    return n


def _pcall(body, *, name, out_shape, grid=(), in_specs=None, out_specs=None, scratch_shapes=(),
           semantics=None, prefetch=0, aliases=None, split_copy=False):
    params = dict(vmem_limit_bytes=VMEM_LIMIT)
    if semantics is not None:
        params["dimension_semantics"] = semantics
    if split_copy:
        params["has_side_effects"] = pltpu.SideEffectType.DATAFLOW_SIDE_EFFECTING
    if prefetch:
        grid_spec = pltpu.PrefetchScalarGridSpec(
            num_scalar_prefetch=prefetch, grid=grid, in_specs=in_specs, out_specs=out_specs,
            scratch_shapes=scratch_shapes)
        return pl.pallas_call(body, name=name, out_shape=out_shape, grid_spec=grid_spec,
                              compiler_params=pltpu.CompilerParams(**params))
    kw = {}
    if aliases is not None:
        kw["input_output_aliases"] = aliases
    if in_specs is not None:
        kw["in_specs"] = in_specs
    if out_specs is not None:
        kw["out_specs"] = out_specs
    return pl.pallas_call(body, name=name, out_shape=out_shape, grid=grid,
                          scratch_shapes=scratch_shapes,
                          compiler_params=pltpu.CompilerParams(**params), **kw)


def _sigmoid(x):
    return 1.0 / (1.0 + jnp.exp(-x))


def _log_sigmoid(x):
    return jnp.minimum(x, 0.0) - jnp.log(1.0 + jnp.exp(-jnp.abs(x)))


def _mm(a, b, mode, *, name, out_dtype, M=None, N=None, K=None, a_off=(0, 0), b_off=(0, 0),
        add=None, b_blocked=False, out_blocks=0, after=None, tm=1536, tn=1536, tk=2048):
    if b_blocked:
        nb, R, Cb = b.shape
        b_rows, b_cols = R, nb * Cb
    else:
        b_rows, b_cols = b.shape
    if mode == "nn":
        M = M or a.shape[0]; K = K or a.shape[1]; N = N or b_cols
    elif mode == "nt":
        M = M or a.shape[0]; K = K or a.shape[1]; N = N or b_rows
    else:
        K = K or a.shape[0]; M = M or a.shape[1]; N = N or b_cols
    tm = _tile(M, tm, LANES if mode == "tn" else 16)
    tn = _tile(Cb if (b_blocked and mode != "nt") else (N // out_blocks if out_blocks else N), tn)
    tk = _tile(Cb if (b_blocked and mode == "nt") else K, tk)
    nk = K // tk
    dn = {"nn": NN, "nt": NT, "tn": TN}[mode]

    def off(o, t):
        assert o % t == 0, (name, o, t)
        return o // t

    if mode == "tn":
        ar, ac = off(a_off[0], tk), off(a_off[1], tm)
        a_spec = pl.BlockSpec((tk, tm), lambda i, j, k: (k + ar, i + ac))
    else:
        ar, ac = off(a_off[0], tm), off(a_off[1], tk)
        a_spec = pl.BlockSpec((tm, tk), lambda i, j, k: (i + ar, k + ac))
    if b_blocked and mode == "nt":
        per = Cb // tk
        b_spec = pl.BlockSpec((None, tn, tk), lambda i, j, k: (k // per, j, k % per))
    elif b_blocked:
        per = Cb // tn
        b_spec = pl.BlockSpec((None, tk, tn), lambda i, j, k: (j // per, k, j % per))
    elif mode == "nt":
        br, bc = off(b_off[0], tn), off(b_off[1], tk)
        b_spec = pl.BlockSpec((tn, tk), lambda i, j, k: (j + br, k + bc))
    else:
        br, bc = off(b_off[0], tk), off(b_off[1], tn)
        b_spec = pl.BlockSpec((tk, tn), lambda i, j, k: (k + br, j + bc))
    if out_blocks:
        per_o = N // out_blocks // tn
        o_spec = pl.BlockSpec((None, tm, tn), lambda i, j, k: (j // per_o, i, j % per_o))
        out_shape = jax.ShapeDtypeStruct((out_blocks, M, N // out_blocks), out_dtype)
    else:
        o_spec = pl.BlockSpec((tm, tn), lambda i, j, k: (i, j))
        out_shape = jax.ShapeDtypeStruct((M, N), out_dtype)
    in_specs = [a_spec, b_spec]
    args = [a, b]
    if add is not None:
        assert not out_blocks
        in_specs.append(o_spec)
        args.append(add)
    if after is not None:
        in_specs.append(pl.BlockSpec(memory_space=pl.ANY))
        args.append(after)
    n_in = len(args)

    def finish(r, refs):
        if add is not None:
            r = r + refs[2][...]
        o_ref = refs[n_in]
        o_ref[...] = r.astype(o_ref.dtype)

    def body_one(*refs):
        finish(_dot(refs[0][...].astype(BF16), refs[1][...].astype(BF16), dn), refs)

    def body_acc(*refs):
        acc_ref = refs[-1]
        k = pl.program_id(2)

        @pl.when(k == 0)
        def _():
            acc_ref[...] = jnp.zeros_like(acc_ref)

        acc_ref[...] += _dot(refs[0][...].astype(BF16), refs[1][...].astype(BF16), dn)

        @pl.when(k == nk - 1)
        def _():
            finish(acc_ref[...], refs)

    return _pcall(body_one if nk == 1 else body_acc, name=name, out_shape=out_shape,
                  grid=(M // tm, N // tn, nk), in_specs=in_specs, out_specs=o_spec,
                  scratch_shapes=[] if nk == 1 else [pltpu.VMEM((tm, tn), F32)],
                  semantics=("parallel", "parallel", "arbitrary"))(*args)


def _rms_fwd(x, g, *, name):
    T, D = x.shape
    tr = _tile(T, 128, 16)

    def body(x_ref, g_ref, h_ref, r_ref):
        xv = x_ref[...]
        r = lax.rsqrt(jnp.mean(xv * xv, axis=-1, keepdims=True) + EPS)
        h_ref[...] = (xv * r * g_ref[...]).astype(h_ref.dtype)
        r_ref[...] = r

    row = pl.BlockSpec((tr, D), lambda i: (i, 0))
    return _pcall(body, name=name,
                  out_shape=(jax.ShapeDtypeStruct((T, D), BF16), jax.ShapeDtypeStruct((T, 1), F32)),
                  grid=(T // tr,),
                  in_specs=[row, pl.BlockSpec((1, D), lambda i: (0, 0))],
                  out_specs=(row, pl.BlockSpec((tr, 1), lambda i: (i, 0))),
                  semantics=("parallel",))(x, g)


def _rms_bwd(dh, x, rstd, g, dres, *, name):
    T, D = x.shape
    tr = _tile(T, 128, 16)
    has_res = dres is not None

    def body(*refs):
        if has_res:
            dh_ref, x_ref, r_ref, g_ref, res_ref, dx_ref, dxb_ref, dg_ref = refs
        else:
            dh_ref, x_ref, r_ref, g_ref, dx_ref, dxb_ref, dg_ref = refs
        r = r_ref[...]
        xh = x_ref[...] * r
        dhv = dh_ref[...].astype(F32)
        dxh = dhv * g_ref[...]
        m = jnp.mean(dxh * xh, axis=-1, keepdims=True)
        dx = r * (dxh - xh * m)
        if has_res:
            dx = dx + res_ref[...]
        dx_ref[...] = dx
        dxb_ref[...] = dx.astype(BF16)

        @pl.when(pl.program_id(0) == 0)
        def _():
            dg_ref[...] = jnp.zeros_like(dg_ref)

        dg_ref[...] += jnp.sum(dhv * xh, axis=0, keepdims=True)

    row = pl.BlockSpec((tr, D), lambda i: (i, 0))
    vec = pl.BlockSpec((1, D), lambda i: (0, 0))
    in_specs = [row, row, pl.BlockSpec((tr, 1), lambda i: (i, 0)), vec]
    args = [dh, x, rstd, g]
    if has_res:
        in_specs.append(row)
        args.append(dres)
    return _pcall(body, name=name,
                  out_shape=(jax.ShapeDtypeStruct((T, D), F32), jax.ShapeDtypeStruct((T, D), BF16),
                             jax.ShapeDtypeStruct((1, D), F32)),
                  grid=(T // tr,), in_specs=in_specs, out_specs=(row, row, vec),
                  semantics=("arbitrary",))(*args)


def _loss_head(x3, g, tgt):
    T, D = x3.shape
    tr = _tile(T, 128, 16)

    def body(x_ref, g_ref, t_ref, loss_ref, dx_ref, dxb_ref, dg_ref):
        xv = x_ref[...]
        gv = g_ref[...]
        r = lax.rsqrt(jnp.mean(xv * xv, axis=-1, keepdims=True) + EPS)
        xh = xv * r
        err = xh * gv - t_ref[...]
        dy = err * (1.0 / D)
        dxh = dy * gv
        m = jnp.mean(dxh * xh, axis=-1, keepdims=True)
        dx = r * (dxh - xh * m)
        dx_ref[...] = dx
        dxb_ref[...] = dx.astype(BF16)

        @pl.when(pl.program_id(0) == 0)
        def _():
            dg_ref[...] = jnp.zeros_like(dg_ref)
            loss_ref[...] = jnp.zeros_like(loss_ref)

        dg_ref[...] += jnp.sum(dy * xh, axis=0, keepdims=True)
        part = 0.5 * jnp.sum(jnp.mean(err * err, axis=-1, keepdims=True), axis=0, keepdims=True)
        loss_ref[...] += jnp.broadcast_to(part, loss_ref.shape)

    row = pl.BlockSpec((tr, D), lambda i: (i, 0))
    vec = pl.BlockSpec((1, D), lambda i: (0, 0))
    return _pcall(body, name="loss_head",
                  out_shape=(jax.ShapeDtypeStruct((1, LANES), F32), jax.ShapeDtypeStruct((T, D), F32),
                             jax.ShapeDtypeStruct((T, D), BF16), jax.ShapeDtypeStruct((1, D), F32)),
                  grid=(T // tr,), in_specs=[row, vec, row],
                  out_specs=(pl.BlockSpec((1, LANES), lambda i: (0, 0)), row, row, vec),
                  semantics=("arbitrary",))(x3, g, tgt)


def _gla_chunk_terms(qk, a_ref, w2_ref, ba_ref, DK):
    C = qk.shape[0]
    gp = _dot(a_ref[...].astype(BF16), w2_ref[...]) + ba_ref[...]
    la = _log_sigmoid(gp) * (1.0 / GLA_GATE_NORM)
    row = lax.broadcasted_iota(jnp.int32, (C, C), 0)
    col = lax.broadcasted_iota(jnp.int32, (C, C), 1)
    causal = row >= col
    b = _dot(causal.astype(F32), la, precision=HIGHEST)
    return gp, b, causal


def _gla_fwd(proj, a_pad, w2, b_a, g_gla, *, T, DK, DV):
    assert 2 * DK == DV
    H = GLA_HEADS
    HK, HV = DK // H, DV // H
    C = GLA_CHUNK
    n = T // C
    RP = a_pad.shape[1]
    scale = HK ** -0.5

    def body(qk_ref, v_ref, r_ref, a_ref, w2_ref, ba_ref, gg_ref, og_ref, oraw_ref, st_ref, s_ref):
        @pl.when(pl.program_id(0) == 0)
        def _():
            s_ref[...] = jnp.zeros_like(s_ref)

        st_ref[...] = s_ref[...]
        qk = qk_ref[...]
        _, b, causal = _gla_chunk_terms(qk, a_ref, w2_ref, ba_ref, DK)
        for h in range(H):
            ks = slice(h * HK, (h + 1) * HK)
            vs = slice(h * HV, (h + 1) * HV)
            bh = b[:, ks]
            b_last = bh[C - 1:C, :]
            qt = qk[:, ks] * scale * jnp.exp(bh)
            kh = qk[:, DK + h * HK:DK + (h + 1) * HK]
            kt = kh * jnp.exp(-bh)
            khat = kh * jnp.exp(b_last - bh)
            a_mat = jnp.where(causal, _dot(qt, kt, NT, ONE_PASS), 0.0)
            vh = v_ref[:, vs]
            s_t = s_ref[h]
            o = _dot(a_mat, vh, NN, ONE_PASS) + _dot(qt, s_t, NT, ONE_PASS)
            s_ref[h] = s_t * jnp.exp(b_last) + _dot(vh, khat, TN, ONE_PASS)
            rs = lax.rsqrt(jnp.mean(o * o, axis=-1, keepdims=True) + EPS)
            rr = r_ref[:, vs]
            og = o * rs * gg_ref[:, vs] * (rr * _sigmoid(rr))
            oraw_ref[:, vs] = o
            og_ref[:, vs] = og.astype(BF16)

    blk = lambda j: pl.BlockSpec((C, DV), lambda i: (i, j))
    full = lambda s: pl.BlockSpec(s, lambda i: (0,) * len(s))
    return _pcall(
        body, name="gla_fwd",
        out_shape=(jax.ShapeDtypeStruct((T, DV), BF16), jax.ShapeDtypeStruct((T, DV), F32),
                   jax.ShapeDtypeStruct((n, H, HV, HK), F32)),
        grid=(n,),
        in_specs=[blk(0), blk(1), blk(2), pl.BlockSpec((C, RP), lambda i: (i, 0)),
                  full((RP, DK)), full((1, DK)), full((1, DV))],
        out_specs=(blk(0), blk(0), pl.BlockSpec((None, H, HV, HK), lambda i: (i, 0, 0, 0))),
        scratch_shapes=[pltpu.VMEM((H, HV, HK), F32)],
        semantics=("arbitrary",))(proj, proj, proj, a_pad, w2, b_a, g_gla)


def _gla_bwd(proj, a_pad, w2, b_a, g_gla, o_raw, states, do_gla, *, T, DK, DV):
    H = GLA_HEADS
    HK, HV = DK // H, DV // H
    C = GLA_CHUNK
    n = T // C
    RP = a_pad.shape[1]
    scale = HK ** -0.5

    def body(qk_ref, v_ref, r_ref, a_ref, w2_ref, ba_ref, gg_ref, oraw_ref, st_ref, dog_ref,
             dqkvr_ref, da_ref, dw2_ref, dba_ref, dgg_ref, ds_ref):
        @pl.when(pl.program_id(0) == 0)
        def _():
            ds_ref[...] = jnp.zeros_like(ds_ref)
            dw2_ref[...] = jnp.zeros_like(dw2_ref)
            dba_ref[...] = jnp.zeros_like(dba_ref)
            dgg_ref[...] = jnp.zeros_like(dgg_ref)

        qk = qk_ref[...]
        gp, b, causal = _gla_chunk_terms(qk, a_ref, w2_ref, ba_ref, DK)
        row = lax.broadcasted_iota(jnp.int32, (C, C), 0)
        col = lax.broadcasted_iota(jnp.int32, (C, C), 1)
        upper = (col >= row).astype(F32)
        dla_parts = []
        for h in range(H):
            ks = slice(h * HK, (h + 1) * HK)
            vs = slice(h * HV, (h + 1) * HV)
            bh = b[:, ks]
            b_last = bh[C - 1:C, :]
            eb = jnp.exp(bh)
            emb = jnp.exp(-bh)
            ehat = jnp.exp(b_last - bh)
            e_last = jnp.exp(b_last)
            qt = qk[:, ks] * scale * eb
            kh = qk[:, DK + h * HK:DK + (h + 1) * HK]
            kt = kh * emb
            khat = kh * ehat
            a_mat = jnp.where(causal, _dot(qt, kt, NT, ONE_PASS), 0.0)
            vh = v_ref[:, vs]
            o = oraw_ref[:, vs]
            rs = lax.rsqrt(jnp.mean(o * o, axis=-1, keepdims=True) + EPS)
            on = o * rs
            gg = gg_ref[:, vs]
            rr = r_ref[:, vs]
            sg = _sigmoid(rr)
            d_out = dog_ref[:, vs]
            dr = d_out * (on * gg) * (sg * (1.0 + rr * (1.0 - sg)))
            d_og = d_out * (rr * sg)
            dgg_ref[:, vs] += jnp.sum(d_og * on, axis=0, keepdims=True)
            d_on = d_og * gg
            d_o = rs * (d_on - on * jnp.mean(d_on * on, axis=-1, keepdims=True))
            s_t = st_ref[h]
            ds_t = ds_ref[h]
            d_a = jnp.where(causal, _dot(d_o, vh, NT, ONE_PASS), 0.0)
            dv = _dot(a_mat, d_o, TN, ONE_PASS) + _dot(khat, ds_t, NT, ONE_PASS)
            dqt = _dot(d_a, kt, NN, ONE_PASS) + _dot(d_o, s_t, NN, ONE_PASS)
            dkt = _dot(d_a, qt, TN, ONE_PASS)
            dkhat = _dot(vh, ds_t, NN, ONE_PASS)
            ds_ref[h] = ds_t * e_last + _dot(d_o, qt, TN, ONE_PASS)
            dq = dqt * eb * scale
            dk = dkt * emb + dkhat * ehat
            db = dqt * qt - dkt * kt - dkhat * khat
            d_last = (jnp.sum(dkhat * khat, axis=0, keepdims=True)
                      + e_last * jnp.sum(ds_t * s_t, axis=0, keepdims=True))
            dla_parts.append(_dot(upper, db, NN, HIGHEST) + d_last)
            dqkvr_ref[:, ks] = dq.astype(BF16)
            dqkvr_ref[:, DK + h * HK:DK + (h + 1) * HK] = dk.astype(BF16)
            dqkvr_ref[:, DV + h * HV:DV + (h + 1) * HV] = dv.astype(BF16)
            dqkvr_ref[:, 2 * DV + h * HV:2 * DV + (h + 1) * HV] = dr.astype(BF16)
        dla = jnp.concatenate(dla_parts, axis=1)
        dgp = dla * (1.0 / GLA_GATE_NORM) * _sigmoid(-gp)
        dba_ref[...] += jnp.sum(dgp, axis=0, keepdims=True)
        dgp_b = dgp.astype(BF16)
        dw2_ref[...] += _dot(a_ref[...].astype(BF16), dgp_b, TN)
        da_ref[...] = _dot(dgp_b, w2_ref[...], NT).astype(BF16)

    rev = lambda j: pl.BlockSpec((C, DV), lambda i: (n - 1 - i, j))
    full = lambda s: pl.BlockSpec(s, lambda i: (0,) * len(s))
    return _pcall(
        body, name="gla_bwd",
        out_shape=(jax.ShapeDtypeStruct((T, 3 * DV), BF16), jax.ShapeDtypeStruct((T, RP), BF16),
                   jax.ShapeDtypeStruct((RP, DK), F32), jax.ShapeDtypeStruct((1, DK), F32),
                   jax.ShapeDtypeStruct((1, DV), F32)),
        grid=(n,),
        in_specs=[rev(0), rev(1), rev(2), pl.BlockSpec((C, RP), lambda i: (n - 1 - i, 0)),
                  full((RP, DK)), full((1, DK)), full((1, DV)), rev(0),
                  pl.BlockSpec((None, H, HV, HK), lambda i: (n - 1 - i, 0, 0, 0)), rev(0)],
        out_specs=(pl.BlockSpec((C, 3 * DV), lambda i: (n - 1 - i, 0)),
                   pl.BlockSpec((C, RP), lambda i: (n - 1 - i, 0)),
                   full((RP, DK)), full((1, DK)), full((1, DV))),
        scratch_shapes=[pltpu.VMEM((H, HV, HK), F32)],
        semantics=("arbitrary",))(proj, proj, proj, a_pad, w2, b_a, g_gla, o_raw, states, do_gla)


def _pool_windows(p, g, T):
    t = lax.broadcasted_iota(jnp.int32, (T, 1), 0)
    s = p
    for lvl in range(POOL_GROUPS):
        sh = 1 << lvl
        nxt = s + jnp.where(t >= sh, pltpu.roll(s, sh, 0), 0.0)
        s = jnp.where(lvl <= g, nxt, s)
    win = jnp.left_shift(2, g)
    inv = 1.0 / jnp.minimum(t + 1, win).astype(F32)
    return s * inv - p, inv


def _pool_fwd(proj, w_pool, scale, *, T, PW, col_block):
    GW = PW // POOL_GROUPS
    per = PW // GW

    def body(p_ref, w_ref, s_ref, o_ref):
        g = pl.program_id(0)
        pooled, _ = _pool_windows(p_ref[...], g, T)
        mixed = _dot(pooled.astype(BF16), w_ref[...])
        o_ref[...] = (mixed * s_ref[...]).astype(BF16)

    return _pcall(body, name="pool_fwd", out_shape=jax.ShapeDtypeStruct((T, PW), BF16),
                  grid=(POOL_GROUPS,),
                  in_specs=[pl.BlockSpec((T, GW), lambda g: (0, col_block * per + g)),
                            pl.BlockSpec((None, GW, GW), lambda g: (g, 0, 0)),
                            pl.BlockSpec((1, GW), lambda g: (0, g))],
                  out_specs=pl.BlockSpec((T, GW), lambda g: (0, g)),
                  semantics=("parallel",))(proj, w_pool, scale)


def _pool_bwd(proj, w_pool, scale, do_pool, *, T, PW, col_block):
    GW = PW // POOL_GROUPS
    per = PW // GW

    def body(p_ref, w_ref, s_ref, do_ref, dp_ref, dw_ref, dsc_ref):
        g = pl.program_id(0)
        pooled, inv = _pool_windows(p_ref[...], g, T)
        pooled_b = pooled.astype(BF16)
        w = w_ref[...]
        mixed = _dot(pooled_b, w)
        d_out = do_ref[...]
        dsc_ref[...] = jnp.sum(d_out * mixed, axis=0, keepdims=True)
        dmixed = (d_out * s_ref[...]).astype(BF16)
        dw_ref[...] = _dot(pooled_b, dmixed, TN)
        dpooled = _dot(dmixed, w, NT)
        t = lax.broadcasted_iota(jnp.int32, (T, 1), 0)
        s = dpooled * inv
        for lvl in range(POOL_GROUPS):
            sh = 1 << lvl
            nxt = s + jnp.where(t < T - sh, pltpu.roll(s, T - sh, 0), 0.0)
            s = jnp.where(lvl <= g, nxt, s)
        dp_ref[...] = (s - dpooled).astype(BF16)

    return _pcall(body, name="pool_bwd",
                  out_shape=(jax.ShapeDtypeStruct((T, PW), BF16),
                             jax.ShapeDtypeStruct((POOL_GROUPS, GW, GW), F32),
                             jax.ShapeDtypeStruct((1, PW), F32)),
                  grid=(POOL_GROUPS,),
                  in_specs=[pl.BlockSpec((T, GW), lambda g: (0, col_block * per + g)),
                            pl.BlockSpec((None, GW, GW), lambda g: (g, 0, 0)),
                            pl.BlockSpec((1, GW), lambda g: (0, g)),
                            pl.BlockSpec((T, GW), lambda g: (0, g))],
                  out_specs=(pl.BlockSpec((T, GW), lambda g: (0, g)),
                             pl.BlockSpec((None, GW, GW), lambda g: (g, 0, 0)),
                             pl.BlockSpec((1, GW), lambda g: (0, g))),
                  semantics=("parallel",))(proj, w_pool, scale, do_pool)


def _merge_fwd(y_gla, y_pool, proj, *, T, D, col_block):
    tr = _tile(T, 128, 16)

    def body(yg_ref, yp_ref, g1_ref, g2_ref, o_ref):
        o_ref[...] = (_sigmoid(g1_ref[...]) * yg_ref[...]
                      + _sigmoid(g2_ref[...]) * yp_ref[...]).astype(BF16)

    row = pl.BlockSpec((tr, D), lambda i: (i, 0))
    return _pcall(body, name="merge_fwd", out_shape=jax.ShapeDtypeStruct((T, D), BF16),
                  grid=(T // tr,),
                  in_specs=[row, row, pl.BlockSpec((tr, D), lambda i: (i, col_block)),
                            pl.BlockSpec((tr, D), lambda i: (i, col_block + 1))],
                  out_specs=row, semantics=("parallel",))(y_gla, y_pool, proj, proj)


def _merge_bwd(dmerged, y_gla, y_pool, proj, *, T, D, col_block):
    tr = _tile(T, 128, 16)

    def body(dm_ref, yg_ref, yp_ref, g1_ref, g2_ref, dyg_ref, dyp_ref, dg_ref):
        dm = dm_ref[...]
        s1 = _sigmoid(g1_ref[...])
        s2 = _sigmoid(g2_ref[...])
        dyg_ref[...] = (dm * s1).astype(BF16)
        dyp_ref[...] = (dm * s2).astype(BF16)
        dg_ref[:, :D] = (dm * yg_ref[...] * s1 * (1.0 - s1)).astype(BF16)
        dg_ref[:, D:] = (dm * yp_ref[...] * s2 * (1.0 - s2)).astype(BF16)

    row = pl.BlockSpec((tr, D), lambda i: (i, 0))
    return _pcall(body, name="merge_bwd",
                  out_shape=(jax.ShapeDtypeStruct((T, D), BF16), jax.ShapeDtypeStruct((T, D), BF16),
                             jax.ShapeDtypeStruct((T, 2 * D), BF16)),
                  grid=(T // tr,),
                  in_specs=[row, row, row, pl.BlockSpec((tr, D), lambda i: (i, col_block)),
                            pl.BlockSpec((tr, D), lambda i: (i, col_block + 1))],
                  out_specs=(row, row, pl.BlockSpec((tr, 2 * D), lambda i: (i, 0))),
                  semantics=("parallel",))(dmerged, y_gla, y_pool, proj, proj)


def _attn_fwd(q, kv, *, T, D, M):
    H = CROSS_HEADS
    HD = D // H
    tq = _tile(T, 512, 16)
    scale = HD ** -0.5

    def body(q_ref, kv_ref, o_ref):
        for h in range(H):
            hs = slice(h * HD, (h + 1) * HD)
            s = _dot(q_ref[:, hs], kv_ref[:, hs], NT) * scale
            e = jnp.exp(s - jnp.max(s, axis=-1, keepdims=True))
            p = e / jnp.sum(e, axis=-1, keepdims=True)
            o_ref[:, hs] = _dot(p.astype(BF16), kv_ref[:, D + h * HD:D + (h + 1) * HD]).astype(BF16)

    row = pl.BlockSpec((tq, D), lambda i: (i, 0))
    return _pcall(body, name="attn_fwd", out_shape=jax.ShapeDtypeStruct((T, D), BF16),
                  grid=(T // tq,), in_specs=[row, pl.BlockSpec((M, 2 * D), lambda i: (0, 0))],
                  out_specs=row, semantics=("parallel",))(q, kv)


def _attn_bwd(q, kv, do, *, T, D, M):
    H = CROSS_HEADS
    HD = D // H
    tq = _tile(T, 512, 16)
    scale = HD ** -0.5

    def body(q_ref, kv_ref, do_ref, dq_ref, dkv_ref):
        @pl.when(pl.program_id(0) == 0)
        def _():
            dkv_ref[...] = jnp.zeros_like(dkv_ref)

        for h in range(H):
            hs = slice(h * HD, (h + 1) * HD)
            vs = slice(D + h * HD, D + (h + 1) * HD)
            qh = q_ref[:, hs]
            kh = kv_ref[:, hs]
            s = _dot(qh, kh, NT) * scale
            e = jnp.exp(s - jnp.max(s, axis=-1, keepdims=True))
            p = e / jnp.sum(e, axis=-1, keepdims=True)
            p_b = p.astype(BF16)
            d_o = do_ref[:, hs]
            dkv_ref[:, vs] += _dot(p_b, d_o, TN)
            dp = _dot(d_o, kv_ref[:, vs], NT)
            ds = (p * (dp - jnp.sum(dp * p, axis=-1, keepdims=True)) * scale).astype(BF16)
            dq_ref[:, hs] = _dot(ds, kh).astype(BF16)
            dkv_ref[:, hs] += _dot(ds, qh, TN)

    row = pl.BlockSpec((tq, D), lambda i: (i, 0))
    full = pl.BlockSpec((M, 2 * D), lambda i: (0, 0))
    return _pcall(body, name="attn_bwd",
                  out_shape=(jax.ShapeDtypeStruct((T, D), BF16), jax.ShapeDtypeStruct((M, 2 * D), F32)),
                  grid=(T // tq,), in_specs=[row, full, row], out_specs=(row, full),
                  semantics=("arbitrary",))(q, kv, do)


def _shift_down(x, halo, s):
    out = pltpu.roll(x, s, 0)
    t8 = lax.broadcasted_iota(jnp.int32, (SUBLANES, 1), 0)
    head = out[:SUBLANES]
    for j in range(s):
        head = jnp.where(t8 == j, halo[SUBLANES - s + j:SUBLANES - s + j + 1, :], head)
    return head if x.shape[0] == SUBLANES else jnp.concatenate([head, out[SUBLANES:]], axis=0)


def _shift_up(x, halo, s):
    rows = x.shape[0]
    out = pltpu.roll(x, rows - s, 0)
    t8 = lax.broadcasted_iota(jnp.int32, (SUBLANES, 1), 0)
    tail = out[rows - SUBLANES:]
    for j in range(s):
        tail = jnp.where(t8 == SUBLANES - s + j, halo[j:j + 1, :], tail)
    return jnp.concatenate([out[:rows - SUBLANES], tail], axis=0)


def _conv_tiles(T):
    tt = _tile(T, 128, SUBLANES)
    return tt, tt // SUBLANES, T // SUBLANES


def _conv_fwd(u0, conv_w, conv_b, *, T, F):
    tt, hb, _ = _conv_tiles(T)
    cw = _tile(F, LANES)

    def body(u_ref, prev_ref, w_ref, b_ref, f_ref):
        i = pl.program_id(0)

        def conv(cs):
            x = u_ref[:, cs]
            halo = jnp.where(i > 0, prev_ref[:, cs], 0.0)
            return (w_ref[2:3, cs] * x + w_ref[1:2, cs] * _shift_down(x, halo, 1)
                    + w_ref[0:1, cs] * _shift_down(x, halo, 2) + b_ref[:, cs])

        for j in range(F // cw):
            gate = conv(slice(j * cw, (j + 1) * cw))
            val = conv(slice(F + j * cw, F + (j + 1) * cw))
            f_ref[:, j * cw:(j + 1) * cw] = (gate * _sigmoid(gate) * val).astype(BF16)

    return _pcall(body, name="conv_fwd", out_shape=jax.ShapeDtypeStruct((T, F), BF16),
                  grid=(T // tt,),
                  in_specs=[pl.BlockSpec((tt, 2 * F), lambda i: (i, 0)),
                            pl.BlockSpec((SUBLANES, 2 * F), lambda i: (jnp.maximum(i * hb - 1, 0), 0)),
                            pl.BlockSpec((CONV_W, 2 * F), lambda i: (0, 0)),
                            pl.BlockSpec((1, 2 * F), lambda i: (0, 0))],
                  out_specs=pl.BlockSpec((tt, F), lambda i: (i, 0)),
                  semantics=("parallel",))(u0, u0, conv_w, conv_b)


def _conv_bwd(u0, conv_w, conv_b, df, *, T, F):
    tt, hb, nb = _conv_tiles(T)
    nt = T // tt
    cw = _tile(F, LANES)

    def body(u_ref, prev_ref, next_ref, df_ref, dfn_ref, w_ref, b_ref, du0_ref, dw_ref, db_ref):
        i = pl.program_id(0)

        @pl.when(i == 0)
        def _():
            dw_ref[...] = jnp.zeros_like(dw_ref)
            db_ref[...] = jnp.zeros_like(db_ref)

        def conv(cs):
            x = u_ref[:, cs]
            halo = jnp.where(i > 0, prev_ref[:, cs], 0.0)
            x1 = _shift_down(x, halo, 1)
            x2 = _shift_down(x, halo, 2)
            u = w_ref[2:3, cs] * x + w_ref[1:2, cs] * x1 + w_ref[0:1, cs] * x2 + b_ref[:, cs]
            xn = next_ref[:, cs]
            tail = x[tt - SUBLANES:, :]
            un = (w_ref[2:3, cs] * xn + w_ref[1:2, cs] * _shift_down(xn, tail, 1)
                  + w_ref[0:1, cs] * _shift_down(xn, tail, 2) + b_ref[:, cs])
            return u, un, (x, x1, x2)

        ones = jnp.ones((SUBLANES, tt), F32)

        def colsum(v):
            return _dot(ones, v, NN, HIGHEST)[0:1]

        def glu_grad(gate, val, dff):
            sg = _sigmoid(gate)
            return dff * val * (sg * (1.0 + gate * (1.0 - sg))), dff * (gate * sg)

        def finish(cs, du, dun, xs):
            du0 = (w_ref[2:3, cs] * du + w_ref[1:2, cs] * _shift_up(du, dun, 1)
                   + w_ref[0:1, cs] * _shift_up(du, dun, 2))
            du0_ref[:, cs] = du0.astype(BF16)
            db_ref[:, cs] += colsum(du)
            dw_ref[2:3, cs] += colsum(du * xs[0])
            dw_ref[1:2, cs] += colsum(du * xs[1])
            dw_ref[0:1, cs] += colsum(du * xs[2])

        for j in range(F // cw):
            fs = slice(j * cw, (j + 1) * cw)
            gs, vs = fs, slice(F + j * cw, F + (j + 1) * cw)
            ug, ung, xg = conv(gs)
            uv, unv, xv = conv(vs)
            dug, duv = glu_grad(ug, uv, df_ref[:, fs].astype(F32))
            dung, dunv = glu_grad(ung, unv, dfn_ref[0:SUBLANES, fs].astype(F32))
            dung = jnp.where(i < nt - 1, dung, 0.0)
            dunv = jnp.where(i < nt - 1, dunv, 0.0)
            finish(gs, dug, dung, xg)
            finish(vs, duv, dunv, xv)

    wide = lambda rows, fn: pl.BlockSpec((rows, 2 * F), fn)
    nxt = lambda i: (jnp.minimum((i + 1) * hb, nb - 1), 0)
    return _pcall(body, name="conv_bwd",
                  out_shape=(jax.ShapeDtypeStruct((T, 2 * F), BF16),
                             jax.ShapeDtypeStruct((CONV_W, 2 * F), F32),
                             jax.ShapeDtypeStruct((1, 2 * F), F32)),
                  grid=(nt,),
                  in_specs=[wide(tt, lambda i: (i, 0)),
                            wide(SUBLANES, lambda i: (jnp.maximum(i * hb - 1, 0), 0)),
                            wide(SUBLANES, nxt),
                            pl.BlockSpec((tt, F), lambda i: (i, 0)),
                            pl.BlockSpec((2 * SUBLANES, F),
                                         lambda i: (jnp.minimum((i + 1) * (hb // 2), nb // 2 - 1), 0)),
                            wide(CONV_W, lambda i: (0, 0)), wide(1, lambda i: (0, 0))],
                  out_specs=(wide(tt, lambda i: (i, 0)), wide(CONV_W, lambda i: (0, 0)),
                             wide(1, lambda i: (0, 0))),
                  semantics=("arbitrary",))(u0, u0, u0, df, df, conv_w, conv_b)


def _adamw(w, g, m, v, *, name):
    R, C = w.shape
    tr = _tile(R, max(SUBLANES, (1 << 19) // max(C, 1) // SUBLANES * SUBLANES), SUBLANES)
    c1 = 1.0 / (1.0 - ADAM_B1 ** ADAM_STEP)
    c2 = 1.0 / (1.0 - ADAM_B2 ** ADAM_STEP)

    def body(w_ref, g_ref, m_ref, v_ref, d_ref, mo_ref, vo_ref):
        gv = g_ref[...]
        mn = ADAM_B1 * m_ref[...] + (1.0 - ADAM_B1) * gv
        vn = ADAM_B2 * v_ref[...] + (1.0 - ADAM_B2) * (gv * gv)
        d_ref[...] = -ADAM_LR * ((mn * c1) / (jnp.sqrt(vn * c2) + ADAM_EPS) + ADAM_WD * w_ref[...])
        mo_ref[...] = mn
        vo_ref[...] = vn

    blk = pl.BlockSpec((tr, C), lambda i: (i, 0))
    shp = jax.ShapeDtypeStruct((R, C), F32)
    return _pcall(body, name=name, out_shape=(shp, shp, shp), grid=(R // tr,),
                  in_specs=[blk] * 4, out_specs=(blk,) * 3, semantics=("parallel",))(w, g, m, v)


def _blk(h, C, elems=1 << 19, align=16):
    th = _tile(h, max(align, elems // C // align * align), align)
    if th < h or h * C <= 2 * elems:
        return th, C
    return h, _tile(C, max(LANES, elems // h // LANES * LANES))


def _adamw_halves(w, m, v, g_mine, g_other, c_idx, *, name):
    _, h, C = w.shape
    th, tc = _blk(h, C, align=SUBLANES)
    c1 = 1.0 / (1.0 - ADAM_B1 ** ADAM_STEP)
    c2 = 1.0 / (1.0 - ADAM_B2 ** ADAM_STEP)

    def body(c_ref, w_ref, m_ref, v_ref, gm_ref, go_ref, g_ref, d_ref, mo_ref, vo_ref):
        gv = jnp.where(pl.program_id(0) == c_ref[0], gm_ref[...], go_ref[...])
        mn = ADAM_B1 * m_ref[...] + (1.0 - ADAM_B1) * gv
        vn = ADAM_B2 * v_ref[...] + (1.0 - ADAM_B2) * (gv * gv)
        d_ref[...] = -ADAM_LR * ((mn * c1) / (jnp.sqrt(vn * c2) + ADAM_EPS) + ADAM_WD * w_ref[...])
        g_ref[...] = gv
        mo_ref[...] = mn
        vo_ref[...] = vn

    blk = pl.BlockSpec((None, th, tc), lambda s, i, j, c: (s, i, j))

    def pick(mine):
        def index(s, i, j, c):
            use = (s == c[0]) if mine else (s != c[0])
            return jnp.where(use, i, 0), jnp.where(use, j, 0)
        return pl.BlockSpec((th, tc), index)

    shp = jax.ShapeDtypeStruct((2, h, C), F32)
    return _pcall(body, name=name, out_shape=(shp,) * 4, grid=(2, h // th, C // tc), prefetch=1,
                  in_specs=[blk, blk, blk, pick(True), pick(False)], out_specs=(blk,) * 4,
                  semantics=("parallel", "parallel", "parallel"))(c_idx, w, m, v, g_mine, g_other)


def _mesh_pos():
    x, y, c = lax.axis_index("x"), lax.axis_index("y"), lax.axis_index("c")
    others = [(1 - x, y), (x, 1 - y), (1 - x, 1 - y)]
    return x, y, c, others


def _gather_copies(shards, lands, send_sems, recv_sems):
    x, y, c, others = _mesh_pos()
    me = 2 * x + y
    return [pltpu.make_async_remote_copy(
        src_ref=shards[a].at[c], dst_ref=lands[a].at[me, c],
        send_sem=send_sems.at[3 * a + j], recv_sem=recv_sems.at[3 * a + j],
        device_id=(*chip, c), device_id_type=MESH)
        for a in range(len(shards)) for j, chip in enumerate(others)]


def _pass_copies(shards, zones, send_sems, recv_sems):
    x, y, c, others = _mesh_pos()
    me = 2 * x + y
    copies = []
    for a in range(len(shards)):
        srcs = [zones[a].at[2 * chip[0] + chip[1], c] for chip in others] + [shards[a]]
        dsts = [zones[a].at[2 * chip[0] + chip[1], c] for chip in others] + [zones[a].at[me]]
        copies += [pltpu.make_async_remote_copy(
            src_ref=s, dst_ref=d, send_sem=send_sems.at[4 * a + k], recv_sem=recv_sems.at[4 * a + k],
            device_id=(x, y, 1 - c), device_id_type=MESH) for k, (s, d) in enumerate(zip(srcs, dsts))]
    return copies


def _exchange_copies(grads, recvs, send_sems, recv_sems):
    x, y, c, _ = _mesh_pos()
    return [pltpu.make_async_remote_copy(
        src_ref=grads[a].at[:, 1 - c], dst_ref=recvs[a], send_sem=send_sems.at[a],
        recv_sem=recv_sems.at[a], device_id=(x, y, 1 - c), device_id_type=MESH) for a in range(len(grads))]


def _split_start(copies, per, srcs, zones, after, *, name):
    n = len(srcs)
    HBM = pl.BlockSpec(memory_space=pltpu.HBM)
    SEM = pl.BlockSpec(memory_space=pltpu.SEMAPHORE)

    def body(*refs):
        send_sems, recv_sems = refs[2 * n + 1], refs[2 * n + 2]
        for cp in copies(refs[:n], refs[n:2 * n], send_sems, recv_sems):
            cp.start()
        refs[-1][...] = jnp.zeros_like(refs[-1])

    hbm = lambda a: pltpu.HBM(a.shape, a.dtype)
    res = _pcall(body, name=name,
                 out_shape=(pltpu.SemaphoreType.DMA((per * n,)), pltpu.SemaphoreType.DMA((per * n,)),
                            *[hbm(a) for a in srcs], *[hbm(a) for a in zones],
                            jax.ShapeDtypeStruct((SUBLANES, LANES), F32)),
                 in_specs=[*[HBM] * (2 * n), pl.BlockSpec(memory_space=pl.ANY)],
                 out_specs=(SEM, SEM, *[HBM] * (2 * n), pl.BlockSpec(memory_space=pltpu.VMEM)),
                 aliases={i: 2 + i for i in range(2 * n)}, split_copy=True)(
        *[pltpu.with_memory_space_constraint(a, pltpu.HBM) for a in [*srcs, *zones]], after)
    return res[0], res[1], list(res[2:2 + n]), list(res[2 + n:2 + 2 * n]), res[-1]


def _split_wait(copies, send_sems, recv_sems, srcs, zones, after, *, name):
    n = len(srcs)
    HBM = pl.BlockSpec(memory_space=pltpu.HBM)
    SEM = pl.BlockSpec(memory_space=pltpu.SEMAPHORE)

    def body(*refs):
        for cp in copies(refs[:n], refs[n:2 * n], refs[2 * n], refs[2 * n + 1]):
            cp.wait_send()
            cp.wait_recv()

    hbm = lambda a: pltpu.HBM(a.shape, a.dtype)
    res = _pcall(body, name=name, out_shape=(*[hbm(a) for a in srcs], *[hbm(a) for a in zones]),
                 in_specs=[*[HBM] * (2 * n), SEM, SEM, pl.BlockSpec(memory_space=pl.ANY)],
                 out_specs=tuple([HBM] * (2 * n)), aliases={i: i for i in range(2 * n)},
                 split_copy=True)(*srcs, *zones, send_sems, recv_sems, after)
    return list(res[:n]), list(res[n:])


def _add_halves(grad, recv, c_idx, *, name):
    S, _, h, C = grad.shape
    th, tc = _blk(h, C)

    def body(c_ref, g_ref, r_ref, o_ref):
        o_ref[...] = (g_ref[...].astype(F32) + r_ref[...].astype(F32)).astype(o_ref.dtype)

    return _pcall(body, name=name, out_shape=jax.ShapeDtypeStruct((S, h, C), grad.dtype),
                  grid=(S, h // th, C // tc), prefetch=1,
                  in_specs=[pl.BlockSpec((None, None, th, tc), lambda s, i, j, c: (s, c[0], i, j)),
                            pl.BlockSpec((None, th, tc), lambda s, i, j, c: (s, i, j))],
                  out_specs=pl.BlockSpec((None, th, tc), lambda s, i, j, c: (s, i, j)),
                  semantics=("parallel", "parallel", "parallel"))(c_idx, grad, recv)


def _scatter_copies(srcs, lands, send_sems, recv_sems):
    x, y, c, others = _mesh_pos()
    return [pltpu.make_async_remote_copy(
        src_ref=srcs[a].at[2 * chip[0] + chip[1]], dst_ref=lands[a].at[j],
        send_sem=send_sems.at[3 * a + j], recv_sem=recv_sems.at[3 * a + j],
        device_id=(*chip, c), device_id_type=MESH)
        for a in range(len(srcs)) for j, chip in enumerate(others)]


def _add_chips(sums, recv, chip_idx, *, name):
    _, h, C = sums.shape
    th, tc = _blk(h, C)

    def body(k_ref, s_ref, r_ref, o_ref):
        acc = s_ref[...].astype(F32) + r_ref[0].astype(F32)
        acc = acc + r_ref[1].astype(F32)
        o_ref[...] = acc + r_ref[2].astype(F32)

    return _pcall(body, name=name, out_shape=jax.ShapeDtypeStruct((h, C), F32),
                  grid=(h // th, C // tc), prefetch=1,
                  in_specs=[pl.BlockSpec((None, th, tc), lambda i, j, k: (k[0], i, j)),
                            pl.BlockSpec((3, th, tc), lambda i, j, k: (0, i, j))],
                  out_specs=pl.BlockSpec((th, tc), lambda i, j, k: (i, j)),
                  semantics=("parallel", "parallel"))(chip_idx, sums, recv)


def _swap_copies(halves, others, send_sems, recv_sems):
    x, y, c, _ = _mesh_pos()
    return [pltpu.make_async_remote_copy(
        src_ref=halves[a], dst_ref=others[a], send_sem=send_sems.at[a], recv_sem=recv_sems.at[a],
        device_id=(x, y, 1 - c), device_id_type=MESH) for a in range(len(halves))]


def _all_reduce_small(buf):
    R, L = buf.shape
    NDEV = 8

    def body(x_ref, sum_ref, all_ref, send_sems, recv_sems, local_sem):
        x, y, c, others = _mesh_pos()
        me, sibling = (x, y, c), (x, y, 1 - c)

        def slot(px, py, pc):
            return all_ref.at[4 * px + 2 * py + pc]

        def copy(k, block, to, src=None):
            return pltpu.make_async_remote_copy(
                src_ref=slot(*block) if src is None else src, dst_ref=slot(*block),
                send_sem=send_sems.at[k], recv_sem=recv_sems.at[k], device_id=to, device_id_type=MESH)

        mine = pltpu.make_async_copy(x_ref, slot(*me), local_sem)
        mine.start()
        first = [copy(0, me, sibling, src=x_ref)]
        first += [copy(1 + j, me, (*chip, c), src=x_ref) for j, chip in enumerate(others)]
        for cp in first:
            cp.start()
        passed = [copy(4 + j, (*chip, c), sibling) for j, chip in enumerate(others)]
        for j, chip in enumerate(others):
            copy(1 + j, (*chip, c), me).wait_recv()
            passed[j].start()
        copy(0, sibling, me).wait_recv()
        for j, chip in enumerate(others):
            copy(4 + j, (*chip, 1 - c), me).wait_recv()
        for cp in first + passed:
            cp.wait_send()
        mine.wait()
        acc = all_ref[0]
        for d in range(1, NDEV):
            acc = acc + all_ref[d]
        sum_ref[...] = acc

    VM = pl.BlockSpec(memory_space=pltpu.VMEM)
    return _pcall(body, name="all_reduce_small",
                  out_shape=(jax.ShapeDtypeStruct((R, L), F32), jax.ShapeDtypeStruct((NDEV, R, L), F32)),
                  in_specs=[VM], out_specs=(VM, VM),
                  scratch_shapes=[pltpu.SemaphoreType.DMA((7,)), pltpu.SemaphoreType.DMA((7,)),
                                  pltpu.SemaphoreType.DMA])(buf)[0]


def _pack(arrs, rows_multiple=16):
    flat = [a.reshape(-1).astype(F32) for a in arrs]
    sizes = [f.shape[0] for f in flat]
    total = sum(sizes)
    per = LANES * rows_multiple
    padded = -(-total // per) * per
    flat.append(jnp.zeros((padded - total,), F32))
    offs = [0]
    for s in sizes:
        offs.append(offs[-1] + s)
    return jnp.concatenate(flat).reshape(padded // LANES, LANES), offs


def _unpack(buf, offs, shapes):
    flat = buf.reshape(-1)
    return [flat[offs[i]:offs[i + 1]].reshape(s) for i, s in enumerate(shapes)]


def kernel(x, mem, g_mix, w_in, w_a2, b_a, g_gla, w_pool, pool_scale, w_branch, w_out, g_cross, g_mem, w_cq, w_ckv, w_co, g_ffn, w_up, conv_w, conv_b, w_down, g_final, loss_target, m_g_mix, m_w_in, m_w_a2, m_b_a, m_g_gla, m_w_pool, m_pool_scale, m_w_branch, m_w_out, m_g_cross, m_g_mem, m_w_cq, m_w_ckv, m_w_co, m_g_ffn, m_w_up, m_conv_w, m_conv_b, m_w_down, m_g_final, v_g_mix, v_w_in, v_w_a2, v_b_a, v_g_gla, v_w_pool, v_pool_scale, v_w_branch, v_w_out, v_g_cross, v_g_mem, v_w_cq, v_w_ckv, v_w_co, v_g_ffn, v_w_up, v_conv_w, v_conv_b, v_w_down, v_g_final):
    weights = dict(g_mix=g_mix, w_in=w_in, w_a2=w_a2, b_a=b_a, g_gla=g_gla, w_pool=w_pool,
                   pool_scale=pool_scale, w_branch=w_branch, w_out=w_out, g_cross=g_cross, g_mem=g_mem,
                   w_cq=w_cq, w_ckv=w_ckv, w_co=w_co, g_ffn=g_ffn, w_up=w_up, conv_w=conv_w,
                   conv_b=conv_b, w_down=w_down, g_final=g_final)
    mom_m = dict(g_mix=m_g_mix, w_in=m_w_in, w_a2=m_w_a2, b_a=m_b_a, g_gla=m_g_gla, w_pool=m_w_pool,
                 pool_scale=m_pool_scale, w_branch=m_w_branch, w_out=m_w_out, g_cross=m_g_cross,
                 g_mem=m_g_mem, w_cq=m_w_cq, w_ckv=m_w_ckv, w_co=m_w_co, g_ffn=m_g_ffn, w_up=m_w_up,
                 conv_w=m_conv_w, conv_b=m_conv_b, w_down=m_w_down, g_final=m_g_final)
    mom_v = dict(g_mix=v_g_mix, w_in=v_w_in, w_a2=v_w_a2, b_a=v_b_a, g_gla=v_g_gla, w_pool=v_w_pool,
                 pool_scale=v_pool_scale, w_branch=v_w_branch, w_out=v_w_out, g_cross=v_g_cross,
                 g_mem=v_g_mem, w_cq=v_w_cq, w_ckv=v_w_ckv, w_co=v_w_co, g_ffn=v_g_ffn, w_up=v_w_up,
                 conv_w=v_conv_w, conv_b=v_conv_b, w_down=v_w_down, g_final=v_g_final)
    order = list(weights)
    big = ["w_in", "w_branch", "w_out", "w_cq", "w_ckv", "w_co", "w_up", "w_down"]
    small_sharded = ["w_a2", "w_pool", "conv_w"]
    small_repl = ["g_mix", "b_a", "g_gla", "pool_scale", "g_cross", "g_mem", "g_ffn", "conv_b", "g_final"]

    xs, ms, tgt = x[0], mem[0], loss_target[0]
    T, D = xs.shape
    M = ms.shape[0]
    DK, DV, PW = b_a.shape[1], g_gla.shape[1], pool_scale.shape[1]
    RANK = w_a2.shape[1]
    F2 = conv_b.shape[1]
    F = F2 // 2
    DIN = N_CHIPS * w_in.shape[2]
    OFF_A = 2 * DK + 2 * DV
    OFF_P = OFF_A + RANK
    RP = LANES
    GW = PW // POOL_GROUPS
    assert PW == DV and 4 * DV == 2 * D and OFF_P + PW + 2 * D == DIN

    cx, cy, cc = lax.axis_index("x"), lax.axis_index("y"), lax.axis_index("c")
    chip = 2 * cx + cy
    c_idx = jnp.reshape(cc, (1,)).astype(jnp.int32)
    chip_idx = jnp.reshape(chip, (1,)).astype(jnp.int32)

    def halves(a):
        return a.reshape(2, a.shape[0] // 2, a.shape[1])

    shard2d = {k: (weights[k][0].T if k == "w_in" else weights[k][0]) for k in big}
    small_pack, small_offs = _pack([weights[k][0] for k in small_sharded], rows_multiple=32)
    flying, passing = {}, {}
    tok = xs
    for group, keys in (("in", ["w_in"]), ("mix", ["w_branch", "w_out", "small"]),
                        ("cross", ["w_cq", "w_ckv", "w_co"]), ("up", ["w_up"]), ("down", ["w_down"])):
        srcs = [small_pack if k == "small" else shard2d[k].astype(BF16) for k in keys]
        if group != "in":
            srcs = [a + tok[0:1, 0:1].astype(a.dtype) for a in srcs]
        srcs = [halves(a) for a in srcs]
        zones = [lax.empty((N_CHIPS, *s.shape), s.dtype) for s in srcs]
        s_sems, r_sems, srcs, zones, tok = _split_start(_gather_copies, 3, srcs, zones, tok,
                                                        name=f"gather_start_{group}")
        flying[group] = (keys, s_sems, r_sems, srcs, zones)

    def landed(group, after):
        keys, s_sems, r_sems, srcs, zones = flying[group]
        srcs, zones = _split_wait(_gather_copies, s_sems, r_sems, srcs, zones, after,
                                  name=f"gather_wait_{group}")
        s_sems, r_sems, srcs, zones, token = _split_start(_pass_copies, 4, srcs, zones, after,
                                                          name=f"gather_pass_start_{group}")
        passing[group] = (keys, s_sems, r_sems, srcs, zones)
        return token

    def arrive(group, after):
        keys, s_sems, r_sems, srcs, zones = passing[group]
        _, full = _split_wait(_pass_copies, s_sems, r_sems, srcs, zones, after,
                              name=f"gather_pass_wait_{group}")
        return {k: f.reshape(N_CHIPS, f.shape[1] * f.shape[2], f.shape[3]) for k, f in zip(keys, full)}

    def rows(g):
        return g.reshape(-1, g.shape[2])

    h1, r1 = _rms_fwd(xs, g_mix + tok[0:1, 0:1], name="norm_mix")
    landed("in", h1)
    gw = arrive("in", h1)
    W_in = rows(gw["w_in"])
    W_main = jnp.concatenate([W_in[:OFF_A], W_in[OFF_P:]], axis=0)
    W_a = jnp.pad(W_in[OFF_A:OFF_P], ((0, RP - RANK), (0, 0)))
    tok = landed("mix", W_a)
    proj = _mm(h1, W_main, "nt", name="proj_main", out_dtype=F32, after=tok)
    gw = arrive("mix", proj)
    W_branch, W_out, small_all = rows(gw["w_branch"]), rows(gw["w_out"]), gw["small"]
    sm = [_unpack(small_all[j], small_offs, [weights[k].shape[1:] for k in small_sharded]) for j in range(N_CHIPS)]
    W_a2 = jnp.concatenate([sm[j][0] for j in range(N_CHIPS)], axis=1)
    W_a2p = jnp.pad(W_a2, ((0, RP - RANK), (0, 0))).astype(BF16)
    W_pool = jnp.concatenate([sm[j][1] for j in range(N_CHIPS)], axis=1).astype(BF16)
    W_conv = jnp.concatenate([sm[j][2] for j in range(N_CHIPS)], axis=1)

    a_pad = _mm(h1, W_a, "nt", name="proj_gate_rank", out_dtype=F32)
    o_gla, o_raw, states = _gla_fwd(proj, a_pad, W_a2p, b_a, g_gla, T=T, DK=DK, DV=DV)
    o_pool = _pool_fwd(proj, W_pool, pool_scale, T=T, PW=PW, col_block=3)
    tok = landed("cross", o_pool)
    y_gla = _mm(o_gla, W_branch, "nn", name="branch_gla", out_dtype=BF16, K=DV, after=tok)
    y_pool = _mm(o_pool, W_branch, "nn", name="branch_pool", out_dtype=BF16, K=PW, b_off=(DV, 0))
    merged = _merge_fwd(y_gla, y_pool, proj, T=T, D=D, col_block=2)
    x1 = _mm(merged, W_out, "nn", name="mix_out", out_dtype=F32, add=xs)

    h2, r2 = _rms_fwd(x1, g_cross, name="norm_cross")
    mem_n, rm = _rms_fwd(ms, g_mem, name="norm_mem")
    gw = arrive("cross", h2)
    W_cq, W_ckv, W_co = rows(gw["w_cq"]), gw["w_ckv"], rows(gw["w_co"])
    qc = _mm(h2, W_cq, "nn", name="cross_q", out_dtype=BF16)
    kv = _mm(mem_n, W_ckv, "nn", name="cross_kv", out_dtype=BF16, b_blocked=True)
    o_att = _attn_fwd(qc, kv, T=T, D=D, M=M)
    x2 = _mm(o_att, W_co, "nn", name="cross_out", out_dtype=F32, add=x1)

    tok = landed("up", x2)
    h3, r3 = _rms_fwd(x2, g_ffn + tok[0:1, 0:1], name="norm_ffn")
    W_up = arrive("up", h3)["w_up"]
    u0 = _mm(h3, W_up, "nn", name="ffn_up", out_dtype=F32, b_blocked=True)
    tok = landed("down", u0)
    f_act = _conv_fwd(u0, W_conv, conv_b + tok[0:1, 0:1], T=T, F=F)
    W_down = rows(arrive("down", f_act)["w_down"])
    x3 =_mm(f_act, W_down, "nn", name="ffn_down", out_dtype=F32, add=x2)

    loss_part, dx3, dx3_b, dg_final = _loss_head(x3, g_final.reshape(1, D), tgt)

    def col_shards(g):
        nb, K, Nb = g.shape
        return g.reshape(nb, 2, K // 2, Nb)

    def row_shards(g):
        R, N = g.shape
        return g.reshape(N_CHIPS, 2, R // N_CHIPS // 2, N)

    exchanging, in_flight = {}, []

    def exchange_start(group, keys, partials, after):
        recvs = [lax.empty((p.shape[0], *p.shape[2:]), p.dtype) for p in partials]
        s_sems, r_sems, partials, recvs, token = _split_start(
            _exchange_copies, 1, partials, recvs, after, name=f"grad_exchange_start_{group}")
        exchanging[group] = (keys, s_sems, r_sems, partials, recvs)
        return token

    def scatter_start(group, after):
        keys, s_sems, r_sems, partials, recvs = exchanging[group]
        partials, recvs = _split_wait(_exchange_copies, s_sems, r_sems, partials, recvs, after,
                                      name=f"grad_exchange_wait_{group}")
        chip_sums = [_add_halves(p, r, c_idx, name=f"grad_add_halves_{k}")
                     for k, p, r in zip(keys, partials, recvs)]
        lands = [lax.empty((3, *s.shape[1:]), s.dtype) for s in chip_sums]
        s_sems, r_sems, sums, lands, token = _split_start(
            _scatter_copies, 3, chip_sums, lands, after, name=f"grad_scatter_start_{group}")
        in_flight.append((group, keys, s_sems, r_sems, sums, lands))
        return token

    df = _mm(dx3_b, W_down, "nt", name="d_ffn_act", out_dtype=BF16)
    dW_down = _mm(f_act, dx3_b, "tn", name="dw_down", out_dtype=BF16)
    du0, dconv_w, dconv_b = _conv_bwd(u0, W_conv, conv_b, df, T=T, F=F)
    dh3 = _mm(du0, W_up, "nt", name="d_ffn_in", out_dtype=F32, b_blocked=True, tk=F2 // N_CHIPS)
    dW_up = _mm(h3, du0, "tn", name="dw_up", out_dtype=BF16, out_blocks=N_CHIPS)
    tok = exchange_start("ffn", ["w_down", "w_up"], [row_shards(dW_down), col_shards(dW_up)], dh3)
    dx2, dx2_b, dg_ffn = _rms_bwd(dh3, x2, r3 + tok[0:1, 0:1], g_ffn, dx3, name="norm_ffn_bwd")

    do_att = _mm(dx2_b, W_co, "nt", name="d_cross_o", out_dtype=BF16)
    dW_co = _mm(o_att, dx2_b, "tn", name="dw_co", out_dtype=BF16)
    tok = scatter_start("ffn", dW_co)
    dq, dkv = _attn_bwd(qc, kv, do_att, T=T, D=D, M=M)
    dkv_b = dkv.astype(BF16)
    dW_cq = _mm(h2, dq, "tn", name="dw_cq", out_dtype=BF16, after=tok)
    dh2 = _mm(dq, W_cq, "nt", name="d_cross_in", out_dtype=F32)
    dW_ckv = _mm(mem_n, dkv_b, "tn", name="dw_ckv", out_dtype=BF16, out_blocks=N_CHIPS)
    dmem_n = _mm(dkv_b, W_ckv, "nt", name="d_mem", out_dtype=F32, b_blocked=True)
    tok = exchange_start("cross", ["w_co", "w_cq", "w_ckv"],
                         [row_shards(dW_co), row_shards(dW_cq), col_shards(dW_ckv)], dmem_n)
    _, _, dg_mem = _rms_bwd(dmem_n, ms, rm, g_mem, None, name="norm_mem_bwd")
    dx1, dx1_b, dg_cross = _rms_bwd(dh2, x1, r2 + tok[0:1, 0:1], g_cross, dx2, name="norm_cross_bwd")

    dmerged = _mm(dx1_b, W_out, "nt", name="d_merged", out_dtype=BF16)
    dW_out = _mm(merged, dx1_b, "tn", name="dw_out", out_dtype=BF16)
    tok = scatter_start("cross", dW_out)
    dy_gla, dy_pool, dgates = _merge_bwd(dmerged, y_gla, y_pool, proj, T=T, D=D, col_block=2)
    dW_br_gla = _mm(o_gla, dy_gla, "tn", name="dw_branch_gla", out_dtype=BF16, after=tok)
    dW_br_pool = _mm(o_pool, dy_pool, "tn", name="dw_branch_pool", out_dtype=BF16)
    do_gla = _mm(dy_gla, W_branch, "nt", name="d_o_gla", out_dtype=F32, N=DV)
    do_pool = _mm(dy_pool, W_branch, "nt", name="d_o_pool", out_dtype=F32, N=PW, b_off=(DV, 0))
    dp, dw_pool, dpool_scale = _pool_bwd(proj, W_pool, pool_scale, do_pool, T=T, PW=PW, col_block=3)
    dW_pool = jnp.transpose(dw_pool.reshape(POOL_GROUPS, N_CHIPS, GW // N_CHIPS, GW), (1, 0, 2, 3))
    tok = exchange_start("mix", ["w_out", "w_branch", "w_pool"],
                         [row_shards(dW_out), row_shards(jnp.concatenate([dW_br_gla, dW_br_pool], axis=0)),
                          row_shards(dW_pool.reshape(N_CHIPS * POOL_GROUPS * (GW // N_CHIPS), GW).astype(BF16))],
                         dp)
    dqkvr, da_pad, dw2, db_a, dg_gla = _gla_bwd(proj, a_pad, W_a2p, b_a + tok[0:1, 0:1], g_gla, o_raw, states,
                                               do_gla, T=T, DK=DK, DV=DV)
    tok = scatter_start("mix", dqkvr)
    dproj = jnp.concatenate([dqkvr, dp, dgates], axis=1)
    dW_main = _mm(dproj, h1, "tn", name="dw_in_main", out_dtype=BF16, after=tok)
    dW_a = _mm(da_pad, h1, "tn", name="dw_in_rank", out_dtype=BF16)
    dW_in = jnp.concatenate([dW_main[:OFF_A], dW_a[:RANK], dW_main[OFF_A:]], axis=0)
    tok = exchange_start("in", ["w_in"], [row_shards(dW_in)], dW_a)
    dh1 = _mm(dproj, W_main, "nn", name="d_mix_in_main", out_dtype=F32, after=tok)
    dh1 = _mm(da_pad, W_a, "nn", name="d_mix_in_rank", out_dtype=F32, add=dh1)
    dx0, _, dg_mix = _rms_bwd(dh1, xs, r1, g_mix, dx1, name="norm_mix_bwd")

    grads = {}

    small_grads = [loss_part, dg_mix, db_a, dg_gla, dpool_scale, dg_cross, dg_mem, dg_ffn, dconv_b, dg_final,
                   dw2[:RANK], dconv_w]
    small_buf, offs = _pack(small_grads)
    small_sum = _all_reduce_small(small_buf)
    red = _unpack(small_sum, offs, [g.shape for g in small_grads])
    loss = red[0][0, 0]
    for k, g in zip(small_repl, red[1:10]):
        grads[k] = g.reshape(weights[k].shape)
    nb = DK // N_CHIPS
    grads["w_a2"] = lax.dynamic_slice_in_dim(red[10], chip * nb, nb, axis=1)[None]
    nb = F2 // N_CHIPS
    grads["conv_w"] = lax.dynamic_slice_in_dim(red[11], chip * nb, nb, axis=1)[None]

    delta, new_m, new_v = {}, {}, {}

    def shard_rows(k, a):
        a = a[0]
        return a.T if k == "w_in" else a.reshape(-1, a.shape[-1])

    def whole(k, a):
        a = a.reshape(-1, a.shape[2])
        return (a.T if k == "w_in" else a).reshape(weights[k].shape)

    scatter_start("in", small_sum)
    after = in_flight[-1][4][0]

    def finish(swapping):
        keys, s_sems, r_sems, mine, others, after = swapping
        mine, others = _split_wait(_swap_copies, s_sems, r_sems, mine, others, after,
                                   name=f"grad_swap_wait_{keys[0]}")
        for k, g_mine, g_other in zip(keys, mine, others):
            wmv = [halves(shard_rows(k, src[k])) for src in (weights, mom_m, mom_v)]
            res = _adamw_halves(*wmv, g_mine, g_other, c_idx, name=f"adamw_{k}")
            grads[k], delta[k], new_m[k], new_v[k] = (whole(k, a) for a in res)
        return res[1]

    swapping = None
    for group, keys, s_sems, r_sems, sums, lands in in_flight:
        sums, from_chips = _split_wait(_scatter_copies, s_sems, r_sems, sums, lands, after,
                                       name=f"grad_scatter_wait_{group}")
        half_sums = [_add_chips(s, r, chip_idx, name=f"grad_add_chips_{k}") for k, s, r in zip(keys, sums, from_chips)]
        others = [lax.empty(h.shape, h.dtype) for h in half_sums]
        s_sems, r_sems, half_sums, others, token = _split_start(
            _swap_copies, 1, half_sums, others, after, name=f"grad_swap_start_{group}")
        if swapping is not None:
            after = finish((*swapping, token))
        swapping = (keys, s_sems, r_sems, half_sums, others)
    finish((*swapping, after))
    small = small_repl + ["w_a2", "conv_w"]
    packs = [_pack([src[k] for k in small])[0] for src in (weights, grads, mom_m, mom_v)]
    _, offs = _pack([weights[k] for k in small])
    outs = _adamw(*packs, name="adamw_small")
    for res, o in zip((delta, new_m, new_v), outs):
        for k, a in zip(small, _unpack(o, offs, [weights[k].shape for k in small])):
            res[k] = a

    return (loss, dx0[None], *[grads[k] for k in order], *[delta[k] for k in order],
            *[new_m[k] for k in order], *[new_v[k] for k in order])
```

```python
import functools

import jax
import jax.numpy as jnp
from jax import lax
from jax.experimental import pallas as pl
from jax.experimental.pallas import tpu as pltpu

F32 = jnp.float32
BF16 = jnp.bfloat16
MESH = pl.DeviceIdType.MESH
HIGHEST = lax.Precision.HIGHEST

EPS = 1e-6
GLA_HEADS = 4
GLA_CHUNK = 64
GLA_GATE_NORM = 16.0
POOL_GROUPS = 4
CROSS_HEADS = 4
CONV_W = 3
N_CHIPS = 4
LANES = 128
SUBLANES = 8
VMEM_LIMIT = 56 << 20

ADAM_LR = 0.001
ADAM_B1 = 0.9
ADAM_B2 = 0.999
ADAM_EPS = 1e-08
ADAM_WD = 0.01
ADAM_STEP = 10

NN = (((1,), (0,)), ((), ()))
NT = (((1,), (1,)), ((), ()))
TN = (((0,), (0,)), ((), ()))


ONE_PASS = lax.Precision.HIGH


def _dot(a, b, dn=NN, precision=None):
    return lax.dot_general(a, b, dn, precision=precision, preferred_element_type=F32)


def _tile(n, pref, align=LANES):
    t = (min(pref, n) // align) * align
    while t >= align:
        if n % t == 0:
            return t
        t -= align
    return n


def _pcall(body, *, name, out_shape, grid=(), in_specs=None, out_specs=None, scratch_shapes=(),
           semantics=None, prefetch=0, aliases=None, split_copy=False):
    params = dict(vmem_limit_bytes=VMEM_LIMIT)
    if semantics is not None:
        params["dimension_semantics"] = semantics
    if split_copy:
        params["has_side_effects"] = pltpu.SideEffectType.DATAFLOW_SIDE_EFFECTING
    if prefetch:
        grid_spec = pltpu.PrefetchScalarGridSpec(
            num_scalar_prefetch=prefetch, grid=grid, in_specs=in_specs, out_specs=out_specs,
            scratch_shapes=scratch_shapes)
        return pl.pallas_call(body, name=name, out_shape=out_shape, grid_spec=grid_spec,
                              compiler_params=pltpu.CompilerParams(**params))
    kw = {}
    if aliases is not None:
        kw["input_output_aliases"] = aliases
    if in_specs is not None:
        kw["in_specs"] = in_specs
    if out_specs is not None:
        kw["out_specs"] = out_specs
    return pl.pallas_call(body, name=name, out_shape=out_shape, grid=grid,
                          scratch_shapes=scratch_shapes,
                          compiler_params=pltpu.CompilerParams(**params), **kw)


def _sigmoid(x):
    return 1.0 / (1.0 + jnp.exp(-x))


def _log_sigmoid(x):
    return jnp.minimum(x, 0.0) - jnp.log(1.0 + jnp.exp(-jnp.abs(x)))


def _mm(a, b, mode, *, name, out_dtype, M=None, N=None, K=None, a_off=(0, 0), b_off=(0, 0),
        add=None, b_blocked=False, out_blocks=0, after=None, tm=1536, tn=1536, tk=2048):
    if b_blocked:
        nb, R, Cb = b.shape
        b_rows, b_cols = R, nb * Cb
    else:
        b_rows, b_cols = b.shape
    if mode == "nn":
        M = M or a.shape[0]; K = K or a.shape[1]; N = N or b_cols
    elif mode == "nt":
        M = M or a.shape[0]; K = K or a.shape[1]; N = N or b_rows
    else:
        K = K or a.shape[0]; M = M or a.shape[1]; N = N or b_cols
    tm = _tile(M, tm, LANES if mode == "tn" else 16)
    tn = _tile(Cb if (b_blocked and mode != "nt") else (N // out_blocks if out_blocks else N), tn)
    tk = _tile(Cb if (b_blocked and mode == "nt") else K, tk)
    nk = K // tk
    dn = {"nn": NN, "nt": NT, "tn": TN}[mode]

    def off(o, t):
        assert o % t == 0, (name, o, t)
        return o // t

    if mode == "tn":
        ar, ac = off(a_off[0], tk), off(a_off[1], tm)
        a_spec = pl.BlockSpec((tk, tm), lambda i, j, k: (k + ar, i + ac))
    else:
        ar, ac = off(a_off[0], tm), off(a_off[1], tk)
        a_spec = pl.BlockSpec((tm, tk), lambda i, j, k: (i + ar, k + ac))
    if b_blocked and mode == "nt":
        per = Cb // tk
        b_spec = pl.BlockSpec((None, tn, tk), lambda i, j, k: (k // per, j, k % per))
    elif b_blocked:
        per = Cb // tn
        b_spec = pl.BlockSpec((None, tk, tn), lambda i, j, k: (j // per, k, j % per))
    elif mode == "nt":
        br, bc = off(b_off[0], tn), off(b_off[1], tk)
        b_spec = pl.BlockSpec((tn, tk), lambda i, j, k: (j + br, k + bc))
    else:
        br, bc = off(b_off[0], tk), off(b_off[1], tn)
        b_spec = pl.BlockSpec((tk, tn), lambda i, j, k: (k + br, j + bc))
    if out_blocks:
        per_o = N // out_blocks // tn
        o_spec = pl.BlockSpec((None, tm, tn), lambda i, j, k: (j // per_o, i, j % per_o))
        out_shape = jax.ShapeDtypeStruct((out_blocks, M, N // out_blocks), out_dtype)
    else:
        o_spec = pl.BlockSpec((tm, tn), lambda i, j, k: (i, j))
        out_shape = jax.ShapeDtypeStruct((M, N), out_dtype)
    in_specs = [a_spec, b_spec]
    args = [a, b]
    if add is not None:
        assert not out_blocks
        in_specs.append(o_spec)
        args.append(add)
    if after is not None:
        in_specs.append(pl.BlockSpec(memory_space=pl.ANY))
        args.append(after)
    n_in = len(args)

    def finish(r, refs):
        if add is not None:
            r = r + refs[2][...]
        o_ref = refs[n_in]
        o_ref[...] = r.astype(o_ref.dtype)

    def body_one(*refs):
        finish(_dot(refs[0][...].astype(BF16), refs[1][...].astype(BF16), dn), refs)

    def body_acc(*refs):
        acc_ref = refs[-1]
        k = pl.program_id(2)

        @pl.when(k == 0)
        def _():
            acc_ref[...] = jnp.zeros_like(acc_ref)

        acc_ref[...] += _dot(refs[0][...].astype(BF16), refs[1][...].astype(BF16), dn)

        @pl.when(k == nk - 1)
        def _():
            finish(acc_ref[...], refs)

    return _pcall(body_one if nk == 1 else body_acc, name=name, out_shape=out_shape,
                  grid=(M // tm, N // tn, nk), in_specs=in_specs, out_specs=o_spec,
                  scratch_shapes=[] if nk == 1 else [pltpu.VMEM((tm, tn), F32)],
                  semantics=("parallel", "parallel", "arbitrary"))(*args)


def _rms_fwd(x, g, *, name):
    T, D = x.shape
    tr = _tile(T, 128, 16)

    def body(x_ref, g_ref, h_ref, r_ref):
        xv = x_ref[...]
        r = lax.rsqrt(jnp.mean(xv * xv, axis=-1, keepdims=True) + EPS)
        h_ref[...] = (xv * r * g_ref[...]).astype(h_ref.dtype)
        r_ref[...] = r

    row = pl.BlockSpec((tr, D), lambda i: (i, 0))
    return _pcall(body, name=name,
                  out_shape=(jax.ShapeDtypeStruct((T, D), BF16), jax.ShapeDtypeStruct((T, 1), F32)),
                  grid=(T // tr,),
                  in_specs=[row, pl.BlockSpec((1, D), lambda i: (0, 0))],
                  out_specs=(row, pl.BlockSpec((tr, 1), lambda i: (i, 0))),
                  semantics=("parallel",))(x, g)


def _rms_bwd(dh, x, rstd, g, dres, *, name):
    T, D = x.shape
    tr = _tile(T, 128, 16)
    has_res = dres is not None

    def body(*refs):
        if has_res:
            dh_ref, x_ref, r_ref, g_ref, res_ref, dx_ref, dxb_ref, dg_ref = refs
        else:
            dh_ref, x_ref, r_ref, g_ref, dx_ref, dxb_ref, dg_ref = refs
        r = r_ref[...]
        xh = x_ref[...] * r
        dhv = dh_ref[...].astype(F32)
        dxh = dhv * g_ref[...]
        m = jnp.mean(dxh * xh, axis=-1, keepdims=True)
        dx = r * (dxh - xh * m)
        if has_res:
            dx = dx + res_ref[...]
        dx_ref[...] = dx
        dxb_ref[...] = dx.astype(BF16)

        @pl.when(pl.program_id(0) == 0)
        def _():
            dg_ref[...] = jnp.zeros_like(dg_ref)

        dg_ref[...] += jnp.sum(dhv * xh, axis=0, keepdims=True)

    row = pl.BlockSpec((tr, D), lambda i: (i, 0))
    vec = pl.BlockSpec((1, D), lambda i: (0, 0))
    in_specs = [row, row, pl.BlockSpec((tr, 1), lambda i: (i, 0)), vec]
    args = [dh, x, rstd, g]
    if has_res:
        in_specs.append(row)
        args.append(dres)
    return _pcall(body, name=name,
                  out_shape=(jax.ShapeDtypeStruct((T, D), F32), jax.ShapeDtypeStruct((T, D), BF16),
                             jax.ShapeDtypeStruct((1, D), F32)),
                  grid=(T // tr,), in_specs=in_specs, out_specs=(row, row, vec),
                  semantics=("arbitrary",))(*args)


def _loss_head(x3, g, tgt):
    T, D = x3.shape
    tr = _tile(T, 128, 16)

    def body(x_ref, g_ref, t_ref, loss_ref, dx_ref, dxb_ref, dg_ref):
        xv = x_ref[...]
        gv = g_ref[...]
        r = lax.rsqrt(jnp.mean(xv * xv, axis=-1, keepdims=True) + EPS)
        xh = xv * r
        err = xh * gv - t_ref[...]
        dy = err * (1.0 / D)
        dxh = dy * gv
        m = jnp.mean(dxh * xh, axis=-1, keepdims=True)
        dx = r * (dxh - xh * m)
        dx_ref[...] = dx
        dxb_ref[...] = dx.astype(BF16)

        @pl.when(pl.program_id(0) == 0)
        def _():
            dg_ref[...] = jnp.zeros_like(dg_ref)
            loss_ref[...] = jnp.zeros_like(loss_ref)

        dg_ref[...] += jnp.sum(dy * xh, axis=0, keepdims=True)
        part = 0.5 * jnp.sum(jnp.mean(err * err, axis=-1, keepdims=True), axis=0, keepdims=True)
        loss_ref[...] += jnp.broadcast_to(part, loss_ref.shape)

    row = pl.BlockSpec((tr, D), lambda i: (i, 0))
    vec = pl.BlockSpec((1, D), lambda i: (0, 0))
    return _pcall(body, name="loss_head",
                  out_shape=(jax.ShapeDtypeStruct((1, LANES), F32), jax.ShapeDtypeStruct((T, D), F32),
                             jax.ShapeDtypeStruct((T, D), BF16), jax.ShapeDtypeStruct((1, D), F32)),
                  grid=(T // tr,), in_specs=[row, vec, row],
                  out_specs=(pl.BlockSpec((1, LANES), lambda i: (0, 0)), row, row, vec),
                  semantics=("arbitrary",))(x3, g, tgt)


def _gla_chunk_terms(qk, a_ref, w2_ref, ba_ref, DK):
    C = qk.shape[0]
    gp = _dot(a_ref[...].astype(BF16), w2_ref[...]) + ba_ref[...]
    la = _log_sigmoid(gp) * (1.0 / GLA_GATE_NORM)
    row = lax.broadcasted_iota(jnp.int32, (C, C), 0)
    col = lax.broadcasted_iota(jnp.int32, (C, C), 1)
    causal = row >= col
    b = _dot(causal.astype(F32), la, precision=HIGHEST)
    return gp, b, causal


def _gla_fwd(proj, a_pad, w2, b_a, g_gla, *, T, DK, DV):
    assert 2 * DK == DV
    H = GLA_HEADS
    HK, HV = DK // H, DV // H
    C = GLA_CHUNK
    n = T // C
    RP = a_pad.shape[1]
    scale = HK ** -0.5

    def body(qk_ref, v_ref, r_ref, a_ref, w2_ref, ba_ref, gg_ref, og_ref, oraw_ref, st_ref, s_ref):
        @pl.when(pl.program_id(0) == 0)
        def _():
            s_ref[...] = jnp.zeros_like(s_ref)

        st_ref[...] = s_ref[...]
        qk = qk_ref[...]
        _, b, causal = _gla_chunk_terms(qk, a_ref, w2_ref, ba_ref, DK)
        for h in range(H):
            ks = slice(h * HK, (h + 1) * HK)
            vs = slice(h * HV, (h + 1) * HV)
            bh = b[:, ks]
            b_last = bh[C - 1:C, :]
            qt = qk[:, ks] * scale * jnp.exp(bh)
            kh = qk[:, DK + h * HK:DK + (h + 1) * HK]
            kt = kh * jnp.exp(-bh)
            khat = kh * jnp.exp(b_last - bh)
            a_mat = jnp.where(causal, _dot(qt, kt, NT, ONE_PASS), 0.0)
            vh = v_ref[:, vs]
            s_t = s_ref[h]
            o = _dot(a_mat, vh, NN, ONE_PASS) + _dot(qt, s_t, NT, ONE_PASS)
            s_ref[h] = s_t * jnp.exp(b_last) + _dot(vh, khat, TN, ONE_PASS)
            rs = lax.rsqrt(jnp.mean(o * o, axis=-1, keepdims=True) + EPS)
            rr = r_ref[:, vs]
            og = o * rs * gg_ref[:, vs] * (rr * _sigmoid(rr))
            oraw_ref[:, vs] = o
            og_ref[:, vs] = og.astype(BF16)

    blk = lambda j: pl.BlockSpec((C, DV), lambda i: (i, j))
    full = lambda s: pl.BlockSpec(s, lambda i: (0,) * len(s))
    return _pcall(
        body, name="gla_fwd",
        out_shape=(jax.ShapeDtypeStruct((T, DV), BF16), jax.ShapeDtypeStruct((T, DV), F32),
                   jax.ShapeDtypeStruct((n, H, HV, HK), F32)),
        grid=(n,),
        in_specs=[blk(0), blk(1), blk(2), pl.BlockSpec((C, RP), lambda i: (i, 0)),
                  full((RP, DK)), full((1, DK)), full((1, DV))],
        out_specs=(blk(0), blk(0), pl.BlockSpec((None, H, HV, HK), lambda i: (i, 0, 0, 0))),
        scratch_shapes=[pltpu.VMEM((H, HV, HK), F32)],
        semantics=("arbitrary",))(proj, proj, proj, a_pad, w2, b_a, g_gla)


def _gla_bwd(proj, a_pad, w2, b_a, g_gla, o_raw, states, do_gla, *, T, DK, DV):
    H = GLA_HEADS
    HK, HV = DK // H, DV // H
    C = GLA_CHUNK
    n = T // C
    RP = a_pad.shape[1]
    scale = HK ** -0.5

    def body(qk_ref, v_ref, r_ref, a_ref, w2_ref, ba_ref, gg_ref, oraw_ref, st_ref, dog_ref,
             dqkvr_ref, da_ref, dw2_ref, dba_ref, dgg_ref, ds_ref):
        @pl.when(pl.program_id(0) == 0)
        def _():
            ds_ref[...] = jnp.zeros_like(ds_ref)
            dw2_ref[...] = jnp.zeros_like(dw2_ref)
            dba_ref[...] = jnp.zeros_like(dba_ref)
            dgg_ref[...] = jnp.zeros_like(dgg_ref)

        qk = qk_ref[...]
        gp, b, causal = _gla_chunk_terms(qk, a_ref, w2_ref, ba_ref, DK)
        row = lax.broadcasted_iota(jnp.int32, (C, C), 0)
        col = lax.broadcasted_iota(jnp.int32, (C, C), 1)
        upper = (col >= row).astype(F32)
        dla_parts = []
        for h in range(H):
            ks = slice(h * HK, (h + 1) * HK)
            vs = slice(h * HV, (h + 1) * HV)
            bh = b[:, ks]
            b_last = bh[C - 1:C, :]
            eb = jnp.exp(bh)
            emb = jnp.exp(-bh)
            ehat = jnp.exp(b_last - bh)
            e_last = jnp.exp(b_last)
            qt = qk[:, ks] * scale * eb
            kh = qk[:, DK + h * HK:DK + (h + 1) * HK]
            kt = kh * emb
            khat = kh * ehat
            a_mat = jnp.where(causal, _dot(qt, kt, NT, ONE_PASS), 0.0)
            vh = v_ref[:, vs]
            o = oraw_ref[:, vs]
            rs = lax.rsqrt(jnp.mean(o * o, axis=-1, keepdims=True) + EPS)
            on = o * rs
            gg = gg_ref[:, vs]
            rr = r_ref[:, vs]
            sg = _sigmoid(rr)
            d_out = dog_ref[:, vs]
            dr = d_out * (on * gg) * (sg * (1.0 + rr * (1.0 - sg)))
            d_og = d_out * (rr * sg)
            dgg_ref[:, vs] += jnp.sum(d_og * on, axis=0, keepdims=True)
            d_on = d_og * gg
            d_o = rs * (d_on - on * jnp.mean(d_on * on, axis=-1, keepdims=True))
            s_t = st_ref[h]
            ds_t = ds_ref[h]
            d_a = jnp.where(causal, _dot(d_o, vh, NT, ONE_PASS), 0.0)
            dv = _dot(a_mat, d_o, TN, ONE_PASS) + _dot(khat, ds_t, NT, ONE_PASS)
            dqt = _dot(d_a, kt, NN, ONE_PASS) + _dot(d_o, s_t, NN, ONE_PASS)
            dkt = _dot(d_a, qt, TN, ONE_PASS)
            dkhat = _dot(vh, ds_t, NN, ONE_PASS)
            ds_ref[h] = ds_t * e_last + _dot(d_o, qt, TN, ONE_PASS)
            dq = dqt * eb * scale
            dk = dkt * emb + dkhat * ehat
            db = dqt * qt - dkt * kt - dkhat * khat
            d_last = (jnp.sum(dkhat * khat, axis=0, keepdims=True)
                      + e_last * jnp.sum(ds_t * s_t, axis=0, keepdims=True))
            dla_parts.append(_dot(upper, db, NN, HIGHEST) + d_last)
            dqkvr_ref[:, ks] = dq.astype(BF16)
            dqkvr_ref[:, DK + h * HK:DK + (h + 1) * HK] = dk.astype(BF16)
            dqkvr_ref[:, DV + h * HV:DV + (h + 1) * HV] = dv.astype(BF16)
            dqkvr_ref[:, 2 * DV + h * HV:2 * DV + (h + 1) * HV] = dr.astype(BF16)
        dla = jnp.concatenate(dla_parts, axis=1)
        dgp = dla * (1.0 / GLA_GATE_NORM) * _sigmoid(-gp)
        dba_ref[...] += jnp.sum(dgp, axis=0, keepdims=True)
        dgp_b = dgp.astype(BF16)
        dw2_ref[...] += _dot(a_ref[...].astype(BF16), dgp_b, TN)
        da_ref[...] = _dot(dgp_b, w2_ref[...], NT).astype(BF16)

    rev = lambda j: pl.BlockSpec((C, DV), lambda i: (n - 1 - i, j))
    full = lambda s: pl.BlockSpec(s, lambda i: (0,) * len(s))
    return _pcall(
        body, name="gla_bwd",
        out_shape=(jax.ShapeDtypeStruct((T, 3 * DV), BF16), jax.ShapeDtypeStruct((T, RP), BF16),
                   jax.ShapeDtypeStruct((RP, DK), F32), jax.ShapeDtypeStruct((1, DK), F32),
                   jax.ShapeDtypeStruct((1, DV), F32)),
        grid=(n,),
        in_specs=[rev(0), rev(1), rev(2), pl.BlockSpec((C, RP), lambda i: (n - 1 - i, 0)),
                  full((RP, DK)), full((1, DK)), full((1, DV)), rev(0),
                  pl.BlockSpec((None, H, HV, HK), lambda i: (n - 1 - i, 0, 0, 0)), rev(0)],
        out_specs=(pl.BlockSpec((C, 3 * DV), lambda i: (n - 1 - i, 0)),
                   pl.BlockSpec((C, RP), lambda i: (n - 1 - i, 0)),
                   full((RP, DK)), full((1, DK)), full((1, DV))),
        scratch_shapes=[pltpu.VMEM((H, HV, HK), F32)],
        semantics=("arbitrary",))(proj, proj, proj, a_pad, w2, b_a, g_gla, o_raw, states, do_gla)


def _pool_windows(p, g, T):
    t = lax.broadcasted_iota(jnp.int32, (T, 1), 0)
    s = p
    for lvl in range(POOL_GROUPS):
        sh = 1 << lvl
        nxt = s + jnp.where(t >= sh, pltpu.roll(s, sh, 0), 0.0)
        s = jnp.where(lvl <= g, nxt, s)
    win = jnp.left_shift(2, g)
    inv = 1.0 / jnp.minimum(t + 1, win).astype(F32)
    return s * inv - p, inv


def _pool_fwd(proj, w_pool, scale, *, T, PW, col_block):
    GW = PW // POOL_GROUPS
    per = PW // GW

    def body(p_ref, w_ref, s_ref, o_ref):
        g = pl.program_id(0)
        pooled, _ = _pool_windows(p_ref[...], g, T)
        mixed = _dot(pooled.astype(BF16), w_ref[...])
        o_ref[...] = (mixed * s_ref[...]).astype(BF16)

    return _pcall(body, name="pool_fwd", out_shape=jax.ShapeDtypeStruct((T, PW), BF16),
                  grid=(POOL_GROUPS,),
                  in_specs=[pl.BlockSpec((T, GW), lambda g: (0, col_block * per + g)),
                            pl.BlockSpec((None, GW, GW), lambda g: (g, 0, 0)),
                            pl.BlockSpec((1, GW), lambda g: (0, g))],
                  out_specs=pl.BlockSpec((T, GW), lambda g: (0, g)),
                  semantics=("parallel",))(proj, w_pool, scale)


def _pool_bwd(proj, w_pool, scale, do_pool, *, T, PW, col_block):
    GW = PW // POOL_GROUPS
    per = PW // GW

    def body(p_ref, w_ref, s_ref, do_ref, dp_ref, dw_ref, dsc_ref):
        g = pl.program_id(0)
        pooled, inv = _pool_windows(p_ref[...], g, T)
        pooled_b = pooled.astype(BF16)
        w = w_ref[...]
        mixed = _dot(pooled_b, w)
        d_out = do_ref[...]
        dsc_ref[...] = jnp.sum(d_out * mixed, axis=0, keepdims=True)
        dmixed = (d_out * s_ref[...]).astype(BF16)
        dw_ref[...] = _dot(pooled_b, dmixed, TN)
        dpooled = _dot(dmixed, w, NT)
        t = lax.broadcasted_iota(jnp.int32, (T, 1), 0)
        s = dpooled * inv
        for lvl in range(POOL_GROUPS):
            sh = 1 << lvl
            nxt = s + jnp.where(t < T - sh, pltpu.roll(s, T - sh, 0), 0.0)
            s = jnp.where(lvl <= g, nxt, s)
        dp_ref[...] = (s - dpooled).astype(BF16)

    return _pcall(body, name="pool_bwd",
                  out_shape=(jax.ShapeDtypeStruct((T, PW), BF16),
                             jax.ShapeDtypeStruct((POOL_GROUPS, GW, GW), F32),
                             jax.ShapeDtypeStruct((1, PW), F32)),
                  grid=(POOL_GROUPS,),
                  in_specs=[pl.BlockSpec((T, GW), lambda g: (0, col_block * per + g)),
                            pl.BlockSpec((None, GW, GW), lambda g: (g, 0, 0)),
                            pl.BlockSpec((1, GW), lambda g: (0, g)),
                            pl.BlockSpec((T, GW), lambda g: (0, g))],
                  out_specs=(pl.BlockSpec((T, GW), lambda g: (0, g)),
                             pl.BlockSpec((None, GW, GW), lambda g: (g, 0, 0)),
                             pl.BlockSpec((1, GW), lambda g: (0, g))),
                  semantics=("parallel",))(proj, w_pool, scale, do_pool)


def _merge_fwd(y_gla, y_pool, proj, *, T, D, col_block):
    tr = _tile(T, 128, 16)

    def body(yg_ref, yp_ref, g1_ref, g2_ref, o_ref):
        o_ref[...] = (_sigmoid(g1_ref[...]) * yg_ref[...]
                      + _sigmoid(g2_ref[...]) * yp_ref[...]).astype(BF16)

    row = pl.BlockSpec((tr, D), lambda i: (i, 0))
    return _pcall(body, name="merge_fwd", out_shape=jax.ShapeDtypeStruct((T, D), BF16),
                  grid=(T // tr,),
                  in_specs=[row, row, pl.BlockSpec((tr, D), lambda i: (i, col_block)),
                            pl.BlockSpec((tr, D), lambda i: (i, col_block + 1))],
                  out_specs=row, semantics=("parallel",))(y_gla, y_pool, proj, proj)


def _merge_bwd(dmerged, y_gla, y_pool, proj, *, T, D, col_block):
    tr = _tile(T, 128, 16)

    def body(dm_ref, yg_ref, yp_ref, g1_ref, g2_ref, dyg_ref, dyp_ref, dg_ref):
        dm = dm_ref[...]
        s1 = _sigmoid(g1_ref[...])
        s2 = _sigmoid(g2_ref[...])
        dyg_ref[...] = (dm * s1).astype(BF16)
        dyp_ref[...] = (dm * s2).astype(BF16)
        dg_ref[:, :D] = (dm * yg_ref[...] * s1 * (1.0 - s1)).astype(BF16)
        dg_ref[:, D:] = (dm * yp_ref[...] * s2 * (1.0 - s2)).astype(BF16)

    row = pl.BlockSpec((tr, D), lambda i: (i, 0))
    return _pcall(body, name="merge_bwd",
                  out_shape=(jax.ShapeDtypeStruct((T, D), BF16), jax.ShapeDtypeStruct((T, D), BF16),
                             jax.ShapeDtypeStruct((T, 2 * D), BF16)),
                  grid=(T // tr,),
                  in_specs=[row, row, row, pl.BlockSpec((tr, D), lambda i: (i, col_block)),
                            pl.BlockSpec((tr, D), lambda i: (i, col_block + 1))],
                  out_specs=(row, row, pl.BlockSpec((tr, 2 * D), lambda i: (i, 0))),
                  semantics=("parallel",))(dmerged, y_gla, y_pool, proj, proj)


def _attn_fwd(q, kv, *, T, D, M):
    H = CROSS_HEADS
    HD = D // H
    tq = _tile(T, 512, 16)
    scale = HD ** -0.5

    def body(q_ref, kv_ref, o_ref):
        for h in range(H):
            hs = slice(h * HD, (h + 1) * HD)
            s = _dot(q_ref[:, hs], kv_ref[:, hs], NT) * scale
            e = jnp.exp(s - jnp.max(s, axis=-1, keepdims=True))
            p = e / jnp.sum(e, axis=-1, keepdims=True)
            o_ref[:, hs] = _dot(p.astype(BF16), kv_ref[:, D + h * HD:D + (h + 1) * HD]).astype(BF16)

    row = pl.BlockSpec((tq, D), lambda i: (i, 0))
    return _pcall(body, name="attn_fwd", out_shape=jax.ShapeDtypeStruct((T, D), BF16),
                  grid=(T // tq,), in_specs=[row, pl.BlockSpec((M, 2 * D), lambda i: (0, 0))],
                  out_specs=row, semantics=("parallel",))(q, kv)


def _attn_bwd(q, kv, do, *, T, D, M):
    H = CROSS_HEADS
    HD = D // H
    tq = _tile(T, 512, 16)
    scale = HD ** -0.5

    def body(q_ref, kv_ref, do_ref, dq_ref, dkv_ref):
        @pl.when(pl.program_id(0) == 0)
        def _():
            dkv_ref[...] = jnp.zeros_like(dkv_ref)

        for h in range(H):
            hs = slice(h * HD, (h + 1) * HD)
            vs = slice(D + h * HD, D + (h + 1) * HD)
            qh = q_ref[:, hs]
            kh = kv_ref[:, hs]
            s = _dot(qh, kh, NT) * scale
            e = jnp.exp(s - jnp.max(s, axis=-1, keepdims=True))
            p = e / jnp.sum(e, axis=-1, keepdims=True)
            p_b = p.astype(BF16)
            d_o = do_ref[:, hs]
            dkv_ref[:, vs] += _dot(p_b, d_o, TN)
            dp = _dot(d_o, kv_ref[:, vs], NT)
            ds = (p * (dp - jnp.sum(dp * p, axis=-1, keepdims=True)) * scale).astype(BF16)
            dq_ref[:, hs] = _dot(ds, kh).astype(BF16)
            dkv_ref[:, hs] += _dot(ds, qh, TN)

    row = pl.BlockSpec((tq, D), lambda i: (i, 0))
    full = pl.BlockSpec((M, 2 * D), lambda i: (0, 0))
    return _pcall(body, name="attn_bwd",
                  out_shape=(jax.ShapeDtypeStruct((T, D), BF16), jax.ShapeDtypeStruct((M, 2 * D), F32)),
                  grid=(T // tq,), in_specs=[row, full, row], out_specs=(row, full),
                  semantics=("arbitrary",))(q, kv, do)


def _shift_down(x, halo, s):
    out = pltpu.roll(x, s, 0)
    t8 = lax.broadcasted_iota(jnp.int32, (SUBLANES, 1), 0)
    head = out[:SUBLANES]
    for j in range(s):
        head = jnp.where(t8 == j, halo[SUBLANES - s + j:SUBLANES - s + j + 1, :], head)
    return head if x.shape[0] == SUBLANES else jnp.concatenate([head, out[SUBLANES:]], axis=0)


def _shift_up(x, halo, s):
    rows = x.shape[0]
    out = pltpu.roll(x, rows - s, 0)
    t8 = lax.broadcasted_iota(jnp.int32, (SUBLANES, 1), 0)
    tail = out[rows - SUBLANES:]
    for j in range(s):
        tail = jnp.where(t8 == SUBLANES - s + j, halo[j:j + 1, :], tail)
    return jnp.concatenate([out[:rows - SUBLANES], tail], axis=0)


def _conv_tiles(T):
    tt = _tile(T, 128, SUBLANES)
    return tt, tt // SUBLANES, T // SUBLANES


def _conv_fwd(u0, conv_w, conv_b, *, T, F):
    tt, hb, _ = _conv_tiles(T)
    cw = _tile(F, LANES)

    def body(u_ref, prev_ref, w_ref, b_ref, f_ref):
        i = pl.program_id(0)

        def conv(cs):
            x = u_ref[:, cs]
            halo = jnp.where(i > 0, prev_ref[:, cs], 0.0)
            return (w_ref[2:3, cs] * x + w_ref[1:2, cs] * _shift_down(x, halo, 1)
                    + w_ref[0:1, cs] * _shift_down(x, halo, 2) + b_ref[:, cs])

        for j in range(F // cw):
            gate = conv(slice(j * cw, (j + 1) * cw))
            val = conv(slice(F + j * cw, F + (j + 1) * cw))
            f_ref[:, j * cw:(j + 1) * cw] = (gate * _sigmoid(gate) * val).astype(BF16)

    return _pcall(body, name="conv_fwd", out_shape=jax.ShapeDtypeStruct((T, F), BF16),
                  grid=(T // tt,),
                  in_specs=[pl.BlockSpec((tt, 2 * F), lambda i: (i, 0)),
                            pl.BlockSpec((SUBLANES, 2 * F), lambda i: (jnp.maximum(i * hb - 1, 0), 0)),
                            pl.BlockSpec((CONV_W, 2 * F), lambda i: (0, 0)),
                            pl.BlockSpec((1, 2 * F), lambda i: (0, 0))],
                  out_specs=pl.BlockSpec((tt, F), lambda i: (i, 0)),
                  semantics=("parallel",))(u0, u0, conv_w, conv_b)


def _conv_bwd(u0, conv_w, conv_b, df, *, T, F):
    tt, hb, nb = _conv_tiles(T)
    nt = T // tt
    cw = _tile(F, LANES)

    def body(u_ref, prev_ref, next_ref, df_ref, dfn_ref, w_ref, b_ref, du0_ref, dw_ref, db_ref):
        i = pl.program_id(0)

        @pl.when(i == 0)
        def _():
            dw_ref[...] = jnp.zeros_like(dw_ref)
            db_ref[...] = jnp.zeros_like(db_ref)

        def conv(cs):
            x = u_ref[:, cs]
            halo = jnp.where(i > 0, prev_ref[:, cs], 0.0)
            x1 = _shift_down(x, halo, 1)
            x2 = _shift_down(x, halo, 2)
            u = w_ref[2:3, cs] * x + w_ref[1:2, cs] * x1 + w_ref[0:1, cs] * x2 + b_ref[:, cs]
            xn = next_ref[:, cs]
            tail = x[tt - SUBLANES:, :]
            un = (w_ref[2:3, cs] * xn + w_ref[1:2, cs] * _shift_down(xn, tail, 1)
                  + w_ref[0:1, cs] * _shift_down(xn, tail, 2) + b_ref[:, cs])
            return u, un, (x, x1, x2)

        def glu_grad(gate, val, dff):
            sg = _sigmoid(gate)
            return dff * val * (sg * (1.0 + gate * (1.0 - sg))), dff * (gate * sg)

        def finish(cs, du, dun, xs):
            du0 = (w_ref[2:3, cs] * du + w_ref[1:2, cs] * _shift_up(du, dun, 1)
                   + w_ref[0:1, cs] * _shift_up(du, dun, 2))
            du0_ref[:, cs] = du0.astype(BF16)
            db_ref[:, cs] += jnp.sum(du, axis=0, keepdims=True)
            dw_ref[2:3, cs] += jnp.sum(du * xs[0], axis=0, keepdims=True)
            dw_ref[1:2, cs] += jnp.sum(du * xs[1], axis=0, keepdims=True)
            dw_ref[0:1, cs] += jnp.sum(du * xs[2], axis=0, keepdims=True)

        for j in range(F // cw):
            fs = slice(j * cw, (j + 1) * cw)
            gs, vs = fs, slice(F + j * cw, F + (j + 1) * cw)
            ug, ung, xg = conv(gs)
            uv, unv, xv = conv(vs)
            dug, duv = glu_grad(ug, uv, df_ref[:, fs].astype(F32))
            dung, dunv = glu_grad(ung, unv, dfn_ref[0:SUBLANES, fs].astype(F32))
            dung = jnp.where(i < nt - 1, dung, 0.0)
            dunv = jnp.where(i < nt - 1, dunv, 0.0)
            finish(gs, dug, dung, xg)
            finish(vs, duv, dunv, xv)

    wide = lambda rows, fn: pl.BlockSpec((rows, 2 * F), fn)
    nxt = lambda i: (jnp.minimum((i + 1) * hb, nb - 1), 0)
    return _pcall(body, name="conv_bwd",
                  out_shape=(jax.ShapeDtypeStruct((T, 2 * F), BF16),
                             jax.ShapeDtypeStruct((CONV_W, 2 * F), F32),
                             jax.ShapeDtypeStruct((1, 2 * F), F32)),
                  grid=(nt,),
                  in_specs=[wide(tt, lambda i: (i, 0)),
                            wide(SUBLANES, lambda i: (jnp.maximum(i * hb - 1, 0), 0)),
                            wide(SUBLANES, nxt),
                            pl.BlockSpec((tt, F), lambda i: (i, 0)),
                            pl.BlockSpec((2 * SUBLANES, F),
                                         lambda i: (jnp.minimum((i + 1) * (hb // 2), nb // 2 - 1), 0)),
                            wide(CONV_W, lambda i: (0, 0)), wide(1, lambda i: (0, 0))],
                  out_specs=(wide(tt, lambda i: (i, 0)), wide(CONV_W, lambda i: (0, 0)),
                             wide(1, lambda i: (0, 0))),
                  semantics=("arbitrary",))(u0, u0, u0, df, df, conv_w, conv_b)


def _adamw(w, g, m, v, *, name):
    R, C = w.shape
    tr = _tile(R, max(SUBLANES, (1 << 19) // max(C, 1) // SUBLANES * SUBLANES), SUBLANES)
    c1 = 1.0 / (1.0 - ADAM_B1 ** ADAM_STEP)
    c2 = 1.0 / (1.0 - ADAM_B2 ** ADAM_STEP)

    def body(w_ref, g_ref, m_ref, v_ref, d_ref, mo_ref, vo_ref):
        gv = g_ref[...]
        mn = ADAM_B1 * m_ref[...] + (1.0 - ADAM_B1) * gv
        vn = ADAM_B2 * v_ref[...] + (1.0 - ADAM_B2) * (gv * gv)
        d_ref[...] = -ADAM_LR * ((mn * c1) / (jnp.sqrt(vn * c2) + ADAM_EPS) + ADAM_WD * w_ref[...])
        mo_ref[...] = mn
        vo_ref[...] = vn

    blk = pl.BlockSpec((tr, C), lambda i: (i, 0))
    shp = jax.ShapeDtypeStruct((R, C), F32)
    return _pcall(body, name=name, out_shape=(shp, shp, shp), grid=(R // tr,),
                  in_specs=[blk] * 4, out_specs=(blk,) * 3, semantics=("parallel",))(w, g, m, v)


def _blk(h, C, elems=1 << 19, align=16):
    th = _tile(h, max(align, elems // C // align * align), align)
    if th < h or h * C <= 2 * elems:
        return th, C
    return h, _tile(C, max(LANES, elems // h // LANES * LANES))


def _adamw_halves(w, m, v, g_mine, g_other, c_idx, *, name):
    _, h, C = w.shape
    th, tc = _blk(h, C, align=SUBLANES)
    c1 = 1.0 / (1.0 - ADAM_B1 ** ADAM_STEP)
    c2 = 1.0 / (1.0 - ADAM_B2 ** ADAM_STEP)

    def body(c_ref, w_ref, m_ref, v_ref, gm_ref, go_ref, g_ref, d_ref, mo_ref, vo_ref):
        gv = jnp.where(pl.program_id(0) == c_ref[0], gm_ref[...], go_ref[...])
        mn = ADAM_B1 * m_ref[...] + (1.0 - ADAM_B1) * gv
        vn = ADAM_B2 * v_ref[...] + (1.0 - ADAM_B2) * (gv * gv)
        d_ref[...] = -ADAM_LR * ((mn * c1) / (jnp.sqrt(vn * c2) + ADAM_EPS) + ADAM_WD * w_ref[...])
        g_ref[...] = gv
        mo_ref[...] = mn
        vo_ref[...] = vn

    blk = pl.BlockSpec((None, th, tc), lambda s, i, j, c: (s, i, j))

    def pick(mine):
        def index(s, i, j, c):
            use = (s == c[0]) if mine else (s != c[0])
            return jnp.where(use, i, 0), jnp.where(use, j, 0)
        return pl.BlockSpec((th, tc), index)

    shp = jax.ShapeDtypeStruct((2, h, C), F32)
    return _pcall(body, name=name, out_shape=(shp,) * 4, grid=(2, h // th, C // tc), prefetch=1,
                  in_specs=[blk, blk, blk, pick(True), pick(False)], out_specs=(blk,) * 4,
                  semantics=("parallel", "parallel", "parallel"))(c_idx, w, m, v, g_mine, g_other)


def _mesh_pos():
    x, y, c = lax.axis_index("x"), lax.axis_index("y"), lax.axis_index("c")
    others = [(1 - x, y), (x, 1 - y), (1 - x, 1 - y)]
    return x, y, c, others


def _gather_copies(shards, lands, send_sems, recv_sems):
    x, y, c, others = _mesh_pos()
    me = 2 * x + y
    return [pltpu.make_async_remote_copy(
        src_ref=shards[a].at[c], dst_ref=lands[a].at[me, c],
        send_sem=send_sems.at[3 * a + j], recv_sem=recv_sems.at[3 * a + j],
        device_id=(*chip, c), device_id_type=MESH)
        for a in range(len(shards)) for j, chip in enumerate(others)]


def _pass_copies(shards, zones, send_sems, recv_sems):
    x, y, c, others = _mesh_pos()
    me = 2 * x + y
    copies = []
    for a in range(len(shards)):
        srcs = [zones[a].at[2 * chip[0] + chip[1], c] for chip in others] + [shards[a]]
        dsts = [zones[a].at[2 * chip[0] + chip[1], c] for chip in others] + [zones[a].at[me]]
        copies += [pltpu.make_async_remote_copy(
            src_ref=s, dst_ref=d, send_sem=send_sems.at[4 * a + k], recv_sem=recv_sems.at[4 * a + k],
            device_id=(x, y, 1 - c), device_id_type=MESH) for k, (s, d) in enumerate(zip(srcs, dsts))]
    return copies


def _exchange_copies(grads, recvs, send_sems, recv_sems):
    x, y, c, _ = _mesh_pos()
    return [pltpu.make_async_remote_copy(
        src_ref=grads[a].at[:, 1 - c], dst_ref=recvs[a], send_sem=send_sems.at[a],
        recv_sem=recv_sems.at[a], device_id=(x, y, 1 - c), device_id_type=MESH) for a in range(len(grads))]


def _split_start(copies, per, srcs, zones, after, *, name):
    n = len(srcs)
    HBM = pl.BlockSpec(memory_space=pltpu.HBM)
    SEM = pl.BlockSpec(memory_space=pltpu.SEMAPHORE)

    def body(*refs):
        send_sems, recv_sems = refs[2 * n + 1], refs[2 * n + 2]
        for cp in copies(refs[:n], refs[n:2 * n], send_sems, recv_sems):
            cp.start()
        refs[-1][...] = jnp.zeros_like(refs[-1])

    hbm = lambda a: pltpu.HBM(a.shape, a.dtype)
    res = _pcall(body, name=name,
                 out_shape=(pltpu.SemaphoreType.DMA((per * n,)), pltpu.SemaphoreType.DMA((per * n,)),
                            *[hbm(a) for a in srcs], *[hbm(a) for a in zones],
                            jax.ShapeDtypeStruct((SUBLANES, LANES), F32)),
                 in_specs=[*[HBM] * (2 * n), pl.BlockSpec(memory_space=pl.ANY)],
                 out_specs=(SEM, SEM, *[HBM] * (2 * n), pl.BlockSpec(memory_space=pltpu.VMEM)),
                 aliases={i: 2 + i for i in range(2 * n)}, split_copy=True)(
        *[pltpu.with_memory_space_constraint(a, pltpu.HBM) for a in [*srcs, *zones]], after)
    return res[0], res[1], list(res[2:2 + n]), list(res[2 + n:2 + 2 * n]), res[-1]


def _split_wait(copies, send_sems, recv_sems, srcs, zones, after, *, name):
    n = len(srcs)
    HBM = pl.BlockSpec(memory_space=pltpu.HBM)
    SEM = pl.BlockSpec(memory_space=pltpu.SEMAPHORE)

    def body(*refs):
        for cp in copies(refs[:n], refs[n:2 * n], refs[2 * n], refs[2 * n + 1]):
            cp.wait_send()
            cp.wait_recv()

    hbm = lambda a: pltpu.HBM(a.shape, a.dtype)
    res = _pcall(body, name=name, out_shape=(*[hbm(a) for a in srcs], *[hbm(a) for a in zones]),
                 in_specs=[*[HBM] * (2 * n), SEM, SEM, pl.BlockSpec(memory_space=pl.ANY)],
                 out_specs=tuple([HBM] * (2 * n)), aliases={i: i for i in range(2 * n)},
                 split_copy=True)(*srcs, *zones, send_sems, recv_sems, after)
    return list(res[:n]), list(res[n:])


def _add_halves(grad, recv, c_idx, *, name):
    S, _, h, C = grad.shape
    th, tc = _blk(h, C)

    def body(c_ref, g_ref, r_ref, o_ref):
        o_ref[...] = (g_ref[...].astype(F32) + r_ref[...].astype(F32)).astype(o_ref.dtype)

    return _pcall(body, name=name, out_shape=jax.ShapeDtypeStruct((S, h, C), grad.dtype),
                  grid=(S, h // th, C // tc), prefetch=1,
                  in_specs=[pl.BlockSpec((None, None, th, tc), lambda s, i, j, c: (s, c[0], i, j)),
                            pl.BlockSpec((None, th, tc), lambda s, i, j, c: (s, i, j))],
                  out_specs=pl.BlockSpec((None, th, tc), lambda s, i, j, c: (s, i, j)),
                  semantics=("parallel", "parallel", "parallel"))(c_idx, grad, recv)


def _scatter_copies(srcs, lands, send_sems, recv_sems):
    x, y, c, others = _mesh_pos()
    return [pltpu.make_async_remote_copy(
        src_ref=srcs[a].at[2 * chip[0] + chip[1]], dst_ref=lands[a].at[j],
        send_sem=send_sems.at[3 * a + j], recv_sem=recv_sems.at[3 * a + j],
        device_id=(*chip, c), device_id_type=MESH)
        for a in range(len(srcs)) for j, chip in enumerate(others)]


def _add_chips(sums, recv, chip_idx, *, name):
    _, h, C = sums.shape
    th, tc = _blk(h, C)

    def body(k_ref, s_ref, r_ref, o_ref):
        acc = s_ref[...].astype(F32) + r_ref[0].astype(F32)
        acc = acc + r_ref[1].astype(F32)
        o_ref[...] = acc + r_ref[2].astype(F32)

    return _pcall(body, name=name, out_shape=jax.ShapeDtypeStruct((h, C), F32),
                  grid=(h // th, C // tc), prefetch=1,
                  in_specs=[pl.BlockSpec((None, th, tc), lambda i, j, k: (k[0], i, j)),
                            pl.BlockSpec((3, th, tc), lambda i, j, k: (0, i, j))],
                  out_specs=pl.BlockSpec((th, tc), lambda i, j, k: (i, j)),
                  semantics=("parallel", "parallel"))(chip_idx, sums, recv)


def _swap_copies(halves, others, send_sems, recv_sems):
    x, y, c, _ = _mesh_pos()
    return [pltpu.make_async_remote_copy(
        src_ref=halves[a], dst_ref=others[a], send_sem=send_sems.at[a], recv_sem=recv_sems.at[a],
        device_id=(x, y, 1 - c), device_id_type=MESH) for a in range(len(halves))]


def _all_reduce_small(buf):
    R, L = buf.shape
    NDEV = 8

    def body(x_ref, sum_ref, all_ref, send_sems, recv_sems, local_sem):
        x, y, c, others = _mesh_pos()
        me, sibling = (x, y, c), (x, y, 1 - c)

        def slot(px, py, pc):
            return all_ref.at[4 * px + 2 * py + pc]

        def copy(k, block, to, src=None):
            return pltpu.make_async_remote_copy(
                src_ref=slot(*block) if src is None else src, dst_ref=slot(*block),
                send_sem=send_sems.at[k], recv_sem=recv_sems.at[k], device_id=to, device_id_type=MESH)

        mine = pltpu.make_async_copy(x_ref, slot(*me), local_sem)
        mine.start()
        first = [copy(0, me, sibling, src=x_ref)]
        first += [copy(1 + j, me, (*chip, c), src=x_ref) for j, chip in enumerate(others)]
        for cp in first:
            cp.start()
        passed = [copy(4 + j, (*chip, c), sibling) for j, chip in enumerate(others)]
        for j, chip in enumerate(others):
            copy(1 + j, (*chip, c), me).wait_recv()
            passed[j].start()
        copy(0, sibling, me).wait_recv()
        for j, chip in enumerate(others):
            copy(4 + j, (*chip, 1 - c), me).wait_recv()
        for cp in first + passed:
            cp.wait_send()
        mine.wait()
        acc = all_ref[0]
        for d in range(1, NDEV):
            acc = acc + all_ref[d]
        sum_ref[...] = acc

    VM = pl.BlockSpec(memory_space=pltpu.VMEM)
    return _pcall(body, name="all_reduce_small",
                  out_shape=(jax.ShapeDtypeStruct((R, L), F32), jax.ShapeDtypeStruct((NDEV, R, L), F32)),
                  in_specs=[VM], out_specs=(VM, VM),
                  scratch_shapes=[pltpu.SemaphoreType.DMA((7,)), pltpu.SemaphoreType.DMA((7,)),
                                  pltpu.SemaphoreType.DMA])(buf)[0]


def _pack(arrs, rows_multiple=16):
    flat = [a.reshape(-1).astype(F32) for a in arrs]
    sizes = [f.shape[0] for f in flat]
    total = sum(sizes)
    per = LANES * rows_multiple
    padded = -(-total // per) * per
    flat.append(jnp.zeros((padded - total,), F32))
    offs = [0]
    for s in sizes:
        offs.append(offs[-1] + s)
    return jnp.concatenate(flat).reshape(padded // LANES, LANES), offs


def _unpack(buf, offs, shapes):
    flat = buf.reshape(-1)
    return [flat[offs[i]:offs[i + 1]].reshape(s) for i, s in enumerate(shapes)]


def kernel(x, mem, g_mix, w_in, w_a2, b_a, g_gla, w_pool, pool_scale, w_branch, w_out, g_cross, g_mem, w_cq, w_ckv, w_co, g_ffn, w_up, conv_w, conv_b, w_down, g_final, loss_target, m_g_mix, m_w_in, m_w_a2, m_b_a, m_g_gla, m_w_pool, m_pool_scale, m_w_branch, m_w_out, m_g_cross, m_g_mem, m_w_cq, m_w_ckv, m_w_co, m_g_ffn, m_w_up, m_conv_w, m_conv_b, m_w_down, m_g_final, v_g_mix, v_w_in, v_w_a2, v_b_a, v_g_gla, v_w_pool, v_pool_scale, v_w_branch, v_w_out, v_g_cross, v_g_mem, v_w_cq, v_w_ckv, v_w_co, v_g_ffn, v_w_up, v_conv_w, v_conv_b, v_w_down, v_g_final):
    weights = dict(g_mix=g_mix, w_in=w_in, w_a2=w_a2, b_a=b_a, g_gla=g_gla, w_pool=w_pool,
                   pool_scale=pool_scale, w_branch=w_branch, w_out=w_out, g_cross=g_cross, g_mem=g_mem,
                   w_cq=w_cq, w_ckv=w_ckv, w_co=w_co, g_ffn=g_ffn, w_up=w_up, conv_w=conv_w,
                   conv_b=conv_b, w_down=w_down, g_final=g_final)
    mom_m = dict(g_mix=m_g_mix, w_in=m_w_in, w_a2=m_w_a2, b_a=m_b_a, g_gla=m_g_gla, w_pool=m_w_pool,
                 pool_scale=m_pool_scale, w_branch=m_w_branch, w_out=m_w_out, g_cross=m_g_cross,
                 g_mem=m_g_mem, w_cq=m_w_cq, w_ckv=m_w_ckv, w_co=m_w_co, g_ffn=m_g_ffn, w_up=m_w_up,
                 conv_w=m_conv_w, conv_b=m_conv_b, w_down=m_w_down, g_final=m_g_final)
    mom_v = dict(g_mix=v_g_mix, w_in=v_w_in, w_a2=v_w_a2, b_a=v_b_a, g_gla=v_g_gla, w_pool=v_w_pool,
                 pool_scale=v_pool_scale, w_branch=v_w_branch, w_out=v_w_out, g_cross=v_g_cross,
                 g_mem=v_g_mem, w_cq=v_w_cq, w_ckv=v_w_ckv, w_co=v_w_co, g_ffn=v_g_ffn, w_up=v_w_up,
                 conv_w=v_conv_w, conv_b=v_conv_b, w_down=v_w_down, g_final=v_g_final)
    order = list(weights)
    big = ["w_in", "w_branch", "w_out", "w_cq", "w_ckv", "w_co", "w_up", "w_down"]
    small_sharded = ["w_a2", "w_pool", "conv_w"]
    small_repl = ["g_mix", "b_a", "g_gla", "pool_scale", "g_cross", "g_mem", "g_ffn", "conv_b", "g_final"]

    xs, ms, tgt = x[0], mem[0], loss_target[0]
    T, D = xs.shape
    M = ms.shape[0]
    DK, DV, PW = b_a.shape[1], g_gla.shape[1], pool_scale.shape[1]
    RANK = w_a2.shape[1]
    F2 = conv_b.shape[1]
    F = F2 // 2
    DIN = N_CHIPS * w_in.shape[2]
    OFF_A = 2 * DK + 2 * DV
    OFF_P = OFF_A + RANK
    RP = LANES
    GW = PW // POOL_GROUPS
    assert PW == DV and 4 * DV == 2 * D and OFF_P + PW + 2 * D == DIN

    cx, cy, cc = lax.axis_index("x"), lax.axis_index("y"), lax.axis_index("c")
    chip = 2 * cx + cy
    c_idx = jnp.reshape(cc, (1,)).astype(jnp.int32)
    chip_idx = jnp.reshape(chip, (1,)).astype(jnp.int32)

    def halves(a):
        return a.reshape(2, a.shape[0] // 2, a.shape[1])

    shard2d = {k: (weights[k][0].T if k == "w_in" else weights[k][0]) for k in big}
    small_pack, small_offs = _pack([weights[k][0] for k in small_sharded], rows_multiple=32)
    flying, passing = {}, {}
    tok = xs
    for group, keys in (("in", ["w_in"]), ("mix", ["w_branch", "w_out", "small"]),
                        ("cross", ["w_cq", "w_ckv", "w_co"]), ("up", ["w_up"]), ("down", ["w_down"])):
        srcs = [small_pack if k == "small" else shard2d[k].astype(BF16) for k in keys]
        if group != "in":
            srcs = [a + tok[0:1, 0:1].astype(a.dtype) for a in srcs]
        srcs = [halves(a) for a in srcs]
        zones = [lax.empty((N_CHIPS, *s.shape), s.dtype) for s in srcs]
        s_sems, r_sems, srcs, zones, tok = _split_start(_gather_copies, 3, srcs, zones, tok,
                                                        name=f"gather_start_{group}")
        flying[group] = (keys, s_sems, r_sems, srcs, zones)

    def landed(group, after):
        keys, s_sems, r_sems, srcs, zones = flying[group]
        srcs, zones = _split_wait(_gather_copies, s_sems, r_sems, srcs, zones, after,
                                  name=f"gather_wait_{group}")
        s_sems, r_sems, srcs, zones, token = _split_start(_pass_copies, 4, srcs, zones, after,
                                                          name=f"gather_pass_start_{group}")
        passing[group] = (keys, s_sems, r_sems, srcs, zones)
        return token

    def arrive(group, after):
        keys, s_sems, r_sems, srcs, zones = passing[group]
        _, full = _split_wait(_pass_copies, s_sems, r_sems, srcs, zones, after,
                              name=f"gather_pass_wait_{group}")
        return {k: f.reshape(N_CHIPS, f.shape[1] * f.shape[2], f.shape[3]) for k, f in zip(keys, full)}

    def rows(g):
        return g.reshape(-1, g.shape[2])

    h1, r1 = _rms_fwd(xs, g_mix + tok[0:1, 0:1], name="norm_mix")
    landed("in", h1)
    gw = arrive("in", h1)
    W_in = rows(gw["w_in"])
    W_main = jnp.concatenate([W_in[:OFF_A], W_in[OFF_P:]], axis=0)
    W_a = jnp.pad(W_in[OFF_A:OFF_P], ((0, RP - RANK), (0, 0)))
    tok = landed("mix", W_a)
    proj = _mm(h1, W_main, "nt", name="proj_main", out_dtype=F32, after=tok)
    gw = arrive("mix", proj)
    W_branch, W_out, small_all = rows(gw["w_branch"]), rows(gw["w_out"]), gw["small"]
    sm = [_unpack(small_all[j], small_offs, [weights[k].shape[1:] for k in small_sharded]) for j in range(N_CHIPS)]
    W_a2 = jnp.concatenate([sm[j][0] for j in range(N_CHIPS)], axis=1)
    W_a2p = jnp.pad(W_a2, ((0, RP - RANK), (0, 0))).astype(BF16)
    W_pool = jnp.concatenate([sm[j][1] for j in range(N_CHIPS)], axis=1).astype(BF16)
    W_conv = jnp.concatenate([sm[j][2] for j in range(N_CHIPS)], axis=1)

    a_pad = _mm(h1, W_a, "nt", name="proj_gate_rank", out_dtype=F32)
    o_gla, o_raw, states = _gla_fwd(proj, a_pad, W_a2p, b_a, g_gla, T=T, DK=DK, DV=DV)
    o_pool = _pool_fwd(proj, W_pool, pool_scale, T=T, PW=PW, col_block=3)
    tok = landed("cross", o_pool)
    y_gla = _mm(o_gla, W_branch, "nn", name="branch_gla", out_dtype=BF16, K=DV, after=tok)
    y_pool = _mm(o_pool, W_branch, "nn", name="branch_pool", out_dtype=BF16, K=PW, b_off=(DV, 0))
    merged = _merge_fwd(y_gla, y_pool, proj, T=T, D=D, col_block=2)
    x1 = _mm(merged, W_out, "nn", name="mix_out", out_dtype=F32, add=xs)

    h2, r2 = _rms_fwd(x1, g_cross, name="norm_cross")
    mem_n, rm = _rms_fwd(ms, g_mem, name="norm_mem")
    gw = arrive("cross", h2)
    W_cq, W_ckv, W_co = rows(gw["w_cq"]), gw["w_ckv"], rows(gw["w_co"])
    qc = _mm(h2, W_cq, "nn", name="cross_q", out_dtype=BF16)
    kv = _mm(mem_n, W_ckv, "nn", name="cross_kv", out_dtype=BF16, b_blocked=True)
    o_att = _attn_fwd(qc, kv, T=T, D=D, M=M)
    x2 = _mm(o_att, W_co, "nn", name="cross_out", out_dtype=F32, add=x1)

    tok = landed("up", x2)
    h3, r3 = _rms_fwd(x2, g_ffn + tok[0:1, 0:1], name="norm_ffn")
    W_up = arrive("up", h3)["w_up"]
    u0 = _mm(h3, W_up, "nn", name="ffn_up", out_dtype=F32, b_blocked=True)
    tok = landed("down", u0)
    f_act = _conv_fwd(u0, W_conv, conv_b + tok[0:1, 0:1], T=T, F=F)
    W_down = rows(arrive("down", f_act)["w_down"])
    x3 =_mm(f_act, W_down, "nn", name="ffn_down", out_dtype=F32, add=x2)

    loss_part, dx3, dx3_b, dg_final = _loss_head(x3, g_final.reshape(1, D), tgt)

    def col_shards(g):
        nb, K, Nb = g.shape
        return g.reshape(nb, 2, K // 2, Nb)

    def row_shards(g):
        R, N = g.shape
        return g.reshape(N_CHIPS, 2, R // N_CHIPS // 2, N)

    exchanging, in_flight = {}, []

    def exchange_start(group, keys, partials, after):
        recvs = [lax.empty((p.shape[0], *p.shape[2:]), p.dtype) for p in partials]
        s_sems, r_sems, partials, recvs, token = _split_start(
            _exchange_copies, 1, partials, recvs, after, name=f"grad_exchange_start_{group}")
        exchanging[group] = (keys, s_sems, r_sems, partials, recvs)
        return token

    def scatter_start(group, after):
        keys, s_sems, r_sems, partials, recvs = exchanging[group]
        partials, recvs = _split_wait(_exchange_copies, s_sems, r_sems, partials, recvs, after,
                                      name=f"grad_exchange_wait_{group}")
        chip_sums = [_add_halves(p, r, c_idx, name=f"grad_add_halves_{k}")
                     for k, p, r in zip(keys, partials, recvs)]
        lands = [lax.empty((3, *s.shape[1:]), s.dtype) for s in chip_sums]
        s_sems, r_sems, sums, lands, token = _split_start(
            _scatter_copies, 3, chip_sums, lands, after, name=f"grad_scatter_start_{group}")
        in_flight.append((group, keys, s_sems, r_sems, sums, lands))
        return token

    df = _mm(dx3_b, W_down, "nt", name="d_ffn_act", out_dtype=BF16)
    dW_down = _mm(f_act, dx3_b, "tn", name="dw_down", out_dtype=BF16)
    du0, dconv_w, dconv_b = _conv_bwd(u0, W_conv, conv_b, df, T=T, F=F)
    dh3 = _mm(du0, W_up, "nt", name="d_ffn_in", out_dtype=F32, b_blocked=True, tk=F2 // N_CHIPS)
    dW_up = _mm(h3, du0, "tn", name="dw_up", out_dtype=BF16, out_blocks=N_CHIPS)
    tok = exchange_start("ffn", ["w_down", "w_up"], [row_shards(dW_down), col_shards(dW_up)], dh3)
    dx2, dx2_b, dg_ffn = _rms_bwd(dh3, x2, r3 + tok[0:1, 0:1], g_ffn, dx3, name="norm_ffn_bwd")

    do_att = _mm(dx2_b, W_co, "nt", name="d_cross_o", out_dtype=BF16)
    dW_co = _mm(o_att, dx2_b, "tn", name="dw_co", out_dtype=BF16)
    tok = scatter_start("ffn", dW_co)
    dq, dkv = _attn_bwd(qc, kv, do_att, T=T, D=D, M=M)
    dkv_b = dkv.astype(BF16)
    dW_cq = _mm(h2, dq, "tn", name="dw_cq", out_dtype=BF16, after=tok)
    dh2 = _mm(dq, W_cq, "nt", name="d_cross_in", out_dtype=F32)
    dW_ckv = _mm(mem_n, dkv_b, "tn", name="dw_ckv", out_dtype=BF16, out_blocks=N_CHIPS)
    dmem_n = _mm(dkv_b, W_ckv, "nt", name="d_mem", out_dtype=F32, b_blocked=True)
    tok = exchange_start("cross", ["w_co", "w_cq", "w_ckv"],
                         [row_shards(dW_co), row_shards(dW_cq), col_shards(dW_ckv)], dmem_n)
    _, _, dg_mem = _rms_bwd(dmem_n, ms, rm, g_mem, None, name="norm_mem_bwd")
    dx1, dx1_b, dg_cross = _rms_bwd(dh2, x1, r2 + tok[0:1, 0:1], g_cross, dx2, name="norm_cross_bwd")

    dmerged = _mm(dx1_b, W_out, "nt", name="d_merged", out_dtype=BF16)
    dW_out = _mm(merged, dx1_b, "tn", name="dw_out", out_dtype=BF16)
    tok = scatter_start("cross", dW_out)
    dy_gla, dy_pool, dgates = _merge_bwd(dmerged, y_gla, y_pool, proj, T=T, D=D, col_block=2)
    dW_br_gla = _mm(o_gla, dy_gla, "tn", name="dw_branch_gla", out_dtype=BF16, after=tok)
    dW_br_pool = _mm(o_pool, dy_pool, "tn", name="dw_branch_pool", out_dtype=BF16)
    do_gla = _mm(dy_gla, W_branch, "nt", name="d_o_gla", out_dtype=F32, N=DV)
    do_pool = _mm(dy_pool, W_branch, "nt", name="d_o_pool", out_dtype=F32, N=PW, b_off=(DV, 0))
    dp, dw_pool, dpool_scale = _pool_bwd(proj, W_pool, pool_scale, do_pool, T=T, PW=PW, col_block=3)
    dW_pool = jnp.transpose(dw_pool.reshape(POOL_GROUPS, N_CHIPS, GW // N_CHIPS, GW), (1, 0, 2, 3))
    tok = exchange_start("mix", ["w_out", "w_branch", "w_pool"],
                         [row_shards(dW_out), row_shards(jnp.concatenate([dW_br_gla, dW_br_pool], axis=0)),
                          row_shards(dW_pool.reshape(N_CHIPS * POOL_GROUPS * (GW // N_CHIPS), GW).astype(BF16))],
                         dp)
    dqkvr, da_pad, dw2, db_a, dg_gla = _gla_bwd(proj, a_pad, W_a2p, b_a + tok[0:1, 0:1], g_gla, o_raw, states,
                                               do_gla, T=T, DK=DK, DV=DV)
    tok = scatter_start("mix", dqkvr)
    dproj = jnp.concatenate([dqkvr, dp, dgates], axis=1)
    dW_main = _mm(dproj, h1, "tn", name="dw_in_main", out_dtype=BF16, after=tok)
    dW_a = _mm(da_pad, h1, "tn", name="dw_in_rank", out_dtype=BF16)
    dW_in = jnp.concatenate([dW_main[:OFF_A], dW_a[:RANK], dW_main[OFF_A:]], axis=0)
    tok = exchange_start("in", ["w_in"], [row_shards(dW_in)], dW_a)
    dh1 = _mm(dproj, W_main, "nn", name="d_mix_in_main", out_dtype=F32, after=tok)
    dh1 = _mm(da_pad, W_a, "nn", name="d_mix_in_rank", out_dtype=F32, add=dh1)
    dx0, _, dg_mix = _rms_bwd(dh1, xs, r1, g_mix, dx1, name="norm_mix_bwd")

    grads = {}

    small_grads = [loss_part, dg_mix, db_a, dg_gla, dpool_scale, dg_cross, dg_mem, dg_ffn, dconv_b, dg_final,
                   dw2[:RANK], dconv_w]
    small_buf, offs = _pack(small_grads)
    small_sum = _all_reduce_small(small_buf)
    red = _unpack(small_sum, offs, [g.shape for g in small_grads])
    loss = red[0][0, 0]
    for k, g in zip(small_repl, red[1:10]):
        grads[k] = g.reshape(weights[k].shape)
    nb = DK // N_CHIPS
    grads["w_a2"] = lax.dynamic_slice_in_dim(red[10], chip * nb, nb, axis=1)[None]
    nb = F2 // N_CHIPS
    grads["conv_w"] = lax.dynamic_slice_in_dim(red[11], chip * nb, nb, axis=1)[None]

    delta, new_m, new_v = {}, {}, {}

    def shard_rows(k, a):
        a = a[0]
        return a.T if k == "w_in" else a.reshape(-1, a.shape[-1])

    def whole(k, a):
        a = a.reshape(-1, a.shape[2])
        return (a.T if k == "w_in" else a).reshape(weights[k].shape)

    scatter_start("in", small_sum)
    after = in_flight[-1][4][0]

    def finish(swapping):
        keys, s_sems, r_sems, mine, others, after = swapping
        mine, others = _split_wait(_swap_copies, s_sems, r_sems, mine, others, after,
                                   name=f"grad_swap_wait_{keys[0]}")
        for k, g_mine, g_other in zip(keys, mine, others):
            wmv = [halves(shard_rows(k, src[k])) for src in (weights, mom_m, mom_v)]
            res = _adamw_halves(*wmv, g_mine, g_other, c_idx, name=f"adamw_{k}")
            grads[k], delta[k], new_m[k], new_v[k] = (whole(k, a) for a in res)
        return res[1]

    swapping = None
    for group, keys, s_sems, r_sems, sums, lands in in_flight:
        sums, from_chips = _split_wait(_scatter_copies, s_sems, r_sems, sums, lands, after,
                                       name=f"grad_scatter_wait_{group}")
        half_sums = [_add_chips(s, r, chip_idx, name=f"grad_add_chips_{k}") for k, s, r in zip(keys, sums, from_chips)]
        others = [lax.empty(h.shape, h.dtype) for h in half_sums]
        s_sems, r_sems, half_sums, others, token = _split_start(
            _swap_copies, 1, half_sums, others, after, name=f"grad_swap_start_{group}")
        if swapping is not None:
            after = finish((*swapping, token))
        swapping = (keys, s_sems, r_sems, half_sums, others)
    finish((*swapping, after))
    small = small_repl + ["w_a2", "conv_w"]
    packs = [_pack([src[k] for k in small])[0] for src in (weights, grads, mom_m, mom_v)]
    _, offs = _pack([weights[k] for k in small])
    outs = _adamw(*packs, name="adamw_small")
    for res, o in zip((delta, new_m, new_v), outs):
        for k, a in zip(small, _unpack(o, offs, [weights[k].shape for k in small])):
            res[k] = a

    return (loss, dx0[None], *[grads[k] for k in order], *[delta[k] for k in order],
            *[new_m[k] for k in order], *[new_v[k] for k in order])
```

```python
import functools

import jax
import jax.numpy as jnp
from jax import lax
from jax.experimental import pallas as pl
from jax.experimental.pallas import tpu as pltpu

F32 = jnp.float32
BF16 = jnp.bfloat16
MESH = pl.DeviceIdType.MESH
HIGHEST = lax.Precision.HIGHEST

EPS = 1e-6
GLA_HEADS = 4
GLA_CHUNK = 64
GLA_GATE_NORM = 16.0
POOL_GROUPS = 4
CROSS_HEADS = 4
CONV_W = 3
N_CHIPS = 4
LANES = 128
SUBLANES = 8
VMEM_LIMIT = 56 << 20

ADAM_LR = 0.001
ADAM_B1 = 0.9
ADAM_B2 = 0.999
ADAM_EPS = 1e-08
ADAM_WD = 0.01
ADAM_STEP = 10

NN = (((1,), (0,)), ((), ()))
NT = (((1,), (1,)), ((), ()))
TN = (((0,), (0,)), ((), ()))


ONE_PASS = lax.Precision.HIGH


def _dot(a, b, dn=NN, precision=None):
    return lax.dot_general(a, b, dn, precision=precision, preferred_element_type=F32)


def _tile(n, pref, align=LANES):
    t = (min(pref, n) // align) * align
    while t >= align:
        if n % t == 0:
            return t
        t -= align
    return n


def _pcall(body, *, name, out_shape, grid=(), in_specs=None, out_specs=None, scratch_shapes=(),
           semantics=None, prefetch=0, aliases=None, split_copy=False):
    params = dict(vmem_limit_bytes=VMEM_LIMIT)
    if semantics is not None:
        params["dimension_semantics"] = semantics
    if split_copy:
        params["has_side_effects"] = pltpu.SideEffectType.DATAFLOW_SIDE_EFFECTING
    if prefetch:
        grid_spec = pltpu.PrefetchScalarGridSpec(
            num_scalar_prefetch=prefetch, grid=grid, in_specs=in_specs, out_specs=out_specs,
            scratch_shapes=scratch_shapes)
        return pl.pallas_call(body, name=name, out_shape=out_shape, grid_spec=grid_spec,
                              compiler_params=pltpu.CompilerParams(**params))
    kw = {}
    if aliases is not None:
        kw["input_output_aliases"] = aliases
    if in_specs is not None:
        kw["in_specs"] = in_specs
    if out_specs is not None:
        kw["out_specs"] = out_specs
    return pl.pallas_call(body, name=name, out_shape=out_shape, grid=grid,
                          scratch_shapes=scratch_shapes,
                          compiler_params=pltpu.CompilerParams(**params), **kw)


def _sigmoid(x):
    return 1.0 / (1.0 + jnp.exp(-x))


def _log_sigmoid(x):
    return jnp.minimum(x, 0.0) - jnp.log(1.0 + jnp.exp(-jnp.abs(x)))


def _mm(a, b, mode, *, name, out_dtype, M=None, N=None, K=None, a_off=(0, 0), b_off=(0, 0),
        add=None, b_blocked=False, out_blocks=0, after=None, tm=1536, tn=1536, tk=2048):
    if b_blocked:
        nb, R, Cb = b.shape
        b_rows, b_cols = R, nb * Cb
    else:
        b_rows, b_cols = b.shape
    if mode == "nn":
        M = M or a.shape[0]; K = K or a.shape[1]; N = N or b_cols
    elif mode == "nt":
        M = M or a.shape[0]; K = K or a.shape[1]; N = N or b_rows
    else:
        K = K or a.shape[0]; M = M or a.shape[1]; N = N or b_cols
    tm = _tile(M, tm, LANES if mode == "tn" else 16)
    tn = _tile(Cb if (b_blocked and mode != "nt") else (N // out_blocks if out_blocks else N), tn)
    tk = _tile(Cb if (b_blocked and mode == "nt") else K, tk)
    nk = K // tk
    dn = {"nn": NN, "nt": NT, "tn": TN}[mode]

    def off(o, t):
        assert o % t == 0, (name, o, t)
        return o // t

    if mode == "tn":
        ar, ac = off(a_off[0], tk), off(a_off[1], tm)
        a_spec = pl.BlockSpec((tk, tm), lambda i, j, k: (k + ar, i + ac))
    else:
        ar, ac = off(a_off[0], tm), off(a_off[1], tk)
        a_spec = pl.BlockSpec((tm, tk), lambda i, j, k: (i + ar, k + ac))
    if b_blocked and mode == "nt":
        per = Cb // tk
        b_spec = pl.BlockSpec((None, tn, tk), lambda i, j, k: (k // per, j, k % per))
    elif b_blocked:
        per = Cb // tn
        b_spec = pl.BlockSpec((None, tk, tn), lambda i, j, k: (j // per, k, j % per))
    elif mode == "nt":
        br, bc = off(b_off[0], tn), off(b_off[1], tk)
        b_spec = pl.BlockSpec((tn, tk), lambda i, j, k: (j + br, k + bc))
    else:
        br, bc = off(b_off[0], tk), off(b_off[1], tn)
        b_spec = pl.BlockSpec((tk, tn), lambda i, j, k: (k + br, j + bc))
    if out_blocks:
        per_o = N // out_blocks // tn
        o_spec = pl.BlockSpec((None, tm, tn), lambda i, j, k: (j // per_o, i, j % per_o))
        out_shape = jax.ShapeDtypeStruct((out_blocks, M, N // out_blocks), out_dtype)
    else:
        o_spec = pl.BlockSpec((tm, tn), lambda i, j, k: (i, j))
        out_shape = jax.ShapeDtypeStruct((M, N), out_dtype)
    in_specs = [a_spec, b_spec]
    args = [a, b]
    if add is not None:
        assert not out_blocks
        in_specs.append(o_spec)
        args.append(add)
    if after is not None:
        in_specs.append(pl.BlockSpec(memory_space=pl.ANY))
        args.append(after)
    n_in = len(args)

    def finish(r, refs):
        if add is not None:
            r = r + refs[2][...]
        o_ref = refs[n_in]
        o_ref[...] = r.astype(o_ref.dtype)

    def body_one(*refs):
        finish(_dot(refs[0][...].astype(BF16), refs[1][...].astype(BF16), dn), refs)

    def body_acc(*refs):
        acc_ref = refs[-1]
        k = pl.program_id(2)

        @pl.when(k == 0)
        def _():
            acc_ref[...] = jnp.zeros_like(acc_ref)

        acc_ref[...] += _dot(refs[0][...].astype(BF16), refs[1][...].astype(BF16), dn)

        @pl.when(k == nk - 1)
        def _():
            finish(acc_ref[...], refs)

    return _pcall(body_one if nk == 1 else body_acc, name=name, out_shape=out_shape,
                  grid=(M // tm, N // tn, nk), in_specs=in_specs, out_specs=o_spec,
                  scratch_shapes=[] if nk == 1 else [pltpu.VMEM((tm, tn), F32)],
                  semantics=("parallel", "parallel", "arbitrary"))(*args)


def _rms_fwd(x, g, *, name):
    T, D = x.shape
    tr = _tile(T, 128, 16)

    def body(x_ref, g_ref, h_ref, r_ref):
        xv = x_ref[...]
        r = lax.rsqrt(jnp.mean(xv * xv, axis=-1, keepdims=True) + EPS)
        h_ref[...] = (xv * r * g_ref[...]).astype(h_ref.dtype)
        r_ref[...] = r

    row = pl.BlockSpec((tr, D), lambda i: (i, 0))
    return _pcall(body, name=name,
                  out_shape=(jax.ShapeDtypeStruct((T, D), BF16), jax.ShapeDtypeStruct((T, 1), F32)),
                  grid=(T // tr,),
                  in_specs=[row, pl.BlockSpec((1, D), lambda i: (0, 0))],
                  out_specs=(row, pl.BlockSpec((tr, 1), lambda i: (i, 0))),
                  semantics=("parallel",))(x, g)


def _rms_bwd(dh, x, rstd, g, dres, *, name):
    T, D = x.shape
    tr = _tile(T, 128, 16)
    has_res = dres is not None

    def body(*refs):
        if has_res:
            dh_ref, x_ref, r_ref, g_ref, res_ref, dx_ref, dxb_ref, dg_ref = refs
        else:
            dh_ref, x_ref, r_ref, g_ref, dx_ref, dxb_ref, dg_ref = refs
        r = r_ref[...]
        xh = x_ref[...] * r
        dhv = dh_ref[...].astype(F32)
        dxh = dhv * g_ref[...]
        m = jnp.mean(dxh * xh, axis=-1, keepdims=True)
        dx = r * (dxh - xh * m)
        if has_res:
            dx = dx + res_ref[...]
        dx_ref[...] = dx
        dxb_ref[...] = dx.astype(BF16)

        @pl.when(pl.program_id(0) == 0)
        def _():
            dg_ref[...] = jnp.zeros_like(dg_ref)

        dg_ref[...] += jnp.sum(dhv * xh, axis=0, keepdims=True)

    row = pl.BlockSpec((tr, D), lambda i: (i, 0))
    vec = pl.BlockSpec((1, D), lambda i: (0, 0))
    in_specs = [row, row, pl.BlockSpec((tr, 1), lambda i: (i, 0)), vec]
    args = [dh, x, rstd, g]
    if has_res:
        in_specs.append(row)
        args.append(dres)
    return _pcall(body, name=name,
                  out_shape=(jax.ShapeDtypeStruct((T, D), F32), jax.ShapeDtypeStruct((T, D), BF16),
                             jax.ShapeDtypeStruct((1, D), F32)),
                  grid=(T // tr,), in_specs=in_specs, out_specs=(row, row, vec),
                  semantics=("arbitrary",))(*args)


def _loss_head(x3, g, tgt):
    T, D = x3.shape
    tr = _tile(T, 128, 16)

    def body(x_ref, g_ref, t_ref, loss_ref, dx_ref, dxb_ref, dg_ref):
        xv = x_ref[...]
        gv = g_ref[...]
        r = lax.rsqrt(jnp.mean(xv * xv, axis=-1, keepdims=True) + EPS)
        xh = xv * r
        err = xh * gv - t_ref[...]
        dy = err * (1.0 / D)
        dxh = dy * gv
        m = jnp.mean(dxh * xh, axis=-1, keepdims=True)
        dx = r * (dxh - xh * m)
        dx_ref[...] = dx
        dxb_ref[...] = dx.astype(BF16)

        @pl.when(pl.program_id(0) == 0)
        def _():
            dg_ref[...] = jnp.zeros_like(dg_ref)
            loss_ref[...] = jnp.zeros_like(loss_ref)

        dg_ref[...] += jnp.sum(dy * xh, axis=0, keepdims=True)
        part = 0.5 * jnp.sum(jnp.mean(err * err, axis=-1, keepdims=True), axis=0, keepdims=True)
        loss_ref[...] += jnp.broadcast_to(part, loss_ref.shape)

    row = pl.BlockSpec((tr, D), lambda i: (i, 0))
    vec = pl.BlockSpec((1, D), lambda i: (0, 0))
    return _pcall(body, name="loss_head",
                  out_shape=(jax.ShapeDtypeStruct((1, LANES), F32), jax.ShapeDtypeStruct((T, D), F32),
                             jax.ShapeDtypeStruct((T, D), BF16), jax.ShapeDtypeStruct((1, D), F32)),
                  grid=(T // tr,), in_specs=[row, vec, row],
                  out_specs=(pl.BlockSpec((1, LANES), lambda i: (0, 0)), row, row, vec),
                  semantics=("arbitrary",))(x3, g, tgt)


def _gla_chunk_terms(qk, a_ref, w2_ref, ba_ref, DK):
    C = qk.shape[0]
    gp = _dot(a_ref[...].astype(BF16), w2_ref[...]) + ba_ref[...]
    la = _log_sigmoid(gp) * (1.0 / GLA_GATE_NORM)
    row = lax.broadcasted_iota(jnp.int32, (C, C), 0)
    col = lax.broadcasted_iota(jnp.int32, (C, C), 1)
    causal = row >= col
    b = _dot(causal.astype(F32), la, precision=HIGHEST)
    return gp, b, causal


def _gla_fwd(proj, a_pad, w2, b_a, g_gla, *, T, DK, DV):
    assert 2 * DK == DV
    H = GLA_HEADS
    HK, HV = DK // H, DV // H
    C = GLA_CHUNK
    n = T // C
    RP = a_pad.shape[1]
    scale = HK ** -0.5

    def body(qk_ref, v_ref, r_ref, a_ref, w2_ref, ba_ref, gg_ref, og_ref, oraw_ref, st_ref, s_ref):
        @pl.when(pl.program_id(0) == 0)
        def _():
            s_ref[...] = jnp.zeros_like(s_ref)

        st_ref[...] = s_ref[...]
        qk = qk_ref[...]
        _, b, causal = _gla_chunk_terms(qk, a_ref, w2_ref, ba_ref, DK)
        for h in range(H):
            ks = slice(h * HK, (h + 1) * HK)
            vs = slice(h * HV, (h + 1) * HV)
            bh = b[:, ks]
            b_last = bh[C - 1:C, :]
            qt = qk[:, ks] * scale * jnp.exp(bh)
            kh = qk[:, DK + h * HK:DK + (h + 1) * HK]
            kt = kh * jnp.exp(-bh)
            khat = kh * jnp.exp(b_last - bh)
            a_mat = jnp.where(causal, _dot(qt, kt, NT, ONE_PASS), 0.0)
            vh = v_ref[:, vs]
            s_t = s_ref[h]
            o = _dot(a_mat, vh, NN, ONE_PASS) + _dot(qt, s_t, NT, ONE_PASS)
            s_ref[h] = s_t * jnp.exp(b_last) + _dot(vh, khat, TN, ONE_PASS)
            rs = lax.rsqrt(jnp.mean(o * o, axis=-1, keepdims=True) + EPS)
            rr = r_ref[:, vs]
            og = o * rs * gg_ref[:, vs] * (rr * _sigmoid(rr))
            oraw_ref[:, vs] = o
            og_ref[:, vs] = og.astype(BF16)

    blk = lambda j: pl.BlockSpec((C, DV), lambda i: (i, j))
    full = lambda s: pl.BlockSpec(s, lambda i: (0,) * len(s))
    return _pcall(
        body, name="gla_fwd",
        out_shape=(jax.ShapeDtypeStruct((T, DV), BF16), jax.ShapeDtypeStruct((T, DV), F32),
                   jax.ShapeDtypeStruct((n, H, HV, HK), F32)),
        grid=(n,),
        in_specs=[blk(0), blk(1), blk(2), pl.BlockSpec((C, RP), lambda i: (i, 0)),
                  full((RP, DK)), full((1, DK)), full((1, DV))],
        out_specs=(blk(0), blk(0), pl.BlockSpec((None, H, HV, HK), lambda i: (i, 0, 0, 0))),
        scratch_shapes=[pltpu.VMEM((H, HV, HK), F32)],
        semantics=("arbitrary",))(proj, proj, proj, a_pad, w2, b_a, g_gla)


def _gla_bwd(proj, a_pad, w2, b_a, g_gla, o_raw, states, do_gla, *, T, DK, DV):
    H = GLA_HEADS
    HK, HV = DK // H, DV // H
    C = GLA_CHUNK
    n = T // C
    RP = a_pad.shape[1]
    scale = HK ** -0.5

    def body(qk_ref, v_ref, r_ref, a_ref, w2_ref, ba_ref, gg_ref, oraw_ref, st_ref, dog_ref,
             dqkvr_ref, da_ref, dw2_ref, dba_ref, dgg_ref, ds_ref):
        @pl.when(pl.program_id(0) == 0)
        def _():
            ds_ref[...] = jnp.zeros_like(ds_ref)
            dw2_ref[...] = jnp.zeros_like(dw2_ref)
            dba_ref[...] = jnp.zeros_like(dba_ref)
            dgg_ref[...] = jnp.zeros_like(dgg_ref)

        qk = qk_ref[...]
        gp, b, causal = _gla_chunk_terms(qk, a_ref, w2_ref, ba_ref, DK)
        row = lax.broadcasted_iota(jnp.int32, (C, C), 0)
        col = lax.broadcasted_iota(jnp.int32, (C, C), 1)
        upper = (col >= row).astype(F32)
        dla_parts = []
        for h in range(H):
            ks = slice(h * HK, (h + 1) * HK)
            vs = slice(h * HV, (h + 1) * HV)
            bh = b[:, ks]
            b_last = bh[C - 1:C, :]
            eb = jnp.exp(bh)
            emb = jnp.exp(-bh)
            ehat = jnp.exp(b_last - bh)
            e_last = jnp.exp(b_last)
            qt = qk[:, ks] * scale * eb
            kh = qk[:, DK + h * HK:DK + (h + 1) * HK]
            kt = kh * emb
            khat = kh * ehat
            a_mat = jnp.where(causal, _dot(qt, kt, NT, ONE_PASS), 0.0)
            vh = v_ref[:, vs]
            o = oraw_ref[:, vs]
            rs = lax.rsqrt(jnp.mean(o * o, axis=-1, keepdims=True) + EPS)
            on = o * rs
            gg = gg_ref[:, vs]
            rr = r_ref[:, vs]
            sg = _sigmoid(rr)
            d_out = dog_ref[:, vs]
            dr = d_out * (on * gg) * (sg * (1.0 + rr * (1.0 - sg)))
            d_og = d_out * (rr * sg)
            dgg_ref[:, vs] += jnp.sum(d_og * on, axis=0, keepdims=True)
            d_on = d_og * gg
            d_o = rs * (d_on - on * jnp.mean(d_on * on, axis=-1, keepdims=True))
            s_t = st_ref[h]
            ds_t = ds_ref[h]
            d_a = jnp.where(causal, _dot(d_o, vh, NT, ONE_PASS), 0.0)
            dv = _dot(a_mat, d_o, TN, ONE_PASS) + _dot(khat, ds_t, NT, ONE_PASS)
            dqt = _dot(d_a, kt, NN, ONE_PASS) + _dot(d_o, s_t, NN, ONE_PASS)
            dkt = _dot(d_a, qt, TN, ONE_PASS)
            dkhat = _dot(vh, ds_t, NN, ONE_PASS)
            ds_ref[h] = ds_t * e_last + _dot(d_o, qt, TN, ONE_PASS)
            dq = dqt * eb * scale
            dk = dkt * emb + dkhat * ehat
            db = dqt * qt - dkt * kt - dkhat * khat
            d_last = (jnp.sum(dkhat * khat, axis=0, keepdims=True)
                      + e_last * jnp.sum(ds_t * s_t, axis=0, keepdims=True))
            dla_parts.append(_dot(upper, db, NN, HIGHEST) + d_last)
            dqkvr_ref[:, ks] = dq.astype(BF16)
            dqkvr_ref[:, DK + h * HK:DK + (h + 1) * HK] = dk.astype(BF16)
            dqkvr_ref[:, DV + h * HV:DV + (h + 1) * HV] = dv.astype(BF16)
            dqkvr_ref[:, 2 * DV + h * HV:2 * DV + (h + 1) * HV] = dr.astype(BF16)
        dla = jnp.concatenate(dla_parts, axis=1)
        dgp = dla * (1.0 / GLA_GATE_NORM) * _sigmoid(-gp)
        dba_ref[...] += jnp.sum(dgp, axis=0, keepdims=True)
        dgp_b = dgp.astype(BF16)
        dw2_ref[...] += _dot(a_ref[...].astype(BF16), dgp_b, TN)
        da_ref[...] = _dot(dgp_b, w2_ref[...], NT).astype(BF16)

    rev = lambda j: pl.BlockSpec((C, DV), lambda i: (n - 1 - i, j))
    full = lambda s: pl.BlockSpec(s, lambda i: (0,) * len(s))
    return _pcall(
        body, name="gla_bwd",
        out_shape=(jax.ShapeDtypeStruct((T, 3 * DV), BF16), jax.ShapeDtypeStruct((T, RP), BF16),
                   jax.ShapeDtypeStruct((RP, DK), F32), jax.ShapeDtypeStruct((1, DK), F32),
                   jax.ShapeDtypeStruct((1, DV), F32)),
        grid=(n,),
        in_specs=[rev(0), rev(1), rev(2), pl.BlockSpec((C, RP), lambda i: (n - 1 - i, 0)),
                  full((RP, DK)), full((1, DK)), full((1, DV)), rev(0),
                  pl.BlockSpec((None, H, HV, HK), lambda i: (n - 1 - i, 0, 0, 0)), rev(0)],
        out_specs=(pl.BlockSpec((C, 3 * DV), lambda i: (n - 1 - i, 0)),
                   pl.BlockSpec((C, RP), lambda i: (n - 1 - i, 0)),
                   full((RP, DK)), full((1, DK)), full((1, DV))),
        scratch_shapes=[pltpu.VMEM((H, HV, HK), F32)],
        semantics=("arbitrary",))(proj, proj, proj, a_pad, w2, b_a, g_gla, o_raw, states, do_gla)


def _pool_windows(p, g, T):
    t = lax.broadcasted_iota(jnp.int32, (T, 1), 0)
    s = p
    for lvl in range(POOL_GROUPS):
        sh = 1 << lvl
        nxt = s + jnp.where(t >= sh, pltpu.roll(s, sh, 0), 0.0)
        s = jnp.where(lvl <= g, nxt, s)
    win = jnp.left_shift(2, g)
    inv = 1.0 / jnp.minimum(t + 1, win).astype(F32)
    return s * inv - p, inv


def _pool_fwd(proj, w_pool, scale, *, T, PW, col_block):
    GW = PW // POOL_GROUPS
    per = PW // GW

    def body(p_ref, w_ref, s_ref, o_ref):
        g = pl.program_id(0)
        pooled, _ = _pool_windows(p_ref[...], g, T)
        mixed = _dot(pooled.astype(BF16), w_ref[...])
        o_ref[...] = (mixed * s_ref[...]).astype(BF16)

    return _pcall(body, name="pool_fwd", out_shape=jax.ShapeDtypeStruct((T, PW), BF16),
                  grid=(POOL_GROUPS,),
                  in_specs=[pl.BlockSpec((T, GW), lambda g: (0, col_block * per + g)),
                            pl.BlockSpec((None, GW, GW), lambda g: (g, 0, 0)),
                            pl.BlockSpec((1, GW), lambda g: (0, g))],
                  out_specs=pl.BlockSpec((T, GW), lambda g: (0, g)),
                  semantics=("parallel",))(proj, w_pool, scale)


def _pool_bwd(proj, w_pool, scale, do_pool, *, T, PW, col_block):
    GW = PW // POOL_GROUPS
    per = PW // GW

    def body(p_ref, w_ref, s_ref, do_ref, dp_ref, dw_ref, dsc_ref):
        g = pl.program_id(0)
        pooled, inv = _pool_windows(p_ref[...], g, T)
        pooled_b = pooled.astype(BF16)
        w = w_ref[...]
        mixed = _dot(pooled_b, w)
        d_out = do_ref[...]
        dsc_ref[...] = jnp.sum(d_out * mixed, axis=0, keepdims=True)
        dmixed = (d_out * s_ref[...]).astype(BF16)
        dw_ref[...] = _dot(pooled_b, dmixed, TN)
        dpooled = _dot(dmixed, w, NT)
        t = lax.broadcasted_iota(jnp.int32, (T, 1), 0)
        s = dpooled * inv
        for lvl in range(POOL_GROUPS):
            sh = 1 << lvl
            nxt = s + jnp.where(t < T - sh, pltpu.roll(s, T - sh, 0), 0.0)
            s = jnp.where(lvl <= g, nxt, s)
        dp_ref[...] = (s - dpooled).astype(BF16)

    return _pcall(body, name="pool_bwd",
                  out_shape=(jax.ShapeDtypeStruct((T, PW), BF16),
                             jax.ShapeDtypeStruct((POOL_GROUPS, GW, GW), F32),
                             jax.ShapeDtypeStruct((1, PW), F32)),
                  grid=(POOL_GROUPS,),
                  in_specs=[pl.BlockSpec((T, GW), lambda g: (0, col_block * per + g)),
                            pl.BlockSpec((None, GW, GW), lambda g: (g, 0, 0)),
                            pl.BlockSpec((1, GW), lambda g: (0, g)),
                            pl.BlockSpec((T, GW), lambda g: (0, g))],
                  out_specs=(pl.BlockSpec((T, GW), lambda g: (0, g)),
                             pl.BlockSpec((None, GW, GW), lambda g: (g, 0, 0)),
                             pl.BlockSpec((1, GW), lambda g: (0, g))),
                  semantics=("parallel",))(proj, w_pool, scale, do_pool)


def _merge_fwd(y_gla, y_pool, proj, *, T, D, col_block):
    tr = _tile(T, 128, 16)

    def body(yg_ref, yp_ref, g1_ref, g2_ref, o_ref):
        o_ref[...] = (_sigmoid(g1_ref[...]) * yg_ref[...]
                      + _sigmoid(g2_ref[...]) * yp_ref[...]).astype(BF16)

    row = pl.BlockSpec((tr, D), lambda i: (i, 0))
    return _pcall(body, name="merge_fwd", out_shape=jax.ShapeDtypeStruct((T, D), BF16),
                  grid=(T // tr,),
                  in_specs=[row, row, pl.BlockSpec((tr, D), lambda i: (i, col_block)),
                            pl.BlockSpec((tr, D), lambda i: (i, col_block + 1))],
                  out_specs=row, semantics=("parallel",))(y_gla, y_pool, proj, proj)


def _merge_bwd(dmerged, y_gla, y_pool, proj, *, T, D, col_block):
    tr = _tile(T, 128, 16)

    def body(dm_ref, yg_ref, yp_ref, g1_ref, g2_ref, dyg_ref, dyp_ref, dg_ref):
        dm = dm_ref[...]
        s1 = _sigmoid(g1_ref[...])
        s2 = _sigmoid(g2_ref[...])
        dyg_ref[...] = (dm * s1).astype(BF16)
        dyp_ref[...] = (dm * s2).astype(BF16)
        dg_ref[:, :D] = (dm * yg_ref[...] * s1 * (1.0 - s1)).astype(BF16)
        dg_ref[:, D:] = (dm * yp_ref[...] * s2 * (1.0 - s2)).astype(BF16)

    row = pl.BlockSpec((tr, D), lambda i: (i, 0))
    return _pcall(body, name="merge_bwd",
                  out_shape=(jax.ShapeDtypeStruct((T, D), BF16), jax.ShapeDtypeStruct((T, D), BF16),
                             jax.ShapeDtypeStruct((T, 2 * D), BF16)),
                  grid=(T // tr,),
                  in_specs=[row, row, row, pl.BlockSpec((tr, D), lambda i: (i, col_block)),
                            pl.BlockSpec((tr, D), lambda i: (i, col_block + 1))],
                  out_specs=(row, row, pl.BlockSpec((tr, 2 * D), lambda i: (i, 0))),
                  semantics=("parallel",))(dmerged, y_gla, y_pool, proj, proj)


def _attn_fwd(q, kv, *, T, D, M):
    H = CROSS_HEADS
    HD = D // H
    tq = _tile(T, 512, 16)
    scale = HD ** -0.5

    def body(q_ref, kv_ref, o_ref):
        for h in range(H):
            hs = slice(h * HD, (h + 1) * HD)
            s = _dot(q_ref[:, hs], kv_ref[:, hs], NT) * scale
            e = jnp.exp(s - jnp.max(s, axis=-1, keepdims=True))
            p = e / jnp.sum(e, axis=-1, keepdims=True)
            o_ref[:, hs] = _dot(p.astype(BF16), kv_ref[:, D + h * HD:D + (h + 1) * HD]).astype(BF16)

    row = pl.BlockSpec((tq, D), lambda i: (i, 0))
    return _pcall(body, name="attn_fwd", out_shape=jax.ShapeDtypeStruct((T, D), BF16),
                  grid=(T // tq,), in_specs=[row, pl.BlockSpec((M, 2 * D), lambda i: (0, 0))],
                  out_specs=row, semantics=("parallel",))(q, kv)


def _attn_bwd(q, kv, do, *, T, D, M):
    H = CROSS_HEADS
    HD = D // H
    tq = _tile(T, 512, 16)
    scale = HD ** -0.5

    def body(q_ref, kv_ref, do_ref, dq_ref, dkv_ref):
        @pl.when(pl.program_id(0) == 0)
        def _():
            dkv_ref[...] = jnp.zeros_like(dkv_ref)

        for h in range(H):
            hs = slice(h * HD, (h + 1) * HD)
            vs = slice(D + h * HD, D + (h + 1) * HD)
            qh = q_ref[:, hs]
            kh = kv_ref[:, hs]
            s = _dot(qh, kh, NT) * scale
            e = jnp.exp(s - jnp.max(s, axis=-1, keepdims=True))
            p = e / jnp.sum(e, axis=-1, keepdims=True)
            p_b = p.astype(BF16)
            d_o = do_ref[:, hs]
            dkv_ref[:, vs] += _dot(p_b, d_o, TN)
            dp = _dot(d_o, kv_ref[:, vs], NT)
            ds = (p * (dp - jnp.sum(dp * p, axis=-1, keepdims=True)) * scale).astype(BF16)
            dq_ref[:, hs] = _dot(ds, kh).astype(BF16)
            dkv_ref[:, hs] += _dot(ds, qh, TN)

    row = pl.BlockSpec((tq, D), lambda i: (i, 0))
    full = pl.BlockSpec((M, 2 * D), lambda i: (0, 0))
    return _pcall(body, name="attn_bwd",
                  out_shape=(jax.ShapeDtypeStruct((T, D), BF16), jax.ShapeDtypeStruct((M, 2 * D), F32)),
                  grid=(T // tq,), in_specs=[row, full, row], out_specs=(row, full),
                  semantics=("arbitrary",))(q, kv, do)


def _shift_down(x, halo, s):
    out = pltpu.roll(x, s, 0)
    t8 = lax.broadcasted_iota(jnp.int32, (SUBLANES, 1), 0)
    head = out[:SUBLANES]
    for j in range(s):
        head = jnp.where(t8 == j, halo[SUBLANES - s + j:SUBLANES - s + j + 1, :], head)
    return head if x.shape[0] == SUBLANES else jnp.concatenate([head, out[SUBLANES:]], axis=0)


def _shift_up(x, halo, s):
    rows = x.shape[0]
    out = pltpu.roll(x, rows - s, 0)
    t8 = lax.broadcasted_iota(jnp.int32, (SUBLANES, 1), 0)
    tail = out[rows - SUBLANES:]
    for j in range(s):
        tail = jnp.where(t8 == SUBLANES - s + j, halo[j:j + 1, :], tail)
    return jnp.concatenate([out[:rows - SUBLANES], tail], axis=0)


def _conv_tiles(T):
    tt = _tile(T, 128, SUBLANES)
    return tt, tt // SUBLANES, T // SUBLANES


def _conv_fwd(u0, conv_w, conv_b, *, T, F):
    tt, hb, _ = _conv_tiles(T)
    cw = _tile(F, LANES)

    def body(u_ref, prev_ref, w_ref, b_ref, f_ref):
        i = pl.program_id(0)

        def conv(cs):
            x = u_ref[:, cs]
            halo = jnp.where(i > 0, prev_ref[:, cs], 0.0)
            return (w_ref[2:3, cs] * x + w_ref[1:2, cs] * _shift_down(x, halo, 1)
                    + w_ref[0:1, cs] * _shift_down(x, halo, 2) + b_ref[:, cs])

        for j in range(F // cw):
            gate = conv(slice(j * cw, (j + 1) * cw))
            val = conv(slice(F + j * cw, F + (j + 1) * cw))
            f_ref[:, j * cw:(j + 1) * cw] = (gate * _sigmoid(gate) * val).astype(BF16)

    return _pcall(body, name="conv_fwd", out_shape=jax.ShapeDtypeStruct((T, F), BF16),
                  grid=(T // tt,),
                  in_specs=[pl.BlockSpec((tt, 2 * F), lambda i: (i, 0)),
                            pl.BlockSpec((SUBLANES, 2 * F), lambda i: (jnp.maximum(i * hb - 1, 0), 0)),
                            pl.BlockSpec((CONV_W, 2 * F), lambda i: (0, 0)),
                            pl.BlockSpec((1, 2 * F), lambda i: (0, 0))],
                  out_specs=pl.BlockSpec((tt, F), lambda i: (i, 0)),
                  semantics=("parallel",))(u0, u0, conv_w, conv_b)


def _conv_bwd(u0, conv_w, conv_b, df, *, T, F):
    tt, hb, nb = _conv_tiles(T)
    nt = T // tt
    cw = _tile(F, LANES)

    def body(u_ref, prev_ref, next_ref, df_ref, dfn_ref, w_ref, b_ref, du0_ref, dw_ref, db_ref):
        i = pl.program_id(0)

        @pl.when(i == 0)
        def _():
            dw_ref[...] = jnp.zeros_like(dw_ref)
            db_ref[...] = jnp.zeros_like(db_ref)

        def conv(cs):
            x = u_ref[:, cs]
            halo = jnp.where(i > 0, prev_ref[:, cs], 0.0)
            x1 = _shift_down(x, halo, 1)
            x2 = _shift_down(x, halo, 2)
            u = w_ref[2:3, cs] * x + w_ref[1:2, cs] * x1 + w_ref[0:1, cs] * x2 + b_ref[:, cs]
            xn = next_ref[:, cs]
            tail = x[tt - SUBLANES:, :]
            un = (w_ref[2:3, cs] * xn + w_ref[1:2, cs] * _shift_down(xn, tail, 1)
                  + w_ref[0:1, cs] * _shift_down(xn, tail, 2) + b_ref[:, cs])
            return u, un, (x, x1, x2)

        def glu_grad(gate, val, dff):
            sg = _sigmoid(gate)
            return dff * val * (sg * (1.0 + gate * (1.0 - sg))), dff * (gate * sg)

        def finish(cs, du, dun, xs):
            du0 = (w_ref[2:3, cs] * du + w_ref[1:2, cs] * _shift_up(du, dun, 1)
                   + w_ref[0:1, cs] * _shift_up(du, dun, 2))
            du0_ref[:, cs] = du0.astype(BF16)
            db_ref[:, cs] += jnp.sum(du, axis=0, keepdims=True)
            dw_ref[2:3, cs] += jnp.sum(du * xs[0], axis=0, keepdims=True)
            dw_ref[1:2, cs] += jnp.sum(du * xs[1], axis=0, keepdims=True)
            dw_ref[0:1, cs] += jnp.sum(du * xs[2], axis=0, keepdims=True)

        for j in range(F // cw):
            fs = slice(j * cw, (j + 1) * cw)
            gs, vs = fs, slice(F + j * cw, F + (j + 1) * cw)
            ug, ung, xg = conv(gs)
            uv, unv, xv = conv(vs)
            dug, duv = glu_grad(ug, uv, df_ref[:, fs].astype(F32))
            dung, dunv = glu_grad(ung, unv, dfn_ref[0:SUBLANES, fs].astype(F32))
            dung = jnp.where(i < nt - 1, dung, 0.0)
            dunv = jnp.where(i < nt - 1, dunv, 0.0)
            finish(gs, dug, dung, xg)
            finish(vs, duv, dunv, xv)

    wide = lambda rows, fn: pl.BlockSpec((rows, 2 * F), fn)
    nxt = lambda i: (jnp.minimum((i + 1) * hb, nb - 1), 0)
    return _pcall(body, name="conv_bwd",
                  out_shape=(jax.ShapeDtypeStruct((T, 2 * F), BF16),
                             jax.ShapeDtypeStruct((CONV_W, 2 * F), F32),
                             jax.ShapeDtypeStruct((1, 2 * F), F32)),
                  grid=(nt,),
                  in_specs=[wide(tt, lambda i: (i, 0)),
                            wide(SUBLANES, lambda i: (jnp.maximum(i * hb - 1, 0), 0)),
                            wide(SUBLANES, nxt),
                            pl.BlockSpec((tt, F), lambda i: (i, 0)),
                            pl.BlockSpec((2 * SUBLANES, F),
                                         lambda i: (jnp.minimum((i + 1) * (hb // 2), nb // 2 - 1), 0)),
                            wide(CONV_W, lambda i: (0, 0)), wide(1, lambda i: (0, 0))],
                  out_specs=(wide(tt, lambda i: (i, 0)), wide(CONV_W, lambda i: (0, 0)),
                             wide(1, lambda i: (0, 0))),
                  semantics=("arbitrary",))(u0, u0, u0, df, df, conv_w, conv_b)


def _adamw(w, g, m, v, *, name):
    R, C = w.shape
    tr = _tile(R, max(SUBLANES, (1 << 19) // max(C, 1) // SUBLANES * SUBLANES), SUBLANES)
    c1 = 1.0 / (1.0 - ADAM_B1 ** ADAM_STEP)
    c2 = 1.0 / (1.0 - ADAM_B2 ** ADAM_STEP)

    def body(w_ref, g_ref, m_ref, v_ref, d_ref, mo_ref, vo_ref):
        gv = g_ref[...]
        mn = ADAM_B1 * m_ref[...] + (1.0 - ADAM_B1) * gv
        vn = ADAM_B2 * v_ref[...] + (1.0 - ADAM_B2) * (gv * gv)
        d_ref[...] = -ADAM_LR * ((mn * c1) / (jnp.sqrt(vn * c2) + ADAM_EPS) + ADAM_WD * w_ref[...])
        mo_ref[...] = mn
        vo_ref[...] = vn

    blk = pl.BlockSpec((tr, C), lambda i: (i, 0))
    shp = jax.ShapeDtypeStruct((R, C), F32)
    return _pcall(body, name=name, out_shape=(shp, shp, shp), grid=(R // tr,),
                  in_specs=[blk] * 4, out_specs=(blk,) * 3, semantics=("parallel",))(w, g, m, v)


def _blk(h, C, elems=1 << 19, align=16):
    th = _tile(h, max(align, elems // C // align * align), align)
    if th < h or h * C <= 2 * elems:
        return th, C
    return h, _tile(C, max(LANES, elems // h // LANES * LANES))


def _adamw_halves(w, m, v, g_mine, g_other, c_idx, *, name):
    _, h, C = w.shape
    th, tc = _blk(h, C, align=SUBLANES)
    c1 = 1.0 / (1.0 - ADAM_B1 ** ADAM_STEP)
    c2 = 1.0 / (1.0 - ADAM_B2 ** ADAM_STEP)

    def body(c_ref, w_ref, m_ref, v_ref, gm_ref, go_ref, g_ref, d_ref, mo_ref, vo_ref):
        gv = jnp.where(pl.program_id(0) == c_ref[0], gm_ref[...], go_ref[...])
        mn = ADAM_B1 * m_ref[...] + (1.0 - ADAM_B1) * gv
        vn = ADAM_B2 * v_ref[...] + (1.0 - ADAM_B2) * (gv * gv)
        d_ref[...] = -ADAM_LR * ((mn * c1) / (jnp.sqrt(vn * c2) + ADAM_EPS) + ADAM_WD * w_ref[...])
        g_ref[...] = gv
        mo_ref[...] = mn
        vo_ref[...] = vn

    blk = pl.BlockSpec((None, th, tc), lambda s, i, j, c: (s, i, j))

    def pick(mine):
        def index(s, i, j, c):
            use = (s == c[0]) if mine else (s != c[0])
            return jnp.where(use, i, 0), jnp.where(use, j, 0)
        return pl.BlockSpec((th, tc), index)

    shp = jax.ShapeDtypeStruct((2, h, C), F32)
    return _pcall(body, name=name, out_shape=(shp,) * 4, grid=(2, h // th, C // tc), prefetch=1,
                  in_specs=[blk, blk, blk, pick(True), pick(False)], out_specs=(blk,) * 4,
                  semantics=("parallel", "parallel", "parallel"))(c_idx, w, m, v, g_mine, g_other)


def _mesh_pos():
    x, y, c = lax.axis_index("x"), lax.axis_index("y"), lax.axis_index("c")
    others = [(1 - x, y), (x, 1 - y), (1 - x, 1 - y)]
    return x, y, c, others


def _gather_copies(shards, lands, send_sems, recv_sems):
    x, y, c, others = _mesh_pos()
    me = 2 * x + y
    return [pltpu.make_async_remote_copy(
        src_ref=shards[a].at[c], dst_ref=lands[a].at[me, c],
        send_sem=send_sems.at[3 * a + j], recv_sem=recv_sems.at[3 * a + j],
        device_id=(*chip, c), device_id_type=MESH)
        for a in range(len(shards)) for j, chip in enumerate(others)]


def _near_copies(shards, lands, send_sems, recv_sems):
    x, y, c, others = _mesh_pos()
    me = 2 * x + y
    return [pltpu.make_async_remote_copy(
        src_ref=shards[a].at[c], dst_ref=lands[a].at[me, c],
        send_sem=send_sems.at[2 * a + j], recv_sem=recv_sems.at[2 * a + j],
        device_id=(*chip, c), device_id_type=MESH)
        for a in range(len(shards)) for j, chip in enumerate(others[:2])]


def _relay_copies(shards, zones, send_sems, recv_sems):
    x, y, c, others = _mesh_pos()
    (nx, ny), copies = others[:2], []
    for a in range(len(zones)):
        hc = zones[a].shape[-1] // 2
        for k, (src_chip, to, lo) in enumerate(((ny, nx, 0), (nx, ny, hc))):
            part = zones[a].at[2 * src_chip[0] + src_chip[1], c, :, pl.ds(lo, hc)]
            copies.append(pltpu.make_async_remote_copy(
                src_ref=part, dst_ref=part, send_sem=send_sems.at[2 * a + k], recv_sem=recv_sems.at[2 * a + k],
                device_id=(*to, c), device_id_type=MESH))
    return copies


def _pass_copies(shards, zones, send_sems, recv_sems, pieces=(0, 1, 2, 3)):
    x, y, c, others = _mesh_pos()
    me = 2 * x + y
    copies = []
    for a in range(len(shards)):
        srcs = [zones[a].at[2 * chip[0] + chip[1], c] for chip in others] + [shards[a]]
        dsts = [zones[a].at[2 * chip[0] + chip[1], c] for chip in others] + [zones[a].at[me]]
        copies += [pltpu.make_async_remote_copy(
            src_ref=srcs[p], dst_ref=dsts[p], send_sem=send_sems.at[len(pieces) * a + k],
            recv_sem=recv_sems.at[len(pieces) * a + k], device_id=(x, y, 1 - c), device_id_type=MESH)
            for k, p in enumerate(pieces)]
    return copies


def _exchange_copies(grads, recvs, send_sems, recv_sems):
    x, y, c, _ = _mesh_pos()
    return [pltpu.make_async_remote_copy(
        src_ref=grads[a].at[:, 1 - c], dst_ref=recvs[a], send_sem=send_sems.at[a],
        recv_sem=recv_sems.at[a], device_id=(x, y, 1 - c), device_id_type=MESH) for a in range(len(grads))]


def _split_start(copies, per, srcs, zones, after, *, name):
    n = len(srcs)
    HBM = pl.BlockSpec(memory_space=pltpu.HBM)
    SEM = pl.BlockSpec(memory_space=pltpu.SEMAPHORE)

    def body(*refs):
        send_sems, recv_sems = refs[2 * n + 1], refs[2 * n + 2]
        for cp in copies(refs[:n], refs[n:2 * n], send_sems, recv_sems):
            cp.start()
        refs[-1][...] = jnp.zeros_like(refs[-1])

    hbm = lambda a: pltpu.HBM(a.shape, a.dtype)
    res = _pcall(body, name=name,
                 out_shape=(pltpu.SemaphoreType.DMA((per * n,)), pltpu.SemaphoreType.DMA((per * n,)),
                            *[hbm(a) for a in srcs], *[hbm(a) for a in zones],
                            jax.ShapeDtypeStruct((SUBLANES, LANES), F32)),
                 in_specs=[*[HBM] * (2 * n), pl.BlockSpec(memory_space=pl.ANY)],
                 out_specs=(SEM, SEM, *[HBM] * (2 * n), pl.BlockSpec(memory_space=pltpu.VMEM)),
                 aliases={i: 2 + i for i in range(2 * n)}, split_copy=True)(
        *[pltpu.with_memory_space_constraint(a, pltpu.HBM) for a in [*srcs, *zones]], after)
    return res[0], res[1], list(res[2:2 + n]), list(res[2 + n:2 + 2 * n]), res[-1]


def _split_wait(copies, send_sems, recv_sems, srcs, zones, after, *, name):
    n = len(srcs)
    HBM = pl.BlockSpec(memory_space=pltpu.HBM)
    SEM = pl.BlockSpec(memory_space=pltpu.SEMAPHORE)

    def body(*refs):
        for cp in copies(refs[:n], refs[n:2 * n], refs[2 * n], refs[2 * n + 1]):
            cp.wait_send()
            cp.wait_recv()

    hbm = lambda a: pltpu.HBM(a.shape, a.dtype)
    res = _pcall(body, name=name, out_shape=(*[hbm(a) for a in srcs], *[hbm(a) for a in zones]),
                 in_specs=[*[HBM] * (2 * n), SEM, SEM, pl.BlockSpec(memory_space=pl.ANY)],
                 out_specs=tuple([HBM] * (2 * n)), aliases={i: i for i in range(2 * n)},
                 split_copy=True)(*srcs, *zones, send_sems, recv_sems, after)
    return list(res[:n]), list(res[n:])


def _add_halves(grad, recv, c_idx, *, name):
    S, _, h, C = grad.shape
    th, tc = _blk(h, C)

    def body(c_ref, g_ref, r_ref, o_ref):
        o_ref[...] = (g_ref[...].astype(F32) + r_ref[...].astype(F32)).astype(o_ref.dtype)

    return _pcall(body, name=name, out_shape=jax.ShapeDtypeStruct((S, h, C), grad.dtype),
                  grid=(S, h // th, C // tc), prefetch=1,
                  in_specs=[pl.BlockSpec((None, None, th, tc), lambda s, i, j, c: (s, c[0], i, j)),
                            pl.BlockSpec((None, th, tc), lambda s, i, j, c: (s, i, j))],
                  out_specs=pl.BlockSpec((None, th, tc), lambda s, i, j, c: (s, i, j)),
                  semantics=("parallel", "parallel", "parallel"))(c_idx, grad, recv)


def _scatter_copies(srcs, lands, send_sems, recv_sems):
    x, y, c, others = _mesh_pos()
    return [pltpu.make_async_remote_copy(
        src_ref=srcs[a].at[2 * chip[0] + chip[1]], dst_ref=lands[a].at[j],
        send_sem=send_sems.at[3 * a + j], recv_sem=recv_sems.at[3 * a + j],
        device_id=(*chip, c), device_id_type=MESH)
        for a in range(len(srcs)) for j, chip in enumerate(others)]


def _add_chips(sums, recv, chip_idx, *, name):
    _, h, C = sums.shape
    th, tc = _blk(h, C)

    def body(k_ref, s_ref, r_ref, o_ref):
        acc = s_ref[...].astype(F32) + r_ref[0].astype(F32)
        acc = acc + r_ref[1].astype(F32)
        o_ref[...] = acc + r_ref[2].astype(F32)

    return _pcall(body, name=name, out_shape=jax.ShapeDtypeStruct((h, C), F32),
                  grid=(h // th, C // tc), prefetch=1,
                  in_specs=[pl.BlockSpec((None, th, tc), lambda i, j, k: (k[0], i, j)),
                            pl.BlockSpec((3, th, tc), lambda i, j, k: (0, i, j))],
                  out_specs=pl.BlockSpec((th, tc), lambda i, j, k: (i, j)),
                  semantics=("parallel", "parallel"))(chip_idx, sums, recv)


def _swap_copies(halves, others, send_sems, recv_sems):
    x, y, c, _ = _mesh_pos()
    return [pltpu.make_async_remote_copy(
        src_ref=halves[a], dst_ref=others[a], send_sem=send_sems.at[a], recv_sem=recv_sems.at[a],
        device_id=(x, y, 1 - c), device_id_type=MESH) for a in range(len(halves))]


def _all_reduce_small(buf):
    R, L = buf.shape
    NDEV = 8

    def body(x_ref, sum_ref, all_ref, send_sems, recv_sems, local_sem):
        x, y, c, others = _mesh_pos()
        me, sibling = (x, y, c), (x, y, 1 - c)

        def slot(px, py, pc):
            return all_ref.at[4 * px + 2 * py + pc]

        def copy(k, block, to, src=None):
            return pltpu.make_async_remote_copy(
                src_ref=slot(*block) if src is None else src, dst_ref=slot(*block),
                send_sem=send_sems.at[k], recv_sem=recv_sems.at[k], device_id=to, device_id_type=MESH)

        mine = pltpu.make_async_copy(x_ref, slot(*me), local_sem)
        mine.start()
        first = [copy(0, me, sibling, src=x_ref)]
        first += [copy(1 + j, me, (*chip, c), src=x_ref) for j, chip in enumerate(others)]
        for cp in first:
            cp.start()
        passed = [copy(4 + j, (*chip, c), sibling) for j, chip in enumerate(others)]
        for j, chip in enumerate(others):
            copy(1 + j, (*chip, c), me).wait_recv()
            passed[j].start()
        copy(0, sibling, me).wait_recv()
        for j, chip in enumerate(others):
            copy(4 + j, (*chip, 1 - c), me).wait_recv()
        for cp in first + passed:
            cp.wait_send()
        mine.wait()
        acc = all_ref[0]
        for d in range(1, NDEV):
            acc = acc + all_ref[d]
        sum_ref[...] = acc

    VM = pl.BlockSpec(memory_space=pltpu.VMEM)
    return _pcall(body, name="all_reduce_small",
                  out_shape=(jax.ShapeDtypeStruct((R, L), F32), jax.ShapeDtypeStruct((NDEV, R, L), F32)),
                  in_specs=[VM], out_specs=(VM, VM),
                  scratch_shapes=[pltpu.SemaphoreType.DMA((7,)), pltpu.SemaphoreType.DMA((7,)),
                                  pltpu.SemaphoreType.DMA])(buf)[0]


def _pack(arrs, rows_multiple=16):
    flat = [a.reshape(-1).astype(F32) for a in arrs]
    sizes = [f.shape[0] for f in flat]
    total = sum(sizes)
    per = LANES * rows_multiple
    padded = -(-total // per) * per
    flat.append(jnp.zeros((padded - total,), F32))
    offs = [0]
    for s in sizes:
        offs.append(offs[-1] + s)
    return jnp.concatenate(flat).reshape(padded // LANES, LANES), offs


def _unpack(buf, offs, shapes):
    flat = buf.reshape(-1)
    return [flat[offs[i]:offs[i + 1]].reshape(s) for i, s in enumerate(shapes)]


def kernel(x, mem, g_mix, w_in, w_a2, b_a, g_gla, w_pool, pool_scale, w_branch, w_out, g_cross, g_mem, w_cq, w_ckv, w_co, g_ffn, w_up, conv_w, conv_b, w_down, g_final, loss_target, m_g_mix, m_w_in, m_w_a2, m_b_a, m_g_gla, m_w_pool, m_pool_scale, m_w_branch, m_w_out, m_g_cross, m_g_mem, m_w_cq, m_w_ckv, m_w_co, m_g_ffn, m_w_up, m_conv_w, m_conv_b, m_w_down, m_g_final, v_g_mix, v_w_in, v_w_a2, v_b_a, v_g_gla, v_w_pool, v_pool_scale, v_w_branch, v_w_out, v_g_cross, v_g_mem, v_w_cq, v_w_ckv, v_w_co, v_g_ffn, v_w_up, v_conv_w, v_conv_b, v_w_down, v_g_final):
    weights = dict(g_mix=g_mix, w_in=w_in, w_a2=w_a2, b_a=b_a, g_gla=g_gla, w_pool=w_pool,
                   pool_scale=pool_scale, w_branch=w_branch, w_out=w_out, g_cross=g_cross, g_mem=g_mem,
                   w_cq=w_cq, w_ckv=w_ckv, w_co=w_co, g_ffn=g_ffn, w_up=w_up, conv_w=conv_w,
                   conv_b=conv_b, w_down=w_down, g_final=g_final)
    mom_m = dict(g_mix=m_g_mix, w_in=m_w_in, w_a2=m_w_a2, b_a=m_b_a, g_gla=m_g_gla, w_pool=m_w_pool,
                 pool_scale=m_pool_scale, w_branch=m_w_branch, w_out=m_w_out, g_cross=m_g_cross,
                 g_mem=m_g_mem, w_cq=m_w_cq, w_ckv=m_w_ckv, w_co=m_w_co, g_ffn=m_g_ffn, w_up=m_w_up,
                 conv_w=m_conv_w, conv_b=m_conv_b, w_down=m_w_down, g_final=m_g_final)
    mom_v = dict(g_mix=v_g_mix, w_in=v_w_in, w_a2=v_w_a2, b_a=v_b_a, g_gla=v_g_gla, w_pool=v_w_pool,
                 pool_scale=v_pool_scale, w_branch=v_w_branch, w_out=v_w_out, g_cross=v_g_cross,
                 g_mem=v_g_mem, w_cq=v_w_cq, w_ckv=v_w_ckv, w_co=v_w_co, g_ffn=v_g_ffn, w_up=v_w_up,
                 conv_w=v_conv_w, conv_b=v_conv_b, w_down=v_w_down, g_final=v_g_final)
    order = list(weights)
    big = ["w_in", "w_branch", "w_out", "w_cq", "w_ckv", "w_co", "w_up", "w_down"]
    small_sharded = ["w_a2", "w_pool", "conv_w"]
    small_repl = ["g_mix", "b_a", "g_gla", "pool_scale", "g_cross", "g_mem", "g_ffn", "conv_b", "g_final"]

    xs, ms, tgt = x[0], mem[0], loss_target[0]
    T, D = xs.shape
    M = ms.shape[0]
    DK, DV, PW = b_a.shape[1], g_gla.shape[1], pool_scale.shape[1]
    RANK = w_a2.shape[1]
    F2 = conv_b.shape[1]
    F = F2 // 2
    DIN = N_CHIPS * w_in.shape[2]
    OFF_A = 2 * DK + 2 * DV
    OFF_P = OFF_A + RANK
    RP = LANES
    GW = PW // POOL_GROUPS
    assert PW == DV and 4 * DV == 2 * D and OFF_P + PW + 2 * D == DIN

    cx, cy, cc = lax.axis_index("x"), lax.axis_index("y"), lax.axis_index("c")
    chip = 2 * cx + cy
    c_idx = jnp.reshape(cc, (1,)).astype(jnp.int32)
    chip_idx = jnp.reshape(chip, (1,)).astype(jnp.int32)

    def halves(a):
        return a.reshape(2, a.shape[0] // 2, a.shape[1])

    shard2d = {k: (weights[k][0].T if k == "w_in" else weights[k][0]) for k in big}
    small_pack, small_offs = _pack([weights[k][0] for k in small_sharded], rows_multiple=32)
    flying, passing = {}, {}
    tok = xs
    for group, keys in (("in", ["w_in"]), ("mix", ["w_branch", "w_out", "small"]),
                        ("cross", ["w_cq", "w_ckv", "w_co"]), ("up", ["w_up"]), ("down", ["w_down"])):
        srcs = [small_pack if k == "small" else shard2d[k].astype(BF16) for k in keys]
        if group != "in":
            srcs = [a + tok[0:1, 0:1].astype(a.dtype) for a in srcs]
        srcs = [halves(a) for a in srcs]
        zones = [lax.empty((N_CHIPS, *s.shape), s.dtype) for s in srcs]
        first = (_near_copies, 2) if group == "in" else (_gather_copies, 3)
        s_sems, r_sems, srcs, zones, tok = _split_start(*first, srcs, zones, tok, name=f"gather_start_{group}")
        flying[group] = (keys, s_sems, r_sems, srcs, zones)

    def arrive_in(after):
        keys, s_sems, r_sems, srcs, zones = flying["in"]
        near, diag = functools.partial(_pass_copies, pieces=(0, 1, 3)), functools.partial(_pass_copies, pieces=(2,))
        srcs, zones = _split_wait(_near_copies, s_sems, r_sems, srcs, zones, after, name="gather_wait_in")
        rs, rr, srcs, zones, _ = _split_start(_relay_copies, 2, srcs, zones, after, name="gather_relay_start_in")
        ns, nr, srcs, zones, _ = _split_start(near, 3, srcs, zones, after, name="gather_pass_near_start_in")
        srcs, zones = _split_wait(_relay_copies, rs, rr, srcs, zones, after, name="gather_relay_wait_in")
        ds, dr, srcs, zones, _ = _split_start(diag, 1, srcs, zones, after, name="gather_pass_diag_start_in")
        srcs, zones = _split_wait(near, ns, nr, srcs, zones, after, name="gather_pass_near_wait_in")
        _, full = _split_wait(diag, ds, dr, srcs, zones, after, name="gather_pass_diag_wait_in")
        return {k: f.reshape(N_CHIPS, f.shape[1] * f.shape[2], f.shape[3]) for k, f in zip(keys, full)}

    def landed(group, after):
        keys, s_sems, r_sems, srcs, zones = flying[group]
        srcs, zones = _split_wait(_gather_copies, s_sems, r_sems, srcs, zones, after,
                                  name=f"gather_wait_{group}")
        s_sems, r_sems, srcs, zones, token = _split_start(_pass_copies, 4, srcs, zones, after,
                                                          name=f"gather_pass_start_{group}")
        passing[group] = (keys, s_sems, r_sems, srcs, zones)
        return token

    def arrive(group, after):
        keys, s_sems, r_sems, srcs, zones = passing[group]
        _, full = _split_wait(_pass_copies, s_sems, r_sems, srcs, zones, after,
                              name=f"gather_pass_wait_{group}")
        return {k: f.reshape(N_CHIPS, f.shape[1] * f.shape[2], f.shape[3]) for k, f in zip(keys, full)}

    def rows(g):
        return g.reshape(-1, g.shape[2])

    h1, r1 = _rms_fwd(xs, g_mix + tok[0:1, 0:1], name="norm_mix")
    W_in = rows(arrive_in(h1)["w_in"])
    W_main = jnp.concatenate([W_in[:OFF_A], W_in[OFF_P:]], axis=0)
    W_a = jnp.pad(W_in[OFF_A:OFF_P], ((0, RP - RANK), (0, 0)))
    tok = landed("mix", W_a)
    proj = _mm(h1, W_main, "nt", name="proj_main", out_dtype=F32, after=tok)
    gw = arrive("mix", proj)
    W_branch, W_out, small_all = rows(gw["w_branch"]), rows(gw["w_out"]), gw["small"]
    sm = [_unpack(small_all[j], small_offs, [weights[k].shape[1:] for k in small_sharded]) for j in range(N_CHIPS)]
    W_a2 = jnp.concatenate([sm[j][0] for j in range(N_CHIPS)], axis=1)
    W_a2p = jnp.pad(W_a2, ((0, RP - RANK), (0, 0))).astype(BF16)
    W_pool = jnp.concatenate([sm[j][1] for j in range(N_CHIPS)], axis=1).astype(BF16)
    W_conv = jnp.concatenate([sm[j][2] for j in range(N_CHIPS)], axis=1)

    a_pad = _mm(h1, W_a, "nt", name="proj_gate_rank", out_dtype=F32)
    o_gla, o_raw, states = _gla_fwd(proj, a_pad, W_a2p, b_a, g_gla, T=T, DK=DK, DV=DV)
    o_pool = _pool_fwd(proj, W_pool, pool_scale, T=T, PW=PW, col_block=3)
    tok = landed("cross", o_pool)
    y_gla = _mm(o_gla, W_branch, "nn", name="branch_gla", out_dtype=BF16, K=DV, after=tok)
    y_pool = _mm(o_pool, W_branch, "nn", name="branch_pool", out_dtype=BF16, K=PW, b_off=(DV, 0))
    merged = _merge_fwd(y_gla, y_pool, proj, T=T, D=D, col_block=2)
    x1 = _mm(merged, W_out, "nn", name="mix_out", out_dtype=F32, add=xs)

    h2, r2 = _rms_fwd(x1, g_cross, name="norm_cross")
    mem_n, rm = _rms_fwd(ms, g_mem, name="norm_mem")
    gw = arrive("cross", h2)
    W_cq, W_ckv, W_co = rows(gw["w_cq"]), gw["w_ckv"], rows(gw["w_co"])
    qc = _mm(h2, W_cq, "nn", name="cross_q", out_dtype=BF16)
    kv = _mm(mem_n, W_ckv, "nn", name="cross_kv", out_dtype=BF16, b_blocked=True)
    o_att = _attn_fwd(qc, kv, T=T, D=D, M=M)
    x2 = _mm(o_att, W_co, "nn", name="cross_out", out_dtype=F32, add=x1)

    tok = landed("up", x2)
    h3, r3 = _rms_fwd(x2, g_ffn + tok[0:1, 0:1], name="norm_ffn")
    W_up = arrive("up", h3)["w_up"]
    u0 = _mm(h3, W_up, "nn", name="ffn_up", out_dtype=F32, b_blocked=True)
    tok = landed("down", u0)
    f_act = _conv_fwd(u0, W_conv, conv_b + tok[0:1, 0:1], T=T, F=F)
    W_down = rows(arrive("down", f_act)["w_down"])
    x3 =_mm(f_act, W_down, "nn", name="ffn_down", out_dtype=F32, add=x2)

    loss_part, dx3, dx3_b, dg_final = _loss_head(x3, g_final.reshape(1, D), tgt)

    def col_shards(g):
        nb, K, Nb = g.shape
        return g.reshape(nb, 2, K // 2, Nb)

    def row_shards(g):
        R, N = g.shape
        return g.reshape(N_CHIPS, 2, R // N_CHIPS // 2, N)

    exchanging, in_flight = {}, []

    def exchange_start(group, keys, partials, after):
        recvs = [lax.empty((p.shape[0], *p.shape[2:]), p.dtype) for p in partials]
        s_sems, r_sems, partials, recvs, token = _split_start(
            _exchange_copies, 1, partials, recvs, after, name=f"grad_exchange_start_{group}")
        exchanging[group] = (keys, s_sems, r_sems, partials, recvs)
        return token

    def scatter_start(group, after):
        keys, s_sems, r_sems, partials, recvs = exchanging[group]
        partials, recvs = _split_wait(_exchange_copies, s_sems, r_sems, partials, recvs, after,
                                      name=f"grad_exchange_wait_{group}")
        chip_sums = [_add_halves(p, r, c_idx, name=f"grad_add_halves_{k}")
                     for k, p, r in zip(keys, partials, recvs)]
        lands = [lax.empty((3, *s.shape[1:]), s.dtype) for s in chip_sums]
        s_sems, r_sems, sums, lands, token = _split_start(
            _scatter_copies, 3, chip_sums, lands, after, name=f"grad_scatter_start_{group}")
        in_flight.append((group, keys, s_sems, r_sems, sums, lands))
        return token

    collected = []

    def collect(after):
        group, keys, s_sems, r_sems, sums, lands = in_flight.pop(0)
        sums, from_chips = _split_wait(_scatter_copies, s_sems, r_sems, sums, lands, after,
                                       name=f"grad_scatter_wait_{group}")
        half_sums = [_add_chips(s, r, chip_idx, name=f"grad_add_chips_{k}") for k, s, r in zip(keys, sums, from_chips)]
        others = [lax.empty(h.shape, h.dtype) for h in half_sums]
        s_sems, r_sems, half_sums, others, token = _split_start(
            _swap_copies, 1, half_sums, others, after, name=f"grad_swap_start_{group}")
        collected.append((keys, s_sems, r_sems, half_sums, others))
        return token

    df = _mm(dx3_b, W_down, "nt", name="d_ffn_act", out_dtype=BF16)
    dW_down = _mm(f_act, dx3_b, "tn", name="dw_down", out_dtype=BF16)
    du0, dconv_w, dconv_b = _conv_bwd(u0, W_conv, conv_b, df, T=T, F=F)
    dh3 = _mm(du0, W_up, "nt", name="d_ffn_in", out_dtype=F32, b_blocked=True, tk=F2 // N_CHIPS)
    dW_up = _mm(h3, du0, "tn", name="dw_up", out_dtype=BF16, out_blocks=N_CHIPS)
    tok = exchange_start("ffn", ["w_down", "w_up"], [row_shards(dW_down), col_shards(dW_up)], dh3)
    dx2, dx2_b, dg_ffn = _rms_bwd(dh3, x2, r3 + tok[0:1, 0:1], g_ffn, dx3, name="norm_ffn_bwd")

    do_att = _mm(dx2_b, W_co, "nt", name="d_cross_o", out_dtype=BF16)
    dW_co = _mm(o_att, dx2_b, "tn", name="dw_co", out_dtype=BF16)
    tok = scatter_start("ffn", dW_co)
    dq, dkv = _attn_bwd(qc, kv, do_att, T=T, D=D, M=M)
    dkv_b = dkv.astype(BF16)
    dW_cq = _mm(h2, dq, "tn", name="dw_cq", out_dtype=BF16, after=tok)
    dh2 = _mm(dq, W_cq, "nt", name="d_cross_in", out_dtype=F32)
    dW_ckv = _mm(mem_n, dkv_b, "tn", name="dw_ckv", out_dtype=BF16, out_blocks=N_CHIPS)
    dmem_n = _mm(dkv_b, W_ckv, "nt", name="d_mem", out_dtype=F32, b_blocked=True)
    tok = exchange_start("cross", ["w_co", "w_cq", "w_ckv"],
                         [row_shards(dW_co), row_shards(dW_cq), col_shards(dW_ckv)], dmem_n)
    _, _, dg_mem = _rms_bwd(dmem_n, ms, rm, g_mem, None, name="norm_mem_bwd")
    dx1, dx1_b, dg_cross = _rms_bwd(dh2, x1, r2 + tok[0:1, 0:1], g_cross, dx2, name="norm_cross_bwd")

    dmerged = _mm(dx1_b, W_out, "nt", name="d_merged", out_dtype=BF16)
    dW_out = _mm(merged, dx1_b, "tn", name="dw_out", out_dtype=BF16)
    tok = scatter_start("cross", dW_out)
    dy_gla, dy_pool, dgates = _merge_bwd(dmerged, y_gla, y_pool, proj, T=T, D=D, col_block=2)
    dW_br_gla = _mm(o_gla, dy_gla, "tn", name="dw_branch_gla", out_dtype=BF16, after=tok)
    dW_br_pool = _mm(o_pool, dy_pool, "tn", name="dw_branch_pool", out_dtype=BF16)
    do_gla = _mm(dy_gla, W_branch, "nt", name="d_o_gla", out_dtype=F32, N=DV)
    do_pool = _mm(dy_pool, W_branch, "nt", name="d_o_pool", out_dtype=F32, N=PW, b_off=(DV, 0))
    dp, dw_pool, dpool_scale = _pool_bwd(proj, W_pool, pool_scale, do_pool, T=T, PW=PW, col_block=3)
    dW_pool = jnp.transpose(dw_pool.reshape(POOL_GROUPS, N_CHIPS, GW // N_CHIPS, GW), (1, 0, 2, 3))
    tok = exchange_start("mix", ["w_out", "w_branch", "w_pool"],
                         [row_shards(dW_out), row_shards(jnp.concatenate([dW_br_gla, dW_br_pool], axis=0)),
                          row_shards(dW_pool.reshape(N_CHIPS * POOL_GROUPS * (GW // N_CHIPS), GW).astype(BF16))],
                         dp)
    dqkvr, da_pad, dw2, db_a, dg_gla = _gla_bwd(proj, a_pad, W_a2p, b_a + tok[0:1, 0:1], g_gla, o_raw, states,
                                               do_gla, T=T, DK=DK, DV=DV)
    tok = scatter_start("mix", dqkvr)
    tok = collect(tok)
    dproj = jnp.concatenate([dqkvr, dp, dgates], axis=1)
    dW_main = _mm(dproj, h1, "tn", name="dw_in_main", out_dtype=BF16, after=tok)
    dW_a = _mm(da_pad, h1, "tn", name="dw_in_rank", out_dtype=BF16)
    dW_in = jnp.concatenate([dW_main[:OFF_A], dW_a[:RANK], dW_main[OFF_A:]], axis=0)
    tok = exchange_start("in", ["w_in"], [row_shards(dW_in)], dW_a)
    tok = collect(tok)
    dh1 = _mm(dproj, W_main, "nn", name="d_mix_in_main", out_dtype=F32, after=tok)
    dh1 = _mm(da_pad, W_a, "nn", name="d_mix_in_rank", out_dtype=F32, add=dh1)
    dx0, _, dg_mix = _rms_bwd(dh1, xs, r1, g_mix, dx1, name="norm_mix_bwd")

    grads = {}

    small_grads = [loss_part, dg_mix, db_a, dg_gla, dpool_scale, dg_cross, dg_mem, dg_ffn, dconv_b, dg_final,
                   dw2[:RANK], dconv_w]
    small_buf, offs = _pack(small_grads)
    small_sum = _all_reduce_small(small_buf)
    red = _unpack(small_sum, offs, [g.shape for g in small_grads])
    loss = red[0][0, 0]
    for k, g in zip(small_repl, red[1:10]):
        grads[k] = g.reshape(weights[k].shape)
    nb = DK // N_CHIPS
    grads["w_a2"] = lax.dynamic_slice_in_dim(red[10], chip * nb, nb, axis=1)[None]
    nb = F2 // N_CHIPS
    grads["conv_w"] = lax.dynamic_slice_in_dim(red[11], chip * nb, nb, axis=1)[None]

    delta, new_m, new_v = {}, {}, {}

    def shard_rows(k, a):
        a = a[0]
        return a.T if k == "w_in" else a.reshape(-1, a.shape[-1])

    def whole(k, a):
        a = a.reshape(-1, a.shape[2])
        return (a.T if k == "w_in" else a).reshape(weights[k].shape)

    scatter_start("in", small_sum)

    def finish(after):
        keys, s_sems, r_sems, mine, others = collected.pop(0)
        mine, others = _split_wait(_swap_copies, s_sems, r_sems, mine, others, after,
                                   name=f"grad_swap_wait_{keys[0]}")
        for k, g_mine, g_other in zip(keys, mine, others):
            wmv = [halves(shard_rows(k, src[k])) for src in (weights, mom_m, mom_v)]
            res = _adamw_halves(*wmv, g_mine, g_other, c_idx, name=f"adamw_{k}")
            grads[k], delta[k], new_m[k], new_v[k] = (whole(k, a) for a in res)
        return res[1]

    after = in_flight[-1][4][0]
    while in_flight:
        after = collect(after)
        while len(collected) > 1:
            after = finish(after)
    finish(after)
    small = small_repl + ["w_a2", "conv_w"]
    packs = [_pack([src[k] for k in small])[0] for src in (weights, grads, mom_m, mom_v)]
    _, offs = _pack([weights[k] for k in small])
    outs = _adamw(*packs, name="adamw_small")
    for res, o in zip((delta, new_m, new_v), outs):
        for k, a in zip(small, _unpack(o, offs, [weights[k].shape for k in small])):
            res[k] = a

    return (loss, dx0[None], *[grads[k] for k in order], *[delta[k] for k in order],
            *[new_m[k] for k in order], *[new_v[k] for k in order])
```

```python
import functools

import jax
import jax.numpy as jnp
from jax import lax
from jax.experimental import pallas as pl
from jax.experimental.pallas import tpu as pltpu

F32 = jnp.float32
BF16 = jnp.bfloat16
MESH = pl.DeviceIdType.MESH
HIGHEST = lax.Precision.HIGHEST

EPS = 1e-6
GLA_HEADS = 4
GLA_CHUNK = 64
GLA_GATE_NORM = 16.0
POOL_GROUPS = 4
CROSS_HEADS = 4
CONV_W = 3
N_CHIPS = 4
LANES = 128
SUBLANES = 8
VMEM_LIMIT = 56 << 20

ADAM_LR = 0.001
ADAM_B1 = 0.9
ADAM_B2 = 0.999
ADAM_EPS = 1e-08
ADAM_WD = 0.01
ADAM_STEP = 10

NN = (((1,), (0,)), ((), ()))
NT = (((1,), (1,)), ((), ()))
TN = (((0,), (0,)), ((), ()))


ONE_PASS = lax.Precision.HIGH


def _dot(a, b, dn=NN, precision=None):
    return lax.dot_general(a, b, dn, precision=precision, preferred_element_type=F32)


def _tile(n, pref, align=LANES):
    t = (min(pref, n) // align) * align
    while t >= align:
        if n % t == 0:
            return t
        t -= align
    return n


def _pcall(body, *, name, out_shape, grid=(), in_specs=None, out_specs=None, scratch_shapes=(),
           semantics=None, prefetch=0, aliases=None, split_copy=False):
    params = dict(vmem_limit_bytes=VMEM_LIMIT)
    if semantics is not None:
        params["dimension_semantics"] = semantics
    if split_copy:
        params["has_side_effects"] = pltpu.SideEffectType.DATAFLOW_SIDE_EFFECTING
    if prefetch:
        grid_spec = pltpu.PrefetchScalarGridSpec(
            num_scalar_prefetch=prefetch, grid=grid, in_specs=in_specs, out_specs=out_specs,
            scratch_shapes=scratch_shapes)
        return pl.pallas_call(body, name=name, out_shape=out_shape, grid_spec=grid_spec,
                              compiler_params=pltpu.CompilerParams(**params))
    kw = {}
    if aliases is not None:
        kw["input_output_aliases"] = aliases
    if in_specs is not None:
        kw["in_specs"] = in_specs
    if out_specs is not None:
        kw["out_specs"] = out_specs
    return pl.pallas_call(body, name=name, out_shape=out_shape, grid=grid,
                          scratch_shapes=scratch_shapes,
                          compiler_params=pltpu.CompilerParams(**params), **kw)


def _sigmoid(x):
    return 1.0 / (1.0 + jnp.exp(-x))


def _log_sigmoid(x):
    return jnp.minimum(x, 0.0) - jnp.log(1.0 + jnp.exp(-jnp.abs(x)))


def _mm(a, b, mode, *, name, out_dtype, M=None, N=None, K=None, a_off=(0, 0), b_off=(0, 0),
        add=None, b_blocked=False, out_blocks=0, after=None, tm=1536, tn=1536, tk=2048):
    if b_blocked:
        nb, R, Cb = b.shape
        b_rows, b_cols = R, nb * Cb
    else:
        b_rows, b_cols = b.shape
    if mode == "nn":
        M = M or a.shape[0]; K = K or a.shape[1]; N = N or b_cols
    elif mode == "nt":
        M = M or a.shape[0]; K = K or a.shape[1]; N = N or b_rows
    else:
        K = K or a.shape[0]; M = M or a.shape[1]; N = N or b_cols
    tm = _tile(M, tm, LANES if mode == "tn" else 16)
    tn = _tile(Cb if (b_blocked and mode != "nt") else (N // out_blocks if out_blocks else N), tn)
    tk = _tile(Cb if (b_blocked and mode == "nt") else K, tk)
    nk = K // tk
    dn = {"nn": NN, "nt": NT, "tn": TN}[mode]

    def off(o, t):
        assert o % t == 0, (name, o, t)
        return o // t

    if mode == "tn":
        ar, ac = off(a_off[0], tk), off(a_off[1], tm)
        a_spec = pl.BlockSpec((tk, tm), lambda i, j, k: (k + ar, i + ac))
    else:
        ar, ac = off(a_off[0], tm), off(a_off[1], tk)
        a_spec = pl.BlockSpec((tm, tk), lambda i, j, k: (i + ar, k + ac))
    if b_blocked and mode == "nt":
        per = Cb // tk
        b_spec = pl.BlockSpec((None, tn, tk), lambda i, j, k: (k // per, j, k % per))
    elif b_blocked:
        per = Cb // tn
        b_spec = pl.BlockSpec((None, tk, tn), lambda i, j, k: (j // per, k, j % per))
    elif mode == "nt":
        br, bc = off(b_off[0], tn), off(b_off[1], tk)
        b_spec = pl.BlockSpec((tn, tk), lambda i, j, k: (j + br, k + bc))
    else:
        br, bc = off(b_off[0], tk), off(b_off[1], tn)
        b_spec = pl.BlockSpec((tk, tn), lambda i, j, k: (k + br, j + bc))
    if out_blocks:
        per_o = N // out_blocks // tn
        o_spec = pl.BlockSpec((None, tm, tn), lambda i, j, k: (j // per_o, i, j % per_o))
        out_shape = jax.ShapeDtypeStruct((out_blocks, M, N // out_blocks), out_dtype)
    else:
        o_spec = pl.BlockSpec((tm, tn), lambda i, j, k: (i, j))
        out_shape = jax.ShapeDtypeStruct((M, N), out_dtype)
    in_specs = [a_spec, b_spec]
    args = [a, b]
    if add is not None:
        assert not out_blocks
        in_specs.append(o_spec)
        args.append(add)
    if after is not None:
        in_specs.append(pl.BlockSpec(memory_space=pl.ANY))
        args.append(after)
    n_in = len(args)

    def finish(r, refs):
        if add is not None:
            r = r + refs[2][...]
        o_ref = refs[n_in]
        o_ref[...] = r.astype(o_ref.dtype)

    def body_one(*refs):
        finish(_dot(refs[0][...].astype(BF16), refs[1][...].astype(BF16), dn), refs)

    def body_acc(*refs):
        acc_ref = refs[-1]
        k = pl.program_id(2)

        @pl.when(k == 0)
        def _():
            acc_ref[...] = jnp.zeros_like(acc_ref)

        acc_ref[...] += _dot(refs[0][...].astype(BF16), refs[1][...].astype(BF16), dn)

        @pl.when(k == nk - 1)
        def _():
            finish(acc_ref[...], refs)

    return _pcall(body_one if nk == 1 else body_acc, name=name, out_shape=out_shape,
                  grid=(M // tm, N // tn, nk), in_specs=in_specs, out_specs=o_spec,
                  scratch_shapes=[] if nk == 1 else [pltpu.VMEM((tm, tn), F32)],
                  semantics=("parallel", "parallel", "arbitrary"))(*args)


def _rms_fwd(x, g, *, name):
    T, D = x.shape
    tr = _tile(T, 128, 16)

    def body(x_ref, g_ref, h_ref, r_ref):
        xv = x_ref[...]
        r = lax.rsqrt(jnp.mean(xv * xv, axis=-1, keepdims=True) + EPS)
        h_ref[...] = (xv * r * g_ref[...]).astype(h_ref.dtype)
        r_ref[...] = r

    row = pl.BlockSpec((tr, D), lambda i: (i, 0))
    return _pcall(body, name=name,
                  out_shape=(jax.ShapeDtypeStruct((T, D), BF16), jax.ShapeDtypeStruct((T, 1), F32)),
                  grid=(T // tr,),
                  in_specs=[row, pl.BlockSpec((1, D), lambda i: (0, 0))],
                  out_specs=(row, pl.BlockSpec((tr, 1), lambda i: (i, 0))),
                  semantics=("parallel",))(x, g)


def _rms_bwd(dh, x, rstd, g, dres, *, name):
    T, D = x.shape
    tr = _tile(T, 128, 16)
    has_res = dres is not None

    def body(*refs):
        if has_res:
            dh_ref, x_ref, r_ref, g_ref, res_ref, dx_ref, dxb_ref, dg_ref = refs
        else:
            dh_ref, x_ref, r_ref, g_ref, dx_ref, dxb_ref, dg_ref = refs
        r = r_ref[...]
        xh = x_ref[...] * r
        dhv = dh_ref[...].astype(F32)
        dxh = dhv * g_ref[...]
        m = jnp.mean(dxh * xh, axis=-1, keepdims=True)
        dx = r * (dxh - xh * m)
        if has_res:
            dx = dx + res_ref[...]
        dx_ref[...] = dx
        dxb_ref[...] = dx.astype(BF16)

        @pl.when(pl.program_id(0) == 0)
        def _():
            dg_ref[...] = jnp.zeros_like(dg_ref)

        dg_ref[...] += jnp.sum(dhv * xh, axis=0, keepdims=True)

    row = pl.BlockSpec((tr, D), lambda i: (i, 0))
    vec = pl.BlockSpec((1, D), lambda i: (0, 0))
    in_specs = [row, row, pl.BlockSpec((tr, 1), lambda i: (i, 0)), vec]
    args = [dh, x, rstd, g]
    if has_res:
        in_specs.append(row)
        args.append(dres)
    return _pcall(body, name=name,
                  out_shape=(jax.ShapeDtypeStruct((T, D), F32), jax.ShapeDtypeStruct((T, D), BF16),
                             jax.ShapeDtypeStruct((1, D), F32)),
                  grid=(T // tr,), in_specs=in_specs, out_specs=(row, row, vec),
                  semantics=("arbitrary",))(*args)


def _loss_head(x3, g, tgt):
    T, D = x3.shape
    tr = _tile(T, 128, 16)

    def body(x_ref, g_ref, t_ref, loss_ref, dx_ref, dxb_ref, dg_ref):
        xv = x_ref[...]
        gv = g_ref[...]
        r = lax.rsqrt(jnp.mean(xv * xv, axis=-1, keepdims=True) + EPS)
        xh = xv * r
        err = xh * gv - t_ref[...]
        dy = err * (1.0 / D)
        dxh = dy * gv
        m = jnp.mean(dxh * xh, axis=-1, keepdims=True)
        dx = r * (dxh - xh * m)
        dx_ref[...] = dx
        dxb_ref[...] = dx.astype(BF16)

        @pl.when(pl.program_id(0) == 0)
        def _():
            dg_ref[...] = jnp.zeros_like(dg_ref)
            loss_ref[...] = jnp.zeros_like(loss_ref)

        dg_ref[...] += jnp.sum(dy * xh, axis=0, keepdims=True)
        part = 0.5 * jnp.sum(jnp.mean(err * err, axis=-1, keepdims=True), axis=0, keepdims=True)
        loss_ref[...] += jnp.broadcast_to(part, loss_ref.shape)

    row = pl.BlockSpec((tr, D), lambda i: (i, 0))
    vec = pl.BlockSpec((1, D), lambda i: (0, 0))
    return _pcall(body, name="loss_head",
                  out_shape=(jax.ShapeDtypeStruct((1, LANES), F32), jax.ShapeDtypeStruct((T, D), F32),
                             jax.ShapeDtypeStruct((T, D), BF16), jax.ShapeDtypeStruct((1, D), F32)),
                  grid=(T // tr,), in_specs=[row, vec, row],
                  out_specs=(pl.BlockSpec((1, LANES), lambda i: (0, 0)), row, row, vec),
                  semantics=("arbitrary",))(x3, g, tgt)


def _gla_chunk_terms(qk, a_ref, w2_ref, ba_ref, DK):
    C = qk.shape[0]
    gp = _dot(a_ref[...].astype(BF16), w2_ref[...]) + ba_ref[...]
    la = _log_sigmoid(gp) * (1.0 / GLA_GATE_NORM)
    row = lax.broadcasted_iota(jnp.int32, (C, C), 0)
    col = lax.broadcasted_iota(jnp.int32, (C, C), 1)
    causal = row >= col
    b = _dot(causal.astype(F32), la, precision=HIGHEST)
    return gp, b, causal


def _gla_fwd(proj, a_pad, w2, b_a, g_gla, *, T, DK, DV):
    assert 2 * DK == DV
    H = GLA_HEADS
    HK, HV = DK // H, DV // H
    C = GLA_CHUNK
    n = T // C
    RP = a_pad.shape[1]
    scale = HK ** -0.5

    def body(qk_ref, v_ref, r_ref, a_ref, w2_ref, ba_ref, gg_ref, og_ref, oraw_ref, st_ref, s_ref):
        @pl.when(pl.program_id(0) == 0)
        def _():
            s_ref[...] = jnp.zeros_like(s_ref)

        st_ref[...] = s_ref[...]
        qk = qk_ref[...]
        _, b, causal = _gla_chunk_terms(qk, a_ref, w2_ref, ba_ref, DK)
        for h in range(H):
            ks = slice(h * HK, (h + 1) * HK)
            vs = slice(h * HV, (h + 1) * HV)
            bh = b[:, ks]
            b_last = bh[C - 1:C, :]
            qt = qk[:, ks] * scale * jnp.exp(bh)
            kh = qk[:, DK + h * HK:DK + (h + 1) * HK]
            kt = kh * jnp.exp(-bh)
            khat = kh * jnp.exp(b_last - bh)
            a_mat = jnp.where(causal, _dot(qt, kt, NT, ONE_PASS), 0.0)
            vh = v_ref[:, vs]
            s_t = s_ref[h]
            o = _dot(a_mat, vh, NN, ONE_PASS) + _dot(qt, s_t, NT, ONE_PASS)
            s_ref[h] = s_t * jnp.exp(b_last) + _dot(vh, khat, TN, ONE_PASS)
            rs = lax.rsqrt(jnp.mean(o * o, axis=-1, keepdims=True) + EPS)
            rr = r_ref[:, vs]
            og = o * rs * gg_ref[:, vs] * (rr * _sigmoid(rr))
            oraw_ref[:, vs] = o
            og_ref[:, vs] = og.astype(BF16)

    blk = lambda j: pl.BlockSpec((C, DV), lambda i: (i, j))
    full = lambda s: pl.BlockSpec(s, lambda i: (0,) * len(s))
    return _pcall(
        body, name="gla_fwd",
        out_shape=(jax.ShapeDtypeStruct((T, DV), BF16), jax.ShapeDtypeStruct((T, DV), F32),
                   jax.ShapeDtypeStruct((n, H, HV, HK), F32)),
        grid=(n,),
        in_specs=[blk(0), blk(1), blk(2), pl.BlockSpec((C, RP), lambda i: (i, 0)),
                  full((RP, DK)), full((1, DK)), full((1, DV))],
        out_specs=(blk(0), blk(0), pl.BlockSpec((None, H, HV, HK), lambda i: (i, 0, 0, 0))),
        scratch_shapes=[pltpu.VMEM((H, HV, HK), F32)],
        semantics=("arbitrary",))(proj, proj, proj, a_pad, w2, b_a, g_gla)


def _gla_bwd(proj, a_pad, w2, b_a, g_gla, o_raw, states, do_gla, *, T, DK, DV):
    H = GLA_HEADS
    HK, HV = DK // H, DV // H
    C = GLA_CHUNK
    n = T // C
    RP = a_pad.shape[1]
    scale = HK ** -0.5

    def body(qk_ref, v_ref, r_ref, a_ref, w2_ref, ba_ref, gg_ref, oraw_ref, st_ref, dog_ref,
             dqkvr_ref, da_ref, dw2_ref, dba_ref, dgg_ref, ds_ref):
        @pl.when(pl.program_id(0) == 0)
        def _():
            ds_ref[...] = jnp.zeros_like(ds_ref)
            dw2_ref[...] = jnp.zeros_like(dw2_ref)
            dba_ref[...] = jnp.zeros_like(dba_ref)
            dgg_ref[...] = jnp.zeros_like(dgg_ref)

        qk = qk_ref[...]
        gp, b, causal = _gla_chunk_terms(qk, a_ref, w2_ref, ba_ref, DK)
        row = lax.broadcasted_iota(jnp.int32, (C, C), 0)
        col = lax.broadcasted_iota(jnp.int32, (C, C), 1)
        upper = (col >= row).astype(F32)
        dla_parts = []
        for h in range(H):
            ks = slice(h * HK, (h + 1) * HK)
            vs = slice(h * HV, (h + 1) * HV)
            bh = b[:, ks]
            b_last = bh[C - 1:C, :]
            eb = jnp.exp(bh)
            emb = jnp.exp(-bh)
            ehat = jnp.exp(b_last - bh)
            e_last = jnp.exp(b_last)
            qt = qk[:, ks] * scale * eb
            kh = qk[:, DK + h * HK:DK + (h + 1) * HK]
            kt = kh * emb
            khat = kh * ehat
            a_mat = jnp.where(causal, _dot(qt, kt, NT, ONE_PASS), 0.0)
            vh = v_ref[:, vs]
            o = oraw_ref[:, vs]
            rs = lax.rsqrt(jnp.mean(o * o, axis=-1, keepdims=True) + EPS)
            on = o * rs
            gg = gg_ref[:, vs]
            rr = r_ref[:, vs]
            sg = _sigmoid(rr)
            d_out = dog_ref[:, vs]
            dr = d_out * (on * gg) * (sg * (1.0 + rr * (1.0 - sg)))
            d_og = d_out * (rr * sg)
            dgg_ref[:, vs] += jnp.sum(d_og * on, axis=0, keepdims=True)
            d_on = d_og * gg
            d_o = rs * (d_on - on * jnp.mean(d_on * on, axis=-1, keepdims=True))
            s_t = st_ref[h]
            ds_t = ds_ref[h]
            d_a = jnp.where(causal, _dot(d_o, vh, NT, ONE_PASS), 0.0)
            dv = _dot(a_mat, d_o, TN, ONE_PASS) + _dot(khat, ds_t, NT, ONE_PASS)
            dqt = _dot(d_a, kt, NN, ONE_PASS) + _dot(d_o, s_t, NN, ONE_PASS)
            dkt = _dot(d_a, qt, TN, ONE_PASS)
            dkhat = _dot(vh, ds_t, NN, ONE_PASS)
            ds_ref[h] = ds_t * e_last + _dot(d_o, qt, TN, ONE_PASS)
            dq = dqt * eb * scale
            dk = dkt * emb + dkhat * ehat
            db = dqt * qt - dkt * kt - dkhat * khat
            d_last = (jnp.sum(dkhat * khat, axis=0, keepdims=True)
                      + e_last * jnp.sum(ds_t * s_t, axis=0, keepdims=True))
            dla_parts.append(_dot(upper, db, NN, HIGHEST) + d_last)
            dqkvr_ref[:, ks] = dq.astype(BF16)
            dqkvr_ref[:, DK + h * HK:DK + (h + 1) * HK] = dk.astype(BF16)
            dqkvr_ref[:, DV + h * HV:DV + (h + 1) * HV] = dv.astype(BF16)
            dqkvr_ref[:, 2 * DV + h * HV:2 * DV + (h + 1) * HV] = dr.astype(BF16)
        dla = jnp.concatenate(dla_parts, axis=1)
        dgp = dla * (1.0 / GLA_GATE_NORM) * _sigmoid(-gp)
        dba_ref[...] += jnp.sum(dgp, axis=0, keepdims=True)
        dgp_b = dgp.astype(BF16)
        dw2_ref[...] += _dot(a_ref[...].astype(BF16), dgp_b, TN)
        da_ref[...] = _dot(dgp_b, w2_ref[...], NT).astype(BF16)

    rev = lambda j: pl.BlockSpec((C, DV), lambda i: (n - 1 - i, j))
    full = lambda s: pl.BlockSpec(s, lambda i: (0,) * len(s))
    return _pcall(
        body, name="gla_bwd",
        out_shape=(jax.ShapeDtypeStruct((T, 3 * DV), BF16), jax.ShapeDtypeStruct((T, RP), BF16),
                   jax.ShapeDtypeStruct((RP, DK), F32), jax.ShapeDtypeStruct((1, DK), F32),
                   jax.ShapeDtypeStruct((1, DV), F32)),
        grid=(n,),
        in_specs=[rev(0), rev(1), rev(2), pl.BlockSpec((C, RP), lambda i: (n - 1 - i, 0)),
                  full((RP, DK)), full((1, DK)), full((1, DV)), rev(0),
                  pl.BlockSpec((None, H, HV, HK), lambda i: (n - 1 - i, 0, 0, 0)), rev(0)],
        out_specs=(pl.BlockSpec((C, 3 * DV), lambda i: (n - 1 - i, 0)),
                   pl.BlockSpec((C, RP), lambda i: (n - 1 - i, 0)),
                   full((RP, DK)), full((1, DK)), full((1, DV))),
        scratch_shapes=[pltpu.VMEM((H, HV, HK), F32)],
        semantics=("arbitrary",))(proj, proj, proj, a_pad, w2, b_a, g_gla, o_raw, states, do_gla)


def _pool_windows(p, g, T):
    t = lax.broadcasted_iota(jnp.int32, (T, 1), 0)
    s = p
    for lvl in range(POOL_GROUPS):
        sh = 1 << lvl
        nxt = s + jnp.where(t >= sh, pltpu.roll(s, sh, 0), 0.0)
        s = jnp.where(lvl <= g, nxt, s)
    win = jnp.left_shift(2, g)
    inv = 1.0 / jnp.minimum(t + 1, win).astype(F32)
    return s * inv - p, inv


def _pool_fwd(proj, w_pool, scale, *, T, PW, col_block):
    GW = PW // POOL_GROUPS
    per = PW // GW

    def body(p_ref, w_ref, s_ref, o_ref):
        g = pl.program_id(0)
        pooled, _ = _pool_windows(p_ref[...], g, T)
        mixed = _dot(pooled.astype(BF16), w_ref[...])
        o_ref[...] = (mixed * s_ref[...]).astype(BF16)

    return _pcall(body, name="pool_fwd", out_shape=jax.ShapeDtypeStruct((T, PW), BF16),
                  grid=(POOL_GROUPS,),
                  in_specs=[pl.BlockSpec((T, GW), lambda g: (0, col_block * per + g)),
                            pl.BlockSpec((None, GW, GW), lambda g: (g, 0, 0)),
                            pl.BlockSpec((1, GW), lambda g: (0, g))],
                  out_specs=pl.BlockSpec((T, GW), lambda g: (0, g)),
                  semantics=("parallel",))(proj, w_pool, scale)


def _pool_bwd(proj, w_pool, scale, do_pool, *, T, PW, col_block):
    GW = PW // POOL_GROUPS
    per = PW // GW

    def body(p_ref, w_ref, s_ref, do_ref, dp_ref, dw_ref, dsc_ref):
        g = pl.program_id(0)
        pooled, inv = _pool_windows(p_ref[...], g, T)
        pooled_b = pooled.astype(BF16)
        w = w_ref[...]
        mixed = _dot(pooled_b, w)
        d_out = do_ref[...]
        dsc_ref[...] = jnp.sum(d_out * mixed, axis=0, keepdims=True)
        dmixed = (d_out * s_ref[...]).astype(BF16)
        dw_ref[...] = _dot(pooled_b, dmixed, TN)
        dpooled = _dot(dmixed, w, NT)
        t = lax.broadcasted_iota(jnp.int32, (T, 1), 0)
        s = dpooled * inv
        for lvl in range(POOL_GROUPS):
            sh = 1 << lvl
            nxt = s + jnp.where(t < T - sh, pltpu.roll(s, T - sh, 0), 0.0)
            s = jnp.where(lvl <= g, nxt, s)
        dp_ref[...] = (s - dpooled).astype(BF16)

    return _pcall(body, name="pool_bwd",
                  out_shape=(jax.ShapeDtypeStruct((T, PW), BF16),
                             jax.ShapeDtypeStruct((POOL_GROUPS, GW, GW), F32),
                             jax.ShapeDtypeStruct((1, PW), F32)),
                  grid=(POOL_GROUPS,),
                  in_specs=[pl.BlockSpec((T, GW), lambda g: (0, col_block * per + g)),
                            pl.BlockSpec((None, GW, GW), lambda g: (g, 0, 0)),
                            pl.BlockSpec((1, GW), lambda g: (0, g)),
                            pl.BlockSpec((T, GW), lambda g: (0, g))],
                  out_specs=(pl.BlockSpec((T, GW), lambda g: (0, g)),
                             pl.BlockSpec((None, GW, GW), lambda g: (g, 0, 0)),
                             pl.BlockSpec((1, GW), lambda g: (0, g))),
                  semantics=("parallel",))(proj, w_pool, scale, do_pool)


def _merge_fwd(y_gla, y_pool, proj, *, T, D, col_block):
    tr = _tile(T, 128, 16)

    def body(yg_ref, yp_ref, g1_ref, g2_ref, o_ref):
        o_ref[...] = (_sigmoid(g1_ref[...]) * yg_ref[...]
                      + _sigmoid(g2_ref[...]) * yp_ref[...]).astype(BF16)

    row = pl.BlockSpec((tr, D), lambda i: (i, 0))
    return _pcall(body, name="merge_fwd", out_shape=jax.ShapeDtypeStruct((T, D), BF16),
                  grid=(T // tr,),
                  in_specs=[row, row, pl.BlockSpec((tr, D), lambda i: (i, col_block)),
                            pl.BlockSpec((tr, D), lambda i: (i, col_block + 1))],
                  out_specs=row, semantics=("parallel",))(y_gla, y_pool, proj, proj)


def _merge_bwd(dmerged, y_gla, y_pool, proj, *, T, D, col_block):
    tr = _tile(T, 128, 16)

    def body(dm_ref, yg_ref, yp_ref, g1_ref, g2_ref, dyg_ref, dyp_ref, dg_ref):
        dm = dm_ref[...]
        s1 = _sigmoid(g1_ref[...])
        s2 = _sigmoid(g2_ref[...])
        dyg_ref[...] = (dm * s1).astype(BF16)
        dyp_ref[...] = (dm * s2).astype(BF16)
        dg_ref[:, :D] = (dm * yg_ref[...] * s1 * (1.0 - s1)).astype(BF16)
        dg_ref[:, D:] = (dm * yp_ref[...] * s2 * (1.0 - s2)).astype(BF16)

    row = pl.BlockSpec((tr, D), lambda i: (i, 0))
    return _pcall(body, name="merge_bwd",
                  out_shape=(jax.ShapeDtypeStruct((T, D), BF16), jax.ShapeDtypeStruct((T, D), BF16),
                             jax.ShapeDtypeStruct((T, 2 * D), BF16)),
                  grid=(T // tr,),
                  in_specs=[row, row, row, pl.BlockSpec((tr, D), lambda i: (i, col_block)),
                            pl.BlockSpec((tr, D), lambda i: (i, col_block + 1))],
                  out_specs=(row, row, pl.BlockSpec((tr, 2 * D), lambda i: (i, 0))),
                  semantics=("parallel",))(dmerged, y_gla, y_pool, proj, proj)


def _attn_fwd(q, kv, *, T, D, M):
    H = CROSS_HEADS
    HD = D // H
    tq = _tile(T, 512, 16)
    scale = HD ** -0.5

    def body(q_ref, kv_ref, o_ref):
        for h in range(H):
            hs = slice(h * HD, (h + 1) * HD)
            s = _dot(q_ref[:, hs], kv_ref[:, hs], NT) * scale
            e = jnp.exp(s - jnp.max(s, axis=-1, keepdims=True))
            p = e / jnp.sum(e, axis=-1, keepdims=True)
            o_ref[:, hs] = _dot(p.astype(BF16), kv_ref[:, D + h * HD:D + (h + 1) * HD]).astype(BF16)

    row = pl.BlockSpec((tq, D), lambda i: (i, 0))
    return _pcall(body, name="attn_fwd", out_shape=jax.ShapeDtypeStruct((T, D), BF16),
                  grid=(T // tq,), in_specs=[row, pl.BlockSpec((M, 2 * D), lambda i: (0, 0))],
                  out_specs=row, semantics=("parallel",))(q, kv)


def _attn_bwd(q, kv, do, *, T, D, M):
    H = CROSS_HEADS
    HD = D // H
    tq = _tile(T, 512, 16)
    scale = HD ** -0.5

    def body(q_ref, kv_ref, do_ref, dq_ref, dkv_ref):
        @pl.when(pl.program_id(0) == 0)
        def _():
            dkv_ref[...] = jnp.zeros_like(dkv_ref)

        for h in range(H):
            hs = slice(h * HD, (h + 1) * HD)
            vs = slice(D + h * HD, D + (h + 1) * HD)
            qh = q_ref[:, hs]
            kh = kv_ref[:, hs]
            s = _dot(qh, kh, NT) * scale
            e = jnp.exp(s - jnp.max(s, axis=-1, keepdims=True))
            p = e / jnp.sum(e, axis=-1, keepdims=True)
            p_b = p.astype(BF16)
            d_o = do_ref[:, hs]
            dkv_ref[:, vs] += _dot(p_b, d_o, TN)
            dp = _dot(d_o, kv_ref[:, vs], NT)
            ds = (p * (dp - jnp.sum(dp * p, axis=-1, keepdims=True)) * scale).astype(BF16)
            dq_ref[:, hs] = _dot(ds, kh).astype(BF16)
            dkv_ref[:, hs] += _dot(ds, qh, TN)

    row = pl.BlockSpec((tq, D), lambda i: (i, 0))
    full = pl.BlockSpec((M, 2 * D), lambda i: (0, 0))
    return _pcall(body, name="attn_bwd",
                  out_shape=(jax.ShapeDtypeStruct((T, D), BF16), jax.ShapeDtypeStruct((M, 2 * D), F32)),
                  grid=(T // tq,), in_specs=[row, full, row], out_specs=(row, full),
                  semantics=("arbitrary",))(q, kv, do)


def _shift_down(x, halo, s):
    out = pltpu.roll(x, s, 0)
    t8 = lax.broadcasted_iota(jnp.int32, (SUBLANES, 1), 0)
    head = out[:SUBLANES]
    for j in range(s):
        head = jnp.where(t8 == j, halo[SUBLANES - s + j:SUBLANES - s + j + 1, :], head)
    return head if x.shape[0] == SUBLANES else jnp.concatenate([head, out[SUBLANES:]], axis=0)


def _shift_up(x, halo, s):
    rows = x.shape[0]
    out = pltpu.roll(x, rows - s, 0)
    t8 = lax.broadcasted_iota(jnp.int32, (SUBLANES, 1), 0)
    tail = out[rows - SUBLANES:]
    for j in range(s):
        tail = jnp.where(t8 == SUBLANES - s + j, halo[j:j + 1, :], tail)
    return jnp.concatenate([out[:rows - SUBLANES], tail], axis=0)


def _conv_tiles(T):
    tt = _tile(T, 128, SUBLANES)
    return tt, tt // SUBLANES, T // SUBLANES


def _conv_fwd(u0, conv_w, conv_b, *, T, F):
    tt, hb, _ = _conv_tiles(T)
    cw = _tile(F, LANES)

    def body(u_ref, prev_ref, w_ref, b_ref, f_ref):
        i = pl.program_id(0)

        def conv(cs):
            x = u_ref[:, cs]
            halo = jnp.where(i > 0, prev_ref[:, cs], 0.0)
            return (w_ref[2:3, cs] * x + w_ref[1:2, cs] * _shift_down(x, halo, 1)
                    + w_ref[0:1, cs] * _shift_down(x, halo, 2) + b_ref[:, cs])

        for j in range(F // cw):
            gate = conv(slice(j * cw, (j + 1) * cw))
            val = conv(slice(F + j * cw, F + (j + 1) * cw))
            f_ref[:, j * cw:(j + 1) * cw] = (gate * _sigmoid(gate) * val).astype(BF16)

    return _pcall(body, name="conv_fwd", out_shape=jax.ShapeDtypeStruct((T, F), BF16),
                  grid=(T // tt,),
                  in_specs=[pl.BlockSpec((tt, 2 * F), lambda i: (i, 0)),
                            pl.BlockSpec((SUBLANES, 2 * F), lambda i: (jnp.maximum(i * hb - 1, 0), 0)),
                            pl.BlockSpec((CONV_W, 2 * F), lambda i: (0, 0)),
                            pl.BlockSpec((1, 2 * F), lambda i: (0, 0))],
                  out_specs=pl.BlockSpec((tt, F), lambda i: (i, 0)),
                  semantics=("parallel",))(u0, u0, conv_w, conv_b)


def _conv_bwd(u0, conv_w, conv_b, df, *, T, F):
    tt, hb, nb = _conv_tiles(T)
    nt = T // tt
    cw = _tile(F, LANES)

    def body(u_ref, prev_ref, next_ref, df_ref, dfn_ref, w_ref, b_ref, du0_ref, dw_ref, db_ref):
        i = pl.program_id(0)

        @pl.when(i == 0)
        def _():
            dw_ref[...] = jnp.zeros_like(dw_ref)
            db_ref[...] = jnp.zeros_like(db_ref)

        def conv(cs):
            x = u_ref[:, cs]
            halo = jnp.where(i > 0, prev_ref[:, cs], 0.0)
            x1 = _shift_down(x, halo, 1)
            x2 = _shift_down(x, halo, 2)
            u = w_ref[2:3, cs] * x + w_ref[1:2, cs] * x1 + w_ref[0:1, cs] * x2 + b_ref[:, cs]
            xn = next_ref[:, cs]
            tail = x[tt - SUBLANES:, :]
            un = (w_ref[2:3, cs] * xn + w_ref[1:2, cs] * _shift_down(xn, tail, 1)
                  + w_ref[0:1, cs] * _shift_down(xn, tail, 2) + b_ref[:, cs])
            return u, un, (x, x1, x2)

        def glu_grad(gate, val, dff):
            sg = _sigmoid(gate)
            return dff * val * (sg * (1.0 + gate * (1.0 - sg))), dff * (gate * sg)

        def finish(cs, du, dun, xs):
            du0 = (w_ref[2:3, cs] * du + w_ref[1:2, cs] * _shift_up(du, dun, 1)
                   + w_ref[0:1, cs] * _shift_up(du, dun, 2))
            du0_ref[:, cs] = du0.astype(BF16)
            db_ref[:, cs] += jnp.sum(du, axis=0, keepdims=True)
            dw_ref[2:3, cs] += jnp.sum(du * xs[0], axis=0, keepdims=True)
            dw_ref[1:2, cs] += jnp.sum(du * xs[1], axis=0, keepdims=True)
            dw_ref[0:1, cs] += jnp.sum(du * xs[2], axis=0, keepdims=True)

        for j in range(F // cw):
            fs = slice(j * cw, (j + 1) * cw)
            gs, vs = fs, slice(F + j * cw, F + (j + 1) * cw)
            ug, ung, xg = conv(gs)
            uv, unv, xv = conv(vs)
            dug, duv = glu_grad(ug, uv, df_ref[:, fs].astype(F32))
            dung, dunv = glu_grad(ung, unv, dfn_ref[0:SUBLANES, fs].astype(F32))
            dung = jnp.where(i < nt - 1, dung, 0.0)
            dunv = jnp.where(i < nt - 1, dunv, 0.0)
            finish(gs, dug, dung, xg)
            finish(vs, duv, dunv, xv)

    wide = lambda rows, fn: pl.BlockSpec((rows, 2 * F), fn)
    nxt = lambda i: (jnp.minimum((i + 1) * hb, nb - 1), 0)
    return _pcall(body, name="conv_bwd",
                  out_shape=(jax.ShapeDtypeStruct((T, 2 * F), BF16),
                             jax.ShapeDtypeStruct((CONV_W, 2 * F), F32),
                             jax.ShapeDtypeStruct((1, 2 * F), F32)),
                  grid=(nt,),
                  in_specs=[wide(tt, lambda i: (i, 0)),
                            wide(SUBLANES, lambda i: (jnp.maximum(i * hb - 1, 0), 0)),
                            wide(SUBLANES, nxt),
                            pl.BlockSpec((tt, F), lambda i: (i, 0)),
                            pl.BlockSpec((2 * SUBLANES, F),
                                         lambda i: (jnp.minimum((i + 1) * (hb // 2), nb // 2 - 1), 0)),
                            wide(CONV_W, lambda i: (0, 0)), wide(1, lambda i: (0, 0))],
                  out_specs=(wide(tt, lambda i: (i, 0)), wide(CONV_W, lambda i: (0, 0)),
                             wide(1, lambda i: (0, 0))),
                  semantics=("arbitrary",))(u0, u0, u0, df, df, conv_w, conv_b)


def _adamw(w, g, m, v, *, name):
    R, C = w.shape
    tr = _tile(R, max(SUBLANES, (1 << 19) // max(C, 1) // SUBLANES * SUBLANES), SUBLANES)
    c1 = 1.0 / (1.0 - ADAM_B1 ** ADAM_STEP)
    c2 = 1.0 / (1.0 - ADAM_B2 ** ADAM_STEP)

    def body(w_ref, g_ref, m_ref, v_ref, d_ref, mo_ref, vo_ref):
        gv = g_ref[...]
        mn = ADAM_B1 * m_ref[...] + (1.0 - ADAM_B1) * gv
        vn = ADAM_B2 * v_ref[...] + (1.0 - ADAM_B2) * (gv * gv)
        d_ref[...] = -ADAM_LR * ((mn * c1) / (jnp.sqrt(vn * c2) + ADAM_EPS) + ADAM_WD * w_ref[...])
        mo_ref[...] = mn
        vo_ref[...] = vn

    blk = pl.BlockSpec((tr, C), lambda i: (i, 0))
    shp = jax.ShapeDtypeStruct((R, C), F32)
    return _pcall(body, name=name, out_shape=(shp, shp, shp), grid=(R // tr,),
                  in_specs=[blk] * 4, out_specs=(blk,) * 3, semantics=("parallel",))(w, g, m, v)


def _blk(h, C, elems=1 << 19, align=16):
    th = _tile(h, max(align, elems // C // align * align), align)
    if th < h or h * C <= 2 * elems:
        return th, C
    return h, _tile(C, max(LANES, elems // h // LANES * LANES))


def _adamw_halves(w, m, v, g_mine, g_other, c_idx, *, name):
    _, h, C = w.shape
    th, tc = _blk(h, C, align=SUBLANES)
    c1 = 1.0 / (1.0 - ADAM_B1 ** ADAM_STEP)
    c2 = 1.0 / (1.0 - ADAM_B2 ** ADAM_STEP)

    def body(c_ref, w_ref, m_ref, v_ref, gm_ref, go_ref, g_ref, d_ref, mo_ref, vo_ref):
        gv = jnp.where(pl.program_id(0) == c_ref[0], gm_ref[...], go_ref[...])
        mn = ADAM_B1 * m_ref[...] + (1.0 - ADAM_B1) * gv
        vn = ADAM_B2 * v_ref[...] + (1.0 - ADAM_B2) * (gv * gv)
        d_ref[...] = -ADAM_LR * ((mn * c1) / (jnp.sqrt(vn * c2) + ADAM_EPS) + ADAM_WD * w_ref[...])
        g_ref[...] = gv
        mo_ref[...] = mn
        vo_ref[...] = vn

    blk = pl.BlockSpec((None, th, tc), lambda s, i, j, c: (s, i, j))

    def pick(mine):
        def index(s, i, j, c):
            use = (s == c[0]) if mine else (s != c[0])
            return jnp.where(use, i, 0), jnp.where(use, j, 0)
        return pl.BlockSpec((th, tc), index)

    shp = jax.ShapeDtypeStruct((2, h, C), F32)
    return _pcall(body, name=name, out_shape=(shp,) * 4, grid=(2, h // th, C // tc), prefetch=1,
                  in_specs=[blk, blk, blk, pick(True), pick(False)], out_specs=(blk,) * 4,
                  semantics=("parallel", "parallel", "parallel"))(c_idx, w, m, v, g_mine, g_other)


def _mesh_pos():
    x, y, c = lax.axis_index("x"), lax.axis_index("y"), lax.axis_index("c")
    others = [(1 - x, y), (x, 1 - y), (1 - x, 1 - y)]
    return x, y, c, others


def _gather_copies(shards, lands, send_sems, recv_sems):
    x, y, c, others = _mesh_pos()
    me = 2 * x + y
    return [pltpu.make_async_remote_copy(
        src_ref=shards[a].at[c], dst_ref=lands[a].at[me, c],
        send_sem=send_sems.at[3 * a + j], recv_sem=recv_sems.at[3 * a + j],
        device_id=(*chip, c), device_id_type=MESH)
        for a in range(len(shards)) for j, chip in enumerate(others)]


def _near_copies(shards, lands, send_sems, recv_sems):
    x, y, c, others = _mesh_pos()
    me = 2 * x + y
    return [pltpu.make_async_remote_copy(
        src_ref=shards[a].at[c], dst_ref=lands[a].at[me, c],
        send_sem=send_sems.at[2 * a + j], recv_sem=recv_sems.at[2 * a + j],
        device_id=(*chip, c), device_id_type=MESH)
        for a in range(len(shards)) for j, chip in enumerate(others[:2])]


def _relay_copies(shards, zones, send_sems, recv_sems):
    x, y, c, others = _mesh_pos()
    (nx, ny), copies = others[:2], []
    for a in range(len(zones)):
        hc = zones[a].shape[-1] // 2
        for k, (src_chip, to, lo) in enumerate(((ny, nx, 0), (nx, ny, hc))):
            part = zones[a].at[2 * src_chip[0] + src_chip[1], c, :, pl.ds(lo, hc)]
            copies.append(pltpu.make_async_remote_copy(
                src_ref=part, dst_ref=part, send_sem=send_sems.at[2 * a + k], recv_sem=recv_sems.at[2 * a + k],
                device_id=(*to, c), device_id_type=MESH))
    return copies


def _pass_copies(shards, zones, send_sems, recv_sems, pieces=(0, 1, 2, 3)):
    x, y, c, others = _mesh_pos()
    me = 2 * x + y
    copies = []
    for a in range(len(shards)):
        srcs = [zones[a].at[2 * chip[0] + chip[1], c] for chip in others] + [shards[a]]
        dsts = [zones[a].at[2 * chip[0] + chip[1], c] for chip in others] + [zones[a].at[me]]
        copies += [pltpu.make_async_remote_copy(
            src_ref=srcs[p], dst_ref=dsts[p], send_sem=send_sems.at[len(pieces) * a + k],
            recv_sem=recv_sems.at[len(pieces) * a + k], device_id=(x, y, 1 - c), device_id_type=MESH)
            for k, p in enumerate(pieces)]
    return copies


def _exchange_copies(grads, recvs, send_sems, recv_sems):
    x, y, c, _ = _mesh_pos()
    return [pltpu.make_async_remote_copy(
        src_ref=grads[a].at[:, 1 - c], dst_ref=recvs[a], send_sem=send_sems.at[a],
        recv_sem=recv_sems.at[a], device_id=(x, y, 1 - c), device_id_type=MESH) for a in range(len(grads))]


def _split_start(copies, per, srcs, zones, after, *, name):
    n = len(srcs)
    HBM = pl.BlockSpec(memory_space=pltpu.HBM)
    SEM = pl.BlockSpec(memory_space=pltpu.SEMAPHORE)

    def body(*refs):
        send_sems, recv_sems = refs[2 * n + 1], refs[2 * n + 2]
        for cp in copies(refs[:n], refs[n:2 * n], send_sems, recv_sems):
            cp.start()
        refs[-1][...] = jnp.zeros_like(refs[-1])

    hbm = lambda a: pltpu.HBM(a.shape, a.dtype)
    res = _pcall(body, name=name,
                 out_shape=(pltpu.SemaphoreType.DMA((per * n,)), pltpu.SemaphoreType.DMA((per * n,)),
                            *[hbm(a) for a in srcs], *[hbm(a) for a in zones],
                            jax.ShapeDtypeStruct((SUBLANES, LANES), F32)),
                 in_specs=[*[HBM] * (2 * n), pl.BlockSpec(memory_space=pl.ANY)],
                 out_specs=(SEM, SEM, *[HBM] * (2 * n), pl.BlockSpec(memory_space=pltpu.VMEM)),
                 aliases={i: 2 + i for i in range(2 * n)}, split_copy=True)(
        *[pltpu.with_memory_space_constraint(a, pltpu.HBM) for a in [*srcs, *zones]], after)
    return res[0], res[1], list(res[2:2 + n]), list(res[2 + n:2 + 2 * n]), res[-1]


def _split_wait(copies, send_sems, recv_sems, srcs, zones, after, *, name):
    n = len(srcs)
    HBM = pl.BlockSpec(memory_space=pltpu.HBM)
    SEM = pl.BlockSpec(memory_space=pltpu.SEMAPHORE)

    def body(*refs):
        for cp in copies(refs[:n], refs[n:2 * n], refs[2 * n], refs[2 * n + 1]):
            cp.wait_send()
            cp.wait_recv()

    hbm = lambda a: pltpu.HBM(a.shape, a.dtype)
    res = _pcall(body, name=name, out_shape=(*[hbm(a) for a in srcs], *[hbm(a) for a in zones]),
                 in_specs=[*[HBM] * (2 * n), SEM, SEM, pl.BlockSpec(memory_space=pl.ANY)],
                 out_specs=tuple([HBM] * (2 * n)), aliases={i: i for i in range(2 * n)},
                 split_copy=True)(*srcs, *zones, send_sems, recv_sems, after)
    return list(res[:n]), list(res[n:])


def _add_halves(grad, recv, c_idx, *, name):
    S, _, h, C = grad.shape
    th, tc = _blk(h, C)

    def body(c_ref, g_ref, r_ref, o_ref):
        o_ref[...] = (g_ref[...].astype(F32) + r_ref[...].astype(F32)).astype(o_ref.dtype)

    return _pcall(body, name=name, out_shape=jax.ShapeDtypeStruct((S, h, C), grad.dtype),
                  grid=(S, h // th, C // tc), prefetch=1,
                  in_specs=[pl.BlockSpec((None, None, th, tc), lambda s, i, j, c: (s, c[0], i, j)),
                            pl.BlockSpec((None, th, tc), lambda s, i, j, c: (s, i, j))],
                  out_specs=pl.BlockSpec((None, th, tc), lambda s, i, j, c: (s, i, j)),
                  semantics=("parallel", "parallel", "parallel"))(c_idx, grad, recv)


def _scatter_copies(srcs, lands, send_sems, recv_sems):
    x, y, c, others = _mesh_pos()
    return [pltpu.make_async_remote_copy(
        src_ref=srcs[a].at[2 * chip[0] + chip[1]], dst_ref=lands[a].at[j],
        send_sem=send_sems.at[3 * a + j], recv_sem=recv_sems.at[3 * a + j],
        device_id=(*chip, c), device_id_type=MESH)
        for a in range(len(srcs)) for j, chip in enumerate(others)]


def _add_chips(sums, recv, chip_idx, *, name):
    _, h, C = sums.shape
    th, tc = _blk(h, C)

    def body(k_ref, s_ref, r_ref, o_ref):
        acc = s_ref[...].astype(F32) + r_ref[0].astype(F32)
        acc = acc + r_ref[1].astype(F32)
        o_ref[...] = acc + r_ref[2].astype(F32)

    return _pcall(body, name=name, out_shape=jax.ShapeDtypeStruct((h, C), F32),
                  grid=(h // th, C // tc), prefetch=1,
                  in_specs=[pl.BlockSpec((None, th, tc), lambda i, j, k: (k[0], i, j)),
                            pl.BlockSpec((3, th, tc), lambda i, j, k: (0, i, j))],
                  out_specs=pl.BlockSpec((th, tc), lambda i, j, k: (i, j)),
                  semantics=("parallel", "parallel"))(chip_idx, sums, recv)


def _swap_copies(halves, others, send_sems, recv_sems):
    x, y, c, _ = _mesh_pos()
    return [pltpu.make_async_remote_copy(
        src_ref=halves[a], dst_ref=others[a], send_sem=send_sems.at[a], recv_sem=recv_sems.at[a],
        device_id=(x, y, 1 - c), device_id_type=MESH) for a in range(len(halves))]


def _all_reduce_small(buf):
    R, L = buf.shape
    NDEV = 8

    def body(x_ref, sum_ref, all_ref, send_sems, recv_sems, local_sem):
        x, y, c, others = _mesh_pos()
        me, sibling = (x, y, c), (x, y, 1 - c)

        def slot(px, py, pc):
            return all_ref.at[4 * px + 2 * py + pc]

        def copy(k, block, to, src=None):
            return pltpu.make_async_remote_copy(
                src_ref=slot(*block) if src is None else src, dst_ref=slot(*block),
                send_sem=send_sems.at[k], recv_sem=recv_sems.at[k], device_id=to, device_id_type=MESH)

        mine = pltpu.make_async_copy(x_ref, slot(*me), local_sem)
        mine.start()
        first = [copy(0, me, sibling, src=x_ref)]
        first += [copy(1 + j, me, (*chip, c), src=x_ref) for j, chip in enumerate(others)]
        for cp in first:
            cp.start()
        passed = [copy(4 + j, (*chip, c), sibling) for j, chip in enumerate(others)]
        for j, chip in enumerate(others):
            copy(1 + j, (*chip, c), me).wait_recv()
            passed[j].start()
        copy(0, sibling, me).wait_recv()
        for j, chip in enumerate(others):
            copy(4 + j, (*chip, 1 - c), me).wait_recv()
        for cp in first + passed:
            cp.wait_send()
        mine.wait()
        acc = all_ref[0]
        for d in range(1, NDEV):
            acc = acc + all_ref[d]
        sum_ref[...] = acc

    VM = pl.BlockSpec(memory_space=pltpu.VMEM)
    return _pcall(body, name="all_reduce_small",
                  out_shape=(jax.ShapeDtypeStruct((R, L), F32), jax.ShapeDtypeStruct((NDEV, R, L), F32)),
                  in_specs=[VM], out_specs=(VM, VM),
                  scratch_shapes=[pltpu.SemaphoreType.DMA((7,)), pltpu.SemaphoreType.DMA((7,)),
                                  pltpu.SemaphoreType.DMA])(buf)[0]


def _pack(arrs, rows_multiple=16):
    flat = [a.reshape(-1).astype(F32) for a in arrs]
    sizes = [f.shape[0] for f in flat]
    total = sum(sizes)
    per = LANES * rows_multiple
    padded = -(-total // per) * per
    flat.append(jnp.zeros((padded - total,), F32))
    offs = [0]
    for s in sizes:
        offs.append(offs[-1] + s)
    return jnp.concatenate(flat).reshape(padded // LANES, LANES), offs


def _unpack(buf, offs, shapes):
    flat = buf.reshape(-1)
    return [flat[offs[i]:offs[i + 1]].reshape(s) for i, s in enumerate(shapes)]


def kernel(x, mem, g_mix, w_in, w_a2, b_a, g_gla, w_pool, pool_scale, w_branch, w_out, g_cross, g_mem, w_cq, w_ckv, w_co, g_ffn, w_up, conv_w, conv_b, w_down, g_final, loss_target, m_g_mix, m_w_in, m_w_a2, m_b_a, m_g_gla, m_w_pool, m_pool_scale, m_w_branch, m_w_out, m_g_cross, m_g_mem, m_w_cq, m_w_ckv, m_w_co, m_g_ffn, m_w_up, m_conv_w, m_conv_b, m_w_down, m_g_final, v_g_mix, v_w_in, v_w_a2, v_b_a, v_g_gla, v_w_pool, v_pool_scale, v_w_branch, v_w_out, v_g_cross, v_g_mem, v_w_cq, v_w_ckv, v_w_co, v_g_ffn, v_w_up, v_conv_w, v_conv_b, v_w_down, v_g_final):
    weights = dict(g_mix=g_mix, w_in=w_in, w_a2=w_a2, b_a=b_a, g_gla=g_gla, w_pool=w_pool,
                   pool_scale=pool_scale, w_branch=w_branch, w_out=w_out, g_cross=g_cross, g_mem=g_mem,
                   w_cq=w_cq, w_ckv=w_ckv, w_co=w_co, g_ffn=g_ffn, w_up=w_up, conv_w=conv_w,
                   conv_b=conv_b, w_down=w_down, g_final=g_final)
    mom_m = dict(g_mix=m_g_mix, w_in=m_w_in, w_a2=m_w_a2, b_a=m_b_a, g_gla=m_g_gla, w_pool=m_w_pool,
                 pool_scale=m_pool_scale, w_branch=m_w_branch, w_out=m_w_out, g_cross=m_g_cross,
                 g_mem=m_g_mem, w_cq=m_w_cq, w_ckv=m_w_ckv, w_co=m_w_co, g_ffn=m_g_ffn, w_up=m_w_up,
                 conv_w=m_conv_w, conv_b=m_conv_b, w_down=m_w_down, g_final=m_g_final)
    mom_v = dict(g_mix=v_g_mix, w_in=v_w_in, w_a2=v_w_a2, b_a=v_b_a, g_gla=v_g_gla, w_pool=v_w_pool,
                 pool_scale=v_pool_scale, w_branch=v_w_branch, w_out=v_w_out, g_cross=v_g_cross,
                 g_mem=v_g_mem, w_cq=v_w_cq, w_ckv=v_w_ckv, w_co=v_w_co, g_ffn=v_g_ffn, w_up=v_w_up,
                 conv_w=v_conv_w, conv_b=v_conv_b, w_down=v_w_down, g_final=v_g_final)
    order = list(weights)
    big = ["w_in", "w_branch", "w_out", "w_cq", "w_ckv", "w_co", "w_up", "w_down"]
    small_sharded = ["w_a2", "w_pool", "conv_w"]
    small_repl = ["g_mix", "b_a", "g_gla", "pool_scale", "g_cross", "g_mem", "g_ffn", "conv_b", "g_final"]

    xs, ms, tgt = x[0], mem[0], loss_target[0]
    T, D = xs.shape
    M = ms.shape[0]
    DK, DV, PW = b_a.shape[1], g_gla.shape[1], pool_scale.shape[1]
    RANK = w_a2.shape[1]
    F2 = conv_b.shape[1]
    F = F2 // 2
    DIN = N_CHIPS * w_in.shape[2]
    OFF_A = 2 * DK + 2 * DV
    OFF_P = OFF_A + RANK
    RP = LANES
    GW = PW // POOL_GROUPS
    assert PW == DV and 4 * DV == 2 * D and OFF_P + PW + 2 * D == DIN

    cx, cy, cc = lax.axis_index("x"), lax.axis_index("y"), lax.axis_index("c")
    chip = 2 * cx + cy
    c_idx = jnp.reshape(cc, (1,)).astype(jnp.int32)
    chip_idx = jnp.reshape(chip, (1,)).astype(jnp.int32)

    def halves(a):
        return a.reshape(2, a.shape[0] // 2, a.shape[1])

    shard2d = {k: (weights[k][0].T if k == "w_in" else weights[k][0]) for k in big}
    small_pack, small_offs = _pack([weights[k][0] for k in small_sharded], rows_multiple=32)
    flying, passing = {}, {}

    def gather_start(group, keys, tok):
        srcs = [small_pack if k == "small" else shard2d[k].astype(BF16) for k in keys]
        if group != "in":
            srcs = [a + tok[0:1, 0:1].astype(a.dtype) for a in srcs]
        srcs = [halves(a) for a in srcs]
        zones = [lax.empty((N_CHIPS, *s.shape), s.dtype) for s in srcs]
        first = (_near_copies, 2) if group == "in" else (_gather_copies, 3)
        s_sems, r_sems, srcs, zones, tok = _split_start(*first, srcs, zones, tok, name=f"gather_start_{group}")
        flying[group] = (keys, s_sems, r_sems, srcs, zones)
        return tok

    tok = gather_start("in", ["w_in"], xs)

    def arrive_in(after):
        keys, s_sems, r_sems, srcs, zones = flying["in"]
        near, diag = functools.partial(_pass_copies, pieces=(0, 1, 3)), functools.partial(_pass_copies, pieces=(2,))
        srcs, zones = _split_wait(_near_copies, s_sems, r_sems, srcs, zones, after, name="gather_wait_in")
        rs, rr, srcs, zones, tok = _split_start(_relay_copies, 2, srcs, zones, after, name="gather_relay_start_in")
        ns, nr, srcs, zones, _ = _split_start(near, 3, srcs, zones, tok, name="gather_pass_near_start_in")
        for group, group_keys in (("mix", ["w_branch", "w_out", "small"]), ("cross", ["w_cq", "w_ckv", "w_co"]),
                                  ("up", ["w_up"]), ("down", ["w_down"])):
            tok = gather_start(group, group_keys, tok)
        after = tok
        srcs, zones = _split_wait(_relay_copies, rs, rr, srcs, zones, after, name="gather_relay_wait_in")
        ds, dr, srcs, zones, _ = _split_start(diag, 1, srcs, zones, after, name="gather_pass_diag_start_in")
        srcs, zones = _split_wait(near, ns, nr, srcs, zones, after, name="gather_pass_near_wait_in")
        _, full = _split_wait(diag, ds, dr, srcs, zones, after, name="gather_pass_diag_wait_in")
        return {k: f.reshape(N_CHIPS, f.shape[1] * f.shape[2], f.shape[3]) for k, f in zip(keys, full)}

    def landed(group, after):
        keys, s_sems, r_sems, srcs, zones = flying[group]
        srcs, zones = _split_wait(_gather_copies, s_sems, r_sems, srcs, zones, after,
                                  name=f"gather_wait_{group}")
        s_sems, r_sems, srcs, zones, token = _split_start(_pass_copies, 4, srcs, zones, after,
                                                          name=f"gather_pass_start_{group}")
        passing[group] = (keys, s_sems, r_sems, srcs, zones)
        return token

    def arrive(group, after):
        keys, s_sems, r_sems, srcs, zones = passing[group]
        _, full = _split_wait(_pass_copies, s_sems, r_sems, srcs, zones, after,
                              name=f"gather_pass_wait_{group}")
        return {k: f.reshape(N_CHIPS, f.shape[1] * f.shape[2], f.shape[3]) for k, f in zip(keys, full)}

    def rows(g):
        return g.reshape(-1, g.shape[2])

    h1, r1 = _rms_fwd(xs, g_mix + tok[0:1, 0:1], name="norm_mix")
    W_in = rows(arrive_in(h1)["w_in"])
    W_main = jnp.concatenate([W_in[:OFF_A], W_in[OFF_P:]], axis=0)
    W_a = jnp.pad(W_in[OFF_A:OFF_P], ((0, RP - RANK), (0, 0)))
    tok = landed("mix", W_a)
    proj = _mm(h1, W_main, "nt", name="proj_main", out_dtype=F32, after=tok)
    gw = arrive("mix", proj)
    W_branch, W_out, small_all = rows(gw["w_branch"]), rows(gw["w_out"]), gw["small"]
    sm = [_unpack(small_all[j], small_offs, [weights[k].shape[1:] for k in small_sharded]) for j in range(N_CHIPS)]
    W_a2 = jnp.concatenate([sm[j][0] for j in range(N_CHIPS)], axis=1)
    W_a2p = jnp.pad(W_a2, ((0, RP - RANK), (0, 0))).astype(BF16)
    W_pool = jnp.concatenate([sm[j][1] for j in range(N_CHIPS)], axis=1).astype(BF16)
    W_conv = jnp.concatenate([sm[j][2] for j in range(N_CHIPS)], axis=1)

    a_pad = _mm(h1, W_a, "nt", name="proj_gate_rank", out_dtype=F32)
    o_gla, o_raw, states = _gla_fwd(proj, a_pad, W_a2p, b_a, g_gla, T=T, DK=DK, DV=DV)
    o_pool = _pool_fwd(proj, W_pool, pool_scale, T=T, PW=PW, col_block=3)
    tok = landed("cross", o_pool)
    y_gla = _mm(o_gla, W_branch, "nn", name="branch_gla", out_dtype=BF16, K=DV, after=tok)
    y_pool = _mm(o_pool, W_branch, "nn", name="branch_pool", out_dtype=BF16, K=PW, b_off=(DV, 0))
    merged = _merge_fwd(y_gla, y_pool, proj, T=T, D=D, col_block=2)
    x1 = _mm(merged, W_out, "nn", name="mix_out", out_dtype=F32, add=xs)

    h2, r2 = _rms_fwd(x1, g_cross, name="norm_cross")
    mem_n, rm = _rms_fwd(ms, g_mem, name="norm_mem")
    gw = arrive("cross", h2)
    W_cq, W_ckv, W_co = rows(gw["w_cq"]), gw["w_ckv"], rows(gw["w_co"])
    qc = _mm(h2, W_cq, "nn", name="cross_q", out_dtype=BF16)
    kv = _mm(mem_n, W_ckv, "nn", name="cross_kv", out_dtype=BF16, b_blocked=True)
    o_att = _attn_fwd(qc, kv, T=T, D=D, M=M)
    x2 = _mm(o_att, W_co, "nn", name="cross_out", out_dtype=F32, add=x1)

    tok = landed("up", x2)
    h3, r3 = _rms_fwd(x2, g_ffn + tok[0:1, 0:1], name="norm_ffn")
    W_up = arrive("up", h3)["w_up"]
    u0 = _mm(h3, W_up, "nn", name="ffn_up", out_dtype=F32, b_blocked=True)
    tok = landed("down", u0)
    f_act = _conv_fwd(u0, W_conv, conv_b + tok[0:1, 0:1], T=T, F=F)
    W_down = rows(arrive("down", f_act)["w_down"])
    x3 =_mm(f_act, W_down, "nn", name="ffn_down", out_dtype=F32, add=x2)

    loss_part, dx3, dx3_b, dg_final = _loss_head(x3, g_final.reshape(1, D), tgt)

    def col_shards(g):
        nb, K, Nb = g.shape
        return g.reshape(nb, 2, K // 2, Nb)

    def row_shards(g):
        R, N = g.shape
        return g.reshape(N_CHIPS, 2, R // N_CHIPS // 2, N)

    exchanging, in_flight = {}, []

    def exchange_start(group, keys, partials, after):
        recvs = [lax.empty((p.shape[0], *p.shape[2:]), p.dtype) for p in partials]
        s_sems, r_sems, partials, recvs, token = _split_start(
            _exchange_copies, 1, partials, recvs, after, name=f"grad_exchange_start_{group}")
        exchanging[group] = (keys, s_sems, r_sems, partials, recvs)
        return token

    def scatter_start(group, after):
        keys, s_sems, r_sems, partials, recvs = exchanging[group]
        partials, recvs = _split_wait(_exchange_copies, s_sems, r_sems, partials, recvs, after,
                                      name=f"grad_exchange_wait_{group}")
        chip_sums = [_add_halves(p, r, c_idx, name=f"grad_add_halves_{k}")
                     for k, p, r in zip(keys, partials, recvs)]
        lands = [lax.empty((3, *s.shape[1:]), s.dtype) for s in chip_sums]
        s_sems, r_sems, sums, lands, token = _split_start(
            _scatter_copies, 3, chip_sums, lands, after, name=f"grad_scatter_start_{group}")
        in_flight.append((group, keys, s_sems, r_sems, sums, lands))
        return token

    collected = []

    def collect(after):
        group, keys, s_sems, r_sems, sums, lands = in_flight.pop(0)
        sums, from_chips = _split_wait(_scatter_copies, s_sems, r_sems, sums, lands, after,
                                       name=f"grad_scatter_wait_{group}")
        half_sums = [_add_chips(s, r, chip_idx, name=f"grad_add_chips_{k}") for k, s, r in zip(keys, sums, from_chips)]
        others = [lax.empty(h.shape, h.dtype) for h in half_sums]
        s_sems, r_sems, half_sums, others, token = _split_start(
            _swap_copies, 1, half_sums, others, after, name=f"grad_swap_start_{group}")
        collected.append((keys, s_sems, r_sems, half_sums, others))
        return token

    df = _mm(dx3_b, W_down, "nt", name="d_ffn_act", out_dtype=BF16)
    dW_down = _mm(f_act, dx3_b, "tn", name="dw_down", out_dtype=BF16)
    du0, dconv_w, dconv_b = _conv_bwd(u0, W_conv, conv_b, df, T=T, F=F)
    dh3 = _mm(du0, W_up, "nt", name="d_ffn_in", out_dtype=F32, b_blocked=True, tk=F2 // N_CHIPS)
    dW_up = _mm(h3, du0, "tn", name="dw_up", out_dtype=BF16, out_blocks=N_CHIPS)
    tok = exchange_start("ffn", ["w_down", "w_up"], [row_shards(dW_down), col_shards(dW_up)], dh3)
    dx2, dx2_b, dg_ffn = _rms_bwd(dh3, x2, r3 + tok[0:1, 0:1], g_ffn, dx3, name="norm_ffn_bwd")

    do_att = _mm(dx2_b, W_co, "nt", name="d_cross_o", out_dtype=BF16)
    dW_co = _mm(o_att, dx2_b, "tn", name="dw_co", out_dtype=BF16)
    tok = scatter_start("ffn", dW_co)
    dq, dkv = _attn_bwd(qc, kv, do_att, T=T, D=D, M=M)
    dkv_b = dkv.astype(BF16)
    dW_cq = _mm(h2, dq, "tn", name="dw_cq", out_dtype=BF16, after=tok)
    dh2 = _mm(dq, W_cq, "nt", name="d_cross_in", out_dtype=F32)
    dW_ckv = _mm(mem_n, dkv_b, "tn", name="dw_ckv", out_dtype=BF16, out_blocks=N_CHIPS)
    dmem_n = _mm(dkv_b, W_ckv, "nt", name="d_mem", out_dtype=F32, b_blocked=True)
    tok = exchange_start("cross", ["w_co", "w_cq", "w_ckv"],
                         [row_shards(dW_co), row_shards(dW_cq), col_shards(dW_ckv)], dmem_n)
    _, _, dg_mem = _rms_bwd(dmem_n, ms, rm, g_mem, None, name="norm_mem_bwd")
    dx1, dx1_b, dg_cross = _rms_bwd(dh2, x1, r2 + tok[0:1, 0:1], g_cross, dx2, name="norm_cross_bwd")

    dmerged = _mm(dx1_b, W_out, "nt", name="d_merged", out_dtype=BF16)
    dW_out = _mm(merged, dx1_b, "tn", name="dw_out", out_dtype=BF16)
    tok = scatter_start("cross", dW_out)
    dy_gla, dy_pool, dgates = _merge_bwd(dmerged, y_gla, y_pool, proj, T=T, D=D, col_block=2)
    dW_br_gla = _mm(o_gla, dy_gla, "tn", name="dw_branch_gla", out_dtype=BF16, after=tok)
    dW_br_pool = _mm(o_pool, dy_pool, "tn", name="dw_branch_pool", out_dtype=BF16)
    do_gla = _mm(dy_gla, W_branch, "nt", name="d_o_gla", out_dtype=F32, N=DV)
    do_pool = _mm(dy_pool, W_branch, "nt", name="d_o_pool", out_dtype=F32, N=PW, b_off=(DV, 0))
    dp, dw_pool, dpool_scale = _pool_bwd(proj, W_pool, pool_scale, do_pool, T=T, PW=PW, col_block=3)
    dW_pool = jnp.transpose(dw_pool.reshape(POOL_GROUPS, N_CHIPS, GW // N_CHIPS, GW), (1, 0, 2, 3))
    tok = exchange_start("mix", ["w_out", "w_branch", "w_pool"],
                         [row_shards(dW_out), row_shards(jnp.concatenate([dW_br_gla, dW_br_pool], axis=0)),
                          row_shards(dW_pool.reshape(N_CHIPS * POOL_GROUPS * (GW // N_CHIPS), GW).astype(BF16))],
                         dp)
    dqkvr, da_pad, dw2, db_a, dg_gla = _gla_bwd(proj, a_pad, W_a2p, b_a + tok[0:1, 0:1], g_gla, o_raw, states,
                                               do_gla, T=T, DK=DK, DV=DV)
    tok = scatter_start("mix", dqkvr)
    dproj = jnp.concatenate([dqkvr, dp, dgates], axis=1)
    dW_main = _mm(dproj, h1, "tn", name="dw_in_main", out_dtype=BF16, after=tok)
    dW_a = _mm(da_pad, h1, "tn", name="dw_in_rank", out_dtype=BF16)
    dW_in = jnp.concatenate([dW_main[:OFF_A], dW_a[:RANK], dW_main[OFF_A:]], axis=0)
    tok = exchange_start("in", ["w_in"], [row_shards(dW_in)], dW_a)
    dh1 = _mm(dproj, W_main, "nn", name="d_mix_in_main", out_dtype=F32, after=tok)
    dh1 = _mm(da_pad, W_a, "nn", name="d_mix_in_rank", out_dtype=F32, add=dh1)
    dx0, _, dg_mix = _rms_bwd(dh1, xs, r1, g_mix, dx1, name="norm_mix_bwd")

    grads = {}

    small_grads = [loss_part, dg_mix, db_a, dg_gla, dpool_scale, dg_cross, dg_mem, dg_ffn, dconv_b, dg_final,
                   dw2[:RANK], dconv_w]
    small_buf, offs = _pack(small_grads)
    small_sum = _all_reduce_small(small_buf)
    red = _unpack(small_sum, offs, [g.shape for g in small_grads])
    loss = red[0][0, 0]
    for k, g in zip(small_repl, red[1:10]):
        grads[k] = g.reshape(weights[k].shape)
    nb = DK // N_CHIPS
    grads["w_a2"] = lax.dynamic_slice_in_dim(red[10], chip * nb, nb, axis=1)[None]
    nb = F2 // N_CHIPS
    grads["conv_w"] = lax.dynamic_slice_in_dim(red[11], chip * nb, nb, axis=1)[None]

    delta, new_m, new_v = {}, {}, {}

    def shard_rows(k, a):
        a = a[0]
        return a.T if k == "w_in" else a.reshape(-1, a.shape[-1])

    def whole(k, a):
        a = a.reshape(-1, a.shape[2])
        return (a.T if k == "w_in" else a).reshape(weights[k].shape)

    scatter_start("in", small_sum)

    def finish(after):
        keys, s_sems, r_sems, mine, others = collected.pop(0)
        mine, others = _split_wait(_swap_copies, s_sems, r_sems, mine, others, after,
                                   name=f"grad_swap_wait_{keys[0]}")
        for k, g_mine, g_other in zip(keys, mine, others):
            wmv = [halves(shard_rows(k, src[k])) for src in (weights, mom_m, mom_v)]
            res = _adamw_halves(*wmv, g_mine, g_other, c_idx, name=f"adamw_{k}")
            grads[k], delta[k], new_m[k], new_v[k] = (whole(k, a) for a in res)
        return res[1]

    after = in_flight[-1][4][0]
    while in_flight:
        after = collect(after)
        while len(collected) > 1:
            after = finish(after)
    finish(after)
    small = small_repl + ["w_a2", "conv_w"]
    packs = [_pack([src[k] for k in small])[0] for src in (weights, grads, mom_m, mom_v)]
    _, offs = _pack([weights[k] for k in small])
    outs = _adamw(*packs, name="adamw_small")
    for res, o in zip((delta, new_m, new_v), outs):
        for k, a in zip(small, _unpack(o, offs, [weights[k].shape for k in small])):
            res[k] = a

    return (loss, dx0[None], *[grads[k] for k in order], *[delta[k] for k in order],
            *[new_m[k] for k in order], *[new_v[k] for k in order])
```

```python
import functools

import jax
import jax.numpy as jnp
from jax import lax
from jax.experimental import pallas as pl
from jax.experimental.pallas import tpu as pltpu

F32 = jnp.float32
BF16 = jnp.bfloat16
MESH = pl.DeviceIdType.MESH
HIGHEST = lax.Precision.HIGHEST

EPS = 1e-6
GLA_HEADS = 4
GLA_CHUNK = 64
GLA_GATE_NORM = 16.0
POOL_GROUPS = 4
CROSS_HEADS = 4
CONV_W = 3
N_CHIPS = 4
LANES = 128
SUBLANES = 8
VMEM_LIMIT = 56 << 20

ADAM_LR = 0.001
ADAM_B1 = 0.9
ADAM_B2 = 0.999
ADAM_EPS = 1e-08
ADAM_WD = 0.01
ADAM_STEP = 10

NN = (((1,), (0,)), ((), ()))
NT = (((1,), (1,)), ((), ()))
TN = (((0,), (0,)), ((), ()))


ONE_PASS = lax.Precision.HIGH


def _dot(a, b, dn=NN, precision=None):
    return lax.dot_general(a, b, dn, precision=precision, preferred_element_type=F32)


def _tile(n, pref, align=LANES):
    t = (min(pref, n) // align) * align
    while t >= align:
        if n % t == 0:
            return t
        t -= align
    return n


def _pcall(body, *, name, out_shape, grid=(), in_specs=None, out_specs=None, scratch_shapes=(),
           semantics=None, prefetch=0, aliases=None, split_copy=False):
    params = dict(vmem_limit_bytes=VMEM_LIMIT)
    if semantics is not None:
        params["dimension_semantics"] = semantics
    if split_copy:
        params["has_side_effects"] = pltpu.SideEffectType.DATAFLOW_SIDE_EFFECTING
    if prefetch:
        grid_spec = pltpu.PrefetchScalarGridSpec(
            num_scalar_prefetch=prefetch, grid=grid, in_specs=in_specs, out_specs=out_specs,
            scratch_shapes=scratch_shapes)
        return pl.pallas_call(body, name=name, out_shape=out_shape, grid_spec=grid_spec,
                              compiler_params=pltpu.CompilerParams(**params))
    kw = {}
    if aliases is not None:
        kw["input_output_aliases"] = aliases
    if in_specs is not None:
        kw["in_specs"] = in_specs
    if out_specs is not None:
        kw["out_specs"] = out_specs
    return pl.pallas_call(body, name=name, out_shape=out_shape, grid=grid,
                          scratch_shapes=scratch_shapes,
                          compiler_params=pltpu.CompilerParams(**params), **kw)


def _sigmoid(x):
    return 1.0 / (1.0 + jnp.exp(-x))


def _log_sigmoid(x):
    return jnp.minimum(x, 0.0) - jnp.log(1.0 + jnp.exp(-jnp.abs(x)))


def _mm(a, b, mode, *, name, out_dtype, M=None, N=None, K=None, a_off=(0, 0), b_off=(0, 0),
        add=None, b_blocked=False, out_blocks=0, after=None, tm=1536, tn=1536, tk=2048):
    if b_blocked:
        nb, R, Cb = b.shape
        b_rows, b_cols = R, nb * Cb
    else:
        b_rows, b_cols = b.shape
    if mode == "nn":
        M = M or a.shape[0]; K = K or a.shape[1]; N = N or b_cols
    elif mode == "nt":
        M = M or a.shape[0]; K = K or a.shape[1]; N = N or b_rows
    else:
        K = K or a.shape[0]; M = M or a.shape[1]; N = N or b_cols
    tm = _tile(M, tm, LANES if mode == "tn" else 16)
    tn = _tile(Cb if (b_blocked and mode != "nt") else (N // out_blocks if out_blocks else N), tn)
    tk = _tile(Cb if (b_blocked and mode == "nt") else K, tk)
    nk = K // tk
    dn = {"nn": NN, "nt": NT, "tn": TN}[mode]

    def off(o, t):
        assert o % t == 0, (name, o, t)
        return o // t

    if mode == "tn":
        ar, ac = off(a_off[0], tk), off(a_off[1], tm)
        a_spec = pl.BlockSpec((tk, tm), lambda i, j, k: (k + ar, i + ac))
    else:
        ar, ac = off(a_off[0], tm), off(a_off[1], tk)
        a_spec = pl.BlockSpec((tm, tk), lambda i, j, k: (i + ar, k + ac))
    if b_blocked and mode == "nt":
        per = Cb // tk
        b_spec = pl.BlockSpec((None, tn, tk), lambda i, j, k: (k // per, j, k % per))
    elif b_blocked:
        per = Cb // tn
        b_spec = pl.BlockSpec((None, tk, tn), lambda i, j, k: (j // per, k, j % per))
    elif mode == "nt":
        br, bc = off(b_off[0], tn), off(b_off[1], tk)
        b_spec = pl.BlockSpec((tn, tk), lambda i, j, k: (j + br, k + bc))
    else:
        br, bc = off(b_off[0], tk), off(b_off[1], tn)
        b_spec = pl.BlockSpec((tk, tn), lambda i, j, k: (k + br, j + bc))
    if out_blocks:
        per_o = N // out_blocks // tn
        o_spec = pl.BlockSpec((None, tm, tn), lambda i, j, k: (j // per_o, i, j % per_o))
        out_shape = jax.ShapeDtypeStruct((out_blocks, M, N // out_blocks), out_dtype)
    else:
        o_spec = pl.BlockSpec((tm, tn), lambda i, j, k: (i, j))
        out_shape = jax.ShapeDtypeStruct((M, N), out_dtype)
    in_specs = [a_spec, b_spec]
    args = [a, b]
    if add is not None:
        assert not out_blocks
        in_specs.append(o_spec)
        args.append(add)
    if after is not None:
        in_specs.append(pl.BlockSpec(memory_space=pl.ANY))
        args.append(after)
    n_in = len(args)

    def finish(r, refs):
        if add is not None:
            r = r + refs[2][...]
        o_ref = refs[n_in]
        o_ref[...] = r.astype(o_ref.dtype)

    def body_one(*refs):
        finish(_dot(refs[0][...].astype(BF16), refs[1][...].astype(BF16), dn), refs)

    def body_acc(*refs):
        acc_ref = refs[-1]
        k = pl.program_id(2)

        @pl.when(k == 0)
        def _():
            acc_ref[...] = jnp.zeros_like(acc_ref)

        acc_ref[...] += _dot(refs[0][...].astype(BF16), refs[1][...].astype(BF16), dn)

        @pl.when(k == nk - 1)
        def _():
            finish(acc_ref[...], refs)

    return _pcall(body_one if nk == 1 else body_acc, name=name, out_shape=out_shape,
                  grid=(M // tm, N // tn, nk), in_specs=in_specs, out_specs=o_spec,
                  scratch_shapes=[] if nk == 1 else [pltpu.VMEM((tm, tn), F32)],
                  semantics=("parallel", "parallel", "arbitrary"))(*args)


def _rms_fwd(x, g, *, name):
    T, D = x.shape
    tr = _tile(T, 128, 16)

    def body(x_ref, g_ref, h_ref, r_ref):
        xv = x_ref[...]
        r = lax.rsqrt(jnp.mean(xv * xv, axis=-1, keepdims=True) + EPS)
        h_ref[...] = (xv * r * g_ref[...]).astype(h_ref.dtype)
        r_ref[...] = r

    row = pl.BlockSpec((tr, D), lambda i: (i, 0))
    return _pcall(body, name=name,
                  out_shape=(jax.ShapeDtypeStruct((T, D), BF16), jax.ShapeDtypeStruct((T, 1), F32)),
                  grid=(T // tr,),
                  in_specs=[row, pl.BlockSpec((1, D), lambda i: (0, 0))],
                  out_specs=(row, pl.BlockSpec((tr, 1), lambda i: (i, 0))),
                  semantics=("parallel",))(x, g)


def _rms_bwd(dh, x, rstd, g, dres, *, name):
    T, D = x.shape
    tr = _tile(T, 128, 16)
    has_res = dres is not None

    def body(*refs):
        if has_res:
            dh_ref, x_ref, r_ref, g_ref, res_ref, dx_ref, dxb_ref, dg_ref = refs
        else:
            dh_ref, x_ref, r_ref, g_ref, dx_ref, dxb_ref, dg_ref = refs
        r = r_ref[...]
        xh = x_ref[...] * r
        dhv = dh_ref[...].astype(F32)
        dxh = dhv * g_ref[...]
        m = jnp.mean(dxh * xh, axis=-1, keepdims=True)
        dx = r * (dxh - xh * m)
        if has_res:
            dx = dx + res_ref[...]
        dx_ref[...] = dx
        dxb_ref[...] = dx.astype(BF16)

        @pl.when(pl.program_id(0) == 0)
        def _():
            dg_ref[...] = jnp.zeros_like(dg_ref)

        dg_ref[...] += jnp.sum(dhv * xh, axis=0, keepdims=True)

    row = pl.BlockSpec((tr, D), lambda i: (i, 0))
    vec = pl.BlockSpec((1, D), lambda i: (0, 0))
    in_specs = [row, row, pl.BlockSpec((tr, 1), lambda i: (i, 0)), vec]
    args = [dh, x, rstd, g]
    if has_res:
        in_specs.append(row)
        args.append(dres)
    return _pcall(body, name=name,
                  out_shape=(jax.ShapeDtypeStruct((T, D), F32), jax.ShapeDtypeStruct((T, D), BF16),
                             jax.ShapeDtypeStruct((1, D), F32)),
                  grid=(T // tr,), in_specs=in_specs, out_specs=(row, row, vec),
                  semantics=("arbitrary",))(*args)


def _loss_head(x3, g, tgt):
    T, D = x3.shape
    tr = _tile(T, 128, 16)

    def body(x_ref, g_ref, t_ref, loss_ref, dx_ref, dxb_ref, dg_ref):
        xv = x_ref[...]
        gv = g_ref[...]
        r = lax.rsqrt(jnp.mean(xv * xv, axis=-1, keepdims=True) + EPS)
        xh = xv * r
        err = xh * gv - t_ref[...]
        dy = err * (1.0 / D)
        dxh = dy * gv
        m = jnp.mean(dxh * xh, axis=-1, keepdims=True)
        dx = r * (dxh - xh * m)
        dx_ref[...] = dx
        dxb_ref[...] = dx.astype(BF16)

        @pl.when(pl.program_id(0) == 0)
        def _():
            dg_ref[...] = jnp.zeros_like(dg_ref)
            loss_ref[...] = jnp.zeros_like(loss_ref)

        dg_ref[...] += jnp.sum(dy * xh, axis=0, keepdims=True)
        part = 0.5 * jnp.sum(jnp.mean(err * err, axis=-1, keepdims=True), axis=0, keepdims=True)
        loss_ref[...] += jnp.broadcast_to(part, loss_ref.shape)

    row = pl.BlockSpec((tr, D), lambda i: (i, 0))
    vec = pl.BlockSpec((1, D), lambda i: (0, 0))
    return _pcall(body, name="loss_head",
                  out_shape=(jax.ShapeDtypeStruct((1, LANES), F32), jax.ShapeDtypeStruct((T, D), F32),
                             jax.ShapeDtypeStruct((T, D), BF16), jax.ShapeDtypeStruct((1, D), F32)),
                  grid=(T // tr,), in_specs=[row, vec, row],
                  out_specs=(pl.BlockSpec((1, LANES), lambda i: (0, 0)), row, row, vec),
                  semantics=("arbitrary",))(x3, g, tgt)


def _gla_chunk_terms(qk, a_ref, w2_ref, ba_ref, DK):
    C = qk.shape[0]
    gp = _dot(a_ref[...].astype(BF16), w2_ref[...]) + ba_ref[...]
    la = _log_sigmoid(gp) * (1.0 / GLA_GATE_NORM)
    row = lax.broadcasted_iota(jnp.int32, (C, C), 0)
    col = lax.broadcasted_iota(jnp.int32, (C, C), 1)
    causal = row >= col
    b = _dot(causal.astype(F32), la, precision=HIGHEST)
    return gp, b, causal


def _gla_fwd(proj, a_pad, w2, b_a, g_gla, *, T, DK, DV):
    assert 2 * DK == DV
    H = GLA_HEADS
    HK, HV = DK // H, DV // H
    C = GLA_CHUNK
    n = T // C
    RP = a_pad.shape[1]
    scale = HK ** -0.5

    def body(qk_ref, v_ref, r_ref, a_ref, w2_ref, ba_ref, gg_ref, og_ref, oraw_ref, st_ref, s_ref):
        @pl.when(pl.program_id(0) == 0)
        def _():
            s_ref[...] = jnp.zeros_like(s_ref)

        st_ref[...] = s_ref[...]
        qk = qk_ref[...]
        _, b, causal = _gla_chunk_terms(qk, a_ref, w2_ref, ba_ref, DK)
        for h in range(H):
            ks = slice(h * HK, (h + 1) * HK)
            vs = slice(h * HV, (h + 1) * HV)
            bh = b[:, ks]
            b_last = bh[C - 1:C, :]
            qt = qk[:, ks] * scale * jnp.exp(bh)
            kh = qk[:, DK + h * HK:DK + (h + 1) * HK]
            kt = kh * jnp.exp(-bh)
            khat = kh * jnp.exp(b_last - bh)
            a_mat = jnp.where(causal, _dot(qt, kt, NT, ONE_PASS), 0.0)
            vh = v_ref[:, vs]
            s_t = s_ref[h]
            o = _dot(a_mat, vh, NN, ONE_PASS) + _dot(qt, s_t, NT, ONE_PASS)
            s_ref[h] = s_t * jnp.exp(b_last) + _dot(vh, khat, TN, ONE_PASS)
            rs = lax.rsqrt(jnp.mean(o * o, axis=-1, keepdims=True) + EPS)
            rr = r_ref[:, vs]
            og = o * rs * gg_ref[:, vs] * (rr * _sigmoid(rr))
            oraw_ref[:, vs] = o
            og_ref[:, vs] = og.astype(BF16)

    blk = lambda j: pl.BlockSpec((C, DV), lambda i: (i, j))
    full = lambda s: pl.BlockSpec(s, lambda i: (0,) * len(s))
    return _pcall(
        body, name="gla_fwd",
        out_shape=(jax.ShapeDtypeStruct((T, DV), BF16), jax.ShapeDtypeStruct((T, DV), F32),
                   jax.ShapeDtypeStruct((n, H, HV, HK), F32)),
        grid=(n,),
        in_specs=[blk(0), blk(1), blk(2), pl.BlockSpec((C, RP), lambda i: (i, 0)),
                  full((RP, DK)), full((1, DK)), full((1, DV))],
        out_specs=(blk(0), blk(0), pl.BlockSpec((None, H, HV, HK), lambda i: (i, 0, 0, 0))),
        scratch_shapes=[pltpu.VMEM((H, HV, HK), F32)],
        semantics=("arbitrary",))(proj, proj, proj, a_pad, w2, b_a, g_gla)


def _gla_bwd(proj, a_pad, w2, b_a, g_gla, o_raw, states, do_gla, *, T, DK, DV):
    H = GLA_HEADS
    HK, HV = DK // H, DV // H
    C = GLA_CHUNK
    n = T // C
    RP = a_pad.shape[1]
    scale = HK ** -0.5

    def body(qk_ref, v_ref, r_ref, a_ref, w2_ref, ba_ref, gg_ref, oraw_ref, st_ref, dog_ref,
             dqkvr_ref, da_ref, dw2_ref, dba_ref, dgg_ref, ds_ref):
        @pl.when(pl.program_id(0) == 0)
        def _():
            ds_ref[...] = jnp.zeros_like(ds_ref)
            dw2_ref[...] = jnp.zeros_like(dw2_ref)
            dba_ref[...] = jnp.zeros_like(dba_ref)
            dgg_ref[...] = jnp.zeros_like(dgg_ref)

        qk = qk_ref[...]
        gp, b, causal = _gla_chunk_terms(qk, a_ref, w2_ref, ba_ref, DK)
        row = lax.broadcasted_iota(jnp.int32, (C, C), 0)
        col = lax.broadcasted_iota(jnp.int32, (C, C), 1)
        upper = (col >= row).astype(F32)
        dla_parts = []
        for h in range(H):
            ks = slice(h * HK, (h + 1) * HK)
            vs = slice(h * HV, (h + 1) * HV)
            bh = b[:, ks]
            b_last = bh[C - 1:C, :]
            eb = jnp.exp(bh)
            emb = jnp.exp(-bh)
            ehat = jnp.exp(b_last - bh)
            e_last = jnp.exp(b_last)
            qt = qk[:, ks] * scale * eb
            kh = qk[:, DK + h * HK:DK + (h + 1) * HK]
            kt = kh * emb
            khat = kh * ehat
            a_mat = jnp.where(causal, _dot(qt, kt, NT, ONE_PASS), 0.0)
            vh = v_ref[:, vs]
            o = oraw_ref[:, vs]
            rs = lax.rsqrt(jnp.mean(o * o, axis=-1, keepdims=True) + EPS)
            on = o * rs
            gg = gg_ref[:, vs]
            rr = r_ref[:, vs]
            sg = _sigmoid(rr)
            d_out = dog_ref[:, vs]
            dr = d_out * (on * gg) * (sg * (1.0 + rr * (1.0 - sg)))
            d_og = d_out * (rr * sg)
            dgg_ref[:, vs] += jnp.sum(d_og * on, axis=0, keepdims=True)
            d_on = d_og * gg
            d_o = rs * (d_on - on * jnp.mean(d_on * on, axis=-1, keepdims=True))
            s_t = st_ref[h]
            ds_t = ds_ref[h]
            d_a = jnp.where(causal, _dot(d_o, vh, NT, ONE_PASS), 0.0)
            dv = _dot(a_mat, d_o, TN, ONE_PASS) + _dot(khat, ds_t, NT, ONE_PASS)
            dqt = _dot(d_a, kt, NN, ONE_PASS) + _dot(d_o, s_t, NN, ONE_PASS)
            dkt = _dot(d_a, qt, TN, ONE_PASS)
            dkhat = _dot(vh, ds_t, NN, ONE_PASS)
            ds_ref[h] = ds_t * e_last + _dot(d_o, qt, TN, ONE_PASS)
            dq = dqt * eb * scale
            dk = dkt * emb + dkhat * ehat
            db = dqt * qt - dkt * kt - dkhat * khat
            d_last = (jnp.sum(dkhat * khat, axis=0, keepdims=True)
                      + e_last * jnp.sum(ds_t * s_t, axis=0, keepdims=True))
            dla_parts.append(_dot(upper, db, NN, HIGHEST) + d_last)
            dqkvr_ref[:, ks] = dq.astype(BF16)
            dqkvr_ref[:, DK + h * HK:DK + (h + 1) * HK] = dk.astype(BF16)
            dqkvr_ref[:, DV + h * HV:DV + (h + 1) * HV] = dv.astype(BF16)
            dqkvr_ref[:, 2 * DV + h * HV:2 * DV + (h + 1) * HV] = dr.astype(BF16)
        dla = jnp.concatenate(dla_parts, axis=1)
        dgp = dla * (1.0 / GLA_GATE_NORM) * _sigmoid(-gp)
        dba_ref[...] += jnp.sum(dgp, axis=0, keepdims=True)
        dgp_b = dgp.astype(BF16)
        dw2_ref[...] += _dot(a_ref[...].astype(BF16), dgp_b, TN)
        da_ref[...] = _dot(dgp_b, w2_ref[...], NT).astype(BF16)

    rev = lambda j: pl.BlockSpec((C, DV), lambda i: (n - 1 - i, j))
    full = lambda s: pl.BlockSpec(s, lambda i: (0,) * len(s))
    return _pcall(
        body, name="gla_bwd",
        out_shape=(jax.ShapeDtypeStruct((T, 3 * DV), BF16), jax.ShapeDtypeStruct((T, RP), BF16),
                   jax.ShapeDtypeStruct((RP, DK), F32), jax.ShapeDtypeStruct((1, DK), F32),
                   jax.ShapeDtypeStruct((1, DV), F32)),
        grid=(n,),
        in_specs=[rev(0), rev(1), rev(2), pl.BlockSpec((C, RP), lambda i: (n - 1 - i, 0)),
                  full((RP, DK)), full((1, DK)), full((1, DV)), rev(0),
                  pl.BlockSpec((None, H, HV, HK), lambda i: (n - 1 - i, 0, 0, 0)), rev(0)],
        out_specs=(pl.BlockSpec((C, 3 * DV), lambda i: (n - 1 - i, 0)),
                   pl.BlockSpec((C, RP), lambda i: (n - 1 - i, 0)),
                   full((RP, DK)), full((1, DK)), full((1, DV))),
        scratch_shapes=[pltpu.VMEM((H, HV, HK), F32)],
        semantics=("arbitrary",))(proj, proj, proj, a_pad, w2, b_a, g_gla, o_raw, states, do_gla)


def _pool_windows(p, g, T):
    t = lax.broadcasted_iota(jnp.int32, (T, 1), 0)
    s = p
    for lvl in range(POOL_GROUPS):
        sh = 1 << lvl
        nxt = s + jnp.where(t >= sh, pltpu.roll(s, sh, 0), 0.0)
        s = jnp.where(lvl <= g, nxt, s)
    win = jnp.left_shift(2, g)
    inv = 1.0 / jnp.minimum(t + 1, win).astype(F32)
    return s * inv - p, inv


def _pool_fwd(proj, w_pool, scale, *, T, PW, col_block):
    GW = PW // POOL_GROUPS
    per = PW // GW

    def body(p_ref, w_ref, s_ref, o_ref):
        g = pl.program_id(0)
        pooled, _ = _pool_windows(p_ref[...], g, T)
        mixed = _dot(pooled.astype(BF16), w_ref[...])
        o_ref[...] = (mixed * s_ref[...]).astype(BF16)

    return _pcall(body, name="pool_fwd", out_shape=jax.ShapeDtypeStruct((T, PW), BF16),
                  grid=(POOL_GROUPS,),
                  in_specs=[pl.BlockSpec((T, GW), lambda g: (0, col_block * per + g)),
                            pl.BlockSpec((None, GW, GW), lambda g: (g, 0, 0)),
                            pl.BlockSpec((1, GW), lambda g: (0, g))],
                  out_specs=pl.BlockSpec((T, GW), lambda g: (0, g)),
                  semantics=("parallel",))(proj, w_pool, scale)


def _pool_bwd(proj, w_pool, scale, do_pool, *, T, PW, col_block):
    GW = PW // POOL_GROUPS
    per = PW // GW

    def body(p_ref, w_ref, s_ref, do_ref, dp_ref, dw_ref, dsc_ref):
        g = pl.program_id(0)
        pooled, inv = _pool_windows(p_ref[...], g, T)
        pooled_b = pooled.astype(BF16)
        w = w_ref[...]
        mixed = _dot(pooled_b, w)
        d_out = do_ref[...]
        dsc_ref[...] = jnp.sum(d_out * mixed, axis=0, keepdims=True)
        dmixed = (d_out * s_ref[...]).astype(BF16)
        dw_ref[...] = _dot(pooled_b, dmixed, TN)
        dpooled = _dot(dmixed, w, NT)
        t = lax.broadcasted_iota(jnp.int32, (T, 1), 0)
        s = dpooled * inv
        for lvl in range(POOL_GROUPS):
            sh = 1 << lvl
            nxt = s + jnp.where(t < T - sh, pltpu.roll(s, T - sh, 0), 0.0)
            s = jnp.where(lvl <= g, nxt, s)
        dp_ref[...] = (s - dpooled).astype(BF16)

    return _pcall(body, name="pool_bwd",
                  out_shape=(jax.ShapeDtypeStruct((T, PW), BF16),
                             jax.ShapeDtypeStruct((POOL_GROUPS, GW, GW), F32),
                             jax.ShapeDtypeStruct((1, PW), F32)),
                  grid=(POOL_GROUPS,),
                  in_specs=[pl.BlockSpec((T, GW), lambda g: (0, col_block * per + g)),
                            pl.BlockSpec((None, GW, GW), lambda g: (g, 0, 0)),
                            pl.BlockSpec((1, GW), lambda g: (0, g)),
                            pl.BlockSpec((T, GW), lambda g: (0, g))],
                  out_specs=(pl.BlockSpec((T, GW), lambda g: (0, g)),
                             pl.BlockSpec((None, GW, GW), lambda g: (g, 0, 0)),
                             pl.BlockSpec((1, GW), lambda g: (0, g))),
                  semantics=("parallel",))(proj, w_pool, scale, do_pool)


def _merge_fwd(y_gla, y_pool, proj, *, T, D, col_block):
    tr = _tile(T, 128, 16)

    def body(yg_ref, yp_ref, g1_ref, g2_ref, o_ref):
        o_ref[...] = (_sigmoid(g1_ref[...]) * yg_ref[...]
                      + _sigmoid(g2_ref[...]) * yp_ref[...]).astype(BF16)

    row = pl.BlockSpec((tr, D), lambda i: (i, 0))
    return _pcall(body, name="merge_fwd", out_shape=jax.ShapeDtypeStruct((T, D), BF16),
                  grid=(T // tr,),
                  in_specs=[row, row, pl.BlockSpec((tr, D), lambda i: (i, col_block)),
                            pl.BlockSpec((tr, D), lambda i: (i, col_block + 1))],
                  out_specs=row, semantics=("parallel",))(y_gla, y_pool, proj, proj)


def _merge_bwd(dmerged, y_gla, y_pool, proj, *, T, D, col_block):
    tr = _tile(T, 128, 16)

    def body(dm_ref, yg_ref, yp_ref, g1_ref, g2_ref, dyg_ref, dyp_ref, dg_ref):
        dm = dm_ref[...]
        s1 = _sigmoid(g1_ref[...])
        s2 = _sigmoid(g2_ref[...])
        dyg_ref[...] = (dm * s1).astype(BF16)
        dyp_ref[...] = (dm * s2).astype(BF16)
        dg_ref[:, :D] = (dm * yg_ref[...] * s1 * (1.0 - s1)).astype(BF16)
        dg_ref[:, D:] = (dm * yp_ref[...] * s2 * (1.0 - s2)).astype(BF16)

    row = pl.BlockSpec((tr, D), lambda i: (i, 0))
    return _pcall(body, name="merge_bwd",
                  out_shape=(jax.ShapeDtypeStruct((T, D), BF16), jax.ShapeDtypeStruct((T, D), BF16),
                             jax.ShapeDtypeStruct((T, 2 * D), BF16)),
                  grid=(T // tr,),
                  in_specs=[row, row, row, pl.BlockSpec((tr, D), lambda i: (i, col_block)),
                            pl.BlockSpec((tr, D), lambda i: (i, col_block + 1))],
                  out_specs=(row, row, pl.BlockSpec((tr, 2 * D), lambda i: (i, 0))),
                  semantics=("parallel",))(dmerged, y_gla, y_pool, proj, proj)


def _attn_fwd(q, kv, *, T, D, M):
    H = CROSS_HEADS
    HD = D // H
    tq = _tile(T, 512, 16)
    scale = HD ** -0.5

    def body(q_ref, kv_ref, o_ref):
        for h in range(H):
            hs = slice(h * HD, (h + 1) * HD)
            s = _dot(q_ref[:, hs], kv_ref[:, hs], NT) * scale
            e = jnp.exp(s - jnp.max(s, axis=-1, keepdims=True))
            p = e / jnp.sum(e, axis=-1, keepdims=True)
            o_ref[:, hs] = _dot(p.astype(BF16), kv_ref[:, D + h * HD:D + (h + 1) * HD]).astype(BF16)

    row = pl.BlockSpec((tq, D), lambda i: (i, 0))
    return _pcall(body, name="attn_fwd", out_shape=jax.ShapeDtypeStruct((T, D), BF16),
                  grid=(T // tq,), in_specs=[row, pl.BlockSpec((M, 2 * D), lambda i: (0, 0))],
                  out_specs=row, semantics=("parallel",))(q, kv)


def _attn_bwd(q, kv, do, *, T, D, M):
    H = CROSS_HEADS
    HD = D // H
    tq = _tile(T, 512, 16)
    scale = HD ** -0.5

    def body(q_ref, kv_ref, do_ref, dq_ref, dkv_ref):
        @pl.when(pl.program_id(0) == 0)
        def _():
            dkv_ref[...] = jnp.zeros_like(dkv_ref)

        for h in range(H):
            hs = slice(h * HD, (h + 1) * HD)
            vs = slice(D + h * HD, D + (h + 1) * HD)
            qh = q_ref[:, hs]
            kh = kv_ref[:, hs]
            s = _dot(qh, kh, NT) * scale
            e = jnp.exp(s - jnp.max(s, axis=-1, keepdims=True))
            p = e / jnp.sum(e, axis=-1, keepdims=True)
            p_b = p.astype(BF16)
            d_o = do_ref[:, hs]
            dkv_ref[:, vs] += _dot(p_b, d_o, TN)
            dp = _dot(d_o, kv_ref[:, vs], NT)
            ds = (p * (dp - jnp.sum(dp * p, axis=-1, keepdims=True)) * scale).astype(BF16)
            dq_ref[:, hs] = _dot(ds, kh).astype(BF16)
            dkv_ref[:, hs] += _dot(ds, qh, TN)

    row = pl.BlockSpec((tq, D), lambda i: (i, 0))
    full = pl.BlockSpec((M, 2 * D), lambda i: (0, 0))
    return _pcall(body, name="attn_bwd",
                  out_shape=(jax.ShapeDtypeStruct((T, D), BF16), jax.ShapeDtypeStruct((M, 2 * D), F32)),
                  grid=(T // tq,), in_specs=[row, full, row], out_specs=(row, full),
                  semantics=("arbitrary",))(q, kv, do)


def _shift_down(x, halo, s):
    out = pltpu.roll(x, s, 0)
    t8 = lax.broadcasted_iota(jnp.int32, (SUBLANES, 1), 0)
    head = out[:SUBLANES]
    for j in range(s):
        head = jnp.where(t8 == j, halo[SUBLANES - s + j:SUBLANES - s + j + 1, :], head)
    return head if x.shape[0] == SUBLANES else jnp.concatenate([head, out[SUBLANES:]], axis=0)


def _shift_up(x, halo, s):
    rows = x.shape[0]
    out = pltpu.roll(x, rows - s, 0)
    t8 = lax.broadcasted_iota(jnp.int32, (SUBLANES, 1), 0)
    tail = out[rows - SUBLANES:]
    for j in range(s):
        tail = jnp.where(t8 == SUBLANES - s + j, halo[j:j + 1, :], tail)
    return jnp.concatenate([out[:rows - SUBLANES], tail], axis=0)


def _conv_tiles(T):
    tt = _tile(T, 128, SUBLANES)
    return tt, tt // SUBLANES, T // SUBLANES


def _conv_fwd(u0, conv_w, conv_b, *, T, F):
    tt, hb, _ = _conv_tiles(T)
    cw = _tile(F, LANES)

    def body(u_ref, prev_ref, w_ref, b_ref, f_ref):
        i = pl.program_id(0)

        def conv(cs):
            x = u_ref[:, cs]
            halo = jnp.where(i > 0, prev_ref[:, cs], 0.0)
            return (w_ref[2:3, cs] * x + w_ref[1:2, cs] * _shift_down(x, halo, 1)
                    + w_ref[0:1, cs] * _shift_down(x, halo, 2) + b_ref[:, cs])

        for j in range(F // cw):
            gate = conv(slice(j * cw, (j + 1) * cw))
            val = conv(slice(F + j * cw, F + (j + 1) * cw))
            f_ref[:, j * cw:(j + 1) * cw] = (gate * _sigmoid(gate) * val).astype(BF16)

    return _pcall(body, name="conv_fwd", out_shape=jax.ShapeDtypeStruct((T, F), BF16),
                  grid=(T // tt,),
                  in_specs=[pl.BlockSpec((tt, 2 * F), lambda i: (i, 0)),
                            pl.BlockSpec((SUBLANES, 2 * F), lambda i: (jnp.maximum(i * hb - 1, 0), 0)),
                            pl.BlockSpec((CONV_W, 2 * F), lambda i: (0, 0)),
                            pl.BlockSpec((1, 2 * F), lambda i: (0, 0))],
                  out_specs=pl.BlockSpec((tt, F), lambda i: (i, 0)),
                  semantics=("parallel",))(u0, u0, conv_w, conv_b)


def _conv_bwd(u0, conv_w, conv_b, df, *, T, F):
    tt, hb, nb = _conv_tiles(T)
    nt = T // tt
    cw = _tile(F, LANES)

    def body(u_ref, prev_ref, next_ref, df_ref, dfn_ref, w_ref, b_ref, du0_ref, dw_ref, db_ref):
        i = pl.program_id(0)

        @pl.when(i == 0)
        def _():
            dw_ref[...] = jnp.zeros_like(dw_ref)
            db_ref[...] = jnp.zeros_like(db_ref)

        def conv(cs):
            x = u_ref[:, cs]
            halo = jnp.where(i > 0, prev_ref[:, cs], 0.0)
            x1 = _shift_down(x, halo, 1)
            x2 = _shift_down(x, halo, 2)
            u = w_ref[2:3, cs] * x + w_ref[1:2, cs] * x1 + w_ref[0:1, cs] * x2 + b_ref[:, cs]
            xn = next_ref[:, cs]
            tail = x[tt - SUBLANES:, :]
            un = (w_ref[2:3, cs] * xn + w_ref[1:2, cs] * _shift_down(xn, tail, 1)
                  + w_ref[0:1, cs] * _shift_down(xn, tail, 2) + b_ref[:, cs])
            return u, un, (x, x1, x2)

        def glu_grad(gate, val, dff):
            sg = _sigmoid(gate)
            return dff * val * (sg * (1.0 + gate * (1.0 - sg))), dff * (gate * sg)

        def finish(cs, du, dun, xs):
            du0 = (w_ref[2:3, cs] * du + w_ref[1:2, cs] * _shift_up(du, dun, 1)
                   + w_ref[0:1, cs] * _shift_up(du, dun, 2))
            du0_ref[:, cs] = du0.astype(BF16)
            db_ref[:, cs] += jnp.sum(du, axis=0, keepdims=True)
            dw_ref[2:3, cs] += jnp.sum(du * xs[0], axis=0, keepdims=True)
            dw_ref[1:2, cs] += jnp.sum(du * xs[1], axis=0, keepdims=True)
            dw_ref[0:1, cs] += jnp.sum(du * xs[2], axis=0, keepdims=True)

        for j in range(F // cw):
            fs = slice(j * cw, (j + 1) * cw)
            gs, vs = fs, slice(F + j * cw, F + (j + 1) * cw)
            ug, ung, xg = conv(gs)
            uv, unv, xv = conv(vs)
            dug, duv = glu_grad(ug, uv, df_ref[:, fs].astype(F32))
            dung, dunv = glu_grad(ung, unv, dfn_ref[0:SUBLANES, fs].astype(F32))
            dung = jnp.where(i < nt - 1, dung, 0.0)
            dunv = jnp.where(i < nt - 1, dunv, 0.0)
            finish(gs, dug, dung, xg)
            finish(vs, duv, dunv, xv)

    wide = lambda rows, fn: pl.BlockSpec((rows, 2 * F), fn)
    nxt = lambda i: (jnp.minimum((i + 1) * hb, nb - 1), 0)
    return _pcall(body, name="conv_bwd",
                  out_shape=(jax.ShapeDtypeStruct((T, 2 * F), BF16),
                             jax.ShapeDtypeStruct((CONV_W, 2 * F), F32),
                             jax.ShapeDtypeStruct((1, 2 * F), F32)),
                  grid=(nt,),
                  in_specs=[wide(tt, lambda i: (i, 0)),
                            wide(SUBLANES, lambda i: (jnp.maximum(i * hb - 1, 0), 0)),
                            wide(SUBLANES, nxt),
                            pl.BlockSpec((tt, F), lambda i: (i, 0)),
                            pl.BlockSpec((2 * SUBLANES, F),
                                         lambda i: (jnp.minimum((i + 1) * (hb // 2), nb // 2 - 1), 0)),
                            wide(CONV_W, lambda i: (0, 0)), wide(1, lambda i: (0, 0))],
                  out_specs=(wide(tt, lambda i: (i, 0)), wide(CONV_W, lambda i: (0, 0)),
                             wide(1, lambda i: (0, 0))),
                  semantics=("arbitrary",))(u0, u0, u0, df, df, conv_w, conv_b)


def _adamw(w, g, m, v, *, name):
    R, C = w.shape
    tr = _tile(R, max(SUBLANES, (1 << 19) // max(C, 1) // SUBLANES * SUBLANES), SUBLANES)
    c1 = 1.0 / (1.0 - ADAM_B1 ** ADAM_STEP)
    c2 = 1.0 / (1.0 - ADAM_B2 ** ADAM_STEP)

    def body(w_ref, g_ref, m_ref, v_ref, d_ref, mo_ref, vo_ref):
        gv = g_ref[...]
        mn = ADAM_B1 * m_ref[...] + (1.0 - ADAM_B1) * gv
        vn = ADAM_B2 * v_ref[...] + (1.0 - ADAM_B2) * (gv * gv)
        d_ref[...] = -ADAM_LR * ((mn * c1) / (jnp.sqrt(vn * c2) + ADAM_EPS) + ADAM_WD * w_ref[...])
        mo_ref[...] = mn
        vo_ref[...] = vn

    blk = pl.BlockSpec((tr, C), lambda i: (i, 0))
    shp = jax.ShapeDtypeStruct((R, C), F32)
    return _pcall(body, name=name, out_shape=(shp, shp, shp), grid=(R // tr,),
                  in_specs=[blk] * 4, out_specs=(blk,) * 3, semantics=("parallel",))(w, g, m, v)


def _blk(h, C, elems=1 << 19, align=16):
    th = _tile(h, max(align, elems // C // align * align), align)
    if th < h or h * C <= 2 * elems:
        return th, C
    return h, _tile(C, max(LANES, elems // h // LANES * LANES))


def _adamw_halves(w, m, v, g_mine, g_other, c_idx, *, name):
    _, h, C = w.shape
    th, tc = _blk(h, C, align=SUBLANES)
    c1 = 1.0 / (1.0 - ADAM_B1 ** ADAM_STEP)
    c2 = 1.0 / (1.0 - ADAM_B2 ** ADAM_STEP)

    def body(c_ref, w_ref, m_ref, v_ref, gm_ref, go_ref, g_ref, d_ref, mo_ref, vo_ref):
        gv = jnp.where(pl.program_id(0) == c_ref[0], gm_ref[...], go_ref[...])
        mn = ADAM_B1 * m_ref[...] + (1.0 - ADAM_B1) * gv
        vn = ADAM_B2 * v_ref[...] + (1.0 - ADAM_B2) * (gv * gv)
        d_ref[...] = -ADAM_LR * ((mn * c1) / (jnp.sqrt(vn * c2) + ADAM_EPS) + ADAM_WD * w_ref[...])
        g_ref[...] = gv
        mo_ref[...] = mn
        vo_ref[...] = vn

    blk = pl.BlockSpec((None, th, tc), lambda s, i, j, c: (s, i, j))

    def pick(mine):
        def index(s, i, j, c):
            use = (s == c[0]) if mine else (s != c[0])
            return jnp.where(use, i, 0), jnp.where(use, j, 0)
        return pl.BlockSpec((th, tc), index)

    shp = jax.ShapeDtypeStruct((2, h, C), F32)
    return _pcall(body, name=name, out_shape=(shp,) * 4, grid=(2, h // th, C // tc), prefetch=1,
                  in_specs=[blk, blk, blk, pick(True), pick(False)], out_specs=(blk,) * 4,
                  semantics=("parallel", "parallel", "parallel"))(c_idx, w, m, v, g_mine, g_other)


def _mesh_pos():
    x, y, c = lax.axis_index("x"), lax.axis_index("y"), lax.axis_index("c")
    others = [(1 - x, y), (x, 1 - y), (1 - x, 1 - y)]
    return x, y, c, others


def _gather_copies(shards, lands, send_sems, recv_sems):
    x, y, c, others = _mesh_pos()
    me = 2 * x + y
    return [pltpu.make_async_remote_copy(
        src_ref=shards[a].at[c], dst_ref=lands[a].at[me, c],
        send_sem=send_sems.at[3 * a + j], recv_sem=recv_sems.at[3 * a + j],
        device_id=(*chip, c), device_id_type=MESH)
        for a in range(len(shards)) for j, chip in enumerate(others)]


def _near_copies(shards, lands, send_sems, recv_sems):
    x, y, c, others = _mesh_pos()
    me = 2 * x + y
    return [pltpu.make_async_remote_copy(
        src_ref=shards[a].at[c], dst_ref=lands[a].at[me, c],
        send_sem=send_sems.at[2 * a + j], recv_sem=recv_sems.at[2 * a + j],
        device_id=(*chip, c), device_id_type=MESH)
        for a in range(len(shards)) for j, chip in enumerate(others[:2])]


def _relay_copies(shards, zones, send_sems, recv_sems):
    x, y, c, others = _mesh_pos()
    (nx, ny), copies = others[:2], []
    for a in range(len(zones)):
        hc = zones[a].shape[-1] // 2
        for k, (src_chip, to, lo) in enumerate(((ny, nx, 0), (nx, ny, hc))):
            part = zones[a].at[2 * src_chip[0] + src_chip[1], c, :, pl.ds(lo, hc)]
            copies.append(pltpu.make_async_remote_copy(
                src_ref=part, dst_ref=part, send_sem=send_sems.at[2 * a + k], recv_sem=recv_sems.at[2 * a + k],
                device_id=(*to, c), device_id_type=MESH))
    return copies


def _pass_copies(shards, zones, send_sems, recv_sems, pieces=(0, 1, 2, 3)):
    x, y, c, others = _mesh_pos()
    me = 2 * x + y
    copies = []
    for a in range(len(shards)):
        srcs = [zones[a].at[2 * chip[0] + chip[1], c] for chip in others] + [shards[a]]
        dsts = [zones[a].at[2 * chip[0] + chip[1], c] for chip in others] + [zones[a].at[me]]
        copies += [pltpu.make_async_remote_copy(
            src_ref=srcs[p], dst_ref=dsts[p], send_sem=send_sems.at[len(pieces) * a + k],
            recv_sem=recv_sems.at[len(pieces) * a + k], device_id=(x, y, 1 - c), device_id_type=MESH)
            for k, p in enumerate(pieces)]
    return copies


def _exchange_copies(grads, recvs, send_sems, recv_sems):
    x, y, c, _ = _mesh_pos()
    return [pltpu.make_async_remote_copy(
        src_ref=grads[a].at[:, 1 - c], dst_ref=recvs[a], send_sem=send_sems.at[a],
        recv_sem=recv_sems.at[a], device_id=(x, y, 1 - c), device_id_type=MESH) for a in range(len(grads))]


def _split_start(copies, per, srcs, zones, after, *, name):
    n = len(srcs)
    HBM = pl.BlockSpec(memory_space=pltpu.HBM)
    SEM = pl.BlockSpec(memory_space=pltpu.SEMAPHORE)

    def body(*refs):
        send_sems, recv_sems = refs[2 * n + 1], refs[2 * n + 2]
        for cp in copies(refs[:n], refs[n:2 * n], send_sems, recv_sems):
            cp.start()
        refs[-1][...] = jnp.zeros_like(refs[-1])

    hbm = lambda a: pltpu.HBM(a.shape, a.dtype)
    res = _pcall(body, name=name,
                 out_shape=(pltpu.SemaphoreType.DMA((per * n,)), pltpu.SemaphoreType.DMA((per * n,)),
                            *[hbm(a) for a in srcs], *[hbm(a) for a in zones],
                            jax.ShapeDtypeStruct((SUBLANES, LANES), F32)),
                 in_specs=[*[HBM] * (2 * n), pl.BlockSpec(memory_space=pl.ANY)],
                 out_specs=(SEM, SEM, *[HBM] * (2 * n), pl.BlockSpec(memory_space=pltpu.VMEM)),
                 aliases={i: 2 + i for i in range(2 * n)}, split_copy=True)(
        *[pltpu.with_memory_space_constraint(a, pltpu.HBM) for a in [*srcs, *zones]], after)
    return res[0], res[1], list(res[2:2 + n]), list(res[2 + n:2 + 2 * n]), res[-1]


def _split_wait(copies, send_sems, recv_sems, srcs, zones, after, *, name):
    n = len(srcs)
    HBM = pl.BlockSpec(memory_space=pltpu.HBM)
    SEM = pl.BlockSpec(memory_space=pltpu.SEMAPHORE)

    def body(*refs):
        for cp in copies(refs[:n], refs[n:2 * n], refs[2 * n], refs[2 * n + 1]):
            cp.wait_send()
            cp.wait_recv()

    hbm = lambda a: pltpu.HBM(a.shape, a.dtype)
    res = _pcall(body, name=name, out_shape=(*[hbm(a) for a in srcs], *[hbm(a) for a in zones]),
                 in_specs=[*[HBM] * (2 * n), SEM, SEM, pl.BlockSpec(memory_space=pl.ANY)],
                 out_specs=tuple([HBM] * (2 * n)), aliases={i: i for i in range(2 * n)},
                 split_copy=True)(*srcs, *zones, send_sems, recv_sems, after)
    return list(res[:n]), list(res[n:])


def _add_halves(grad, recv, c_idx, *, name):
    S, _, h, C = grad.shape
    th, tc = _blk(h, C)

    def body(c_ref, g_ref, r_ref, o_ref):
        o_ref[...] = (g_ref[...].astype(F32) + r_ref[...].astype(F32)).astype(o_ref.dtype)

    return _pcall(body, name=name, out_shape=jax.ShapeDtypeStruct((S, h, C), grad.dtype),
                  grid=(S, h // th, C // tc), prefetch=1,
                  in_specs=[pl.BlockSpec((None, None, th, tc), lambda s, i, j, c: (s, c[0], i, j)),
                            pl.BlockSpec((None, th, tc), lambda s, i, j, c: (s, i, j))],
                  out_specs=pl.BlockSpec((None, th, tc), lambda s, i, j, c: (s, i, j)),
                  semantics=("parallel", "parallel", "parallel"))(c_idx, grad, recv)


def _scatter_copies(srcs, lands, send_sems, recv_sems):
    x, y, c, others = _mesh_pos()
    return [pltpu.make_async_remote_copy(
        src_ref=srcs[a].at[2 * chip[0] + chip[1]], dst_ref=lands[a].at[j],
        send_sem=send_sems.at[3 * a + j], recv_sem=recv_sems.at[3 * a + j],
        device_id=(*chip, c), device_id_type=MESH)
        for a in range(len(srcs)) for j, chip in enumerate(others)]


def _add_chips(sums, recv, chip_idx, *, name):
    _, h, C = sums.shape
    th, tc = _blk(h, C)

    def body(k_ref, s_ref, r_ref, o_ref):
        acc = s_ref[...].astype(F32) + r_ref[0].astype(F32)
        acc = acc + r_ref[1].astype(F32)
        o_ref[...] = acc + r_ref[2].astype(F32)

    return _pcall(body, name=name, out_shape=jax.ShapeDtypeStruct((h, C), F32),
                  grid=(h // th, C // tc), prefetch=1,
                  in_specs=[pl.BlockSpec((None, th, tc), lambda i, j, k: (k[0], i, j)),
                            pl.BlockSpec((3, th, tc), lambda i, j, k: (0, i, j))],
                  out_specs=pl.BlockSpec((th, tc), lambda i, j, k: (i, j)),
                  semantics=("parallel", "parallel"))(chip_idx, sums, recv)


def _swap_copies(halves, others, send_sems, recv_sems):
    x, y, c, _ = _mesh_pos()
    return [pltpu.make_async_remote_copy(
        src_ref=halves[a], dst_ref=others[a], send_sem=send_sems.at[a], recv_sem=recv_sems.at[a],
        device_id=(x, y, 1 - c), device_id_type=MESH) for a in range(len(halves))]


def _all_reduce_small(buf):
    R, L = buf.shape
    NDEV = 8

    def body(x_ref, sum_ref, all_ref, send_sems, recv_sems, local_sem):
        x, y, c, others = _mesh_pos()
        me, sibling = (x, y, c), (x, y, 1 - c)

        def slot(px, py, pc):
            return all_ref.at[4 * px + 2 * py + pc]

        def copy(k, block, to, src=None):
            return pltpu.make_async_remote_copy(
                src_ref=slot(*block) if src is None else src, dst_ref=slot(*block),
                send_sem=send_sems.at[k], recv_sem=recv_sems.at[k], device_id=to, device_id_type=MESH)

        mine = pltpu.make_async_copy(x_ref, slot(*me), local_sem)
        mine.start()
        first = [copy(0, me, sibling, src=x_ref)]
        first += [copy(1 + j, me, (*chip, c), src=x_ref) for j, chip in enumerate(others)]
        for cp in first:
            cp.start()
        passed = [copy(4 + j, (*chip, c), sibling) for j, chip in enumerate(others)]
        for j, chip in enumerate(others):
            copy(1 + j, (*chip, c), me).wait_recv()
            passed[j].start()
        copy(0, sibling, me).wait_recv()
        for j, chip in enumerate(others):
            copy(4 + j, (*chip, 1 - c), me).wait_recv()
        for cp in first + passed:
            cp.wait_send()
        mine.wait()
        acc = all_ref[0]
        for d in range(1, NDEV):
            acc = acc + all_ref[d]
        sum_ref[...] = acc

    VM = pl.BlockSpec(memory_space=pltpu.VMEM)
    return _pcall(body, name="all_reduce_small",
                  out_shape=(jax.ShapeDtypeStruct((R, L), F32), jax.ShapeDtypeStruct((NDEV, R, L), F32)),
                  in_specs=[VM], out_specs=(VM, VM),
                  scratch_shapes=[pltpu.SemaphoreType.DMA((7,)), pltpu.SemaphoreType.DMA((7,)),
                                  pltpu.SemaphoreType.DMA])(buf)[0]


def _pack(arrs, rows_multiple=16):
    flat = [a.reshape(-1).astype(F32) for a in arrs]
    sizes = [f.shape[0] for f in flat]
    total = sum(sizes)
    per = LANES * rows_multiple
    padded = -(-total // per) * per
    flat.append(jnp.zeros((padded - total,), F32))
    offs = [0]
    for s in sizes:
        offs.append(offs[-1] + s)
    return jnp.concatenate(flat).reshape(padded // LANES, LANES), offs


def _unpack(buf, offs, shapes):
    flat = buf.reshape(-1)
    return [flat[offs[i]:offs[i + 1]].reshape(s) for i, s in enumerate(shapes)]


def kernel(x, mem, g_mix, w_in, w_a2, b_a, g_gla, w_pool, pool_scale, w_branch, w_out, g_cross, g_mem, w_cq, w_ckv, w_co, g_ffn, w_up, conv_w, conv_b, w_down, g_final, loss_target, m_g_mix, m_w_in, m_w_a2, m_b_a, m_g_gla, m_w_pool, m_pool_scale, m_w_branch, m_w_out, m_g_cross, m_g_mem, m_w_cq, m_w_ckv, m_w_co, m_g_ffn, m_w_up, m_conv_w, m_conv_b, m_w_down, m_g_final, v_g_mix, v_w_in, v_w_a2, v_b_a, v_g_gla, v_w_pool, v_pool_scale, v_w_branch, v_w_out, v_g_cross, v_g_mem, v_w_cq, v_w_ckv, v_w_co, v_g_ffn, v_w_up, v_conv_w, v_conv_b, v_w_down, v_g_final):
    weights = dict(g_mix=g_mix, w_in=w_in, w_a2=w_a2, b_a=b_a, g_gla=g_gla, w_pool=w_pool,
                   pool_scale=pool_scale, w_branch=w_branch, w_out=w_out, g_cross=g_cross, g_mem=g_mem,
                   w_cq=w_cq, w_ckv=w_ckv, w_co=w_co, g_ffn=g_ffn, w_up=w_up, conv_w=conv_w,
                   conv_b=conv_b, w_down=w_down, g_final=g_final)
    mom_m = dict(g_mix=m_g_mix, w_in=m_w_in, w_a2=m_w_a2, b_a=m_b_a, g_gla=m_g_gla, w_pool=m_w_pool,
                 pool_scale=m_pool_scale, w_branch=m_w_branch, w_out=m_w_out, g_cross=m_g_cross,
                 g_mem=m_g_mem, w_cq=m_w_cq, w_ckv=m_w_ckv, w_co=m_w_co, g_ffn=m_g_ffn, w_up=m_w_up,
                 conv_w=m_conv_w, conv_b=m_conv_b, w_down=m_w_down, g_final=m_g_final)
    mom_v = dict(g_mix=v_g_mix, w_in=v_w_in, w_a2=v_w_a2, b_a=v_b_a, g_gla=v_g_gla, w_pool=v_w_pool,
                 pool_scale=v_pool_scale, w_branch=v_w_branch, w_out=v_w_out, g_cross=v_g_cross,
                 g_mem=v_g_mem, w_cq=v_w_cq, w_ckv=v_w_ckv, w_co=v_w_co, g_ffn=v_g_ffn, w_up=v_w_up,
                 conv_w=v_conv_w, conv_b=v_conv_b, w_down=v_w_down, g_final=v_g_final)
    order = list(weights)
    big = ["w_in", "w_branch", "w_out", "w_cq", "w_ckv", "w_co", "w_up", "w_down"]
    small_sharded = ["w_a2", "w_pool", "conv_w"]
    small_repl = ["g_mix", "b_a", "g_gla", "pool_scale", "g_cross", "g_mem", "g_ffn", "conv_b", "g_final"]

    xs, ms, tgt = x[0], mem[0], loss_target[0]
    T, D = xs.shape
    M = ms.shape[0]
    DK, DV, PW = b_a.shape[1], g_gla.shape[1], pool_scale.shape[1]
    RANK = w_a2.shape[1]
    F2 = conv_b.shape[1]
    F = F2 // 2
    DIN = N_CHIPS * w_in.shape[2]
    OFF_A = 2 * DK + 2 * DV
    OFF_P = OFF_A + RANK
    RP = LANES
    GW = PW // POOL_GROUPS
    assert PW == DV and 4 * DV == 2 * D and OFF_P + PW + 2 * D == DIN

    cx, cy, cc = lax.axis_index("x"), lax.axis_index("y"), lax.axis_index("c")
    chip = 2 * cx + cy
    c_idx = jnp.reshape(cc, (1,)).astype(jnp.int32)
    chip_idx = jnp.reshape(chip, (1,)).astype(jnp.int32)

    def halves(a):
        return a.reshape(2, a.shape[0] // 2, a.shape[1])

    shard2d = {k: (weights[k][0].T if k == "w_in" else weights[k][0]) for k in big}
    small_pack, small_offs = _pack([weights[k][0] for k in small_sharded], rows_multiple=32)
    flying, passing = {}, {}

    def gather_start(group, keys, tok):
        srcs = [small_pack if k == "small" else shard2d[k].astype(BF16) for k in keys]
        if group != "in":
            srcs = [a + tok[0:1, 0:1].astype(a.dtype) for a in srcs]
        srcs = [halves(a) for a in srcs]
        zones = [lax.empty((N_CHIPS, *s.shape), s.dtype) for s in srcs]
        first = (_near_copies, 2) if group == "in" else (_gather_copies, 3)
        s_sems, r_sems, srcs, zones, tok = _split_start(*first, srcs, zones, tok, name=f"gather_start_{group}")
        flying[group] = (keys, s_sems, r_sems, srcs, zones)
        return tok

    tok = gather_start("in", ["w_in"], xs)

    def arrive_in(after):
        keys, s_sems, r_sems, srcs, zones = flying["in"]
        near, diag = functools.partial(_pass_copies, pieces=(0, 1, 3)), functools.partial(_pass_copies, pieces=(2,))
        srcs, zones = _split_wait(_near_copies, s_sems, r_sems, srcs, zones, after, name="gather_wait_in")
        rs, rr, srcs, zones, tok = _split_start(_relay_copies, 2, srcs, zones, after, name="gather_relay_start_in")
        ns, nr, srcs, zones, tok = _split_start(near, 3, srcs, zones, tok, name="gather_pass_near_start_in")
        for group, group_keys in (("mix", ["w_branch", "w_out", "small"]), ("cross", ["w_cq", "w_ckv", "w_co"]),
                                  ("up", ["w_up"]), ("down", ["w_down"])):
            tok = gather_start(group, group_keys, tok)
        after = tok
        srcs, zones = _split_wait(_relay_copies, rs, rr, srcs, zones, after, name="gather_relay_wait_in")
        ds, dr, srcs, zones, _ = _split_start(diag, 1, srcs, zones, after, name="gather_pass_diag_start_in")
        srcs, zones = _split_wait(near, ns, nr, srcs, zones, after, name="gather_pass_near_wait_in")
        _, full = _split_wait(diag, ds, dr, srcs, zones, after, name="gather_pass_diag_wait_in")
        return {k: f.reshape(N_CHIPS, f.shape[1] * f.shape[2], f.shape[3]) for k, f in zip(keys, full)}

    def landed(group, after):
        keys, s_sems, r_sems, srcs, zones = flying[group]
        srcs, zones = _split_wait(_gather_copies, s_sems, r_sems, srcs, zones, after,
                                  name=f"gather_wait_{group}")
        s_sems, r_sems, srcs, zones, token = _split_start(_pass_copies, 4, srcs, zones, after,
                                                          name=f"gather_pass_start_{group}")
        passing[group] = (keys, s_sems, r_sems, srcs, zones)
        return token

    def arrive(group, after):
        keys, s_sems, r_sems, srcs, zones = passing[group]
        _, full = _split_wait(_pass_copies, s_sems, r_sems, srcs, zones, after,
                              name=f"gather_pass_wait_{group}")
        return {k: f.reshape(N_CHIPS, f.shape[1] * f.shape[2], f.shape[3]) for k, f in zip(keys, full)}

    def rows(g):
        return g.reshape(-1, g.shape[2])

    h1, r1 = _rms_fwd(xs, g_mix + tok[0:1, 0:1], name="norm_mix")
    W_in = rows(arrive_in(h1)["w_in"])
    W_main = jnp.concatenate([W_in[:OFF_A], W_in[OFF_P:]], axis=0)
    W_a = jnp.pad(W_in[OFF_A:OFF_P], ((0, RP - RANK), (0, 0)))
    tok = landed("mix", W_a)
    proj = _mm(h1, W_main, "nt", name="proj_main", out_dtype=F32, after=tok)
    gw = arrive("mix", proj)
    W_branch, W_out, small_all = rows(gw["w_branch"]), rows(gw["w_out"]), gw["small"]
    sm = [_unpack(small_all[j], small_offs, [weights[k].shape[1:] for k in small_sharded]) for j in range(N_CHIPS)]
    W_a2 = jnp.concatenate([sm[j][0] for j in range(N_CHIPS)], axis=1)
    W_a2p = jnp.pad(W_a2, ((0, RP - RANK), (0, 0))).astype(BF16)
    W_pool = jnp.concatenate([sm[j][1] for j in range(N_CHIPS)], axis=1).astype(BF16)
    W_conv = jnp.concatenate([sm[j][2] for j in range(N_CHIPS)], axis=1)

    a_pad = _mm(h1, W_a, "nt", name="proj_gate_rank", out_dtype=F32)
    o_gla, o_raw, states = _gla_fwd(proj, a_pad, W_a2p, b_a, g_gla, T=T, DK=DK, DV=DV)
    o_pool = _pool_fwd(proj, W_pool, pool_scale, T=T, PW=PW, col_block=3)
    tok = landed("cross", o_pool)
    y_gla = _mm(o_gla, W_branch, "nn", name="branch_gla", out_dtype=BF16, K=DV, after=tok)
    y_pool = _mm(o_pool, W_branch, "nn", name="branch_pool", out_dtype=BF16, K=PW, b_off=(DV, 0))
    merged = _merge_fwd(y_gla, y_pool, proj, T=T, D=D, col_block=2)
    x1 = _mm(merged, W_out, "nn", name="mix_out", out_dtype=F32, add=xs)

    h2, r2 = _rms_fwd(x1, g_cross, name="norm_cross")
    mem_n, rm = _rms_fwd(ms, g_mem, name="norm_mem")
    gw = arrive("cross", h2)
    W_cq, W_ckv, W_co = rows(gw["w_cq"]), gw["w_ckv"], rows(gw["w_co"])
    qc = _mm(h2, W_cq, "nn", name="cross_q", out_dtype=BF16)
    kv = _mm(mem_n, W_ckv, "nn", name="cross_kv", out_dtype=BF16, b_blocked=True)
    o_att = _attn_fwd(qc, kv, T=T, D=D, M=M)
    x2 = _mm(o_att, W_co, "nn", name="cross_out", out_dtype=F32, add=x1)

    tok = landed("up", x2)
    h3, r3 = _rms_fwd(x2, g_ffn + tok[0:1, 0:1], name="norm_ffn")
    W_up = arrive("up", h3)["w_up"]
    u0 = _mm(h3, W_up, "nn", name="ffn_up", out_dtype=F32, b_blocked=True)
    tok = landed("down", u0)
    f_act = _conv_fwd(u0, W_conv, conv_b + tok[0:1, 0:1], T=T, F=F)
    W_down = rows(arrive("down", f_act)["w_down"])
    x3 =_mm(f_act, W_down, "nn", name="ffn_down", out_dtype=F32, add=x2)

    loss_part, dx3, dx3_b, dg_final = _loss_head(x3, g_final.reshape(1, D), tgt)

    def col_shards(g):
        nb, K, Nb = g.shape
        return g.reshape(nb, 2, K // 2, Nb)

    def row_shards(g):
        R, N = g.shape
        return g.reshape(N_CHIPS, 2, R // N_CHIPS // 2, N)

    exchanging, in_flight = {}, []

    def exchange_start(group, keys, partials, after):
        recvs = [lax.empty((p.shape[0], *p.shape[2:]), p.dtype) for p in partials]
        s_sems, r_sems, partials, recvs, token = _split_start(
            _exchange_copies, 1, partials, recvs, after, name=f"grad_exchange_start_{group}")
        exchanging[group] = (keys, s_sems, r_sems, partials, recvs)
        return token

    def scatter_start(group, after):
        keys, s_sems, r_sems, partials, recvs = exchanging[group]
        partials, recvs = _split_wait(_exchange_copies, s_sems, r_sems, partials, recvs, after,
                                      name=f"grad_exchange_wait_{group}")
        chip_sums = [_add_halves(p, r, c_idx, name=f"grad_add_halves_{k}")
                     for k, p, r in zip(keys, partials, recvs)]
        lands = [lax.empty((3, *s.shape[1:]), s.dtype) for s in chip_sums]
        s_sems, r_sems, sums, lands, token = _split_start(
            _scatter_copies, 3, chip_sums, lands, after, name=f"grad_scatter_start_{group}")
        in_flight.append((group, keys, s_sems, r_sems, sums, lands))
        return token

    collected = []

    def collect(after):
        group, keys, s_sems, r_sems, sums, lands = in_flight.pop(0)
        sums, from_chips = _split_wait(_scatter_copies, s_sems, r_sems, sums, lands, after,
                                       name=f"grad_scatter_wait_{group}")
        half_sums = [_add_chips(s, r, chip_idx, name=f"grad_add_chips_{k}") for k, s, r in zip(keys, sums, from_chips)]
        others = [lax.empty(h.shape, h.dtype) for h in half_sums]
        s_sems, r_sems, half_sums, others, token = _split_start(
            _swap_copies, 1, half_sums, others, after, name=f"grad_swap_start_{group}")
        collected.append((keys, s_sems, r_sems, half_sums, others))
        return token

    df = _mm(dx3_b, W_down, "nt", name="d_ffn_act", out_dtype=BF16)
    dW_down = _mm(f_act, dx3_b, "tn", name="dw_down", out_dtype=BF16)
    du0, dconv_w, dconv_b = _conv_bwd(u0, W_conv, conv_b, df, T=T, F=F)
    dh3 = _mm(du0, W_up, "nt", name="d_ffn_in", out_dtype=F32, b_blocked=True, tk=F2 // N_CHIPS)
    dW_up = _mm(h3, du0, "tn", name="dw_up", out_dtype=BF16, out_blocks=N_CHIPS)
    tok = exchange_start("ffn", ["w_down", "w_up"], [row_shards(dW_down), col_shards(dW_up)], dh3)
    dx2, dx2_b, dg_ffn = _rms_bwd(dh3, x2, r3 + tok[0:1, 0:1], g_ffn, dx3, name="norm_ffn_bwd")

    do_att = _mm(dx2_b, W_co, "nt", name="d_cross_o", out_dtype=BF16)
    dW_co = _mm(o_att, dx2_b, "tn", name="dw_co", out_dtype=BF16)
    tok = scatter_start("ffn", dW_co)
    dq, dkv = _attn_bwd(qc, kv, do_att, T=T, D=D, M=M)
    dkv_b = dkv.astype(BF16)
    dW_cq = _mm(h2, dq, "tn", name="dw_cq", out_dtype=BF16, after=tok)
    dh2 = _mm(dq, W_cq, "nt", name="d_cross_in", out_dtype=F32)
    dW_ckv = _mm(mem_n, dkv_b, "tn", name="dw_ckv", out_dtype=BF16, out_blocks=N_CHIPS)
    dmem_n = _mm(dkv_b, W_ckv, "nt", name="d_mem", out_dtype=F32, b_blocked=True)
    tok = exchange_start("cross", ["w_co", "w_cq", "w_ckv"],
                         [row_shards(dW_co), row_shards(dW_cq), col_shards(dW_ckv)], dmem_n)
    _, _, dg_mem = _rms_bwd(dmem_n, ms, rm, g_mem, None, name="norm_mem_bwd")
    dx1, dx1_b, dg_cross = _rms_bwd(dh2, x1, r2 + tok[0:1, 0:1], g_cross, dx2, name="norm_cross_bwd")

    dmerged = _mm(dx1_b, W_out, "nt", name="d_merged", out_dtype=BF16)
    dW_out = _mm(merged, dx1_b, "tn", name="dw_out", out_dtype=BF16)
    tok = scatter_start("cross", dW_out)
    dy_gla, dy_pool, dgates = _merge_bwd(dmerged, y_gla, y_pool, proj, T=T, D=D, col_block=2)
    dW_br_gla = _mm(o_gla, dy_gla, "tn", name="dw_branch_gla", out_dtype=BF16, after=tok)
    dW_br_pool = _mm(o_pool, dy_pool, "tn", name="dw_branch_pool", out_dtype=BF16)
    do_gla = _mm(dy_gla, W_branch, "nt", name="d_o_gla", out_dtype=F32, N=DV)
    do_pool = _mm(dy_pool, W_branch, "nt", name="d_o_pool", out_dtype=F32, N=PW, b_off=(DV, 0))
    dp, dw_pool, dpool_scale = _pool_bwd(proj, W_pool, pool_scale, do_pool, T=T, PW=PW, col_block=3)
    dW_pool = jnp.transpose(dw_pool.reshape(POOL_GROUPS, N_CHIPS, GW // N_CHIPS, GW), (1, 0, 2, 3))
    tok = exchange_start("mix", ["w_out", "w_branch", "w_pool"],
                         [row_shards(dW_out), row_shards(jnp.concatenate([dW_br_gla, dW_br_pool], axis=0)),
                          row_shards(dW_pool.reshape(N_CHIPS * POOL_GROUPS * (GW // N_CHIPS), GW).astype(BF16))],
                         dp)
    dqkvr, da_pad, dw2, db_a, dg_gla = _gla_bwd(proj, a_pad, W_a2p, b_a + tok[0:1, 0:1], g_gla, o_raw, states,
                                               do_gla, T=T, DK=DK, DV=DV)
    tok = scatter_start("mix", dqkvr)
    dproj = jnp.concatenate([dqkvr, dp, dgates], axis=1)
    dW_main = _mm(dproj, h1, "tn", name="dw_in_main", out_dtype=BF16, after=tok)
    dW_a = _mm(da_pad, h1, "tn", name="dw_in_rank", out_dtype=BF16)
    dW_in = jnp.concatenate([dW_main[:OFF_A], dW_a[:RANK], dW_main[OFF_A:]], axis=0)
    tok = exchange_start("in", ["w_in"], [row_shards(dW_in)], dW_a)
    dh1 = _mm(dproj, W_main, "nn", name="d_mix_in_main", out_dtype=F32, after=tok)
    dh1 = _mm(da_pad, W_a, "nn", name="d_mix_in_rank", out_dtype=F32, add=dh1)
    dx0, _, dg_mix = _rms_bwd(dh1, xs, r1, g_mix, dx1, name="norm_mix_bwd")

    grads = {}

    small_grads = [loss_part, dg_mix, db_a, dg_gla, dpool_scale, dg_cross, dg_mem, dg_ffn, dconv_b, dg_final,
                   dw2[:RANK], dconv_w]
    small_buf, offs = _pack(small_grads)
    small_sum = _all_reduce_small(small_buf)
    red = _unpack(small_sum, offs, [g.shape for g in small_grads])
    loss = red[0][0, 0]
    for k, g in zip(small_repl, red[1:10]):
        grads[k] = g.reshape(weights[k].shape)
    nb = DK // N_CHIPS
    grads["w_a2"] = lax.dynamic_slice_in_dim(red[10], chip * nb, nb, axis=1)[None]
    nb = F2 // N_CHIPS
    grads["conv_w"] = lax.dynamic_slice_in_dim(red[11], chip * nb, nb, axis=1)[None]

    delta, new_m, new_v = {}, {}, {}

    def shard_rows(k, a):
        a = a[0]
        return a.T if k == "w_in" else a.reshape(-1, a.shape[-1])

    def whole(k, a):
        a = a.reshape(-1, a.shape[2])
        return (a.T if k == "w_in" else a).reshape(weights[k].shape)

    scatter_start("in", small_sum)

    def finish(after):
        keys, s_sems, r_sems, mine, others = collected.pop(0)
        mine, others = _split_wait(_swap_copies, s_sems, r_sems, mine, others, after,
                                   name=f"grad_swap_wait_{keys[0]}")
        for k, g_mine, g_other in zip(keys, mine, others):
            wmv = [halves(shard_rows(k, src[k])) for src in (weights, mom_m, mom_v)]
            res = _adamw_halves(*wmv, g_mine, g_other, c_idx, name=f"adamw_{k}")
            grads[k], delta[k], new_m[k], new_v[k] = (whole(k, a) for a in res)
        return res[1]

    after = in_flight[-1][4][0]
    while in_flight:
        after = collect(after)
        while len(collected) > 1:
            after = finish(after)
    finish(after)
    small = small_repl + ["w_a2", "conv_w"]
    packs = [_pack([src[k] for k in small])[0] for src in (weights, grads, mom_m, mom_v)]
    _, offs = _pack([weights[k] for k in small])
    outs = _adamw(*packs, name="adamw_small")
    for res, o in zip((delta, new_m, new_v), outs):
        for k, a in zip(small, _unpack(o, offs, [weights[k].shape for k in small])):
            res[k] = a

    return (loss, dx0[None], *[grads[k] for k in order], *[delta[k] for k in order],
            *[new_m[k] for k in order], *[new_v[k] for k in order])
```

```python
import functools

import jax
import jax.numpy as jnp
from jax import lax
from jax.experimental import pallas as pl
from jax.experimental.pallas import tpu as pltpu

F32 = jnp.float32
BF16 = jnp.bfloat16
MESH = pl.DeviceIdType.MESH
HIGHEST = lax.Precision.HIGHEST

EPS = 1e-6
GLA_HEADS = 4
GLA_CHUNK = 64
GLA_GATE_NORM = 16.0
POOL_GROUPS = 4
CROSS_HEADS = 4
CONV_W = 3
N_CHIPS = 4
LANES = 128
SUBLANES = 8
VMEM_LIMIT = 56 << 20

ADAM_LR = 0.001
ADAM_B1 = 0.9
ADAM_B2 = 0.999
ADAM_EPS = 1e-08
ADAM_WD = 0.01
ADAM_STEP = 10

NN = (((1,), (0,)), ((), ()))
NT = (((1,), (1,)), ((), ()))
TN = (((0,), (0,)), ((), ()))


ONE_PASS = lax.Precision.HIGH


def _dot(a, b, dn=NN, precision=None):
    return lax.dot_general(a, b, dn, precision=precision, preferred_element_type=F32)


def _tile(n, pref, align=LANES):
    t = (min(pref, n) // align) * align
    while t >= align:
        if n % t == 0:
            return t
        t -= align
    return n


def _pcall(body, *, name, out_shape, grid=(), in_specs=None, out_specs=None, scratch_shapes=(),
           semantics=None, prefetch=0, aliases=None, split_copy=False):
    params = dict(vmem_limit_bytes=VMEM_LIMIT)
    if semantics is not None:
        params["dimension_semantics"] = semantics
    if split_copy:
        params["has_side_effects"] = pltpu.SideEffectType.DATAFLOW_SIDE_EFFECTING
    if prefetch:
        grid_spec = pltpu.PrefetchScalarGridSpec(
            num_scalar_prefetch=prefetch, grid=grid, in_specs=in_specs, out_specs=out_specs,
            scratch_shapes=scratch_shapes)
        return pl.pallas_call(body, name=name, out_shape=out_shape, grid_spec=grid_spec,
                              compiler_params=pltpu.CompilerParams(**params))
    kw = {}
    if aliases is not None:
        kw["input_output_aliases"] = aliases
    if in_specs is not None:
        kw["in_specs"] = in_specs
    if out_specs is not None:
        kw["out_specs"] = out_specs
    return pl.pallas_call(body, name=name, out_shape=out_shape, grid=grid,
                          scratch_shapes=scratch_shapes,
                          compiler_params=pltpu.CompilerParams(**params), **kw)


def _sigmoid(x):
    return 1.0 / (1.0 + jnp.exp(-x))


def _log_sigmoid(x):
    return jnp.minimum(x, 0.0) - jnp.log(1.0 + jnp.exp(-jnp.abs(x)))


def _mm(a, b, mode, *, name, out_dtype, M=None, N=None, K=None, a_off=(0, 0), b_off=(0, 0),
        add=None, b_blocked=False, out_blocks=0, after=None, tm=1536, tn=1536, tk=2048):
    if b_blocked:
        nb, R, Cb = b.shape
        b_rows, b_cols = R, nb * Cb
    else:
        b_rows, b_cols = b.shape
    if mode == "nn":
        M = M or a.shape[0]; K = K or a.shape[1]; N = N or b_cols
    elif mode == "nt":
        M = M or a.shape[0]; K = K or a.shape[1]; N = N or b_rows
    else:
        K = K or a.shape[0]; M = M or a.shape[1]; N = N or b_cols
    tm = _tile(M, tm, LANES if mode == "tn" else 16)
    tn = _tile(Cb if (b_blocked and mode != "nt") else (N // out_blocks if out_blocks else N), tn)
    tk = _tile(Cb if (b_blocked and mode == "nt") else K, tk)
    nk = K // tk
    dn = {"nn": NN, "nt": NT, "tn": TN}[mode]

    def off(o, t):
        assert o % t == 0, (name, o, t)
        return o // t

    if mode == "tn":
        ar, ac = off(a_off[0], tk), off(a_off[1], tm)
        a_spec = pl.BlockSpec((tk, tm), lambda i, j, k: (k + ar, i + ac))
    else:
        ar, ac = off(a_off[0], tm), off(a_off[1], tk)
        a_spec = pl.BlockSpec((tm, tk), lambda i, j, k: (i + ar, k + ac))
    if b_blocked and mode == "nt":
        per = Cb // tk
        b_spec = pl.BlockSpec((None, tn, tk), lambda i, j, k: (k // per, j, k % per))
    elif b_blocked:
        per = Cb // tn
        b_spec = pl.BlockSpec((None, tk, tn), lambda i, j, k: (j // per, k, j % per))
    elif mode == "nt":
        br, bc = off(b_off[0], tn), off(b_off[1], tk)
        b_spec = pl.BlockSpec((tn, tk), lambda i, j, k: (j + br, k + bc))
    else:
        br, bc = off(b_off[0], tk), off(b_off[1], tn)
        b_spec = pl.BlockSpec((tk, tn), lambda i, j, k: (k + br, j + bc))
    if out_blocks:
        per_o = N // out_blocks // tn
        o_spec = pl.BlockSpec((None, tm, tn), lambda i, j, k: (j // per_o, i, j % per_o))
        out_shape = jax.ShapeDtypeStruct((out_blocks, M, N // out_blocks), out_dtype)
    else:
        o_spec = pl.BlockSpec((tm, tn), lambda i, j, k: (i, j))
        out_shape = jax.ShapeDtypeStruct((M, N), out_dtype)
    in_specs = [a_spec, b_spec]
    args = [a, b]
    if add is not None:
        assert not out_blocks
        in_specs.append(o_spec)
        args.append(add)
    if after is not None:
        in_specs.append(pl.BlockSpec(memory_space=pl.ANY))
        args.append(after)
    n_in = len(args)

    def finish(r, refs):
        if add is not None:
            r = r + refs[2][...]
        o_ref = refs[n_in]
        o_ref[...] = r.astype(o_ref.dtype)

    def body_one(*refs):
        finish(_dot(refs[0][...].astype(BF16), refs[1][...].astype(BF16), dn), refs)

    def body_acc(*refs):
        acc_ref = refs[-1]
        k = pl.program_id(2)

        @pl.when(k == 0)
        def _():
            acc_ref[...] = jnp.zeros_like(acc_ref)

        acc_ref[...] += _dot(refs[0][...].astype(BF16), refs[1][...].astype(BF16), dn)

        @pl.when(k == nk - 1)
        def _():
            finish(acc_ref[...], refs)

    return _pcall(body_one if nk == 1 else body_acc, name=name, out_shape=out_shape,
                  grid=(M // tm, N // tn, nk), in_specs=in_specs, out_specs=o_spec,
                  scratch_shapes=[] if nk == 1 else [pltpu.VMEM((tm, tn), F32)],
                  semantics=("parallel", "parallel", "arbitrary"))(*args)


def _rms_fwd(x, g, *, name):
    T, D = x.shape
    tr = _tile(T, 128, 16)

    def body(x_ref, g_ref, h_ref, r_ref):
        xv = x_ref[...]
        r = lax.rsqrt(jnp.mean(xv * xv, axis=-1, keepdims=True) + EPS)
        h_ref[...] = (xv * r * g_ref[...]).astype(h_ref.dtype)
        r_ref[...] = r

    row = pl.BlockSpec((tr, D), lambda i: (i, 0))
    return _pcall(body, name=name,
                  out_shape=(jax.ShapeDtypeStruct((T, D), BF16), jax.ShapeDtypeStruct((T, 1), F32)),
                  grid=(T // tr,),
                  in_specs=[row, pl.BlockSpec((1, D), lambda i: (0, 0))],
                  out_specs=(row, pl.BlockSpec((tr, 1), lambda i: (i, 0))),
                  semantics=("parallel",))(x, g)


def _rms_bwd(dh, x, rstd, g, dres, *, name):
    T, D = x.shape
    tr = _tile(T, 128, 16)
    has_res = dres is not None

    def body(*refs):
        if has_res:
            dh_ref, x_ref, r_ref, g_ref, res_ref, dx_ref, dxb_ref, dg_ref = refs
        else:
            dh_ref, x_ref, r_ref, g_ref, dx_ref, dxb_ref, dg_ref = refs
        r = r_ref[...]
        xh = x_ref[...] * r
        dhv = dh_ref[...].astype(F32)
        dxh = dhv * g_ref[...]
        m = jnp.mean(dxh * xh, axis=-1, keepdims=True)
        dx = r * (dxh - xh * m)
        if has_res:
            dx = dx + res_ref[...]
        dx_ref[...] = dx
        dxb_ref[...] = dx.astype(BF16)

        @pl.when(pl.program_id(0) == 0)
        def _():
            dg_ref[...] = jnp.zeros_like(dg_ref)

        dg_ref[...] += jnp.sum(dhv * xh, axis=0, keepdims=True)

    row = pl.BlockSpec((tr, D), lambda i: (i, 0))
    vec = pl.BlockSpec((1, D), lambda i: (0, 0))
    in_specs = [row, row, pl.BlockSpec((tr, 1), lambda i: (i, 0)), vec]
    args = [dh, x, rstd, g]
    if has_res:
        in_specs.append(row)
        args.append(dres)
    return _pcall(body, name=name,
                  out_shape=(jax.ShapeDtypeStruct((T, D), F32), jax.ShapeDtypeStruct((T, D), BF16),
                             jax.ShapeDtypeStruct((1, D), F32)),
                  grid=(T // tr,), in_specs=in_specs, out_specs=(row, row, vec),
                  semantics=("arbitrary",))(*args)


def _loss_head(x3, g, tgt):
    T, D = x3.shape
    tr = _tile(T, 128, 16)

    def body(x_ref, g_ref, t_ref, loss_ref, dx_ref, dxb_ref, dg_ref):
        xv = x_ref[...]
        gv = g_ref[...]
        r = lax.rsqrt(jnp.mean(xv * xv, axis=-1, keepdims=True) + EPS)
        xh = xv * r
        err = xh * gv - t_ref[...]
        dy = err * (1.0 / D)
        dxh = dy * gv
        m = jnp.mean(dxh * xh, axis=-1, keepdims=True)
        dx = r * (dxh - xh * m)
        dx_ref[...] = dx
        dxb_ref[...] = dx.astype(BF16)

        @pl.when(pl.program_id(0) == 0)
        def _():
            dg_ref[...] = jnp.zeros_like(dg_ref)
            loss_ref[...] = jnp.zeros_like(loss_ref)

        dg_ref[...] += jnp.sum(dy * xh, axis=0, keepdims=True)
        part = 0.5 * jnp.sum(jnp.mean(err * err, axis=-1, keepdims=True), axis=0, keepdims=True)
        loss_ref[...] += jnp.broadcast_to(part, loss_ref.shape)

    row = pl.BlockSpec((tr, D), lambda i: (i, 0))
    vec = pl.BlockSpec((1, D), lambda i: (0, 0))
    return _pcall(body, name="loss_head",
                  out_shape=(jax.ShapeDtypeStruct((1, LANES), F32), jax.ShapeDtypeStruct((T, D), F32),
                             jax.ShapeDtypeStruct((T, D), BF16), jax.ShapeDtypeStruct((1, D), F32)),
                  grid=(T // tr,), in_specs=[row, vec, row],
                  out_specs=(pl.BlockSpec((1, LANES), lambda i: (0, 0)), row, row, vec),
                  semantics=("arbitrary",))(x3, g, tgt)


def _gla_chunk_terms(qk, a_ref, w2_ref, ba_ref, DK):
    C = qk.shape[0]
    gp = _dot(a_ref[...].astype(BF16), w2_ref[...]) + ba_ref[...]
    la = _log_sigmoid(gp) * (1.0 / GLA_GATE_NORM)
    row = lax.broadcasted_iota(jnp.int32, (C, C), 0)
    col = lax.broadcasted_iota(jnp.int32, (C, C), 1)
    causal = row >= col
    b = _dot(causal.astype(F32), la, precision=HIGHEST)
    return gp, b, causal


def _gla_fwd(proj, a_pad, w2, b_a, g_gla, *, T, DK, DV):
    assert 2 * DK == DV
    H = GLA_HEADS
    HK, HV = DK // H, DV // H
    C = GLA_CHUNK
    n = T // C
    RP = a_pad.shape[1]
    scale = HK ** -0.5

    def body(qk_ref, v_ref, r_ref, a_ref, w2_ref, ba_ref, gg_ref, og_ref, oraw_ref, st_ref, s_ref):
        @pl.when(pl.program_id(0) == 0)
        def _():
            s_ref[...] = jnp.zeros_like(s_ref)

        st_ref[...] = s_ref[...]
        qk = qk_ref[...]
        _, b, causal = _gla_chunk_terms(qk, a_ref, w2_ref, ba_ref, DK)
        for h in range(H):
            ks = slice(h * HK, (h + 1) * HK)
            vs = slice(h * HV, (h + 1) * HV)
            bh = b[:, ks]
            b_last = bh[C - 1:C, :]
            qt = qk[:, ks] * scale * jnp.exp(bh)
            kh = qk[:, DK + h * HK:DK + (h + 1) * HK]
            kt = kh * jnp.exp(-bh)
            khat = kh * jnp.exp(b_last - bh)
            a_mat = jnp.where(causal, _dot(qt, kt, NT, ONE_PASS), 0.0)
            vh = v_ref[:, vs]
            s_t = s_ref[h]
            o = _dot(a_mat, vh, NN, ONE_PASS) + _dot(qt, s_t, NT, ONE_PASS)
            s_ref[h] = s_t * jnp.exp(b_last) + _dot(vh, khat, TN, ONE_PASS)
            rs = lax.rsqrt(jnp.mean(o * o, axis=-1, keepdims=True) + EPS)
            rr = r_ref[:, vs]
            og = o * rs * gg_ref[:, vs] * (rr * _sigmoid(rr))
            oraw_ref[:, vs] = o
            og_ref[:, vs] = og.astype(BF16)

    blk = lambda j: pl.BlockSpec((C, DV), lambda i: (i, j))
    full = lambda s: pl.BlockSpec(s, lambda i: (0,) * len(s))
    return _pcall(
        body, name="gla_fwd",
        out_shape=(jax.ShapeDtypeStruct((T, DV), BF16), jax.ShapeDtypeStruct((T, DV), F32),
                   jax.ShapeDtypeStruct((n, H, HV, HK), F32)),
        grid=(n,),
        in_specs=[blk(0), blk(1), blk(2), pl.BlockSpec((C, RP), lambda i: (i, 0)),
                  full((RP, DK)), full((1, DK)), full((1, DV))],
        out_specs=(blk(0), blk(0), pl.BlockSpec((None, H, HV, HK), lambda i: (i, 0, 0, 0))),
        scratch_shapes=[pltpu.VMEM((H, HV, HK), F32)],
        semantics=("arbitrary",))(proj, proj, proj, a_pad, w2, b_a, g_gla)


def _gla_bwd(proj, a_pad, w2, b_a, g_gla, o_raw, states, do_gla, *, T, DK, DV):
    H = GLA_HEADS
    HK, HV = DK // H, DV // H
    C = GLA_CHUNK
    n = T // C
    RP = a_pad.shape[1]
    scale = HK ** -0.5

    def body(qk_ref, v_ref, r_ref, a_ref, w2_ref, ba_ref, gg_ref, oraw_ref, st_ref, dog_ref,
             dqkvr_ref, da_ref, dw2_ref, dba_ref, dgg_ref, ds_ref):
        @pl.when(pl.program_id(0) == 0)
        def _():
            ds_ref[...] = jnp.zeros_like(ds_ref)
            dw2_ref[...] = jnp.zeros_like(dw2_ref)
            dba_ref[...] = jnp.zeros_like(dba_ref)
            dgg_ref[...] = jnp.zeros_like(dgg_ref)

        qk = qk_ref[...]
        gp, b, causal = _gla_chunk_terms(qk, a_ref, w2_ref, ba_ref, DK)
        row = lax.broadcasted_iota(jnp.int32, (C, C), 0)
        col = lax.broadcasted_iota(jnp.int32, (C, C), 1)
        upper = (col >= row).astype(F32)
        dla_parts = []
        for h in range(H):
            ks = slice(h * HK, (h + 1) * HK)
            vs = slice(h * HV, (h + 1) * HV)
            bh = b[:, ks]
            b_last = bh[C - 1:C, :]
            eb = jnp.exp(bh)
            emb = jnp.exp(-bh)
            ehat = jnp.exp(b_last - bh)
            e_last = jnp.exp(b_last)
            qt = qk[:, ks] * scale * eb
            kh = qk[:, DK + h * HK:DK + (h + 1) * HK]
            kt = kh * emb
            khat = kh * ehat
            a_mat = jnp.where(causal, _dot(qt, kt, NT, ONE_PASS), 0.0)
            vh = v_ref[:, vs]
            o = oraw_ref[:, vs]
            rs = lax.rsqrt(jnp.mean(o * o, axis=-1, keepdims=True) + EPS)
            on = o * rs
            gg = gg_ref[:, vs]
            rr = r_ref[:, vs]
            sg = _sigmoid(rr)
            d_out = dog_ref[:, vs]
            dr = d_out * (on * gg) * (sg * (1.0 + rr * (1.0 - sg)))
            d_og = d_out * (rr * sg)
            dgg_ref[:, vs] += jnp.sum(d_og * on, axis=0, keepdims=True)
            d_on = d_og * gg
            d_o = rs * (d_on - on * jnp.mean(d_on * on, axis=-1, keepdims=True))
            s_t = st_ref[h]
            ds_t = ds_ref[h]
            d_a = jnp.where(causal, _dot(d_o, vh, NT, ONE_PASS), 0.0)
            dv = _dot(a_mat, d_o, TN, ONE_PASS) + _dot(khat, ds_t, NT, ONE_PASS)
            dqt = _dot(d_a, kt, NN, ONE_PASS) + _dot(d_o, s_t, NN, ONE_PASS)
            dkt = _dot(d_a, qt, TN, ONE_PASS)
            dkhat = _dot(vh, ds_t, NN, ONE_PASS)
            ds_ref[h] = ds_t * e_last + _dot(d_o, qt, TN, ONE_PASS)
            dq = dqt * eb * scale
            dk = dkt * emb + dkhat * ehat
            db = dqt * qt - dkt * kt - dkhat * khat
            d_last = (jnp.sum(dkhat * khat, axis=0, keepdims=True)
                      + e_last * jnp.sum(ds_t * s_t, axis=0, keepdims=True))
            dla_parts.append(_dot(upper, db, NN, HIGHEST) + d_last)
            dqkvr_ref[:, ks] = dq.astype(BF16)
            dqkvr_ref[:, DK + h * HK:DK + (h + 1) * HK] = dk.astype(BF16)
            dqkvr_ref[:, DV + h * HV:DV + (h + 1) * HV] = dv.astype(BF16)
            dqkvr_ref[:, 2 * DV + h * HV:2 * DV + (h + 1) * HV] = dr.astype(BF16)
        dla = jnp.concatenate(dla_parts, axis=1)
        dgp = dla * (1.0 / GLA_GATE_NORM) * _sigmoid(-gp)
        dba_ref[...] += jnp.sum(dgp, axis=0, keepdims=True)
        dgp_b = dgp.astype(BF16)
        dw2_ref[...] += _dot(a_ref[...].astype(BF16), dgp_b, TN)
        da_ref[...] = _dot(dgp_b, w2_ref[...], NT).astype(BF16)

    rev = lambda j: pl.BlockSpec((C, DV), lambda i: (n - 1 - i, j))
    full = lambda s: pl.BlockSpec(s, lambda i: (0,) * len(s))
    return _pcall(
        body, name="gla_bwd",
        out_shape=(jax.ShapeDtypeStruct((T, 3 * DV), BF16), jax.ShapeDtypeStruct((T, RP), BF16),
                   jax.ShapeDtypeStruct((RP, DK), F32), jax.ShapeDtypeStruct((1, DK), F32),
                   jax.ShapeDtypeStruct((1, DV), F32)),
        grid=(n,),
        in_specs=[rev(0), rev(1), rev(2), pl.BlockSpec((C, RP), lambda i: (n - 1 - i, 0)),
                  full((RP, DK)), full((1, DK)), full((1, DV)), rev(0),
                  pl.BlockSpec((None, H, HV, HK), lambda i: (n - 1 - i, 0, 0, 0)), rev(0)],
        out_specs=(pl.BlockSpec((C, 3 * DV), lambda i: (n - 1 - i, 0)),
                   pl.BlockSpec((C, RP), lambda i: (n - 1 - i, 0)),
                   full((RP, DK)), full((1, DK)), full((1, DV))),
        scratch_shapes=[pltpu.VMEM((H, HV, HK), F32)],
        semantics=("arbitrary",))(proj, proj, proj, a_pad, w2, b_a, g_gla, o_raw, states, do_gla)


def _pool_windows(p, g, T):
    t = lax.broadcasted_iota(jnp.int32, (T, 1), 0)
    s = p
    for lvl in range(POOL_GROUPS):
        sh = 1 << lvl
        nxt = s + jnp.where(t >= sh, pltpu.roll(s, sh, 0), 0.0)
        s = jnp.where(lvl <= g, nxt, s)
    win = jnp.left_shift(2, g)
    inv = 1.0 / jnp.minimum(t + 1, win).astype(F32)
    return s * inv - p, inv


def _pool_fwd(proj, w_pool, scale, *, T, PW, col_block):
    GW = PW // POOL_GROUPS
    per = PW // GW

    def body(p_ref, w_ref, s_ref, o_ref):
        g = pl.program_id(0)
        pooled, _ = _pool_windows(p_ref[...], g, T)
        mixed = _dot(pooled.astype(BF16), w_ref[...])
        o_ref[...] = (mixed * s_ref[...]).astype(BF16)

    return _pcall(body, name="pool_fwd", out_shape=jax.ShapeDtypeStruct((T, PW), BF16),
                  grid=(POOL_GROUPS,),
                  in_specs=[pl.BlockSpec((T, GW), lambda g: (0, col_block * per + g)),
                            pl.BlockSpec((None, GW, GW), lambda g: (g, 0, 0)),
                            pl.BlockSpec((1, GW), lambda g: (0, g))],
                  out_specs=pl.BlockSpec((T, GW), lambda g: (0, g)),
                  semantics=("parallel",))(proj, w_pool, scale)


def _pool_bwd(proj, w_pool, scale, do_pool, *, T, PW, col_block):
    GW = PW // POOL_GROUPS
    per = PW // GW

    def body(p_ref, w_ref, s_ref, do_ref, dp_ref, dw_ref, dsc_ref):
        g = pl.program_id(0)
        pooled, inv = _pool_windows(p_ref[...], g, T)
        pooled_b = pooled.astype(BF16)
        w = w_ref[...]
        mixed = _dot(pooled_b, w)
        d_out = do_ref[...]
        dsc_ref[...] = jnp.sum(d_out * mixed, axis=0, keepdims=True)
        dmixed = (d_out * s_ref[...]).astype(BF16)
        dw_ref[...] = _dot(pooled_b, dmixed, TN)
        dpooled = _dot(dmixed, w, NT)
        t = lax.broadcasted_iota(jnp.int32, (T, 1), 0)
        s = dpooled * inv
        for lvl in range(POOL_GROUPS):
            sh = 1 << lvl
            nxt = s + jnp.where(t < T - sh, pltpu.roll(s, T - sh, 0), 0.0)
            s = jnp.where(lvl <= g, nxt, s)
        dp_ref[...] = (s - dpooled).astype(BF16)

    return _pcall(body, name="pool_bwd",
                  out_shape=(jax.ShapeDtypeStruct((T, PW), BF16),
                             jax.ShapeDtypeStruct((POOL_GROUPS, GW, GW), F32),
                             jax.ShapeDtypeStruct((1, PW), F32)),
                  grid=(POOL_GROUPS,),
                  in_specs=[pl.BlockSpec((T, GW), lambda g: (0, col_block * per + g)),
                            pl.BlockSpec((None, GW, GW), lambda g: (g, 0, 0)),
                            pl.BlockSpec((1, GW), lambda g: (0, g)),
                            pl.BlockSpec((T, GW), lambda g: (0, g))],
                  out_specs=(pl.BlockSpec((T, GW), lambda g: (0, g)),
                             pl.BlockSpec((None, GW, GW), lambda g: (g, 0, 0)),
                             pl.BlockSpec((1, GW), lambda g: (0, g))),
                  semantics=("parallel",))(proj, w_pool, scale, do_pool)


def _merge_fwd(y_gla, y_pool, proj, *, T, D, col_block):
    tr = _tile(T, 128, 16)

    def body(yg_ref, yp_ref, g1_ref, g2_ref, o_ref):
        o_ref[...] = (_sigmoid(g1_ref[...]) * yg_ref[...]
                      + _sigmoid(g2_ref[...]) * yp_ref[...]).astype(BF16)

    row = pl.BlockSpec((tr, D), lambda i: (i, 0))
    return _pcall(body, name="merge_fwd", out_shape=jax.ShapeDtypeStruct((T, D), BF16),
                  grid=(T // tr,),
                  in_specs=[row, row, pl.BlockSpec((tr, D), lambda i: (i, col_block)),
                            pl.BlockSpec((tr, D), lambda i: (i, col_block + 1))],
                  out_specs=row, semantics=("parallel",))(y_gla, y_pool, proj, proj)


def _merge_bwd(dmerged, y_gla, y_pool, proj, *, T, D, col_block):
    tr = _tile(T, 128, 16)

    def body(dm_ref, yg_ref, yp_ref, g1_ref, g2_ref, dyg_ref, dyp_ref, dg_ref):
        dm = dm_ref[...]
        s1 = _sigmoid(g1_ref[...])
        s2 = _sigmoid(g2_ref[...])
        dyg_ref[...] = (dm * s1).astype(BF16)
        dyp_ref[...] = (dm * s2).astype(BF16)
        dg_ref[:, :D] = (dm * yg_ref[...] * s1 * (1.0 - s1)).astype(BF16)
        dg_ref[:, D:] = (dm * yp_ref[...] * s2 * (1.0 - s2)).astype(BF16)

    row = pl.BlockSpec((tr, D), lambda i: (i, 0))
    return _pcall(body, name="merge_bwd",
                  out_shape=(jax.ShapeDtypeStruct((T, D), BF16), jax.ShapeDtypeStruct((T, D), BF16),
                             jax.ShapeDtypeStruct((T, 2 * D), BF16)),
                  grid=(T // tr,),
                  in_specs=[row, row, row, pl.BlockSpec((tr, D), lambda i: (i, col_block)),
                            pl.BlockSpec((tr, D), lambda i: (i, col_block + 1))],
                  out_specs=(row, row, pl.BlockSpec((tr, 2 * D), lambda i: (i, 0))),
                  semantics=("parallel",))(dmerged, y_gla, y_pool, proj, proj)


def _attn_fwd(q, kv, *, T, D, M):
    H = CROSS_HEADS
    HD = D // H
    tq = _tile(T, 512, 16)
    scale = HD ** -0.5

    def body(q_ref, kv_ref, o_ref):
        for h in range(H):
            hs = slice(h * HD, (h + 1) * HD)
            s = _dot(q_ref[:, hs], kv_ref[:, hs], NT) * scale
            e = jnp.exp(s - jnp.max(s, axis=-1, keepdims=True))
            p = e / jnp.sum(e, axis=-1, keepdims=True)
            o_ref[:, hs] = _dot(p.astype(BF16), kv_ref[:, D + h * HD:D + (h + 1) * HD]).astype(BF16)

    row = pl.BlockSpec((tq, D), lambda i: (i, 0))
    return _pcall(body, name="attn_fwd", out_shape=jax.ShapeDtypeStruct((T, D), BF16),
                  grid=(T // tq,), in_specs=[row, pl.BlockSpec((M, 2 * D), lambda i: (0, 0))],
                  out_specs=row, semantics=("parallel",))(q, kv)


def _attn_bwd(q, kv, do, *, T, D, M):
    H = CROSS_HEADS
    HD = D // H
    tq = _tile(T, 512, 16)
    scale = HD ** -0.5

    def body(q_ref, kv_ref, do_ref, dq_ref, dkv_ref):
        @pl.when(pl.program_id(0) == 0)
        def _():
            dkv_ref[...] = jnp.zeros_like(dkv_ref)

        for h in range(H):
            hs = slice(h * HD, (h + 1) * HD)
            vs = slice(D + h * HD, D + (h + 1) * HD)
            qh = q_ref[:, hs]
            kh = kv_ref[:, hs]
            s = _dot(qh, kh, NT) * scale
            e = jnp.exp(s - jnp.max(s, axis=-1, keepdims=True))
            p = e / jnp.sum(e, axis=-1, keepdims=True)
            p_b = p.astype(BF16)
            d_o = do_ref[:, hs]
            dkv_ref[:, vs] += _dot(p_b, d_o, TN)
            dp = _dot(d_o, kv_ref[:, vs], NT)
            ds = (p * (dp - jnp.sum(dp * p, axis=-1, keepdims=True)) * scale).astype(BF16)
            dq_ref[:, hs] = _dot(ds, kh).astype(BF16)
            dkv_ref[:, hs] += _dot(ds, qh, TN)

    row = pl.BlockSpec((tq, D), lambda i: (i, 0))
    full = pl.BlockSpec((M, 2 * D), lambda i: (0, 0))
    return _pcall(body, name="attn_bwd",
                  out_shape=(jax.ShapeDtypeStruct((T, D), BF16), jax.ShapeDtypeStruct((M, 2 * D), F32)),
                  grid=(T // tq,), in_specs=[row, full, row], out_specs=(row, full),
                  semantics=("arbitrary",))(q, kv, do)


def _shift_down(x, halo, s):
    out = pltpu.roll(x, s, 0)
    t8 = lax.broadcasted_iota(jnp.int32, (SUBLANES, 1), 0)
    head = out[:SUBLANES]
    for j in range(s):
        head = jnp.where(t8 == j, halo[SUBLANES - s + j:SUBLANES - s + j + 1, :], head)
    return head if x.shape[0] == SUBLANES else jnp.concatenate([head, out[SUBLANES:]], axis=0)


def _shift_up(x, halo, s):
    rows = x.shape[0]
    out = pltpu.roll(x, rows - s, 0)
    t8 = lax.broadcasted_iota(jnp.int32, (SUBLANES, 1), 0)
    tail = out[rows - SUBLANES:]
    for j in range(s):
        tail = jnp.where(t8 == SUBLANES - s + j, halo[j:j + 1, :], tail)
    return jnp.concatenate([out[:rows - SUBLANES], tail], axis=0)


def _conv_tiles(T):
    tt = _tile(T, 128, SUBLANES)
    return tt, tt // SUBLANES, T // SUBLANES


def _conv_fwd(u0, conv_w, conv_b, *, T, F):
    tt, hb, _ = _conv_tiles(T)
    cw = _tile(F, LANES)

    def body(u_ref, prev_ref, w_ref, b_ref, f_ref):
        i = pl.program_id(0)

        def conv(cs):
            x = u_ref[:, cs]
            halo = jnp.where(i > 0, prev_ref[:, cs], 0.0)
            return (w_ref[2:3, cs] * x + w_ref[1:2, cs] * _shift_down(x, halo, 1)
                    + w_ref[0:1, cs] * _shift_down(x, halo, 2) + b_ref[:, cs])

        for j in range(F // cw):
            gate = conv(slice(j * cw, (j + 1) * cw))
            val = conv(slice(F + j * cw, F + (j + 1) * cw))
            f_ref[:, j * cw:(j + 1) * cw] = (gate * _sigmoid(gate) * val).astype(BF16)

    return _pcall(body, name="conv_fwd", out_shape=jax.ShapeDtypeStruct((T, F), BF16),
                  grid=(T // tt,),
                  in_specs=[pl.BlockSpec((tt, 2 * F), lambda i: (i, 0)),
                            pl.BlockSpec((SUBLANES, 2 * F), lambda i: (jnp.maximum(i * hb - 1, 0), 0)),
                            pl.BlockSpec((CONV_W, 2 * F), lambda i: (0, 0)),
                            pl.BlockSpec((1, 2 * F), lambda i: (0, 0))],
                  out_specs=pl.BlockSpec((tt, F), lambda i: (i, 0)),
                  semantics=("parallel",))(u0, u0, conv_w, conv_b)


def _conv_bwd(u0, conv_w, conv_b, df, *, T, F):
    tt, hb, nb = _conv_tiles(T)
    nt = T // tt
    cw = _tile(F, LANES)

    def body(u_ref, prev_ref, next_ref, df_ref, dfn_ref, w_ref, b_ref, du0_ref, dw_ref, db_ref):
        i = pl.program_id(0)

        @pl.when(i == 0)
        def _():
            dw_ref[...] = jnp.zeros_like(dw_ref)
            db_ref[...] = jnp.zeros_like(db_ref)

        def conv(cs):
            x = u_ref[:, cs]
            halo = jnp.where(i > 0, prev_ref[:, cs], 0.0)
            x1 = _shift_down(x, halo, 1)
            x2 = _shift_down(x, halo, 2)
            u = w_ref[2:3, cs] * x + w_ref[1:2, cs] * x1 + w_ref[0:1, cs] * x2 + b_ref[:, cs]
            xn = next_ref[:, cs]
            tail = x[tt - SUBLANES:, :]
            un = (w_ref[2:3, cs] * xn + w_ref[1:2, cs] * _shift_down(xn, tail, 1)
                  + w_ref[0:1, cs] * _shift_down(xn, tail, 2) + b_ref[:, cs])
            return u, un, (x, x1, x2)

        def glu_grad(gate, val, dff):
            sg = _sigmoid(gate)
            return dff * val * (sg * (1.0 + gate * (1.0 - sg))), dff * (gate * sg)

        def finish(cs, du, dun, xs):
            du0 = (w_ref[2:3, cs] * du + w_ref[1:2, cs] * _shift_up(du, dun, 1)
                   + w_ref[0:1, cs] * _shift_up(du, dun, 2))
            du0_ref[:, cs] = du0.astype(BF16)
            db_ref[:, cs] += jnp.sum(du, axis=0, keepdims=True)
            dw_ref[2:3, cs] += jnp.sum(du * xs[0], axis=0, keepdims=True)
            dw_ref[1:2, cs] += jnp.sum(du * xs[1], axis=0, keepdims=True)
            dw_ref[0:1, cs] += jnp.sum(du * xs[2], axis=0, keepdims=True)

        for j in range(F // cw):
            fs = slice(j * cw, (j + 1) * cw)
            gs, vs = fs, slice(F + j * cw, F + (j + 1) * cw)
            ug, ung, xg = conv(gs)
            uv, unv, xv = conv(vs)
            dug, duv = glu_grad(ug, uv, df_ref[:, fs].astype(F32))
            dung, dunv = glu_grad(ung, unv, dfn_ref[0:SUBLANES, fs].astype(F32))
            dung = jnp.where(i < nt - 1, dung, 0.0)
            dunv = jnp.where(i < nt - 1, dunv, 0.0)
            finish(gs, dug, dung, xg)
            finish(vs, duv, dunv, xv)

    wide = lambda rows, fn: pl.BlockSpec((rows, 2 * F), fn)
    nxt = lambda i: (jnp.minimum((i + 1) * hb, nb - 1), 0)
    return _pcall(body, name="conv_bwd",
                  out_shape=(jax.ShapeDtypeStruct((T, 2 * F), BF16),
                             jax.ShapeDtypeStruct((CONV_W, 2 * F), F32),
                             jax.ShapeDtypeStruct((1, 2 * F), F32)),
                  grid=(nt,),
                  in_specs=[wide(tt, lambda i: (i, 0)),
                            wide(SUBLANES, lambda i: (jnp.maximum(i * hb - 1, 0), 0)),
                            wide(SUBLANES, nxt),
                            pl.BlockSpec((tt, F), lambda i: (i, 0)),
                            pl.BlockSpec((2 * SUBLANES, F),
                                         lambda i: (jnp.minimum((i + 1) * (hb // 2), nb // 2 - 1), 0)),
                            wide(CONV_W, lambda i: (0, 0)), wide(1, lambda i: (0, 0))],
                  out_specs=(wide(tt, lambda i: (i, 0)), wide(CONV_W, lambda i: (0, 0)),
                             wide(1, lambda i: (0, 0))),
                  semantics=("arbitrary",))(u0, u0, u0, df, df, conv_w, conv_b)


def _adamw(w, g, m, v, *, name):
    R, C = w.shape
    tr = _tile(R, max(SUBLANES, (1 << 19) // max(C, 1) // SUBLANES * SUBLANES), SUBLANES)
    c1 = 1.0 / (1.0 - ADAM_B1 ** ADAM_STEP)
    c2 = 1.0 / (1.0 - ADAM_B2 ** ADAM_STEP)

    def body(w_ref, g_ref, m_ref, v_ref, d_ref, mo_ref, vo_ref):
        gv = g_ref[...]
        mn = ADAM_B1 * m_ref[...] + (1.0 - ADAM_B1) * gv
        vn = ADAM_B2 * v_ref[...] + (1.0 - ADAM_B2) * (gv * gv)
        d_ref[...] = -ADAM_LR * ((mn * c1) / (jnp.sqrt(vn * c2) + ADAM_EPS) + ADAM_WD * w_ref[...])
        mo_ref[...] = mn
        vo_ref[...] = vn

    blk = pl.BlockSpec((tr, C), lambda i: (i, 0))
    shp = jax.ShapeDtypeStruct((R, C), F32)
    return _pcall(body, name=name, out_shape=(shp, shp, shp), grid=(R // tr,),
                  in_specs=[blk] * 4, out_specs=(blk,) * 3, semantics=("parallel",))(w, g, m, v)


def _blk(h, C, elems=1 << 19, align=16):
    th = _tile(h, max(align, elems // C // align * align), align)
    if th < h or h * C <= 2 * elems:
        return th, C
    return h, _tile(C, max(LANES, elems // h // LANES * LANES))


def _adamw_halves(w, m, v, g_mine, g_other, c_idx, *, name):
    _, h, C = w.shape
    th, tc = _blk(h, C, align=SUBLANES)
    c1 = 1.0 / (1.0 - ADAM_B1 ** ADAM_STEP)
    c2 = 1.0 / (1.0 - ADAM_B2 ** ADAM_STEP)

    def body(c_ref, w_ref, m_ref, v_ref, gm_ref, go_ref, g_ref, d_ref, mo_ref, vo_ref):
        gv = jnp.where(pl.program_id(0) == c_ref[0], gm_ref[...], go_ref[...])
        mn = ADAM_B1 * m_ref[...] + (1.0 - ADAM_B1) * gv
        vn = ADAM_B2 * v_ref[...] + (1.0 - ADAM_B2) * (gv * gv)
        d_ref[...] = -ADAM_LR * ((mn * c1) / (jnp.sqrt(vn * c2) + ADAM_EPS) + ADAM_WD * w_ref[...])
        g_ref[...] = gv
        mo_ref[...] = mn
        vo_ref[...] = vn

    blk = pl.BlockSpec((None, th, tc), lambda s, i, j, c: (s, i, j))

    def pick(mine):
        def index(s, i, j, c):
            use = (s == c[0]) if mine else (s != c[0])
            return jnp.where(use, i, 0), jnp.where(use, j, 0)
        return pl.BlockSpec((th, tc), index)

    shp = jax.ShapeDtypeStruct((2, h, C), F32)
    return _pcall(body, name=name, out_shape=(shp,) * 4, grid=(2, h // th, C // tc), prefetch=1,
                  in_specs=[blk, blk, blk, pick(True), pick(False)], out_specs=(blk,) * 4,
                  semantics=("parallel", "parallel", "parallel"))(c_idx, w, m, v, g_mine, g_other)


def _mesh_pos():
    x, y, c = lax.axis_index("x"), lax.axis_index("y"), lax.axis_index("c")
    others = [(1 - x, y), (x, 1 - y), (1 - x, 1 - y)]
    return x, y, c, others


def _gather_copies(shards, lands, send_sems, recv_sems):
    x, y, c, others = _mesh_pos()
    me = 2 * x + y
    return [pltpu.make_async_remote_copy(
        src_ref=shards[a].at[c], dst_ref=lands[a].at[me, c],
        send_sem=send_sems.at[3 * a + j], recv_sem=recv_sems.at[3 * a + j],
        device_id=(*chip, c), device_id_type=MESH)
        for a in range(len(shards)) for j, chip in enumerate(others)]


def _near_copies(shards, lands, send_sems, recv_sems):
    x, y, c, others = _mesh_pos()
    me = 2 * x + y
    return [pltpu.make_async_remote_copy(
        src_ref=shards[a].at[c], dst_ref=lands[a].at[me, c],
        send_sem=send_sems.at[2 * a + j], recv_sem=recv_sems.at[2 * a + j],
        device_id=(*chip, c), device_id_type=MESH)
        for a in range(len(shards)) for j, chip in enumerate(others[:2])]


def _relay_copies(shards, zones, send_sems, recv_sems):
    x, y, c, others = _mesh_pos()
    (nx, ny), copies = others[:2], []
    for a in range(len(zones)):
        hc = zones[a].shape[-1] // 2
        for k, (src_chip, to, lo) in enumerate(((ny, nx, 0), (nx, ny, hc))):
            part = zones[a].at[2 * src_chip[0] + src_chip[1], c, :, pl.ds(lo, hc)]
            copies.append(pltpu.make_async_remote_copy(
                src_ref=part, dst_ref=part, send_sem=send_sems.at[2 * a + k], recv_sem=recv_sems.at[2 * a + k],
                device_id=(*to, c), device_id_type=MESH))
    return copies


def _pass_copies(shards, zones, send_sems, recv_sems, pieces=(0, 1, 2, 3)):
    x, y, c, others = _mesh_pos()
    me = 2 * x + y
    copies = []
    for a in range(len(shards)):
        srcs = [zones[a].at[2 * chip[0] + chip[1], c] for chip in others] + [shards[a]]
        dsts = [zones[a].at[2 * chip[0] + chip[1], c] for chip in others] + [zones[a].at[me]]
        copies += [pltpu.make_async_remote_copy(
            src_ref=srcs[p], dst_ref=dsts[p], send_sem=send_sems.at[len(pieces) * a + k],
            recv_sem=recv_sems.at[len(pieces) * a + k], device_id=(x, y, 1 - c), device_id_type=MESH)
            for k, p in enumerate(pieces)]
    return copies


def _exchange_copies(grads, recvs, send_sems, recv_sems):
    x, y, c, _ = _mesh_pos()
    return [pltpu.make_async_remote_copy(
        src_ref=grads[a].at[:, 1 - c], dst_ref=recvs[a], send_sem=send_sems.at[a],
        recv_sem=recv_sems.at[a], device_id=(x, y, 1 - c), device_id_type=MESH) for a in range(len(grads))]


def _split_start(copies, per, srcs, zones, after, *, name):
    n = len(srcs)
    HBM = pl.BlockSpec(memory_space=pltpu.HBM)
    SEM = pl.BlockSpec(memory_space=pltpu.SEMAPHORE)

    def body(*refs):
        send_sems, recv_sems = refs[2 * n + 1], refs[2 * n + 2]
        for cp in copies(refs[:n], refs[n:2 * n], send_sems, recv_sems):
            cp.start()
        refs[-1][...] = jnp.zeros_like(refs[-1])

    hbm = lambda a: pltpu.HBM(a.shape, a.dtype)
    res = _pcall(body, name=name,
                 out_shape=(pltpu.SemaphoreType.DMA((per * n,)), pltpu.SemaphoreType.DMA((per * n,)),
                            *[hbm(a) for a in srcs], *[hbm(a) for a in zones],
                            jax.ShapeDtypeStruct((SUBLANES, LANES), F32)),
                 in_specs=[*[HBM] * (2 * n), pl.BlockSpec(memory_space=pl.ANY)],
                 out_specs=(SEM, SEM, *[HBM] * (2 * n), pl.BlockSpec(memory_space=pltpu.VMEM)),
                 aliases={i: 2 + i for i in range(2 * n)}, split_copy=True)(
        *[pltpu.with_memory_space_constraint(a, pltpu.HBM) for a in [*srcs, *zones]], after)
    return res[0], res[1], list(res[2:2 + n]), list(res[2 + n:2 + 2 * n]), res[-1]


def _split_wait(copies, send_sems, recv_sems, srcs, zones, after, *, name):
    n = len(srcs)
    afters = list(after) if isinstance(after, (list, tuple)) else [after]
    HBM = pl.BlockSpec(memory_space=pltpu.HBM)
    SEM = pl.BlockSpec(memory_space=pltpu.SEMAPHORE)

    def body(*refs):
        for cp in copies(refs[:n], refs[n:2 * n], refs[2 * n], refs[2 * n + 1]):
            cp.wait_send()
            cp.wait_recv()

    hbm = lambda a: pltpu.HBM(a.shape, a.dtype)
    res = _pcall(body, name=name, out_shape=(*[hbm(a) for a in srcs], *[hbm(a) for a in zones]),
                 in_specs=[*[HBM] * (2 * n), SEM, SEM, *[pl.BlockSpec(memory_space=pl.ANY)] * len(afters)],
                 out_specs=tuple([HBM] * (2 * n)), aliases={i: i for i in range(2 * n)},
                 split_copy=True)(*srcs, *zones, send_sems, recv_sems, *afters)
    return list(res[:n]), list(res[n:])


def _add_halves(grad, recv, c_idx, *, name):
    S, _, h, C = grad.shape
    th, tc = _blk(h, C)

    def body(c_ref, g_ref, r_ref, o_ref):
        o_ref[...] = (g_ref[...].astype(F32) + r_ref[...].astype(F32)).astype(o_ref.dtype)

    return _pcall(body, name=name, out_shape=jax.ShapeDtypeStruct((S, h, C), grad.dtype),
                  grid=(S, h // th, C // tc), prefetch=1,
                  in_specs=[pl.BlockSpec((None, None, th, tc), lambda s, i, j, c: (s, c[0], i, j)),
                            pl.BlockSpec((None, th, tc), lambda s, i, j, c: (s, i, j))],
                  out_specs=pl.BlockSpec((None, th, tc), lambda s, i, j, c: (s, i, j)),
                  semantics=("parallel", "parallel", "parallel"))(c_idx, grad, recv)


def _scatter_copies(srcs, lands, send_sems, recv_sems):
    x, y, c, others = _mesh_pos()
    return [pltpu.make_async_remote_copy(
        src_ref=srcs[a].at[2 * chip[0] + chip[1]], dst_ref=lands[a].at[j],
        send_sem=send_sems.at[3 * a + j], recv_sem=recv_sems.at[3 * a + j],
        device_id=(*chip, c), device_id_type=MESH)
        for a in range(len(srcs)) for j, chip in enumerate(others)]


def _add_chips(sums, recv, chip_idx, *, name):
    _, h, C = sums.shape
    th, tc = _blk(h, C)

    def body(k_ref, s_ref, r_ref, o_ref):
        acc = s_ref[...].astype(F32) + r_ref[0].astype(F32)
        acc = acc + r_ref[1].astype(F32)
        o_ref[...] = acc + r_ref[2].astype(F32)

    return _pcall(body, name=name, out_shape=jax.ShapeDtypeStruct((h, C), F32),
                  grid=(h // th, C // tc), prefetch=1,
                  in_specs=[pl.BlockSpec((None, th, tc), lambda i, j, k: (k[0], i, j)),
                            pl.BlockSpec((3, th, tc), lambda i, j, k: (0, i, j))],
                  out_specs=pl.BlockSpec((th, tc), lambda i, j, k: (i, j)),
                  semantics=("parallel", "parallel"))(chip_idx, sums, recv)


def _swap_copies(halves, others, send_sems, recv_sems):
    x, y, c, _ = _mesh_pos()
    return [pltpu.make_async_remote_copy(
        src_ref=halves[a], dst_ref=others[a], send_sem=send_sems.at[a], recv_sem=recv_sems.at[a],
        device_id=(x, y, 1 - c), device_id_type=MESH) for a in range(len(halves))]


def _all_reduce_small(buf):
    R, L = buf.shape
    NDEV = 8

    def body(x_ref, sum_ref, all_ref, send_sems, recv_sems, local_sem):
        x, y, c, others = _mesh_pos()
        me, sibling = (x, y, c), (x, y, 1 - c)

        def slot(px, py, pc):
            return all_ref.at[4 * px + 2 * py + pc]

        def copy(k, block, to, src=None):
            return pltpu.make_async_remote_copy(
                src_ref=slot(*block) if src is None else src, dst_ref=slot(*block),
                send_sem=send_sems.at[k], recv_sem=recv_sems.at[k], device_id=to, device_id_type=MESH)

        mine = pltpu.make_async_copy(x_ref, slot(*me), local_sem)
        mine.start()
        first = [copy(0, me, sibling, src=x_ref)]
        first += [copy(1 + j, me, (*chip, c), src=x_ref) for j, chip in enumerate(others)]
        for cp in first:
            cp.start()
        passed = [copy(4 + j, (*chip, c), sibling) for j, chip in enumerate(others)]
        for j, chip in enumerate(others):
            copy(1 + j, (*chip, c), me).wait_recv()
            passed[j].start()
        copy(0, sibling, me).wait_recv()
        for j, chip in enumerate(others):
            copy(4 + j, (*chip, 1 - c), me).wait_recv()
        for cp in first + passed:
            cp.wait_send()
        mine.wait()
        acc = all_ref[0]
        for d in range(1, NDEV):
            acc = acc + all_ref[d]
        sum_ref[...] = acc

    VM = pl.BlockSpec(memory_space=pltpu.VMEM)
    return _pcall(body, name="all_reduce_small",
                  out_shape=(jax.ShapeDtypeStruct((R, L), F32), jax.ShapeDtypeStruct((NDEV, R, L), F32)),
                  in_specs=[VM], out_specs=(VM, VM),
                  scratch_shapes=[pltpu.SemaphoreType.DMA((7,)), pltpu.SemaphoreType.DMA((7,)),
                                  pltpu.SemaphoreType.DMA])(buf)[0]


def _pack(arrs, rows_multiple=16):
    flat = [a.reshape(-1).astype(F32) for a in arrs]
    sizes = [f.shape[0] for f in flat]
    total = sum(sizes)
    per = LANES * rows_multiple
    padded = -(-total // per) * per
    flat.append(jnp.zeros((padded - total,), F32))
    offs = [0]
    for s in sizes:
        offs.append(offs[-1] + s)
    return jnp.concatenate(flat).reshape(padded // LANES, LANES), offs


def _unpack(buf, offs, shapes):
    flat = buf.reshape(-1)
    return [flat[offs[i]:offs[i + 1]].reshape(s) for i, s in enumerate(shapes)]


def kernel(x, mem, g_mix, w_in, w_a2, b_a, g_gla, w_pool, pool_scale, w_branch, w_out, g_cross, g_mem, w_cq, w_ckv, w_co, g_ffn, w_up, conv_w, conv_b, w_down, g_final, loss_target, m_g_mix, m_w_in, m_w_a2, m_b_a, m_g_gla, m_w_pool, m_pool_scale, m_w_branch, m_w_out, m_g_cross, m_g_mem, m_w_cq, m_w_ckv, m_w_co, m_g_ffn, m_w_up, m_conv_w, m_conv_b, m_w_down, m_g_final, v_g_mix, v_w_in, v_w_a2, v_b_a, v_g_gla, v_w_pool, v_pool_scale, v_w_branch, v_w_out, v_g_cross, v_g_mem, v_w_cq, v_w_ckv, v_w_co, v_g_ffn, v_w_up, v_conv_w, v_conv_b, v_w_down, v_g_final):
    weights = dict(g_mix=g_mix, w_in=w_in, w_a2=w_a2, b_a=b_a, g_gla=g_gla, w_pool=w_pool,
                   pool_scale=pool_scale, w_branch=w_branch, w_out=w_out, g_cross=g_cross, g_mem=g_mem,
                   w_cq=w_cq, w_ckv=w_ckv, w_co=w_co, g_ffn=g_ffn, w_up=w_up, conv_w=conv_w,
                   conv_b=conv_b, w_down=w_down, g_final=g_final)
    mom_m = dict(g_mix=m_g_mix, w_in=m_w_in, w_a2=m_w_a2, b_a=m_b_a, g_gla=m_g_gla, w_pool=m_w_pool,
                 pool_scale=m_pool_scale, w_branch=m_w_branch, w_out=m_w_out, g_cross=m_g_cross,
                 g_mem=m_g_mem, w_cq=m_w_cq, w_ckv=m_w_ckv, w_co=m_w_co, g_ffn=m_g_ffn, w_up=m_w_up,
                 conv_w=m_conv_w, conv_b=m_conv_b, w_down=m_w_down, g_final=m_g_final)
    mom_v = dict(g_mix=v_g_mix, w_in=v_w_in, w_a2=v_w_a2, b_a=v_b_a, g_gla=v_g_gla, w_pool=v_w_pool,
                 pool_scale=v_pool_scale, w_branch=v_w_branch, w_out=v_w_out, g_cross=v_g_cross,
                 g_mem=v_g_mem, w_cq=v_w_cq, w_ckv=v_w_ckv, w_co=v_w_co, g_ffn=v_g_ffn, w_up=v_w_up,
                 conv_w=v_conv_w, conv_b=v_conv_b, w_down=v_w_down, g_final=v_g_final)
    order = list(weights)
    big = ["w_in", "w_branch", "w_out", "w_cq", "w_ckv", "w_co", "w_up", "w_down"]
    small_sharded = ["w_a2", "w_pool", "conv_w"]
    small_repl = ["g_mix", "b_a", "g_gla", "pool_scale", "g_cross", "g_mem", "g_ffn", "conv_b", "g_final"]

    xs, ms, tgt = x[0], mem[0], loss_target[0]
    T, D = xs.shape
    M = ms.shape[0]
    DK, DV, PW = b_a.shape[1], g_gla.shape[1], pool_scale.shape[1]
    RANK = w_a2.shape[1]
    F2 = conv_b.shape[1]
    F = F2 // 2
    DIN = N_CHIPS * w_in.shape[2]
    OFF_A = 2 * DK + 2 * DV
    OFF_P = OFF_A + RANK
    RP = LANES
    GW = PW // POOL_GROUPS
    assert PW == DV and 4 * DV == 2 * D and OFF_P + PW + 2 * D == DIN

    cx, cy, cc = lax.axis_index("x"), lax.axis_index("y"), lax.axis_index("c")
    chip = 2 * cx + cy
    c_idx = jnp.reshape(cc, (1,)).astype(jnp.int32)
    chip_idx = jnp.reshape(chip, (1,)).astype(jnp.int32)

    def halves(a):
        return a.reshape(2, a.shape[0] // 2, a.shape[1])

    shard2d = {k: (weights[k][0].T if k == "w_in" else weights[k][0]) for k in big}
    small_pack, small_offs = _pack([weights[k][0] for k in small_sharded], rows_multiple=32)
    flying, passing = {}, {}

    later = (("mix", ["w_branch", "w_out", "small"]), ("cross", ["w_cq", "w_ckv", "w_co"]),
             ("up", ["w_up"]), ("down", ["w_down"]))

    def bf16_halves(keys, tok):
        srcs = [small_pack if k == "small" else shard2d[k].astype(BF16) for k in keys]
        if tok is not None:
            srcs = [a + tok[0:1, 0:1].astype(a.dtype) for a in srcs]
        return [halves(a) for a in srcs]

    def gather_start(group, keys, srcs, tok):
        zones = [lax.empty((N_CHIPS, *s.shape), s.dtype) for s in srcs]
        first = (_near_copies, 2) if group == "in" else (_gather_copies, 3)
        s_sems, r_sems, srcs, zones, tok = _split_start(*first, srcs, zones, tok, name=f"gather_start_{group}")
        flying[group] = (keys, s_sems, r_sems, srcs, zones)
        return tok

    tok = gather_start("in", ["w_in"], bf16_halves(["w_in"], None), xs)
    ready = {group: bf16_halves(keys, tok) for group, keys in later}

    def arrive_in(after):
        keys, s_sems, r_sems, srcs, zones = flying["in"]
        near, diag = functools.partial(_pass_copies, pieces=(0, 1, 3)), functools.partial(_pass_copies, pieces=(2,))
        srcs, zones = _split_wait(_near_copies, s_sems, r_sems, srcs, zones,
                                  [after, *[a for group, _ in later for a in ready[group]]], name="gather_wait_in")
        rs, rr, srcs, zones, tok = _split_start(_relay_copies, 2, srcs, zones, after, name="gather_relay_start_in")
        ns, nr, srcs, zones, tok = _split_start(near, 3, srcs, zones, tok, name="gather_pass_near_start_in")
        for group, group_keys in later:
            tok = gather_start(group, group_keys, ready[group], tok)
        after = tok
        srcs, zones = _split_wait(_relay_copies, rs, rr, srcs, zones, after, name="gather_relay_wait_in")
        ds, dr, srcs, zones, _ = _split_start(diag, 1, srcs, zones, after, name="gather_pass_diag_start_in")
        srcs, zones = _split_wait(near, ns, nr, srcs, zones, after, name="gather_pass_near_wait_in")
        _, full = _split_wait(diag, ds, dr, srcs, zones, after, name="gather_pass_diag_wait_in")
        return {k: f.reshape(N_CHIPS, f.shape[1] * f.shape[2], f.shape[3]) for k, f in zip(keys, full)}

    def landed(group, after):
        keys, s_sems, r_sems, srcs, zones = flying[group]
        srcs, zones = _split_wait(_gather_copies, s_sems, r_sems, srcs, zones, after,
                                  name=f"gather_wait_{group}")
        s_sems, r_sems, srcs, zones, token = _split_start(_pass_copies, 4, srcs, zones, after,
                                                          name=f"gather_pass_start_{group}")
        passing[group] = (keys, s_sems, r_sems, srcs, zones)
        return token

    def arrive(group, after):
        keys, s_sems, r_sems, srcs, zones = passing[group]
        _, full = _split_wait(_pass_copies, s_sems, r_sems, srcs, zones, after,
                              name=f"gather_pass_wait_{group}")
        return {k: f.reshape(N_CHIPS, f.shape[1] * f.shape[2], f.shape[3]) for k, f in zip(keys, full)}

    def rows(g):
        return g.reshape(-1, g.shape[2])

    h1, r1 = _rms_fwd(xs, g_mix + tok[0:1, 0:1], name="norm_mix")
    W_in = rows(arrive_in(h1)["w_in"])
    W_main = jnp.concatenate([W_in[:OFF_A], W_in[OFF_P:]], axis=0)
    W_a = jnp.pad(W_in[OFF_A:OFF_P], ((0, RP - RANK), (0, 0)))
    tok = landed("mix", W_a)
    proj = _mm(h1, W_main, "nt", name="proj_main", out_dtype=F32, after=tok)
    gw = arrive("mix", proj)
    W_branch, W_out, small_all = rows(gw["w_branch"]), rows(gw["w_out"]), gw["small"]
    sm = [_unpack(small_all[j], small_offs, [weights[k].shape[1:] for k in small_sharded]) for j in range(N_CHIPS)]
    W_a2 = jnp.concatenate([sm[j][0] for j in range(N_CHIPS)], axis=1)
    W_a2p = jnp.pad(W_a2, ((0, RP - RANK), (0, 0))).astype(BF16)
    W_pool = jnp.concatenate([sm[j][1] for j in range(N_CHIPS)], axis=1).astype(BF16)
    W_conv = jnp.concatenate([sm[j][2] for j in range(N_CHIPS)], axis=1)

    a_pad = _mm(h1, W_a, "nt", name="proj_gate_rank", out_dtype=F32)
    o_gla, o_raw, states = _gla_fwd(proj, a_pad, W_a2p, b_a, g_gla, T=T, DK=DK, DV=DV)
    o_pool = _pool_fwd(proj, W_pool, pool_scale, T=T, PW=PW, col_block=3)
    tok = landed("cross", o_pool)
    y_gla = _mm(o_gla, W_branch, "nn", name="branch_gla", out_dtype=BF16, K=DV, after=tok)
    y_pool = _mm(o_pool, W_branch, "nn", name="branch_pool", out_dtype=BF16, K=PW, b_off=(DV, 0))
    merged = _merge_fwd(y_gla, y_pool, proj, T=T, D=D, col_block=2)
    x1 = _mm(merged, W_out, "nn", name="mix_out", out_dtype=F32, add=xs)

    h2, r2 = _rms_fwd(x1, g_cross, name="norm_cross")
    mem_n, rm = _rms_fwd(ms, g_mem, name="norm_mem")
    gw = arrive("cross", h2)
    W_cq, W_ckv, W_co = rows(gw["w_cq"]), gw["w_ckv"], rows(gw["w_co"])
    qc = _mm(h2, W_cq, "nn", name="cross_q", out_dtype=BF16)
    kv = _mm(mem_n, W_ckv, "nn", name="cross_kv", out_dtype=BF16, b_blocked=True)
    o_att = _attn_fwd(qc, kv, T=T, D=D, M=M)
    x2 = _mm(o_att, W_co, "nn", name="cross_out", out_dtype=F32, add=x1)

    tok = landed("up", x2)
    h3, r3 = _rms_fwd(x2, g_ffn + tok[0:1, 0:1], name="norm_ffn")
    W_up = arrive("up", h3)["w_up"]
    u0 = _mm(h3, W_up, "nn", name="ffn_up", out_dtype=F32, b_blocked=True)
    tok = landed("down", u0)
    f_act = _conv_fwd(u0, W_conv, conv_b + tok[0:1, 0:1], T=T, F=F)
    W_down = rows(arrive("down", f_act)["w_down"])
    x3 =_mm(f_act, W_down, "nn", name="ffn_down", out_dtype=F32, add=x2)

    loss_part, dx3, dx3_b, dg_final = _loss_head(x3, g_final.reshape(1, D), tgt)

    def col_shards(g):
        nb, K, Nb = g.shape
        return g.reshape(nb, 2, K // 2, Nb)

    def row_shards(g):
        R, N = g.shape
        return g.reshape(N_CHIPS, 2, R // N_CHIPS // 2, N)

    exchanging, in_flight = {}, []

    def exchange_start(group, keys, partials, after):
        recvs = [lax.empty((p.shape[0], *p.shape[2:]), p.dtype) for p in partials]
        s_sems, r_sems, partials, recvs, token = _split_start(
            _exchange_copies, 1, partials, recvs, after, name=f"grad_exchange_start_{group}")
        exchanging[group] = (keys, s_sems, r_sems, partials, recvs)
        return token

    def scatter_start(group, after):
        keys, s_sems, r_sems, partials, recvs = exchanging[group]
        partials, recvs = _split_wait(_exchange_copies, s_sems, r_sems, partials, recvs, after,
                                      name=f"grad_exchange_wait_{group}")
        chip_sums = [_add_halves(p, r, c_idx, name=f"grad_add_halves_{k}")
                     for k, p, r in zip(keys, partials, recvs)]
        lands = [lax.empty((3, *s.shape[1:]), s.dtype) for s in chip_sums]
        s_sems, r_sems, sums, lands, token = _split_start(
            _scatter_copies, 3, chip_sums, lands, after, name=f"grad_scatter_start_{group}")
        in_flight.append((group, keys, s_sems, r_sems, sums, lands))
        return token

    collected = []

    def collect(after):
        group, keys, s_sems, r_sems, sums, lands = in_flight.pop(0)
        sums, from_chips = _split_wait(_scatter_copies, s_sems, r_sems, sums, lands, after,
                                       name=f"grad_scatter_wait_{group}")
        half_sums = [_add_chips(s, r, chip_idx, name=f"grad_add_chips_{k}") for k, s, r in zip(keys, sums, from_chips)]
        others = [lax.empty(h.shape, h.dtype) for h in half_sums]
        s_sems, r_sems, half_sums, others, token = _split_start(
            _swap_copies, 1, half_sums, others, after, name=f"grad_swap_start_{group}")
        collected.append((keys, s_sems, r_sems, half_sums, others))
        return token

    df = _mm(dx3_b, W_down, "nt", name="d_ffn_act", out_dtype=BF16)
    dW_down = _mm(f_act, dx3_b, "tn", name="dw_down", out_dtype=BF16)
    du0, dconv_w, dconv_b = _conv_bwd(u0, W_conv, conv_b, df, T=T, F=F)
    dh3 = _mm(du0, W_up, "nt", name="d_ffn_in", out_dtype=F32, b_blocked=True, tk=F2 // N_CHIPS)
    dW_up = _mm(h3, du0, "tn", name="dw_up", out_dtype=BF16, out_blocks=N_CHIPS)
    tok = exchange_start("ffn", ["w_down", "w_up"], [row_shards(dW_down), col_shards(dW_up)], dh3)
    dx2, dx2_b, dg_ffn = _rms_bwd(dh3, x2, r3 + tok[0:1, 0:1], g_ffn, dx3, name="norm_ffn_bwd")

    do_att = _mm(dx2_b, W_co, "nt", name="d_cross_o", out_dtype=BF16)
    dW_co = _mm(o_att, dx2_b, "tn", name="dw_co", out_dtype=BF16)
    tok = scatter_start("ffn", dW_co)
    dq, dkv = _attn_bwd(qc, kv, do_att, T=T, D=D, M=M)
    dkv_b = dkv.astype(BF16)
    dW_cq = _mm(h2, dq, "tn", name="dw_cq", out_dtype=BF16, after=tok)
    dh2 = _mm(dq, W_cq, "nt", name="d_cross_in", out_dtype=F32)
    dW_ckv = _mm(mem_n, dkv_b, "tn", name="dw_ckv", out_dtype=BF16, out_blocks=N_CHIPS)
    dmem_n = _mm(dkv_b, W_ckv, "nt", name="d_mem", out_dtype=F32, b_blocked=True)
    tok = exchange_start("cross", ["w_co", "w_cq", "w_ckv"],
                         [row_shards(dW_co), row_shards(dW_cq), col_shards(dW_ckv)], dmem_n)
    _, _, dg_mem = _rms_bwd(dmem_n, ms, rm, g_mem, None, name="norm_mem_bwd")
    dx1, dx1_b, dg_cross = _rms_bwd(dh2, x1, r2 + tok[0:1, 0:1], g_cross, dx2, name="norm_cross_bwd")

    dmerged = _mm(dx1_b, W_out, "nt", name="d_merged", out_dtype=BF16)
    dW_out = _mm(merged, dx1_b, "tn", name="dw_out", out_dtype=BF16)
    tok = scatter_start("cross", dW_out)
    dy_gla, dy_pool, dgates = _merge_bwd(dmerged, y_gla, y_pool, proj, T=T, D=D, col_block=2)
    dW_br_gla = _mm(o_gla, dy_gla, "tn", name="dw_branch_gla", out_dtype=BF16, after=tok)
    dW_br_pool = _mm(o_pool, dy_pool, "tn", name="dw_branch_pool", out_dtype=BF16)
    do_gla = _mm(dy_gla, W_branch, "nt", name="d_o_gla", out_dtype=F32, N=DV)
    do_pool = _mm(dy_pool, W_branch, "nt", name="d_o_pool", out_dtype=F32, N=PW, b_off=(DV, 0))
    dp, dw_pool, dpool_scale = _pool_bwd(proj, W_pool, pool_scale, do_pool, T=T, PW=PW, col_block=3)
    dW_pool = jnp.transpose(dw_pool.reshape(POOL_GROUPS, N_CHIPS, GW // N_CHIPS, GW), (1, 0, 2, 3))
    tok = exchange_start("mix", ["w_out", "w_branch", "w_pool"],
                         [row_shards(dW_out), row_shards(jnp.concatenate([dW_br_gla, dW_br_pool], axis=0)),
                          row_shards(dW_pool.reshape(N_CHIPS * POOL_GROUPS * (GW // N_CHIPS), GW).astype(BF16))],
                         dp)
    dqkvr, da_pad, dw2, db_a, dg_gla = _gla_bwd(proj, a_pad, W_a2p, b_a + tok[0:1, 0:1], g_gla, o_raw, states,
                                               do_gla, T=T, DK=DK, DV=DV)
    tok = scatter_start("mix", dqkvr)
    dproj = jnp.concatenate([dqkvr, dp, dgates], axis=1)
    dW_main = _mm(dproj, h1, "tn", name="dw_in_main", out_dtype=BF16, after=tok)
    dW_a = _mm(da_pad, h1, "tn", name="dw_in_rank", out_dtype=BF16)
    dW_in = jnp.concatenate([dW_main[:OFF_A], dW_a[:RANK], dW_main[OFF_A:]], axis=0)
    tok = exchange_start("in", ["w_in"], [row_shards(dW_in)], dW_a)
    dh1 = _mm(dproj, W_main, "nn", name="d_mix_in_main", out_dtype=F32, after=tok)
    dh1 = _mm(da_pad, W_a, "nn", name="d_mix_in_rank", out_dtype=F32, add=dh1)
    dx0, _, dg_mix = _rms_bwd(dh1, xs, r1, g_mix, dx1, name="norm_mix_bwd")

    grads = {}

    small_grads = [loss_part, dg_mix, db_a, dg_gla, dpool_scale, dg_cross, dg_mem, dg_ffn, dconv_b, dg_final,
                   dw2[:RANK], dconv_w]
    small_buf, offs = _pack(small_grads)
    small_sum = _all_reduce_small(small_buf)
    red = _unpack(small_sum, offs, [g.shape for g in small_grads])
    loss = red[0][0, 0]
    for k, g in zip(small_repl, red[1:10]):
        grads[k] = g.reshape(weights[k].shape)
    nb = DK // N_CHIPS
    grads["w_a2"] = lax.dynamic_slice_in_dim(red[10], chip * nb, nb, axis=1)[None]
    nb = F2 // N_CHIPS
    grads["conv_w"] = lax.dynamic_slice_in_dim(red[11], chip * nb, nb, axis=1)[None]

    delta, new_m, new_v = {}, {}, {}

    def shard_rows(k, a):
        a = a[0]
        return a.T if k == "w_in" else a.reshape(-1, a.shape[-1])

    def whole(k, a):
        a = a.reshape(-1, a.shape[2])
        return (a.T if k == "w_in" else a).reshape(weights[k].shape)

    scatter_start("in", small_sum)

    def finish(after):
        keys, s_sems, r_sems, mine, others = collected.pop(0)
        mine, others = _split_wait(_swap_copies, s_sems, r_sems, mine, others, after,
                                   name=f"grad_swap_wait_{keys[0]}")
        for k, g_mine, g_other in zip(keys, mine, others):
            wmv = [halves(shard_rows(k, src[k])) for src in (weights, mom_m, mom_v)]
            res = _adamw_halves(*wmv, g_mine, g_other, c_idx, name=f"adamw_{k}")
            grads[k], delta[k], new_m[k], new_v[k] = (whole(k, a) for a in res)
        return res[1]

    after = in_flight[-1][4][0]
    while in_flight:
        after = collect(after)
        while len(collected) > 1:
            after = finish(after)
    finish(after)
    small = small_repl + ["w_a2", "conv_w"]
    packs = [_pack([src[k] for k in small])[0] for src in (weights, grads, mom_m, mom_v)]
    _, offs = _pack([weights[k] for k in small])
    outs = _adamw(*packs, name="adamw_small")
    for res, o in zip((delta, new_m, new_v), outs):
        for k, a in zip(small, _unpack(o, offs, [weights[k].shape for k in small])):
            res[k] = a

    return (loss, dx0[None], *[grads[k] for k in order], *[delta[k] for k in order],
            *[new_m[k] for k in order], *[new_v[k] for k in order])
```

```python
import functools

import jax
import jax.numpy as jnp
from jax import lax
from jax.experimental import pallas as pl
from jax.experimental.pallas import tpu as pltpu

F32 = jnp.float32
BF16 = jnp.bfloat16
MESH = pl.DeviceIdType.MESH
HIGHEST = lax.Precision.HIGHEST

EPS = 1e-6
GLA_HEADS = 4
GLA_CHUNK = 128
GLA_GATE_NORM = 16.0
POOL_GROUPS = 4
CROSS_HEADS = 4
CONV_W = 3
N_CHIPS = 4
LANES = 128
SUBLANES = 8
VMEM_LIMIT = 56 << 20

ADAM_LR = 0.001
ADAM_B1 = 0.9
ADAM_B2 = 0.999
ADAM_EPS = 1e-08
ADAM_WD = 0.01
ADAM_STEP = 10

NN = (((1,), (0,)), ((), ()))
NT = (((1,), (1,)), ((), ()))
TN = (((0,), (0,)), ((), ()))


ONE_PASS = lax.Precision.HIGH


def _dot(a, b, dn=NN, precision=None):
    return lax.dot_general(a, b, dn, precision=precision, preferred_element_type=F32)


def _tile(n, pref, align=LANES):
    t = (min(pref, n) // align) * align
    while t >= align:
        if n % t == 0:
            return t
        t -= align
    return n


def _pcall(body, *, name, out_shape, grid=(), in_specs=None, out_specs=None, scratch_shapes=(),
           semantics=None, prefetch=0, aliases=None, split_copy=False):
    params = dict(vmem_limit_bytes=VMEM_LIMIT)
    if semantics is not None:
        params["dimension_semantics"] = semantics
    if split_copy:
        params["has_side_effects"] = pltpu.SideEffectType.DATAFLOW_SIDE_EFFECTING
    if prefetch:
        grid_spec = pltpu.PrefetchScalarGridSpec(
            num_scalar_prefetch=prefetch, grid=grid, in_specs=in_specs, out_specs=out_specs,
            scratch_shapes=scratch_shapes)
        return pl.pallas_call(body, name=name, out_shape=out_shape, grid_spec=grid_spec,
                              compiler_params=pltpu.CompilerParams(**params))
    kw = {}
    if aliases is not None:
        kw["input_output_aliases"] = aliases
    if in_specs is not None:
        kw["in_specs"] = in_specs
    if out_specs is not None:
        kw["out_specs"] = out_specs
    return pl.pallas_call(body, name=name, out_shape=out_shape, grid=grid,
                          scratch_shapes=scratch_shapes,
                          compiler_params=pltpu.CompilerParams(**params), **kw)


def _sigmoid(x):
    return 1.0 / (1.0 + jnp.exp(-x))


def _log_sigmoid(x):
    return jnp.minimum(x, 0.0) - jnp.log(1.0 + jnp.exp(-jnp.abs(x)))


def _mm(a, b, mode, *, name, out_dtype, M=None, N=None, K=None, a_off=(0, 0), b_off=(0, 0),
        add=None, b_blocked=False, out_blocks=0, after=None, tm=1536, tn=1536, tk=2048):
    if b_blocked:
        nb, R, Cb = b.shape
        b_rows, b_cols = R, nb * Cb
    else:
        b_rows, b_cols = b.shape
    if mode == "nn":
        M = M or a.shape[0]; K = K or a.shape[1]; N = N or b_cols
    elif mode == "nt":
        M = M or a.shape[0]; K = K or a.shape[1]; N = N or b_rows
    else:
        K = K or a.shape[0]; M = M or a.shape[1]; N = N or b_cols
    tm = _tile(M, tm, LANES if mode == "tn" else 16)
    tn = _tile(Cb if (b_blocked and mode != "nt") else (N // out_blocks if out_blocks else N), tn)
    tk = _tile(Cb if (b_blocked and mode == "nt") else K, tk)
    nk = K // tk
    dn = {"nn": NN, "nt": NT, "tn": TN}[mode]

    def off(o, t):
        assert o % t == 0, (name, o, t)
        return o // t

    if mode == "tn":
        ar, ac = off(a_off[0], tk), off(a_off[1], tm)
        a_spec = pl.BlockSpec((tk, tm), lambda i, j, k: (k + ar, i + ac))
    else:
        ar, ac = off(a_off[0], tm), off(a_off[1], tk)
        a_spec = pl.BlockSpec((tm, tk), lambda i, j, k: (i + ar, k + ac))
    if b_blocked and mode == "nt":
        per = Cb // tk
        b_spec = pl.BlockSpec((None, tn, tk), lambda i, j, k: (k // per, j, k % per))
    elif b_blocked:
        per = Cb // tn
        b_spec = pl.BlockSpec((None, tk, tn), lambda i, j, k: (j // per, k, j % per))
    elif mode == "nt":
        br, bc = off(b_off[0], tn), off(b_off[1], tk)
        b_spec = pl.BlockSpec((tn, tk), lambda i, j, k: (j + br, k + bc))
    else:
        br, bc = off(b_off[0], tk), off(b_off[1], tn)
        b_spec = pl.BlockSpec((tk, tn), lambda i, j, k: (k + br, j + bc))
    if out_blocks:
        per_o = N // out_blocks // tn
        o_spec = pl.BlockSpec((None, tm, tn), lambda i, j, k: (j // per_o, i, j % per_o))
        out_shape = jax.ShapeDtypeStruct((out_blocks, M, N // out_blocks), out_dtype)
    else:
        o_spec = pl.BlockSpec((tm, tn), lambda i, j, k: (i, j))
        out_shape = jax.ShapeDtypeStruct((M, N), out_dtype)
    in_specs = [a_spec, b_spec]
    args = [a, b]
    if add is not None:
        assert not out_blocks
        in_specs.append(o_spec)
        args.append(add)
    if after is not None:
        in_specs.append(pl.BlockSpec(memory_space=pl.ANY))
        args.append(after)
    n_in = len(args)

    def finish(r, refs):
        if add is not None:
            r = r + refs[2][...]
        o_ref = refs[n_in]
        o_ref[...] = r.astype(o_ref.dtype)

    def body_one(*refs):
        finish(_dot(refs[0][...].astype(BF16), refs[1][...].astype(BF16), dn), refs)

    def body_acc(*refs):
        acc_ref = refs[-1]
        k = pl.program_id(2)

        @pl.when(k == 0)
        def _():
            acc_ref[...] = jnp.zeros_like(acc_ref)

        acc_ref[...] += _dot(refs[0][...].astype(BF16), refs[1][...].astype(BF16), dn)

        @pl.when(k == nk - 1)
        def _():
            finish(acc_ref[...], refs)

    return _pcall(body_one if nk == 1 else body_acc, name=name, out_shape=out_shape,
                  grid=(M // tm, N // tn, nk), in_specs=in_specs, out_specs=o_spec,
                  scratch_shapes=[] if nk == 1 else [pltpu.VMEM((tm, tn), F32)],
                  semantics=("parallel", "parallel", "arbitrary"))(*args)


def _rms_fwd(x, g, *, name):
    T, D = x.shape
    tr = _tile(T, 128, 16)

    def body(x_ref, g_ref, h_ref, r_ref):
        xv = x_ref[...]
        r = lax.rsqrt(jnp.mean(xv * xv, axis=-1, keepdims=True) + EPS)
        h_ref[...] = (xv * r * g_ref[...]).astype(h_ref.dtype)
        r_ref[...] = r

    row = pl.BlockSpec((tr, D), lambda i: (i, 0))
    return _pcall(body, name=name,
                  out_shape=(jax.ShapeDtypeStruct((T, D), BF16), jax.ShapeDtypeStruct((T, 1), F32)),
                  grid=(T // tr,),
                  in_specs=[row, pl.BlockSpec((1, D), lambda i: (0, 0))],
                  out_specs=(row, pl.BlockSpec((tr, 1), lambda i: (i, 0))),
                  semantics=("parallel",))(x, g)


def _rms_bwd(dh, x, rstd, g, dres, *, name):
    T, D = x.shape
    tr = _tile(T, 128, 16)
    has_res = dres is not None

    def body(*refs):
        if has_res:
            dh_ref, x_ref, r_ref, g_ref, res_ref, dx_ref, dxb_ref, dg_ref = refs
        else:
            dh_ref, x_ref, r_ref, g_ref, dx_ref, dxb_ref, dg_ref = refs
        r = r_ref[...]
        xh = x_ref[...] * r
        dhv = dh_ref[...].astype(F32)
        dxh = dhv * g_ref[...]
        m = jnp.mean(dxh * xh, axis=-1, keepdims=True)
        dx = r * (dxh - xh * m)
        if has_res:
            dx = dx + res_ref[...]
        dx_ref[...] = dx
        dxb_ref[...] = dx.astype(BF16)

        @pl.when(pl.program_id(0) == 0)
        def _():
            dg_ref[...] = jnp.zeros_like(dg_ref)

        dg_ref[...] += jnp.sum(dhv * xh, axis=0, keepdims=True)

    row = pl.BlockSpec((tr, D), lambda i: (i, 0))
    vec = pl.BlockSpec((1, D), lambda i: (0, 0))
    in_specs = [row, row, pl.BlockSpec((tr, 1), lambda i: (i, 0)), vec]
    args = [dh, x, rstd, g]
    if has_res:
        in_specs.append(row)
        args.append(dres)
    return _pcall(body, name=name,
                  out_shape=(jax.ShapeDtypeStruct((T, D), F32), jax.ShapeDtypeStruct((T, D), BF16),
                             jax.ShapeDtypeStruct((1, D), F32)),
                  grid=(T // tr,), in_specs=in_specs, out_specs=(row, row, vec),
                  semantics=("arbitrary",))(*args)


def _loss_head(x3, g, tgt):
    T, D = x3.shape
    tr = _tile(T, 128, 16)

    def body(x_ref, g_ref, t_ref, loss_ref, dx_ref, dxb_ref, dg_ref):
        xv = x_ref[...]
        gv = g_ref[...]
        r = lax.rsqrt(jnp.mean(xv * xv, axis=-1, keepdims=True) + EPS)
        xh = xv * r
        err = xh * gv - t_ref[...]
        dy = err * (1.0 / D)
        dxh = dy * gv
        m = jnp.mean(dxh * xh, axis=-1, keepdims=True)
        dx = r * (dxh - xh * m)
        dx_ref[...] = dx
        dxb_ref[...] = dx.astype(BF16)

        @pl.when(pl.program_id(0) == 0)
        def _():
            dg_ref[...] = jnp.zeros_like(dg_ref)
            loss_ref[...] = jnp.zeros_like(loss_ref)

        dg_ref[...] += jnp.sum(dy * xh, axis=0, keepdims=True)
        part = 0.5 * jnp.sum(jnp.mean(err * err, axis=-1, keepdims=True), axis=0, keepdims=True)
        loss_ref[...] += jnp.broadcast_to(part, loss_ref.shape)

    row = pl.BlockSpec((tr, D), lambda i: (i, 0))
    vec = pl.BlockSpec((1, D), lambda i: (0, 0))
    return _pcall(body, name="loss_head",
                  out_shape=(jax.ShapeDtypeStruct((1, LANES), F32), jax.ShapeDtypeStruct((T, D), F32),
                             jax.ShapeDtypeStruct((T, D), BF16), jax.ShapeDtypeStruct((1, D), F32)),
                  grid=(T // tr,), in_specs=[row, vec, row],
                  out_specs=(pl.BlockSpec((1, LANES), lambda i: (0, 0)), row, row, vec),
                  semantics=("arbitrary",))(x3, g, tgt)


def _gla_chunk_terms(qk, a_ref, w2_ref, ba_ref, DK):
    C = qk.shape[0]
    gp = _dot(a_ref[...].astype(BF16), w2_ref[...]) + ba_ref[...]
    la = _log_sigmoid(gp) * (1.0 / GLA_GATE_NORM)
    row = lax.broadcasted_iota(jnp.int32, (C, C), 0)
    col = lax.broadcasted_iota(jnp.int32, (C, C), 1)
    causal = row >= col
    b = _dot(causal.astype(F32), la, precision=HIGHEST)
    return gp, b, causal


def _gla_fwd(proj, a_pad, w2, b_a, g_gla, *, T, DK, DV):
    assert 2 * DK == DV
    H = GLA_HEADS
    HK, HV = DK // H, DV // H
    C = GLA_CHUNK
    n = T // C
    RP = a_pad.shape[1]
    scale = HK ** -0.5

    def body(qk_ref, v_ref, r_ref, a_ref, w2_ref, ba_ref, gg_ref, og_ref, oraw_ref, st_ref, s_ref):
        @pl.when(pl.program_id(0) == 0)
        def _():
            s_ref[...] = jnp.zeros_like(s_ref)

        st_ref[...] = s_ref[...]
        qk = qk_ref[...]
        _, b, causal = _gla_chunk_terms(qk, a_ref, w2_ref, ba_ref, DK)
        for h in range(H):
            ks = slice(h * HK, (h + 1) * HK)
            vs = slice(h * HV, (h + 1) * HV)
            bh = b[:, ks]
            b_last = bh[C - 1:C, :]
            qt = qk[:, ks] * scale * jnp.exp(bh)
            kh = qk[:, DK + h * HK:DK + (h + 1) * HK]
            kt = kh * jnp.exp(-bh)
            khat = kh * jnp.exp(b_last - bh)
            a_mat = jnp.where(causal, _dot(qt, kt, NT, ONE_PASS), 0.0)
            vh = v_ref[:, vs]
            s_t = s_ref[h]
            o = _dot(a_mat, vh, NN, ONE_PASS) + _dot(qt, s_t, NT, ONE_PASS)
            s_ref[h] = s_t * jnp.exp(b_last) + _dot(vh, khat, TN, ONE_PASS)
            rs = lax.rsqrt(jnp.mean(o * o, axis=-1, keepdims=True) + EPS)
            rr = r_ref[:, vs]
            og = o * rs * gg_ref[:, vs] * (rr * _sigmoid(rr))
            oraw_ref[:, vs] = o
            og_ref[:, vs] = og.astype(BF16)

    blk = lambda j: pl.BlockSpec((C, DV), lambda i: (i, j))
    full = lambda s: pl.BlockSpec(s, lambda i: (0,) * len(s))
    return _pcall(
        body, name="gla_fwd",
        out_shape=(jax.ShapeDtypeStruct((T, DV), BF16), jax.ShapeDtypeStruct((T, DV), F32),
                   jax.ShapeDtypeStruct((n, H, HV, HK), F32)),
        grid=(n,),
        in_specs=[blk(0), blk(1), blk(2), pl.BlockSpec((C, RP), lambda i: (i, 0)),
                  full((RP, DK)), full((1, DK)), full((1, DV))],
        out_specs=(blk(0), blk(0), pl.BlockSpec((None, H, HV, HK), lambda i: (i, 0, 0, 0))),
        scratch_shapes=[pltpu.VMEM((H, HV, HK), F32)],
        semantics=("arbitrary",))(proj, proj, proj, a_pad, w2, b_a, g_gla)


def _gla_bwd(proj, a_pad, w2, b_a, g_gla, o_raw, states, do_gla, *, T, DK, DV):
    H = GLA_HEADS
    HK, HV = DK // H, DV // H
    C = GLA_CHUNK
    n = T // C
    RP = a_pad.shape[1]
    scale = HK ** -0.5

    def body(qk_ref, v_ref, r_ref, a_ref, w2_ref, ba_ref, gg_ref, oraw_ref, st_ref, dog_ref,
             dqkvr_ref, da_ref, dw2_ref, dba_ref, dgg_ref, ds_ref):
        @pl.when(pl.program_id(0) == 0)
        def _():
            ds_ref[...] = jnp.zeros_like(ds_ref)
            dw2_ref[...] = jnp.zeros_like(dw2_ref)
            dba_ref[...] = jnp.zeros_like(dba_ref)
            dgg_ref[...] = jnp.zeros_like(dgg_ref)

        qk = qk_ref[...]
        gp, b, causal = _gla_chunk_terms(qk, a_ref, w2_ref, ba_ref, DK)
        row = lax.broadcasted_iota(jnp.int32, (C, C), 0)
        col = lax.broadcasted_iota(jnp.int32, (C, C), 1)
        upper = (col >= row).astype(F32)
        dla_parts = []
        for h in range(H):
            ks = slice(h * HK, (h + 1) * HK)
            vs = slice(h * HV, (h + 1) * HV)
            bh = b[:, ks]
            b_last = bh[C - 1:C, :]
            eb = jnp.exp(bh)
            emb = jnp.exp(-bh)
            ehat = jnp.exp(b_last - bh)
            e_last = jnp.exp(b_last)
            qt = qk[:, ks] * scale * eb
            kh = qk[:, DK + h * HK:DK + (h + 1) * HK]
            kt = kh * emb
            khat = kh * ehat
            a_mat = jnp.where(causal, _dot(qt, kt, NT, ONE_PASS), 0.0)
            vh = v_ref[:, vs]
            o = oraw_ref[:, vs]
            rs = lax.rsqrt(jnp.mean(o * o, axis=-1, keepdims=True) + EPS)
            on = o * rs
            gg = gg_ref[:, vs]
            rr = r_ref[:, vs]
            sg = _sigmoid(rr)
            d_out = dog_ref[:, vs]
            dr = d_out * (on * gg) * (sg * (1.0 + rr * (1.0 - sg)))
            d_og = d_out * (rr * sg)
            dgg_ref[:, vs] += jnp.sum(d_og * on, axis=0, keepdims=True)
            d_on = d_og * gg
            d_o = rs * (d_on - on * jnp.mean(d_on * on, axis=-1, keepdims=True))
            s_t = st_ref[h]
            ds_t = ds_ref[h]
            d_a = jnp.where(causal, _dot(d_o, vh, NT, ONE_PASS), 0.0)
            dv = _dot(a_mat, d_o, TN, ONE_PASS) + _dot(khat, ds_t, NT, ONE_PASS)
            dqt = _dot(d_a, kt, NN, ONE_PASS) + _dot(d_o, s_t, NN, ONE_PASS)
            dkt = _dot(d_a, qt, TN, ONE_PASS)
            dkhat = _dot(vh, ds_t, NN, ONE_PASS)
            ds_ref[h] = ds_t * e_last + _dot(d_o, qt, TN, ONE_PASS)
            dq = dqt * eb * scale
            dk = dkt * emb + dkhat * ehat
            db = dqt * qt - dkt * kt - dkhat * khat
            d_last = (jnp.sum(dkhat * khat, axis=0, keepdims=True)
                      + e_last * jnp.sum(ds_t * s_t, axis=0, keepdims=True))
            dla_parts.append(_dot(upper, db, NN, HIGHEST) + d_last)
            dqkvr_ref[:, ks] = dq.astype(BF16)
            dqkvr_ref[:, DK + h * HK:DK + (h + 1) * HK] = dk.astype(BF16)
            dqkvr_ref[:, DV + h * HV:DV + (h + 1) * HV] = dv.astype(BF16)
            dqkvr_ref[:, 2 * DV + h * HV:2 * DV + (h + 1) * HV] = dr.astype(BF16)
        dla = jnp.concatenate(dla_parts, axis=1)
        dgp = dla * (1.0 / GLA_GATE_NORM) * _sigmoid(-gp)
        dba_ref[...] += jnp.sum(dgp, axis=0, keepdims=True)
        dgp_b = dgp.astype(BF16)
        dw2_ref[...] += _dot(a_ref[...].astype(BF16), dgp_b, TN)
        da_ref[...] = _dot(dgp_b, w2_ref[...], NT).astype(BF16)

    rev = lambda j: pl.BlockSpec((C, DV), lambda i: (n - 1 - i, j))
    full = lambda s: pl.BlockSpec(s, lambda i: (0,) * len(s))
    return _pcall(
        body, name="gla_bwd",
        out_shape=(jax.ShapeDtypeStruct((T, 3 * DV), BF16), jax.ShapeDtypeStruct((T, RP), BF16),
                   jax.ShapeDtypeStruct((RP, DK), F32), jax.ShapeDtypeStruct((1, DK), F32),
                   jax.ShapeDtypeStruct((1, DV), F32)),
        grid=(n,),
        in_specs=[rev(0), rev(1), rev(2), pl.BlockSpec((C, RP), lambda i: (n - 1 - i, 0)),
                  full((RP, DK)), full((1, DK)), full((1, DV)), rev(0),
                  pl.BlockSpec((None, H, HV, HK), lambda i: (n - 1 - i, 0, 0, 0)), rev(0)],
        out_specs=(pl.BlockSpec((C, 3 * DV), lambda i: (n - 1 - i, 0)),
                   pl.BlockSpec((C, RP), lambda i: (n - 1 - i, 0)),
                   full((RP, DK)), full((1, DK)), full((1, DV))),
        scratch_shapes=[pltpu.VMEM((H, HV, HK), F32)],
        semantics=("arbitrary",))(proj, proj, proj, a_pad, w2, b_a, g_gla, o_raw, states, do_gla)


def _pool_windows(p, g, T):
    t = lax.broadcasted_iota(jnp.int32, (T, 1), 0)
    s = p
    for lvl in range(POOL_GROUPS):
        sh = 1 << lvl
        nxt = s + jnp.where(t >= sh, pltpu.roll(s, sh, 0), 0.0)
        s = jnp.where(lvl <= g, nxt, s)
    win = jnp.left_shift(2, g)
    inv = 1.0 / jnp.minimum(t + 1, win).astype(F32)
    return s * inv - p, inv


def _pool_fwd(proj, w_pool, scale, *, T, PW, col_block):
    GW = PW // POOL_GROUPS
    per = PW // GW

    def body(p_ref, w_ref, s_ref, o_ref):
        g = pl.program_id(0)
        pooled, _ = _pool_windows(p_ref[...], g, T)
        mixed = _dot(pooled.astype(BF16), w_ref[...])
        o_ref[...] = (mixed * s_ref[...]).astype(BF16)

    return _pcall(body, name="pool_fwd", out_shape=jax.ShapeDtypeStruct((T, PW), BF16),
                  grid=(POOL_GROUPS,),
                  in_specs=[pl.BlockSpec((T, GW), lambda g: (0, col_block * per + g)),
                            pl.BlockSpec((None, GW, GW), lambda g: (g, 0, 0)),
                            pl.BlockSpec((1, GW), lambda g: (0, g))],
                  out_specs=pl.BlockSpec((T, GW), lambda g: (0, g)),
                  semantics=("parallel",))(proj, w_pool, scale)


def _pool_bwd(proj, w_pool, scale, do_pool, *, T, PW, col_block):
    GW = PW // POOL_GROUPS
    per = PW // GW

    def body(p_ref, w_ref, s_ref, do_ref, dp_ref, dw_ref, dsc_ref):
        g = pl.program_id(0)
        pooled, inv = _pool_windows(p_ref[...], g, T)
        pooled_b = pooled.astype(BF16)
        w = w_ref[...]
        mixed = _dot(pooled_b, w)
        d_out = do_ref[...]
        dsc_ref[...] = jnp.sum(d_out * mixed, axis=0, keepdims=True)
        dmixed = (d_out * s_ref[...]).astype(BF16)
        dw_ref[...] = _dot(pooled_b, dmixed, TN)
        dpooled = _dot(dmixed, w, NT)
        t = lax.broadcasted_iota(jnp.int32, (T, 1), 0)
        s = dpooled * inv
        for lvl in range(POOL_GROUPS):
            sh = 1 << lvl
            nxt = s + jnp.where(t < T - sh, pltpu.roll(s, T - sh, 0), 0.0)
            s = jnp.where(lvl <= g, nxt, s)
        dp_ref[...] = (s - dpooled).astype(BF16)

    return _pcall(body, name="pool_bwd",
                  out_shape=(jax.ShapeDtypeStruct((T, PW), BF16),
                             jax.ShapeDtypeStruct((POOL_GROUPS, GW, GW), F32),
                             jax.ShapeDtypeStruct((1, PW), F32)),
                  grid=(POOL_GROUPS,),
                  in_specs=[pl.BlockSpec((T, GW), lambda g: (0, col_block * per + g)),
                            pl.BlockSpec((None, GW, GW), lambda g: (g, 0, 0)),
                            pl.BlockSpec((1, GW), lambda g: (0, g)),
                            pl.BlockSpec((T, GW), lambda g: (0, g))],
                  out_specs=(pl.BlockSpec((T, GW), lambda g: (0, g)),
                             pl.BlockSpec((None, GW, GW), lambda g: (g, 0, 0)),
                             pl.BlockSpec((1, GW), lambda g: (0, g))),
                  semantics=("parallel",))(proj, w_pool, scale, do_pool)


def _merge_fwd(y_gla, y_pool, proj, *, T, D, col_block):
    tr = _tile(T, 128, 16)

    def body(yg_ref, yp_ref, g1_ref, g2_ref, o_ref):
        o_ref[...] = (_sigmoid(g1_ref[...]) * yg_ref[...]
                      + _sigmoid(g2_ref[...]) * yp_ref[...]).astype(BF16)

    row = pl.BlockSpec((tr, D), lambda i: (i, 0))
    return _pcall(body, name="merge_fwd", out_shape=jax.ShapeDtypeStruct((T, D), BF16),
                  grid=(T // tr,),
                  in_specs=[row, row, pl.BlockSpec((tr, D), lambda i: (i, col_block)),
                            pl.BlockSpec((tr, D), lambda i: (i, col_block + 1))],
                  out_specs=row, semantics=("parallel",))(y_gla, y_pool, proj, proj)


def _merge_bwd(dmerged, y_gla, y_pool, proj, *, T, D, col_block):
    tr = _tile(T, 128, 16)

    def body(dm_ref, yg_ref, yp_ref, g1_ref, g2_ref, dyg_ref, dyp_ref, dg_ref):
        dm = dm_ref[...]
        s1 = _sigmoid(g1_ref[...])
        s2 = _sigmoid(g2_ref[...])
        dyg_ref[...] = (dm * s1).astype(BF16)
        dyp_ref[...] = (dm * s2).astype(BF16)
        dg_ref[:, :D] = (dm * yg_ref[...] * s1 * (1.0 - s1)).astype(BF16)
        dg_ref[:, D:] = (dm * yp_ref[...] * s2 * (1.0 - s2)).astype(BF16)

    row = pl.BlockSpec((tr, D), lambda i: (i, 0))
    return _pcall(body, name="merge_bwd",
                  out_shape=(jax.ShapeDtypeStruct((T, D), BF16), jax.ShapeDtypeStruct((T, D), BF16),
                             jax.ShapeDtypeStruct((T, 2 * D), BF16)),
                  grid=(T // tr,),
                  in_specs=[row, row, row, pl.BlockSpec((tr, D), lambda i: (i, col_block)),
                            pl.BlockSpec((tr, D), lambda i: (i, col_block + 1))],
                  out_specs=(row, row, pl.BlockSpec((tr, 2 * D), lambda i: (i, 0))),
                  semantics=("parallel",))(dmerged, y_gla, y_pool, proj, proj)


def _attn_fwd(q, kv, *, T, D, M):
    H = CROSS_HEADS
    HD = D // H
    tq = _tile(T, 512, 16)
    scale = HD ** -0.5

    def body(q_ref, kv_ref, o_ref):
        for h in range(H):
            hs = slice(h * HD, (h + 1) * HD)
            s = _dot(q_ref[:, hs], kv_ref[:, hs], NT) * scale
            e = jnp.exp(s - jnp.max(s, axis=-1, keepdims=True))
            p = e / jnp.sum(e, axis=-1, keepdims=True)
            o_ref[:, hs] = _dot(p.astype(BF16), kv_ref[:, D + h * HD:D + (h + 1) * HD]).astype(BF16)

    row = pl.BlockSpec((tq, D), lambda i: (i, 0))
    return _pcall(body, name="attn_fwd", out_shape=jax.ShapeDtypeStruct((T, D), BF16),
                  grid=(T // tq,), in_specs=[row, pl.BlockSpec((M, 2 * D), lambda i: (0, 0))],
                  out_specs=row, semantics=("parallel",))(q, kv)


def _attn_bwd(q, kv, do, *, T, D, M):
    H = CROSS_HEADS
    HD = D // H
    tq = _tile(T, 512, 16)
    scale = HD ** -0.5

    def body(q_ref, kv_ref, do_ref, dq_ref, dkv_ref):
        @pl.when(pl.program_id(0) == 0)
        def _():
            dkv_ref[...] = jnp.zeros_like(dkv_ref)

        for h in range(H):
            hs = slice(h * HD, (h + 1) * HD)
            vs = slice(D + h * HD, D + (h + 1) * HD)
            qh = q_ref[:, hs]
            kh = kv_ref[:, hs]
            s = _dot(qh, kh, NT) * scale
            e = jnp.exp(s - jnp.max(s, axis=-1, keepdims=True))
            p = e / jnp.sum(e, axis=-1, keepdims=True)
            p_b = p.astype(BF16)
            d_o = do_ref[:, hs]
            dkv_ref[:, vs] += _dot(p_b, d_o, TN)
            dp = _dot(d_o, kv_ref[:, vs], NT)
            ds = (p * (dp - jnp.sum(dp * p, axis=-1, keepdims=True)) * scale).astype(BF16)
            dq_ref[:, hs] = _dot(ds, kh).astype(BF16)
            dkv_ref[:, hs] += _dot(ds, qh, TN)

    row = pl.BlockSpec((tq, D), lambda i: (i, 0))
    full = pl.BlockSpec((M, 2 * D), lambda i: (0, 0))
    return _pcall(body, name="attn_bwd",
                  out_shape=(jax.ShapeDtypeStruct((T, D), BF16), jax.ShapeDtypeStruct((M, 2 * D), F32)),
                  grid=(T // tq,), in_specs=[row, full, row], out_specs=(row, full),
                  semantics=("arbitrary",))(q, kv, do)


def _shift_down(x, halo, s):
    out = pltpu.roll(x, s, 0)
    t8 = lax.broadcasted_iota(jnp.int32, (SUBLANES, 1), 0)
    head = out[:SUBLANES]
    for j in range(s):
        head = jnp.where(t8 == j, halo[SUBLANES - s + j:SUBLANES - s + j + 1, :], head)
    return head if x.shape[0] == SUBLANES else jnp.concatenate([head, out[SUBLANES:]], axis=0)


def _shift_up(x, halo, s):
    rows = x.shape[0]
    out = pltpu.roll(x, rows - s, 0)
    t8 = lax.broadcasted_iota(jnp.int32, (SUBLANES, 1), 0)
    tail = out[rows - SUBLANES:]
    for j in range(s):
        tail = jnp.where(t8 == SUBLANES - s + j, halo[j:j + 1, :], tail)
    return jnp.concatenate([out[:rows - SUBLANES], tail], axis=0)


def _conv_tiles(T):
    tt = _tile(T, 128, SUBLANES)
    return tt, tt // SUBLANES, T // SUBLANES


def _conv_fwd(u0, conv_w, conv_b, *, T, F):
    tt, hb, _ = _conv_tiles(T)
    cw = _tile(F, LANES)

    def body(u_ref, prev_ref, w_ref, b_ref, f_ref):
        i = pl.program_id(0)

        def conv(cs):
            x = u_ref[:, cs]
            halo = jnp.where(i > 0, prev_ref[:, cs], 0.0)
            return (w_ref[2:3, cs] * x + w_ref[1:2, cs] * _shift_down(x, halo, 1)
                    + w_ref[0:1, cs] * _shift_down(x, halo, 2) + b_ref[:, cs])

        for j in range(F // cw):
            gate = conv(slice(j * cw, (j + 1) * cw))
            val = conv(slice(F + j * cw, F + (j + 1) * cw))
            f_ref[:, j * cw:(j + 1) * cw] = (gate * _sigmoid(gate) * val).astype(BF16)

    return _pcall(body, name="conv_fwd", out_shape=jax.ShapeDtypeStruct((T, F), BF16),
                  grid=(T // tt,),
                  in_specs=[pl.BlockSpec((tt, 2 * F), lambda i: (i, 0)),
                            pl.BlockSpec((SUBLANES, 2 * F), lambda i: (jnp.maximum(i * hb - 1, 0), 0)),
                            pl.BlockSpec((CONV_W, 2 * F), lambda i: (0, 0)),
                            pl.BlockSpec((1, 2 * F), lambda i: (0, 0))],
                  out_specs=pl.BlockSpec((tt, F), lambda i: (i, 0)),
                  semantics=("parallel",))(u0, u0, conv_w, conv_b)


def _conv_bwd(u0, conv_w, conv_b, df, *, T, F):
    tt, hb, nb = _conv_tiles(T)
    nt = T // tt
    cw = _tile(F, LANES)

    def body(u_ref, prev_ref, next_ref, df_ref, dfn_ref, w_ref, b_ref, du0_ref, dw_ref, db_ref):
        i = pl.program_id(0)

        @pl.when(i == 0)
        def _():
            dw_ref[...] = jnp.zeros_like(dw_ref)
            db_ref[...] = jnp.zeros_like(db_ref)

        def conv(cs):
            x = u_ref[:, cs]
            halo = jnp.where(i > 0, prev_ref[:, cs], 0.0)
            x1 = _shift_down(x, halo, 1)
            x2 = _shift_down(x, halo, 2)
            u = w_ref[2:3, cs] * x + w_ref[1:2, cs] * x1 + w_ref[0:1, cs] * x2 + b_ref[:, cs]
            xn = next_ref[:, cs]
            tail = x[tt - SUBLANES:, :]
            un = (w_ref[2:3, cs] * xn + w_ref[1:2, cs] * _shift_down(xn, tail, 1)
                  + w_ref[0:1, cs] * _shift_down(xn, tail, 2) + b_ref[:, cs])
            return u, un, (x, x1, x2)

        def glu_grad(gate, val, dff):
            sg = _sigmoid(gate)
            return dff * val * (sg * (1.0 + gate * (1.0 - sg))), dff * (gate * sg)

        def finish(cs, du, dun, xs):
            du0 = (w_ref[2:3, cs] * du + w_ref[1:2, cs] * _shift_up(du, dun, 1)
                   + w_ref[0:1, cs] * _shift_up(du, dun, 2))
            du0_ref[:, cs] = du0.astype(BF16)
            db_ref[:, cs] += jnp.sum(du, axis=0, keepdims=True)
            dw_ref[2:3, cs] += jnp.sum(du * xs[0], axis=0, keepdims=True)
            dw_ref[1:2, cs] += jnp.sum(du * xs[1], axis=0, keepdims=True)
            dw_ref[0:1, cs] += jnp.sum(du * xs[2], axis=0, keepdims=True)

        for j in range(F // cw):
            fs = slice(j * cw, (j + 1) * cw)
            gs, vs = fs, slice(F + j * cw, F + (j + 1) * cw)
            ug, ung, xg = conv(gs)
            uv, unv, xv = conv(vs)
            dug, duv = glu_grad(ug, uv, df_ref[:, fs].astype(F32))
            dung, dunv = glu_grad(ung, unv, dfn_ref[0:SUBLANES, fs].astype(F32))
            dung = jnp.where(i < nt - 1, dung, 0.0)
            dunv = jnp.where(i < nt - 1, dunv, 0.0)
            finish(gs, dug, dung, xg)
            finish(vs, duv, dunv, xv)

    wide = lambda rows, fn: pl.BlockSpec((rows, 2 * F), fn)
    nxt = lambda i: (jnp.minimum((i + 1) * hb, nb - 1), 0)
    return _pcall(body, name="conv_bwd",
                  out_shape=(jax.ShapeDtypeStruct((T, 2 * F), BF16),
                             jax.ShapeDtypeStruct((CONV_W, 2 * F), F32),
                             jax.ShapeDtypeStruct((1, 2 * F), F32)),
                  grid=(nt,),
                  in_specs=[wide(tt, lambda i: (i, 0)),
                            wide(SUBLANES, lambda i: (jnp.maximum(i * hb - 1, 0), 0)),
                            wide(SUBLANES, nxt),
                            pl.BlockSpec((tt, F), lambda i: (i, 0)),
                            pl.BlockSpec((2 * SUBLANES, F),
                                         lambda i: (jnp.minimum((i + 1) * (hb // 2), nb // 2 - 1), 0)),
                            wide(CONV_W, lambda i: (0, 0)), wide(1, lambda i: (0, 0))],
                  out_specs=(wide(tt, lambda i: (i, 0)), wide(CONV_W, lambda i: (0, 0)),
                             wide(1, lambda i: (0, 0))),
                  semantics=("arbitrary",))(u0, u0, u0, df, df, conv_w, conv_b)


def _adamw(w, g, m, v, *, name):
    R, C = w.shape
    tr = _tile(R, max(SUBLANES, (1 << 19) // max(C, 1) // SUBLANES * SUBLANES), SUBLANES)
    c1 = 1.0 / (1.0 - ADAM_B1 ** ADAM_STEP)
    c2 = 1.0 / (1.0 - ADAM_B2 ** ADAM_STEP)

    def body(w_ref, g_ref, m_ref, v_ref, d_ref, mo_ref, vo_ref):
        gv = g_ref[...]
        mn = ADAM_B1 * m_ref[...] + (1.0 - ADAM_B1) * gv
        vn = ADAM_B2 * v_ref[...] + (1.0 - ADAM_B2) * (gv * gv)
        d_ref[...] = -ADAM_LR * ((mn * c1) / (jnp.sqrt(vn * c2) + ADAM_EPS) + ADAM_WD * w_ref[...])
        mo_ref[...] = mn
        vo_ref[...] = vn

    blk = pl.BlockSpec((tr, C), lambda i: (i, 0))
    shp = jax.ShapeDtypeStruct((R, C), F32)
    return _pcall(body, name=name, out_shape=(shp, shp, shp), grid=(R // tr,),
                  in_specs=[blk] * 4, out_specs=(blk,) * 3, semantics=("parallel",))(w, g, m, v)


def _blk(h, C, elems=1 << 19, align=16):
    th = _tile(h, max(align, elems // C // align * align), align)
    if th < h or h * C <= 2 * elems:
        return th, C
    return h, _tile(C, max(LANES, elems // h // LANES * LANES))


def _adamw_halves(w, m, v, g_mine, g_other, c_idx, *, name):
    _, h, C = w.shape
    th, tc = _blk(h, C, align=SUBLANES)
    c1 = 1.0 / (1.0 - ADAM_B1 ** ADAM_STEP)
    c2 = 1.0 / (1.0 - ADAM_B2 ** ADAM_STEP)

    def body(c_ref, w_ref, m_ref, v_ref, gm_ref, go_ref, g_ref, d_ref, mo_ref, vo_ref):
        gv = jnp.where(pl.program_id(0) == c_ref[0], gm_ref[...], go_ref[...])
        mn = ADAM_B1 * m_ref[...] + (1.0 - ADAM_B1) * gv
        vn = ADAM_B2 * v_ref[...] + (1.0 - ADAM_B2) * (gv * gv)
        d_ref[...] = -ADAM_LR * ((mn * c1) / (jnp.sqrt(vn * c2) + ADAM_EPS) + ADAM_WD * w_ref[...])
        g_ref[...] = gv
        mo_ref[...] = mn
        vo_ref[...] = vn

    blk = pl.BlockSpec((None, th, tc), lambda s, i, j, c: (s, i, j))

    def pick(mine):
        def index(s, i, j, c):
            use = (s == c[0]) if mine else (s != c[0])
            return jnp.where(use, i, 0), jnp.where(use, j, 0)
        return pl.BlockSpec((th, tc), index)

    shp = jax.ShapeDtypeStruct((2, h, C), F32)
    return _pcall(body, name=name, out_shape=(shp,) * 4, grid=(2, h // th, C // tc), prefetch=1,
                  in_specs=[blk, blk, blk, pick(True), pick(False)], out_specs=(blk,) * 4,
                  semantics=("parallel", "parallel", "parallel"))(c_idx, w, m, v, g_mine, g_other)


def _mesh_pos():
    x, y, c = lax.axis_index("x"), lax.axis_index("y"), lax.axis_index("c")
    others = [(1 - x, y), (x, 1 - y), (1 - x, 1 - y)]
    return x, y, c, others


def _gather_copies(shards, lands, send_sems, recv_sems):
    x, y, c, others = _mesh_pos()
    me = 2 * x + y
    return [pltpu.make_async_remote_copy(
        src_ref=shards[a].at[c], dst_ref=lands[a].at[me, c],
        send_sem=send_sems.at[3 * a + j], recv_sem=recv_sems.at[3 * a + j],
        device_id=(*chip, c), device_id_type=MESH)
        for a in range(len(shards)) for j, chip in enumerate(others)]


def _near_copies(shards, lands, send_sems, recv_sems):
    x, y, c, others = _mesh_pos()
    me = 2 * x + y
    return [pltpu.make_async_remote_copy(
        src_ref=shards[a].at[c], dst_ref=lands[a].at[me, c],
        send_sem=send_sems.at[2 * a + j], recv_sem=recv_sems.at[2 * a + j],
        device_id=(*chip, c), device_id_type=MESH)
        for a in range(len(shards)) for j, chip in enumerate(others[:2])]


def _relay_copies(shards, zones, send_sems, recv_sems):
    x, y, c, others = _mesh_pos()
    (nx, ny), copies = others[:2], []
    for a in range(len(zones)):
        hc = zones[a].shape[-1] // 2
        for k, (src_chip, to, lo) in enumerate(((ny, nx, 0), (nx, ny, hc))):
            part = zones[a].at[2 * src_chip[0] + src_chip[1], c, :, pl.ds(lo, hc)]
            copies.append(pltpu.make_async_remote_copy(
                src_ref=part, dst_ref=part, send_sem=send_sems.at[2 * a + k], recv_sem=recv_sems.at[2 * a + k],
                device_id=(*to, c), device_id_type=MESH))
    return copies


def _pass_copies(shards, zones, send_sems, recv_sems, pieces=(0, 1, 2, 3)):
    x, y, c, others = _mesh_pos()
    me = 2 * x + y
    copies = []
    for a in range(len(shards)):
        srcs = [zones[a].at[2 * chip[0] + chip[1], c] for chip in others] + [shards[a]]
        dsts = [zones[a].at[2 * chip[0] + chip[1], c] for chip in others] + [zones[a].at[me]]
        copies += [pltpu.make_async_remote_copy(
            src_ref=srcs[p], dst_ref=dsts[p], send_sem=send_sems.at[len(pieces) * a + k],
            recv_sem=recv_sems.at[len(pieces) * a + k], device_id=(x, y, 1 - c), device_id_type=MESH)
            for k, p in enumerate(pieces)]
    return copies


def _exchange_copies(grads, recvs, send_sems, recv_sems):
    x, y, c, _ = _mesh_pos()
    return [pltpu.make_async_remote_copy(
        src_ref=grads[a].at[:, 1 - c], dst_ref=recvs[a], send_sem=send_sems.at[a],
        recv_sem=recv_sems.at[a], device_id=(x, y, 1 - c), device_id_type=MESH) for a in range(len(grads))]


def _split_start(copies, per, srcs, zones, after, *, name):
    n = len(srcs)
    HBM = pl.BlockSpec(memory_space=pltpu.HBM)
    SEM = pl.BlockSpec(memory_space=pltpu.SEMAPHORE)

    def body(*refs):
        send_sems, recv_sems = refs[2 * n + 1], refs[2 * n + 2]
        for cp in copies(refs[:n], refs[n:2 * n], send_sems, recv_sems):
            cp.start()
        refs[-1][...] = jnp.zeros_like(refs[-1])

    hbm = lambda a: pltpu.HBM(a.shape, a.dtype)
    res = _pcall(body, name=name,
                 out_shape=(pltpu.SemaphoreType.DMA((per * n,)), pltpu.SemaphoreType.DMA((per * n,)),
                            *[hbm(a) for a in srcs], *[hbm(a) for a in zones],
                            jax.ShapeDtypeStruct((SUBLANES, LANES), F32)),
                 in_specs=[*[HBM] * (2 * n), pl.BlockSpec(memory_space=pl.ANY)],
                 out_specs=(SEM, SEM, *[HBM] * (2 * n), pl.BlockSpec(memory_space=pltpu.VMEM)),
                 aliases={i: 2 + i for i in range(2 * n)}, split_copy=True)(
        *[pltpu.with_memory_space_constraint(a, pltpu.HBM) for a in [*srcs, *zones]], after)
    return res[0], res[1], list(res[2:2 + n]), list(res[2 + n:2 + 2 * n]), res[-1]


def _split_wait(copies, send_sems, recv_sems, srcs, zones, after, *, name):
    n = len(srcs)
    HBM = pl.BlockSpec(memory_space=pltpu.HBM)
    SEM = pl.BlockSpec(memory_space=pltpu.SEMAPHORE)

    def body(*refs):
        for cp in copies(refs[:n], refs[n:2 * n], refs[2 * n], refs[2 * n + 1]):
            cp.wait_send()
            cp.wait_recv()

    hbm = lambda a: pltpu.HBM(a.shape, a.dtype)
    res = _pcall(body, name=name, out_shape=(*[hbm(a) for a in srcs], *[hbm(a) for a in zones]),
                 in_specs=[*[HBM] * (2 * n), SEM, SEM, pl.BlockSpec(memory_space=pl.ANY)],
                 out_specs=tuple([HBM] * (2 * n)), aliases={i: i for i in range(2 * n)},
                 split_copy=True)(*srcs, *zones, send_sems, recv_sems, after)
    return list(res[:n]), list(res[n:])


def _add_halves(grad, recv, c_idx, *, name):
    S, _, h, C = grad.shape
    th, tc = _blk(h, C)

    def body(c_ref, g_ref, r_ref, o_ref):
        o_ref[...] = (g_ref[...].astype(F32) + r_ref[...].astype(F32)).astype(o_ref.dtype)

    return _pcall(body, name=name, out_shape=jax.ShapeDtypeStruct((S, h, C), grad.dtype),
                  grid=(S, h // th, C // tc), prefetch=1,
                  in_specs=[pl.BlockSpec((None, None, th, tc), lambda s, i, j, c: (s, c[0], i, j)),
                            pl.BlockSpec((None, th, tc), lambda s, i, j, c: (s, i, j))],
                  out_specs=pl.BlockSpec((None, th, tc), lambda s, i, j, c: (s, i, j)),
                  semantics=("parallel", "parallel", "parallel"))(c_idx, grad, recv)


def _scatter_copies(srcs, lands, send_sems, recv_sems):
    x, y, c, others = _mesh_pos()
    return [pltpu.make_async_remote_copy(
        src_ref=srcs[a].at[2 * chip[0] + chip[1]], dst_ref=lands[a].at[j],
        send_sem=send_sems.at[3 * a + j], recv_sem=recv_sems.at[3 * a + j],
        device_id=(*chip, c), device_id_type=MESH)
        for a in range(len(srcs)) for j, chip in enumerate(others)]


def _add_chips(sums, recv, chip_idx, *, name):
    _, h, C = sums.shape
    th, tc = _blk(h, C)

    def body(k_ref, s_ref, r_ref, o_ref):
        acc = s_ref[...].astype(F32) + r_ref[0].astype(F32)
        acc = acc + r_ref[1].astype(F32)
        o_ref[...] = acc + r_ref[2].astype(F32)

    return _pcall(body, name=name, out_shape=jax.ShapeDtypeStruct((h, C), F32),
                  grid=(h // th, C // tc), prefetch=1,
                  in_specs=[pl.BlockSpec((None, th, tc), lambda i, j, k: (k[0], i, j)),
                            pl.BlockSpec((3, th, tc), lambda i, j, k: (0, i, j))],
                  out_specs=pl.BlockSpec((th, tc), lambda i, j, k: (i, j)),
                  semantics=("parallel", "parallel"))(chip_idx, sums, recv)


def _swap_copies(halves, others, send_sems, recv_sems):
    x, y, c, _ = _mesh_pos()
    return [pltpu.make_async_remote_copy(
        src_ref=halves[a], dst_ref=others[a], send_sem=send_sems.at[a], recv_sem=recv_sems.at[a],
        device_id=(x, y, 1 - c), device_id_type=MESH) for a in range(len(halves))]


def _all_reduce_small(buf):
    R, L = buf.shape
    NDEV = 8

    def body(x_ref, sum_ref, all_ref, send_sems, recv_sems, local_sem):
        x, y, c, others = _mesh_pos()
        me, sibling = (x, y, c), (x, y, 1 - c)

        def slot(px, py, pc):
            return all_ref.at[4 * px + 2 * py + pc]

        def copy(k, block, to, src=None):
            return pltpu.make_async_remote_copy(
                src_ref=slot(*block) if src is None else src, dst_ref=slot(*block),
                send_sem=send_sems.at[k], recv_sem=recv_sems.at[k], device_id=to, device_id_type=MESH)

        mine = pltpu.make_async_copy(x_ref, slot(*me), local_sem)
        mine.start()
        first = [copy(0, me, sibling, src=x_ref)]
        first += [copy(1 + j, me, (*chip, c), src=x_ref) for j, chip in enumerate(others)]
        for cp in first:
            cp.start()
        passed = [copy(4 + j, (*chip, c), sibling) for j, chip in enumerate(others)]
        for j, chip in enumerate(others):
            copy(1 + j, (*chip, c), me).wait_recv()
            passed[j].start()
        copy(0, sibling, me).wait_recv()
        for j, chip in enumerate(others):
            copy(4 + j, (*chip, 1 - c), me).wait_recv()
        for cp in first + passed:
            cp.wait_send()
        mine.wait()
        acc = all_ref[0]
        for d in range(1, NDEV):
            acc = acc + all_ref[d]
        sum_ref[...] = acc

    VM = pl.BlockSpec(memory_space=pltpu.VMEM)
    return _pcall(body, name="all_reduce_small",
                  out_shape=(jax.ShapeDtypeStruct((R, L), F32), jax.ShapeDtypeStruct((NDEV, R, L), F32)),
                  in_specs=[VM], out_specs=(VM, VM),
                  scratch_shapes=[pltpu.SemaphoreType.DMA((7,)), pltpu.SemaphoreType.DMA((7,)),
                                  pltpu.SemaphoreType.DMA])(buf)[0]


def _pack(arrs, rows_multiple=16):
    flat = [a.reshape(-1).astype(F32) for a in arrs]
    sizes = [f.shape[0] for f in flat]
    total = sum(sizes)
    per = LANES * rows_multiple
    padded = -(-total // per) * per
    flat.append(jnp.zeros((padded - total,), F32))
    offs = [0]
    for s in sizes:
        offs.append(offs[-1] + s)
    return jnp.concatenate(flat).reshape(padded // LANES, LANES), offs


def _unpack(buf, offs, shapes):
    flat = buf.reshape(-1)
    return [flat[offs[i]:offs[i + 1]].reshape(s) for i, s in enumerate(shapes)]


def kernel(x, mem, g_mix, w_in, w_a2, b_a, g_gla, w_pool, pool_scale, w_branch, w_out, g_cross, g_mem, w_cq, w_ckv, w_co, g_ffn, w_up, conv_w, conv_b, w_down, g_final, loss_target, m_g_mix, m_w_in, m_w_a2, m_b_a, m_g_gla, m_w_pool, m_pool_scale, m_w_branch, m_w_out, m_g_cross, m_g_mem, m_w_cq, m_w_ckv, m_w_co, m_g_ffn, m_w_up, m_conv_w, m_conv_b, m_w_down, m_g_final, v_g_mix, v_w_in, v_w_a2, v_b_a, v_g_gla, v_w_pool, v_pool_scale, v_w_branch, v_w_out, v_g_cross, v_g_mem, v_w_cq, v_w_ckv, v_w_co, v_g_ffn, v_w_up, v_conv_w, v_conv_b, v_w_down, v_g_final):
    weights = dict(g_mix=g_mix, w_in=w_in, w_a2=w_a2, b_a=b_a, g_gla=g_gla, w_pool=w_pool,
                   pool_scale=pool_scale, w_branch=w_branch, w_out=w_out, g_cross=g_cross, g_mem=g_mem,
                   w_cq=w_cq, w_ckv=w_ckv, w_co=w_co, g_ffn=g_ffn, w_up=w_up, conv_w=conv_w,
                   conv_b=conv_b, w_down=w_down, g_final=g_final)
    mom_m = dict(g_mix=m_g_mix, w_in=m_w_in, w_a2=m_w_a2, b_a=m_b_a, g_gla=m_g_gla, w_pool=m_w_pool,
                 pool_scale=m_pool_scale, w_branch=m_w_branch, w_out=m_w_out, g_cross=m_g_cross,
                 g_mem=m_g_mem, w_cq=m_w_cq, w_ckv=m_w_ckv, w_co=m_w_co, g_ffn=m_g_ffn, w_up=m_w_up,
                 conv_w=m_conv_w, conv_b=m_conv_b, w_down=m_w_down, g_final=m_g_final)
    mom_v = dict(g_mix=v_g_mix, w_in=v_w_in, w_a2=v_w_a2, b_a=v_b_a, g_gla=v_g_gla, w_pool=v_w_pool,
                 pool_scale=v_pool_scale, w_branch=v_w_branch, w_out=v_w_out, g_cross=v_g_cross,
                 g_mem=v_g_mem, w_cq=v_w_cq, w_ckv=v_w_ckv, w_co=v_w_co, g_ffn=v_g_ffn, w_up=v_w_up,
                 conv_w=v_conv_w, conv_b=v_conv_b, w_down=v_w_down, g_final=v_g_final)
    order = list(weights)
    big = ["w_in", "w_branch", "w_out", "w_cq", "w_ckv", "w_co", "w_up", "w_down"]
    small_sharded = ["w_a2", "w_pool", "conv_w"]
    small_repl = ["g_mix", "b_a", "g_gla", "pool_scale", "g_cross", "g_mem", "g_ffn", "conv_b", "g_final"]

    xs, ms, tgt = x[0], mem[0], loss_target[0]
    T, D = xs.shape
    M = ms.shape[0]
    DK, DV, PW = b_a.shape[1], g_gla.shape[1], pool_scale.shape[1]
    RANK = w_a2.shape[1]
    F2 = conv_b.shape[1]
    F = F2 // 2
    DIN = N_CHIPS * w_in.shape[2]
    OFF_A = 2 * DK + 2 * DV
    OFF_P = OFF_A + RANK
    RP = LANES
    GW = PW // POOL_GROUPS
    assert PW == DV and 4 * DV == 2 * D and OFF_P + PW + 2 * D == DIN

    cx, cy, cc = lax.axis_index("x"), lax.axis_index("y"), lax.axis_index("c")
    chip = 2 * cx + cy
    c_idx = jnp.reshape(cc, (1,)).astype(jnp.int32)
    chip_idx = jnp.reshape(chip, (1,)).astype(jnp.int32)

    def halves(a):
        return a.reshape(2, a.shape[0] // 2, a.shape[1])

    shard2d = {k: (weights[k][0].T if k == "w_in" else weights[k][0]) for k in big}
    small_pack, small_offs = _pack([weights[k][0] for k in small_sharded], rows_multiple=32)
    flying, passing = {}, {}

    def gather_start(group, keys, tok):
        srcs = [small_pack if k == "small" else shard2d[k].astype(BF16) for k in keys]
        if group != "in":
            srcs = [a + tok[0:1, 0:1].astype(a.dtype) for a in srcs]
        srcs = [halves(a) for a in srcs]
        zones = [lax.empty((N_CHIPS, *s.shape), s.dtype) for s in srcs]
        first = (_near_copies, 2) if group == "in" else (_gather_copies, 3)
        s_sems, r_sems, srcs, zones, tok = _split_start(*first, srcs, zones, tok, name=f"gather_start_{group}")
        flying[group] = (keys, s_sems, r_sems, srcs, zones)
        return tok

    tok = gather_start("in", ["w_in"], xs)

    def arrive_in(after):
        keys, s_sems, r_sems, srcs, zones = flying["in"]
        near, diag = functools.partial(_pass_copies, pieces=(0, 1, 3)), functools.partial(_pass_copies, pieces=(2,))
        srcs, zones = _split_wait(_near_copies, s_sems, r_sems, srcs, zones, after, name="gather_wait_in")
        rs, rr, srcs, zones, tok = _split_start(_relay_copies, 2, srcs, zones, after, name="gather_relay_start_in")
        ns, nr, srcs, zones, tok = _split_start(near, 3, srcs, zones, tok, name="gather_pass_near_start_in")
        for group, group_keys in (("mix", ["w_branch", "w_out", "small"]), ("cross", ["w_cq", "w_ckv", "w_co"]),
                                  ("up", ["w_up"]), ("down", ["w_down"])):
            tok = gather_start(group, group_keys, tok)
        after = tok
        srcs, zones = _split_wait(_relay_copies, rs, rr, srcs, zones, after, name="gather_relay_wait_in")
        ds, dr, srcs, zones, _ = _split_start(diag, 1, srcs, zones, after, name="gather_pass_diag_start_in")
        srcs, zones = _split_wait(near, ns, nr, srcs, zones, after, name="gather_pass_near_wait_in")
        _, full = _split_wait(diag, ds, dr, srcs, zones, after, name="gather_pass_diag_wait_in")
        return {k: f.reshape(N_CHIPS, f.shape[1] * f.shape[2], f.shape[3]) for k, f in zip(keys, full)}

    def landed(group, after):
        keys, s_sems, r_sems, srcs, zones = flying[group]
        srcs, zones = _split_wait(_gather_copies, s_sems, r_sems, srcs, zones, after,
                                  name=f"gather_wait_{group}")
        s_sems, r_sems, srcs, zones, token = _split_start(_pass_copies, 4, srcs, zones, after,
                                                          name=f"gather_pass_start_{group}")
        passing[group] = (keys, s_sems, r_sems, srcs, zones)
        return token

    def arrive(group, after):
        keys, s_sems, r_sems, srcs, zones = passing[group]
        _, full = _split_wait(_pass_copies, s_sems, r_sems, srcs, zones, after,
                              name=f"gather_pass_wait_{group}")
        return {k: f.reshape(N_CHIPS, f.shape[1] * f.shape[2], f.shape[3]) for k, f in zip(keys, full)}

    def rows(g):
        return g.reshape(-1, g.shape[2])

    h1, r1 = _rms_fwd(xs, g_mix + tok[0:1, 0:1], name="norm_mix")
    W_in = rows(arrive_in(h1)["w_in"])
    W_main = jnp.concatenate([W_in[:OFF_A], W_in[OFF_P:]], axis=0)
    W_a = jnp.pad(W_in[OFF_A:OFF_P], ((0, RP - RANK), (0, 0)))
    tok = landed("mix", W_a)
    proj = _mm(h1, W_main, "nt", name="proj_main", out_dtype=F32, after=tok)
    gw = arrive("mix", proj)
    W_branch, W_out, small_all = rows(gw["w_branch"]), rows(gw["w_out"]), gw["small"]
    sm = [_unpack(small_all[j], small_offs, [weights[k].shape[1:] for k in small_sharded]) for j in range(N_CHIPS)]
    W_a2 = jnp.concatenate([sm[j][0] for j in range(N_CHIPS)], axis=1)
    W_a2p = jnp.pad(W_a2, ((0, RP - RANK), (0, 0))).astype(BF16)
    W_pool = jnp.concatenate([sm[j][1] for j in range(N_CHIPS)], axis=1).astype(BF16)
    W_conv = jnp.concatenate([sm[j][2] for j in range(N_CHIPS)], axis=1)

    a_pad = _mm(h1, W_a, "nt", name="proj_gate_rank", out_dtype=F32)
    o_gla, o_raw, states = _gla_fwd(proj, a_pad, W_a2p, b_a, g_gla, T=T, DK=DK, DV=DV)
    o_pool = _pool_fwd(proj, W_pool, pool_scale, T=T, PW=PW, col_block=3)
    tok = landed("cross", o_pool)
    y_gla = _mm(o_gla, W_branch, "nn", name="branch_gla", out_dtype=BF16, K=DV, after=tok)
    y_pool = _mm(o_pool, W_branch, "nn", name="branch_pool", out_dtype=BF16, K=PW, b_off=(DV, 0))
    merged = _merge_fwd(y_gla, y_pool, proj, T=T, D=D, col_block=2)
    x1 = _mm(merged, W_out, "nn", name="mix_out", out_dtype=F32, add=xs)

    h2, r2 = _rms_fwd(x1, g_cross, name="norm_cross")
    mem_n, rm = _rms_fwd(ms, g_mem, name="norm_mem")
    gw = arrive("cross", h2)
    W_cq, W_ckv, W_co = rows(gw["w_cq"]), gw["w_ckv"], rows(gw["w_co"])
    qc = _mm(h2, W_cq, "nn", name="cross_q", out_dtype=BF16)
    kv = _mm(mem_n, W_ckv, "nn", name="cross_kv", out_dtype=BF16, b_blocked=True)
    o_att = _attn_fwd(qc, kv, T=T, D=D, M=M)
    x2 = _mm(o_att, W_co, "nn", name="cross_out", out_dtype=F32, add=x1)

    tok = landed("up", x2)
    h3, r3 = _rms_fwd(x2, g_ffn + tok[0:1, 0:1], name="norm_ffn")
    W_up = arrive("up", h3)["w_up"]
    u0 = _mm(h3, W_up, "nn", name="ffn_up", out_dtype=F32, b_blocked=True)
    tok = landed("down", u0)
    f_act = _conv_fwd(u0, W_conv, conv_b + tok[0:1, 0:1], T=T, F=F)
    W_down = rows(arrive("down", f_act)["w_down"])
    x3 =_mm(f_act, W_down, "nn", name="ffn_down", out_dtype=F32, add=x2)

    loss_part, dx3, dx3_b, dg_final = _loss_head(x3, g_final.reshape(1, D), tgt)

    def col_shards(g):
        nb, K, Nb = g.shape
        return g.reshape(nb, 2, K // 2, Nb)

    def row_shards(g):
        R, N = g.shape
        return g.reshape(N_CHIPS, 2, R // N_CHIPS // 2, N)

    exchanging, in_flight = {}, []

    def exchange_start(group, keys, partials, after):
        recvs = [lax.empty((p.shape[0], *p.shape[2:]), p.dtype) for p in partials]
        s_sems, r_sems, partials, recvs, token = _split_start(
            _exchange_copies, 1, partials, recvs, after, name=f"grad_exchange_start_{group}")
        exchanging[group] = (keys, s_sems, r_sems, partials, recvs)
        return token

    def scatter_start(group, after):
        keys, s_sems, r_sems, partials, recvs = exchanging[group]
        partials, recvs = _split_wait(_exchange_copies, s_sems, r_sems, partials, recvs, after,
                                      name=f"grad_exchange_wait_{group}")
        chip_sums = [_add_halves(p, r, c_idx, name=f"grad_add_halves_{k}")
                     for k, p, r in zip(keys, partials, recvs)]
        lands = [lax.empty((3, *s.shape[1:]), s.dtype) for s in chip_sums]
        s_sems, r_sems, sums, lands, token = _split_start(
            _scatter_copies, 3, chip_sums, lands, after, name=f"grad_scatter_start_{group}")
        in_flight.append((group, keys, s_sems, r_sems, sums, lands))
        return token

    collected = []

    def collect(after):
        group, keys, s_sems, r_sems, sums, lands = in_flight.pop(0)
        sums, from_chips = _split_wait(_scatter_copies, s_sems, r_sems, sums, lands, after,
                                       name=f"grad_scatter_wait_{group}")
        half_sums = [_add_chips(s, r, chip_idx, name=f"grad_add_chips_{k}") for k, s, r in zip(keys, sums, from_chips)]
        others = [lax.empty(h.shape, h.dtype) for h in half_sums]
        s_sems, r_sems, half_sums, others, token = _split_start(
            _swap_copies, 1, half_sums, others, after, name=f"grad_swap_start_{group}")
        collected.append((keys, s_sems, r_sems, half_sums, others))
        return token

    df = _mm(dx3_b, W_down, "nt", name="d_ffn_act", out_dtype=BF16)
    dW_down = _mm(f_act, dx3_b, "tn", name="dw_down", out_dtype=BF16)
    du0, dconv_w, dconv_b = _conv_bwd(u0, W_conv, conv_b, df, T=T, F=F)
    dh3 = _mm(du0, W_up, "nt", name="d_ffn_in", out_dtype=F32, b_blocked=True, tk=F2 // N_CHIPS)
    dW_up = _mm(h3, du0, "tn", name="dw_up", out_dtype=BF16, out_blocks=N_CHIPS)
    tok = exchange_start("ffn", ["w_down", "w_up"], [row_shards(dW_down), col_shards(dW_up)], dh3)
    dx2, dx2_b, dg_ffn = _rms_bwd(dh3, x2, r3 + tok[0:1, 0:1], g_ffn, dx3, name="norm_ffn_bwd")

    do_att = _mm(dx2_b, W_co, "nt", name="d_cross_o", out_dtype=BF16)
    dW_co = _mm(o_att, dx2_b, "tn", name="dw_co", out_dtype=BF16)
    tok = scatter_start("ffn", dW_co)
    dq, dkv = _attn_bwd(qc, kv, do_att, T=T, D=D, M=M)
    dkv_b = dkv.astype(BF16)
    dW_cq = _mm(h2, dq, "tn", name="dw_cq", out_dtype=BF16, after=tok)
    dh2 = _mm(dq, W_cq, "nt", name="d_cross_in", out_dtype=F32)
    dW_ckv = _mm(mem_n, dkv_b, "tn", name="dw_ckv", out_dtype=BF16, out_blocks=N_CHIPS)
    dmem_n = _mm(dkv_b, W_ckv, "nt", name="d_mem", out_dtype=F32, b_blocked=True)
    tok = exchange_start("cross", ["w_co", "w_cq", "w_ckv"],
                         [row_shards(dW_co), row_shards(dW_cq), col_shards(dW_ckv)], dmem_n)
    _, _, dg_mem = _rms_bwd(dmem_n, ms, rm, g_mem, None, name="norm_mem_bwd")
    dx1, dx1_b, dg_cross = _rms_bwd(dh2, x1, r2 + tok[0:1, 0:1], g_cross, dx2, name="norm_cross_bwd")

    dmerged = _mm(dx1_b, W_out, "nt", name="d_merged", out_dtype=BF16)
    dW_out = _mm(merged, dx1_b, "tn", name="dw_out", out_dtype=BF16)
    tok = scatter_start("cross", dW_out)
    dy_gla, dy_pool, dgates = _merge_bwd(dmerged, y_gla, y_pool, proj, T=T, D=D, col_block=2)
    dW_br_gla = _mm(o_gla, dy_gla, "tn", name="dw_branch_gla", out_dtype=BF16, after=tok)
    dW_br_pool = _mm(o_pool, dy_pool, "tn", name="dw_branch_pool", out_dtype=BF16)
    do_gla = _mm(dy_gla, W_branch, "nt", name="d_o_gla", out_dtype=F32, N=DV)
    do_pool = _mm(dy_pool, W_branch, "nt", name="d_o_pool", out_dtype=F32, N=PW, b_off=(DV, 0))
    dp, dw_pool, dpool_scale = _pool_bwd(proj, W_pool, pool_scale, do_pool, T=T, PW=PW, col_block=3)
    dW_pool = jnp.transpose(dw_pool.reshape(POOL_GROUPS, N_CHIPS, GW // N_CHIPS, GW), (1, 0, 2, 3))
    tok = exchange_start("mix", ["w_out", "w_branch", "w_pool"],
                         [row_shards(dW_out), row_shards(jnp.concatenate([dW_br_gla, dW_br_pool], axis=0)),
                          row_shards(dW_pool.reshape(N_CHIPS * POOL_GROUPS * (GW // N_CHIPS), GW).astype(BF16))],
                         dp)
    dqkvr, da_pad, dw2, db_a, dg_gla = _gla_bwd(proj, a_pad, W_a2p, b_a + tok[0:1, 0:1], g_gla, o_raw, states,
                                               do_gla, T=T, DK=DK, DV=DV)
    tok = scatter_start("mix", dqkvr)
    dproj = jnp.concatenate([dqkvr, dp, dgates], axis=1)
    dW_main = _mm(dproj, h1, "tn", name="dw_in_main", out_dtype=BF16, after=tok)
    dW_a = _mm(da_pad, h1, "tn", name="dw_in_rank", out_dtype=BF16)
    dW_in = jnp.concatenate([dW_main[:OFF_A], dW_a[:RANK], dW_main[OFF_A:]], axis=0)
    tok = exchange_start("in", ["w_in"], [row_shards(dW_in)], dW_a)
    dh1 = _mm(dproj, W_main, "nn", name="d_mix_in_main", out_dtype=F32, after=tok)
    dh1 = _mm(da_pad, W_a, "nn", name="d_mix_in_rank", out_dtype=F32, add=dh1)
    dx0, _, dg_mix = _rms_bwd(dh1, xs, r1, g_mix, dx1, name="norm_mix_bwd")

    grads = {}

    small_grads = [loss_part, dg_mix, db_a, dg_gla, dpool_scale, dg_cross, dg_mem, dg_ffn, dconv_b, dg_final,
                   dw2[:RANK], dconv_w]
    small_buf, offs = _pack(small_grads)
    small_sum = _all_reduce_small(small_buf)
    red = _unpack(small_sum, offs, [g.shape for g in small_grads])
    loss = red[0][0, 0]
    for k, g in zip(small_repl, red[1:10]):
        grads[k] = g.reshape(weights[k].shape)
    nb = DK // N_CHIPS
    grads["w_a2"] = lax.dynamic_slice_in_dim(red[10], chip * nb, nb, axis=1)[None]
    nb = F2 // N_CHIPS
    grads["conv_w"] = lax.dynamic_slice_in_dim(red[11], chip * nb, nb, axis=1)[None]

    delta, new_m, new_v = {}, {}, {}

    def shard_rows(k, a):
        a = a[0]
        return a.T if k == "w_in" else a.reshape(-1, a.shape[-1])

    def whole(k, a):
        a = a.reshape(-1, a.shape[2])
        return (a.T if k == "w_in" else a).reshape(weights[k].shape)

    scatter_start("in", small_sum)

    def finish(after):
        keys, s_sems, r_sems, mine, others = collected.pop(0)
        mine, others = _split_wait(_swap_copies, s_sems, r_sems, mine, others, after,
                                   name=f"grad_swap_wait_{keys[0]}")
        for k, g_mine, g_other in zip(keys, mine, others):
            wmv = [halves(shard_rows(k, src[k])) for src in (weights, mom_m, mom_v)]
            res = _adamw_halves(*wmv, g_mine, g_other, c_idx, name=f"adamw_{k}")
            grads[k], delta[k], new_m[k], new_v[k] = (whole(k, a) for a in res)
        return res[1]

    after = in_flight[-1][4][0]
    while in_flight:
        after = collect(after)
        while len(collected) > 1:
            after = finish(after)
    finish(after)
    small = small_repl + ["w_a2", "conv_w"]
    packs = [_pack([src[k] for k in small])[0] for src in (weights, grads, mom_m, mom_v)]
    _, offs = _pack([weights[k] for k in small])
    outs = _adamw(*packs, name="adamw_small")
    for res, o in zip((delta, new_m, new_v), outs):
        for k, a in zip(small, _unpack(o, offs, [weights[k].shape for k in small])):
            res[k] = a

    return (loss, dx0[None], *[grads[k] for k in order], *[delta[k] for k in order],
            *[new_m[k] for k in order], *[new_v[k] for k in order])
```

```python
import functools

import jax
import jax.numpy as jnp
from jax import lax
from jax.experimental import pallas as pl
from jax.experimental.pallas import tpu as pltpu

F32 = jnp.float32
BF16 = jnp.bfloat16
MESH = pl.DeviceIdType.MESH
HIGHEST = lax.Precision.HIGHEST

EPS = 1e-6
GLA_HEADS = 4
GLA_CHUNK = 128
GLA_GATE_NORM = 16.0
POOL_GROUPS = 4
CROSS_HEADS = 4
CONV_W = 3
N_CHIPS = 4
LANES = 128
SUBLANES = 8
VMEM_LIMIT = 56 << 20

ADAM_LR = 0.001
ADAM_B1 = 0.9
ADAM_B2 = 0.999
ADAM_EPS = 1e-08
ADAM_WD = 0.01
ADAM_STEP = 10

NN = (((1,), (0,)), ((), ()))
NT = (((1,), (1,)), ((), ()))
TN = (((0,), (0,)), ((), ()))


ONE_PASS = lax.Precision.HIGH


def _dot(a, b, dn=NN, precision=None):
    return lax.dot_general(a, b, dn, precision=precision, preferred_element_type=F32)


def _tile(n, pref, align=LANES):
    t = (min(pref, n) // align) * align
    while t >= align:
        if n % t == 0:
            return t
        t -= align
    return n


def _pcall(body, *, name, out_shape, grid=(), in_specs=None, out_specs=None, scratch_shapes=(),
           semantics=None, prefetch=0, aliases=None, split_copy=False):
    params = dict(vmem_limit_bytes=VMEM_LIMIT)
    if semantics is not None:
        params["dimension_semantics"] = semantics
    if split_copy:
        params["has_side_effects"] = pltpu.SideEffectType.DATAFLOW_SIDE_EFFECTING
    if prefetch:
        grid_spec = pltpu.PrefetchScalarGridSpec(
            num_scalar_prefetch=prefetch, grid=grid, in_specs=in_specs, out_specs=out_specs,
            scratch_shapes=scratch_shapes)
        return pl.pallas_call(body, name=name, out_shape=out_shape, grid_spec=grid_spec,
                              compiler_params=pltpu.CompilerParams(**params))
    kw = {}
    if aliases is not None:
        kw["input_output_aliases"] = aliases
    if in_specs is not None:
        kw["in_specs"] = in_specs
    if out_specs is not None:
        kw["out_specs"] = out_specs
    return pl.pallas_call(body, name=name, out_shape=out_shape, grid=grid,
                          scratch_shapes=scratch_shapes,
                          compiler_params=pltpu.CompilerParams(**params), **kw)


def _sigmoid(x):
    return 1.0 / (1.0 + jnp.exp(-x))


def _log_sigmoid(x):
    return jnp.minimum(x, 0.0) - jnp.log(1.0 + jnp.exp(-jnp.abs(x)))


def _mm(a, b, mode, *, name, out_dtype, M=None, N=None, K=None, a_off=(0, 0), b_off=(0, 0),
        add=None, b_blocked=False, out_blocks=0, after=None, tm=1536, tn=1536, tk=2048):
    if b_blocked:
        nb, R, Cb = b.shape
        b_rows, b_cols = R, nb * Cb
    else:
        b_rows, b_cols = b.shape
    if mode == "nn":
        M = M or a.shape[0]; K = K or a.shape[1]; N = N or b_cols
    elif mode == "nt":
        M = M or a.shape[0]; K = K or a.shape[1]; N = N or b_rows
    else:
        K = K or a.shape[0]; M = M or a.shape[1]; N = N or b_cols
    tm = _tile(M, tm, LANES if mode == "tn" else 16)
    tn = _tile(Cb if (b_blocked and mode != "nt") else (N // out_blocks if out_blocks else N), tn)
    tk = _tile(Cb if (b_blocked and mode == "nt") else K, tk)
    nk = K // tk
    dn = {"nn": NN, "nt": NT, "tn": TN}[mode]

    def off(o, t):
        assert o % t == 0, (name, o, t)
        return o // t

    if mode == "tn":
        ar, ac = off(a_off[0], tk), off(a_off[1], tm)
        a_spec = pl.BlockSpec((tk, tm), lambda i, j, k: (k + ar, i + ac))
    else:
        ar, ac = off(a_off[0], tm), off(a_off[1], tk)
        a_spec = pl.BlockSpec((tm, tk), lambda i, j, k: (i + ar, k + ac))
    if b_blocked and mode == "nt":
        per = Cb // tk
        b_spec = pl.BlockSpec((None, tn, tk), lambda i, j, k: (k // per, j, k % per))
    elif b_blocked:
        per = Cb // tn
        b_spec = pl.BlockSpec((None, tk, tn), lambda i, j, k: (j // per, k, j % per))
    elif mode == "nt":
        br, bc = off(b_off[0], tn), off(b_off[1], tk)
        b_spec = pl.BlockSpec((tn, tk), lambda i, j, k: (j + br, k + bc))
    else:
        br, bc = off(b_off[0], tk), off(b_off[1], tn)
        b_spec = pl.BlockSpec((tk, tn), lambda i, j, k: (k + br, j + bc))
    if out_blocks:
        per_o = N // out_blocks // tn
        o_spec = pl.BlockSpec((None, tm, tn), lambda i, j, k: (j // per_o, i, j % per_o))
        out_shape = jax.ShapeDtypeStruct((out_blocks, M, N // out_blocks), out_dtype)
    else:
        o_spec = pl.BlockSpec((tm, tn), lambda i, j, k: (i, j))
        out_shape = jax.ShapeDtypeStruct((M, N), out_dtype)
    in_specs = [a_spec, b_spec]
    args = [a, b]
    if add is not None:
        assert not out_blocks
        in_specs.append(o_spec)
        args.append(add)
    if after is not None:
        in_specs.append(pl.BlockSpec(memory_space=pl.ANY))
        args.append(after)
    n_in = len(args)

    def finish(r, refs):
        if add is not None:
            r = r + refs[2][...]
        o_ref = refs[n_in]
        o_ref[...] = r.astype(o_ref.dtype)

    def body_one(*refs):
        finish(_dot(refs[0][...].astype(BF16), refs[1][...].astype(BF16), dn), refs)

    def body_acc(*refs):
        acc_ref = refs[-1]
        k = pl.program_id(2)

        @pl.when(k == 0)
        def _():
            acc_ref[...] = jnp.zeros_like(acc_ref)

        acc_ref[...] += _dot(refs[0][...].astype(BF16), refs[1][...].astype(BF16), dn)

        @pl.when(k == nk - 1)
        def _():
            finish(acc_ref[...], refs)

    return _pcall(body_one if nk == 1 else body_acc, name=name, out_shape=out_shape,
                  grid=(M // tm, N // tn, nk), in_specs=in_specs, out_specs=o_spec,
                  scratch_shapes=[] if nk == 1 else [pltpu.VMEM((tm, tn), F32)],
                  semantics=("parallel", "parallel", "arbitrary"))(*args)


def _rms_fwd(x, g, *, name):
    T, D = x.shape
    tr = _tile(T, 256, 16)

    def body(x_ref, g_ref, h_ref, r_ref):
        xv = x_ref[...]
        r = lax.rsqrt(jnp.mean(xv * xv, axis=-1, keepdims=True) + EPS)
        h_ref[...] = (xv * r * g_ref[...]).astype(h_ref.dtype)
        r_ref[...] = r

    row = pl.BlockSpec((tr, D), lambda i: (i, 0))
    return _pcall(body, name=name,
                  out_shape=(jax.ShapeDtypeStruct((T, D), BF16), jax.ShapeDtypeStruct((T, 1), F32)),
                  grid=(T // tr,),
                  in_specs=[row, pl.BlockSpec((1, D), lambda i: (0, 0))],
                  out_specs=(row, pl.BlockSpec((tr, 1), lambda i: (i, 0))),
                  semantics=("parallel",))(x, g)


def _rms_bwd(dh, x, rstd, g, dres, *, name):
    T, D = x.shape
    tr = _tile(T, 256, 16)
    has_res = dres is not None

    def body(*refs):
        if has_res:
            dh_ref, x_ref, r_ref, g_ref, res_ref, dx_ref, dxb_ref, dg_ref = refs
        else:
            dh_ref, x_ref, r_ref, g_ref, dx_ref, dxb_ref, dg_ref = refs
        r = r_ref[...]
        xh = x_ref[...] * r
        dhv = dh_ref[...].astype(F32)
        dxh = dhv * g_ref[...]
        m = jnp.mean(dxh * xh, axis=-1, keepdims=True)
        dx = r * (dxh - xh * m)
        if has_res:
            dx = dx + res_ref[...]
        dx_ref[...] = dx
        dxb_ref[...] = dx.astype(BF16)

        @pl.when(pl.program_id(0) == 0)
        def _():
            dg_ref[...] = jnp.zeros_like(dg_ref)

        dg_ref[...] += jnp.sum(dhv * xh, axis=0, keepdims=True)

    row = pl.BlockSpec((tr, D), lambda i: (i, 0))
    vec = pl.BlockSpec((1, D), lambda i: (0, 0))
    in_specs = [row, row, pl.BlockSpec((tr, 1), lambda i: (i, 0)), vec]
    args = [dh, x, rstd, g]
    if has_res:
        in_specs.append(row)
        args.append(dres)
    return _pcall(body, name=name,
                  out_shape=(jax.ShapeDtypeStruct((T, D), F32), jax.ShapeDtypeStruct((T, D), BF16),
                             jax.ShapeDtypeStruct((1, D), F32)),
                  grid=(T // tr,), in_specs=in_specs, out_specs=(row, row, vec),
                  semantics=("arbitrary",))(*args)


def _loss_head(x3, g, tgt):
    T, D = x3.shape
    tr = _tile(T, 256, 16)

    def body(x_ref, g_ref, t_ref, loss_ref, dx_ref, dxb_ref, dg_ref):
        xv = x_ref[...]
        gv = g_ref[...]
        r = lax.rsqrt(jnp.mean(xv * xv, axis=-1, keepdims=True) + EPS)
        xh = xv * r
        err = xh * gv - t_ref[...]
        dy = err * (1.0 / D)
        dxh = dy * gv
        m = jnp.mean(dxh * xh, axis=-1, keepdims=True)
        dx = r * (dxh - xh * m)
        dx_ref[...] = dx
        dxb_ref[...] = dx.astype(BF16)

        @pl.when(pl.program_id(0) == 0)
        def _():
            dg_ref[...] = jnp.zeros_like(dg_ref)
            loss_ref[...] = jnp.zeros_like(loss_ref)

        dg_ref[...] += jnp.sum(dy * xh, axis=0, keepdims=True)
        part = 0.5 * jnp.sum(jnp.mean(err * err, axis=-1, keepdims=True), axis=0, keepdims=True)
        loss_ref[...] += jnp.broadcast_to(part, loss_ref.shape)

    row = pl.BlockSpec((tr, D), lambda i: (i, 0))
    vec = pl.BlockSpec((1, D), lambda i: (0, 0))
    return _pcall(body, name="loss_head",
                  out_shape=(jax.ShapeDtypeStruct((1, LANES), F32), jax.ShapeDtypeStruct((T, D), F32),
                             jax.ShapeDtypeStruct((T, D), BF16), jax.ShapeDtypeStruct((1, D), F32)),
                  grid=(T // tr,), in_specs=[row, vec, row],
                  out_specs=(pl.BlockSpec((1, LANES), lambda i: (0, 0)), row, row, vec),
                  semantics=("arbitrary",))(x3, g, tgt)


def _gla_chunk_terms(qk, a_ref, w2_ref, ba_ref, DK):
    C = qk.shape[0]
    gp = _dot(a_ref[...].astype(BF16), w2_ref[...]) + ba_ref[...]
    la = _log_sigmoid(gp) * (1.0 / GLA_GATE_NORM)
    row = lax.broadcasted_iota(jnp.int32, (C, C), 0)
    col = lax.broadcasted_iota(jnp.int32, (C, C), 1)
    causal = row >= col
    b = _dot(causal.astype(F32), la, precision=HIGHEST)
    return gp, b, causal


def _gla_fwd(proj, a_pad, w2, b_a, g_gla, *, T, DK, DV):
    assert 2 * DK == DV
    H = GLA_HEADS
    HK, HV = DK // H, DV // H
    C = GLA_CHUNK
    n = T // C
    RP = a_pad.shape[1]
    scale = HK ** -0.5

    def body(qk_ref, v_ref, r_ref, a_ref, w2_ref, ba_ref, gg_ref, og_ref, oraw_ref, st_ref, s_ref):
        @pl.when(pl.program_id(0) == 0)
        def _():
            s_ref[...] = jnp.zeros_like(s_ref)

        st_ref[...] = s_ref[...]
        qk = qk_ref[...]
        _, b, causal = _gla_chunk_terms(qk, a_ref, w2_ref, ba_ref, DK)
        for h in range(H):
            ks = slice(h * HK, (h + 1) * HK)
            vs = slice(h * HV, (h + 1) * HV)
            bh = b[:, ks]
            b_last = bh[C - 1:C, :]
            qt = qk[:, ks] * scale * jnp.exp(bh)
            kh = qk[:, DK + h * HK:DK + (h + 1) * HK]
            kt = kh * jnp.exp(-bh)
            khat = kh * jnp.exp(b_last - bh)
            a_mat = jnp.where(causal, _dot(qt, kt, NT, ONE_PASS), 0.0)
            vh = v_ref[:, vs]
            s_t = s_ref[h]
            o = _dot(a_mat, vh, NN, ONE_PASS) + _dot(qt, s_t, NT, ONE_PASS)
            s_ref[h] = s_t * jnp.exp(b_last) + _dot(vh, khat, TN, ONE_PASS)
            rs = lax.rsqrt(jnp.mean(o * o, axis=-1, keepdims=True) + EPS)
            rr = r_ref[:, vs]
            og = o * rs * gg_ref[:, vs] * (rr * _sigmoid(rr))
            oraw_ref[:, vs] = o
            og_ref[:, vs] = og.astype(BF16)

    blk = lambda j: pl.BlockSpec((C, DV), lambda i: (i, j))
    full = lambda s: pl.BlockSpec(s, lambda i: (0,) * len(s))
    return _pcall(
        body, name="gla_fwd",
        out_shape=(jax.ShapeDtypeStruct((T, DV), BF16), jax.ShapeDtypeStruct((T, DV), F32),
                   jax.ShapeDtypeStruct((n, H, HV, HK), F32)),
        grid=(n,),
        in_specs=[blk(0), blk(1), blk(2), pl.BlockSpec((C, RP), lambda i: (i, 0)),
                  full((RP, DK)), full((1, DK)), full((1, DV))],
        out_specs=(blk(0), blk(0), pl.BlockSpec((None, H, HV, HK), lambda i: (i, 0, 0, 0))),
        scratch_shapes=[pltpu.VMEM((H, HV, HK), F32)],
        semantics=("arbitrary",))(proj, proj, proj, a_pad, w2, b_a, g_gla)


def _gla_bwd(proj, a_pad, w2, b_a, g_gla, o_raw, states, do_gla, *, T, DK, DV):
    H = GLA_HEADS
    HK, HV = DK // H, DV // H
    C = GLA_CHUNK
    n = T // C
    RP = a_pad.shape[1]
    scale = HK ** -0.5

    def body(qk_ref, v_ref, r_ref, a_ref, w2_ref, ba_ref, gg_ref, oraw_ref, st_ref, dog_ref,
             dqkvr_ref, da_ref, dw2_ref, dba_ref, dgg_ref, ds_ref):
        @pl.when(pl.program_id(0) == 0)
        def _():
            ds_ref[...] = jnp.zeros_like(ds_ref)
            dw2_ref[...] = jnp.zeros_like(dw2_ref)
            dba_ref[...] = jnp.zeros_like(dba_ref)
            dgg_ref[...] = jnp.zeros_like(dgg_ref)

        qk = qk_ref[...]
        gp, b, causal = _gla_chunk_terms(qk, a_ref, w2_ref, ba_ref, DK)
        row = lax.broadcasted_iota(jnp.int32, (C, C), 0)
        col = lax.broadcasted_iota(jnp.int32, (C, C), 1)
        upper = (col >= row).astype(F32)
        dla_parts = []
        for h in range(H):
            ks = slice(h * HK, (h + 1) * HK)
            vs = slice(h * HV, (h + 1) * HV)
            bh = b[:, ks]
            b_last = bh[C - 1:C, :]
            eb = jnp.exp(bh)
            emb = jnp.exp(-bh)
            ehat = jnp.exp(b_last - bh)
            e_last = jnp.exp(b_last)
            qt = qk[:, ks] * scale * eb
            kh = qk[:, DK + h * HK:DK + (h + 1) * HK]
            kt = kh * emb
            khat = kh * ehat
            a_mat = jnp.where(causal, _dot(qt, kt, NT, ONE_PASS), 0.0)
            vh = v_ref[:, vs]
            o = oraw_ref[:, vs]
            rs = lax.rsqrt(jnp.mean(o * o, axis=-1, keepdims=True) + EPS)
            on = o * rs
            gg = gg_ref[:, vs]
            rr = r_ref[:, vs]
            sg = _sigmoid(rr)
            d_out = dog_ref[:, vs]
            dr = d_out * (on * gg) * (sg * (1.0 + rr * (1.0 - sg)))
            d_og = d_out * (rr * sg)
            dgg_ref[:, vs] += jnp.sum(d_og * on, axis=0, keepdims=True)
            d_on = d_og * gg
            d_o = rs * (d_on - on * jnp.mean(d_on * on, axis=-1, keepdims=True))
            s_t = st_ref[h]
            ds_t = ds_ref[h]
            d_a = jnp.where(causal, _dot(d_o, vh, NT, ONE_PASS), 0.0)
            dv = _dot(a_mat, d_o, TN, ONE_PASS) + _dot(khat, ds_t, NT, ONE_PASS)
            dqt = _dot(d_a, kt, NN, ONE_PASS) + _dot(d_o, s_t, NN, ONE_PASS)
            dkt = _dot(d_a, qt, TN, ONE_PASS)
            dkhat = _dot(vh, ds_t, NN, ONE_PASS)
            ds_ref[h] = ds_t * e_last + _dot(d_o, qt, TN, ONE_PASS)
            dq = dqt * eb * scale
            dk = dkt * emb + dkhat * ehat
            db = dqt * qt - dkt * kt - dkhat * khat
            d_last = (jnp.sum(dkhat * khat, axis=0, keepdims=True)
                      + e_last * jnp.sum(ds_t * s_t, axis=0, keepdims=True))
            dla_parts.append(_dot(upper, db, NN, HIGHEST) + d_last)
            dqkvr_ref[:, ks] = dq.astype(BF16)
            dqkvr_ref[:, DK + h * HK:DK + (h + 1) * HK] = dk.astype(BF16)
            dqkvr_ref[:, DV + h * HV:DV + (h + 1) * HV] = dv.astype(BF16)
            dqkvr_ref[:, 2 * DV + h * HV:2 * DV + (h + 1) * HV] = dr.astype(BF16)
        dla = jnp.concatenate(dla_parts, axis=1)
        dgp = dla * (1.0 / GLA_GATE_NORM) * _sigmoid(-gp)
        dba_ref[...] += jnp.sum(dgp, axis=0, keepdims=True)
        dgp_b = dgp.astype(BF16)
        dw2_ref[...] += _dot(a_ref[...].astype(BF16), dgp_b, TN)
        da_ref[...] = _dot(dgp_b, w2_ref[...], NT).astype(BF16)

    rev = lambda j: pl.BlockSpec((C, DV), lambda i: (n - 1 - i, j))
    full = lambda s: pl.BlockSpec(s, lambda i: (0,) * len(s))
    return _pcall(
        body, name="gla_bwd",
        out_shape=(jax.ShapeDtypeStruct((T, 3 * DV), BF16), jax.ShapeDtypeStruct((T, RP), BF16),
                   jax.ShapeDtypeStruct((RP, DK), F32), jax.ShapeDtypeStruct((1, DK), F32),
                   jax.ShapeDtypeStruct((1, DV), F32)),
        grid=(n,),
        in_specs=[rev(0), rev(1), rev(2), pl.BlockSpec((C, RP), lambda i: (n - 1 - i, 0)),
                  full((RP, DK)), full((1, DK)), full((1, DV)), rev(0),
                  pl.BlockSpec((None, H, HV, HK), lambda i: (n - 1 - i, 0, 0, 0)), rev(0)],
        out_specs=(pl.BlockSpec((C, 3 * DV), lambda i: (n - 1 - i, 0)),
                   pl.BlockSpec((C, RP), lambda i: (n - 1 - i, 0)),
                   full((RP, DK)), full((1, DK)), full((1, DV))),
        scratch_shapes=[pltpu.VMEM((H, HV, HK), F32)],
        semantics=("arbitrary",))(proj, proj, proj, a_pad, w2, b_a, g_gla, o_raw, states, do_gla)


def _pool_windows(p, g, T):
    t = lax.broadcasted_iota(jnp.int32, (T, 1), 0)
    s = p
    for lvl in range(POOL_GROUPS):
        sh = 1 << lvl
        nxt = s + jnp.where(t >= sh, pltpu.roll(s, sh, 0), 0.0)
        s = jnp.where(lvl <= g, nxt, s)
    win = jnp.left_shift(2, g)
    inv = 1.0 / jnp.minimum(t + 1, win).astype(F32)
    return s * inv - p, inv


def _pool_fwd(proj, w_pool, scale, *, T, PW, col_block):
    GW = PW // POOL_GROUPS
    per = PW // GW

    def body(p_ref, w_ref, s_ref, o_ref):
        g = pl.program_id(0)
        pooled, _ = _pool_windows(p_ref[...], g, T)
        mixed = _dot(pooled.astype(BF16), w_ref[...])
        o_ref[...] = (mixed * s_ref[...]).astype(BF16)

    return _pcall(body, name="pool_fwd", out_shape=jax.ShapeDtypeStruct((T, PW), BF16),
                  grid=(POOL_GROUPS,),
                  in_specs=[pl.BlockSpec((T, GW), lambda g: (0, col_block * per + g)),
                            pl.BlockSpec((None, GW, GW), lambda g: (g, 0, 0)),
                            pl.BlockSpec((1, GW), lambda g: (0, g))],
                  out_specs=pl.BlockSpec((T, GW), lambda g: (0, g)),
                  semantics=("parallel",))(proj, w_pool, scale)


def _pool_bwd(proj, w_pool, scale, do_pool, *, T, PW, col_block):
    GW = PW // POOL_GROUPS
    per = PW // GW

    def body(p_ref, w_ref, s_ref, do_ref, dp_ref, dw_ref, dsc_ref):
        g = pl.program_id(0)
        pooled, inv = _pool_windows(p_ref[...], g, T)
        pooled_b = pooled.astype(BF16)
        w = w_ref[...]
        mixed = _dot(pooled_b, w)
        d_out = do_ref[...]
        dsc_ref[...] = jnp.sum(d_out * mixed, axis=0, keepdims=True)
        dmixed = (d_out * s_ref[...]).astype(BF16)
        dw_ref[...] = _dot(pooled_b, dmixed, TN)
        dpooled = _dot(dmixed, w, NT)
        t = lax.broadcasted_iota(jnp.int32, (T, 1), 0)
        s = dpooled * inv
        for lvl in range(POOL_GROUPS):
            sh = 1 << lvl
            nxt = s + jnp.where(t < T - sh, pltpu.roll(s, T - sh, 0), 0.0)
            s = jnp.where(lvl <= g, nxt, s)
        dp_ref[...] = (s - dpooled).astype(BF16)

    return _pcall(body, name="pool_bwd",
                  out_shape=(jax.ShapeDtypeStruct((T, PW), BF16),
                             jax.ShapeDtypeStruct((POOL_GROUPS, GW, GW), F32),
                             jax.ShapeDtypeStruct((1, PW), F32)),
                  grid=(POOL_GROUPS,),
                  in_specs=[pl.BlockSpec((T, GW), lambda g: (0, col_block * per + g)),
                            pl.BlockSpec((None, GW, GW), lambda g: (g, 0, 0)),
                            pl.BlockSpec((1, GW), lambda g: (0, g)),
                            pl.BlockSpec((T, GW), lambda g: (0, g))],
                  out_specs=(pl.BlockSpec((T, GW), lambda g: (0, g)),
                             pl.BlockSpec((None, GW, GW), lambda g: (g, 0, 0)),
                             pl.BlockSpec((1, GW), lambda g: (0, g))),
                  semantics=("parallel",))(proj, w_pool, scale, do_pool)


def _merge_fwd(y_gla, y_pool, proj, *, T, D, col_block):
    tr = _tile(T, 256, 16)

    def body(yg_ref, yp_ref, g1_ref, g2_ref, o_ref):
        o_ref[...] = (_sigmoid(g1_ref[...]) * yg_ref[...]
                      + _sigmoid(g2_ref[...]) * yp_ref[...]).astype(BF16)

    row = pl.BlockSpec((tr, D), lambda i: (i, 0))
    return _pcall(body, name="merge_fwd", out_shape=jax.ShapeDtypeStruct((T, D), BF16),
                  grid=(T // tr,),
                  in_specs=[row, row, pl.BlockSpec((tr, D), lambda i: (i, col_block)),
                            pl.BlockSpec((tr, D), lambda i: (i, col_block + 1))],
                  out_specs=row, semantics=("parallel",))(y_gla, y_pool, proj, proj)


def _merge_bwd(dmerged, y_gla, y_pool, proj, *, T, D, col_block):
    tr = _tile(T, 256, 16)

    def body(dm_ref, yg_ref, yp_ref, g1_ref, g2_ref, dyg_ref, dyp_ref, dg_ref):
        dm = dm_ref[...]
        s1 = _sigmoid(g1_ref[...])
        s2 = _sigmoid(g2_ref[...])
        dyg_ref[...] = (dm * s1).astype(BF16)
        dyp_ref[...] = (dm * s2).astype(BF16)
        dg_ref[:, :D] = (dm * yg_ref[...] * s1 * (1.0 - s1)).astype(BF16)
        dg_ref[:, D:] = (dm * yp_ref[...] * s2 * (1.0 - s2)).astype(BF16)

    row = pl.BlockSpec((tr, D), lambda i: (i, 0))
    return _pcall(body, name="merge_bwd",
                  out_shape=(jax.ShapeDtypeStruct((T, D), BF16), jax.ShapeDtypeStruct((T, D), BF16),
                             jax.ShapeDtypeStruct((T, 2 * D), BF16)),
                  grid=(T // tr,),
                  in_specs=[row, row, row, pl.BlockSpec((tr, D), lambda i: (i, col_block)),
                            pl.BlockSpec((tr, D), lambda i: (i, col_block + 1))],
                  out_specs=(row, row, pl.BlockSpec((tr, 2 * D), lambda i: (i, 0))),
                  semantics=("parallel",))(dmerged, y_gla, y_pool, proj, proj)


def _attn_fwd(q, kv, *, T, D, M):
    H = CROSS_HEADS
    HD = D // H
    tq = _tile(T, 512, 16)
    scale = HD ** -0.5

    def body(q_ref, kv_ref, o_ref):
        for h in range(H):
            hs = slice(h * HD, (h + 1) * HD)
            s = _dot(q_ref[:, hs], kv_ref[:, hs], NT) * scale
            e = jnp.exp(s - jnp.max(s, axis=-1, keepdims=True))
            p = e / jnp.sum(e, axis=-1, keepdims=True)
            o_ref[:, hs] = _dot(p.astype(BF16), kv_ref[:, D + h * HD:D + (h + 1) * HD]).astype(BF16)

    row = pl.BlockSpec((tq, D), lambda i: (i, 0))
    return _pcall(body, name="attn_fwd", out_shape=jax.ShapeDtypeStruct((T, D), BF16),
                  grid=(T // tq,), in_specs=[row, pl.BlockSpec((M, 2 * D), lambda i: (0, 0))],
                  out_specs=row, semantics=("parallel",))(q, kv)


def _attn_bwd(q, kv, do, *, T, D, M):
    H = CROSS_HEADS
    HD = D // H
    tq = _tile(T, 512, 16)
    scale = HD ** -0.5

    def body(q_ref, kv_ref, do_ref, dq_ref, dkv_ref):
        @pl.when(pl.program_id(0) == 0)
        def _():
            dkv_ref[...] = jnp.zeros_like(dkv_ref)

        for h in range(H):
            hs = slice(h * HD, (h + 1) * HD)
            vs = slice(D + h * HD, D + (h + 1) * HD)
            qh = q_ref[:, hs]
            kh = kv_ref[:, hs]
            s = _dot(qh, kh, NT) * scale
            e = jnp.exp(s - jnp.max(s, axis=-1, keepdims=True))
            p = e / jnp.sum(e, axis=-1, keepdims=True)
            p_b = p.astype(BF16)
            d_o = do_ref[:, hs]
            dkv_ref[:, vs] += _dot(p_b, d_o, TN)
            dp = _dot(d_o, kv_ref[:, vs], NT)
            ds = (p * (dp - jnp.sum(dp * p, axis=-1, keepdims=True)) * scale).astype(BF16)
            dq_ref[:, hs] = _dot(ds, kh).astype(BF16)
            dkv_ref[:, hs] += _dot(ds, qh, TN)

    row = pl.BlockSpec((tq, D), lambda i: (i, 0))
    full = pl.BlockSpec((M, 2 * D), lambda i: (0, 0))
    return _pcall(body, name="attn_bwd",
                  out_shape=(jax.ShapeDtypeStruct((T, D), BF16), jax.ShapeDtypeStruct((M, 2 * D), F32)),
                  grid=(T // tq,), in_specs=[row, full, row], out_specs=(row, full),
                  semantics=("arbitrary",))(q, kv, do)


def _shift_down(x, halo, s):
    out = pltpu.roll(x, s, 0)
    t8 = lax.broadcasted_iota(jnp.int32, (SUBLANES, 1), 0)
    head = out[:SUBLANES]
    for j in range(s):
        head = jnp.where(t8 == j, halo[SUBLANES - s + j:SUBLANES - s + j + 1, :], head)
    return head if x.shape[0] == SUBLANES else jnp.concatenate([head, out[SUBLANES:]], axis=0)


def _shift_up(x, halo, s):
    rows = x.shape[0]
    out = pltpu.roll(x, rows - s, 0)
    t8 = lax.broadcasted_iota(jnp.int32, (SUBLANES, 1), 0)
    tail = out[rows - SUBLANES:]
    for j in range(s):
        tail = jnp.where(t8 == SUBLANES - s + j, halo[j:j + 1, :], tail)
    return jnp.concatenate([out[:rows - SUBLANES], tail], axis=0)


def _conv_tiles(T):
    tt = _tile(T, 128, SUBLANES)
    return tt, tt // SUBLANES, T // SUBLANES


def _conv_fwd(u0, conv_w, conv_b, *, T, F):
    tt, hb, _ = _conv_tiles(T)
    cw = _tile(F, LANES)

    def body(u_ref, prev_ref, w_ref, b_ref, f_ref):
        i = pl.program_id(0)

        def conv(cs):
            x = u_ref[:, cs]
            halo = jnp.where(i > 0, prev_ref[:, cs], 0.0)
            return (w_ref[2:3, cs] * x + w_ref[1:2, cs] * _shift_down(x, halo, 1)
                    + w_ref[0:1, cs] * _shift_down(x, halo, 2) + b_ref[:, cs])

        for j in range(F // cw):
            gate = conv(slice(j * cw, (j + 1) * cw))
            val = conv(slice(F + j * cw, F + (j + 1) * cw))
            f_ref[:, j * cw:(j + 1) * cw] = (gate * _sigmoid(gate) * val).astype(BF16)

    return _pcall(body, name="conv_fwd", out_shape=jax.ShapeDtypeStruct((T, F), BF16),
                  grid=(T // tt,),
                  in_specs=[pl.BlockSpec((tt, 2 * F), lambda i: (i, 0)),
                            pl.BlockSpec((SUBLANES, 2 * F), lambda i: (jnp.maximum(i * hb - 1, 0), 0)),
                            pl.BlockSpec((CONV_W, 2 * F), lambda i: (0, 0)),
                            pl.BlockSpec((1, 2 * F), lambda i: (0, 0))],
                  out_specs=pl.BlockSpec((tt, F), lambda i: (i, 0)),
                  semantics=("parallel",))(u0, u0, conv_w, conv_b)


def _conv_bwd(u0, conv_w, conv_b, df, *, T, F):
    tt, hb, nb = _conv_tiles(T)
    nt = T // tt
    cw = _tile(F, LANES)

    def body(u_ref, prev_ref, next_ref, df_ref, dfn_ref, w_ref, b_ref, du0_ref, dw_ref, db_ref):
        i = pl.program_id(0)

        @pl.when(i == 0)
        def _():
            dw_ref[...] = jnp.zeros_like(dw_ref)
            db_ref[...] = jnp.zeros_like(db_ref)

        def conv(cs):
            x = u_ref[:, cs]
            halo = jnp.where(i > 0, prev_ref[:, cs], 0.0)
            x1 = _shift_down(x, halo, 1)
            x2 = _shift_down(x, halo, 2)
            u = w_ref[2:3, cs] * x + w_ref[1:2, cs] * x1 + w_ref[0:1, cs] * x2 + b_ref[:, cs]
            xn = next_ref[:, cs]
            tail = x[tt - SUBLANES:, :]
            un = (w_ref[2:3, cs] * xn + w_ref[1:2, cs] * _shift_down(xn, tail, 1)
                  + w_ref[0:1, cs] * _shift_down(xn, tail, 2) + b_ref[:, cs])
            return u, un, (x, x1, x2)

        def glu_grad(gate, val, dff):
            sg = _sigmoid(gate)
            return dff * val * (sg * (1.0 + gate * (1.0 - sg))), dff * (gate * sg)

        def finish(cs, du, dun, xs):
            du0 = (w_ref[2:3, cs] * du + w_ref[1:2, cs] * _shift_up(du, dun, 1)
                   + w_ref[0:1, cs] * _shift_up(du, dun, 2))
            du0_ref[:, cs] = du0.astype(BF16)
            db_ref[:, cs] += jnp.sum(du, axis=0, keepdims=True)
            dw_ref[2:3, cs] += jnp.sum(du * xs[0], axis=0, keepdims=True)
            dw_ref[1:2, cs] += jnp.sum(du * xs[1], axis=0, keepdims=True)
            dw_ref[0:1, cs] += jnp.sum(du * xs[2], axis=0, keepdims=True)

        for j in range(F // cw):
            fs = slice(j * cw, (j + 1) * cw)
            gs, vs = fs, slice(F + j * cw, F + (j + 1) * cw)
            ug, ung, xg = conv(gs)
            uv, unv, xv = conv(vs)
            dug, duv = glu_grad(ug, uv, df_ref[:, fs].astype(F32))
            dung, dunv = glu_grad(ung, unv, dfn_ref[0:SUBLANES, fs].astype(F32))
            dung = jnp.where(i < nt - 1, dung, 0.0)
            dunv = jnp.where(i < nt - 1, dunv, 0.0)
            finish(gs, dug, dung, xg)
            finish(vs, duv, dunv, xv)

    wide = lambda rows, fn: pl.BlockSpec((rows, 2 * F), fn)
    nxt = lambda i: (jnp.minimum((i + 1) * hb, nb - 1), 0)
    return _pcall(body, name="conv_bwd",
                  out_shape=(jax.ShapeDtypeStruct((T, 2 * F), BF16),
                             jax.ShapeDtypeStruct((CONV_W, 2 * F), F32),
                             jax.ShapeDtypeStruct((1, 2 * F), F32)),
                  grid=(nt,),
                  in_specs=[wide(tt, lambda i: (i, 0)),
                            wide(SUBLANES, lambda i: (jnp.maximum(i * hb - 1, 0), 0)),
                            wide(SUBLANES, nxt),
                            pl.BlockSpec((tt, F), lambda i: (i, 0)),
                            pl.BlockSpec((2 * SUBLANES, F),
                                         lambda i: (jnp.minimum((i + 1) * (hb // 2), nb // 2 - 1), 0)),
                            wide(CONV_W, lambda i: (0, 0)), wide(1, lambda i: (0, 0))],
                  out_specs=(wide(tt, lambda i: (i, 0)), wide(CONV_W, lambda i: (0, 0)),
                             wide(1, lambda i: (0, 0))),
                  semantics=("arbitrary",))(u0, u0, u0, df, df, conv_w, conv_b)


def _adamw(w, g, m, v, *, name):
    R, C = w.shape
    tr = _tile(R, max(SUBLANES, (1 << 19) // max(C, 1) // SUBLANES * SUBLANES), SUBLANES)
    c1 = 1.0 / (1.0 - ADAM_B1 ** ADAM_STEP)
    c2 = 1.0 / (1.0 - ADAM_B2 ** ADAM_STEP)

    def body(w_ref, g_ref, m_ref, v_ref, d_ref, mo_ref, vo_ref):
        gv = g_ref[...]
        mn = ADAM_B1 * m_ref[...] + (1.0 - ADAM_B1) * gv
        vn = ADAM_B2 * v_ref[...] + (1.0 - ADAM_B2) * (gv * gv)
        d_ref[...] = -ADAM_LR * ((mn * c1) / (jnp.sqrt(vn * c2) + ADAM_EPS) + ADAM_WD * w_ref[...])
        mo_ref[...] = mn
        vo_ref[...] = vn

    blk = pl.BlockSpec((tr, C), lambda i: (i, 0))
    shp = jax.ShapeDtypeStruct((R, C), F32)
    return _pcall(body, name=name, out_shape=(shp, shp, shp), grid=(R // tr,),
                  in_specs=[blk] * 4, out_specs=(blk,) * 3, semantics=("parallel",))(w, g, m, v)


def _blk(h, C, elems=1 << 19, align=16):
    th = _tile(h, max(align, elems // C // align * align), align)
    if th < h or h * C <= 2 * elems:
        return th, C
    return h, _tile(C, max(LANES, elems // h // LANES * LANES))


def _adamw_halves(w, m, v, g_mine, g_other, c_idx, *, name):
    _, h, C = w.shape
    th, tc = _blk(h, C, align=SUBLANES)
    c1 = 1.0 / (1.0 - ADAM_B1 ** ADAM_STEP)
    c2 = 1.0 / (1.0 - ADAM_B2 ** ADAM_STEP)

    def body(c_ref, w_ref, m_ref, v_ref, gm_ref, go_ref, g_ref, d_ref, mo_ref, vo_ref):
        gv = jnp.where(pl.program_id(0) == c_ref[0], gm_ref[...], go_ref[...])
        mn = ADAM_B1 * m_ref[...] + (1.0 - ADAM_B1) * gv
        vn = ADAM_B2 * v_ref[...] + (1.0 - ADAM_B2) * (gv * gv)
        d_ref[...] = -ADAM_LR * ((mn * c1) / (jnp.sqrt(vn * c2) + ADAM_EPS) + ADAM_WD * w_ref[...])
        g_ref[...] = gv
        mo_ref[...] = mn
        vo_ref[...] = vn

    blk = pl.BlockSpec((None, th, tc), lambda s, i, j, c: (s, i, j))

    def pick(mine):
        def index(s, i, j, c):
            use = (s == c[0]) if mine else (s != c[0])
            return jnp.where(use, i, 0), jnp.where(use, j, 0)
        return pl.BlockSpec((th, tc), index)

    shp = jax.ShapeDtypeStruct((2, h, C), F32)
    return _pcall(body, name=name, out_shape=(shp,) * 4, grid=(2, h // th, C // tc), prefetch=1,
                  in_specs=[blk, blk, blk, pick(True), pick(False)], out_specs=(blk,) * 4,
                  semantics=("parallel", "parallel", "parallel"))(c_idx, w, m, v, g_mine, g_other)


def _mesh_pos():
    x, y, c = lax.axis_index("x"), lax.axis_index("y"), lax.axis_index("c")
    others = [(1 - x, y), (x, 1 - y), (1 - x, 1 - y)]
    return x, y, c, others


def _gather_copies(shards, lands, send_sems, recv_sems):
    x, y, c, others = _mesh_pos()
    me = 2 * x + y
    return [pltpu.make_async_remote_copy(
        src_ref=shards[a].at[c], dst_ref=lands[a].at[me, c],
        send_sem=send_sems.at[3 * a + j], recv_sem=recv_sems.at[3 * a + j],
        device_id=(*chip, c), device_id_type=MESH)
        for a in range(len(shards)) for j, chip in enumerate(others)]


def _near_copies(shards, lands, send_sems, recv_sems):
    x, y, c, others = _mesh_pos()
    me = 2 * x + y
    return [pltpu.make_async_remote_copy(
        src_ref=shards[a].at[c], dst_ref=lands[a].at[me, c],
        send_sem=send_sems.at[2 * a + j], recv_sem=recv_sems.at[2 * a + j],
        device_id=(*chip, c), device_id_type=MESH)
        for a in range(len(shards)) for j, chip in enumerate(others[:2])]


def _relay_copies(shards, zones, send_sems, recv_sems):
    x, y, c, others = _mesh_pos()
    (nx, ny), copies = others[:2], []
    for a in range(len(zones)):
        hc = zones[a].shape[-1] // 2
        for k, (src_chip, to, lo) in enumerate(((ny, nx, 0), (nx, ny, hc))):
            part = zones[a].at[2 * src_chip[0] + src_chip[1], c, :, pl.ds(lo, hc)]
            copies.append(pltpu.make_async_remote_copy(
                src_ref=part, dst_ref=part, send_sem=send_sems.at[2 * a + k], recv_sem=recv_sems.at[2 * a + k],
                device_id=(*to, c), device_id_type=MESH))
    return copies


def _pass_copies(shards, zones, send_sems, recv_sems, pieces=(0, 1, 2, 3)):
    x, y, c, others = _mesh_pos()
    me = 2 * x + y
    copies = []
    for a in range(len(shards)):
        srcs = [zones[a].at[2 * chip[0] + chip[1], c] for chip in others] + [shards[a]]
        dsts = [zones[a].at[2 * chip[0] + chip[1], c] for chip in others] + [zones[a].at[me]]
        copies += [pltpu.make_async_remote_copy(
            src_ref=srcs[p], dst_ref=dsts[p], send_sem=send_sems.at[len(pieces) * a + k],
            recv_sem=recv_sems.at[len(pieces) * a + k], device_id=(x, y, 1 - c), device_id_type=MESH)
            for k, p in enumerate(pieces)]
    return copies


def _exchange_copies(grads, recvs, send_sems, recv_sems):
    x, y, c, _ = _mesh_pos()
    return [pltpu.make_async_remote_copy(
        src_ref=grads[a].at[:, 1 - c], dst_ref=recvs[a], send_sem=send_sems.at[a],
        recv_sem=recv_sems.at[a], device_id=(x, y, 1 - c), device_id_type=MESH) for a in range(len(grads))]


def _split_start(copies, per, srcs, zones, after, *, name):
    n = len(srcs)
    HBM = pl.BlockSpec(memory_space=pltpu.HBM)
    SEM = pl.BlockSpec(memory_space=pltpu.SEMAPHORE)

    def body(*refs):
        send_sems, recv_sems = refs[2 * n + 1], refs[2 * n + 2]
        for cp in copies(refs[:n], refs[n:2 * n], send_sems, recv_sems):
            cp.start()
        refs[-1][...] = jnp.zeros_like(refs[-1])

    hbm = lambda a: pltpu.HBM(a.shape, a.dtype)
    res = _pcall(body, name=name,
                 out_shape=(pltpu.SemaphoreType.DMA((per * n,)), pltpu.SemaphoreType.DMA((per * n,)),
                            *[hbm(a) for a in srcs], *[hbm(a) for a in zones],
                            jax.ShapeDtypeStruct((SUBLANES, LANES), F32)),
                 in_specs=[*[HBM] * (2 * n), pl.BlockSpec(memory_space=pl.ANY)],
                 out_specs=(SEM, SEM, *[HBM] * (2 * n), pl.BlockSpec(memory_space=pltpu.VMEM)),
                 aliases={i: 2 + i for i in range(2 * n)}, split_copy=True)(
        *[pltpu.with_memory_space_constraint(a, pltpu.HBM) for a in [*srcs, *zones]], after)
    return res[0], res[1], list(res[2:2 + n]), list(res[2 + n:2 + 2 * n]), res[-1]


def _split_wait(copies, send_sems, recv_sems, srcs, zones, after, *, name):
    n = len(srcs)
    HBM = pl.BlockSpec(memory_space=pltpu.HBM)
    SEM = pl.BlockSpec(memory_space=pltpu.SEMAPHORE)

    def body(*refs):
        for cp in copies(refs[:n], refs[n:2 * n], refs[2 * n], refs[2 * n + 1]):
            cp.wait_send()
            cp.wait_recv()

    hbm = lambda a: pltpu.HBM(a.shape, a.dtype)
    res = _pcall(body, name=name, out_shape=(*[hbm(a) for a in srcs], *[hbm(a) for a in zones]),
                 in_specs=[*[HBM] * (2 * n), SEM, SEM, pl.BlockSpec(memory_space=pl.ANY)],
                 out_specs=tuple([HBM] * (2 * n)), aliases={i: i for i in range(2 * n)},
                 split_copy=True)(*srcs, *zones, send_sems, recv_sems, after)
    return list(res[:n]), list(res[n:])


def _add_halves(grad, recv, c_idx, *, name):
    S, _, h, C = grad.shape
    th, tc = _blk(h, C)

    def body(c_ref, g_ref, r_ref, o_ref):
        o_ref[...] = (g_ref[...].astype(F32) + r_ref[...].astype(F32)).astype(o_ref.dtype)

    return _pcall(body, name=name, out_shape=jax.ShapeDtypeStruct((S, h, C), grad.dtype),
                  grid=(S, h // th, C // tc), prefetch=1,
                  in_specs=[pl.BlockSpec((None, None, th, tc), lambda s, i, j, c: (s, c[0], i, j)),
                            pl.BlockSpec((None, th, tc), lambda s, i, j, c: (s, i, j))],
                  out_specs=pl.BlockSpec((None, th, tc), lambda s, i, j, c: (s, i, j)),
                  semantics=("parallel", "parallel", "parallel"))(c_idx, grad, recv)


def _scatter_copies(srcs, lands, send_sems, recv_sems):
    x, y, c, others = _mesh_pos()
    return [pltpu.make_async_remote_copy(
        src_ref=srcs[a].at[2 * chip[0] + chip[1]], dst_ref=lands[a].at[j],
        send_sem=send_sems.at[3 * a + j], recv_sem=recv_sems.at[3 * a + j],
        device_id=(*chip, c), device_id_type=MESH)
        for a in range(len(srcs)) for j, chip in enumerate(others)]


def _add_chips(sums, recv, chip_idx, *, name):
    _, h, C = sums.shape
    th, tc = _blk(h, C)

    def body(k_ref, s_ref, r_ref, o_ref):
        acc = s_ref[...].astype(F32) + r_ref[0].astype(F32)
        acc = acc + r_ref[1].astype(F32)
        o_ref[...] = acc + r_ref[2].astype(F32)

    return _pcall(body, name=name, out_shape=jax.ShapeDtypeStruct((h, C), F32),
                  grid=(h // th, C // tc), prefetch=1,
                  in_specs=[pl.BlockSpec((None, th, tc), lambda i, j, k: (k[0], i, j)),
                            pl.BlockSpec((3, th, tc), lambda i, j, k: (0, i, j))],
                  out_specs=pl.BlockSpec((th, tc), lambda i, j, k: (i, j)),
                  semantics=("parallel", "parallel"))(chip_idx, sums, recv)


def _swap_copies(halves, others, send_sems, recv_sems):
    x, y, c, _ = _mesh_pos()
    return [pltpu.make_async_remote_copy(
        src_ref=halves[a], dst_ref=others[a], send_sem=send_sems.at[a], recv_sem=recv_sems.at[a],
        device_id=(x, y, 1 - c), device_id_type=MESH) for a in range(len(halves))]


def _all_reduce_small(buf):
    R, L = buf.shape
    NDEV = 8

    def body(x_ref, sum_ref, all_ref, send_sems, recv_sems, local_sem):
        x, y, c, others = _mesh_pos()
        me, sibling = (x, y, c), (x, y, 1 - c)

        def slot(px, py, pc):
            return all_ref.at[4 * px + 2 * py + pc]

        def copy(k, block, to, src=None):
            return pltpu.make_async_remote_copy(
                src_ref=slot(*block) if src is None else src, dst_ref=slot(*block),
                send_sem=send_sems.at[k], recv_sem=recv_sems.at[k], device_id=to, device_id_type=MESH)

        mine = pltpu.make_async_copy(x_ref, slot(*me), local_sem)
        mine.start()
        first = [copy(0, me, sibling, src=x_ref)]
        first += [copy(1 + j, me, (*chip, c), src=x_ref) for j, chip in enumerate(others)]
        for cp in first:
            cp.start()
        passed = [copy(4 + j, (*chip, c), sibling) for j, chip in enumerate(others)]
        for j, chip in enumerate(others):
            copy(1 + j, (*chip, c), me).wait_recv()
            passed[j].start()
        copy(0, sibling, me).wait_recv()
        for j, chip in enumerate(others):
            copy(4 + j, (*chip, 1 - c), me).wait_recv()
        for cp in first + passed:
            cp.wait_send()
        mine.wait()
        acc = all_ref[0]
        for d in range(1, NDEV):
            acc = acc + all_ref[d]
        sum_ref[...] = acc

    VM = pl.BlockSpec(memory_space=pltpu.VMEM)
    return _pcall(body, name="all_reduce_small",
                  out_shape=(jax.ShapeDtypeStruct((R, L), F32), jax.ShapeDtypeStruct((NDEV, R, L), F32)),
                  in_specs=[VM], out_specs=(VM, VM),
                  scratch_shapes=[pltpu.SemaphoreType.DMA((7,)), pltpu.SemaphoreType.DMA((7,)),
                                  pltpu.SemaphoreType.DMA])(buf)[0]


def _pack(arrs, rows_multiple=16):
    flat = [a.reshape(-1).astype(F32) for a in arrs]
    sizes = [f.shape[0] for f in flat]
    total = sum(sizes)
    per = LANES * rows_multiple
    padded = -(-total // per) * per
    flat.append(jnp.zeros((padded - total,), F32))
    offs = [0]
    for s in sizes:
        offs.append(offs[-1] + s)
    return jnp.concatenate(flat).reshape(padded // LANES, LANES), offs


def _unpack(buf, offs, shapes):
    flat = buf.reshape(-1)
    return [flat[offs[i]:offs[i + 1]].reshape(s) for i, s in enumerate(shapes)]


def kernel(x, mem, g_mix, w_in, w_a2, b_a, g_gla, w_pool, pool_scale, w_branch, w_out, g_cross, g_mem, w_cq, w_ckv, w_co, g_ffn, w_up, conv_w, conv_b, w_down, g_final, loss_target, m_g_mix, m_w_in, m_w_a2, m_b_a, m_g_gla, m_w_pool, m_pool_scale, m_w_branch, m_w_out, m_g_cross, m_g_mem, m_w_cq, m_w_ckv, m_w_co, m_g_ffn, m_w_up, m_conv_w, m_conv_b, m_w_down, m_g_final, v_g_mix, v_w_in, v_w_a2, v_b_a, v_g_gla, v_w_pool, v_pool_scale, v_w_branch, v_w_out, v_g_cross, v_g_mem, v_w_cq, v_w_ckv, v_w_co, v_g_ffn, v_w_up, v_conv_w, v_conv_b, v_w_down, v_g_final):
    weights = dict(g_mix=g_mix, w_in=w_in, w_a2=w_a2, b_a=b_a, g_gla=g_gla, w_pool=w_pool,
                   pool_scale=pool_scale, w_branch=w_branch, w_out=w_out, g_cross=g_cross, g_mem=g_mem,
                   w_cq=w_cq, w_ckv=w_ckv, w_co=w_co, g_ffn=g_ffn, w_up=w_up, conv_w=conv_w,
                   conv_b=conv_b, w_down=w_down, g_final=g_final)
    mom_m = dict(g_mix=m_g_mix, w_in=m_w_in, w_a2=m_w_a2, b_a=m_b_a, g_gla=m_g_gla, w_pool=m_w_pool,
                 pool_scale=m_pool_scale, w_branch=m_w_branch, w_out=m_w_out, g_cross=m_g_cross,
                 g_mem=m_g_mem, w_cq=m_w_cq, w_ckv=m_w_ckv, w_co=m_w_co, g_ffn=m_g_ffn, w_up=m_w_up,
                 conv_w=m_conv_w, conv_b=m_conv_b, w_down=m_w_down, g_final=m_g_final)
    mom_v = dict(g_mix=v_g_mix, w_in=v_w_in, w_a2=v_w_a2, b_a=v_b_a, g_gla=v_g_gla, w_pool=v_w_pool,
                 pool_scale=v_pool_scale, w_branch=v_w_branch, w_out=v_w_out, g_cross=v_g_cross,
                 g_mem=v_g_mem, w_cq=v_w_cq, w_ckv=v_w_ckv, w_co=v_w_co, g_ffn=v_g_ffn, w_up=v_w_up,
                 conv_w=v_conv_w, conv_b=v_conv_b, w_down=v_w_down, g_final=v_g_final)
    order = list(weights)
    big = ["w_in", "w_branch", "w_out", "w_cq", "w_ckv", "w_co", "w_up", "w_down"]
    small_sharded = ["w_a2", "w_pool", "conv_w"]
    small_repl = ["g_mix", "b_a", "g_gla", "pool_scale", "g_cross", "g_mem", "g_ffn", "conv_b", "g_final"]

    xs, ms, tgt = x[0], mem[0], loss_target[0]
    T, D = xs.shape
    M = ms.shape[0]
    DK, DV, PW = b_a.shape[1], g_gla.shape[1], pool_scale.shape[1]
    RANK = w_a2.shape[1]
    F2 = conv_b.shape[1]
    F = F2 // 2
    DIN = N_CHIPS * w_in.shape[2]
    OFF_A = 2 * DK + 2 * DV
    OFF_P = OFF_A + RANK
    RP = LANES
    GW = PW // POOL_GROUPS
    assert PW == DV and 4 * DV == 2 * D and OFF_P + PW + 2 * D == DIN

    cx, cy, cc = lax.axis_index("x"), lax.axis_index("y"), lax.axis_index("c")
    chip = 2 * cx + cy
    c_idx = jnp.reshape(cc, (1,)).astype(jnp.int32)
    chip_idx = jnp.reshape(chip, (1,)).astype(jnp.int32)

    def halves(a):
        return a.reshape(2, a.shape[0] // 2, a.shape[1])

    shard2d = {k: (weights[k][0].T if k == "w_in" else weights[k][0]) for k in big}
    small_pack, small_offs = _pack([weights[k][0] for k in small_sharded], rows_multiple=32)
    flying, passing = {}, {}

    def gather_start(group, keys, tok):
        srcs = [small_pack if k == "small" else shard2d[k].astype(BF16) for k in keys]
        if group != "in":
            srcs = [a + tok[0:1, 0:1].astype(a.dtype) for a in srcs]
        srcs = [halves(a) for a in srcs]
        zones = [lax.empty((N_CHIPS, *s.shape), s.dtype) for s in srcs]
        first = (_near_copies, 2) if group == "in" else (_gather_copies, 3)
        s_sems, r_sems, srcs, zones, tok = _split_start(*first, srcs, zones, tok, name=f"gather_start_{group}")
        flying[group] = (keys, s_sems, r_sems, srcs, zones)
        return tok

    tok = gather_start("in", ["w_in"], xs)

    def arrive_in(after):
        keys, s_sems, r_sems, srcs, zones = flying["in"]
        near, diag = functools.partial(_pass_copies, pieces=(0, 1, 3)), functools.partial(_pass_copies, pieces=(2,))
        srcs, zones = _split_wait(_near_copies, s_sems, r_sems, srcs, zones, after, name="gather_wait_in")
        rs, rr, srcs, zones, tok = _split_start(_relay_copies, 2, srcs, zones, after, name="gather_relay_start_in")
        ns, nr, srcs, zones, tok = _split_start(near, 3, srcs, zones, tok, name="gather_pass_near_start_in")
        for group, group_keys in (("mix", ["w_branch", "w_out", "small"]), ("cross", ["w_cq", "w_ckv", "w_co"]),
                                  ("up", ["w_up"]), ("down", ["w_down"])):
            tok = gather_start(group, group_keys, tok)
        after = tok
        srcs, zones = _split_wait(_relay_copies, rs, rr, srcs, zones, after, name="gather_relay_wait_in")
        ds, dr, srcs, zones, _ = _split_start(diag, 1, srcs, zones, after, name="gather_pass_diag_start_in")
        srcs, zones = _split_wait(near, ns, nr, srcs, zones, after, name="gather_pass_near_wait_in")
        _, full = _split_wait(diag, ds, dr, srcs, zones, after, name="gather_pass_diag_wait_in")
        return {k: f.reshape(N_CHIPS, f.shape[1] * f.shape[2], f.shape[3]) for k, f in zip(keys, full)}

    def landed(group, after):
        keys, s_sems, r_sems, srcs, zones = flying[group]
        srcs, zones = _split_wait(_gather_copies, s_sems, r_sems, srcs, zones, after,
                                  name=f"gather_wait_{group}")
        s_sems, r_sems, srcs, zones, token = _split_start(_pass_copies, 4, srcs, zones, after,
                                                          name=f"gather_pass_start_{group}")
        passing[group] = (keys, s_sems, r_sems, srcs, zones)
        return token

    def arrive(group, after):
        keys, s_sems, r_sems, srcs, zones = passing[group]
        _, full = _split_wait(_pass_copies, s_sems, r_sems, srcs, zones, after,
                              name=f"gather_pass_wait_{group}")
        return {k: f.reshape(N_CHIPS, f.shape[1] * f.shape[2], f.shape[3]) for k, f in zip(keys, full)}

    def rows(g):
        return g.reshape(-1, g.shape[2])

    h1, r1 = _rms_fwd(xs, g_mix + tok[0:1, 0:1], name="norm_mix")
    W_in = rows(arrive_in(h1)["w_in"])
    W_main = jnp.concatenate([W_in[:OFF_A], W_in[OFF_P:]], axis=0)
    W_a = jnp.pad(W_in[OFF_A:OFF_P], ((0, RP - RANK), (0, 0)))
    tok = landed("mix", W_a)
    proj = _mm(h1, W_main, "nt", name="proj_main", out_dtype=F32, after=tok)
    gw = arrive("mix", proj)
    W_branch, W_out, small_all = rows(gw["w_branch"]), rows(gw["w_out"]), gw["small"]
    sm = [_unpack(small_all[j], small_offs, [weights[k].shape[1:] for k in small_sharded]) for j in range(N_CHIPS)]
    W_a2 = jnp.concatenate([sm[j][0] for j in range(N_CHIPS)], axis=1)
    W_a2p = jnp.pad(W_a2, ((0, RP - RANK), (0, 0))).astype(BF16)
    W_pool = jnp.concatenate([sm[j][1] for j in range(N_CHIPS)], axis=1).astype(BF16)
    W_conv = jnp.concatenate([sm[j][2] for j in range(N_CHIPS)], axis=1)

    a_pad = _mm(h1, W_a, "nt", name="proj_gate_rank", out_dtype=F32)
    o_gla, o_raw, states = _gla_fwd(proj, a_pad, W_a2p, b_a, g_gla, T=T, DK=DK, DV=DV)
    o_pool = _pool_fwd(proj, W_pool, pool_scale, T=T, PW=PW, col_block=3)
    tok = landed("cross", o_pool)
    y_gla = _mm(o_gla, W_branch, "nn", name="branch_gla", out_dtype=BF16, K=DV, after=tok)
    y_pool = _mm(o_pool, W_branch, "nn", name="branch_pool", out_dtype=BF16, K=PW, b_off=(DV, 0))
    merged = _merge_fwd(y_gla, y_pool, proj, T=T, D=D, col_block=2)
    x1 = _mm(merged, W_out, "nn", name="mix_out", out_dtype=F32, add=xs)

    h2, r2 = _rms_fwd(x1, g_cross, name="norm_cross")
    mem_n, rm = _rms_fwd(ms, g_mem, name="norm_mem")
    gw = arrive("cross", h2)
    W_cq, W_ckv, W_co = rows(gw["w_cq"]), gw["w_ckv"], rows(gw["w_co"])
    qc = _mm(h2, W_cq, "nn", name="cross_q", out_dtype=BF16)
    kv = _mm(mem_n, W_ckv, "nn", name="cross_kv", out_dtype=BF16, b_blocked=True)
    o_att = _attn_fwd(qc, kv, T=T, D=D, M=M)
    x2 = _mm(o_att, W_co, "nn", name="cross_out", out_dtype=F32, add=x1)

    tok = landed("up", x2)
    h3, r3 = _rms_fwd(x2, g_ffn + tok[0:1, 0:1], name="norm_ffn")
    W_up = arrive("up", h3)["w_up"]
    u0 = _mm(h3, W_up, "nn", name="ffn_up", out_dtype=F32, b_blocked=True)
    tok = landed("down", u0)
    f_act = _conv_fwd(u0, W_conv, conv_b + tok[0:1, 0:1], T=T, F=F)
    W_down = rows(arrive("down", f_act)["w_down"])
    x3 =_mm(f_act, W_down, "nn", name="ffn_down", out_dtype=F32, add=x2)

    loss_part, dx3, dx3_b, dg_final = _loss_head(x3, g_final.reshape(1, D), tgt)

    def col_shards(g):
        nb, K, Nb = g.shape
        return g.reshape(nb, 2, K // 2, Nb)

    def row_shards(g):
        R, N = g.shape
        return g.reshape(N_CHIPS, 2, R // N_CHIPS // 2, N)

    exchanging, in_flight = {}, []

    def exchange_start(group, keys, partials, after):
        recvs = [lax.empty((p.shape[0], *p.shape[2:]), p.dtype) for p in partials]
        s_sems, r_sems, partials, recvs, token = _split_start(
            _exchange_copies, 1, partials, recvs, after, name=f"grad_exchange_start_{group}")
        exchanging[group] = (keys, s_sems, r_sems, partials, recvs)
        return token

    def scatter_start(group, after):
        keys, s_sems, r_sems, partials, recvs = exchanging[group]
        partials, recvs = _split_wait(_exchange_copies, s_sems, r_sems, partials, recvs, after,
                                      name=f"grad_exchange_wait_{group}")
        chip_sums = [_add_halves(p, r, c_idx, name=f"grad_add_halves_{k}")
                     for k, p, r in zip(keys, partials, recvs)]
        lands = [lax.empty((3, *s.shape[1:]), s.dtype) for s in chip_sums]
        s_sems, r_sems, sums, lands, token = _split_start(
            _scatter_copies, 3, chip_sums, lands, after, name=f"grad_scatter_start_{group}")
        in_flight.append((group, keys, s_sems, r_sems, sums, lands))
        return token

    collected = []

    def collect(after):
        group, keys, s_sems, r_sems, sums, lands = in_flight.pop(0)
        sums, from_chips = _split_wait(_scatter_copies, s_sems, r_sems, sums, lands, after,
                                       name=f"grad_scatter_wait_{group}")
        half_sums = [_add_chips(s, r, chip_idx, name=f"grad_add_chips_{k}") for k, s, r in zip(keys, sums, from_chips)]
        others = [lax.empty(h.shape, h.dtype) for h in half_sums]
        s_sems, r_sems, half_sums, others, token = _split_start(
            _swap_copies, 1, half_sums, others, after, name=f"grad_swap_start_{group}")
        collected.append((keys, s_sems, r_sems, half_sums, others))
        return token

    df = _mm(dx3_b, W_down, "nt", name="d_ffn_act", out_dtype=BF16)
    dW_down = _mm(f_act, dx3_b, "tn", name="dw_down", out_dtype=BF16)
    du0, dconv_w, dconv_b = _conv_bwd(u0, W_conv, conv_b, df, T=T, F=F)
    dh3 = _mm(du0, W_up, "nt", name="d_ffn_in", out_dtype=F32, b_blocked=True, tk=F2 // N_CHIPS)
    dW_up = _mm(h3, du0, "tn", name="dw_up", out_dtype=BF16, out_blocks=N_CHIPS)
    tok = exchange_start("ffn", ["w_down", "w_up"], [row_shards(dW_down), col_shards(dW_up)], dh3)
    dx2, dx2_b, dg_ffn = _rms_bwd(dh3, x2, r3 + tok[0:1, 0:1], g_ffn, dx3, name="norm_ffn_bwd")

    do_att = _mm(dx2_b, W_co, "nt", name="d_cross_o", out_dtype=BF16)
    dW_co = _mm(o_att, dx2_b, "tn", name="dw_co", out_dtype=BF16)
    tok = scatter_start("ffn", dW_co)
    dq, dkv = _attn_bwd(qc, kv, do_att, T=T, D=D, M=M)
    dkv_b = dkv.astype(BF16)
    dW_cq = _mm(h2, dq, "tn", name="dw_cq", out_dtype=BF16, after=tok)
    dh2 = _mm(dq, W_cq, "nt", name="d_cross_in", out_dtype=F32)
    dW_ckv = _mm(mem_n, dkv_b, "tn", name="dw_ckv", out_dtype=BF16, out_blocks=N_CHIPS)
    dmem_n = _mm(dkv_b, W_ckv, "nt", name="d_mem", out_dtype=F32, b_blocked=True)
    tok = exchange_start("cross", ["w_co", "w_cq", "w_ckv"],
                         [row_shards(dW_co), row_shards(dW_cq), col_shards(dW_ckv)], dmem_n)
    _, _, dg_mem = _rms_bwd(dmem_n, ms, rm, g_mem, None, name="norm_mem_bwd")
    dx1, dx1_b, dg_cross = _rms_bwd(dh2, x1, r2 + tok[0:1, 0:1], g_cross, dx2, name="norm_cross_bwd")

    dmerged = _mm(dx1_b, W_out, "nt", name="d_merged", out_dtype=BF16)
    dW_out = _mm(merged, dx1_b, "tn", name="dw_out", out_dtype=BF16)
    tok = scatter_start("cross", dW_out)
    dy_gla, dy_pool, dgates = _merge_bwd(dmerged, y_gla, y_pool, proj, T=T, D=D, col_block=2)
    dW_br_gla = _mm(o_gla, dy_gla, "tn", name="dw_branch_gla", out_dtype=BF16, after=tok)
    dW_br_pool = _mm(o_pool, dy_pool, "tn", name="dw_branch_pool", out_dtype=BF16)
    do_gla = _mm(dy_gla, W_branch, "nt", name="d_o_gla", out_dtype=F32, N=DV)
    do_pool = _mm(dy_pool, W_branch, "nt", name="d_o_pool", out_dtype=F32, N=PW, b_off=(DV, 0))
    dp, dw_pool, dpool_scale = _pool_bwd(proj, W_pool, pool_scale, do_pool, T=T, PW=PW, col_block=3)
    dW_pool = jnp.transpose(dw_pool.reshape(POOL_GROUPS, N_CHIPS, GW // N_CHIPS, GW), (1, 0, 2, 3))
    tok = exchange_start("mix", ["w_out", "w_branch", "w_pool"],
                         [row_shards(dW_out), row_shards(jnp.concatenate([dW_br_gla, dW_br_pool], axis=0)),
                          row_shards(dW_pool.reshape(N_CHIPS * POOL_GROUPS * (GW // N_CHIPS), GW).astype(BF16))],
                         dp)
    dqkvr, da_pad, dw2, db_a, dg_gla = _gla_bwd(proj, a_pad, W_a2p, b_a + tok[0:1, 0:1], g_gla, o_raw, states,
                                               do_gla, T=T, DK=DK, DV=DV)
    tok = scatter_start("mix", dqkvr)
    dproj = jnp.concatenate([dqkvr, dp, dgates], axis=1)
    dW_main = _mm(dproj, h1, "tn", name="dw_in_main", out_dtype=BF16, after=tok)
    dW_a = _mm(da_pad, h1, "tn", name="dw_in_rank", out_dtype=BF16)
    dW_in = jnp.concatenate([dW_main[:OFF_A], dW_a[:RANK], dW_main[OFF_A:]], axis=0)
    tok = exchange_start("in", ["w_in"], [row_shards(dW_in)], dW_a)
    dh1 = _mm(dproj, W_main, "nn", name="d_mix_in_main", out_dtype=F32, after=tok)
    dh1 = _mm(da_pad, W_a, "nn", name="d_mix_in_rank", out_dtype=F32, add=dh1)
    dx0, _, dg_mix = _rms_bwd(dh1, xs, r1, g_mix, dx1, name="norm_mix_bwd")

    grads = {}

    small_grads = [loss_part, dg_mix, db_a, dg_gla, dpool_scale, dg_cross, dg_mem, dg_ffn, dconv_b, dg_final,
                   dw2[:RANK], dconv_w]
    small_buf, offs = _pack(small_grads)
    small_sum = _all_reduce_small(small_buf)
    red = _unpack(small_sum, offs, [g.shape for g in small_grads])
    loss = red[0][0, 0]
    for k, g in zip(small_repl, red[1:10]):
        grads[k] = g.reshape(weights[k].shape)
    nb = DK // N_CHIPS
    grads["w_a2"] = lax.dynamic_slice_in_dim(red[10], chip * nb, nb, axis=1)[None]
    nb = F2 // N_CHIPS
    grads["conv_w"] = lax.dynamic_slice_in_dim(red[11], chip * nb, nb, axis=1)[None]

    delta, new_m, new_v = {}, {}, {}

    def shard_rows(k, a):
        a = a[0]
        return a.T if k == "w_in" else a.reshape(-1, a.shape[-1])

    def whole(k, a):
        a = a.reshape(-1, a.shape[2])
        return (a.T if k == "w_in" else a).reshape(weights[k].shape)

    scatter_start("in", small_sum)

    def finish(after):
        keys, s_sems, r_sems, mine, others = collected.pop(0)
        mine, others = _split_wait(_swap_copies, s_sems, r_sems, mine, others, after,
                                   name=f"grad_swap_wait_{keys[0]}")
        for k, g_mine, g_other in zip(keys, mine, others):
            wmv = [halves(shard_rows(k, src[k])) for src in (weights, mom_m, mom_v)]
            res = _adamw_halves(*wmv, g_mine, g_other, c_idx, name=f"adamw_{k}")
            grads[k], delta[k], new_m[k], new_v[k] = (whole(k, a) for a in res)
        return res[1]

    after = in_flight[-1][4][0]
    while in_flight:
        after = collect(after)
        while len(collected) > 1:
            after = finish(after)
    finish(after)
    small = small_repl + ["w_a2", "conv_w"]
    packs = [_pack([src[k] for k in small])[0] for src in (weights, grads, mom_m, mom_v)]
    _, offs = _pack([weights[k] for k in small])
    outs = _adamw(*packs, name="adamw_small")
    for res, o in zip((delta, new_m, new_v), outs):
        for k, a in zip(small, _unpack(o, offs, [weights[k].shape for k in small])):
            res[k] = a

    return (loss, dx0[None], *[grads[k] for k in order], *[delta[k] for k in order],
            *[new_m[k] for k in order], *[new_v[k] for k in order])
```

```python
import functools

import jax
import jax.numpy as jnp
from jax import lax
from jax.experimental import pallas as pl
from jax.experimental.pallas import tpu as pltpu

F32 = jnp.float32
BF16 = jnp.bfloat16
MESH = pl.DeviceIdType.MESH
HIGHEST = lax.Precision.HIGHEST

EPS = 1e-6
GLA_HEADS = 4
GLA_CHUNK = 128
GLA_GATE_NORM = 16.0
POOL_GROUPS = 4
CROSS_HEADS = 4
CONV_W = 3
N_CHIPS = 4
LANES = 128
SUBLANES = 8
VMEM_LIMIT = 56 << 20

ADAM_LR = 0.001
ADAM_B1 = 0.9
ADAM_B2 = 0.999
ADAM_EPS = 1e-08
ADAM_WD = 0.01
ADAM_STEP = 10

NN = (((1,), (0,)), ((), ()))
NT = (((1,), (1,)), ((), ()))
TN = (((0,), (0,)), ((), ()))


CHUNK_PRECISION = lax.Precision.HIGH


def _dot(a, b, dn=NN, precision=None):
    return lax.dot_general(a, b, dn, precision=precision, preferred_element_type=F32)


def _tile(n, pref, align=LANES):
    t = (min(pref, n) // align) * align
    while t >= align:
        if n % t == 0:
            return t
        t -= align
    return n


def _pcall(body, *, name, out_shape, grid=(), in_specs=None, out_specs=None, scratch_shapes=(),
           semantics=None, prefetch=0, aliases=None, split_copy=False):
    params = dict(vmem_limit_bytes=VMEM_LIMIT)
    if semantics is not None:
        params["dimension_semantics"] = semantics
    if split_copy:
        params["has_side_effects"] = pltpu.SideEffectType.DATAFLOW_SIDE_EFFECTING
    if prefetch:
        grid_spec = pltpu.PrefetchScalarGridSpec(
            num_scalar_prefetch=prefetch, grid=grid, in_specs=in_specs, out_specs=out_specs,
            scratch_shapes=scratch_shapes)
        return pl.pallas_call(body, name=name, out_shape=out_shape, grid_spec=grid_spec,
                              compiler_params=pltpu.CompilerParams(**params))
    kw = {}
    if aliases is not None:
        kw["input_output_aliases"] = aliases
    if in_specs is not None:
        kw["in_specs"] = in_specs
    if out_specs is not None:
        kw["out_specs"] = out_specs
    return pl.pallas_call(body, name=name, out_shape=out_shape, grid=grid,
                          scratch_shapes=scratch_shapes,
                          compiler_params=pltpu.CompilerParams(**params), **kw)


def _sigmoid(x):
    return 1.0 / (1.0 + jnp.exp(-x))


def _log_sigmoid(x):
    return jnp.minimum(x, 0.0) - jnp.log(1.0 + jnp.exp(-jnp.abs(x)))


def _mm(a, b, mode, *, name, out_dtype, M=None, N=None, K=None, a_off=(0, 0), b_off=(0, 0),
        add=None, b_blocked=False, out_blocks=0, after=None, tm=1536, tn=1536, tk=2048):
    if b_blocked:
        nb, R, Cb = b.shape
        b_rows, b_cols = R, nb * Cb
    else:
        b_rows, b_cols = b.shape
    if mode == "nn":
        M = M or a.shape[0]; K = K or a.shape[1]; N = N or b_cols
    elif mode == "nt":
        M = M or a.shape[0]; K = K or a.shape[1]; N = N or b_rows
    else:
        K = K or a.shape[0]; M = M or a.shape[1]; N = N or b_cols
    tm = _tile(M, tm, LANES if mode == "tn" else 16)
    tn = _tile(Cb if (b_blocked and mode != "nt") else (N // out_blocks if out_blocks else N), tn)
    tk = _tile(Cb if (b_blocked and mode == "nt") else K, tk)
    nk = K // tk
    dn = {"nn": NN, "nt": NT, "tn": TN}[mode]

    def off(o, t):
        assert o % t == 0, (name, o, t)
        return o // t

    if mode == "tn":
        ar, ac = off(a_off[0], tk), off(a_off[1], tm)
        a_spec = pl.BlockSpec((tk, tm), lambda i, j, k: (k + ar, i + ac))
    else:
        ar, ac = off(a_off[0], tm), off(a_off[1], tk)
        a_spec = pl.BlockSpec((tm, tk), lambda i, j, k: (i + ar, k + ac))
    if b_blocked and mode == "nt":
        per = Cb // tk
        b_spec = pl.BlockSpec((None, tn, tk), lambda i, j, k: (k // per, j, k % per))
    elif b_blocked:
        per = Cb // tn
        b_spec = pl.BlockSpec((None, tk, tn), lambda i, j, k: (j // per, k, j % per))
    elif mode == "nt":
        br, bc = off(b_off[0], tn), off(b_off[1], tk)
        b_spec = pl.BlockSpec((tn, tk), lambda i, j, k: (j + br, k + bc))
    else:
        br, bc = off(b_off[0], tk), off(b_off[1], tn)
        b_spec = pl.BlockSpec((tk, tn), lambda i, j, k: (k + br, j + bc))
    if out_blocks:
        per_o = N // out_blocks // tn
        o_spec = pl.BlockSpec((None, tm, tn), lambda i, j, k: (j // per_o, i, j % per_o))
        out_shape = jax.ShapeDtypeStruct((out_blocks, M, N // out_blocks), out_dtype)
    else:
        o_spec = pl.BlockSpec((tm, tn), lambda i, j, k: (i, j))
        out_shape = jax.ShapeDtypeStruct((M, N), out_dtype)
    in_specs = [a_spec, b_spec]
    args = [a, b]
    if add is not None:
        assert not out_blocks
        in_specs.append(o_spec)
        args.append(add)
    if after is not None:
        in_specs.append(pl.BlockSpec(memory_space=pl.ANY))
        args.append(after)
    n_in = len(args)

    def finish(r, refs):
        if add is not None:
            r = r + refs[2][...]
        o_ref = refs[n_in]
        o_ref[...] = r.astype(o_ref.dtype)

    def body_one(*refs):
        finish(_dot(refs[0][...].astype(BF16), refs[1][...].astype(BF16), dn), refs)

    def body_acc(*refs):
        acc_ref = refs[-1]
        k = pl.program_id(2)

        @pl.when(k == 0)
        def _():
            acc_ref[...] = jnp.zeros_like(acc_ref)

        acc_ref[...] += _dot(refs[0][...].astype(BF16), refs[1][...].astype(BF16), dn)

        @pl.when(k == nk - 1)
        def _():
            finish(acc_ref[...], refs)

    return _pcall(body_one if nk == 1 else body_acc, name=name, out_shape=out_shape,
                  grid=(M // tm, N // tn, nk), in_specs=in_specs, out_specs=o_spec,
                  scratch_shapes=[] if nk == 1 else [pltpu.VMEM((tm, tn), F32)],
                  semantics=("parallel", "parallel", "arbitrary"))(*args)


def _rms_fwd(x, g, *, name):
    T, D = x.shape
    tr = _tile(T, 128, 16)

    def body(x_ref, g_ref, h_ref, r_ref):
        xv = x_ref[...]
        r = lax.rsqrt(jnp.mean(xv * xv, axis=-1, keepdims=True) + EPS)
        h_ref[...] = (xv * r * g_ref[...]).astype(h_ref.dtype)
        r_ref[...] = r

    row = pl.BlockSpec((tr, D), lambda i: (i, 0))
    return _pcall(body, name=name,
                  out_shape=(jax.ShapeDtypeStruct((T, D), BF16), jax.ShapeDtypeStruct((T, 1), F32)),
                  grid=(T // tr,),
                  in_specs=[row, pl.BlockSpec((1, D), lambda i: (0, 0))],
                  out_specs=(row, pl.BlockSpec((tr, 1), lambda i: (i, 0))),
                  semantics=("parallel",))(x, g)


def _rms_bwd(dh, x, rstd, g, dres, *, name):
    T, D = x.shape
    tr = _tile(T, 128, 16)
    has_res = dres is not None

    def body(*refs):
        if has_res:
            dh_ref, x_ref, r_ref, g_ref, res_ref, dx_ref, dxb_ref, dg_ref = refs
        else:
            dh_ref, x_ref, r_ref, g_ref, dx_ref, dxb_ref, dg_ref = refs
        r = r_ref[...]
        xh = x_ref[...] * r
        dhv = dh_ref[...].astype(F32)
        dxh = dhv * g_ref[...]
        m = jnp.mean(dxh * xh, axis=-1, keepdims=True)
        dx = r * (dxh - xh * m)
        if has_res:
            dx = dx + res_ref[...]
        dx_ref[...] = dx
        dxb_ref[...] = dx.astype(BF16)

        @pl.when(pl.program_id(0) == 0)
        def _():
            dg_ref[...] = jnp.zeros_like(dg_ref)

        dg_ref[...] += jnp.sum(dhv * xh, axis=0, keepdims=True)

    row = pl.BlockSpec((tr, D), lambda i: (i, 0))
    vec = pl.BlockSpec((1, D), lambda i: (0, 0))
    in_specs = [row, row, pl.BlockSpec((tr, 1), lambda i: (i, 0)), vec]
    args = [dh, x, rstd, g]
    if has_res:
        in_specs.append(row)
        args.append(dres)
    return _pcall(body, name=name,
                  out_shape=(jax.ShapeDtypeStruct((T, D), F32), jax.ShapeDtypeStruct((T, D), BF16),
                             jax.ShapeDtypeStruct((1, D), F32)),
                  grid=(T // tr,), in_specs=in_specs, out_specs=(row, row, vec),
                  semantics=("arbitrary",))(*args)


def _loss_head(x3, g, tgt):
    T, D = x3.shape
    tr = _tile(T, 128, 16)

    def body(x_ref, g_ref, t_ref, loss_ref, dx_ref, dxb_ref, dg_ref):
        xv = x_ref[...]
        gv = g_ref[...]
        r = lax.rsqrt(jnp.mean(xv * xv, axis=-1, keepdims=True) + EPS)
        xh = xv * r
        err = xh * gv - t_ref[...]
        dy = err * (1.0 / D)
        dxh = dy * gv
        m = jnp.mean(dxh * xh, axis=-1, keepdims=True)
        dx = r * (dxh - xh * m)
        dx_ref[...] = dx
        dxb_ref[...] = dx.astype(BF16)

        @pl.when(pl.program_id(0) == 0)
        def _():
            dg_ref[...] = jnp.zeros_like(dg_ref)
            loss_ref[...] = jnp.zeros_like(loss_ref)

        dg_ref[...] += jnp.sum(dy * xh, axis=0, keepdims=True)
        part = 0.5 * jnp.sum(jnp.mean(err * err, axis=-1, keepdims=True), axis=0, keepdims=True)
        loss_ref[...] += jnp.broadcast_to(part, loss_ref.shape)

    row = pl.BlockSpec((tr, D), lambda i: (i, 0))
    vec = pl.BlockSpec((1, D), lambda i: (0, 0))
    return _pcall(body, name="loss_head",
                  out_shape=(jax.ShapeDtypeStruct((1, LANES), F32), jax.ShapeDtypeStruct((T, D), F32),
                             jax.ShapeDtypeStruct((T, D), BF16), jax.ShapeDtypeStruct((1, D), F32)),
                  grid=(T // tr,), in_specs=[row, vec, row],
                  out_specs=(pl.BlockSpec((1, LANES), lambda i: (0, 0)), row, row, vec),
                  semantics=("arbitrary",))(x3, g, tgt)


def _gla_chunk_terms(qk, a_ref, w2_ref, ba_ref, DK):
    C = qk.shape[0]
    gp = _dot(a_ref[...].astype(BF16), w2_ref[...]) + ba_ref[...]
    la = _log_sigmoid(gp) * (1.0 / GLA_GATE_NORM)
    row = lax.broadcasted_iota(jnp.int32, (C, C), 0)
    col = lax.broadcasted_iota(jnp.int32, (C, C), 1)
    causal = row >= col
    b = _dot(causal.astype(F32), la, precision=HIGHEST)
    return gp, b, causal


def _gla_fwd(proj, a_pad, w2, b_a, g_gla, *, T, DK, DV):
    assert 2 * DK == DV
    H = GLA_HEADS
    HK, HV = DK // H, DV // H
    C = GLA_CHUNK
    n = T // C
    RP = a_pad.shape[1]
    scale = HK ** -0.5

    def body(qk_ref, v_ref, r_ref, a_ref, w2_ref, ba_ref, gg_ref, og_ref, oraw_ref, st_ref, s_ref):
        @pl.when(pl.program_id(0) == 0)
        def _():
            s_ref[...] = jnp.zeros_like(s_ref)

        st_ref[...] = s_ref[...]
        qk = qk_ref[...]
        _, b, causal = _gla_chunk_terms(qk, a_ref, w2_ref, ba_ref, DK)
        for h in range(H):
            ks = slice(h * HK, (h + 1) * HK)
            vs = slice(h * HV, (h + 1) * HV)
            bh = b[:, ks]
            b_last = bh[C - 1:C, :]
            qt = qk[:, ks] * scale * jnp.exp(bh)
            kh = qk[:, DK + h * HK:DK + (h + 1) * HK]
            kt = kh * jnp.exp(-bh)
            khat = kh * jnp.exp(b_last - bh)
            a_mat = jnp.where(causal, _dot(qt, kt, NT, CHUNK_PRECISION), 0.0)
            vh = v_ref[:, vs]
            s_t = s_ref[h]
            o = _dot(a_mat, vh, NN, CHUNK_PRECISION) + _dot(qt, s_t, NT, CHUNK_PRECISION)
            s_ref[h] = s_t * jnp.exp(b_last) + _dot(vh, khat, TN, CHUNK_PRECISION)
            rs = lax.rsqrt(jnp.mean(o * o, axis=-1, keepdims=True) + EPS)
            rr = r_ref[:, vs]
            og = o * rs * gg_ref[:, vs] * (rr * _sigmoid(rr))
            oraw_ref[:, vs] = o
            og_ref[:, vs] = og.astype(BF16)

    blk = lambda j: pl.BlockSpec((C, DV), lambda i: (i, j))
    full = lambda s: pl.BlockSpec(s, lambda i: (0,) * len(s))
    return _pcall(
        body, name="gla_fwd",
        out_shape=(jax.ShapeDtypeStruct((T, DV), BF16), jax.ShapeDtypeStruct((T, DV), F32),
                   jax.ShapeDtypeStruct((n, H, HV, HK), F32)),
        grid=(n,),
        in_specs=[blk(0), blk(1), blk(2), pl.BlockSpec((C, RP), lambda i: (i, 0)),
                  full((RP, DK)), full((1, DK)), full((1, DV))],
        out_specs=(blk(0), blk(0), pl.BlockSpec((None, H, HV, HK), lambda i: (i, 0, 0, 0))),
        scratch_shapes=[pltpu.VMEM((H, HV, HK), F32)],
        semantics=("arbitrary",))(proj, proj, proj, a_pad, w2, b_a, g_gla)


def _gla_bwd(proj, a_pad, w2, b_a, g_gla, o_raw, states, do_gla, *, T, DK, DV):
    H = GLA_HEADS
    HK, HV = DK // H, DV // H
    C = GLA_CHUNK
    n = T // C
    RP = a_pad.shape[1]
    scale = HK ** -0.5

    def body(qk_ref, v_ref, r_ref, a_ref, w2_ref, ba_ref, gg_ref, oraw_ref, st_ref, dog_ref,
             dqkvr_ref, da_ref, dw2_ref, dba_ref, dgg_ref, ds_ref):
        @pl.when(pl.program_id(0) == 0)
        def _():
            ds_ref[...] = jnp.zeros_like(ds_ref)
            dw2_ref[...] = jnp.zeros_like(dw2_ref)
            dba_ref[...] = jnp.zeros_like(dba_ref)
            dgg_ref[...] = jnp.zeros_like(dgg_ref)

        qk = qk_ref[...]
        gp, b, causal = _gla_chunk_terms(qk, a_ref, w2_ref, ba_ref, DK)
        row = lax.broadcasted_iota(jnp.int32, (C, C), 0)
        col = lax.broadcasted_iota(jnp.int32, (C, C), 1)
        upper = (col >= row).astype(F32)
        dla_parts = []
        for h in range(H):
            ks = slice(h * HK, (h + 1) * HK)
            vs = slice(h * HV, (h + 1) * HV)
            bh = b[:, ks]
            b_last = bh[C - 1:C, :]
            eb = jnp.exp(bh)
            emb = jnp.exp(-bh)
            ehat = jnp.exp(b_last - bh)
            e_last = jnp.exp(b_last)
            qt = qk[:, ks] * scale * eb
            kh = qk[:, DK + h * HK:DK + (h + 1) * HK]
            kt = kh * emb
            khat = kh * ehat
            a_mat = jnp.where(causal, _dot(qt, kt, NT, CHUNK_PRECISION), 0.0)
            vh = v_ref[:, vs]
            o = oraw_ref[:, vs]
            rs = lax.rsqrt(jnp.mean(o * o, axis=-1, keepdims=True) + EPS)
            on = o * rs
            gg = gg_ref[:, vs]
            rr = r_ref[:, vs]
            sg = _sigmoid(rr)
            d_out = dog_ref[:, vs]
            dr = d_out * (on * gg) * (sg * (1.0 + rr * (1.0 - sg)))
            d_og = d_out * (rr * sg)
            dgg_ref[:, vs] += jnp.sum(d_og * on, axis=0, keepdims=True)
            d_on = d_og * gg
            d_o = rs * (d_on - on * jnp.mean(d_on * on, axis=-1, keepdims=True))
            s_t = st_ref[h]
            ds_t = ds_ref[h]
            d_a = jnp.where(causal, _dot(d_o, vh, NT, CHUNK_PRECISION), 0.0)
            dv = _dot(a_mat, d_o, TN, CHUNK_PRECISION) + _dot(khat, ds_t, NT, CHUNK_PRECISION)
            dqt = _dot(d_a, kt, NN, CHUNK_PRECISION) + _dot(d_o, s_t, NN, CHUNK_PRECISION)
            dkt = _dot(d_a, qt, TN, CHUNK_PRECISION)
            dkhat = _dot(vh, ds_t, NN, CHUNK_PRECISION)
            ds_ref[h] = ds_t * e_last + _dot(d_o, qt, TN, CHUNK_PRECISION)
            dq = dqt * eb * scale
            dk = dkt * emb + dkhat * ehat
            db = dqt * qt - dkt * kt - dkhat * khat
            d_last = (jnp.sum(dkhat * khat, axis=0, keepdims=True)
                      + e_last * jnp.sum(ds_t * s_t, axis=0, keepdims=True))
            dla_parts.append(_dot(upper, db, NN, HIGHEST) + d_last)
            dqkvr_ref[:, ks] = dq.astype(BF16)
            dqkvr_ref[:, DK + h * HK:DK + (h + 1) * HK] = dk.astype(BF16)
            dqkvr_ref[:, DV + h * HV:DV + (h + 1) * HV] = dv.astype(BF16)
            dqkvr_ref[:, 2 * DV + h * HV:2 * DV + (h + 1) * HV] = dr.astype(BF16)
        dla = jnp.concatenate(dla_parts, axis=1)
        dgp = dla * (1.0 / GLA_GATE_NORM) * _sigmoid(-gp)
        dba_ref[...] += jnp.sum(dgp, axis=0, keepdims=True)
        dgp_b = dgp.astype(BF16)
        dw2_ref[...] += _dot(a_ref[...].astype(BF16), dgp_b, TN)
        da_ref[...] = _dot(dgp_b, w2_ref[...], NT).astype(BF16)

    rev = lambda j: pl.BlockSpec((C, DV), lambda i: (n - 1 - i, j))
    full = lambda s: pl.BlockSpec(s, lambda i: (0,) * len(s))
    return _pcall(
        body, name="gla_bwd",
        out_shape=(jax.ShapeDtypeStruct((T, 3 * DV), BF16), jax.ShapeDtypeStruct((T, RP), BF16),
                   jax.ShapeDtypeStruct((RP, DK), F32), jax.ShapeDtypeStruct((1, DK), F32),
                   jax.ShapeDtypeStruct((1, DV), F32)),
        grid=(n,),
        in_specs=[rev(0), rev(1), rev(2), pl.BlockSpec((C, RP), lambda i: (n - 1 - i, 0)),
                  full((RP, DK)), full((1, DK)), full((1, DV)), rev(0),
                  pl.BlockSpec((None, H, HV, HK), lambda i: (n - 1 - i, 0, 0, 0)), rev(0)],
        out_specs=(pl.BlockSpec((C, 3 * DV), lambda i: (n - 1 - i, 0)),
                   pl.BlockSpec((C, RP), lambda i: (n - 1 - i, 0)),
                   full((RP, DK)), full((1, DK)), full((1, DV))),
        scratch_shapes=[pltpu.VMEM((H, HV, HK), F32)],
        semantics=("arbitrary",))(proj, proj, proj, a_pad, w2, b_a, g_gla, o_raw, states, do_gla)


def _pool_windows(p, g, T):
    t = lax.broadcasted_iota(jnp.int32, (T, 1), 0)
    s = p
    for lvl in range(POOL_GROUPS):
        sh = 1 << lvl
        nxt = s + jnp.where(t >= sh, pltpu.roll(s, sh, 0), 0.0)
        s = jnp.where(lvl <= g, nxt, s)
    win = jnp.left_shift(2, g)
    inv = 1.0 / jnp.minimum(t + 1, win).astype(F32)
    return s * inv - p, inv


def _pool_fwd(proj, w_pool, scale, *, T, PW, col_block):
    GW = PW // POOL_GROUPS
    per = PW // GW

    def body(p_ref, w_ref, s_ref, o_ref):
        g = pl.program_id(0)
        pooled, _ = _pool_windows(p_ref[...], g, T)
        mixed = _dot(pooled.astype(BF16), w_ref[...])
        o_ref[...] = (mixed * s_ref[...]).astype(BF16)

    return _pcall(body, name="pool_fwd", out_shape=jax.ShapeDtypeStruct((T, PW), BF16),
                  grid=(POOL_GROUPS,),
                  in_specs=[pl.BlockSpec((T, GW), lambda g: (0, col_block * per + g)),
                            pl.BlockSpec((None, GW, GW), lambda g: (g, 0, 0)),
                            pl.BlockSpec((1, GW), lambda g: (0, g))],
                  out_specs=pl.BlockSpec((T, GW), lambda g: (0, g)),
                  semantics=("parallel",))(proj, w_pool, scale)


def _pool_bwd(proj, w_pool, scale, do_pool, *, T, PW, col_block):
    GW = PW // POOL_GROUPS
    per = PW // GW

    def body(p_ref, w_ref, s_ref, do_ref, dp_ref, dw_ref, dsc_ref):
        g = pl.program_id(0)
        pooled, inv = _pool_windows(p_ref[...], g, T)
        pooled_b = pooled.astype(BF16)
        w = w_ref[...]
        mixed = _dot(pooled_b, w)
        d_out = do_ref[...]
        dsc_ref[...] = jnp.sum(d_out * mixed, axis=0, keepdims=True)
        dmixed = (d_out * s_ref[...]).astype(BF16)
        dw_ref[...] = _dot(pooled_b, dmixed, TN)
        dpooled = _dot(dmixed, w, NT)
        t = lax.broadcasted_iota(jnp.int32, (T, 1), 0)
        s = dpooled * inv
        for lvl in range(POOL_GROUPS):
            sh = 1 << lvl
            nxt = s + jnp.where(t < T - sh, pltpu.roll(s, T - sh, 0), 0.0)
            s = jnp.where(lvl <= g, nxt, s)
        dp_ref[...] = (s - dpooled).astype(BF16)

    return _pcall(body, name="pool_bwd",
                  out_shape=(jax.ShapeDtypeStruct((T, PW), BF16),
                             jax.ShapeDtypeStruct((POOL_GROUPS, GW, GW), F32),
                             jax.ShapeDtypeStruct((1, PW), F32)),
                  grid=(POOL_GROUPS,),
                  in_specs=[pl.BlockSpec((T, GW), lambda g: (0, col_block * per + g)),
                            pl.BlockSpec((None, GW, GW), lambda g: (g, 0, 0)),
                            pl.BlockSpec((1, GW), lambda g: (0, g)),
                            pl.BlockSpec((T, GW), lambda g: (0, g))],
                  out_specs=(pl.BlockSpec((T, GW), lambda g: (0, g)),
                             pl.BlockSpec((None, GW, GW), lambda g: (g, 0, 0)),
                             pl.BlockSpec((1, GW), lambda g: (0, g))),
                  semantics=("parallel",))(proj, w_pool, scale, do_pool)


def _merge_fwd(y_gla, y_pool, proj, *, T, D, col_block):
    tr = _tile(T, 128, 16)

    def body(yg_ref, yp_ref, g1_ref, g2_ref, o_ref):
        o_ref[...] = (_sigmoid(g1_ref[...]) * yg_ref[...]
                      + _sigmoid(g2_ref[...]) * yp_ref[...]).astype(BF16)

    row = pl.BlockSpec((tr, D), lambda i: (i, 0))
    return _pcall(body, name="merge_fwd", out_shape=jax.ShapeDtypeStruct((T, D), BF16),
                  grid=(T // tr,),
                  in_specs=[row, row, pl.BlockSpec((tr, D), lambda i: (i, col_block)),
                            pl.BlockSpec((tr, D), lambda i: (i, col_block + 1))],
                  out_specs=row, semantics=("parallel",))(y_gla, y_pool, proj, proj)


def _merge_bwd(dmerged, y_gla, y_pool, proj, *, T, D, col_block):
    tr = _tile(T, 128, 16)

    def body(dm_ref, yg_ref, yp_ref, g1_ref, g2_ref, dyg_ref, dyp_ref, dg_ref):
        dm = dm_ref[...]
        s1 = _sigmoid(g1_ref[...])
        s2 = _sigmoid(g2_ref[...])
        dyg_ref[...] = (dm * s1).astype(BF16)
        dyp_ref[...] = (dm * s2).astype(BF16)
        dg_ref[:, :D] = (dm * yg_ref[...] * s1 * (1.0 - s1)).astype(BF16)
        dg_ref[:, D:] = (dm * yp_ref[...] * s2 * (1.0 - s2)).astype(BF16)

    row = pl.BlockSpec((tr, D), lambda i: (i, 0))
    return _pcall(body, name="merge_bwd",
                  out_shape=(jax.ShapeDtypeStruct((T, D), BF16), jax.ShapeDtypeStruct((T, D), BF16),
                             jax.ShapeDtypeStruct((T, 2 * D), BF16)),
                  grid=(T // tr,),
                  in_specs=[row, row, row, pl.BlockSpec((tr, D), lambda i: (i, col_block)),
                            pl.BlockSpec((tr, D), lambda i: (i, col_block + 1))],
                  out_specs=(row, row, pl.BlockSpec((tr, 2 * D), lambda i: (i, 0))),
                  semantics=("parallel",))(dmerged, y_gla, y_pool, proj, proj)


def _attn_fwd(q, kv, *, T, D, M):
    H = CROSS_HEADS
    HD = D // H
    tq = _tile(T, 512, 16)
    scale = HD ** -0.5

    def body(q_ref, kv_ref, o_ref):
        for h in range(H):
            hs = slice(h * HD, (h + 1) * HD)
            s = _dot(q_ref[:, hs], kv_ref[:, hs], NT) * scale
            e = jnp.exp(s - jnp.max(s, axis=-1, keepdims=True))
            p = e / jnp.sum(e, axis=-1, keepdims=True)
            o_ref[:, hs] = _dot(p.astype(BF16), kv_ref[:, D + h * HD:D + (h + 1) * HD]).astype(BF16)

    row = pl.BlockSpec((tq, D), lambda i: (i, 0))
    return _pcall(body, name="attn_fwd", out_shape=jax.ShapeDtypeStruct((T, D), BF16),
                  grid=(T // tq,), in_specs=[row, pl.BlockSpec((M, 2 * D), lambda i: (0, 0))],
                  out_specs=row, semantics=("parallel",))(q, kv)


def _attn_bwd(q, kv, do, *, T, D, M):
    H = CROSS_HEADS
    HD = D // H
    tq = _tile(T, 512, 16)
    scale = HD ** -0.5

    def body(q_ref, kv_ref, do_ref, dq_ref, dkv_ref):
        @pl.when(pl.program_id(0) == 0)
        def _():
            dkv_ref[...] = jnp.zeros_like(dkv_ref)

        for h in range(H):
            hs = slice(h * HD, (h + 1) * HD)
            vs = slice(D + h * HD, D + (h + 1) * HD)
            qh = q_ref[:, hs]
            kh = kv_ref[:, hs]
            s = _dot(qh, kh, NT) * scale
            e = jnp.exp(s - jnp.max(s, axis=-1, keepdims=True))
            p = e / jnp.sum(e, axis=-1, keepdims=True)
            p_b = p.astype(BF16)
            d_o = do_ref[:, hs]
            dkv_ref[:, vs] += _dot(p_b, d_o, TN)
            dp = _dot(d_o, kv_ref[:, vs], NT)
            ds = (p * (dp - jnp.sum(dp * p, axis=-1, keepdims=True)) * scale).astype(BF16)
            dq_ref[:, hs] = _dot(ds, kh).astype(BF16)
            dkv_ref[:, hs] += _dot(ds, qh, TN)

    row = pl.BlockSpec((tq, D), lambda i: (i, 0))
    full = pl.BlockSpec((M, 2 * D), lambda i: (0, 0))
    return _pcall(body, name="attn_bwd",
                  out_shape=(jax.ShapeDtypeStruct((T, D), BF16), jax.ShapeDtypeStruct((M, 2 * D), F32)),
                  grid=(T // tq,), in_specs=[row, full, row], out_specs=(row, full),
                  semantics=("arbitrary",))(q, kv, do)


def _shift_down(x, halo, s):
    out = pltpu.roll(x, s, 0)
    t8 = lax.broadcasted_iota(jnp.int32, (SUBLANES, 1), 0)
    head = out[:SUBLANES]
    for j in range(s):
        head = jnp.where(t8 == j, halo[SUBLANES - s + j:SUBLANES - s + j + 1, :], head)
    return head if x.shape[0] == SUBLANES else jnp.concatenate([head, out[SUBLANES:]], axis=0)


def _shift_up(x, halo, s):
    rows = x.shape[0]
    out = pltpu.roll(x, rows - s, 0)
    t8 = lax.broadcasted_iota(jnp.int32, (SUBLANES, 1), 0)
    tail = out[rows - SUBLANES:]
    for j in range(s):
        tail = jnp.where(t8 == SUBLANES - s + j, halo[j:j + 1, :], tail)
    return jnp.concatenate([out[:rows - SUBLANES], tail], axis=0)


def _conv_tiles(T):
    tt = _tile(T, 128, SUBLANES)
    return tt, tt // SUBLANES, T // SUBLANES


def _conv_fwd(u0, conv_w, conv_b, *, T, F):
    tt, hb, _ = _conv_tiles(T)
    cw = _tile(F, LANES)

    def body(u_ref, prev_ref, w_ref, b_ref, f_ref):
        i = pl.program_id(0)

        def conv(cs):
            x = u_ref[:, cs]
            halo = jnp.where(i > 0, prev_ref[:, cs], 0.0)
            return (w_ref[2:3, cs] * x + w_ref[1:2, cs] * _shift_down(x, halo, 1)
                    + w_ref[0:1, cs] * _shift_down(x, halo, 2) + b_ref[:, cs])

        for j in range(F // cw):
            gate = conv(slice(j * cw, (j + 1) * cw))
            val = conv(slice(F + j * cw, F + (j + 1) * cw))
            f_ref[:, j * cw:(j + 1) * cw] = (gate * _sigmoid(gate) * val).astype(BF16)

    return _pcall(body, name="conv_fwd", out_shape=jax.ShapeDtypeStruct((T, F), BF16),
                  grid=(T // tt,),
                  in_specs=[pl.BlockSpec((tt, 2 * F), lambda i: (i, 0)),
                            pl.BlockSpec((SUBLANES, 2 * F), lambda i: (jnp.maximum(i * hb - 1, 0), 0)),
                            pl.BlockSpec((CONV_W, 2 * F), lambda i: (0, 0)),
                            pl.BlockSpec((1, 2 * F), lambda i: (0, 0))],
                  out_specs=pl.BlockSpec((tt, F), lambda i: (i, 0)),
                  semantics=("parallel",))(u0, u0, conv_w, conv_b)


def _conv_bwd(u0, conv_w, conv_b, df, *, T, F):
    tt, hb, nb = _conv_tiles(T)
    nt = T // tt
    cw = _tile(F, LANES)

    def body(u_ref, prev_ref, next_ref, df_ref, dfn_ref, w_ref, b_ref, du0_ref, dw_ref, db_ref):
        i = pl.program_id(0)

        @pl.when(i == 0)
        def _():
            dw_ref[...] = jnp.zeros_like(dw_ref)
            db_ref[...] = jnp.zeros_like(db_ref)

        def conv(cs):
            x = u_ref[:, cs]
            halo = jnp.where(i > 0, prev_ref[:, cs], 0.0)
            x1 = _shift_down(x, halo, 1)
            x2 = _shift_down(x, halo, 2)
            u = w_ref[2:3, cs] * x + w_ref[1:2, cs] * x1 + w_ref[0:1, cs] * x2 + b_ref[:, cs]
            xn = next_ref[:, cs]
            tail = x[tt - SUBLANES:, :]
            un = (w_ref[2:3, cs] * xn + w_ref[1:2, cs] * _shift_down(xn, tail, 1)
                  + w_ref[0:1, cs] * _shift_down(xn, tail, 2) + b_ref[:, cs])
            return u, un, (x, x1, x2)

        def glu_grad(gate, val, dff):
            sg = _sigmoid(gate)
            return dff * val * (sg * (1.0 + gate * (1.0 - sg))), dff * (gate * sg)

        def finish(cs, du, dun, xs):
            du0 = (w_ref[2:3, cs] * du + w_ref[1:2, cs] * _shift_up(du, dun, 1)
                   + w_ref[0:1, cs] * _shift_up(du, dun, 2))
            du0_ref[:, cs] = du0.astype(BF16)
            db_ref[:, cs] += jnp.sum(du, axis=0, keepdims=True)
            dw_ref[2:3, cs] += jnp.sum(du * xs[0], axis=0, keepdims=True)
            dw_ref[1:2, cs] += jnp.sum(du * xs[1], axis=0, keepdims=True)
            dw_ref[0:1, cs] += jnp.sum(du * xs[2], axis=0, keepdims=True)

        for j in range(F // cw):
            fs = slice(j * cw, (j + 1) * cw)
            gs, vs = fs, slice(F + j * cw, F + (j + 1) * cw)
            ug, ung, xg = conv(gs)
            uv, unv, xv = conv(vs)
            dug, duv = glu_grad(ug, uv, df_ref[:, fs].astype(F32))
            dung, dunv = glu_grad(ung, unv, dfn_ref[0:SUBLANES, fs].astype(F32))
            dung = jnp.where(i < nt - 1, dung, 0.0)
            dunv = jnp.where(i < nt - 1, dunv, 0.0)
            finish(gs, dug, dung, xg)
            finish(vs, duv, dunv, xv)

    wide = lambda rows, fn: pl.BlockSpec((rows, 2 * F), fn)
    nxt = lambda i: (jnp.minimum((i + 1) * hb, nb - 1), 0)
    return _pcall(body, name="conv_bwd",
                  out_shape=(jax.ShapeDtypeStruct((T, 2 * F), BF16),
                             jax.ShapeDtypeStruct((CONV_W, 2 * F), F32),
                             jax.ShapeDtypeStruct((1, 2 * F), F32)),
                  grid=(nt,),
                  in_specs=[wide(tt, lambda i: (i, 0)),
                            wide(SUBLANES, lambda i: (jnp.maximum(i * hb - 1, 0), 0)),
                            wide(SUBLANES, nxt),
                            pl.BlockSpec((tt, F), lambda i: (i, 0)),
                            pl.BlockSpec((2 * SUBLANES, F),
                                         lambda i: (jnp.minimum((i + 1) * (hb // 2), nb // 2 - 1), 0)),
                            wide(CONV_W, lambda i: (0, 0)), wide(1, lambda i: (0, 0))],
                  out_specs=(wide(tt, lambda i: (i, 0)), wide(CONV_W, lambda i: (0, 0)),
                             wide(1, lambda i: (0, 0))),
                  semantics=("arbitrary",))(u0, u0, u0, df, df, conv_w, conv_b)


def _adamw(w, g, m, v, *, name):
    R, C = w.shape
    tr = _tile(R, max(SUBLANES, (1 << 19) // max(C, 1) // SUBLANES * SUBLANES), SUBLANES)
    c1 = 1.0 / (1.0 - ADAM_B1 ** ADAM_STEP)
    c2 = 1.0 / (1.0 - ADAM_B2 ** ADAM_STEP)

    def body(w_ref, g_ref, m_ref, v_ref, d_ref, mo_ref, vo_ref):
        gv = g_ref[...]
        mn = ADAM_B1 * m_ref[...] + (1.0 - ADAM_B1) * gv
        vn = ADAM_B2 * v_ref[...] + (1.0 - ADAM_B2) * (gv * gv)
        d_ref[...] = -ADAM_LR * ((mn * c1) / (jnp.sqrt(vn * c2) + ADAM_EPS) + ADAM_WD * w_ref[...])
        mo_ref[...] = mn
        vo_ref[...] = vn

    blk = pl.BlockSpec((tr, C), lambda i: (i, 0))
    shp = jax.ShapeDtypeStruct((R, C), F32)
    return _pcall(body, name=name, out_shape=(shp, shp, shp), grid=(R // tr,),
                  in_specs=[blk] * 4, out_specs=(blk,) * 3, semantics=("parallel",))(w, g, m, v)


def _blk(h, C, elems=1 << 19, align=16):
    th = _tile(h, max(align, elems // C // align * align), align)
    if th < h or h * C <= 2 * elems:
        return th, C
    return h, _tile(C, max(LANES, elems // h // LANES * LANES))


def _adamw_halves(w, m, v, g_mine, g_other, c_idx, *, name):
    _, h, C = w.shape
    th, tc = _blk(h, C, align=SUBLANES)
    c1 = 1.0 / (1.0 - ADAM_B1 ** ADAM_STEP)
    c2 = 1.0 / (1.0 - ADAM_B2 ** ADAM_STEP)

    def body(c_ref, w_ref, m_ref, v_ref, gm_ref, go_ref, g_ref, d_ref, mo_ref, vo_ref):
        gv = jnp.where(pl.program_id(0) == c_ref[0], gm_ref[...], go_ref[...])
        mn = ADAM_B1 * m_ref[...] + (1.0 - ADAM_B1) * gv
        vn = ADAM_B2 * v_ref[...] + (1.0 - ADAM_B2) * (gv * gv)
        d_ref[...] = -ADAM_LR * ((mn * c1) / (jnp.sqrt(vn * c2) + ADAM_EPS) + ADAM_WD * w_ref[...])
        g_ref[...] = gv
        mo_ref[...] = mn
        vo_ref[...] = vn

    blk = pl.BlockSpec((None, th, tc), lambda s, i, j, c: (s, i, j))

    def pick(mine):
        def index(s, i, j, c):
            use = (s == c[0]) if mine else (s != c[0])
            return jnp.where(use, i, 0), jnp.where(use, j, 0)
        return pl.BlockSpec((th, tc), index)

    shp = jax.ShapeDtypeStruct((2, h, C), F32)
    return _pcall(body, name=name, out_shape=(shp,) * 4, grid=(2, h // th, C // tc), prefetch=1,
                  in_specs=[blk, blk, blk, pick(True), pick(False)], out_specs=(blk,) * 4,
                  semantics=("parallel", "parallel", "parallel"))(c_idx, w, m, v, g_mine, g_other)


def _mesh_pos():
    x, y, c = lax.axis_index("x"), lax.axis_index("y"), lax.axis_index("c")
    others = [(1 - x, y), (x, 1 - y), (1 - x, 1 - y)]
    return x, y, c, others


def _gather_copies(shards, lands, send_sems, recv_sems):
    x, y, c, others = _mesh_pos()
    me = 2 * x + y
    return [pltpu.make_async_remote_copy(
        src_ref=shards[a].at[c], dst_ref=lands[a].at[me, c],
        send_sem=send_sems.at[3 * a + j], recv_sem=recv_sems.at[3 * a + j],
        device_id=(*chip, c), device_id_type=MESH)
        for a in range(len(shards)) for j, chip in enumerate(others)]


def _near_copies(shards, lands, send_sems, recv_sems):
    x, y, c, others = _mesh_pos()
    me = 2 * x + y
    return [pltpu.make_async_remote_copy(
        src_ref=shards[a].at[c], dst_ref=lands[a].at[me, c],
        send_sem=send_sems.at[2 * a + j], recv_sem=recv_sems.at[2 * a + j],
        device_id=(*chip, c), device_id_type=MESH)
        for a in range(len(shards)) for j, chip in enumerate(others[:2])]


def _relay_copies(shards, zones, send_sems, recv_sems):
    x, y, c, others = _mesh_pos()
    (nx, ny), copies = others[:2], []
    for a in range(len(zones)):
        hc = zones[a].shape[-1] // 2
        for k, (src_chip, to, lo) in enumerate(((ny, nx, 0), (nx, ny, hc))):
            part = zones[a].at[2 * src_chip[0] + src_chip[1], c, :, pl.ds(lo, hc)]
            copies.append(pltpu.make_async_remote_copy(
                src_ref=part, dst_ref=part, send_sem=send_sems.at[2 * a + k], recv_sem=recv_sems.at[2 * a + k],
                device_id=(*to, c), device_id_type=MESH))
    return copies


def _pass_copies(shards, zones, send_sems, recv_sems, pieces=(0, 1, 2, 3)):
    x, y, c, others = _mesh_pos()
    me = 2 * x + y
    copies = []
    for a in range(len(shards)):
        srcs = [zones[a].at[2 * chip[0] + chip[1], c] for chip in others] + [shards[a]]
        dsts = [zones[a].at[2 * chip[0] + chip[1], c] for chip in others] + [zones[a].at[me]]
        copies += [pltpu.make_async_remote_copy(
            src_ref=srcs[p], dst_ref=dsts[p], send_sem=send_sems.at[len(pieces) * a + k],
            recv_sem=recv_sems.at[len(pieces) * a + k], device_id=(x, y, 1 - c), device_id_type=MESH)
            for k, p in enumerate(pieces)]
    return copies


def _exchange_copies(grads, recvs, send_sems, recv_sems):
    x, y, c, _ = _mesh_pos()
    return [pltpu.make_async_remote_copy(
        src_ref=grads[a].at[:, 1 - c], dst_ref=recvs[a], send_sem=send_sems.at[a],
        recv_sem=recv_sems.at[a], device_id=(x, y, 1 - c), device_id_type=MESH) for a in range(len(grads))]


def _split_start(copies, per, srcs, zones, after, *, name):
    n = len(srcs)
    HBM = pl.BlockSpec(memory_space=pltpu.HBM)
    SEM = pl.BlockSpec(memory_space=pltpu.SEMAPHORE)

    def body(*refs):
        send_sems, recv_sems = refs[2 * n + 1], refs[2 * n + 2]
        for cp in copies(refs[:n], refs[n:2 * n], send_sems, recv_sems):
            cp.start()
        refs[-1][...] = jnp.zeros_like(refs[-1])

    hbm = lambda a: pltpu.HBM(a.shape, a.dtype)
    res = _pcall(body, name=name,
                 out_shape=(pltpu.SemaphoreType.DMA((per * n,)), pltpu.SemaphoreType.DMA((per * n,)),
                            *[hbm(a) for a in srcs], *[hbm(a) for a in zones],
                            jax.ShapeDtypeStruct((SUBLANES, LANES), F32)),
                 in_specs=[*[HBM] * (2 * n), pl.BlockSpec(memory_space=pl.ANY)],
                 out_specs=(SEM, SEM, *[HBM] * (2 * n), pl.BlockSpec(memory_space=pltpu.VMEM)),
                 aliases={i: 2 + i for i in range(2 * n)}, split_copy=True)(
        *[pltpu.with_memory_space_constraint(a, pltpu.HBM) for a in [*srcs, *zones]], after)
    return res[0], res[1], list(res[2:2 + n]), list(res[2 + n:2 + 2 * n]), res[-1]


def _split_wait(copies, send_sems, recv_sems, srcs, zones, after, *, name):
    n = len(srcs)
    HBM = pl.BlockSpec(memory_space=pltpu.HBM)
    SEM = pl.BlockSpec(memory_space=pltpu.SEMAPHORE)

    def body(*refs):
        for cp in copies(refs[:n], refs[n:2 * n], refs[2 * n], refs[2 * n + 1]):
            cp.wait_send()
            cp.wait_recv()

    hbm = lambda a: pltpu.HBM(a.shape, a.dtype)
    res = _pcall(body, name=name, out_shape=(*[hbm(a) for a in srcs], *[hbm(a) for a in zones]),
                 in_specs=[*[HBM] * (2 * n), SEM, SEM, pl.BlockSpec(memory_space=pl.ANY)],
                 out_specs=tuple([HBM] * (2 * n)), aliases={i: i for i in range(2 * n)},
                 split_copy=True)(*srcs, *zones, send_sems, recv_sems, after)
    return list(res[:n]), list(res[n:])


def _add_halves(grad, recv, c_idx, *, name):
    S, _, h, C = grad.shape
    th, tc = _blk(h, C)

    def body(c_ref, g_ref, r_ref, o_ref):
        o_ref[...] = (g_ref[...].astype(F32) + r_ref[...].astype(F32)).astype(o_ref.dtype)

    return _pcall(body, name=name, out_shape=jax.ShapeDtypeStruct((S, h, C), grad.dtype),
                  grid=(S, h // th, C // tc), prefetch=1,
                  in_specs=[pl.BlockSpec((None, None, th, tc), lambda s, i, j, c: (s, c[0], i, j)),
                            pl.BlockSpec((None, th, tc), lambda s, i, j, c: (s, i, j))],
                  out_specs=pl.BlockSpec((None, th, tc), lambda s, i, j, c: (s, i, j)),
                  semantics=("parallel", "parallel", "parallel"))(c_idx, grad, recv)


def _scatter_copies(srcs, lands, send_sems, recv_sems):
    x, y, c, others = _mesh_pos()
    return [pltpu.make_async_remote_copy(
        src_ref=srcs[a].at[2 * chip[0] + chip[1]], dst_ref=lands[a].at[j],
        send_sem=send_sems.at[3 * a + j], recv_sem=recv_sems.at[3 * a + j],
        device_id=(*chip, c), device_id_type=MESH)
        for a in range(len(srcs)) for j, chip in enumerate(others)]


def _add_chips(sums, recv, chip_idx, *, name):
    _, h, C = sums.shape
    th, tc = _blk(h, C)

    def body(k_ref, s_ref, r_ref, o_ref):
        acc = s_ref[...].astype(F32) + r_ref[0].astype(F32)
        acc = acc + r_ref[1].astype(F32)
        o_ref[...] = acc + r_ref[2].astype(F32)

    return _pcall(body, name=name, out_shape=jax.ShapeDtypeStruct((h, C), F32),
                  grid=(h // th, C // tc), prefetch=1,
                  in_specs=[pl.BlockSpec((None, th, tc), lambda i, j, k: (k[0], i, j)),
                            pl.BlockSpec((3, th, tc), lambda i, j, k: (0, i, j))],
                  out_specs=pl.BlockSpec((th, tc), lambda i, j, k: (i, j)),
                  semantics=("parallel", "parallel"))(chip_idx, sums, recv)


def _swap_copies(halves, others, send_sems, recv_sems):
    x, y, c, _ = _mesh_pos()
    return [pltpu.make_async_remote_copy(
        src_ref=halves[a], dst_ref=others[a], send_sem=send_sems.at[a], recv_sem=recv_sems.at[a],
        device_id=(x, y, 1 - c), device_id_type=MESH) for a in range(len(halves))]


def _all_reduce_small(buf):
    R, L = buf.shape
    NDEV = 8

    def body(x_ref, sum_ref, all_ref, send_sems, recv_sems, local_sem):
        x, y, c, others = _mesh_pos()
        me, sibling = (x, y, c), (x, y, 1 - c)

        def slot(px, py, pc):
            return all_ref.at[4 * px + 2 * py + pc]

        def copy(k, block, to, src=None):
            return pltpu.make_async_remote_copy(
                src_ref=slot(*block) if src is None else src, dst_ref=slot(*block),
                send_sem=send_sems.at[k], recv_sem=recv_sems.at[k], device_id=to, device_id_type=MESH)

        mine = pltpu.make_async_copy(x_ref, slot(*me), local_sem)
        mine.start()
        first = [copy(0, me, sibling, src=x_ref)]
        first += [copy(1 + j, me, (*chip, c), src=x_ref) for j, chip in enumerate(others)]
        for cp in first:
            cp.start()
        passed = [copy(4 + j, (*chip, c), sibling) for j, chip in enumerate(others)]
        for j, chip in enumerate(others):
            copy(1 + j, (*chip, c), me).wait_recv()
            passed[j].start()
        copy(0, sibling, me).wait_recv()
        for j, chip in enumerate(others):
            copy(4 + j, (*chip, 1 - c), me).wait_recv()
        for cp in first + passed:
            cp.wait_send()
        mine.wait()
        acc = all_ref[0]
        for d in range(1, NDEV):
            acc = acc + all_ref[d]
        sum_ref[...] = acc

    VM = pl.BlockSpec(memory_space=pltpu.VMEM)
    return _pcall(body, name="all_reduce_small",
                  out_shape=(jax.ShapeDtypeStruct((R, L), F32), jax.ShapeDtypeStruct((NDEV, R, L), F32)),
                  in_specs=[VM], out_specs=(VM, VM),
                  scratch_shapes=[pltpu.SemaphoreType.DMA((7,)), pltpu.SemaphoreType.DMA((7,)),
                                  pltpu.SemaphoreType.DMA])(buf)[0]


def _pack(arrs, rows_multiple=16):
    flat = [a.reshape(-1).astype(F32) for a in arrs]
    sizes = [f.shape[0] for f in flat]
    total = sum(sizes)
    per = LANES * rows_multiple
    padded = -(-total // per) * per
    flat.append(jnp.zeros((padded - total,), F32))
    offs = [0]
    for s in sizes:
        offs.append(offs[-1] + s)
    return jnp.concatenate(flat).reshape(padded // LANES, LANES), offs


def _unpack(buf, offs, shapes):
    flat = buf.reshape(-1)
    return [flat[offs[i]:offs[i + 1]].reshape(s) for i, s in enumerate(shapes)]


def kernel(x, mem, g_mix, w_in, w_a2, b_a, g_gla, w_pool, pool_scale, w_branch, w_out, g_cross, g_mem, w_cq, w_ckv, w_co, g_ffn, w_up, conv_w, conv_b, w_down, g_final, loss_target, m_g_mix, m_w_in, m_w_a2, m_b_a, m_g_gla, m_w_pool, m_pool_scale, m_w_branch, m_w_out, m_g_cross, m_g_mem, m_w_cq, m_w_ckv, m_w_co, m_g_ffn, m_w_up, m_conv_w, m_conv_b, m_w_down, m_g_final, v_g_mix, v_w_in, v_w_a2, v_b_a, v_g_gla, v_w_pool, v_pool_scale, v_w_branch, v_w_out, v_g_cross, v_g_mem, v_w_cq, v_w_ckv, v_w_co, v_g_ffn, v_w_up, v_conv_w, v_conv_b, v_w_down, v_g_final):
    weights = dict(g_mix=g_mix, w_in=w_in, w_a2=w_a2, b_a=b_a, g_gla=g_gla, w_pool=w_pool,
                   pool_scale=pool_scale, w_branch=w_branch, w_out=w_out, g_cross=g_cross, g_mem=g_mem,
                   w_cq=w_cq, w_ckv=w_ckv, w_co=w_co, g_ffn=g_ffn, w_up=w_up, conv_w=conv_w,
                   conv_b=conv_b, w_down=w_down, g_final=g_final)
    mom_m = dict(g_mix=m_g_mix, w_in=m_w_in, w_a2=m_w_a2, b_a=m_b_a, g_gla=m_g_gla, w_pool=m_w_pool,
                 pool_scale=m_pool_scale, w_branch=m_w_branch, w_out=m_w_out, g_cross=m_g_cross,
                 g_mem=m_g_mem, w_cq=m_w_cq, w_ckv=m_w_ckv, w_co=m_w_co, g_ffn=m_g_ffn, w_up=m_w_up,
                 conv_w=m_conv_w, conv_b=m_conv_b, w_down=m_w_down, g_final=m_g_final)
    mom_v = dict(g_mix=v_g_mix, w_in=v_w_in, w_a2=v_w_a2, b_a=v_b_a, g_gla=v_g_gla, w_pool=v_w_pool,
                 pool_scale=v_pool_scale, w_branch=v_w_branch, w_out=v_w_out, g_cross=v_g_cross,
                 g_mem=v_g_mem, w_cq=v_w_cq, w_ckv=v_w_ckv, w_co=v_w_co, g_ffn=v_g_ffn, w_up=v_w_up,
                 conv_w=v_conv_w, conv_b=v_conv_b, w_down=v_w_down, g_final=v_g_final)
    order = list(weights)
    big = ["w_in", "w_branch", "w_out", "w_cq", "w_ckv", "w_co", "w_up", "w_down"]
    small_sharded = ["w_a2", "w_pool", "conv_w"]
    small_repl = ["g_mix", "b_a", "g_gla", "pool_scale", "g_cross", "g_mem", "g_ffn", "conv_b", "g_final"]

    xs, ms, tgt = x[0], mem[0], loss_target[0]
    T, D = xs.shape
    M = ms.shape[0]
    DK, DV, PW = b_a.shape[1], g_gla.shape[1], pool_scale.shape[1]
    RANK = w_a2.shape[1]
    F2 = conv_b.shape[1]
    F = F2 // 2
    DIN = N_CHIPS * w_in.shape[2]
    OFF_A = 2 * DK + 2 * DV
    OFF_P = OFF_A + RANK
    RP = LANES
    GW = PW // POOL_GROUPS
    assert PW == DV and 4 * DV == 2 * D and OFF_P + PW + 2 * D == DIN

    cx, cy, cc = lax.axis_index("x"), lax.axis_index("y"), lax.axis_index("c")
    chip = 2 * cx + cy
    c_idx = jnp.reshape(cc, (1,)).astype(jnp.int32)
    chip_idx = jnp.reshape(chip, (1,)).astype(jnp.int32)

    def halves(a):
        return a.reshape(2, a.shape[0] // 2, a.shape[1])

    shard2d = {k: (weights[k][0].T if k == "w_in" else weights[k][0]) for k in big}
    small_pack, small_offs = _pack([weights[k][0] for k in small_sharded], rows_multiple=32)
    flying, passing = {}, {}

    def gather_start(group, keys, tok):
        srcs = [small_pack if k == "small" else shard2d[k].astype(BF16) for k in keys]
        if group != "in":
            srcs = [a + tok[0:1, 0:1].astype(a.dtype) for a in srcs]
        srcs = [halves(a) for a in srcs]
        zones = [lax.empty((N_CHIPS, *s.shape), s.dtype) for s in srcs]
        first = (_near_copies, 2) if group == "in" else (_gather_copies, 3)
        s_sems, r_sems, srcs, zones, tok = _split_start(*first, srcs, zones, tok, name=f"gather_start_{group}")
        flying[group] = (keys, s_sems, r_sems, srcs, zones)
        return tok

    tok = gather_start("in", ["w_in"], xs)

    def arrive_in(after):
        keys, s_sems, r_sems, srcs, zones = flying["in"]
        near, diag = functools.partial(_pass_copies, pieces=(0, 1, 3)), functools.partial(_pass_copies, pieces=(2,))
        srcs, zones = _split_wait(_near_copies, s_sems, r_sems, srcs, zones, after, name="gather_wait_in")
        rs, rr, srcs, zones, tok = _split_start(_relay_copies, 2, srcs, zones, after, name="gather_relay_start_in")
        ns, nr, srcs, zones, tok = _split_start(near, 3, srcs, zones, tok, name="gather_pass_near_start_in")
        for group, group_keys in (("mix", ["w_branch", "w_out", "small"]), ("cross", ["w_cq", "w_ckv", "w_co"]),
                                  ("up", ["w_up"]), ("down", ["w_down"])):
            tok = gather_start(group, group_keys, tok)
        after = tok
        srcs, zones = _split_wait(_relay_copies, rs, rr, srcs, zones, after, name="gather_relay_wait_in")
        ds, dr, srcs, zones, _ = _split_start(diag, 1, srcs, zones, after, name="gather_pass_diag_start_in")
        srcs, zones = _split_wait(near, ns, nr, srcs, zones, after, name="gather_pass_near_wait_in")
        _, full = _split_wait(diag, ds, dr, srcs, zones, after, name="gather_pass_diag_wait_in")
        return {k: f.reshape(N_CHIPS, f.shape[1] * f.shape[2], f.shape[3]) for k, f in zip(keys, full)}

    def landed(group, after):
        keys, s_sems, r_sems, srcs, zones = flying[group]
        srcs, zones = _split_wait(_gather_copies, s_sems, r_sems, srcs, zones, after,
                                  name=f"gather_wait_{group}")
        s_sems, r_sems, srcs, zones, token = _split_start(_pass_copies, 4, srcs, zones, after,
                                                          name=f"gather_pass_start_{group}")
        passing[group] = (keys, s_sems, r_sems, srcs, zones)
        return token

    def arrive(group, after):
        keys, s_sems, r_sems, srcs, zones = passing[group]
        _, full = _split_wait(_pass_copies, s_sems, r_sems, srcs, zones, after,
                              name=f"gather_pass_wait_{group}")
        return {k: f.reshape(N_CHIPS, f.shape[1] * f.shape[2], f.shape[3]) for k, f in zip(keys, full)}

    def rows(g):
        return g.reshape(-1, g.shape[2])

    h1, r1 = _rms_fwd(xs, g_mix + tok[0:1, 0:1], name="norm_mix")
    W_in = rows(arrive_in(h1)["w_in"])
    W_main = jnp.concatenate([W_in[:OFF_A], W_in[OFF_P:]], axis=0)
    W_a = jnp.pad(W_in[OFF_A:OFF_P], ((0, RP - RANK), (0, 0)))
    tok = landed("mix", W_a)
    proj = _mm(h1, W_main, "nt", name="proj_main", out_dtype=F32, after=tok)
    gw = arrive("mix", proj)
    W_branch, W_out, small_all = rows(gw["w_branch"]), rows(gw["w_out"]), gw["small"]
    sm = [_unpack(small_all[j], small_offs, [weights[k].shape[1:] for k in small_sharded]) for j in range(N_CHIPS)]
    W_a2 = jnp.concatenate([sm[j][0] for j in range(N_CHIPS)], axis=1)
    W_a2p = jnp.pad(W_a2, ((0, RP - RANK), (0, 0))).astype(BF16)
    W_pool = jnp.concatenate([sm[j][1] for j in range(N_CHIPS)], axis=1).astype(BF16)
    W_conv = jnp.concatenate([sm[j][2] for j in range(N_CHIPS)], axis=1)

    a_pad = _mm(h1, W_a, "nt", name="proj_gate_rank", out_dtype=F32)
    o_gla, o_raw, states = _gla_fwd(proj, a_pad, W_a2p, b_a, g_gla, T=T, DK=DK, DV=DV)
    o_pool = _pool_fwd(proj, W_pool, pool_scale, T=T, PW=PW, col_block=3)
    tok = landed("cross", o_pool)
    y_gla = _mm(o_gla, W_branch, "nn", name="branch_gla", out_dtype=BF16, K=DV, after=tok)
    y_pool = _mm(o_pool, W_branch, "nn", name="branch_pool", out_dtype=BF16, K=PW, b_off=(DV, 0))
    merged = _merge_fwd(y_gla, y_pool, proj, T=T, D=D, col_block=2)
    x1 = _mm(merged, W_out, "nn", name="mix_out", out_dtype=F32, add=xs)

    h2, r2 = _rms_fwd(x1, g_cross, name="norm_cross")
    mem_n, rm = _rms_fwd(ms, g_mem, name="norm_mem")
    gw = arrive("cross", h2)
    W_cq, W_ckv, W_co = rows(gw["w_cq"]), gw["w_ckv"], rows(gw["w_co"])
    qc = _mm(h2, W_cq, "nn", name="cross_q", out_dtype=BF16)
    kv = _mm(mem_n, W_ckv, "nn", name="cross_kv", out_dtype=BF16, b_blocked=True)
    o_att = _attn_fwd(qc, kv, T=T, D=D, M=M)
    x2 = _mm(o_att, W_co, "nn", name="cross_out", out_dtype=F32, add=x1)

    tok = landed("up", x2)
    h3, r3 = _rms_fwd(x2, g_ffn + tok[0:1, 0:1], name="norm_ffn")
    W_up = arrive("up", h3)["w_up"]
    u0 = _mm(h3, W_up, "nn", name="ffn_up", out_dtype=F32, b_blocked=True)
    tok = landed("down", u0)
    f_act = _conv_fwd(u0, W_conv, conv_b + tok[0:1, 0:1], T=T, F=F)
    W_down = rows(arrive("down", f_act)["w_down"])
    x3 = _mm(f_act, W_down, "nn", name="ffn_down", out_dtype=F32, add=x2, tk=F // 2)

    loss_part, dx3, dx3_b, dg_final = _loss_head(x3, g_final.reshape(1, D), tgt)

    def col_shards(g):
        nb, K, Nb = g.shape
        return g.reshape(nb, 2, K // 2, Nb)

    def row_shards(g):
        R, N = g.shape
        return g.reshape(N_CHIPS, 2, R // N_CHIPS // 2, N)

    exchanging, in_flight = {}, []

    def exchange_start(group, keys, partials, after):
        recvs = [lax.empty((p.shape[0], *p.shape[2:]), p.dtype) for p in partials]
        s_sems, r_sems, partials, recvs, token = _split_start(
            _exchange_copies, 1, partials, recvs, after, name=f"grad_exchange_start_{group}")
        exchanging[group] = (keys, s_sems, r_sems, partials, recvs)
        return token

    def scatter_start(group, after):
        keys, s_sems, r_sems, partials, recvs = exchanging[group]
        partials, recvs = _split_wait(_exchange_copies, s_sems, r_sems, partials, recvs, after,
                                      name=f"grad_exchange_wait_{group}")
        chip_sums = [_add_halves(p, r, c_idx, name=f"grad_add_halves_{k}")
                     for k, p, r in zip(keys, partials, recvs)]
        lands = [lax.empty((3, *s.shape[1:]), s.dtype) for s in chip_sums]
        s_sems, r_sems, sums, lands, token = _split_start(
            _scatter_copies, 3, chip_sums, lands, after, name=f"grad_scatter_start_{group}")
        in_flight.append((group, keys, s_sems, r_sems, sums, lands))
        return token

    collected = []

    def collect(after):
        group, keys, s_sems, r_sems, sums, lands = in_flight.pop(0)
        sums, from_chips = _split_wait(_scatter_copies, s_sems, r_sems, sums, lands, after,
                                       name=f"grad_scatter_wait_{group}")
        half_sums = [_add_chips(s, r, chip_idx, name=f"grad_add_chips_{k}") for k, s, r in zip(keys, sums, from_chips)]
        others = [lax.empty(h.shape, h.dtype) for h in half_sums]
        s_sems, r_sems, half_sums, others, token = _split_start(
            _swap_copies, 1, half_sums, others, after, name=f"grad_swap_start_{group}")
        collected.append((keys, s_sems, r_sems, half_sums, others))
        return token

    df = _mm(dx3_b, W_down, "nt", name="d_ffn_act", out_dtype=BF16)
    dW_down = _mm(f_act, dx3_b, "tn", name="dw_down", out_dtype=BF16)
    du0, dconv_w, dconv_b = _conv_bwd(u0, W_conv, conv_b, df, T=T, F=F)
    dh3 = _mm(du0, W_up, "nt", name="d_ffn_in", out_dtype=F32, b_blocked=True, tk=F2 // N_CHIPS)
    dW_up = _mm(h3, du0, "tn", name="dw_up", out_dtype=BF16, out_blocks=N_CHIPS)
    tok = exchange_start("ffn", ["w_down", "w_up"], [row_shards(dW_down), col_shards(dW_up)], dh3)
    dx2, dx2_b, dg_ffn = _rms_bwd(dh3, x2, r3 + tok[0:1, 0:1], g_ffn, dx3, name="norm_ffn_bwd")

    do_att = _mm(dx2_b, W_co, "nt", name="d_cross_o", out_dtype=BF16)
    dW_co = _mm(o_att, dx2_b, "tn", name="dw_co", out_dtype=BF16)
    tok = scatter_start("ffn", dW_co)
    dq, dkv = _attn_bwd(qc, kv, do_att, T=T, D=D, M=M)
    dkv_b = dkv.astype(BF16)
    dW_cq = _mm(h2, dq, "tn", name="dw_cq", out_dtype=BF16, after=tok)
    dh2 = _mm(dq, W_cq, "nt", name="d_cross_in", out_dtype=F32)
    dW_ckv = _mm(mem_n, dkv_b, "tn", name="dw_ckv", out_dtype=BF16, out_blocks=N_CHIPS)
    dmem_n = _mm(dkv_b, W_ckv, "nt", name="d_mem", out_dtype=F32, b_blocked=True)
    tok = exchange_start("cross", ["w_co", "w_cq", "w_ckv"],
                         [row_shards(dW_co), row_shards(dW_cq), col_shards(dW_ckv)], dmem_n)
    _, _, dg_mem = _rms_bwd(dmem_n, ms, rm, g_mem, None, name="norm_mem_bwd")
    dx1, dx1_b, dg_cross = _rms_bwd(dh2, x1, r2 + tok[0:1, 0:1], g_cross, dx2, name="norm_cross_bwd")

    dmerged = _mm(dx1_b, W_out, "nt", name="d_merged", out_dtype=BF16)
    dW_out = _mm(merged, dx1_b, "tn", name="dw_out", out_dtype=BF16)
    tok = scatter_start("cross", dW_out)
    dy_gla, dy_pool, dgates = _merge_bwd(dmerged, y_gla, y_pool, proj, T=T, D=D, col_block=2)
    dW_br_gla = _mm(o_gla, dy_gla, "tn", name="dw_branch_gla", out_dtype=BF16, after=tok)
    dW_br_pool = _mm(o_pool, dy_pool, "tn", name="dw_branch_pool", out_dtype=BF16)
    do_gla = _mm(dy_gla, W_branch, "nt", name="d_o_gla", out_dtype=F32, N=DV)
    do_pool = _mm(dy_pool, W_branch, "nt", name="d_o_pool", out_dtype=F32, N=PW, b_off=(DV, 0))
    dp, dw_pool, dpool_scale = _pool_bwd(proj, W_pool, pool_scale, do_pool, T=T, PW=PW, col_block=3)
    dW_pool = jnp.transpose(dw_pool.reshape(POOL_GROUPS, N_CHIPS, GW // N_CHIPS, GW), (1, 0, 2, 3))
    tok = exchange_start("mix", ["w_out", "w_branch", "w_pool"],
                         [row_shards(dW_out), row_shards(jnp.concatenate([dW_br_gla, dW_br_pool], axis=0)),
                          row_shards(dW_pool.reshape(N_CHIPS * POOL_GROUPS * (GW // N_CHIPS), GW).astype(BF16))],
                         dp)
    dqkvr, da_pad, dw2, db_a, dg_gla = _gla_bwd(proj, a_pad, W_a2p, b_a + tok[0:1, 0:1], g_gla, o_raw, states,
                                               do_gla, T=T, DK=DK, DV=DV)
    tok = scatter_start("mix", dqkvr)
    dproj = jnp.concatenate([dqkvr, dp, dgates], axis=1)
    dW_main = _mm(dproj, h1, "tn", name="dw_in_main", out_dtype=BF16, after=tok)
    dW_a = _mm(da_pad, h1, "tn", name="dw_in_rank", out_dtype=BF16)
    dW_in = jnp.concatenate([dW_main[:OFF_A], dW_a[:RANK], dW_main[OFF_A:]], axis=0)
    tok = exchange_start("in", ["w_in"], [row_shards(dW_in)], dW_a)
    dh1 = _mm(dproj, W_main, "nn", name="d_mix_in_main", out_dtype=F32, after=tok)
    dh1 = _mm(da_pad, W_a, "nn", name="d_mix_in_rank", out_dtype=F32, add=dh1)
    dx0, _, dg_mix = _rms_bwd(dh1, xs, r1, g_mix, dx1, name="norm_mix_bwd")

    grads = {}

    small_grads = [loss_part, dg_mix, db_a, dg_gla, dpool_scale, dg_cross, dg_mem, dg_ffn, dconv_b, dg_final,
                   dw2[:RANK], dconv_w]
    small_buf, offs = _pack(small_grads)
    small_sum = _all_reduce_small(small_buf)
    red = _unpack(small_sum, offs, [g.shape for g in small_grads])
    loss = red[0][0, 0]
    for k, g in zip(small_repl, red[1:10]):
        grads[k] = g.reshape(weights[k].shape)
    nb = DK // N_CHIPS
    grads["w_a2"] = lax.dynamic_slice_in_dim(red[10], chip * nb, nb, axis=1)[None]
    nb = F2 // N_CHIPS
    grads["conv_w"] = lax.dynamic_slice_in_dim(red[11], chip * nb, nb, axis=1)[None]

    delta, new_m, new_v = {}, {}, {}

    def shard_rows(k, a):
        a = a[0]
        return a.T if k == "w_in" else a.reshape(-1, a.shape[-1])

    def whole(k, a):
        a = a.reshape(-1, a.shape[2])
        return (a.T if k == "w_in" else a).reshape(weights[k].shape)

    scatter_start("in", small_sum)

    def finish(after):
        keys, s_sems, r_sems, mine, others = collected.pop(0)
        mine, others = _split_wait(_swap_copies, s_sems, r_sems, mine, others, after,
                                   name=f"grad_swap_wait_{keys[0]}")
        for k, g_mine, g_other in zip(keys, mine, others):
            wmv = [halves(shard_rows(k, src[k])) for src in (weights, mom_m, mom_v)]
            res = _adamw_halves(*wmv, g_mine, g_other, c_idx, name=f"adamw_{k}")
            grads[k], delta[k], new_m[k], new_v[k] = (whole(k, a) for a in res)
        return res[1]

    after = in_flight[-1][4][0]
    while in_flight:
        after = collect(after)
        while len(collected) > 1:
            after = finish(after)
    finish(after)
    small = small_repl + ["w_a2", "conv_w"]
    packs = [_pack([src[k] for k in small])[0] for src in (weights, grads, mom_m, mom_v)]
    _, offs = _pack([weights[k] for k in small])
    outs = _adamw(*packs, name="adamw_small")
    for res, o in zip((delta, new_m, new_v), outs):
        for k, a in zip(small, _unpack(o, offs, [weights[k].shape for k in small])):
            res[k] = a

    return (loss, dx0[None], *[grads[k] for k in order], *[delta[k] for k in order],
            *[new_m[k] for k in order], *[new_v[k] for k in order])
```

```python
import functools

import jax
import jax.numpy as jnp
from jax import lax
from jax.experimental import pallas as pl
from jax.experimental.pallas import tpu as pltpu

F32 = jnp.float32
BF16 = jnp.bfloat16
MESH = pl.DeviceIdType.MESH
HIGHEST = lax.Precision.HIGHEST

EPS = 1e-6
GLA_HEADS = 4
GLA_CHUNK = 128
GLA_GATE_NORM = 16.0
POOL_GROUPS = 4
CROSS_HEADS = 4
CONV_W = 3
N_CHIPS = 4
LANES = 128
SUBLANES = 8
VMEM_LIMIT = 56 << 20

ADAM_LR = 0.001
ADAM_B1 = 0.9
ADAM_B2 = 0.999
ADAM_EPS = 1e-08
ADAM_WD = 0.01
ADAM_STEP = 10

NN = (((1,), (0,)), ((), ()))
NT = (((1,), (1,)), ((), ()))
TN = (((0,), (0,)), ((), ()))


CHUNK_PRECISION = lax.Precision.HIGH


def _dot(a, b, dn=NN, precision=None):
    return lax.dot_general(a, b, dn, precision=precision, preferred_element_type=F32)


def _tile(n, pref, align=LANES):
    t = (min(pref, n) // align) * align
    while t >= align:
        if n % t == 0:
            return t
        t -= align
    return n


def _pcall(body, *, name, out_shape, grid=(), in_specs=None, out_specs=None, scratch_shapes=(),
           semantics=None, prefetch=0, aliases=None, split_copy=False):
    params = dict(vmem_limit_bytes=VMEM_LIMIT)
    if semantics is not None:
        params["dimension_semantics"] = semantics
    if split_copy:
        params["has_side_effects"] = pltpu.SideEffectType.DATAFLOW_SIDE_EFFECTING
    if prefetch:
        grid_spec = pltpu.PrefetchScalarGridSpec(
            num_scalar_prefetch=prefetch, grid=grid, in_specs=in_specs, out_specs=out_specs,
            scratch_shapes=scratch_shapes)
        return pl.pallas_call(body, name=name, out_shape=out_shape, grid_spec=grid_spec,
                              compiler_params=pltpu.CompilerParams(**params))
    kw = {}
    if aliases is not None:
        kw["input_output_aliases"] = aliases
    if in_specs is not None:
        kw["in_specs"] = in_specs
    if out_specs is not None:
        kw["out_specs"] = out_specs
    return pl.pallas_call(body, name=name, out_shape=out_shape, grid=grid,
                          scratch_shapes=scratch_shapes,
                          compiler_params=pltpu.CompilerParams(**params), **kw)


def _sigmoid(x):
    return 1.0 / (1.0 + jnp.exp(-x))


def _log_sigmoid(x):
    return jnp.minimum(x, 0.0) - jnp.log(1.0 + jnp.exp(-jnp.abs(x)))


def _mm(a, b, mode, *, name, out_dtype, M=None, N=None, K=None, a_off=(0, 0), b_off=(0, 0),
        add=None, b_blocked=False, out_blocks=0, after=None, tm=1536, tn=1536, tk=2048):
    if b_blocked:
        nb, R, Cb = b.shape
        b_rows, b_cols = R, nb * Cb
    else:
        b_rows, b_cols = b.shape
    if mode == "nn":
        M = M or a.shape[0]; K = K or a.shape[1]; N = N or b_cols
    elif mode == "nt":
        M = M or a.shape[0]; K = K or a.shape[1]; N = N or b_rows
    else:
        K = K or a.shape[0]; M = M or a.shape[1]; N = N or b_cols
    tm = _tile(M, tm, LANES if mode == "tn" else 16)
    if mode == "tn" and not out_blocks:
        tn = max(tn, 2048)
    tn = _tile(Cb if (b_blocked and mode != "nt") else (N // out_blocks if out_blocks else N), tn)
    tk = _tile(Cb if (b_blocked and mode == "nt") else K, tk)
    nk = K // tk
    dn = {"nn": NN, "nt": NT, "tn": TN}[mode]

    def off(o, t):
        assert o % t == 0, (name, o, t)
        return o // t

    if mode == "tn":
        ar, ac = off(a_off[0], tk), off(a_off[1], tm)
        a_spec = pl.BlockSpec((tk, tm), lambda i, j, k: (k + ar, i + ac))
    else:
        ar, ac = off(a_off[0], tm), off(a_off[1], tk)
        a_spec = pl.BlockSpec((tm, tk), lambda i, j, k: (i + ar, k + ac))
    if b_blocked and mode == "nt":
        per = Cb // tk
        b_spec = pl.BlockSpec((None, tn, tk), lambda i, j, k: (k // per, j, k % per))
    elif b_blocked:
        per = Cb // tn
        b_spec = pl.BlockSpec((None, tk, tn), lambda i, j, k: (j // per, k, j % per))
    elif mode == "nt":
        br, bc = off(b_off[0], tn), off(b_off[1], tk)
        b_spec = pl.BlockSpec((tn, tk), lambda i, j, k: (j + br, k + bc))
    else:
        br, bc = off(b_off[0], tk), off(b_off[1], tn)
        b_spec = pl.BlockSpec((tk, tn), lambda i, j, k: (k + br, j + bc))
    if out_blocks:
        per_o = N // out_blocks // tn
        o_spec = pl.BlockSpec((None, tm, tn), lambda i, j, k: (j // per_o, i, j % per_o))
        out_shape = jax.ShapeDtypeStruct((out_blocks, M, N // out_blocks), out_dtype)
    else:
        o_spec = pl.BlockSpec((tm, tn), lambda i, j, k: (i, j))
        out_shape = jax.ShapeDtypeStruct((M, N), out_dtype)
    in_specs = [a_spec, b_spec]
    args = [a, b]
    if add is not None:
        assert not out_blocks
        in_specs.append(o_spec)
        args.append(add)
    if after is not None:
        in_specs.append(pl.BlockSpec(memory_space=pl.ANY))
        args.append(after)
    n_in = len(args)

    def finish(r, refs):
        if add is not None:
            r = r + refs[2][...]
        o_ref = refs[n_in]
        o_ref[...] = r.astype(o_ref.dtype)

    def body_one(*refs):
        finish(_dot(refs[0][...].astype(BF16), refs[1][...].astype(BF16), dn), refs)

    def body_acc(*refs):
        acc_ref = refs[-1]
        k = pl.program_id(2)

        @pl.when(k == 0)
        def _():
            acc_ref[...] = jnp.zeros_like(acc_ref)

        acc_ref[...] += _dot(refs[0][...].astype(BF16), refs[1][...].astype(BF16), dn)

        @pl.when(k == nk - 1)
        def _():
            finish(acc_ref[...], refs)

    return _pcall(body_one if nk == 1 else body_acc, name=name, out_shape=out_shape,
                  grid=(M // tm, N // tn, nk), in_specs=in_specs, out_specs=o_spec,
                  scratch_shapes=[] if nk == 1 else [pltpu.VMEM((tm, tn), F32)],
                  semantics=("parallel", "parallel", "arbitrary"))(*args)


def _rms_fwd(x, g, *, name):
    T, D = x.shape
    tr = _tile(T, 128, 16)

    def body(x_ref, g_ref, h_ref, r_ref):
        xv = x_ref[...]
        r = lax.rsqrt(jnp.mean(xv * xv, axis=-1, keepdims=True) + EPS)
        h_ref[...] = (xv * r * g_ref[...]).astype(h_ref.dtype)
        r_ref[...] = r

    row = pl.BlockSpec((tr, D), lambda i: (i, 0))
    return _pcall(body, name=name,
                  out_shape=(jax.ShapeDtypeStruct((T, D), BF16), jax.ShapeDtypeStruct((T, 1), F32)),
                  grid=(T // tr,),
                  in_specs=[row, pl.BlockSpec((1, D), lambda i: (0, 0))],
                  out_specs=(row, pl.BlockSpec((tr, 1), lambda i: (i, 0))),
                  semantics=("parallel",))(x, g)


def _rms_bwd(dh, x, rstd, g, dres, *, name):
    T, D = x.shape
    tr = _tile(T, 128, 16)
    has_res = dres is not None

    def body(*refs):
        if has_res:
            dh_ref, x_ref, r_ref, g_ref, res_ref, dx_ref, dxb_ref, dg_ref = refs
        else:
            dh_ref, x_ref, r_ref, g_ref, dx_ref, dxb_ref, dg_ref = refs
        r = r_ref[...]
        xh = x_ref[...] * r
        dhv = dh_ref[...].astype(F32)
        dxh = dhv * g_ref[...]
        m = jnp.mean(dxh * xh, axis=-1, keepdims=True)
        dx = r * (dxh - xh * m)
        if has_res:
            dx = dx + res_ref[...]
        dx_ref[...] = dx
        dxb_ref[...] = dx.astype(BF16)

        @pl.when(pl.program_id(0) == 0)
        def _():
            dg_ref[...] = jnp.zeros_like(dg_ref)

        dg_ref[...] += jnp.sum(dhv * xh, axis=0, keepdims=True)

    row = pl.BlockSpec((tr, D), lambda i: (i, 0))
    vec = pl.BlockSpec((1, D), lambda i: (0, 0))
    in_specs = [row, row, pl.BlockSpec((tr, 1), lambda i: (i, 0)), vec]
    args = [dh, x, rstd, g]
    if has_res:
        in_specs.append(row)
        args.append(dres)
    return _pcall(body, name=name,
                  out_shape=(jax.ShapeDtypeStruct((T, D), F32), jax.ShapeDtypeStruct((T, D), BF16),
                             jax.ShapeDtypeStruct((1, D), F32)),
                  grid=(T // tr,), in_specs=in_specs, out_specs=(row, row, vec),
                  semantics=("arbitrary",))(*args)


def _loss_head(x3, g, tgt):
    T, D = x3.shape
    tr = _tile(T, 128, 16)

    def body(x_ref, g_ref, t_ref, loss_ref, dx_ref, dxb_ref, dg_ref):
        xv = x_ref[...]
        gv = g_ref[...]
        r = lax.rsqrt(jnp.mean(xv * xv, axis=-1, keepdims=True) + EPS)
        xh = xv * r
        err = xh * gv - t_ref[...]
        dy = err * (1.0 / D)
        dxh = dy * gv
        m = jnp.mean(dxh * xh, axis=-1, keepdims=True)
        dx = r * (dxh - xh * m)
        dx_ref[...] = dx
        dxb_ref[...] = dx.astype(BF16)

        @pl.when(pl.program_id(0) == 0)
        def _():
            dg_ref[...] = jnp.zeros_like(dg_ref)
            loss_ref[...] = jnp.zeros_like(loss_ref)

        dg_ref[...] += jnp.sum(dy * xh, axis=0, keepdims=True)
        part = 0.5 * jnp.sum(jnp.mean(err * err, axis=-1, keepdims=True), axis=0, keepdims=True)
        loss_ref[...] += jnp.broadcast_to(part, loss_ref.shape)

    row = pl.BlockSpec((tr, D), lambda i: (i, 0))
    vec = pl.BlockSpec((1, D), lambda i: (0, 0))
    return _pcall(body, name="loss_head",
                  out_shape=(jax.ShapeDtypeStruct((1, LANES), F32), jax.ShapeDtypeStruct((T, D), F32),
                             jax.ShapeDtypeStruct((T, D), BF16), jax.ShapeDtypeStruct((1, D), F32)),
                  grid=(T // tr,), in_specs=[row, vec, row],
                  out_specs=(pl.BlockSpec((1, LANES), lambda i: (0, 0)), row, row, vec),
                  semantics=("arbitrary",))(x3, g, tgt)


def _gla_chunk_terms(qk, a_ref, w2_ref, ba_ref, DK):
    C = qk.shape[0]
    gp = _dot(a_ref[...].astype(BF16), w2_ref[...]) + ba_ref[...]
    la = _log_sigmoid(gp) * (1.0 / GLA_GATE_NORM)
    row = lax.broadcasted_iota(jnp.int32, (C, C), 0)
    col = lax.broadcasted_iota(jnp.int32, (C, C), 1)
    causal = row >= col
    b = _dot(causal.astype(F32), la, precision=HIGHEST)
    return gp, b, causal


def _gla_fwd(proj, a_pad, w2, b_a, g_gla, *, T, DK, DV):
    assert 2 * DK == DV
    H = GLA_HEADS
    HK, HV = DK // H, DV // H
    C = GLA_CHUNK
    n = T // C
    RP = a_pad.shape[1]
    scale = HK ** -0.5

    def body(qk_ref, v_ref, r_ref, a_ref, w2_ref, ba_ref, gg_ref, og_ref, oraw_ref, st_ref, s_ref):
        @pl.when(pl.program_id(0) == 0)
        def _():
            s_ref[...] = jnp.zeros_like(s_ref)

        st_ref[...] = s_ref[...]
        qk = qk_ref[...]
        _, b, causal = _gla_chunk_terms(qk, a_ref, w2_ref, ba_ref, DK)
        for h in range(H):
            ks = slice(h * HK, (h + 1) * HK)
            vs = slice(h * HV, (h + 1) * HV)
            bh = b[:, ks]
            b_last = bh[C - 1:C, :]
            qt = qk[:, ks] * scale * jnp.exp(bh)
            kh = qk[:, DK + h * HK:DK + (h + 1) * HK]
            kt = kh * jnp.exp(-bh)
            khat = kh * jnp.exp(b_last - bh)
            a_mat = jnp.where(causal, _dot(qt, kt, NT, CHUNK_PRECISION), 0.0)
            vh = v_ref[:, vs]
            s_t = s_ref[h]
            o = _dot(a_mat, vh, NN, CHUNK_PRECISION) + _dot(qt, s_t, NT, CHUNK_PRECISION)
            s_ref[h] = s_t * jnp.exp(b_last) + _dot(vh, khat, TN, CHUNK_PRECISION)
            rs = lax.rsqrt(jnp.mean(o * o, axis=-1, keepdims=True) + EPS)
            rr = r_ref[:, vs]
            og = o * rs * gg_ref[:, vs] * (rr * _sigmoid(rr))
            oraw_ref[:, vs] = o
            og_ref[:, vs] = og.astype(BF16)

    blk = lambda j: pl.BlockSpec((C, DV), lambda i: (i, j))
    full = lambda s: pl.BlockSpec(s, lambda i: (0,) * len(s))
    return _pcall(
        body, name="gla_fwd",
        out_shape=(jax.ShapeDtypeStruct((T, DV), BF16), jax.ShapeDtypeStruct((T, DV), F32),
                   jax.ShapeDtypeStruct((n, H, HV, HK), F32)),
        grid=(n,),
        in_specs=[blk(0), blk(1), blk(2), pl.BlockSpec((C, RP), lambda i: (i, 0)),
                  full((RP, DK)), full((1, DK)), full((1, DV))],
        out_specs=(blk(0), blk(0), pl.BlockSpec((None, H, HV, HK), lambda i: (i, 0, 0, 0))),
        scratch_shapes=[pltpu.VMEM((H, HV, HK), F32)],
        semantics=("arbitrary",))(proj, proj, proj, a_pad, w2, b_a, g_gla)


def _gla_bwd(proj, a_pad, w2, b_a, g_gla, o_raw, states, do_gla, *, T, DK, DV):
    H = GLA_HEADS
    HK, HV = DK // H, DV // H
    C = GLA_CHUNK
    n = T // C
    RP = a_pad.shape[1]
    scale = HK ** -0.5

    def body(qk_ref, v_ref, r_ref, a_ref, w2_ref, ba_ref, gg_ref, oraw_ref, st_ref, dog_ref,
             dqkvr_ref, da_ref, dw2_ref, dba_ref, dgg_ref, ds_ref):
        @pl.when(pl.program_id(0) == 0)
        def _():
            ds_ref[...] = jnp.zeros_like(ds_ref)
            dw2_ref[...] = jnp.zeros_like(dw2_ref)
            dba_ref[...] = jnp.zeros_like(dba_ref)
            dgg_ref[...] = jnp.zeros_like(dgg_ref)

        qk = qk_ref[...]
        gp, b, causal = _gla_chunk_terms(qk, a_ref, w2_ref, ba_ref, DK)
        row = lax.broadcasted_iota(jnp.int32, (C, C), 0)
        col = lax.broadcasted_iota(jnp.int32, (C, C), 1)
        upper = (col >= row).astype(F32)
        dla_parts = []
        for h in range(H):
            ks = slice(h * HK, (h + 1) * HK)
            vs = slice(h * HV, (h + 1) * HV)
            bh = b[:, ks]
            b_last = bh[C - 1:C, :]
            eb = jnp.exp(bh)
            emb = jnp.exp(-bh)
            ehat = jnp.exp(b_last - bh)
            e_last = jnp.exp(b_last)
            qt = qk[:, ks] * scale * eb
            kh = qk[:, DK + h * HK:DK + (h + 1) * HK]
            kt = kh * emb
            khat = kh * ehat
            a_mat = jnp.where(causal, _dot(qt, kt, NT, CHUNK_PRECISION), 0.0)
            vh = v_ref[:, vs]
            o = oraw_ref[:, vs]
            rs = lax.rsqrt(jnp.mean(o * o, axis=-1, keepdims=True) + EPS)
            on = o * rs
            gg = gg_ref[:, vs]
            rr = r_ref[:, vs]
            sg = _sigmoid(rr)
            d_out = dog_ref[:, vs]
            dr = d_out * (on * gg) * (sg * (1.0 + rr * (1.0 - sg)))
            d_og = d_out * (rr * sg)
            dgg_ref[:, vs] += jnp.sum(d_og * on, axis=0, keepdims=True)
            d_on = d_og * gg
            d_o = rs * (d_on - on * jnp.mean(d_on * on, axis=-1, keepdims=True))
            s_t = st_ref[h]
            ds_t = ds_ref[h]
            d_a = jnp.where(causal, _dot(d_o, vh, NT, CHUNK_PRECISION), 0.0)
            dv = _dot(a_mat, d_o, TN, CHUNK_PRECISION) + _dot(khat, ds_t, NT, CHUNK_PRECISION)
            dqt = _dot(d_a, kt, NN, CHUNK_PRECISION) + _dot(d_o, s_t, NN, CHUNK_PRECISION)
            dkt = _dot(d_a, qt, TN, CHUNK_PRECISION)
            dkhat = _dot(vh, ds_t, NN, CHUNK_PRECISION)
            ds_ref[h] = ds_t * e_last + _dot(d_o, qt, TN, CHUNK_PRECISION)
            dq = dqt * eb * scale
            dk = dkt * emb + dkhat * ehat
            db = dqt * qt - dkt * kt - dkhat * khat
            d_last = (jnp.sum(dkhat * khat, axis=0, keepdims=True)
                      + e_last * jnp.sum(ds_t * s_t, axis=0, keepdims=True))
            dla_parts.append(_dot(upper, db, NN, HIGHEST) + d_last)
            dqkvr_ref[:, ks] = dq.astype(BF16)
            dqkvr_ref[:, DK + h * HK:DK + (h + 1) * HK] = dk.astype(BF16)
            dqkvr_ref[:, DV + h * HV:DV + (h + 1) * HV] = dv.astype(BF16)
            dqkvr_ref[:, 2 * DV + h * HV:2 * DV + (h + 1) * HV] = dr.astype(BF16)
        dla = jnp.concatenate(dla_parts, axis=1)
        dgp = dla * (1.0 / GLA_GATE_NORM) * _sigmoid(-gp)
        dba_ref[...] += jnp.sum(dgp, axis=0, keepdims=True)
        dgp_b = dgp.astype(BF16)
        dw2_ref[...] += _dot(a_ref[...].astype(BF16), dgp_b, TN)
        da_ref[...] = _dot(dgp_b, w2_ref[...], NT).astype(BF16)

    rev = lambda j: pl.BlockSpec((C, DV), lambda i: (n - 1 - i, j))
    full = lambda s: pl.BlockSpec(s, lambda i: (0,) * len(s))
    return _pcall(
        body, name="gla_bwd",
        out_shape=(jax.ShapeDtypeStruct((T, 3 * DV), BF16), jax.ShapeDtypeStruct((T, RP), BF16),
                   jax.ShapeDtypeStruct((RP, DK), F32), jax.ShapeDtypeStruct((1, DK), F32),
                   jax.ShapeDtypeStruct((1, DV), F32)),
        grid=(n,),
        in_specs=[rev(0), rev(1), rev(2), pl.BlockSpec((C, RP), lambda i: (n - 1 - i, 0)),
                  full((RP, DK)), full((1, DK)), full((1, DV)), rev(0),
                  pl.BlockSpec((None, H, HV, HK), lambda i: (n - 1 - i, 0, 0, 0)), rev(0)],
        out_specs=(pl.BlockSpec((C, 3 * DV), lambda i: (n - 1 - i, 0)),
                   pl.BlockSpec((C, RP), lambda i: (n - 1 - i, 0)),
                   full((RP, DK)), full((1, DK)), full((1, DV))),
        scratch_shapes=[pltpu.VMEM((H, HV, HK), F32)],
        semantics=("arbitrary",))(proj, proj, proj, a_pad, w2, b_a, g_gla, o_raw, states, do_gla)


def _pool_windows(p, g, T):
    t = lax.broadcasted_iota(jnp.int32, (T, 1), 0)
    s = p
    for lvl in range(POOL_GROUPS):
        sh = 1 << lvl
        nxt = s + jnp.where(t >= sh, pltpu.roll(s, sh, 0), 0.0)
        s = jnp.where(lvl <= g, nxt, s)
    win = jnp.left_shift(2, g)
    inv = 1.0 / jnp.minimum(t + 1, win).astype(F32)
    return s * inv - p, inv


def _pool_fwd(proj, w_pool, scale, *, T, PW, col_block):
    GW = PW // POOL_GROUPS
    per = PW // GW

    def body(p_ref, w_ref, s_ref, o_ref):
        g = pl.program_id(0)
        pooled, _ = _pool_windows(p_ref[...], g, T)
        mixed = _dot(pooled.astype(BF16), w_ref[...])
        o_ref[...] = (mixed * s_ref[...]).astype(BF16)

    return _pcall(body, name="pool_fwd", out_shape=jax.ShapeDtypeStruct((T, PW), BF16),
                  grid=(POOL_GROUPS,),
                  in_specs=[pl.BlockSpec((T, GW), lambda g: (0, col_block * per + g)),
                            pl.BlockSpec((None, GW, GW), lambda g: (g, 0, 0)),
                            pl.BlockSpec((1, GW), lambda g: (0, g))],
                  out_specs=pl.BlockSpec((T, GW), lambda g: (0, g)),
                  semantics=("parallel",))(proj, w_pool, scale)


def _pool_bwd(proj, w_pool, scale, do_pool, *, T, PW, col_block):
    GW = PW // POOL_GROUPS
    per = PW // GW

    def body(p_ref, w_ref, s_ref, do_ref, dp_ref, dw_ref, dsc_ref):
        g = pl.program_id(0)
        pooled, inv = _pool_windows(p_ref[...], g, T)
        pooled_b = pooled.astype(BF16)
        w = w_ref[...]
        mixed = _dot(pooled_b, w)
        d_out = do_ref[...]
        dsc_ref[...] = jnp.sum(d_out * mixed, axis=0, keepdims=True)
        dmixed = (d_out * s_ref[...]).astype(BF16)
        dw_ref[...] = _dot(pooled_b, dmixed, TN)
        dpooled = _dot(dmixed, w, NT)
        t = lax.broadcasted_iota(jnp.int32, (T, 1), 0)
        s = dpooled * inv
        for lvl in range(POOL_GROUPS):
            sh = 1 << lvl
            nxt = s + jnp.where(t < T - sh, pltpu.roll(s, T - sh, 0), 0.0)
            s = jnp.where(lvl <= g, nxt, s)
        dp_ref[...] = (s - dpooled).astype(BF16)

    return _pcall(body, name="pool_bwd",
                  out_shape=(jax.ShapeDtypeStruct((T, PW), BF16),
                             jax.ShapeDtypeStruct((POOL_GROUPS, GW, GW), F32),
                             jax.ShapeDtypeStruct((1, PW), F32)),
                  grid=(POOL_GROUPS,),
                  in_specs=[pl.BlockSpec((T, GW), lambda g: (0, col_block * per + g)),
                            pl.BlockSpec((None, GW, GW), lambda g: (g, 0, 0)),
                            pl.BlockSpec((1, GW), lambda g: (0, g)),
                            pl.BlockSpec((T, GW), lambda g: (0, g))],
                  out_specs=(pl.BlockSpec((T, GW), lambda g: (0, g)),
                             pl.BlockSpec((None, GW, GW), lambda g: (g, 0, 0)),
                             pl.BlockSpec((1, GW), lambda g: (0, g))),
                  semantics=("parallel",))(proj, w_pool, scale, do_pool)


def _merge_fwd(y_gla, y_pool, proj, *, T, D, col_block):
    tr = _tile(T, 128, 16)

    def body(yg_ref, yp_ref, g1_ref, g2_ref, o_ref):
        o_ref[...] = (_sigmoid(g1_ref[...]) * yg_ref[...]
                      + _sigmoid(g2_ref[...]) * yp_ref[...]).astype(BF16)

    row = pl.BlockSpec((tr, D), lambda i: (i, 0))
    return _pcall(body, name="merge_fwd", out_shape=jax.ShapeDtypeStruct((T, D), BF16),
                  grid=(T // tr,),
                  in_specs=[row, row, pl.BlockSpec((tr, D), lambda i: (i, col_block)),
                            pl.BlockSpec((tr, D), lambda i: (i, col_block + 1))],
                  out_specs=row, semantics=("parallel",))(y_gla, y_pool, proj, proj)


def _merge_bwd(dmerged, y_gla, y_pool, proj, *, T, D, col_block):
    tr = _tile(T, 128, 16)

    def body(dm_ref, yg_ref, yp_ref, g1_ref, g2_ref, dyg_ref, dyp_ref, dg_ref):
        dm = dm_ref[...]
        s1 = _sigmoid(g1_ref[...])
        s2 = _sigmoid(g2_ref[...])
        dyg_ref[...] = (dm * s1).astype(BF16)
        dyp_ref[...] = (dm * s2).astype(BF16)
        dg_ref[:, :D] = (dm * yg_ref[...] * s1 * (1.0 - s1)).astype(BF16)
        dg_ref[:, D:] = (dm * yp_ref[...] * s2 * (1.0 - s2)).astype(BF16)

    row = pl.BlockSpec((tr, D), lambda i: (i, 0))
    return _pcall(body, name="merge_bwd",
                  out_shape=(jax.ShapeDtypeStruct((T, D), BF16), jax.ShapeDtypeStruct((T, D), BF16),
                             jax.ShapeDtypeStruct((T, 2 * D), BF16)),
                  grid=(T // tr,),
                  in_specs=[row, row, row, pl.BlockSpec((tr, D), lambda i: (i, col_block)),
                            pl.BlockSpec((tr, D), lambda i: (i, col_block + 1))],
                  out_specs=(row, row, pl.BlockSpec((tr, 2 * D), lambda i: (i, 0))),
                  semantics=("parallel",))(dmerged, y_gla, y_pool, proj, proj)


def _attn_fwd(q, kv, *, T, D, M):
    H = CROSS_HEADS
    HD = D // H
    tq = _tile(T, 512, 16)
    scale = HD ** -0.5

    def body(q_ref, kv_ref, o_ref):
        for h in range(H):
            hs = slice(h * HD, (h + 1) * HD)
            s = _dot(q_ref[:, hs], kv_ref[:, hs], NT) * scale
            e = jnp.exp(s - jnp.max(s, axis=-1, keepdims=True))
            p = e / jnp.sum(e, axis=-1, keepdims=True)
            o_ref[:, hs] = _dot(p.astype(BF16), kv_ref[:, D + h * HD:D + (h + 1) * HD]).astype(BF16)

    row = pl.BlockSpec((tq, D), lambda i: (i, 0))
    return _pcall(body, name="attn_fwd", out_shape=jax.ShapeDtypeStruct((T, D), BF16),
                  grid=(T // tq,), in_specs=[row, pl.BlockSpec((M, 2 * D), lambda i: (0, 0))],
                  out_specs=row, semantics=("parallel",))(q, kv)


def _attn_bwd(q, kv, do, *, T, D, M):
    H = CROSS_HEADS
    HD = D // H
    tq = _tile(T, 512, 16)
    scale = HD ** -0.5

    def body(q_ref, kv_ref, do_ref, dq_ref, dkv_ref):
        @pl.when(pl.program_id(0) == 0)
        def _():
            dkv_ref[...] = jnp.zeros_like(dkv_ref)

        for h in range(H):
            hs = slice(h * HD, (h + 1) * HD)
            vs = slice(D + h * HD, D + (h + 1) * HD)
            qh = q_ref[:, hs]
            kh = kv_ref[:, hs]
            s = _dot(qh, kh, NT) * scale
            e = jnp.exp(s - jnp.max(s, axis=-1, keepdims=True))
            p = e / jnp.sum(e, axis=-1, keepdims=True)
            p_b = p.astype(BF16)
            d_o = do_ref[:, hs]
            dkv_ref[:, vs] += _dot(p_b, d_o, TN)
            dp = _dot(d_o, kv_ref[:, vs], NT)
            ds = (p * (dp - jnp.sum(dp * p, axis=-1, keepdims=True)) * scale).astype(BF16)
            dq_ref[:, hs] = _dot(ds, kh).astype(BF16)
            dkv_ref[:, hs] += _dot(ds, qh, TN)

    row = pl.BlockSpec((tq, D), lambda i: (i, 0))
    full = pl.BlockSpec((M, 2 * D), lambda i: (0, 0))
    return _pcall(body, name="attn_bwd",
                  out_shape=(jax.ShapeDtypeStruct((T, D), BF16), jax.ShapeDtypeStruct((M, 2 * D), F32)),
                  grid=(T // tq,), in_specs=[row, full, row], out_specs=(row, full),
                  semantics=("arbitrary",))(q, kv, do)


def _shift_down(x, halo, s):
    out = pltpu.roll(x, s, 0)
    t8 = lax.broadcasted_iota(jnp.int32, (SUBLANES, 1), 0)
    head = out[:SUBLANES]
    for j in range(s):
        head = jnp.where(t8 == j, halo[SUBLANES - s + j:SUBLANES - s + j + 1, :], head)
    return head if x.shape[0] == SUBLANES else jnp.concatenate([head, out[SUBLANES:]], axis=0)


def _shift_up(x, halo, s):
    rows = x.shape[0]
    out = pltpu.roll(x, rows - s, 0)
    t8 = lax.broadcasted_iota(jnp.int32, (SUBLANES, 1), 0)
    tail = out[rows - SUBLANES:]
    for j in range(s):
        tail = jnp.where(t8 == SUBLANES - s + j, halo[j:j + 1, :], tail)
    return jnp.concatenate([out[:rows - SUBLANES], tail], axis=0)


def _conv_tiles(T):
    tt = _tile(T, 128, SUBLANES)
    return tt, tt // SUBLANES, T // SUBLANES


def _conv_fwd(u0, conv_w, conv_b, *, T, F):
    tt, hb, _ = _conv_tiles(T)
    cw = _tile(F, LANES)

    def body(u_ref, prev_ref, w_ref, b_ref, f_ref):
        i = pl.program_id(0)

        def conv(cs):
            x = u_ref[:, cs]
            halo = jnp.where(i > 0, prev_ref[:, cs], 0.0)
            return (w_ref[2:3, cs] * x + w_ref[1:2, cs] * _shift_down(x, halo, 1)
                    + w_ref[0:1, cs] * _shift_down(x, halo, 2) + b_ref[:, cs])

        for j in range(F // cw):
            gate = conv(slice(j * cw, (j + 1) * cw))
            val = conv(slice(F + j * cw, F + (j + 1) * cw))
            f_ref[:, j * cw:(j + 1) * cw] = (gate * _sigmoid(gate) * val).astype(BF16)

    return _pcall(body, name="conv_fwd", out_shape=jax.ShapeDtypeStruct((T, F), BF16),
                  grid=(T // tt,),
                  in_specs=[pl.BlockSpec((tt, 2 * F), lambda i: (i, 0)),
                            pl.BlockSpec((SUBLANES, 2 * F), lambda i: (jnp.maximum(i * hb - 1, 0), 0)),
                            pl.BlockSpec((CONV_W, 2 * F), lambda i: (0, 0)),
                            pl.BlockSpec((1, 2 * F), lambda i: (0, 0))],
                  out_specs=pl.BlockSpec((tt, F), lambda i: (i, 0)),
                  semantics=("parallel",))(u0, u0, conv_w, conv_b)


def _conv_bwd(u0, conv_w, conv_b, df, *, T, F):
    tt, hb, nb = _conv_tiles(T)
    nt = T // tt
    cw = _tile(F, LANES)

    def body(u_ref, prev_ref, next_ref, df_ref, dfn_ref, w_ref, b_ref, du0_ref, dw_ref, db_ref):
        i = pl.program_id(0)

        @pl.when(i == 0)
        def _():
            dw_ref[...] = jnp.zeros_like(dw_ref)
            db_ref[...] = jnp.zeros_like(db_ref)

        def conv(cs):
            x = u_ref[:, cs]
            halo = jnp.where(i > 0, prev_ref[:, cs], 0.0)
            x1 = _shift_down(x, halo, 1)
            x2 = _shift_down(x, halo, 2)
            u = w_ref[2:3, cs] * x + w_ref[1:2, cs] * x1 + w_ref[0:1, cs] * x2 + b_ref[:, cs]
            xn = next_ref[:, cs]
            tail = x[tt - SUBLANES:, :]
            un = (w_ref[2:3, cs] * xn + w_ref[1:2, cs] * _shift_down(xn, tail, 1)
                  + w_ref[0:1, cs] * _shift_down(xn, tail, 2) + b_ref[:, cs])
            return u, un, (x, x1, x2)

        def glu_grad(gate, val, dff):
            sg = _sigmoid(gate)
            return dff * val * (sg * (1.0 + gate * (1.0 - sg))), dff * (gate * sg)

        def finish(cs, du, dun, xs):
            du0 = (w_ref[2:3, cs] * du + w_ref[1:2, cs] * _shift_up(du, dun, 1)
                   + w_ref[0:1, cs] * _shift_up(du, dun, 2))
            du0_ref[:, cs] = du0.astype(BF16)
            db_ref[:, cs] += jnp.sum(du, axis=0, keepdims=True)
            dw_ref[2:3, cs] += jnp.sum(du * xs[0], axis=0, keepdims=True)
            dw_ref[1:2, cs] += jnp.sum(du * xs[1], axis=0, keepdims=True)
            dw_ref[0:1, cs] += jnp.sum(du * xs[2], axis=0, keepdims=True)

        for j in range(F // cw):
            fs = slice(j * cw, (j + 1) * cw)
            gs, vs = fs, slice(F + j * cw, F + (j + 1) * cw)
            ug, ung, xg = conv(gs)
            uv, unv, xv = conv(vs)
            dug, duv = glu_grad(ug, uv, df_ref[:, fs].astype(F32))
            dung, dunv = glu_grad(ung, unv, dfn_ref[0:SUBLANES, fs].astype(F32))
            dung = jnp.where(i < nt - 1, dung, 0.0)
            dunv = jnp.where(i < nt - 1, dunv, 0.0)
            finish(gs, dug, dung, xg)
            finish(vs, duv, dunv, xv)

    wide = lambda rows, fn: pl.BlockSpec((rows, 2 * F), fn)
    nxt = lambda i: (jnp.minimum((i + 1) * hb, nb - 1), 0)
    return _pcall(body, name="conv_bwd",
                  out_shape=(jax.ShapeDtypeStruct((T, 2 * F), BF16),
                             jax.ShapeDtypeStruct((CONV_W, 2 * F), F32),
                             jax.ShapeDtypeStruct((1, 2 * F), F32)),
                  grid=(nt,),
                  in_specs=[wide(tt, lambda i: (i, 0)),
                            wide(SUBLANES, lambda i: (jnp.maximum(i * hb - 1, 0), 0)),
                            wide(SUBLANES, nxt),
                            pl.BlockSpec((tt, F), lambda i: (i, 0)),
                            pl.BlockSpec((2 * SUBLANES, F),
                                         lambda i: (jnp.minimum((i + 1) * (hb // 2), nb // 2 - 1), 0)),
                            wide(CONV_W, lambda i: (0, 0)), wide(1, lambda i: (0, 0))],
                  out_specs=(wide(tt, lambda i: (i, 0)), wide(CONV_W, lambda i: (0, 0)),
                             wide(1, lambda i: (0, 0))),
                  semantics=("arbitrary",))(u0, u0, u0, df, df, conv_w, conv_b)


def _adamw(w, g, m, v, *, name):
    R, C = w.shape
    tr = _tile(R, max(SUBLANES, (1 << 19) // max(C, 1) // SUBLANES * SUBLANES), SUBLANES)
    c1 = 1.0 / (1.0 - ADAM_B1 ** ADAM_STEP)
    c2 = 1.0 / (1.0 - ADAM_B2 ** ADAM_STEP)

    def body(w_ref, g_ref, m_ref, v_ref, d_ref, mo_ref, vo_ref):
        gv = g_ref[...]
        mn = ADAM_B1 * m_ref[...] + (1.0 - ADAM_B1) * gv
        vn = ADAM_B2 * v_ref[...] + (1.0 - ADAM_B2) * (gv * gv)
        d_ref[...] = -ADAM_LR * ((mn * c1) / (jnp.sqrt(vn * c2) + ADAM_EPS) + ADAM_WD * w_ref[...])
        mo_ref[...] = mn
        vo_ref[...] = vn

    blk = pl.BlockSpec((tr, C), lambda i: (i, 0))
    shp = jax.ShapeDtypeStruct((R, C), F32)
    return _pcall(body, name=name, out_shape=(shp, shp, shp), grid=(R // tr,),
                  in_specs=[blk] * 4, out_specs=(blk,) * 3, semantics=("parallel",))(w, g, m, v)


def _blk(h, C, elems=1 << 19, align=16):
    th = _tile(h, max(align, elems // C // align * align), align)
    if th < h or h * C <= 2 * elems:
        return th, C
    return h, _tile(C, max(LANES, elems // h // LANES * LANES))


def _adamw_halves(w, m, v, g_mine, g_other, c_idx, *, name):
    _, h, C = w.shape
    th, tc = _blk(h, C, align=SUBLANES)
    c1 = 1.0 / (1.0 - ADAM_B1 ** ADAM_STEP)
    c2 = 1.0 / (1.0 - ADAM_B2 ** ADAM_STEP)

    def body(c_ref, w_ref, m_ref, v_ref, gm_ref, go_ref, g_ref, d_ref, mo_ref, vo_ref):
        gv = jnp.where(pl.program_id(0) == c_ref[0], gm_ref[...], go_ref[...])
        mn = ADAM_B1 * m_ref[...] + (1.0 - ADAM_B1) * gv
        vn = ADAM_B2 * v_ref[...] + (1.0 - ADAM_B2) * (gv * gv)
        d_ref[...] = -ADAM_LR * ((mn * c1) / (jnp.sqrt(vn * c2) + ADAM_EPS) + ADAM_WD * w_ref[...])
        g_ref[...] = gv
        mo_ref[...] = mn
        vo_ref[...] = vn

    blk = pl.BlockSpec((None, th, tc), lambda s, i, j, c: (s, i, j))

    def pick(mine):
        def index(s, i, j, c):
            use = (s == c[0]) if mine else (s != c[0])
            return jnp.where(use, i, 0), jnp.where(use, j, 0)
        return pl.BlockSpec((th, tc), index)

    shp = jax.ShapeDtypeStruct((2, h, C), F32)
    return _pcall(body, name=name, out_shape=(shp,) * 4, grid=(2, h // th, C // tc), prefetch=1,
                  in_specs=[blk, blk, blk, pick(True), pick(False)], out_specs=(blk,) * 4,
                  semantics=("parallel", "parallel", "parallel"))(c_idx, w, m, v, g_mine, g_other)


def _mesh_pos():
    x, y, c = lax.axis_index("x"), lax.axis_index("y"), lax.axis_index("c")
    others = [(1 - x, y), (x, 1 - y), (1 - x, 1 - y)]
    return x, y, c, others


def _gather_copies(shards, lands, send_sems, recv_sems):
    x, y, c, others = _mesh_pos()
    me = 2 * x + y
    return [pltpu.make_async_remote_copy(
        src_ref=shards[a].at[c], dst_ref=lands[a].at[me, c],
        send_sem=send_sems.at[3 * a + j], recv_sem=recv_sems.at[3 * a + j],
        device_id=(*chip, c), device_id_type=MESH)
        for a in range(len(shards)) for j, chip in enumerate(others)]


def _near_copies(shards, lands, send_sems, recv_sems):
    x, y, c, others = _mesh_pos()
    me = 2 * x + y
    return [pltpu.make_async_remote_copy(
        src_ref=shards[a].at[c], dst_ref=lands[a].at[me, c],
        send_sem=send_sems.at[2 * a + j], recv_sem=recv_sems.at[2 * a + j],
        device_id=(*chip, c), device_id_type=MESH)
        for a in range(len(shards)) for j, chip in enumerate(others[:2])]


def _relay_copies(shards, zones, send_sems, recv_sems):
    x, y, c, others = _mesh_pos()
    (nx, ny), copies = others[:2], []
    for a in range(len(zones)):
        hc = zones[a].shape[-1] // 2
        for k, (src_chip, to, lo) in enumerate(((ny, nx, 0), (nx, ny, hc))):
            part = zones[a].at[2 * src_chip[0] + src_chip[1], c, :, pl.ds(lo, hc)]
            copies.append(pltpu.make_async_remote_copy(
                src_ref=part, dst_ref=part, send_sem=send_sems.at[2 * a + k], recv_sem=recv_sems.at[2 * a + k],
                device_id=(*to, c), device_id_type=MESH))
    return copies


def _pass_copies(shards, zones, send_sems, recv_sems, pieces=(0, 1, 2, 3)):
    x, y, c, others = _mesh_pos()
    me = 2 * x + y
    copies = []
    for a in range(len(shards)):
        srcs = [zones[a].at[2 * chip[0] + chip[1], c] for chip in others] + [shards[a]]
        dsts = [zones[a].at[2 * chip[0] + chip[1], c] for chip in others] + [zones[a].at[me]]
        copies += [pltpu.make_async_remote_copy(
            src_ref=srcs[p], dst_ref=dsts[p], send_sem=send_sems.at[len(pieces) * a + k],
            recv_sem=recv_sems.at[len(pieces) * a + k], device_id=(x, y, 1 - c), device_id_type=MESH)
            for k, p in enumerate(pieces)]
    return copies


def _exchange_copies(grads, recvs, send_sems, recv_sems):
    x, y, c, _ = _mesh_pos()
    return [pltpu.make_async_remote_copy(
        src_ref=grads[a].at[:, 1 - c], dst_ref=recvs[a], send_sem=send_sems.at[a],
        recv_sem=recv_sems.at[a], device_id=(x, y, 1 - c), device_id_type=MESH) for a in range(len(grads))]


def _split_start(copies, per, srcs, zones, after, *, name):
    n = len(srcs)
    HBM = pl.BlockSpec(memory_space=pltpu.HBM)
    SEM = pl.BlockSpec(memory_space=pltpu.SEMAPHORE)

    def body(*refs):
        send_sems, recv_sems = refs[2 * n + 1], refs[2 * n + 2]
        for cp in copies(refs[:n], refs[n:2 * n], send_sems, recv_sems):
            cp.start()
        refs[-1][...] = jnp.zeros_like(refs[-1])

    hbm = lambda a: pltpu.HBM(a.shape, a.dtype)
    res = _pcall(body, name=name,
                 out_shape=(pltpu.SemaphoreType.DMA((per * n,)), pltpu.SemaphoreType.DMA((per * n,)),
                            *[hbm(a) for a in srcs], *[hbm(a) for a in zones],
                            jax.ShapeDtypeStruct((SUBLANES, LANES), F32)),
                 in_specs=[*[HBM] * (2 * n), pl.BlockSpec(memory_space=pl.ANY)],
                 out_specs=(SEM, SEM, *[HBM] * (2 * n), pl.BlockSpec(memory_space=pltpu.VMEM)),
                 aliases={i: 2 + i for i in range(2 * n)}, split_copy=True)(
        *[pltpu.with_memory_space_constraint(a, pltpu.HBM) for a in [*srcs, *zones]], after)
    return res[0], res[1], list(res[2:2 + n]), list(res[2 + n:2 + 2 * n]), res[-1]


def _split_wait(copies, send_sems, recv_sems, srcs, zones, after, *, name):
    n = len(srcs)
    HBM = pl.BlockSpec(memory_space=pltpu.HBM)
    SEM = pl.BlockSpec(memory_space=pltpu.SEMAPHORE)

    def body(*refs):
        for cp in copies(refs[:n], refs[n:2 * n], refs[2 * n], refs[2 * n + 1]):
            cp.wait_send()
            cp.wait_recv()

    hbm = lambda a: pltpu.HBM(a.shape, a.dtype)
    res = _pcall(body, name=name, out_shape=(*[hbm(a) for a in srcs], *[hbm(a) for a in zones]),
                 in_specs=[*[HBM] * (2 * n), SEM, SEM, pl.BlockSpec(memory_space=pl.ANY)],
                 out_specs=tuple([HBM] * (2 * n)), aliases={i: i for i in range(2 * n)},
                 split_copy=True)(*srcs, *zones, send_sems, recv_sems, after)
    return list(res[:n]), list(res[n:])


def _add_halves(grad, recv, c_idx, *, name):
    S, _, h, C = grad.shape
    th, tc = _blk(h, C)

    def body(c_ref, g_ref, r_ref, o_ref):
        o_ref[...] = (g_ref[...].astype(F32) + r_ref[...].astype(F32)).astype(o_ref.dtype)

    return _pcall(body, name=name, out_shape=jax.ShapeDtypeStruct((S, h, C), grad.dtype),
                  grid=(S, h // th, C // tc), prefetch=1,
                  in_specs=[pl.BlockSpec((None, None, th, tc), lambda s, i, j, c: (s, c[0], i, j)),
                            pl.BlockSpec((None, th, tc), lambda s, i, j, c: (s, i, j))],
                  out_specs=pl.BlockSpec((None, th, tc), lambda s, i, j, c: (s, i, j)),
                  semantics=("parallel", "parallel", "parallel"))(c_idx, grad, recv)


def _scatter_copies(srcs, lands, send_sems, recv_sems):
    x, y, c, others = _mesh_pos()
    return [pltpu.make_async_remote_copy(
        src_ref=srcs[a].at[2 * chip[0] + chip[1]], dst_ref=lands[a].at[j],
        send_sem=send_sems.at[3 * a + j], recv_sem=recv_sems.at[3 * a + j],
        device_id=(*chip, c), device_id_type=MESH)
        for a in range(len(srcs)) for j, chip in enumerate(others)]


def _add_chips(sums, recv, chip_idx, *, name):
    _, h, C = sums.shape
    th, tc = _blk(h, C)

    def body(k_ref, s_ref, r_ref, o_ref):
        acc = s_ref[...].astype(F32) + r_ref[0].astype(F32)
        acc = acc + r_ref[1].astype(F32)
        o_ref[...] = acc + r_ref[2].astype(F32)

    return _pcall(body, name=name, out_shape=jax.ShapeDtypeStruct((h, C), F32),
                  grid=(h // th, C // tc), prefetch=1,
                  in_specs=[pl.BlockSpec((None, th, tc), lambda i, j, k: (k[0], i, j)),
                            pl.BlockSpec((3, th, tc), lambda i, j, k: (0, i, j))],
                  out_specs=pl.BlockSpec((th, tc), lambda i, j, k: (i, j)),
                  semantics=("parallel", "parallel"))(chip_idx, sums, recv)


def _swap_copies(halves, others, send_sems, recv_sems):
    x, y, c, _ = _mesh_pos()
    return [pltpu.make_async_remote_copy(
        src_ref=halves[a], dst_ref=others[a], send_sem=send_sems.at[a], recv_sem=recv_sems.at[a],
        device_id=(x, y, 1 - c), device_id_type=MESH) for a in range(len(halves))]


def _all_reduce_small(buf):
    R, L = buf.shape
    NDEV = 8

    def body(x_ref, sum_ref, all_ref, send_sems, recv_sems, local_sem):
        x, y, c, others = _mesh_pos()
        me, sibling = (x, y, c), (x, y, 1 - c)

        def slot(px, py, pc):
            return all_ref.at[4 * px + 2 * py + pc]

        def copy(k, block, to, src=None):
            return pltpu.make_async_remote_copy(
                src_ref=slot(*block) if src is None else src, dst_ref=slot(*block),
                send_sem=send_sems.at[k], recv_sem=recv_sems.at[k], device_id=to, device_id_type=MESH)

        mine = pltpu.make_async_copy(x_ref, slot(*me), local_sem)
        mine.start()
        first = [copy(0, me, sibling, src=x_ref)]
        first += [copy(1 + j, me, (*chip, c), src=x_ref) for j, chip in enumerate(others)]
        for cp in first:
            cp.start()
        passed = [copy(4 + j, (*chip, c), sibling) for j, chip in enumerate(others)]
        for j, chip in enumerate(others):
            copy(1 + j, (*chip, c), me).wait_recv()
            passed[j].start()
        copy(0, sibling, me).wait_recv()
        for j, chip in enumerate(others):
            copy(4 + j, (*chip, 1 - c), me).wait_recv()
        for cp in first + passed:
            cp.wait_send()
        mine.wait()
        acc = all_ref[0]
        for d in range(1, NDEV):
            acc = acc + all_ref[d]
        sum_ref[...] = acc

    VM = pl.BlockSpec(memory_space=pltpu.VMEM)
    return _pcall(body, name="all_reduce_small",
                  out_shape=(jax.ShapeDtypeStruct((R, L), F32), jax.ShapeDtypeStruct((NDEV, R, L), F32)),
                  in_specs=[VM], out_specs=(VM, VM),
                  scratch_shapes=[pltpu.SemaphoreType.DMA((7,)), pltpu.SemaphoreType.DMA((7,)),
                                  pltpu.SemaphoreType.DMA])(buf)[0]


def _pack(arrs, rows_multiple=16):
    flat = [a.reshape(-1).astype(F32) for a in arrs]
    sizes = [f.shape[0] for f in flat]
    total = sum(sizes)
    per = LANES * rows_multiple
    padded = -(-total // per) * per
    flat.append(jnp.zeros((padded - total,), F32))
    offs = [0]
    for s in sizes:
        offs.append(offs[-1] + s)
    return jnp.concatenate(flat).reshape(padded // LANES, LANES), offs


def _unpack(buf, offs, shapes):
    flat = buf.reshape(-1)
    return [flat[offs[i]:offs[i + 1]].reshape(s) for i, s in enumerate(shapes)]


def kernel(x, mem, g_mix, w_in, w_a2, b_a, g_gla, w_pool, pool_scale, w_branch, w_out, g_cross, g_mem, w_cq, w_ckv, w_co, g_ffn, w_up, conv_w, conv_b, w_down, g_final, loss_target, m_g_mix, m_w_in, m_w_a2, m_b_a, m_g_gla, m_w_pool, m_pool_scale, m_w_branch, m_w_out, m_g_cross, m_g_mem, m_w_cq, m_w_ckv, m_w_co, m_g_ffn, m_w_up, m_conv_w, m_conv_b, m_w_down, m_g_final, v_g_mix, v_w_in, v_w_a2, v_b_a, v_g_gla, v_w_pool, v_pool_scale, v_w_branch, v_w_out, v_g_cross, v_g_mem, v_w_cq, v_w_ckv, v_w_co, v_g_ffn, v_w_up, v_conv_w, v_conv_b, v_w_down, v_g_final):
    weights = dict(g_mix=g_mix, w_in=w_in, w_a2=w_a2, b_a=b_a, g_gla=g_gla, w_pool=w_pool,
                   pool_scale=pool_scale, w_branch=w_branch, w_out=w_out, g_cross=g_cross, g_mem=g_mem,
                   w_cq=w_cq, w_ckv=w_ckv, w_co=w_co, g_ffn=g_ffn, w_up=w_up, conv_w=conv_w,
                   conv_b=conv_b, w_down=w_down, g_final=g_final)
    mom_m = dict(g_mix=m_g_mix, w_in=m_w_in, w_a2=m_w_a2, b_a=m_b_a, g_gla=m_g_gla, w_pool=m_w_pool,
                 pool_scale=m_pool_scale, w_branch=m_w_branch, w_out=m_w_out, g_cross=m_g_cross,
                 g_mem=m_g_mem, w_cq=m_w_cq, w_ckv=m_w_ckv, w_co=m_w_co, g_ffn=m_g_ffn, w_up=m_w_up,
                 conv_w=m_conv_w, conv_b=m_conv_b, w_down=m_w_down, g_final=m_g_final)
    mom_v = dict(g_mix=v_g_mix, w_in=v_w_in, w_a2=v_w_a2, b_a=v_b_a, g_gla=v_g_gla, w_pool=v_w_pool,
                 pool_scale=v_pool_scale, w_branch=v_w_branch, w_out=v_w_out, g_cross=v_g_cross,
                 g_mem=v_g_mem, w_cq=v_w_cq, w_ckv=v_w_ckv, w_co=v_w_co, g_ffn=v_g_ffn, w_up=v_w_up,
                 conv_w=v_conv_w, conv_b=v_conv_b, w_down=v_w_down, g_final=v_g_final)
    order = list(weights)
    big = ["w_in", "w_branch", "w_out", "w_cq", "w_ckv", "w_co", "w_up", "w_down"]
    small_sharded = ["w_a2", "w_pool", "conv_w"]
    small_repl = ["g_mix", "b_a", "g_gla", "pool_scale", "g_cross", "g_mem", "g_ffn", "conv_b", "g_final"]

    xs, ms, tgt = x[0], mem[0], loss_target[0]
    T, D = xs.shape
    M = ms.shape[0]
    DK, DV, PW = b_a.shape[1], g_gla.shape[1], pool_scale.shape[1]
    RANK = w_a2.shape[1]
    F2 = conv_b.shape[1]
    F = F2 // 2
    DIN = N_CHIPS * w_in.shape[2]
    OFF_A = 2 * DK + 2 * DV
    OFF_P = OFF_A + RANK
    RP = LANES
    GW = PW // POOL_GROUPS
    assert PW == DV and 4 * DV == 2 * D and OFF_P + PW + 2 * D == DIN

    cx, cy, cc = lax.axis_index("x"), lax.axis_index("y"), lax.axis_index("c")
    chip = 2 * cx + cy
    c_idx = jnp.reshape(cc, (1,)).astype(jnp.int32)
    chip_idx = jnp.reshape(chip, (1,)).astype(jnp.int32)

    def halves(a):
        return a.reshape(2, a.shape[0] // 2, a.shape[1])

    shard2d = {k: (weights[k][0].T if k == "w_in" else weights[k][0]) for k in big}
    small_pack, small_offs = _pack([weights[k][0] for k in small_sharded], rows_multiple=32)
    flying, passing = {}, {}

    def gather_start(group, keys, tok):
        srcs = [small_pack if k == "small" else shard2d[k].astype(BF16) for k in keys]
        if group != "in":
            srcs = [a + tok[0:1, 0:1].astype(a.dtype) for a in srcs]
        srcs = [halves(a) for a in srcs]
        zones = [lax.empty((N_CHIPS, *s.shape), s.dtype) for s in srcs]
        first = (_near_copies, 2) if group == "in" else (_gather_copies, 3)
        s_sems, r_sems, srcs, zones, tok = _split_start(*first, srcs, zones, tok, name=f"gather_start_{group}")
        flying[group] = (keys, s_sems, r_sems, srcs, zones)
        return tok

    tok = gather_start("in", ["w_in"], xs)

    def arrive_in(after):
        keys, s_sems, r_sems, srcs, zones = flying["in"]
        near, diag = functools.partial(_pass_copies, pieces=(0, 1, 3)), functools.partial(_pass_copies, pieces=(2,))
        srcs, zones = _split_wait(_near_copies, s_sems, r_sems, srcs, zones, after, name="gather_wait_in")
        rs, rr, srcs, zones, tok = _split_start(_relay_copies, 2, srcs, zones, after, name="gather_relay_start_in")
        ns, nr, srcs, zones, tok = _split_start(near, 3, srcs, zones, tok, name="gather_pass_near_start_in")
        for group, group_keys in (("mix", ["w_branch", "w_out", "small"]), ("cross", ["w_cq", "w_ckv", "w_co"]),
                                  ("up", ["w_up"]), ("down", ["w_down"])):
            tok = gather_start(group, group_keys, tok)
        after = tok
        srcs, zones = _split_wait(_relay_copies, rs, rr, srcs, zones, after, name="gather_relay_wait_in")
        ds, dr, srcs, zones, _ = _split_start(diag, 1, srcs, zones, after, name="gather_pass_diag_start_in")
        srcs, zones = _split_wait(near, ns, nr, srcs, zones, after, name="gather_pass_near_wait_in")
        _, full = _split_wait(diag, ds, dr, srcs, zones, after, name="gather_pass_diag_wait_in")
        return {k: f.reshape(N_CHIPS, f.shape[1] * f.shape[2], f.shape[3]) for k, f in zip(keys, full)}

    def landed(group, after):
        keys, s_sems, r_sems, srcs, zones = flying[group]
        srcs, zones = _split_wait(_gather_copies, s_sems, r_sems, srcs, zones, after,
                                  name=f"gather_wait_{group}")
        s_sems, r_sems, srcs, zones, token = _split_start(_pass_copies, 4, srcs, zones, after,
                                                          name=f"gather_pass_start_{group}")
        passing[group] = (keys, s_sems, r_sems, srcs, zones)
        return token

    def arrive(group, after):
        keys, s_sems, r_sems, srcs, zones = passing[group]
        _, full = _split_wait(_pass_copies, s_sems, r_sems, srcs, zones, after,
                              name=f"gather_pass_wait_{group}")
        return {k: f.reshape(N_CHIPS, f.shape[1] * f.shape[2], f.shape[3]) for k, f in zip(keys, full)}

    def rows(g):
        return g.reshape(-1, g.shape[2])

    h1, r1 = _rms_fwd(xs, g_mix + tok[0:1, 0:1], name="norm_mix")
    W_in = rows(arrive_in(h1)["w_in"])
    W_main = jnp.concatenate([W_in[:OFF_A], W_in[OFF_P:]], axis=0)
    W_a = jnp.pad(W_in[OFF_A:OFF_P], ((0, RP - RANK), (0, 0)))
    tok = landed("mix", W_a)
    proj = _mm(h1, W_main, "nt", name="proj_main", out_dtype=F32, after=tok)
    gw = arrive("mix", proj)
    W_branch, W_out, small_all = rows(gw["w_branch"]), rows(gw["w_out"]), gw["small"]
    sm = [_unpack(small_all[j], small_offs, [weights[k].shape[1:] for k in small_sharded]) for j in range(N_CHIPS)]
    W_a2 = jnp.concatenate([sm[j][0] for j in range(N_CHIPS)], axis=1)
    W_a2p = jnp.pad(W_a2, ((0, RP - RANK), (0, 0))).astype(BF16)
    W_pool = jnp.concatenate([sm[j][1] for j in range(N_CHIPS)], axis=1).astype(BF16)
    W_conv = jnp.concatenate([sm[j][2] for j in range(N_CHIPS)], axis=1)

    a_pad = _mm(h1, W_a, "nt", name="proj_gate_rank", out_dtype=F32)
    o_gla, o_raw, states = _gla_fwd(proj, a_pad, W_a2p, b_a, g_gla, T=T, DK=DK, DV=DV)
    o_pool = _pool_fwd(proj, W_pool, pool_scale, T=T, PW=PW, col_block=3)
    tok = landed("cross", o_pool)
    y_gla = _mm(o_gla, W_branch, "nn", name="branch_gla", out_dtype=BF16, K=DV, after=tok)
    y_pool = _mm(o_pool, W_branch, "nn", name="branch_pool", out_dtype=BF16, K=PW, b_off=(DV, 0))
    merged = _merge_fwd(y_gla, y_pool, proj, T=T, D=D, col_block=2)
    x1 = _mm(merged, W_out, "nn", name="mix_out", out_dtype=F32, add=xs)

    h2, r2 = _rms_fwd(x1, g_cross, name="norm_cross")
    mem_n, rm = _rms_fwd(ms, g_mem, name="norm_mem")
    gw = arrive("cross", h2)
    W_cq, W_ckv, W_co = rows(gw["w_cq"]), gw["w_ckv"], rows(gw["w_co"])
    qc = _mm(h2, W_cq, "nn", name="cross_q", out_dtype=BF16)
    kv = _mm(mem_n, W_ckv, "nn", name="cross_kv", out_dtype=BF16, b_blocked=True)
    o_att = _attn_fwd(qc, kv, T=T, D=D, M=M)
    x2 = _mm(o_att, W_co, "nn", name="cross_out", out_dtype=F32, add=x1)

    tok = landed("up", x2)
    h3, r3 = _rms_fwd(x2, g_ffn + tok[0:1, 0:1], name="norm_ffn")
    W_up = arrive("up", h3)["w_up"]
    u0 = _mm(h3, W_up, "nn", name="ffn_up", out_dtype=F32, b_blocked=True)
    tok = landed("down", u0)
    f_act = _conv_fwd(u0, W_conv, conv_b + tok[0:1, 0:1], T=T, F=F)
    W_down = rows(arrive("down", f_act)["w_down"])
    x3 = _mm(f_act, W_down, "nn", name="ffn_down", out_dtype=F32, add=x2, tk=F // 2)

    loss_part, dx3, dx3_b, dg_final = _loss_head(x3, g_final.reshape(1, D), tgt)

    def col_shards(g):
        nb, K, Nb = g.shape
        return g.reshape(nb, 2, K // 2, Nb)

    def row_shards(g):
        R, N = g.shape
        return g.reshape(N_CHIPS, 2, R // N_CHIPS // 2, N)

    exchanging, in_flight = {}, []

    def exchange_start(group, keys, partials, after):
        recvs = [lax.empty((p.shape[0], *p.shape[2:]), p.dtype) for p in partials]
        s_sems, r_sems, partials, recvs, token = _split_start(
            _exchange_copies, 1, partials, recvs, after, name=f"grad_exchange_start_{group}")
        exchanging[group] = (keys, s_sems, r_sems, partials, recvs)
        return token

    def scatter_start(group, after):
        keys, s_sems, r_sems, partials, recvs = exchanging[group]
        partials, recvs = _split_wait(_exchange_copies, s_sems, r_sems, partials, recvs, after,
                                      name=f"grad_exchange_wait_{group}")
        chip_sums = [_add_halves(p, r, c_idx, name=f"grad_add_halves_{k}")
                     for k, p, r in zip(keys, partials, recvs)]
        lands = [lax.empty((3, *s.shape[1:]), s.dtype) for s in chip_sums]
        s_sems, r_sems, sums, lands, token = _split_start(
            _scatter_copies, 3, chip_sums, lands, after, name=f"grad_scatter_start_{group}")
        in_flight.append((group, keys, s_sems, r_sems, sums, lands))
        return token

    collected = []

    def collect(after):
        group, keys, s_sems, r_sems, sums, lands = in_flight.pop(0)
        sums, from_chips = _split_wait(_scatter_copies, s_sems, r_sems, sums, lands, after,
                                       name=f"grad_scatter_wait_{group}")
        half_sums = [_add_chips(s, r, chip_idx, name=f"grad_add_chips_{k}") for k, s, r in zip(keys, sums, from_chips)]
        others = [lax.empty(h.shape, h.dtype) for h in half_sums]
        s_sems, r_sems, half_sums, others, token = _split_start(
            _swap_copies, 1, half_sums, others, after, name=f"grad_swap_start_{group}")
        collected.append((keys, s_sems, r_sems, half_sums, others))
        return token

    df = _mm(dx3_b, W_down, "nt", name="d_ffn_act", out_dtype=BF16)
    dW_down = _mm(f_act, dx3_b, "tn", name="dw_down", out_dtype=BF16)
    du0, dconv_w, dconv_b = _conv_bwd(u0, W_conv, conv_b, df, T=T, F=F)
    dh3 = _mm(du0, W_up, "nt", name="d_ffn_in", out_dtype=F32, b_blocked=True, tk=F2 // N_CHIPS)
    dW_up = _mm(h3, du0, "tn", name="dw_up", out_dtype=BF16, out_blocks=N_CHIPS)
    tok = exchange_start("ffn", ["w_down", "w_up"], [row_shards(dW_down), col_shards(dW_up)], dh3)
    dx2, dx2_b, dg_ffn = _rms_bwd(dh3, x2, r3 + tok[0:1, 0:1], g_ffn, dx3, name="norm_ffn_bwd")

    do_att = _mm(dx2_b, W_co, "nt", name="d_cross_o", out_dtype=BF16)
    dW_co = _mm(o_att, dx2_b, "tn", name="dw_co", out_dtype=BF16)
    tok = scatter_start("ffn", dW_co)
    dq, dkv = _attn_bwd(qc, kv, do_att, T=T, D=D, M=M)
    dkv_b = dkv.astype(BF16)
    dW_cq = _mm(h2, dq, "tn", name="dw_cq", out_dtype=BF16, after=tok)
    dh2 = _mm(dq, W_cq, "nt", name="d_cross_in", out_dtype=F32)
    dW_ckv = _mm(mem_n, dkv_b, "tn", name="dw_ckv", out_dtype=BF16, out_blocks=N_CHIPS)
    dmem_n = _mm(dkv_b, W_ckv, "nt", name="d_mem", out_dtype=F32, b_blocked=True)
    tok = exchange_start("cross", ["w_co", "w_cq", "w_ckv"],
                         [row_shards(dW_co), row_shards(dW_cq), col_shards(dW_ckv)], dmem_n)
    _, _, dg_mem = _rms_bwd(dmem_n, ms, rm, g_mem, None, name="norm_mem_bwd")
    dx1, dx1_b, dg_cross = _rms_bwd(dh2, x1, r2 + tok[0:1, 0:1], g_cross, dx2, name="norm_cross_bwd")

    dmerged = _mm(dx1_b, W_out, "nt", name="d_merged", out_dtype=BF16)
    dW_out = _mm(merged, dx1_b, "tn", name="dw_out", out_dtype=BF16)
    tok = scatter_start("cross", dW_out)
    dy_gla, dy_pool, dgates = _merge_bwd(dmerged, y_gla, y_pool, proj, T=T, D=D, col_block=2)
    dW_br_gla = _mm(o_gla, dy_gla, "tn", name="dw_branch_gla", out_dtype=BF16, after=tok)
    dW_br_pool = _mm(o_pool, dy_pool, "tn", name="dw_branch_pool", out_dtype=BF16)
    do_gla = _mm(dy_gla, W_branch, "nt", name="d_o_gla", out_dtype=F32, N=DV)
    do_pool = _mm(dy_pool, W_branch, "nt", name="d_o_pool", out_dtype=F32, N=PW, b_off=(DV, 0))
    dp, dw_pool, dpool_scale = _pool_bwd(proj, W_pool, pool_scale, do_pool, T=T, PW=PW, col_block=3)
    dW_pool = jnp.transpose(dw_pool.reshape(POOL_GROUPS, N_CHIPS, GW // N_CHIPS, GW), (1, 0, 2, 3))
    tok = exchange_start("mix", ["w_out", "w_branch", "w_pool"],
                         [row_shards(dW_out), row_shards(jnp.concatenate([dW_br_gla, dW_br_pool], axis=0)),
                          row_shards(dW_pool.reshape(N_CHIPS * POOL_GROUPS * (GW // N_CHIPS), GW).astype(BF16))],
                         dp)
    dqkvr, da_pad, dw2, db_a, dg_gla = _gla_bwd(proj, a_pad, W_a2p, b_a + tok[0:1, 0:1], g_gla, o_raw, states,
                                               do_gla, T=T, DK=DK, DV=DV)
    tok = scatter_start("mix", dqkvr)
    dproj = jnp.concatenate([dqkvr, dp, dgates], axis=1)
    dW_main = _mm(dproj, h1, "tn", name="dw_in_main", out_dtype=BF16, after=tok)
    dW_a = _mm(da_pad, h1, "tn", name="dw_in_rank", out_dtype=BF16)
    dW_in = jnp.concatenate([dW_main[:OFF_A], dW_a[:RANK], dW_main[OFF_A:]], axis=0)
    tok = exchange_start("in", ["w_in"], [row_shards(dW_in)], dW_a)
    dh1 = _mm(dproj, W_main, "nn", name="d_mix_in_main", out_dtype=F32, after=tok, tk=4096)
    dh1 = _mm(da_pad, W_a, "nn", name="d_mix_in_rank", out_dtype=F32, add=dh1)
    dx0, _, dg_mix = _rms_bwd(dh1, xs, r1, g_mix, dx1, name="norm_mix_bwd")

    grads = {}

    small_grads = [loss_part, dg_mix, db_a, dg_gla, dpool_scale, dg_cross, dg_mem, dg_ffn, dconv_b, dg_final,
                   dw2[:RANK], dconv_w]
    small_buf, offs = _pack(small_grads)
    small_sum = _all_reduce_small(small_buf)
    red = _unpack(small_sum, offs, [g.shape for g in small_grads])
    loss = red[0][0, 0]
    for k, g in zip(small_repl, red[1:10]):
        grads[k] = g.reshape(weights[k].shape)
    nb = DK // N_CHIPS
    grads["w_a2"] = lax.dynamic_slice_in_dim(red[10], chip * nb, nb, axis=1)[None]
    nb = F2 // N_CHIPS
    grads["conv_w"] = lax.dynamic_slice_in_dim(red[11], chip * nb, nb, axis=1)[None]

    delta, new_m, new_v = {}, {}, {}

    def shard_rows(k, a):
        a = a[0]
        return a.T if k == "w_in" else a.reshape(-1, a.shape[-1])

    def whole(k, a):
        a = a.reshape(-1, a.shape[2])
        return (a.T if k == "w_in" else a).reshape(weights[k].shape)

    scatter_start("in", small_sum)

    def finish(after):
        keys, s_sems, r_sems, mine, others = collected.pop(0)
        mine, others = _split_wait(_swap_copies, s_sems, r_sems, mine, others, after,
                                   name=f"grad_swap_wait_{keys[0]}")
        for k, g_mine, g_other in zip(keys, mine, others):
            wmv = [halves(shard_rows(k, src[k])) for src in (weights, mom_m, mom_v)]
            res = _adamw_halves(*wmv, g_mine, g_other, c_idx, name=f"adamw_{k}")
            grads[k], delta[k], new_m[k], new_v[k] = (whole(k, a) for a in res)
        return res[1]

    after = in_flight[-1][4][0]
    while in_flight:
        after = collect(after)
        while len(collected) > 1:
            after = finish(after)
    finish(after)
    small = small_repl + ["w_a2", "conv_w"]
    packs = [_pack([src[k] for k in small])[0] for src in (weights, grads, mom_m, mom_v)]
    _, offs = _pack([weights[k] for k in small])
    outs = _adamw(*packs, name="adamw_small")
    for res, o in zip((delta, new_m, new_v), outs):
        for k, a in zip(small, _unpack(o, offs, [weights[k].shape for k in small])):
            res[k] = a

    return (loss, dx0[None], *[grads[k] for k in order], *[delta[k] for k in order],
            *[new_m[k] for k in order], *[new_v[k] for k in order])
```

```python
import functools

import jax
import jax.numpy as jnp
from jax import lax
from jax.experimental import pallas as pl
from jax.experimental.pallas import tpu as pltpu

F32 = jnp.float32
BF16 = jnp.bfloat16
MESH = pl.DeviceIdType.MESH
HIGHEST = lax.Precision.HIGHEST

EPS = 1e-6
GLA_HEADS = 4
GLA_CHUNK = 128
GLA_GATE_NORM = 16.0
POOL_GROUPS = 4
CROSS_HEADS = 4
CONV_W = 3
N_CHIPS = 4
LANES = 128
SUBLANES = 8
VMEM_LIMIT = 56 << 20

ADAM_LR = 0.001
ADAM_B1 = 0.9
ADAM_B2 = 0.999
ADAM_EPS = 1e-08
ADAM_WD = 0.01
ADAM_STEP = 10

NN = (((1,), (0,)), ((), ()))
NT = (((1,), (1,)), ((), ()))
TN = (((0,), (0,)), ((), ()))


CHUNK_PRECISION = lax.Precision.HIGH


def _dot(a, b, dn=NN, precision=None):
    return lax.dot_general(a, b, dn, precision=precision, preferred_element_type=F32)


def _tile(n, pref, align=LANES):
    t = (min(pref, n) // align) * align
    while t >= align:
        if n % t == 0:
            return t
        t -= align
    return n


def _pcall(body, *, name, out_shape, grid=(), in_specs=None, out_specs=None, scratch_shapes=(),
           semantics=None, prefetch=0, aliases=None, split_copy=False):
    params = dict(vmem_limit_bytes=VMEM_LIMIT)
    if semantics is not None:
        params["dimension_semantics"] = semantics
    if split_copy:
        params["has_side_effects"] = pltpu.SideEffectType.DATAFLOW_SIDE_EFFECTING
    if prefetch:
        grid_spec = pltpu.PrefetchScalarGridSpec(
            num_scalar_prefetch=prefetch, grid=grid, in_specs=in_specs, out_specs=out_specs,
            scratch_shapes=scratch_shapes)
        return pl.pallas_call(body, name=name, out_shape=out_shape, grid_spec=grid_spec,
                              compiler_params=pltpu.CompilerParams(**params))
    kw = {}
    if aliases is not None:
        kw["input_output_aliases"] = aliases
    if in_specs is not None:
        kw["in_specs"] = in_specs
    if out_specs is not None:
        kw["out_specs"] = out_specs
    return pl.pallas_call(body, name=name, out_shape=out_shape, grid=grid,
                          scratch_shapes=scratch_shapes,
                          compiler_params=pltpu.CompilerParams(**params), **kw)


def _sigmoid(x):
    return 0.5 * jnp.tanh(0.5 * x) + 0.5


def _log_sigmoid(x):
    return jnp.minimum(x, 0.0) - jnp.log(1.0 + jnp.exp(-jnp.abs(x)))


def _mm(a, b, mode, *, name, out_dtype, M=None, N=None, K=None, a_off=(0, 0), b_off=(0, 0),
        add=None, b_blocked=False, out_blocks=0, after=None, tm=1536, tn=1536, tk=2048):
    if b_blocked:
        nb, R, Cb = b.shape
        b_rows, b_cols = R, nb * Cb
    else:
        b_rows, b_cols = b.shape
    if mode == "nn":
        M = M or a.shape[0]; K = K or a.shape[1]; N = N or b_cols
    elif mode == "nt":
        M = M or a.shape[0]; K = K or a.shape[1]; N = N or b_rows
    else:
        K = K or a.shape[0]; M = M or a.shape[1]; N = N or b_cols
    tm = _tile(M, tm, LANES if mode == "tn" else 16)
    tn = _tile(Cb if (b_blocked and mode != "nt") else (N // out_blocks if out_blocks else N), tn)
    tk = _tile(Cb if (b_blocked and mode == "nt") else K, tk)
    nk = K // tk
    dn = {"nn": NN, "nt": NT, "tn": TN}[mode]

    def off(o, t):
        assert o % t == 0, (name, o, t)
        return o // t

    if mode == "tn":
        ar, ac = off(a_off[0], tk), off(a_off[1], tm)
        a_spec = pl.BlockSpec((tk, tm), lambda i, j, k: (k + ar, i + ac))
    else:
        ar, ac = off(a_off[0], tm), off(a_off[1], tk)
        a_spec = pl.BlockSpec((tm, tk), lambda i, j, k: (i + ar, k + ac))
    if b_blocked and mode == "nt":
        per = Cb // tk
        b_spec = pl.BlockSpec((None, tn, tk), lambda i, j, k: (k // per, j, k % per))
    elif b_blocked:
        per = Cb // tn
        b_spec = pl.BlockSpec((None, tk, tn), lambda i, j, k: (j // per, k, j % per))
    elif mode == "nt":
        br, bc = off(b_off[0], tn), off(b_off[1], tk)
        b_spec = pl.BlockSpec((tn, tk), lambda i, j, k: (j + br, k + bc))
    else:
        br, bc = off(b_off[0], tk), off(b_off[1], tn)
        b_spec = pl.BlockSpec((tk, tn), lambda i, j, k: (k + br, j + bc))
    if out_blocks:
        per_o = N // out_blocks // tn
        o_spec = pl.BlockSpec((None, tm, tn), lambda i, j, k: (j // per_o, i, j % per_o))
        out_shape = jax.ShapeDtypeStruct((out_blocks, M, N // out_blocks), out_dtype)
    else:
        o_spec = pl.BlockSpec((tm, tn), lambda i, j, k: (i, j))
        out_shape = jax.ShapeDtypeStruct((M, N), out_dtype)
    in_specs = [a_spec, b_spec]
    args = [a, b]
    if add is not None:
        assert not out_blocks
        in_specs.append(o_spec)
        args.append(add)
    if after is not None:
        in_specs.append(pl.BlockSpec(memory_space=pl.ANY))
        args.append(after)
    n_in = len(args)

    def finish(r, refs):
        if add is not None:
            r = r + refs[2][...]
        o_ref = refs[n_in]
        o_ref[...] = r.astype(o_ref.dtype)

    def body_one(*refs):
        finish(_dot(refs[0][...].astype(BF16), refs[1][...].astype(BF16), dn), refs)

    def body_acc(*refs):
        acc_ref = refs[-1]
        k = pl.program_id(2)

        @pl.when(k == 0)
        def _():
            acc_ref[...] = jnp.zeros_like(acc_ref)

        acc_ref[...] += _dot(refs[0][...].astype(BF16), refs[1][...].astype(BF16), dn)

        @pl.when(k == nk - 1)
        def _():
            finish(acc_ref[...], refs)

    return _pcall(body_one if nk == 1 else body_acc, name=name, out_shape=out_shape,
                  grid=(M // tm, N // tn, nk), in_specs=in_specs, out_specs=o_spec,
                  scratch_shapes=[] if nk == 1 else [pltpu.VMEM((tm, tn), F32)],
                  semantics=("parallel", "parallel", "arbitrary"))(*args)


def _rms_fwd(x, g, *, name):
    T, D = x.shape
    tr = _tile(T, 128, 16)

    def body(x_ref, g_ref, h_ref, r_ref):
        xv = x_ref[...]
        r = lax.rsqrt(jnp.mean(xv * xv, axis=-1, keepdims=True) + EPS)
        h_ref[...] = (xv * r * g_ref[...]).astype(h_ref.dtype)
        r_ref[...] = r

    row = pl.BlockSpec((tr, D), lambda i: (i, 0))
    return _pcall(body, name=name,
                  out_shape=(jax.ShapeDtypeStruct((T, D), BF16), jax.ShapeDtypeStruct((T, 1), F32)),
                  grid=(T // tr,),
                  in_specs=[row, pl.BlockSpec((1, D), lambda i: (0, 0))],
                  out_specs=(row, pl.BlockSpec((tr, 1), lambda i: (i, 0))),
                  semantics=("parallel",))(x, g)


def _rms_bwd(dh, x, rstd, g, dres, *, name):
    T, D = x.shape
    tr = _tile(T, 128, 16)
    has_res = dres is not None

    def body(*refs):
        if has_res:
            dh_ref, x_ref, r_ref, g_ref, res_ref, dx_ref, dxb_ref, dg_ref = refs
        else:
            dh_ref, x_ref, r_ref, g_ref, dx_ref, dxb_ref, dg_ref = refs
        r = r_ref[...]
        xh = x_ref[...] * r
        dhv = dh_ref[...].astype(F32)
        dxh = dhv * g_ref[...]
        m = jnp.mean(dxh * xh, axis=-1, keepdims=True)
        dx = r * (dxh - xh * m)
        if has_res:
            dx = dx + res_ref[...]
        dx_ref[...] = dx
        dxb_ref[...] = dx.astype(BF16)

        @pl.when(pl.program_id(0) == 0)
        def _():
            dg_ref[...] = jnp.zeros_like(dg_ref)

        dg_ref[...] += jnp.sum(dhv * xh, axis=0, keepdims=True)

    row = pl.BlockSpec((tr, D), lambda i: (i, 0))
    vec = pl.BlockSpec((1, D), lambda i: (0, 0))
    in_specs = [row, row, pl.BlockSpec((tr, 1), lambda i: (i, 0)), vec]
    args = [dh, x, rstd, g]
    if has_res:
        in_specs.append(row)
        args.append(dres)
    return _pcall(body, name=name,
                  out_shape=(jax.ShapeDtypeStruct((T, D), F32), jax.ShapeDtypeStruct((T, D), BF16),
                             jax.ShapeDtypeStruct((1, D), F32)),
                  grid=(T // tr,), in_specs=in_specs, out_specs=(row, row, vec),
                  semantics=("arbitrary",))(*args)


def _loss_head(x3, g, tgt):
    T, D = x3.shape
    tr = _tile(T, 128, 16)

    def body(x_ref, g_ref, t_ref, loss_ref, dx_ref, dxb_ref, dg_ref):
        xv = x_ref[...]
        gv = g_ref[...]
        r = lax.rsqrt(jnp.mean(xv * xv, axis=-1, keepdims=True) + EPS)
        xh = xv * r
        err = xh * gv - t_ref[...]
        dy = err * (1.0 / D)
        dxh = dy * gv
        m = jnp.mean(dxh * xh, axis=-1, keepdims=True)
        dx = r * (dxh - xh * m)
        dx_ref[...] = dx
        dxb_ref[...] = dx.astype(BF16)

        @pl.when(pl.program_id(0) == 0)
        def _():
            dg_ref[...] = jnp.zeros_like(dg_ref)
            loss_ref[...] = jnp.zeros_like(loss_ref)

        dg_ref[...] += jnp.sum(dy * xh, axis=0, keepdims=True)
        part = 0.5 * jnp.sum(jnp.mean(err * err, axis=-1, keepdims=True), axis=0, keepdims=True)
        loss_ref[...] += jnp.broadcast_to(part, loss_ref.shape)

    row = pl.BlockSpec((tr, D), lambda i: (i, 0))
    vec = pl.BlockSpec((1, D), lambda i: (0, 0))
    return _pcall(body, name="loss_head",
                  out_shape=(jax.ShapeDtypeStruct((1, LANES), F32), jax.ShapeDtypeStruct((T, D), F32),
                             jax.ShapeDtypeStruct((T, D), BF16), jax.ShapeDtypeStruct((1, D), F32)),
                  grid=(T // tr,), in_specs=[row, vec, row],
                  out_specs=(pl.BlockSpec((1, LANES), lambda i: (0, 0)), row, row, vec),
                  semantics=("arbitrary",))(x3, g, tgt)


def _gla_chunk_terms(qk, a_ref, w2_ref, ba_ref, DK):
    C = qk.shape[0]
    gp = _dot(a_ref[...].astype(BF16), w2_ref[...]) + ba_ref[...]
    la = _log_sigmoid(gp) * (1.0 / GLA_GATE_NORM)
    row = lax.broadcasted_iota(jnp.int32, (C, C), 0)
    col = lax.broadcasted_iota(jnp.int32, (C, C), 1)
    causal = row >= col
    b = _dot(causal.astype(F32), la, precision=HIGHEST)
    return gp, b, causal


def _gla_fwd(proj, a_pad, w2, b_a, g_gla, *, T, DK, DV):
    assert 2 * DK == DV
    H = GLA_HEADS
    HK, HV = DK // H, DV // H
    C = GLA_CHUNK
    n = T // C
    RP = a_pad.shape[1]
    scale = HK ** -0.5

    def body(qk_ref, v_ref, r_ref, a_ref, w2_ref, ba_ref, gg_ref, og_ref, oraw_ref, st_ref, s_ref):
        @pl.when(pl.program_id(0) == 0)
        def _():
            s_ref[...] = jnp.zeros_like(s_ref)

        st_ref[...] = s_ref[...]
        qk = qk_ref[...]
        _, b, causal = _gla_chunk_terms(qk, a_ref, w2_ref, ba_ref, DK)
        for h in range(H):
            ks = slice(h * HK, (h + 1) * HK)
            vs = slice(h * HV, (h + 1) * HV)
            bh = b[:, ks]
            b_last = bh[C - 1:C, :]
            qt = qk[:, ks] * scale * jnp.exp(bh)
            kh = qk[:, DK + h * HK:DK + (h + 1) * HK]
            kt = kh * jnp.exp(-bh)
            khat = kh * jnp.exp(b_last - bh)
            a_mat = jnp.where(causal, _dot(qt, kt, NT, CHUNK_PRECISION), 0.0)
            vh = v_ref[:, vs]
            s_t = s_ref[h]
            o = _dot(a_mat, vh, NN, CHUNK_PRECISION) + _dot(qt, s_t, NT, CHUNK_PRECISION)
            s_ref[h] = s_t * jnp.exp(b_last) + _dot(vh, khat, TN, CHUNK_PRECISION)
            rs = lax.rsqrt(jnp.mean(o * o, axis=-1, keepdims=True) + EPS)
            rr = r_ref[:, vs]
            og = o * rs * gg_ref[:, vs] * (rr * _sigmoid(rr))
            oraw_ref[:, vs] = o
            og_ref[:, vs] = og.astype(BF16)

    blk = lambda j: pl.BlockSpec((C, DV), lambda i: (i, j))
    full = lambda s: pl.BlockSpec(s, lambda i: (0,) * len(s))
    return _pcall(
        body, name="gla_fwd",
        out_shape=(jax.ShapeDtypeStruct((T, DV), BF16), jax.ShapeDtypeStruct((T, DV), F32),
                   jax.ShapeDtypeStruct((n, H, HV, HK), F32)),
        grid=(n,),
        in_specs=[blk(0), blk(1), blk(2), pl.BlockSpec((C, RP), lambda i: (i, 0)),
                  full((RP, DK)), full((1, DK)), full((1, DV))],
        out_specs=(blk(0), blk(0), pl.BlockSpec((None, H, HV, HK), lambda i: (i, 0, 0, 0))),
        scratch_shapes=[pltpu.VMEM((H, HV, HK), F32)],
        semantics=("arbitrary",))(proj, proj, proj, a_pad, w2, b_a, g_gla)


def _gla_bwd(proj, a_pad, w2, b_a, g_gla, o_raw, states, do_gla, *, T, DK, DV):
    H = GLA_HEADS
    HK, HV = DK // H, DV // H
    C = GLA_CHUNK
    n = T // C
    RP = a_pad.shape[1]
    scale = HK ** -0.5

    def body(qk_ref, v_ref, r_ref, a_ref, w2_ref, ba_ref, gg_ref, oraw_ref, st_ref, dog_ref,
             dqkvr_ref, da_ref, dw2_ref, dba_ref, dgg_ref, ds_ref):
        @pl.when(pl.program_id(0) == 0)
        def _():
            ds_ref[...] = jnp.zeros_like(ds_ref)
            dw2_ref[...] = jnp.zeros_like(dw2_ref)
            dba_ref[...] = jnp.zeros_like(dba_ref)
            dgg_ref[...] = jnp.zeros_like(dgg_ref)

        qk = qk_ref[...]
        gp, b, causal = _gla_chunk_terms(qk, a_ref, w2_ref, ba_ref, DK)
        row = lax.broadcasted_iota(jnp.int32, (C, C), 0)
        col = lax.broadcasted_iota(jnp.int32, (C, C), 1)
        upper = (col >= row).astype(F32)
        dla_parts = []
        for h in range(H):
            ks = slice(h * HK, (h + 1) * HK)
            vs = slice(h * HV, (h + 1) * HV)
            bh = b[:, ks]
            b_last = bh[C - 1:C, :]
            eb = jnp.exp(bh)
            emb = jnp.exp(-bh)
            ehat = jnp.exp(b_last - bh)
            e_last = jnp.exp(b_last)
            qt = qk[:, ks] * scale * eb
            kh = qk[:, DK + h * HK:DK + (h + 1) * HK]
            kt = kh * emb
            khat = kh * ehat
            a_mat = jnp.where(causal, _dot(qt, kt, NT, CHUNK_PRECISION), 0.0)
            vh = v_ref[:, vs]
            o = oraw_ref[:, vs]
            rs = lax.rsqrt(jnp.mean(o * o, axis=-1, keepdims=True) + EPS)
            on = o * rs
            gg = gg_ref[:, vs]
            rr = r_ref[:, vs]
            sg = _sigmoid(rr)
            d_out = dog_ref[:, vs]
            dr = d_out * (on * gg) * (sg * (1.0 + rr * (1.0 - sg)))
            d_og = d_out * (rr * sg)
            dgg_ref[:, vs] += jnp.sum(d_og * on, axis=0, keepdims=True)
            d_on = d_og * gg
            d_o = rs * (d_on - on * jnp.mean(d_on * on, axis=-1, keepdims=True))
            s_t = st_ref[h]
            ds_t = ds_ref[h]
            d_a = jnp.where(causal, _dot(d_o, vh, NT, CHUNK_PRECISION), 0.0)
            dv = _dot(a_mat, d_o, TN, CHUNK_PRECISION) + _dot(khat, ds_t, NT, CHUNK_PRECISION)
            dqt = _dot(d_a, kt, NN, CHUNK_PRECISION) + _dot(d_o, s_t, NN, CHUNK_PRECISION)
            dkt = _dot(d_a, qt, TN, CHUNK_PRECISION)
            dkhat = _dot(vh, ds_t, NN, CHUNK_PRECISION)
            ds_ref[h] = ds_t * e_last + _dot(d_o, qt, TN, CHUNK_PRECISION)
            dq = dqt * eb * scale
            dk = dkt * emb + dkhat * ehat
            db = dqt * qt - dkt * kt - dkhat * khat
            d_last = (jnp.sum(dkhat * khat, axis=0, keepdims=True)
                      + e_last * jnp.sum(ds_t * s_t, axis=0, keepdims=True))
            dla_parts.append(_dot(upper, db, NN, HIGHEST) + d_last)
            dqkvr_ref[:, ks] = dq.astype(BF16)
            dqkvr_ref[:, DK + h * HK:DK + (h + 1) * HK] = dk.astype(BF16)
            dqkvr_ref[:, DV + h * HV:DV + (h + 1) * HV] = dv.astype(BF16)
            dqkvr_ref[:, 2 * DV + h * HV:2 * DV + (h + 1) * HV] = dr.astype(BF16)
        dla = jnp.concatenate(dla_parts, axis=1)
        dgp = dla * (1.0 / GLA_GATE_NORM) * _sigmoid(-gp)
        dba_ref[...] += jnp.sum(dgp, axis=0, keepdims=True)
        dgp_b = dgp.astype(BF16)
        dw2_ref[...] += _dot(a_ref[...].astype(BF16), dgp_b, TN)
        da_ref[...] = _dot(dgp_b, w2_ref[...], NT).astype(BF16)

    rev = lambda j: pl.BlockSpec((C, DV), lambda i: (n - 1 - i, j))
    full = lambda s: pl.BlockSpec(s, lambda i: (0,) * len(s))
    return _pcall(
        body, name="gla_bwd",
        out_shape=(jax.ShapeDtypeStruct((T, 3 * DV), BF16), jax.ShapeDtypeStruct((T, RP), BF16),
                   jax.ShapeDtypeStruct((RP, DK), F32), jax.ShapeDtypeStruct((1, DK), F32),
                   jax.ShapeDtypeStruct((1, DV), F32)),
        grid=(n,),
        in_specs=[rev(0), rev(1), rev(2), pl.BlockSpec((C, RP), lambda i: (n - 1 - i, 0)),
                  full((RP, DK)), full((1, DK)), full((1, DV)), rev(0),
                  pl.BlockSpec((None, H, HV, HK), lambda i: (n - 1 - i, 0, 0, 0)), rev(0)],
        out_specs=(pl.BlockSpec((C, 3 * DV), lambda i: (n - 1 - i, 0)),
                   pl.BlockSpec((C, RP), lambda i: (n - 1 - i, 0)),
                   full((RP, DK)), full((1, DK)), full((1, DV))),
        scratch_shapes=[pltpu.VMEM((H, HV, HK), F32)],
        semantics=("arbitrary",))(proj, proj, proj, a_pad, w2, b_a, g_gla, o_raw, states, do_gla)


def _pool_windows(p, g, T):
    t = lax.broadcasted_iota(jnp.int32, (T, 1), 0)
    s = p
    for lvl in range(POOL_GROUPS):
        sh = 1 << lvl
        nxt = s + jnp.where(t >= sh, pltpu.roll(s, sh, 0), 0.0)
        s = jnp.where(lvl <= g, nxt, s)
    win = jnp.left_shift(2, g)
    inv = 1.0 / jnp.minimum(t + 1, win).astype(F32)
    return s * inv - p, inv


def _pool_fwd(proj, w_pool, scale, *, T, PW, col_block):
    GW = PW // POOL_GROUPS
    per = PW // GW

    def body(p_ref, w_ref, s_ref, o_ref):
        g = pl.program_id(0)
        pooled, _ = _pool_windows(p_ref[...], g, T)
        mixed = _dot(pooled.astype(BF16), w_ref[...])
        o_ref[...] = (mixed * s_ref[...]).astype(BF16)

    return _pcall(body, name="pool_fwd", out_shape=jax.ShapeDtypeStruct((T, PW), BF16),
                  grid=(POOL_GROUPS,),
                  in_specs=[pl.BlockSpec((T, GW), lambda g: (0, col_block * per + g)),
                            pl.BlockSpec((None, GW, GW), lambda g: (g, 0, 0)),
                            pl.BlockSpec((1, GW), lambda g: (0, g))],
                  out_specs=pl.BlockSpec((T, GW), lambda g: (0, g)),
                  semantics=("parallel",))(proj, w_pool, scale)


def _pool_bwd(proj, w_pool, scale, do_pool, *, T, PW, col_block):
    GW = PW // POOL_GROUPS
    per = PW // GW

    def body(p_ref, w_ref, s_ref, do_ref, dp_ref, dw_ref, dsc_ref):
        g = pl.program_id(0)
        pooled, inv = _pool_windows(p_ref[...], g, T)
        pooled_b = pooled.astype(BF16)
        w = w_ref[...]
        mixed = _dot(pooled_b, w)
        d_out = do_ref[...]
        dsc_ref[...] = jnp.sum(d_out * mixed, axis=0, keepdims=True)
        dmixed = (d_out * s_ref[...]).astype(BF16)
        dw_ref[...] = _dot(pooled_b, dmixed, TN)
        dpooled = _dot(dmixed, w, NT)
        t = lax.broadcasted_iota(jnp.int32, (T, 1), 0)
        s = dpooled * inv
        for lvl in range(POOL_GROUPS):
            sh = 1 << lvl
            nxt = s + jnp.where(t < T - sh, pltpu.roll(s, T - sh, 0), 0.0)
            s = jnp.where(lvl <= g, nxt, s)
        dp_ref[...] = (s - dpooled).astype(BF16)

    return _pcall(body, name="pool_bwd",
                  out_shape=(jax.ShapeDtypeStruct((T, PW), BF16),
                             jax.ShapeDtypeStruct((POOL_GROUPS, GW, GW), F32),
                             jax.ShapeDtypeStruct((1, PW), F32)),
                  grid=(POOL_GROUPS,),
                  in_specs=[pl.BlockSpec((T, GW), lambda g: (0, col_block * per + g)),
                            pl.BlockSpec((None, GW, GW), lambda g: (g, 0, 0)),
                            pl.BlockSpec((1, GW), lambda g: (0, g)),
                            pl.BlockSpec((T, GW), lambda g: (0, g))],
                  out_specs=(pl.BlockSpec((T, GW), lambda g: (0, g)),
                             pl.BlockSpec((None, GW, GW), lambda g: (g, 0, 0)),
                             pl.BlockSpec((1, GW), lambda g: (0, g))),
                  semantics=("parallel",))(proj, w_pool, scale, do_pool)


def _merge_fwd(y_gla, y_pool, proj, *, T, D, col_block):
    tr = _tile(T, 128, 16)

    def body(yg_ref, yp_ref, g1_ref, g2_ref, o_ref):
        o_ref[...] = (_sigmoid(g1_ref[...]) * yg_ref[...]
                      + _sigmoid(g2_ref[...]) * yp_ref[...]).astype(BF16)

    row = pl.BlockSpec((tr, D), lambda i: (i, 0))
    return _pcall(body, name="merge_fwd", out_shape=jax.ShapeDtypeStruct((T, D), BF16),
                  grid=(T // tr,),
                  in_specs=[row, row, pl.BlockSpec((tr, D), lambda i: (i, col_block)),
                            pl.BlockSpec((tr, D), lambda i: (i, col_block + 1))],
                  out_specs=row, semantics=("parallel",))(y_gla, y_pool, proj, proj)


def _merge_bwd(dmerged, y_gla, y_pool, proj, *, T, D, col_block):
    tr = _tile(T, 128, 16)

    def body(dm_ref, yg_ref, yp_ref, g1_ref, g2_ref, dyg_ref, dyp_ref, dg_ref):
        dm = dm_ref[...]
        s1 = _sigmoid(g1_ref[...])
        s2 = _sigmoid(g2_ref[...])
        dyg_ref[...] = (dm * s1).astype(BF16)
        dyp_ref[...] = (dm * s2).astype(BF16)
        dg_ref[:, :D] = (dm * yg_ref[...] * s1 * (1.0 - s1)).astype(BF16)
        dg_ref[:, D:] = (dm * yp_ref[...] * s2 * (1.0 - s2)).astype(BF16)

    row = pl.BlockSpec((tr, D), lambda i: (i, 0))
    return _pcall(body, name="merge_bwd",
                  out_shape=(jax.ShapeDtypeStruct((T, D), BF16), jax.ShapeDtypeStruct((T, D), BF16),
                             jax.ShapeDtypeStruct((T, 2 * D), BF16)),
                  grid=(T // tr,),
                  in_specs=[row, row, row, pl.BlockSpec((tr, D), lambda i: (i, col_block)),
                            pl.BlockSpec((tr, D), lambda i: (i, col_block + 1))],
                  out_specs=(row, row, pl.BlockSpec((tr, 2 * D), lambda i: (i, 0))),
                  semantics=("parallel",))(dmerged, y_gla, y_pool, proj, proj)


def _attn_fwd(q, kv, *, T, D, M):
    H = CROSS_HEADS
    HD = D // H
    tq = _tile(T, 512, 16)
    scale = HD ** -0.5

    def body(q_ref, kv_ref, o_ref):
        for h in range(H):
            hs = slice(h * HD, (h + 1) * HD)
            s = _dot(q_ref[:, hs], kv_ref[:, hs], NT) * scale
            e = jnp.exp(s - jnp.max(s, axis=-1, keepdims=True))
            p = e / jnp.sum(e, axis=-1, keepdims=True)
            o_ref[:, hs] = _dot(p.astype(BF16), kv_ref[:, D + h * HD:D + (h + 1) * HD]).astype(BF16)

    row = pl.BlockSpec((tq, D), lambda i: (i, 0))
    return _pcall(body, name="attn_fwd", out_shape=jax.ShapeDtypeStruct((T, D), BF16),
                  grid=(T // tq,), in_specs=[row, pl.BlockSpec((M, 2 * D), lambda i: (0, 0))],
                  out_specs=row, semantics=("parallel",))(q, kv)


def _attn_bwd(q, kv, do, *, T, D, M):
    H = CROSS_HEADS
    HD = D // H
    tq = _tile(T, 512, 16)
    scale = HD ** -0.5

    def body(q_ref, kv_ref, do_ref, dq_ref, dkv_ref):
        @pl.when(pl.program_id(0) == 0)
        def _():
            dkv_ref[...] = jnp.zeros_like(dkv_ref)

        for h in range(H):
            hs = slice(h * HD, (h + 1) * HD)
            vs = slice(D + h * HD, D + (h + 1) * HD)
            qh = q_ref[:, hs]
            kh = kv_ref[:, hs]
            s = _dot(qh, kh, NT) * scale
            e = jnp.exp(s - jnp.max(s, axis=-1, keepdims=True))
            p = e / jnp.sum(e, axis=-1, keepdims=True)
            p_b = p.astype(BF16)
            d_o = do_ref[:, hs]
            dkv_ref[:, vs] += _dot(p_b, d_o, TN)
            dp = _dot(d_o, kv_ref[:, vs], NT)
            ds = (p * (dp - jnp.sum(dp * p, axis=-1, keepdims=True)) * scale).astype(BF16)
            dq_ref[:, hs] = _dot(ds, kh).astype(BF16)
            dkv_ref[:, hs] += _dot(ds, qh, TN)

    row = pl.BlockSpec((tq, D), lambda i: (i, 0))
    full = pl.BlockSpec((M, 2 * D), lambda i: (0, 0))
    return _pcall(body, name="attn_bwd",
                  out_shape=(jax.ShapeDtypeStruct((T, D), BF16), jax.ShapeDtypeStruct((M, 2 * D), F32)),
                  grid=(T // tq,), in_specs=[row, full, row], out_specs=(row, full),
                  semantics=("arbitrary",))(q, kv, do)


def _shift_down(x, halo, s):
    out = pltpu.roll(x, s, 0)
    t8 = lax.broadcasted_iota(jnp.int32, (SUBLANES, 1), 0)
    head = out[:SUBLANES]
    for j in range(s):
        head = jnp.where(t8 == j, halo[SUBLANES - s + j:SUBLANES - s + j + 1, :], head)
    return head if x.shape[0] == SUBLANES else jnp.concatenate([head, out[SUBLANES:]], axis=0)


def _shift_up(x, halo, s):
    rows = x.shape[0]
    out = pltpu.roll(x, rows - s, 0)
    t8 = lax.broadcasted_iota(jnp.int32, (SUBLANES, 1), 0)
    tail = out[rows - SUBLANES:]
    for j in range(s):
        tail = jnp.where(t8 == SUBLANES - s + j, halo[j:j + 1, :], tail)
    return jnp.concatenate([out[:rows - SUBLANES], tail], axis=0)


def _conv_tiles(T):
    tt = _tile(T, 128, SUBLANES)
    return tt, tt // SUBLANES, T // SUBLANES


def _conv_fwd(u0, conv_w, conv_b, *, T, F):
    tt, hb, _ = _conv_tiles(T)
    cw = _tile(F, LANES)

    def body(u_ref, prev_ref, w_ref, b_ref, f_ref):
        i = pl.program_id(0)

        def conv(cs):
            x = u_ref[:, cs]
            halo = jnp.where(i > 0, prev_ref[:, cs], 0.0)
            return (w_ref[2:3, cs] * x + w_ref[1:2, cs] * _shift_down(x, halo, 1)
                    + w_ref[0:1, cs] * _shift_down(x, halo, 2) + b_ref[:, cs])

        for j in range(F // cw):
            gate = conv(slice(j * cw, (j + 1) * cw))
            val = conv(slice(F + j * cw, F + (j + 1) * cw))
            f_ref[:, j * cw:(j + 1) * cw] = (gate * _sigmoid(gate) * val).astype(BF16)

    return _pcall(body, name="conv_fwd", out_shape=jax.ShapeDtypeStruct((T, F), BF16),
                  grid=(T // tt,),
                  in_specs=[pl.BlockSpec((tt, 2 * F), lambda i: (i, 0)),
                            pl.BlockSpec((SUBLANES, 2 * F), lambda i: (jnp.maximum(i * hb - 1, 0), 0)),
                            pl.BlockSpec((CONV_W, 2 * F), lambda i: (0, 0)),
                            pl.BlockSpec((1, 2 * F), lambda i: (0, 0))],
                  out_specs=pl.BlockSpec((tt, F), lambda i: (i, 0)),
                  semantics=("parallel",))(u0, u0, conv_w, conv_b)


def _conv_bwd(u0, conv_w, conv_b, df, *, T, F):
    tt, hb, nb = _conv_tiles(T)
    nt = T // tt
    cw = _tile(F, LANES)

    def body(u_ref, prev_ref, next_ref, df_ref, dfn_ref, w_ref, b_ref, du0_ref, dw_ref, db_ref):
        i = pl.program_id(0)

        @pl.when(i == 0)
        def _():
            dw_ref[...] = jnp.zeros_like(dw_ref)
            db_ref[...] = jnp.zeros_like(db_ref)

        def conv(cs):
            x = u_ref[:, cs]
            halo = jnp.where(i > 0, prev_ref[:, cs], 0.0)
            x1 = _shift_down(x, halo, 1)
            x2 = _shift_down(x, halo, 2)
            u = w_ref[2:3, cs] * x + w_ref[1:2, cs] * x1 + w_ref[0:1, cs] * x2 + b_ref[:, cs]
            xn = next_ref[:, cs]
            tail = x[tt - SUBLANES:, :]
            un = (w_ref[2:3, cs] * xn + w_ref[1:2, cs] * _shift_down(xn, tail, 1)
                  + w_ref[0:1, cs] * _shift_down(xn, tail, 2) + b_ref[:, cs])
            return u, un, (x, x1, x2)

        def glu_grad(gate, val, dff):
            sg = _sigmoid(gate)
            return dff * val * (sg * (1.0 + gate * (1.0 - sg))), dff * (gate * sg)

        def finish(cs, du, dun, xs):
            du0 = (w_ref[2:3, cs] * du + w_ref[1:2, cs] * _shift_up(du, dun, 1)
                   + w_ref[0:1, cs] * _shift_up(du, dun, 2))
            du0_ref[:, cs] = du0.astype(BF16)
            db_ref[:, cs] += jnp.sum(du, axis=0, keepdims=True)
            dw_ref[2:3, cs] += jnp.sum(du * xs[0], axis=0, keepdims=True)
            dw_ref[1:2, cs] += jnp.sum(du * xs[1], axis=0, keepdims=True)
            dw_ref[0:1, cs] += jnp.sum(du * xs[2], axis=0, keepdims=True)

        for j in range(F // cw):
            fs = slice(j * cw, (j + 1) * cw)
            gs, vs = fs, slice(F + j * cw, F + (j + 1) * cw)
            ug, ung, xg = conv(gs)
            uv, unv, xv = conv(vs)
            dug, duv = glu_grad(ug, uv, df_ref[:, fs].astype(F32))
            dung, dunv = glu_grad(ung, unv, dfn_ref[0:SUBLANES, fs].astype(F32))
            dung = jnp.where(i < nt - 1, dung, 0.0)
            dunv = jnp.where(i < nt - 1, dunv, 0.0)
            finish(gs, dug, dung, xg)
            finish(vs, duv, dunv, xv)

    wide = lambda rows, fn: pl.BlockSpec((rows, 2 * F), fn)
    nxt = lambda i: (jnp.minimum((i + 1) * hb, nb - 1), 0)
    return _pcall(body, name="conv_bwd",
                  out_shape=(jax.ShapeDtypeStruct((T, 2 * F), BF16),
                             jax.ShapeDtypeStruct((CONV_W, 2 * F), F32),
                             jax.ShapeDtypeStruct((1, 2 * F), F32)),
                  grid=(nt,),
                  in_specs=[wide(tt, lambda i: (i, 0)),
                            wide(SUBLANES, lambda i: (jnp.maximum(i * hb - 1, 0), 0)),
                            wide(SUBLANES, nxt),
                            pl.BlockSpec((tt, F), lambda i: (i, 0)),
                            pl.BlockSpec((2 * SUBLANES, F),
                                         lambda i: (jnp.minimum((i + 1) * (hb // 2), nb // 2 - 1), 0)),
                            wide(CONV_W, lambda i: (0, 0)), wide(1, lambda i: (0, 0))],
                  out_specs=(wide(tt, lambda i: (i, 0)), wide(CONV_W, lambda i: (0, 0)),
                             wide(1, lambda i: (0, 0))),
                  semantics=("arbitrary",))(u0, u0, u0, df, df, conv_w, conv_b)


def _adamw(w, g, m, v, *, name):
    R, C = w.shape
    tr = _tile(R, max(SUBLANES, (1 << 19) // max(C, 1) // SUBLANES * SUBLANES), SUBLANES)
    c1 = 1.0 / (1.0 - ADAM_B1 ** ADAM_STEP)
    c2 = 1.0 / (1.0 - ADAM_B2 ** ADAM_STEP)

    def body(w_ref, g_ref, m_ref, v_ref, d_ref, mo_ref, vo_ref):
        gv = g_ref[...]
        mn = ADAM_B1 * m_ref[...] + (1.0 - ADAM_B1) * gv
        vn = ADAM_B2 * v_ref[...] + (1.0 - ADAM_B2) * (gv * gv)
        d_ref[...] = -ADAM_LR * ((mn * c1) / (jnp.sqrt(vn * c2) + ADAM_EPS) + ADAM_WD * w_ref[...])
        mo_ref[...] = mn
        vo_ref[...] = vn

    blk = pl.BlockSpec((tr, C), lambda i: (i, 0))
    shp = jax.ShapeDtypeStruct((R, C), F32)
    return _pcall(body, name=name, out_shape=(shp, shp, shp), grid=(R // tr,),
                  in_specs=[blk] * 4, out_specs=(blk,) * 3, semantics=("parallel",))(w, g, m, v)


def _blk(h, C, elems=1 << 19, align=16):
    th = _tile(h, max(align, elems // C // align * align), align)
    if th < h or h * C <= 2 * elems:
        return th, C
    return h, _tile(C, max(LANES, elems // h // LANES * LANES))


def _adamw_halves(w, m, v, g_mine, g_other, c_idx, *, name):
    _, h, C = w.shape
    th, tc = _blk(h, C, align=SUBLANES)
    c1 = 1.0 / (1.0 - ADAM_B1 ** ADAM_STEP)
    c2 = 1.0 / (1.0 - ADAM_B2 ** ADAM_STEP)

    def body(c_ref, w_ref, m_ref, v_ref, gm_ref, go_ref, g_ref, d_ref, mo_ref, vo_ref):
        gv = jnp.where(pl.program_id(0) == c_ref[0], gm_ref[...], go_ref[...])
        mn = ADAM_B1 * m_ref[...] + (1.0 - ADAM_B1) * gv
        vn = ADAM_B2 * v_ref[...] + (1.0 - ADAM_B2) * (gv * gv)
        d_ref[...] = -ADAM_LR * ((mn * c1) / (jnp.sqrt(vn * c2) + ADAM_EPS) + ADAM_WD * w_ref[...])
        g_ref[...] = gv
        mo_ref[...] = mn
        vo_ref[...] = vn

    blk = pl.BlockSpec((None, th, tc), lambda s, i, j, c: (s, i, j))

    def pick(mine):
        def index(s, i, j, c):
            use = (s == c[0]) if mine else (s != c[0])
            return jnp.where(use, i, 0), jnp.where(use, j, 0)
        return pl.BlockSpec((th, tc), index)

    shp = jax.ShapeDtypeStruct((2, h, C), F32)
    return _pcall(body, name=name, out_shape=(shp,) * 4, grid=(2, h // th, C // tc), prefetch=1,
                  in_specs=[blk, blk, blk, pick(True), pick(False)], out_specs=(blk,) * 4,
                  semantics=("parallel", "parallel", "parallel"))(c_idx, w, m, v, g_mine, g_other)


def _mesh_pos():
    x, y, c = lax.axis_index("x"), lax.axis_index("y"), lax.axis_index("c")
    others = [(1 - x, y), (x, 1 - y), (1 - x, 1 - y)]
    return x, y, c, others


def _gather_copies(shards, lands, send_sems, recv_sems):
    x, y, c, others = _mesh_pos()
    me = 2 * x + y
    return [pltpu.make_async_remote_copy(
        src_ref=shards[a].at[c], dst_ref=lands[a].at[me, c],
        send_sem=send_sems.at[3 * a + j], recv_sem=recv_sems.at[3 * a + j],
        device_id=(*chip, c), device_id_type=MESH)
        for a in range(len(shards)) for j, chip in enumerate(others)]


def _near_copies(shards, lands, send_sems, recv_sems):
    x, y, c, others = _mesh_pos()
    me = 2 * x + y
    return [pltpu.make_async_remote_copy(
        src_ref=shards[a].at[c], dst_ref=lands[a].at[me, c],
        send_sem=send_sems.at[2 * a + j], recv_sem=recv_sems.at[2 * a + j],
        device_id=(*chip, c), device_id_type=MESH)
        for a in range(len(shards)) for j, chip in enumerate(others[:2])]


def _relay_copies(shards, zones, send_sems, recv_sems):
    x, y, c, others = _mesh_pos()
    (nx, ny), copies = others[:2], []
    for a in range(len(zones)):
        hc = zones[a].shape[-1] // 2
        for k, (src_chip, to, lo) in enumerate(((ny, nx, 0), (nx, ny, hc))):
            part = zones[a].at[2 * src_chip[0] + src_chip[1], c, :, pl.ds(lo, hc)]
            copies.append(pltpu.make_async_remote_copy(
                src_ref=part, dst_ref=part, send_sem=send_sems.at[2 * a + k], recv_sem=recv_sems.at[2 * a + k],
                device_id=(*to, c), device_id_type=MESH))
    return copies


def _pass_copies(shards, zones, send_sems, recv_sems, pieces=(0, 1, 2, 3)):
    x, y, c, others = _mesh_pos()
    me = 2 * x + y
    copies = []
    for a in range(len(shards)):
        srcs = [zones[a].at[2 * chip[0] + chip[1], c] for chip in others] + [shards[a]]
        dsts = [zones[a].at[2 * chip[0] + chip[1], c] for chip in others] + [zones[a].at[me]]
        copies += [pltpu.make_async_remote_copy(
            src_ref=srcs[p], dst_ref=dsts[p], send_sem=send_sems.at[len(pieces) * a + k],
            recv_sem=recv_sems.at[len(pieces) * a + k], device_id=(x, y, 1 - c), device_id_type=MESH)
            for k, p in enumerate(pieces)]
    return copies


def _exchange_copies(grads, recvs, send_sems, recv_sems):
    x, y, c, _ = _mesh_pos()
    return [pltpu.make_async_remote_copy(
        src_ref=grads[a].at[:, 1 - c], dst_ref=recvs[a], send_sem=send_sems.at[a],
        recv_sem=recv_sems.at[a], device_id=(x, y, 1 - c), device_id_type=MESH) for a in range(len(grads))]


def _split_start(copies, per, srcs, zones, after, *, name):
    n = len(srcs)
    HBM = pl.BlockSpec(memory_space=pltpu.HBM)
    SEM = pl.BlockSpec(memory_space=pltpu.SEMAPHORE)

    def body(*refs):
        send_sems, recv_sems = refs[2 * n + 1], refs[2 * n + 2]
        for cp in copies(refs[:n], refs[n:2 * n], send_sems, recv_sems):
            cp.start()
        refs[-1][...] = jnp.zeros_like(refs[-1])

    hbm = lambda a: pltpu.HBM(a.shape, a.dtype)
    res = _pcall(body, name=name,
                 out_shape=(pltpu.SemaphoreType.DMA((per * n,)), pltpu.SemaphoreType.DMA((per * n,)),
                            *[hbm(a) for a in srcs], *[hbm(a) for a in zones],
                            jax.ShapeDtypeStruct((SUBLANES, LANES), F32)),
                 in_specs=[*[HBM] * (2 * n), pl.BlockSpec(memory_space=pl.ANY)],
                 out_specs=(SEM, SEM, *[HBM] * (2 * n), pl.BlockSpec(memory_space=pltpu.VMEM)),
                 aliases={i: 2 + i for i in range(2 * n)}, split_copy=True)(
        *[pltpu.with_memory_space_constraint(a, pltpu.HBM) for a in [*srcs, *zones]], after)
    return res[0], res[1], list(res[2:2 + n]), list(res[2 + n:2 + 2 * n]), res[-1]


def _split_wait(copies, send_sems, recv_sems, srcs, zones, after, *, name):
    n = len(srcs)
    HBM = pl.BlockSpec(memory_space=pltpu.HBM)
    SEM = pl.BlockSpec(memory_space=pltpu.SEMAPHORE)

    def body(*refs):
        for cp in copies(refs[:n], refs[n:2 * n], refs[2 * n], refs[2 * n + 1]):
            cp.wait_send()
            cp.wait_recv()

    hbm = lambda a: pltpu.HBM(a.shape, a.dtype)
    res = _pcall(body, name=name, out_shape=(*[hbm(a) for a in srcs], *[hbm(a) for a in zones]),
                 in_specs=[*[HBM] * (2 * n), SEM, SEM, pl.BlockSpec(memory_space=pl.ANY)],
                 out_specs=tuple([HBM] * (2 * n)), aliases={i: i for i in range(2 * n)},
                 split_copy=True)(*srcs, *zones, send_sems, recv_sems, after)
    return list(res[:n]), list(res[n:])


def _add_halves(grad, recv, c_idx, *, name):
    S, _, h, C = grad.shape
    th, tc = _blk(h, C)

    def body(c_ref, g_ref, r_ref, o_ref):
        o_ref[...] = (g_ref[...].astype(F32) + r_ref[...].astype(F32)).astype(o_ref.dtype)

    return _pcall(body, name=name, out_shape=jax.ShapeDtypeStruct((S, h, C), grad.dtype),
                  grid=(S, h // th, C // tc), prefetch=1,
                  in_specs=[pl.BlockSpec((None, None, th, tc), lambda s, i, j, c: (s, c[0], i, j)),
                            pl.BlockSpec((None, th, tc), lambda s, i, j, c: (s, i, j))],
                  out_specs=pl.BlockSpec((None, th, tc), lambda s, i, j, c: (s, i, j)),
                  semantics=("parallel", "parallel", "parallel"))(c_idx, grad, recv)


def _scatter_copies(srcs, lands, send_sems, recv_sems):
    x, y, c, others = _mesh_pos()
    return [pltpu.make_async_remote_copy(
        src_ref=srcs[a].at[2 * chip[0] + chip[1]], dst_ref=lands[a].at[j],
        send_sem=send_sems.at[3 * a + j], recv_sem=recv_sems.at[3 * a + j],
        device_id=(*chip, c), device_id_type=MESH)
        for a in range(len(srcs)) for j, chip in enumerate(others)]


def _add_chips(sums, recv, chip_idx, *, name):
    _, h, C = sums.shape
    th, tc = _blk(h, C)

    def body(k_ref, s_ref, r_ref, o_ref):
        acc = s_ref[...].astype(F32) + r_ref[0].astype(F32)
        acc = acc + r_ref[1].astype(F32)
        o_ref[...] = acc + r_ref[2].astype(F32)

    return _pcall(body, name=name, out_shape=jax.ShapeDtypeStruct((h, C), F32),
                  grid=(h // th, C // tc), prefetch=1,
                  in_specs=[pl.BlockSpec((None, th, tc), lambda i, j, k: (k[0], i, j)),
                            pl.BlockSpec((3, th, tc), lambda i, j, k: (0, i, j))],
                  out_specs=pl.BlockSpec((th, tc), lambda i, j, k: (i, j)),
                  semantics=("parallel", "parallel"))(chip_idx, sums, recv)


def _swap_copies(halves, others, send_sems, recv_sems):
    x, y, c, _ = _mesh_pos()
    return [pltpu.make_async_remote_copy(
        src_ref=halves[a], dst_ref=others[a], send_sem=send_sems.at[a], recv_sem=recv_sems.at[a],
        device_id=(x, y, 1 - c), device_id_type=MESH) for a in range(len(halves))]


def _all_reduce_small(buf):
    R, L = buf.shape
    NDEV = 8

    def body(x_ref, sum_ref, all_ref, send_sems, recv_sems, local_sem):
        x, y, c, others = _mesh_pos()
        me, sibling = (x, y, c), (x, y, 1 - c)

        def slot(px, py, pc):
            return all_ref.at[4 * px + 2 * py + pc]

        def copy(k, block, to, src=None):
            return pltpu.make_async_remote_copy(
                src_ref=slot(*block) if src is None else src, dst_ref=slot(*block),
                send_sem=send_sems.at[k], recv_sem=recv_sems.at[k], device_id=to, device_id_type=MESH)

        mine = pltpu.make_async_copy(x_ref, slot(*me), local_sem)
        mine.start()
        first = [copy(0, me, sibling, src=x_ref)]
        first += [copy(1 + j, me, (*chip, c), src=x_ref) for j, chip in enumerate(others)]
        for cp in first:
            cp.start()
        passed = [copy(4 + j, (*chip, c), sibling) for j, chip in enumerate(others)]
        for j, chip in enumerate(others):
            copy(1 + j, (*chip, c), me).wait_recv()
            passed[j].start()
        copy(0, sibling, me).wait_recv()
        for j, chip in enumerate(others):
            copy(4 + j, (*chip, 1 - c), me).wait_recv()
        for cp in first + passed:
            cp.wait_send()
        mine.wait()
        acc = all_ref[0]
        for d in range(1, NDEV):
            acc = acc + all_ref[d]
        sum_ref[...] = acc

    VM = pl.BlockSpec(memory_space=pltpu.VMEM)
    return _pcall(body, name="all_reduce_small",
                  out_shape=(jax.ShapeDtypeStruct((R, L), F32), jax.ShapeDtypeStruct((NDEV, R, L), F32)),
                  in_specs=[VM], out_specs=(VM, VM),
                  scratch_shapes=[pltpu.SemaphoreType.DMA((7,)), pltpu.SemaphoreType.DMA((7,)),
                                  pltpu.SemaphoreType.DMA])(buf)[0]


def _pack(arrs, rows_multiple=16):
    flat = [a.reshape(-1).astype(F32) for a in arrs]
    sizes = [f.shape[0] for f in flat]
    total = sum(sizes)
    per = LANES * rows_multiple
    padded = -(-total // per) * per
    flat.append(jnp.zeros((padded - total,), F32))
    offs = [0]
    for s in sizes:
        offs.append(offs[-1] + s)
    return jnp.concatenate(flat).reshape(padded // LANES, LANES), offs


def _unpack(buf, offs, shapes):
    flat = buf.reshape(-1)
    return [flat[offs[i]:offs[i + 1]].reshape(s) for i, s in enumerate(shapes)]


def kernel(x, mem, g_mix, w_in, w_a2, b_a, g_gla, w_pool, pool_scale, w_branch, w_out, g_cross, g_mem, w_cq, w_ckv, w_co, g_ffn, w_up, conv_w, conv_b, w_down, g_final, loss_target, m_g_mix, m_w_in, m_w_a2, m_b_a, m_g_gla, m_w_pool, m_pool_scale, m_w_branch, m_w_out, m_g_cross, m_g_mem, m_w_cq, m_w_ckv, m_w_co, m_g_ffn, m_w_up, m_conv_w, m_conv_b, m_w_down, m_g_final, v_g_mix, v_w_in, v_w_a2, v_b_a, v_g_gla, v_w_pool, v_pool_scale, v_w_branch, v_w_out, v_g_cross, v_g_mem, v_w_cq, v_w_ckv, v_w_co, v_g_ffn, v_w_up, v_conv_w, v_conv_b, v_w_down, v_g_final):
    weights = dict(g_mix=g_mix, w_in=w_in, w_a2=w_a2, b_a=b_a, g_gla=g_gla, w_pool=w_pool,
                   pool_scale=pool_scale, w_branch=w_branch, w_out=w_out, g_cross=g_cross, g_mem=g_mem,
                   w_cq=w_cq, w_ckv=w_ckv, w_co=w_co, g_ffn=g_ffn, w_up=w_up, conv_w=conv_w,
                   conv_b=conv_b, w_down=w_down, g_final=g_final)
    mom_m = dict(g_mix=m_g_mix, w_in=m_w_in, w_a2=m_w_a2, b_a=m_b_a, g_gla=m_g_gla, w_pool=m_w_pool,
                 pool_scale=m_pool_scale, w_branch=m_w_branch, w_out=m_w_out, g_cross=m_g_cross,
                 g_mem=m_g_mem, w_cq=m_w_cq, w_ckv=m_w_ckv, w_co=m_w_co, g_ffn=m_g_ffn, w_up=m_w_up,
                 conv_w=m_conv_w, conv_b=m_conv_b, w_down=m_w_down, g_final=m_g_final)
    mom_v = dict(g_mix=v_g_mix, w_in=v_w_in, w_a2=v_w_a2, b_a=v_b_a, g_gla=v_g_gla, w_pool=v_w_pool,
                 pool_scale=v_pool_scale, w_branch=v_w_branch, w_out=v_w_out, g_cross=v_g_cross,
                 g_mem=v_g_mem, w_cq=v_w_cq, w_ckv=v_w_ckv, w_co=v_w_co, g_ffn=v_g_ffn, w_up=v_w_up,
                 conv_w=v_conv_w, conv_b=v_conv_b, w_down=v_w_down, g_final=v_g_final)
    order = list(weights)
    big = ["w_in", "w_branch", "w_out", "w_cq", "w_ckv", "w_co", "w_up", "w_down"]
    small_sharded = ["w_a2", "w_pool", "conv_w"]
    small_repl = ["g_mix", "b_a", "g_gla", "pool_scale", "g_cross", "g_mem", "g_ffn", "conv_b", "g_final"]

    xs, ms, tgt = x[0], mem[0], loss_target[0]
    T, D = xs.shape
    M = ms.shape[0]
    DK, DV, PW = b_a.shape[1], g_gla.shape[1], pool_scale.shape[1]
    RANK = w_a2.shape[1]
    F2 = conv_b.shape[1]
    F = F2 // 2
    DIN = N_CHIPS * w_in.shape[2]
    OFF_A = 2 * DK + 2 * DV
    OFF_P = OFF_A + RANK
    RP = LANES
    GW = PW // POOL_GROUPS
    assert PW == DV and 4 * DV == 2 * D and OFF_P + PW + 2 * D == DIN

    cx, cy, cc = lax.axis_index("x"), lax.axis_index("y"), lax.axis_index("c")
    chip = 2 * cx + cy
    c_idx = jnp.reshape(cc, (1,)).astype(jnp.int32)
    chip_idx = jnp.reshape(chip, (1,)).astype(jnp.int32)

    def halves(a):
        return a.reshape(2, a.shape[0] // 2, a.shape[1])

    shard2d = {k: (weights[k][0].T if k == "w_in" else weights[k][0]) for k in big}
    small_pack, small_offs = _pack([weights[k][0] for k in small_sharded], rows_multiple=32)
    flying, passing = {}, {}

    def gather_start(group, keys, tok):
        srcs = [small_pack if k == "small" else shard2d[k].astype(BF16) for k in keys]
        if group != "in":
            srcs = [a + tok[0:1, 0:1].astype(a.dtype) for a in srcs]
        srcs = [halves(a) for a in srcs]
        zones = [lax.empty((N_CHIPS, *s.shape), s.dtype) for s in srcs]
        first = (_near_copies, 2) if group == "in" else (_gather_copies, 3)
        s_sems, r_sems, srcs, zones, tok = _split_start(*first, srcs, zones, tok, name=f"gather_start_{group}")
        flying[group] = (keys, s_sems, r_sems, srcs, zones)
        return tok

    tok = gather_start("in", ["w_in"], xs)

    def arrive_in(after):
        keys, s_sems, r_sems, srcs, zones = flying["in"]
        near, diag = functools.partial(_pass_copies, pieces=(0, 1, 3)), functools.partial(_pass_copies, pieces=(2,))
        srcs, zones = _split_wait(_near_copies, s_sems, r_sems, srcs, zones, after, name="gather_wait_in")
        rs, rr, srcs, zones, tok = _split_start(_relay_copies, 2, srcs, zones, after, name="gather_relay_start_in")
        ns, nr, srcs, zones, tok = _split_start(near, 3, srcs, zones, tok, name="gather_pass_near_start_in")
        for group, group_keys in (("mix", ["w_branch", "w_out", "small"]), ("cross", ["w_cq", "w_ckv", "w_co"]),
                                  ("up", ["w_up"]), ("down", ["w_down"])):
            tok = gather_start(group, group_keys, tok)
        after = tok
        srcs, zones = _split_wait(_relay_copies, rs, rr, srcs, zones, after, name="gather_relay_wait_in")
        ds, dr, srcs, zones, _ = _split_start(diag, 1, srcs, zones, after, name="gather_pass_diag_start_in")
        srcs, zones = _split_wait(near, ns, nr, srcs, zones, after, name="gather_pass_near_wait_in")
        _, full = _split_wait(diag, ds, dr, srcs, zones, after, name="gather_pass_diag_wait_in")
        return {k: f.reshape(N_CHIPS, f.shape[1] * f.shape[2], f.shape[3]) for k, f in zip(keys, full)}

    def landed(group, after):
        keys, s_sems, r_sems, srcs, zones = flying[group]
        srcs, zones = _split_wait(_gather_copies, s_sems, r_sems, srcs, zones, after,
                                  name=f"gather_wait_{group}")
        s_sems, r_sems, srcs, zones, token = _split_start(_pass_copies, 4, srcs, zones, after,
                                                          name=f"gather_pass_start_{group}")
        passing[group] = (keys, s_sems, r_sems, srcs, zones)
        return token

    def arrive(group, after):
        keys, s_sems, r_sems, srcs, zones = passing[group]
        _, full = _split_wait(_pass_copies, s_sems, r_sems, srcs, zones, after,
                              name=f"gather_pass_wait_{group}")
        return {k: f.reshape(N_CHIPS, f.shape[1] * f.shape[2], f.shape[3]) for k, f in zip(keys, full)}

    def rows(g):
        return g.reshape(-1, g.shape[2])

    h1, r1 = _rms_fwd(xs, g_mix + tok[0:1, 0:1], name="norm_mix")
    W_in = rows(arrive_in(h1)["w_in"])
    W_main = jnp.concatenate([W_in[:OFF_A], W_in[OFF_P:]], axis=0)
    W_a = jnp.pad(W_in[OFF_A:OFF_P], ((0, RP - RANK), (0, 0)))
    tok = landed("mix", W_a)
    proj = _mm(h1, W_main, "nt", name="proj_main", out_dtype=F32, after=tok)
    gw = arrive("mix", proj)
    W_branch, W_out, small_all = rows(gw["w_branch"]), rows(gw["w_out"]), gw["small"]
    sm = [_unpack(small_all[j], small_offs, [weights[k].shape[1:] for k in small_sharded]) for j in range(N_CHIPS)]
    W_a2 = jnp.concatenate([sm[j][0] for j in range(N_CHIPS)], axis=1)
    W_a2p = jnp.pad(W_a2, ((0, RP - RANK), (0, 0))).astype(BF16)
    W_pool = jnp.concatenate([sm[j][1] for j in range(N_CHIPS)], axis=1).astype(BF16)
    W_conv = jnp.concatenate([sm[j][2] for j in range(N_CHIPS)], axis=1)

    a_pad = _mm(h1, W_a, "nt", name="proj_gate_rank", out_dtype=F32)
    o_gla, o_raw, states = _gla_fwd(proj, a_pad, W_a2p, b_a, g_gla, T=T, DK=DK, DV=DV)
    o_pool = _pool_fwd(proj, W_pool, pool_scale, T=T, PW=PW, col_block=3)
    tok = landed("cross", o_pool)
    y_gla = _mm(o_gla, W_branch, "nn", name="branch_gla", out_dtype=BF16, K=DV, after=tok)
    y_pool = _mm(o_pool, W_branch, "nn", name="branch_pool", out_dtype=BF16, K=PW, b_off=(DV, 0))
    merged = _merge_fwd(y_gla, y_pool, proj, T=T, D=D, col_block=2)
    x1 = _mm(merged, W_out, "nn", name="mix_out", out_dtype=F32, add=xs)

    h2, r2 = _rms_fwd(x1, g_cross, name="norm_cross")
    mem_n, rm = _rms_fwd(ms, g_mem, name="norm_mem")
    gw = arrive("cross", h2)
    W_cq, W_ckv, W_co = rows(gw["w_cq"]), gw["w_ckv"], rows(gw["w_co"])
    qc = _mm(h2, W_cq, "nn", name="cross_q", out_dtype=BF16)
    kv = _mm(mem_n, W_ckv, "nn", name="cross_kv", out_dtype=BF16, b_blocked=True)
    o_att = _attn_fwd(qc, kv, T=T, D=D, M=M)
    x2 = _mm(o_att, W_co, "nn", name="cross_out", out_dtype=F32, add=x1)

    tok = landed("up", x2)
    h3, r3 = _rms_fwd(x2, g_ffn + tok[0:1, 0:1], name="norm_ffn")
    W_up = arrive("up", h3)["w_up"]
    u0 = _mm(h3, W_up, "nn", name="ffn_up", out_dtype=F32, b_blocked=True)
    tok = landed("down", u0)
    f_act = _conv_fwd(u0, W_conv, conv_b + tok[0:1, 0:1], T=T, F=F)
    W_down = rows(arrive("down", f_act)["w_down"])
    x3 = _mm(f_act, W_down, "nn", name="ffn_down", out_dtype=F32, add=x2, tk=F // 2)

    loss_part, dx3, dx3_b, dg_final = _loss_head(x3, g_final.reshape(1, D), tgt)

    def col_shards(g):
        nb, K, Nb = g.shape
        return g.reshape(nb, 2, K // 2, Nb)

    def row_shards(g):
        R, N = g.shape
        return g.reshape(N_CHIPS, 2, R // N_CHIPS // 2, N)

    exchanging, in_flight = {}, []

    def exchange_start(group, keys, partials, after):
        recvs = [lax.empty((p.shape[0], *p.shape[2:]), p.dtype) for p in partials]
        s_sems, r_sems, partials, recvs, token = _split_start(
            _exchange_copies, 1, partials, recvs, after, name=f"grad_exchange_start_{group}")
        exchanging[group] = (keys, s_sems, r_sems, partials, recvs)
        return token

    def scatter_start(group, after):
        keys, s_sems, r_sems, partials, recvs = exchanging[group]
        partials, recvs = _split_wait(_exchange_copies, s_sems, r_sems, partials, recvs, after,
                                      name=f"grad_exchange_wait_{group}")
        chip_sums = [_add_halves(p, r, c_idx, name=f"grad_add_halves_{k}")
                     for k, p, r in zip(keys, partials, recvs)]
        lands = [lax.empty((3, *s.shape[1:]), s.dtype) for s in chip_sums]
        s_sems, r_sems, sums, lands, token = _split_start(
            _scatter_copies, 3, chip_sums, lands, after, name=f"grad_scatter_start_{group}")
        in_flight.append((group, keys, s_sems, r_sems, sums, lands))
        return token

    collected = []

    def collect(after):
        group, keys, s_sems, r_sems, sums, lands = in_flight.pop(0)
        sums, from_chips = _split_wait(_scatter_copies, s_sems, r_sems, sums, lands, after,
                                       name=f"grad_scatter_wait_{group}")
        half_sums = [_add_chips(s, r, chip_idx, name=f"grad_add_chips_{k}") for k, s, r in zip(keys, sums, from_chips)]
        others = [lax.empty(h.shape, h.dtype) for h in half_sums]
        s_sems, r_sems, half_sums, others, token = _split_start(
            _swap_copies, 1, half_sums, others, after, name=f"grad_swap_start_{group}")
        collected.append((keys, s_sems, r_sems, half_sums, others))
        return token

    df = _mm(dx3_b, W_down, "nt", name="d_ffn_act", out_dtype=BF16)
    dW_down = _mm(f_act, dx3_b, "tn", name="dw_down", out_dtype=BF16)
    du0, dconv_w, dconv_b = _conv_bwd(u0, W_conv, conv_b, df, T=T, F=F)
    dh3 = _mm(du0, W_up, "nt", name="d_ffn_in", out_dtype=F32, b_blocked=True, tk=F2 // N_CHIPS)
    dW_up = _mm(h3, du0, "tn", name="dw_up", out_dtype=BF16, out_blocks=N_CHIPS)
    tok = exchange_start("ffn", ["w_down", "w_up"], [row_shards(dW_down), col_shards(dW_up)], dh3)
    dx2, dx2_b, dg_ffn = _rms_bwd(dh3, x2, r3 + tok[0:1, 0:1], g_ffn, dx3, name="norm_ffn_bwd")

    do_att = _mm(dx2_b, W_co, "nt", name="d_cross_o", out_dtype=BF16)
    dW_co = _mm(o_att, dx2_b, "tn", name="dw_co", out_dtype=BF16)
    tok = scatter_start("ffn", dW_co)
    dq, dkv = _attn_bwd(qc, kv, do_att, T=T, D=D, M=M)
    dkv_b = dkv.astype(BF16)
    dW_cq = _mm(h2, dq, "tn", name="dw_cq", out_dtype=BF16, after=tok)
    dh2 = _mm(dq, W_cq, "nt", name="d_cross_in", out_dtype=F32)
    dW_ckv = _mm(mem_n, dkv_b, "tn", name="dw_ckv", out_dtype=BF16, out_blocks=N_CHIPS)
    dmem_n = _mm(dkv_b, W_ckv, "nt", name="d_mem", out_dtype=F32, b_blocked=True)
    tok = exchange_start("cross", ["w_co", "w_cq", "w_ckv"],
                         [row_shards(dW_co), row_shards(dW_cq), col_shards(dW_ckv)], dmem_n)
    _, _, dg_mem = _rms_bwd(dmem_n, ms, rm, g_mem, None, name="norm_mem_bwd")
    dx1, dx1_b, dg_cross = _rms_bwd(dh2, x1, r2 + tok[0:1, 0:1], g_cross, dx2, name="norm_cross_bwd")

    dmerged = _mm(dx1_b, W_out, "nt", name="d_merged", out_dtype=BF16)
    dW_out = _mm(merged, dx1_b, "tn", name="dw_out", out_dtype=BF16)
    tok = scatter_start("cross", dW_out)
    dy_gla, dy_pool, dgates = _merge_bwd(dmerged, y_gla, y_pool, proj, T=T, D=D, col_block=2)
    dW_br_gla = _mm(o_gla, dy_gla, "tn", name="dw_branch_gla", out_dtype=BF16, after=tok)
    dW_br_pool = _mm(o_pool, dy_pool, "tn", name="dw_branch_pool", out_dtype=BF16)
    do_gla = _mm(dy_gla, W_branch, "nt", name="d_o_gla", out_dtype=F32, N=DV)
    do_pool = _mm(dy_pool, W_branch, "nt", name="d_o_pool", out_dtype=F32, N=PW, b_off=(DV, 0))
    dp, dw_pool, dpool_scale = _pool_bwd(proj, W_pool, pool_scale, do_pool, T=T, PW=PW, col_block=3)
    dW_pool = jnp.transpose(dw_pool.reshape(POOL_GROUPS, N_CHIPS, GW // N_CHIPS, GW), (1, 0, 2, 3))
    tok = exchange_start("mix", ["w_out", "w_branch", "w_pool"],
                         [row_shards(dW_out), row_shards(jnp.concatenate([dW_br_gla, dW_br_pool], axis=0)),
                          row_shards(dW_pool.reshape(N_CHIPS * POOL_GROUPS * (GW // N_CHIPS), GW).astype(BF16))],
                         dp)
    dqkvr, da_pad, dw2, db_a, dg_gla = _gla_bwd(proj, a_pad, W_a2p, b_a + tok[0:1, 0:1], g_gla, o_raw, states,
                                               do_gla, T=T, DK=DK, DV=DV)
    tok = scatter_start("mix", dqkvr)
    dproj = jnp.concatenate([dqkvr, dp, dgates], axis=1)
    dW_main = _mm(dproj, h1, "tn", name="dw_in_main", out_dtype=BF16, after=tok)
    dW_a = _mm(da_pad, h1, "tn", name="dw_in_rank", out_dtype=BF16)
    dW_in = jnp.concatenate([dW_main[:OFF_A], dW_a[:RANK], dW_main[OFF_A:]], axis=0)
    tok = exchange_start("in", ["w_in"], [row_shards(dW_in)], dW_a)
    dh1 = _mm(dproj, W_main, "nn", name="d_mix_in_main", out_dtype=F32, after=tok)
    dh1 = _mm(da_pad, W_a, "nn", name="d_mix_in_rank", out_dtype=F32, add=dh1)
    dx0, _, dg_mix = _rms_bwd(dh1, xs, r1, g_mix, dx1, name="norm_mix_bwd")

    grads = {}

    small_grads = [loss_part, dg_mix, db_a, dg_gla, dpool_scale, dg_cross, dg_mem, dg_ffn, dconv_b, dg_final,
                   dw2[:RANK], dconv_w]
    small_buf, offs = _pack(small_grads)
    small_sum = _all_reduce_small(small_buf)
    red = _unpack(small_sum, offs, [g.shape for g in small_grads])
    loss = red[0][0, 0]
    for k, g in zip(small_repl, red[1:10]):
        grads[k] = g.reshape(weights[k].shape)
    nb = DK // N_CHIPS
    grads["w_a2"] = lax.dynamic_slice_in_dim(red[10], chip * nb, nb, axis=1)[None]
    nb = F2 // N_CHIPS
    grads["conv_w"] = lax.dynamic_slice_in_dim(red[11], chip * nb, nb, axis=1)[None]

    delta, new_m, new_v = {}, {}, {}

    def shard_rows(k, a):
        a = a[0]
        return a.T if k == "w_in" else a.reshape(-1, a.shape[-1])

    def whole(k, a):
        a = a.reshape(-1, a.shape[2])
        return (a.T if k == "w_in" else a).reshape(weights[k].shape)

    scatter_start("in", small_sum)

    def finish(after):
        keys, s_sems, r_sems, mine, others = collected.pop(0)
        mine, others = _split_wait(_swap_copies, s_sems, r_sems, mine, others, after,
                                   name=f"grad_swap_wait_{keys[0]}")
        for k, g_mine, g_other in zip(keys, mine, others):
            wmv = [halves(shard_rows(k, src[k])) for src in (weights, mom_m, mom_v)]
            res = _adamw_halves(*wmv, g_mine, g_other, c_idx, name=f"adamw_{k}")
            grads[k], delta[k], new_m[k], new_v[k] = (whole(k, a) for a in res)
        return res[1]

    after = in_flight[-1][4][0]
    while in_flight:
        after = collect(after)
        while len(collected) > 1:
            after = finish(after)
    finish(after)
    small = small_repl + ["w_a2", "conv_w"]
    packs = [_pack([src[k] for k in small])[0] for src in (weights, grads, mom_m, mom_v)]
    _, offs = _pack([weights[k] for k in small])
    outs = _adamw(*packs, name="adamw_small")
    for res, o in zip((delta, new_m, new_v), outs):
        for k, a in zip(small, _unpack(o, offs, [weights[k].shape for k in small])):
            res[k] = a

    return (loss, dx0[None], *[grads[k] for k in order], *[delta[k] for k in order],
            *[new_m[k] for k in order], *[new_v[k] for k in order])
```

```python
import functools

import jax
import jax.numpy as jnp
from jax import lax
from jax.experimental import pallas as pl
from jax.experimental.pallas import tpu as pltpu

F32 = jnp.float32
BF16 = jnp.bfloat16
MESH = pl.DeviceIdType.MESH
HIGHEST = lax.Precision.HIGHEST

EPS = 1e-6
GLA_HEADS = 4
GLA_CHUNK = 128
GLA_GATE_NORM = 16.0
POOL_GROUPS = 4
CROSS_HEADS = 4
CONV_W = 3
N_CHIPS = 4
LANES = 128
SUBLANES = 8
VMEM_LIMIT = 56 << 20

ADAM_LR = 0.001
ADAM_B1 = 0.9
ADAM_B2 = 0.999
ADAM_EPS = 1e-08
ADAM_WD = 0.01
ADAM_STEP = 10

NN = (((1,), (0,)), ((), ()))
NT = (((1,), (1,)), ((), ()))
TN = (((0,), (0,)), ((), ()))


CHUNK_PRECISION = lax.Precision.HIGH


def _dot(a, b, dn=NN, precision=None):
    return lax.dot_general(a, b, dn, precision=precision, preferred_element_type=F32)


def _tile(n, pref, align=LANES):
    t = (min(pref, n) // align) * align
    while t >= align:
        if n % t == 0:
            return t
        t -= align
    return n


def _pcall(body, *, name, out_shape, grid=(), in_specs=None, out_specs=None, scratch_shapes=(),
           semantics=None, prefetch=0, aliases=None, split_copy=False):
    params = dict(vmem_limit_bytes=VMEM_LIMIT)
    if semantics is not None:
        params["dimension_semantics"] = semantics
    if split_copy:
        params["has_side_effects"] = pltpu.SideEffectType.DATAFLOW_SIDE_EFFECTING
    if prefetch:
        grid_spec = pltpu.PrefetchScalarGridSpec(
            num_scalar_prefetch=prefetch, grid=grid, in_specs=in_specs, out_specs=out_specs,
            scratch_shapes=scratch_shapes)
        return pl.pallas_call(body, name=name, out_shape=out_shape, grid_spec=grid_spec,
                              compiler_params=pltpu.CompilerParams(**params))
    kw = {}
    if aliases is not None:
        kw["input_output_aliases"] = aliases
    if in_specs is not None:
        kw["in_specs"] = in_specs
    if out_specs is not None:
        kw["out_specs"] = out_specs
    return pl.pallas_call(body, name=name, out_shape=out_shape, grid=grid,
                          scratch_shapes=scratch_shapes,
                          compiler_params=pltpu.CompilerParams(**params), **kw)


def _sigmoid(x):
    return 0.5 * jnp.tanh(0.5 * x) + 0.5


def _log_sigmoid(x):
    return jnp.minimum(x, 0.0) - jnp.log(1.0 + jnp.exp(-jnp.abs(x)))


def _mm(a, b, mode, *, name, out_dtype, M=None, N=None, K=None, a_off=(0, 0), b_off=(0, 0),
        add=None, b_blocked=False, out_blocks=0, after=None, tm=1536, tn=1536, tk=2048):
    if b_blocked:
        nb, R, Cb = b.shape
        b_rows, b_cols = R, nb * Cb
    else:
        b_rows, b_cols = b.shape
    if mode == "nn":
        M = M or a.shape[0]; K = K or a.shape[1]; N = N or b_cols
    elif mode == "nt":
        M = M or a.shape[0]; K = K or a.shape[1]; N = N or b_rows
    else:
        K = K or a.shape[0]; M = M or a.shape[1]; N = N or b_cols
    tm = _tile(M, tm, LANES if mode == "tn" else 16)
    tn = _tile(Cb if (b_blocked and mode != "nt") else (N // out_blocks if out_blocks else N), tn)
    tk = _tile(Cb if (b_blocked and mode == "nt") else K, tk)
    nk = K // tk
    dn = {"nn": NN, "nt": NT, "tn": TN}[mode]

    def off(o, t):
        assert o % t == 0, (name, o, t)
        return o // t

    if mode == "tn":
        ar, ac = off(a_off[0], tk), off(a_off[1], tm)
        a_spec = pl.BlockSpec((tk, tm), lambda i, j, k: (k + ar, i + ac))
    else:
        ar, ac = off(a_off[0], tm), off(a_off[1], tk)
        a_spec = pl.BlockSpec((tm, tk), lambda i, j, k: (i + ar, k + ac))
    if b_blocked and mode == "nt":
        per = Cb // tk
        b_spec = pl.BlockSpec((None, tn, tk), lambda i, j, k: (k // per, j, k % per))
    elif b_blocked:
        per = Cb // tn
        b_spec = pl.BlockSpec((None, tk, tn), lambda i, j, k: (j // per, k, j % per))
    elif mode == "nt":
        br, bc = off(b_off[0], tn), off(b_off[1], tk)
        b_spec = pl.BlockSpec((tn, tk), lambda i, j, k: (j + br, k + bc))
    else:
        br, bc = off(b_off[0], tk), off(b_off[1], tn)
        b_spec = pl.BlockSpec((tk, tn), lambda i, j, k: (k + br, j + bc))
    if out_blocks:
        per_o = N // out_blocks // tn
        o_spec = pl.BlockSpec((None, tm, tn), lambda i, j, k: (j // per_o, i, j % per_o))
        out_shape = jax.ShapeDtypeStruct((out_blocks, M, N // out_blocks), out_dtype)
    else:
        o_spec = pl.BlockSpec((tm, tn), lambda i, j, k: (i, j))
        out_shape = jax.ShapeDtypeStruct((M, N), out_dtype)
    in_specs = [a_spec, b_spec]
    args = [a, b]
    if add is not None:
        assert not out_blocks
        in_specs.append(o_spec)
        args.append(add)
    if after is not None:
        in_specs.append(pl.BlockSpec(memory_space=pl.ANY))
        args.append(after)
    n_in = len(args)

    def finish(r, refs):
        if add is not None:
            r = r + refs[2][...]
        o_ref = refs[n_in]
        o_ref[...] = r.astype(o_ref.dtype)

    def body_one(*refs):
        finish(_dot(refs[0][...].astype(BF16), refs[1][...].astype(BF16), dn), refs)

    def body_acc(*refs):
        acc_ref = refs[-1]
        k = pl.program_id(2)

        @pl.when(k == 0)
        def _():
            acc_ref[...] = jnp.zeros_like(acc_ref)

        acc_ref[...] += _dot(refs[0][...].astype(BF16), refs[1][...].astype(BF16), dn)

        @pl.when(k == nk - 1)
        def _():
            finish(acc_ref[...], refs)

    return _pcall(body_one if nk == 1 else body_acc, name=name, out_shape=out_shape,
                  grid=(M // tm, N // tn, nk), in_specs=in_specs, out_specs=o_spec,
                  scratch_shapes=[] if nk == 1 else [pltpu.VMEM((tm, tn), F32)],
                  semantics=("parallel", "parallel", "arbitrary"))(*args)


def _rms_fwd(x, g, *, name):
    T, D = x.shape
    tr = _tile(T, 128, 16)

    def body(x_ref, g_ref, h_ref, r_ref):
        xv = x_ref[...]
        r = lax.rsqrt(jnp.mean(xv * xv, axis=-1, keepdims=True) + EPS)
        h_ref[...] = (xv * r * g_ref[...]).astype(h_ref.dtype)
        r_ref[...] = r

    row = pl.BlockSpec((tr, D), lambda i: (i, 0))
    return _pcall(body, name=name,
                  out_shape=(jax.ShapeDtypeStruct((T, D), BF16), jax.ShapeDtypeStruct((T, 1), F32)),
                  grid=(T // tr,),
                  in_specs=[row, pl.BlockSpec((1, D), lambda i: (0, 0))],
                  out_specs=(row, pl.BlockSpec((tr, 1), lambda i: (i, 0))),
                  semantics=("parallel",))(x, g)


def _rms_bwd(dh, x, rstd, g, dres, *, name):
    T, D = x.shape
    tr = _tile(T, 128, 16)
    has_res = dres is not None

    def body(*refs):
        if has_res:
            dh_ref, x_ref, r_ref, g_ref, res_ref, dx_ref, dxb_ref, dg_ref = refs
        else:
            dh_ref, x_ref, r_ref, g_ref, dx_ref, dxb_ref, dg_ref = refs
        r = r_ref[...]
        xh = x_ref[...] * r
        dhv = dh_ref[...].astype(F32)
        dxh = dhv * g_ref[...]
        m = jnp.mean(dxh * xh, axis=-1, keepdims=True)
        dx = r * (dxh - xh * m)
        if has_res:
            dx = dx + res_ref[...]
        dx_ref[...] = dx
        dxb_ref[...] = dx.astype(BF16)

        @pl.when(pl.program_id(0) == 0)
        def _():
            dg_ref[...] = jnp.zeros_like(dg_ref)

        dg_ref[...] += jnp.sum(dhv * xh, axis=0, keepdims=True)

    row = pl.BlockSpec((tr, D), lambda i: (i, 0))
    vec = pl.BlockSpec((1, D), lambda i: (0, 0))
    in_specs = [row, row, pl.BlockSpec((tr, 1), lambda i: (i, 0)), vec]
    args = [dh, x, rstd, g]
    if has_res:
        in_specs.append(row)
        args.append(dres)
    return _pcall(body, name=name,
                  out_shape=(jax.ShapeDtypeStruct((T, D), F32), jax.ShapeDtypeStruct((T, D), BF16),
                             jax.ShapeDtypeStruct((1, D), F32)),
                  grid=(T // tr,), in_specs=in_specs, out_specs=(row, row, vec),
                  semantics=("arbitrary",))(*args)


def _loss_head(x3, g, tgt):
    T, D = x3.shape
    tr = _tile(T, 128, 16)

    def body(x_ref, g_ref, t_ref, loss_ref, dx_ref, dxb_ref, dg_ref):
        xv = x_ref[...]
        gv = g_ref[...]
        r = lax.rsqrt(jnp.mean(xv * xv, axis=-1, keepdims=True) + EPS)
        xh = xv * r
        err = xh * gv - t_ref[...]
        dy = err * (1.0 / D)
        dxh = dy * gv
        m = jnp.mean(dxh * xh, axis=-1, keepdims=True)
        dx = r * (dxh - xh * m)
        dx_ref[...] = dx
        dxb_ref[...] = dx.astype(BF16)

        @pl.when(pl.program_id(0) == 0)
        def _():
            dg_ref[...] = jnp.zeros_like(dg_ref)
            loss_ref[...] = jnp.zeros_like(loss_ref)

        dg_ref[...] += jnp.sum(dy * xh, axis=0, keepdims=True)
        part = 0.5 * jnp.sum(jnp.mean(err * err, axis=-1, keepdims=True), axis=0, keepdims=True)
        loss_ref[...] += jnp.broadcast_to(part, loss_ref.shape)

    row = pl.BlockSpec((tr, D), lambda i: (i, 0))
    vec = pl.BlockSpec((1, D), lambda i: (0, 0))
    return _pcall(body, name="loss_head",
                  out_shape=(jax.ShapeDtypeStruct((1, LANES), F32), jax.ShapeDtypeStruct((T, D), F32),
                             jax.ShapeDtypeStruct((T, D), BF16), jax.ShapeDtypeStruct((1, D), F32)),
                  grid=(T // tr,), in_specs=[row, vec, row],
                  out_specs=(pl.BlockSpec((1, LANES), lambda i: (0, 0)), row, row, vec),
                  semantics=("arbitrary",))(x3, g, tgt)


def _gla_chunk_terms(qk, a_ref, w2_ref, ba_ref, DK):
    C = qk.shape[0]
    gp = _dot(a_ref[...].astype(BF16), w2_ref[...]) + ba_ref[...]
    la = _log_sigmoid(gp) * (1.0 / GLA_GATE_NORM)
    row = lax.broadcasted_iota(jnp.int32, (C, C), 0)
    col = lax.broadcasted_iota(jnp.int32, (C, C), 1)
    causal = row >= col
    b = _dot(causal.astype(F32), la, precision=HIGHEST)
    return gp, b, causal


def _gla_fwd(proj, a_pad, w2, b_a, g_gla, *, T, DK, DV):
    assert 2 * DK == DV
    H = GLA_HEADS
    HK, HV = DK // H, DV // H
    C = GLA_CHUNK
    n = T // C
    RP = a_pad.shape[1]
    scale = HK ** -0.5

    def body(qk_ref, v_ref, r_ref, a_ref, w2_ref, ba_ref, gg_ref, og_ref, oraw_ref, st_ref, s_ref):
        @pl.when(pl.program_id(0) == 0)
        def _():
            s_ref[...] = jnp.zeros_like(s_ref)

        st_ref[...] = s_ref[...]
        qk = qk_ref[...]
        _, b, causal = _gla_chunk_terms(qk, a_ref, w2_ref, ba_ref, DK)
        for h in range(H):
            ks = slice(h * HK, (h + 1) * HK)
            vs = slice(h * HV, (h + 1) * HV)
            bh = b[:, ks]
            b_last = bh[C - 1:C, :]
            qt = qk[:, ks] * scale * jnp.exp(bh)
            kh = qk[:, DK + h * HK:DK + (h + 1) * HK]
            kt = kh * jnp.exp(-bh)
            khat = kh * jnp.exp(b_last - bh)
            a_mat = jnp.where(causal, _dot(qt, kt, NT, CHUNK_PRECISION), 0.0)
            vh = v_ref[:, vs]
            s_t = s_ref[h]
            o = _dot(a_mat, vh, NN, CHUNK_PRECISION) + _dot(qt, s_t, NT, CHUNK_PRECISION)
            s_ref[h] = s_t * jnp.exp(b_last) + _dot(vh, khat, TN, CHUNK_PRECISION)
            rs = lax.rsqrt(jnp.mean(o * o, axis=-1, keepdims=True) + EPS)
            rr = r_ref[:, vs]
            og = o * rs * gg_ref[:, vs] * (rr * _sigmoid(rr))
            oraw_ref[:, vs] = o
            og_ref[:, vs] = og.astype(BF16)

    blk = lambda j: pl.BlockSpec((C, DV), lambda i: (i, j))
    full = lambda s: pl.BlockSpec(s, lambda i: (0,) * len(s))
    return _pcall(
        body, name="gla_fwd",
        out_shape=(jax.ShapeDtypeStruct((T, DV), BF16), jax.ShapeDtypeStruct((T, DV), F32),
                   jax.ShapeDtypeStruct((n, H, HV, HK), F32)),
        grid=(n,),
        in_specs=[blk(0), blk(1), blk(2), pl.BlockSpec((C, RP), lambda i: (i, 0)),
                  full((RP, DK)), full((1, DK)), full((1, DV))],
        out_specs=(blk(0), blk(0), pl.BlockSpec((None, H, HV, HK), lambda i: (i, 0, 0, 0))),
        scratch_shapes=[pltpu.VMEM((H, HV, HK), F32)],
        semantics=("arbitrary",))(proj, proj, proj, a_pad, w2, b_a, g_gla)


def _gla_bwd(proj, a_pad, w2, b_a, g_gla, o_raw, states, do_gla, *, T, DK, DV):
    H = GLA_HEADS
    HK, HV = DK // H, DV // H
    C = GLA_CHUNK
    n = T // C
    RP = a_pad.shape[1]
    scale = HK ** -0.5

    def body(qk_ref, v_ref, r_ref, a_ref, w2_ref, ba_ref, gg_ref, oraw_ref, st_ref, dog_ref,
             dqkvr_ref, da_ref, dw2_ref, dba_ref, dgg_ref, ds_ref):
        @pl.when(pl.program_id(0) == 0)
        def _():
            ds_ref[...] = jnp.zeros_like(ds_ref)
            dw2_ref[...] = jnp.zeros_like(dw2_ref)
            dba_ref[...] = jnp.zeros_like(dba_ref)
            dgg_ref[...] = jnp.zeros_like(dgg_ref)

        qk = qk_ref[...]
        gp, b, causal = _gla_chunk_terms(qk, a_ref, w2_ref, ba_ref, DK)
        row = lax.broadcasted_iota(jnp.int32, (C, C), 0)
        col = lax.broadcasted_iota(jnp.int32, (C, C), 1)
        upper = (col >= row).astype(F32)
        dla_parts = []
        for h in range(H):
            ks = slice(h * HK, (h + 1) * HK)
            vs = slice(h * HV, (h + 1) * HV)
            bh = b[:, ks]
            b_last = bh[C - 1:C, :]
            eb = jnp.exp(bh)
            emb = jnp.exp(-bh)
            ehat = jnp.exp(b_last - bh)
            e_last = jnp.exp(b_last)
            qt = qk[:, ks] * scale * eb
            kh = qk[:, DK + h * HK:DK + (h + 1) * HK]
            kt = kh * emb
            khat = kh * ehat
            a_mat = jnp.where(causal, _dot(qt, kt, NT, CHUNK_PRECISION), 0.0)
            vh = v_ref[:, vs]
            o = oraw_ref[:, vs]
            rs = lax.rsqrt(jnp.mean(o * o, axis=-1, keepdims=True) + EPS)
            on = o * rs
            gg = gg_ref[:, vs]
            rr = r_ref[:, vs]
            sg = _sigmoid(rr)
            d_out = dog_ref[:, vs]
            dr = d_out * (on * gg) * (sg * (1.0 + rr * (1.0 - sg)))
            d_og = d_out * (rr * sg)
            dgg_ref[:, vs] += jnp.sum(d_og * on, axis=0, keepdims=True)
            d_on = d_og * gg
            d_o = rs * (d_on - on * jnp.mean(d_on * on, axis=-1, keepdims=True))
            s_t = st_ref[h]
            ds_t = ds_ref[h]
            d_a = jnp.where(causal, _dot(d_o, vh, NT, CHUNK_PRECISION), 0.0)
            dv = _dot(a_mat, d_o, TN, CHUNK_PRECISION) + _dot(khat, ds_t, NT, CHUNK_PRECISION)
            dqt = _dot(d_a, kt, NN, CHUNK_PRECISION) + _dot(d_o, s_t, NN, CHUNK_PRECISION)
            dkt = _dot(d_a, qt, TN, CHUNK_PRECISION)
            dkhat = _dot(vh, ds_t, NN, CHUNK_PRECISION)
            ds_ref[h] = ds_t * e_last + _dot(d_o, qt, TN, CHUNK_PRECISION)
            dq = dqt * eb * scale
            dk = dkt * emb + dkhat * ehat
            db = dqt * qt - dkt * kt - dkhat * khat
            d_last = (jnp.sum(dkhat * khat, axis=0, keepdims=True)
                      + e_last * jnp.sum(ds_t * s_t, axis=0, keepdims=True))
            dla_parts.append(_dot(upper, db, NN, HIGHEST) + d_last)
            dqkvr_ref[:, ks] = dq.astype(BF16)
            dqkvr_ref[:, DK + h * HK:DK + (h + 1) * HK] = dk.astype(BF16)
            dqkvr_ref[:, DV + h * HV:DV + (h + 1) * HV] = dv.astype(BF16)
            dqkvr_ref[:, 2 * DV + h * HV:2 * DV + (h + 1) * HV] = dr.astype(BF16)
        dla = jnp.concatenate(dla_parts, axis=1)
        dgp = dla * (1.0 / GLA_GATE_NORM) * _sigmoid(-gp)
        dba_ref[...] += jnp.sum(dgp, axis=0, keepdims=True)
        dgp_b = dgp.astype(BF16)
        dw2_ref[...] += _dot(a_ref[...].astype(BF16), dgp_b, TN)
        da_ref[...] = _dot(dgp_b, w2_ref[...], NT).astype(BF16)

    rev = lambda j: pl.BlockSpec((C, DV), lambda i: (n - 1 - i, j))
    full = lambda s: pl.BlockSpec(s, lambda i: (0,) * len(s))
    return _pcall(
        body, name="gla_bwd",
        out_shape=(jax.ShapeDtypeStruct((T, 3 * DV), BF16), jax.ShapeDtypeStruct((T, RP), BF16),
                   jax.ShapeDtypeStruct((RP, DK), F32), jax.ShapeDtypeStruct((1, DK), F32),
                   jax.ShapeDtypeStruct((1, DV), F32)),
        grid=(n,),
        in_specs=[rev(0), rev(1), rev(2), pl.BlockSpec((C, RP), lambda i: (n - 1 - i, 0)),
                  full((RP, DK)), full((1, DK)), full((1, DV)), rev(0),
                  pl.BlockSpec((None, H, HV, HK), lambda i: (n - 1 - i, 0, 0, 0)), rev(0)],
        out_specs=(pl.BlockSpec((C, 3 * DV), lambda i: (n - 1 - i, 0)),
                   pl.BlockSpec((C, RP), lambda i: (n - 1 - i, 0)),
                   full((RP, DK)), full((1, DK)), full((1, DV))),
        scratch_shapes=[pltpu.VMEM((H, HV, HK), F32)],
        semantics=("arbitrary",))(proj, proj, proj, a_pad, w2, b_a, g_gla, o_raw, states, do_gla)


def _pool_windows(p, g, T):
    t = lax.broadcasted_iota(jnp.int32, (T, 1), 0)
    s = p
    for lvl in range(POOL_GROUPS):
        sh = 1 << lvl
        nxt = s + jnp.where(t >= sh, pltpu.roll(s, sh, 0), 0.0)
        s = jnp.where(lvl <= g, nxt, s)
    win = jnp.left_shift(2, g)
    inv = 1.0 / jnp.minimum(t + 1, win).astype(F32)
    return s * inv - p, inv


def _pool_fwd(proj, w_pool, scale, *, T, PW, col_block):
    GW = PW // POOL_GROUPS
    per = PW // GW

    def body(p_ref, w_ref, s_ref, o_ref):
        g = pl.program_id(0)
        pooled, _ = _pool_windows(p_ref[...], g, T)
        mixed = _dot(pooled.astype(BF16), w_ref[...])
        o_ref[...] = (mixed * s_ref[...]).astype(BF16)

    return _pcall(body, name="pool_fwd", out_shape=jax.ShapeDtypeStruct((T, PW), BF16),
                  grid=(POOL_GROUPS,),
                  in_specs=[pl.BlockSpec((T, GW), lambda g: (0, col_block * per + g)),
                            pl.BlockSpec((None, GW, GW), lambda g: (g, 0, 0)),
                            pl.BlockSpec((1, GW), lambda g: (0, g))],
                  out_specs=pl.BlockSpec((T, GW), lambda g: (0, g)),
                  semantics=("parallel",))(proj, w_pool, scale)


def _pool_bwd(proj, w_pool, scale, do_pool, *, T, PW, col_block):
    GW = PW // POOL_GROUPS
    per = PW // GW

    def body(p_ref, w_ref, s_ref, do_ref, dp_ref, dw_ref, dsc_ref):
        g = pl.program_id(0)
        pooled, inv = _pool_windows(p_ref[...], g, T)
        pooled_b = pooled.astype(BF16)
        w = w_ref[...]
        mixed = _dot(pooled_b, w)
        d_out = do_ref[...]
        dsc_ref[...] = jnp.sum(d_out * mixed, axis=0, keepdims=True)
        dmixed = (d_out * s_ref[...]).astype(BF16)
        dw_ref[...] = _dot(pooled_b, dmixed, TN)
        dpooled = _dot(dmixed, w, NT)
        t = lax.broadcasted_iota(jnp.int32, (T, 1), 0)
        s = dpooled * inv
        for lvl in range(POOL_GROUPS):
            sh = 1 << lvl
            nxt = s + jnp.where(t < T - sh, pltpu.roll(s, T - sh, 0), 0.0)
            s = jnp.where(lvl <= g, nxt, s)
        dp_ref[...] = (s - dpooled).astype(BF16)

    return _pcall(body, name="pool_bwd",
                  out_shape=(jax.ShapeDtypeStruct((T, PW), BF16),
                             jax.ShapeDtypeStruct((POOL_GROUPS, GW, GW), F32),
                             jax.ShapeDtypeStruct((1, PW), F32)),
                  grid=(POOL_GROUPS,),
                  in_specs=[pl.BlockSpec((T, GW), lambda g: (0, col_block * per + g)),
                            pl.BlockSpec((None, GW, GW), lambda g: (g, 0, 0)),
                            pl.BlockSpec((1, GW), lambda g: (0, g)),
                            pl.BlockSpec((T, GW), lambda g: (0, g))],
                  out_specs=(pl.BlockSpec((T, GW), lambda g: (0, g)),
                             pl.BlockSpec((None, GW, GW), lambda g: (g, 0, 0)),
                             pl.BlockSpec((1, GW), lambda g: (0, g))),
                  semantics=("parallel",))(proj, w_pool, scale, do_pool)


def _merge_fwd(y_gla, y_pool, proj, *, T, D, col_block):
    tr = _tile(T, 128, 16)

    def body(yg_ref, yp_ref, g1_ref, g2_ref, o_ref):
        o_ref[...] = (_sigmoid(g1_ref[...]) * yg_ref[...]
                      + _sigmoid(g2_ref[...]) * yp_ref[...]).astype(BF16)

    row = pl.BlockSpec((tr, D), lambda i: (i, 0))
    return _pcall(body, name="merge_fwd", out_shape=jax.ShapeDtypeStruct((T, D), BF16),
                  grid=(T // tr,),
                  in_specs=[row, row, pl.BlockSpec((tr, D), lambda i: (i, col_block)),
                            pl.BlockSpec((tr, D), lambda i: (i, col_block + 1))],
                  out_specs=row, semantics=("parallel",))(y_gla, y_pool, proj, proj)


def _merge_bwd(dmerged, y_gla, y_pool, proj, *, T, D, col_block):
    tr = _tile(T, 128, 16)

    def body(dm_ref, yg_ref, yp_ref, g1_ref, g2_ref, dyg_ref, dyp_ref, dg_ref):
        dm = dm_ref[...]
        s1 = _sigmoid(g1_ref[...])
        s2 = _sigmoid(g2_ref[...])
        dyg_ref[...] = (dm * s1).astype(BF16)
        dyp_ref[...] = (dm * s2).astype(BF16)
        dg_ref[:, :D] = (dm * yg_ref[...] * s1 * (1.0 - s1)).astype(BF16)
        dg_ref[:, D:] = (dm * yp_ref[...] * s2 * (1.0 - s2)).astype(BF16)

    row = pl.BlockSpec((tr, D), lambda i: (i, 0))
    return _pcall(body, name="merge_bwd",
                  out_shape=(jax.ShapeDtypeStruct((T, D), BF16), jax.ShapeDtypeStruct((T, D), BF16),
                             jax.ShapeDtypeStruct((T, 2 * D), BF16)),
                  grid=(T // tr,),
                  in_specs=[row, row, row, pl.BlockSpec((tr, D), lambda i: (i, col_block)),
                            pl.BlockSpec((tr, D), lambda i: (i, col_block + 1))],
                  out_specs=(row, row, pl.BlockSpec((tr, 2 * D), lambda i: (i, 0))),
                  semantics=("parallel",))(dmerged, y_gla, y_pool, proj, proj)


def _attn_fwd(q, kv, *, T, D, M):
    H = CROSS_HEADS
    HD = D // H
    tq = _tile(T, 512, 16)
    scale = HD ** -0.5

    def body(q_ref, kv_ref, o_ref):
        for h in range(H):
            hs = slice(h * HD, (h + 1) * HD)
            s = _dot(q_ref[:, hs], kv_ref[:, hs], NT) * scale
            e = jnp.exp(s - jnp.max(s, axis=-1, keepdims=True))
            p = e / jnp.sum(e, axis=-1, keepdims=True)
            o_ref[:, hs] = _dot(p.astype(BF16), kv_ref[:, D + h * HD:D + (h + 1) * HD]).astype(BF16)

    row = pl.BlockSpec((tq, D), lambda i: (i, 0))
    return _pcall(body, name="attn_fwd", out_shape=jax.ShapeDtypeStruct((T, D), BF16),
                  grid=(T // tq,), in_specs=[row, pl.BlockSpec((M, 2 * D), lambda i: (0, 0))],
                  out_specs=row, semantics=("parallel",))(q, kv)


def _attn_bwd(q, kv, do, *, T, D, M):
    H = CROSS_HEADS
    HD = D // H
    tq = _tile(T, 512, 16)
    scale = HD ** -0.5

    def body(q_ref, kv_ref, do_ref, dq_ref, dkv_ref):
        @pl.when(pl.program_id(0) == 0)
        def _():
            dkv_ref[...] = jnp.zeros_like(dkv_ref)

        for h in range(H):
            hs = slice(h * HD, (h + 1) * HD)
            vs = slice(D + h * HD, D + (h + 1) * HD)
            qh = q_ref[:, hs]
            kh = kv_ref[:, hs]
            s = _dot(qh, kh, NT) * scale
            e = jnp.exp(s - jnp.max(s, axis=-1, keepdims=True))
            p = e / jnp.sum(e, axis=-1, keepdims=True)
            p_b = p.astype(BF16)
            d_o = do_ref[:, hs]
            dkv_ref[:, vs] += _dot(p_b, d_o, TN)
            dp = _dot(d_o, kv_ref[:, vs], NT)
            ds = (p * (dp - jnp.sum(dp * p, axis=-1, keepdims=True)) * scale).astype(BF16)
            dq_ref[:, hs] = _dot(ds, kh).astype(BF16)
            dkv_ref[:, hs] += _dot(ds, qh, TN)

    row = pl.BlockSpec((tq, D), lambda i: (i, 0))
    full = pl.BlockSpec((M, 2 * D), lambda i: (0, 0))
    return _pcall(body, name="attn_bwd",
                  out_shape=(jax.ShapeDtypeStruct((T, D), BF16), jax.ShapeDtypeStruct((M, 2 * D), F32)),
                  grid=(T // tq,), in_specs=[row, full, row], out_specs=(row, full),
                  semantics=("arbitrary",))(q, kv, do)


def _shift_down(x, halo, s):
    out = pltpu.roll(x, s, 0)
    t8 = lax.broadcasted_iota(jnp.int32, (SUBLANES, 1), 0)
    head = out[:SUBLANES]
    for j in range(s):
        head = jnp.where(t8 == j, halo[SUBLANES - s + j:SUBLANES - s + j + 1, :], head)
    return head if x.shape[0] == SUBLANES else jnp.concatenate([head, out[SUBLANES:]], axis=0)


def _shift_up(x, halo, s):
    rows = x.shape[0]
    out = pltpu.roll(x, rows - s, 0)
    t8 = lax.broadcasted_iota(jnp.int32, (SUBLANES, 1), 0)
    tail = out[rows - SUBLANES:]
    for j in range(s):
        tail = jnp.where(t8 == SUBLANES - s + j, halo[j:j + 1, :], tail)
    return jnp.concatenate([out[:rows - SUBLANES], tail], axis=0)


def _conv_tiles(T):
    tt = _tile(T, 128, SUBLANES)
    return tt, tt // SUBLANES, T // SUBLANES


def _conv_fwd(u0, conv_w, conv_b, *, T, F):
    tt, hb, _ = _conv_tiles(T)
    cw = _tile(F, LANES)

    def body(u_ref, prev_ref, w_ref, b_ref, f_ref):
        i = pl.program_id(0)

        def conv(cs):
            x = u_ref[:, cs]
            halo = jnp.where(i > 0, prev_ref[:, cs], 0.0)
            return (w_ref[2:3, cs] * x + w_ref[1:2, cs] * _shift_down(x, halo, 1)
                    + w_ref[0:1, cs] * _shift_down(x, halo, 2) + b_ref[:, cs])

        for j in range(F // cw):
            gate = conv(slice(j * cw, (j + 1) * cw))
            val = conv(slice(F + j * cw, F + (j + 1) * cw))
            f_ref[:, j * cw:(j + 1) * cw] = (gate * _sigmoid(gate) * val).astype(BF16)

    return _pcall(body, name="conv_fwd", out_shape=jax.ShapeDtypeStruct((T, F), BF16),
                  grid=(T // tt,),
                  in_specs=[pl.BlockSpec((tt, 2 * F), lambda i: (i, 0)),
                            pl.BlockSpec((SUBLANES, 2 * F), lambda i: (jnp.maximum(i * hb - 1, 0), 0)),
                            pl.BlockSpec((CONV_W, 2 * F), lambda i: (0, 0)),
                            pl.BlockSpec((1, 2 * F), lambda i: (0, 0))],
                  out_specs=pl.BlockSpec((tt, F), lambda i: (i, 0)),
                  semantics=("parallel",))(u0, u0, conv_w, conv_b)


def _conv_bwd(u0, conv_w, conv_b, df, *, T, F):
    tt, hb, nb = _conv_tiles(T)
    nt = T // tt
    cw = _tile(F, LANES)

    def body(u_ref, prev_ref, next_ref, df_ref, dfn_ref, w_ref, b_ref, du0_ref, dw_ref, db_ref):
        i = pl.program_id(0)

        @pl.when(i == 0)
        def _():
            dw_ref[...] = jnp.zeros_like(dw_ref)
            db_ref[...] = jnp.zeros_like(db_ref)

        def conv(cs):
            x = u_ref[:, cs]
            halo = jnp.where(i > 0, prev_ref[:, cs], 0.0)
            x1 = _shift_down(x, halo, 1)
            x2 = _shift_down(x, halo, 2)
            u = w_ref[2:3, cs] * x + w_ref[1:2, cs] * x1 + w_ref[0:1, cs] * x2 + b_ref[:, cs]
            xn = next_ref[:, cs]
            tail = x[tt - SUBLANES:, :]
            un = (w_ref[2:3, cs] * xn + w_ref[1:2, cs] * _shift_down(xn, tail, 1)
                  + w_ref[0:1, cs] * _shift_down(xn, tail, 2) + b_ref[:, cs])
            return u, un, (x, x1, x2)

        def glu_grad(gate, val, dff):
            sg = _sigmoid(gate)
            return dff * val * (sg * (1.0 + gate * (1.0 - sg))), dff * (gate * sg)

        def finish(cs, du, dun, xs):
            du0 = (w_ref[2:3, cs] * du + w_ref[1:2, cs] * _shift_up(du, dun, 1)
                   + w_ref[0:1, cs] * _shift_up(du, dun, 2))
            du0_ref[:, cs] = du0.astype(BF16)
            db_ref[:, cs] += jnp.sum(du, axis=0, keepdims=True)
            dw_ref[2:3, cs] += jnp.sum(du * xs[0], axis=0, keepdims=True)
            dw_ref[1:2, cs] += jnp.sum(du * xs[1], axis=0, keepdims=True)
            dw_ref[0:1, cs] += jnp.sum(du * xs[2], axis=0, keepdims=True)

        for j in range(F // cw):
            fs = slice(j * cw, (j + 1) * cw)
            gs, vs = fs, slice(F + j * cw, F + (j + 1) * cw)
            ug, ung, xg = conv(gs)
            uv, unv, xv = conv(vs)
            dug, duv = glu_grad(ug, uv, df_ref[:, fs].astype(F32))
            dung, dunv = glu_grad(ung, unv, dfn_ref[0:SUBLANES, fs].astype(F32))
            dung = jnp.where(i < nt - 1, dung, 0.0)
            dunv = jnp.where(i < nt - 1, dunv, 0.0)
            finish(gs, dug, dung, xg)
            finish(vs, duv, dunv, xv)

    wide = lambda rows, fn: pl.BlockSpec((rows, 2 * F), fn)
    nxt = lambda i: (jnp.minimum((i + 1) * hb, nb - 1), 0)
    return _pcall(body, name="conv_bwd",
                  out_shape=(jax.ShapeDtypeStruct((T, 2 * F), BF16),
                             jax.ShapeDtypeStruct((CONV_W, 2 * F), F32),
                             jax.ShapeDtypeStruct((1, 2 * F), F32)),
                  grid=(nt,),
                  in_specs=[wide(tt, lambda i: (i, 0)),
                            wide(SUBLANES, lambda i: (jnp.maximum(i * hb - 1, 0), 0)),
                            wide(SUBLANES, nxt),
                            pl.BlockSpec((tt, F), lambda i: (i, 0)),
                            pl.BlockSpec((2 * SUBLANES, F),
                                         lambda i: (jnp.minimum((i + 1) * (hb // 2), nb // 2 - 1), 0)),
                            wide(CONV_W, lambda i: (0, 0)), wide(1, lambda i: (0, 0))],
                  out_specs=(wide(tt, lambda i: (i, 0)), wide(CONV_W, lambda i: (0, 0)),
                             wide(1, lambda i: (0, 0))),
                  semantics=("arbitrary",))(u0, u0, u0, df, df, conv_w, conv_b)


def _adamw(w, g, m, v, *, name):
    R, C = w.shape
    tr = _tile(R, max(SUBLANES, (1 << 19) // max(C, 1) // SUBLANES * SUBLANES), SUBLANES)
    c1 = 1.0 / (1.0 - ADAM_B1 ** ADAM_STEP)
    c2 = 1.0 / (1.0 - ADAM_B2 ** ADAM_STEP)

    def body(w_ref, g_ref, m_ref, v_ref, d_ref, mo_ref, vo_ref):
        gv = g_ref[...]
        mn = ADAM_B1 * m_ref[...] + (1.0 - ADAM_B1) * gv
        vn = ADAM_B2 * v_ref[...] + (1.0 - ADAM_B2) * (gv * gv)
        d_ref[...] = -ADAM_LR * ((mn * c1) / (jnp.sqrt(vn * c2) + ADAM_EPS) + ADAM_WD * w_ref[...])
        mo_ref[...] = mn
        vo_ref[...] = vn

    blk = pl.BlockSpec((tr, C), lambda i: (i, 0))
    shp = jax.ShapeDtypeStruct((R, C), F32)
    return _pcall(body, name=name, out_shape=(shp, shp, shp), grid=(R // tr,),
                  in_specs=[blk] * 4, out_specs=(blk,) * 3, semantics=("parallel",))(w, g, m, v)


def _blk(h, C, elems=1 << 19, align=16):
    th = _tile(h, max(align, elems // C // align * align), align)
    if th < h or h * C <= 2 * elems:
        return th, C
    return h, _tile(C, max(LANES, elems // h // LANES * LANES))


def _adamw_halves(w, m, v, g_mine, g_other, c_idx, *, name):
    _, h, C = w.shape
    th, tc = _blk(h, C, align=SUBLANES)
    c1 = 1.0 / (1.0 - ADAM_B1 ** ADAM_STEP)
    c2 = 1.0 / (1.0 - ADAM_B2 ** ADAM_STEP)

    def body(c_ref, w_ref, m_ref, v_ref, gm_ref, go_ref, g_ref, d_ref, mo_ref, vo_ref):
        gv = jnp.where(pl.program_id(0) == c_ref[0], gm_ref[...], go_ref[...])
        mn = ADAM_B1 * m_ref[...] + (1.0 - ADAM_B1) * gv
        vn = ADAM_B2 * v_ref[...] + (1.0 - ADAM_B2) * (gv * gv)
        d_ref[...] = -ADAM_LR * ((mn * c1) / (jnp.sqrt(vn * c2) + ADAM_EPS) + ADAM_WD * w_ref[...])
        g_ref[...] = gv
        mo_ref[...] = mn
        vo_ref[...] = vn

    blk = pl.BlockSpec((None, th, tc), lambda s, i, j, c: (s, i, j))

    def pick(mine):
        def index(s, i, j, c):
            use = (s == c[0]) if mine else (s != c[0])
            return jnp.where(use, i, 0), jnp.where(use, j, 0)
        return pl.BlockSpec((th, tc), index)

    shp = jax.ShapeDtypeStruct((2, h, C), F32)
    return _pcall(body, name=name, out_shape=(shp,) * 4, grid=(2, h // th, C // tc), prefetch=1,
                  in_specs=[blk, blk, blk, pick(True), pick(False)], out_specs=(blk,) * 4,
                  semantics=("parallel", "parallel", "parallel"))(c_idx, w, m, v, g_mine, g_other)


def _mesh_pos():
    x, y, c = lax.axis_index("x"), lax.axis_index("y"), lax.axis_index("c")
    others = [(1 - x, y), (x, 1 - y), (1 - x, 1 - y)]
    return x, y, c, others


def _gather_copies(shards, lands, send_sems, recv_sems):
    x, y, c, others = _mesh_pos()
    me = 2 * x + y
    return [pltpu.make_async_remote_copy(
        src_ref=shards[a].at[c], dst_ref=lands[a].at[me, c],
        send_sem=send_sems.at[3 * a + j], recv_sem=recv_sems.at[3 * a + j],
        device_id=(*chip, c), device_id_type=MESH)
        for a in range(len(shards)) for j, chip in enumerate(others)]


def _near_copies(shards, lands, send_sems, recv_sems):
    x, y, c, others = _mesh_pos()
    me = 2 * x + y
    return [pltpu.make_async_remote_copy(
        src_ref=shards[a].at[c], dst_ref=lands[a].at[me, c],
        send_sem=send_sems.at[2 * a + j], recv_sem=recv_sems.at[2 * a + j],
        device_id=(*chip, c), device_id_type=MESH)
        for a in range(len(shards)) for j, chip in enumerate(others[:2])]


def _relay_copies(shards, zones, send_sems, recv_sems):
    x, y, c, others = _mesh_pos()
    (nx, ny), copies = others[:2], []
    for a in range(len(zones)):
        hc = zones[a].shape[-1] // 2
        for k, (src_chip, to, lo) in enumerate(((ny, nx, 0), (nx, ny, hc))):
            part = zones[a].at[2 * src_chip[0] + src_chip[1], c, :, pl.ds(lo, hc)]
            copies.append(pltpu.make_async_remote_copy(
                src_ref=part, dst_ref=part, send_sem=send_sems.at[2 * a + k], recv_sem=recv_sems.at[2 * a + k],
                device_id=(*to, c), device_id_type=MESH))
    return copies


def _pass_copies(shards, zones, send_sems, recv_sems, pieces=(0, 1, 2, 3)):
    x, y, c, others = _mesh_pos()
    me = 2 * x + y
    copies = []
    for a in range(len(shards)):
        srcs = [zones[a].at[2 * chip[0] + chip[1], c] for chip in others] + [shards[a]]
        dsts = [zones[a].at[2 * chip[0] + chip[1], c] for chip in others] + [zones[a].at[me]]
        copies += [pltpu.make_async_remote_copy(
            src_ref=srcs[p], dst_ref=dsts[p], send_sem=send_sems.at[len(pieces) * a + k],
            recv_sem=recv_sems.at[len(pieces) * a + k], device_id=(x, y, 1 - c), device_id_type=MESH)
            for k, p in enumerate(pieces)]
    return copies


def _exchange_copies(grads, recvs, send_sems, recv_sems):
    x, y, c, _ = _mesh_pos()
    return [pltpu.make_async_remote_copy(
        src_ref=grads[a].at[:, 1 - c], dst_ref=recvs[a], send_sem=send_sems.at[a],
        recv_sem=recv_sems.at[a], device_id=(x, y, 1 - c), device_id_type=MESH) for a in range(len(grads))]


def _split_start(copies, per, srcs, zones, after, *, name):
    n = len(srcs)
    HBM = pl.BlockSpec(memory_space=pltpu.HBM)
    SEM = pl.BlockSpec(memory_space=pltpu.SEMAPHORE)

    def body(*refs):
        send_sems, recv_sems = refs[2 * n + 1], refs[2 * n + 2]
        for cp in copies(refs[:n], refs[n:2 * n], send_sems, recv_sems):
            cp.start()
        refs[-1][...] = jnp.zeros_like(refs[-1])

    hbm = lambda a: pltpu.HBM(a.shape, a.dtype)
    res = _pcall(body, name=name,
                 out_shape=(pltpu.SemaphoreType.DMA((per * n,)), pltpu.SemaphoreType.DMA((per * n,)),
                            *[hbm(a) for a in srcs], *[hbm(a) for a in zones],
                            jax.ShapeDtypeStruct((SUBLANES, LANES), F32)),
                 in_specs=[*[HBM] * (2 * n), pl.BlockSpec(memory_space=pl.ANY)],
                 out_specs=(SEM, SEM, *[HBM] * (2 * n), pl.BlockSpec(memory_space=pltpu.VMEM)),
                 aliases={i: 2 + i for i in range(2 * n)}, split_copy=True)(
        *[pltpu.with_memory_space_constraint(a, pltpu.HBM) for a in [*srcs, *zones]], after)
    return res[0], res[1], list(res[2:2 + n]), list(res[2 + n:2 + 2 * n]), res[-1]


def _split_wait(copies, send_sems, recv_sems, srcs, zones, after, *, name):
    n = len(srcs)
    HBM = pl.BlockSpec(memory_space=pltpu.HBM)
    SEM = pl.BlockSpec(memory_space=pltpu.SEMAPHORE)

    def body(*refs):
        for cp in copies(refs[:n], refs[n:2 * n], refs[2 * n], refs[2 * n + 1]):
            cp.wait_send()
            cp.wait_recv()

    hbm = lambda a: pltpu.HBM(a.shape, a.dtype)
    res = _pcall(body, name=name, out_shape=(*[hbm(a) for a in srcs], *[hbm(a) for a in zones]),
                 in_specs=[*[HBM] * (2 * n), SEM, SEM, pl.BlockSpec(memory_space=pl.ANY)],
                 out_specs=tuple([HBM] * (2 * n)), aliases={i: i for i in range(2 * n)},
                 split_copy=True)(*srcs, *zones, send_sems, recv_sems, after)
    return list(res[:n]), list(res[n:])


def _add_halves(grad, recv, c_idx, *, name):
    S, _, h, C = grad.shape
    th, tc = _blk(h, C)

    def body(c_ref, g_ref, r_ref, o_ref):
        o_ref[...] = (g_ref[...].astype(F32) + r_ref[...].astype(F32)).astype(o_ref.dtype)

    return _pcall(body, name=name, out_shape=jax.ShapeDtypeStruct((S, h, C), grad.dtype),
                  grid=(S, h // th, C // tc), prefetch=1,
                  in_specs=[pl.BlockSpec((None, None, th, tc), lambda s, i, j, c: (s, c[0], i, j)),
                            pl.BlockSpec((None, th, tc), lambda s, i, j, c: (s, i, j))],
                  out_specs=pl.BlockSpec((None, th, tc), lambda s, i, j, c: (s, i, j)),
                  semantics=("parallel", "parallel", "parallel"))(c_idx, grad, recv)


def _scatter_copies(srcs, lands, send_sems, recv_sems):
    x, y, c, others = _mesh_pos()
    return [pltpu.make_async_remote_copy(
        src_ref=srcs[a].at[2 * chip[0] + chip[1]], dst_ref=lands[a].at[j],
        send_sem=send_sems.at[3 * a + j], recv_sem=recv_sems.at[3 * a + j],
        device_id=(*chip, c), device_id_type=MESH)
        for a in range(len(srcs)) for j, chip in enumerate(others)]


def _add_chips(sums, recv, chip_idx, *, name):
    _, h, C = sums.shape
    th, tc = _blk(h, C)

    def body(k_ref, s_ref, r_ref, o_ref):
        acc = s_ref[...].astype(F32) + r_ref[0].astype(F32)
        acc = acc + r_ref[1].astype(F32)
        o_ref[...] = acc + r_ref[2].astype(F32)

    return _pcall(body, name=name, out_shape=jax.ShapeDtypeStruct((h, C), F32),
                  grid=(h // th, C // tc), prefetch=1,
                  in_specs=[pl.BlockSpec((None, th, tc), lambda i, j, k: (k[0], i, j)),
                            pl.BlockSpec((3, th, tc), lambda i, j, k: (0, i, j))],
                  out_specs=pl.BlockSpec((th, tc), lambda i, j, k: (i, j)),
                  semantics=("parallel", "parallel"))(chip_idx, sums, recv)


def _swap_copies(halves, others, send_sems, recv_sems):
    x, y, c, _ = _mesh_pos()
    return [pltpu.make_async_remote_copy(
        src_ref=halves[a], dst_ref=others[a], send_sem=send_sems.at[a], recv_sem=recv_sems.at[a],
        device_id=(x, y, 1 - c), device_id_type=MESH) for a in range(len(halves))]


def _all_reduce_small(buf):
    R, L = buf.shape
    NDEV = 8

    def body(x_ref, sum_ref, all_ref, send_sems, recv_sems, local_sem):
        x, y, c, others = _mesh_pos()
        me, sibling = (x, y, c), (x, y, 1 - c)

        def slot(px, py, pc):
            return all_ref.at[4 * px + 2 * py + pc]

        def copy(k, block, to, src=None):
            return pltpu.make_async_remote_copy(
                src_ref=slot(*block) if src is None else src, dst_ref=slot(*block),
                send_sem=send_sems.at[k], recv_sem=recv_sems.at[k], device_id=to, device_id_type=MESH)

        mine = pltpu.make_async_copy(x_ref, slot(*me), local_sem)
        mine.start()
        first = [copy(0, me, sibling, src=x_ref)]
        first += [copy(1 + j, me, (*chip, c), src=x_ref) for j, chip in enumerate(others)]
        for cp in first:
            cp.start()
        passed = [copy(4 + j, (*chip, c), sibling) for j, chip in enumerate(others)]
        for j, chip in enumerate(others):
            copy(1 + j, (*chip, c), me).wait_recv()
            passed[j].start()
        copy(0, sibling, me).wait_recv()
        for j, chip in enumerate(others):
            copy(4 + j, (*chip, 1 - c), me).wait_recv()
        for cp in first + passed:
            cp.wait_send()
        mine.wait()
        acc = all_ref[0]
        for d in range(1, NDEV):
            acc = acc + all_ref[d]
        sum_ref[...] = acc

    VM = pl.BlockSpec(memory_space=pltpu.VMEM)
    return _pcall(body, name="all_reduce_small",
                  out_shape=(jax.ShapeDtypeStruct((R, L), F32), jax.ShapeDtypeStruct((NDEV, R, L), F32)),
                  in_specs=[VM], out_specs=(VM, VM),
                  scratch_shapes=[pltpu.SemaphoreType.DMA((7,)), pltpu.SemaphoreType.DMA((7,)),
                                  pltpu.SemaphoreType.DMA])(buf)[0]


def _pack(arrs, rows_multiple=16):
    flat = [a.reshape(-1).astype(F32) for a in arrs]
    sizes = [f.shape[0] for f in flat]
    total = sum(sizes)
    per = LANES * rows_multiple
    padded = -(-total // per) * per
    flat.append(jnp.zeros((padded - total,), F32))
    offs = [0]
    for s in sizes:
        offs.append(offs[-1] + s)
    return jnp.concatenate(flat).reshape(padded // LANES, LANES), offs


def _unpack(buf, offs, shapes):
    flat = buf.reshape(-1)
    return [flat[offs[i]:offs[i + 1]].reshape(s) for i, s in enumerate(shapes)]


def kernel(x, mem, g_mix, w_in, w_a2, b_a, g_gla, w_pool, pool_scale, w_branch, w_out, g_cross, g_mem, w_cq, w_ckv, w_co, g_ffn, w_up, conv_w, conv_b, w_down, g_final, loss_target, m_g_mix, m_w_in, m_w_a2, m_b_a, m_g_gla, m_w_pool, m_pool_scale, m_w_branch, m_w_out, m_g_cross, m_g_mem, m_w_cq, m_w_ckv, m_w_co, m_g_ffn, m_w_up, m_conv_w, m_conv_b, m_w_down, m_g_final, v_g_mix, v_w_in, v_w_a2, v_b_a, v_g_gla, v_w_pool, v_pool_scale, v_w_branch, v_w_out, v_g_cross, v_g_mem, v_w_cq, v_w_ckv, v_w_co, v_g_ffn, v_w_up, v_conv_w, v_conv_b, v_w_down, v_g_final):
    weights = dict(g_mix=g_mix, w_in=w_in, w_a2=w_a2, b_a=b_a, g_gla=g_gla, w_pool=w_pool,
                   pool_scale=pool_scale, w_branch=w_branch, w_out=w_out, g_cross=g_cross, g_mem=g_mem,
                   w_cq=w_cq, w_ckv=w_ckv, w_co=w_co, g_ffn=g_ffn, w_up=w_up, conv_w=conv_w,
                   conv_b=conv_b, w_down=w_down, g_final=g_final)
    mom_m = dict(g_mix=m_g_mix, w_in=m_w_in, w_a2=m_w_a2, b_a=m_b_a, g_gla=m_g_gla, w_pool=m_w_pool,
                 pool_scale=m_pool_scale, w_branch=m_w_branch, w_out=m_w_out, g_cross=m_g_cross,
                 g_mem=m_g_mem, w_cq=m_w_cq, w_ckv=m_w_ckv, w_co=m_w_co, g_ffn=m_g_ffn, w_up=m_w_up,
                 conv_w=m_conv_w, conv_b=m_conv_b, w_down=m_w_down, g_final=m_g_final)
    mom_v = dict(g_mix=v_g_mix, w_in=v_w_in, w_a2=v_w_a2, b_a=v_b_a, g_gla=v_g_gla, w_pool=v_w_pool,
                 pool_scale=v_pool_scale, w_branch=v_w_branch, w_out=v_w_out, g_cross=v_g_cross,
                 g_mem=v_g_mem, w_cq=v_w_cq, w_ckv=v_w_ckv, w_co=v_w_co, g_ffn=v_g_ffn, w_up=v_w_up,
                 conv_w=v_conv_w, conv_b=v_conv_b, w_down=v_w_down, g_final=v_g_final)
    order = list(weights)
    big = ["w_in", "w_branch", "w_out", "w_cq", "w_ckv", "w_co", "w_up", "w_down"]
    small_sharded = ["w_a2", "w_pool", "conv_w"]
    small_repl = ["g_mix", "b_a", "g_gla", "pool_scale", "g_cross", "g_mem", "g_ffn", "conv_b", "g_final"]

    xs, ms, tgt = x[0], mem[0], loss_target[0]
    T, D = xs.shape
    M = ms.shape[0]
    DK, DV, PW = b_a.shape[1], g_gla.shape[1], pool_scale.shape[1]
    RANK = w_a2.shape[1]
    F2 = conv_b.shape[1]
    F = F2 // 2
    DIN = N_CHIPS * w_in.shape[2]
    OFF_A = 2 * DK + 2 * DV
    OFF_P = OFF_A + RANK
    RP = LANES
    GW = PW // POOL_GROUPS
    assert PW == DV and 4 * DV == 2 * D and OFF_P + PW + 2 * D == DIN

    cx, cy, cc = lax.axis_index("x"), lax.axis_index("y"), lax.axis_index("c")
    chip = 2 * cx + cy
    c_idx = jnp.reshape(cc, (1,)).astype(jnp.int32)
    chip_idx = jnp.reshape(chip, (1,)).astype(jnp.int32)

    def halves(a):
        return a.reshape(2, a.shape[0] // 2, a.shape[1])

    shard2d = {k: (weights[k][0].T if k == "w_in" else weights[k][0]) for k in big}
    small_pack, small_offs = _pack([weights[k][0] for k in small_sharded], rows_multiple=32)
    flying, passing = {}, {}

    def gather_start(group, keys, tok):
        srcs = [small_pack if k == "small" else shard2d[k].astype(BF16) for k in keys]
        if group != "in":
            srcs = [a + tok[0:1, 0:1].astype(a.dtype) for a in srcs]
        srcs = [halves(a) for a in srcs]
        zones = [lax.empty((N_CHIPS, *s.shape), s.dtype) for s in srcs]
        first = (_near_copies, 2) if group == "in" else (_gather_copies, 3)
        s_sems, r_sems, srcs, zones, tok = _split_start(*first, srcs, zones, tok, name=f"gather_start_{group}")
        flying[group] = (keys, s_sems, r_sems, srcs, zones)
        return tok

    tok = gather_start("in", ["w_in"], xs)

    def arrive_in(after):
        keys, s_sems, r_sems, srcs, zones = flying["in"]
        near, diag = functools.partial(_pass_copies, pieces=(0, 1, 3)), functools.partial(_pass_copies, pieces=(2,))
        srcs, zones = _split_wait(_near_copies, s_sems, r_sems, srcs, zones, after, name="gather_wait_in")
        rs, rr, srcs, zones, tok = _split_start(_relay_copies, 2, srcs, zones, after, name="gather_relay_start_in")
        ns, nr, srcs, zones, tok = _split_start(near, 3, srcs, zones, tok, name="gather_pass_near_start_in")
        for group, group_keys in (("mix", ["w_branch", "w_out", "small"]), ("cross", ["w_cq", "w_ckv", "w_co"]),
                                  ("up", ["w_up"]), ("down", ["w_down"])):
            tok = gather_start(group, group_keys, tok)
        after = tok
        srcs, zones = _split_wait(_relay_copies, rs, rr, srcs, zones, after, name="gather_relay_wait_in")
        ds, dr, srcs, zones, _ = _split_start(diag, 1, srcs, zones, after, name="gather_pass_diag_start_in")
        srcs, zones = _split_wait(near, ns, nr, srcs, zones, after, name="gather_pass_near_wait_in")
        _, full = _split_wait(diag, ds, dr, srcs, zones, after, name="gather_pass_diag_wait_in")
        return {k: f.reshape(N_CHIPS, f.shape[1] * f.shape[2], f.shape[3]) for k, f in zip(keys, full)}

    def landed(group, after):
        keys, s_sems, r_sems, srcs, zones = flying[group]
        srcs, zones = _split_wait(_gather_copies, s_sems, r_sems, srcs, zones, after,
                                  name=f"gather_wait_{group}")
        s_sems, r_sems, srcs, zones, token = _split_start(_pass_copies, 4, srcs, zones, after,
                                                          name=f"gather_pass_start_{group}")
        passing[group] = (keys, s_sems, r_sems, srcs, zones)
        return token

    def arrive(group, after):
        keys, s_sems, r_sems, srcs, zones = passing[group]
        _, full = _split_wait(_pass_copies, s_sems, r_sems, srcs, zones, after,
                              name=f"gather_pass_wait_{group}")
        return {k: f.reshape(N_CHIPS, f.shape[1] * f.shape[2], f.shape[3]) for k, f in zip(keys, full)}

    def rows(g):
        return g.reshape(-1, g.shape[2])

    h1, r1 = _rms_fwd(xs, g_mix + tok[0:1, 0:1], name="norm_mix")
    W_in = rows(arrive_in(h1)["w_in"])
    W_main = jnp.concatenate([W_in[:OFF_A], W_in[OFF_P:]], axis=0)
    W_a = jnp.pad(W_in[OFF_A:OFF_P], ((0, RP - RANK), (0, 0)))
    tok = landed("mix", W_a)
    proj = _mm(h1, W_main, "nt", name="proj_main", out_dtype=F32, after=tok)
    gw = arrive("mix", proj)
    W_branch, W_out, small_all = rows(gw["w_branch"]), rows(gw["w_out"]), gw["small"]
    sm = [_unpack(small_all[j], small_offs, [weights[k].shape[1:] for k in small_sharded]) for j in range(N_CHIPS)]
    W_a2 = jnp.concatenate([sm[j][0] for j in range(N_CHIPS)], axis=1)
    W_a2p = jnp.pad(W_a2, ((0, RP - RANK), (0, 0))).astype(BF16)
    W_pool = jnp.concatenate([sm[j][1] for j in range(N_CHIPS)], axis=1).astype(BF16)
    W_conv = jnp.concatenate([sm[j][2] for j in range(N_CHIPS)], axis=1)

    a_pad = _mm(h1, W_a, "nt", name="proj_gate_rank", out_dtype=F32)
    o_gla, o_raw, states = _gla_fwd(proj, a_pad, W_a2p, b_a, g_gla, T=T, DK=DK, DV=DV)
    o_pool = _pool_fwd(proj, W_pool, pool_scale, T=T, PW=PW, col_block=3)
    tok = landed("cross", o_pool)
    y_gla = _mm(o_gla, W_branch, "nn", name="branch_gla", out_dtype=BF16, K=DV, after=tok)
    y_pool = _mm(o_pool, W_branch, "nn", name="branch_pool", out_dtype=BF16, K=PW, b_off=(DV, 0))
    merged = _merge_fwd(y_gla, y_pool, proj, T=T, D=D, col_block=2)
    x1 = _mm(merged, W_out, "nn", name="mix_out", out_dtype=F32, add=xs)

    h2, r2 = _rms_fwd(x1, g_cross, name="norm_cross")
    mem_n, rm = _rms_fwd(ms, g_mem, name="norm_mem")
    gw = arrive("cross", h2)
    W_cq, W_ckv, W_co = rows(gw["w_cq"]), gw["w_ckv"], rows(gw["w_co"])
    qc = _mm(h2, W_cq, "nn", name="cross_q", out_dtype=BF16)
    kv = _mm(mem_n, W_ckv, "nn", name="cross_kv", out_dtype=BF16, b_blocked=True)
    o_att = _attn_fwd(qc, kv, T=T, D=D, M=M)
    x2 = _mm(o_att, W_co, "nn", name="cross_out", out_dtype=F32, add=x1)

    tok = landed("up", x2)
    h3, r3 = _rms_fwd(x2, g_ffn + tok[0:1, 0:1], name="norm_ffn")
    W_up = arrive("up", h3)["w_up"]
    u0 = _mm(h3, W_up, "nn", name="ffn_up", out_dtype=F32, b_blocked=True)
    tok = landed("down", u0)
    f_act = _conv_fwd(u0, W_conv, conv_b + tok[0:1, 0:1], T=T, F=F)
    W_down = rows(arrive("down", f_act)["w_down"])
    x3 = _mm(f_act, W_down, "nn", name="ffn_down", out_dtype=F32, add=x2, tk=F // 2)

    loss_part, dx3, dx3_b, dg_final = _loss_head(x3, g_final.reshape(1, D), tgt)

    def col_shards(g):
        nb, K, Nb = g.shape
        return g.reshape(nb, 2, K // 2, Nb)

    def row_shards(g):
        R, N = g.shape
        return g.reshape(N_CHIPS, 2, R // N_CHIPS // 2, N)

    exchanging, in_flight = {}, []

    def exchange_start(group, keys, partials, after):
        recvs = [lax.empty((p.shape[0], *p.shape[2:]), p.dtype) for p in partials]
        s_sems, r_sems, partials, recvs, token = _split_start(
            _exchange_copies, 1, partials, recvs, after, name=f"grad_exchange_start_{group}")
        exchanging[group] = (keys, s_sems, r_sems, partials, recvs)
        return token

    def scatter_start(group, after):
        keys, s_sems, r_sems, partials, recvs = exchanging[group]
        partials, recvs = _split_wait(_exchange_copies, s_sems, r_sems, partials, recvs, after,
                                      name=f"grad_exchange_wait_{group}")
        chip_sums = [_add_halves(p, r, c_idx, name=f"grad_add_halves_{k}")
                     for k, p, r in zip(keys, partials, recvs)]
        lands = [lax.empty((3, *s.shape[1:]), s.dtype) for s in chip_sums]
        s_sems, r_sems, sums, lands, token = _split_start(
            _scatter_copies, 3, chip_sums, lands, after, name=f"grad_scatter_start_{group}")
        in_flight.append((group, keys, s_sems, r_sems, sums, lands))
        return token

    collected = []

    def collect(after):
        group, keys, s_sems, r_sems, sums, lands = in_flight.pop(0)
        sums, from_chips = _split_wait(_scatter_copies, s_sems, r_sems, sums, lands, after,
                                       name=f"grad_scatter_wait_{group}")
        half_sums = [_add_chips(s, r, chip_idx, name=f"grad_add_chips_{k}") for k, s, r in zip(keys, sums, from_chips)]
        others = [lax.empty(h.shape, h.dtype) for h in half_sums]
        s_sems, r_sems, half_sums, others, token = _split_start(
            _swap_copies, 1, half_sums, others, after, name=f"grad_swap_start_{group}")
        collected.append((keys, s_sems, r_sems, half_sums, others))
        return token

    df = _mm(dx3_b, W_down, "nt", name="d_ffn_act", out_dtype=BF16)
    dW_down = _mm(f_act, dx3_b, "tn", name="dw_down", out_dtype=BF16)
    du0, dconv_w, dconv_b = _conv_bwd(u0, W_conv, conv_b, df, T=T, F=F)
    dh3 = _mm(du0, W_up, "nt", name="d_ffn_in", out_dtype=BF16, b_blocked=True, tk=F2 // N_CHIPS)
    dW_up = _mm(h3, du0, "tn", name="dw_up", out_dtype=BF16, out_blocks=N_CHIPS)
    tok = exchange_start("ffn", ["w_down", "w_up"], [row_shards(dW_down), col_shards(dW_up)], dh3)
    dx2, dx2_b, dg_ffn = _rms_bwd(dh3, x2, r3 + tok[0:1, 0:1], g_ffn, dx3, name="norm_ffn_bwd")

    do_att = _mm(dx2_b, W_co, "nt", name="d_cross_o", out_dtype=BF16)
    dW_co = _mm(o_att, dx2_b, "tn", name="dw_co", out_dtype=BF16)
    tok = scatter_start("ffn", dW_co)
    dq, dkv = _attn_bwd(qc, kv, do_att, T=T, D=D, M=M)
    dkv_b = dkv.astype(BF16)
    dW_cq = _mm(h2, dq, "tn", name="dw_cq", out_dtype=BF16, after=tok)
    dh2 = _mm(dq, W_cq, "nt", name="d_cross_in", out_dtype=BF16)
    dW_ckv = _mm(mem_n, dkv_b, "tn", name="dw_ckv", out_dtype=BF16, out_blocks=N_CHIPS)
    dmem_n = _mm(dkv_b, W_ckv, "nt", name="d_mem", out_dtype=F32, b_blocked=True)
    tok = exchange_start("cross", ["w_co", "w_cq", "w_ckv"],
                         [row_shards(dW_co), row_shards(dW_cq), col_shards(dW_ckv)], dmem_n)
    _, _, dg_mem = _rms_bwd(dmem_n, ms, rm, g_mem, None, name="norm_mem_bwd")
    dx1, dx1_b, dg_cross = _rms_bwd(dh2, x1, r2 + tok[0:1, 0:1], g_cross, dx2, name="norm_cross_bwd")

    dmerged = _mm(dx1_b, W_out, "nt", name="d_merged", out_dtype=BF16)
    dW_out = _mm(merged, dx1_b, "tn", name="dw_out", out_dtype=BF16)
    tok = scatter_start("cross", dW_out)
    dy_gla, dy_pool, dgates = _merge_bwd(dmerged, y_gla, y_pool, proj, T=T, D=D, col_block=2)
    dW_br_gla = _mm(o_gla, dy_gla, "tn", name="dw_branch_gla", out_dtype=BF16, after=tok)
    dW_br_pool = _mm(o_pool, dy_pool, "tn", name="dw_branch_pool", out_dtype=BF16)
    do_gla = _mm(dy_gla, W_branch, "nt", name="d_o_gla", out_dtype=F32, N=DV)
    do_pool = _mm(dy_pool, W_branch, "nt", name="d_o_pool", out_dtype=F32, N=PW, b_off=(DV, 0))
    dp, dw_pool, dpool_scale = _pool_bwd(proj, W_pool, pool_scale, do_pool, T=T, PW=PW, col_block=3)
    dW_pool = jnp.transpose(dw_pool.reshape(POOL_GROUPS, N_CHIPS, GW // N_CHIPS, GW), (1, 0, 2, 3))
    tok = exchange_start("mix", ["w_out", "w_branch", "w_pool"],
                         [row_shards(dW_out), row_shards(jnp.concatenate([dW_br_gla, dW_br_pool], axis=0)),
                          row_shards(dW_pool.reshape(N_CHIPS * POOL_GROUPS * (GW // N_CHIPS), GW).astype(BF16))],
                         dp)
    dqkvr, da_pad, dw2, db_a, dg_gla = _gla_bwd(proj, a_pad, W_a2p, b_a + tok[0:1, 0:1], g_gla, o_raw, states,
                                               do_gla, T=T, DK=DK, DV=DV)
    tok = scatter_start("mix", dqkvr)
    dproj = jnp.concatenate([dqkvr, dp, dgates], axis=1)
    dW_main = _mm(dproj, h1, "tn", name="dw_in_main", out_dtype=BF16, after=tok)
    dW_a = _mm(da_pad, h1, "tn", name="dw_in_rank", out_dtype=BF16)
    dW_in = jnp.concatenate([dW_main[:OFF_A], dW_a[:RANK], dW_main[OFF_A:]], axis=0)
    tok = exchange_start("in", ["w_in"], [row_shards(dW_in)], dW_a)
    dh1 = _mm(dproj, W_main, "nn", name="d_mix_in_main", out_dtype=F32, after=tok)
    dh1 = _mm(da_pad, W_a, "nn", name="d_mix_in_rank", out_dtype=BF16, add=dh1)
    dx0, _, dg_mix = _rms_bwd(dh1, xs, r1, g_mix, dx1, name="norm_mix_bwd")

    grads = {}

    small_grads = [loss_part, dg_mix, db_a, dg_gla, dpool_scale, dg_cross, dg_mem, dg_ffn, dconv_b, dg_final,
                   dw2[:RANK], dconv_w]
    small_buf, offs = _pack(small_grads)
    small_sum = _all_reduce_small(small_buf)
    red = _unpack(small_sum, offs, [g.shape for g in small_grads])
    loss = red[0][0, 0]
    for k, g in zip(small_repl, red[1:10]):
        grads[k] = g.reshape(weights[k].shape)
    nb = DK // N_CHIPS
    grads["w_a2"] = lax.dynamic_slice_in_dim(red[10], chip * nb, nb, axis=1)[None]
    nb = F2 // N_CHIPS
    grads["conv_w"] = lax.dynamic_slice_in_dim(red[11], chip * nb, nb, axis=1)[None]

    delta, new_m, new_v = {}, {}, {}

    def shard_rows(k, a):
        a = a[0]
        return a.T if k == "w_in" else a.reshape(-1, a.shape[-1])

    def whole(k, a):
        a = a.reshape(-1, a.shape[2])
        return (a.T if k == "w_in" else a).reshape(weights[k].shape)

    scatter_start("in", small_sum)

    def finish(after):
        keys, s_sems, r_sems, mine, others = collected.pop(0)
        mine, others = _split_wait(_swap_copies, s_sems, r_sems, mine, others, after,
                                   name=f"grad_swap_wait_{keys[0]}")
        for k, g_mine, g_other in zip(keys, mine, others):
            wmv = [halves(shard_rows(k, src[k])) for src in (weights, mom_m, mom_v)]
            res = _adamw_halves(*wmv, g_mine, g_other, c_idx, name=f"adamw_{k}")
            grads[k], delta[k], new_m[k], new_v[k] = (whole(k, a) for a in res)
        return res[1]

    after = in_flight[-1][4][0]
    while in_flight:
        after = collect(after)
        while len(collected) > 1:
            after = finish(after)
    finish(after)
    small = small_repl + ["w_a2", "conv_w"]
    packs = [_pack([src[k] for k in small])[0] for src in (weights, grads, mom_m, mom_v)]
    _, offs = _pack([weights[k] for k in small])
    outs = _adamw(*packs, name="adamw_small")
    for res, o in zip((delta, new_m, new_v), outs):
        for k, a in zip(small, _unpack(o, offs, [weights[k].shape for k in small])):
            res[k] = a

    return (loss, dx0[None], *[grads[k] for k in order], *[delta[k] for k in order],
            *[new_m[k] for k in order], *[new_v[k] for k in order])
```
